```python
import math
import jax, jax.numpy as jnp
from jax import lax
import numpy as np


D_MODEL = 1024
BATCH = 8
SEQ = 8192
DEPTH = 1

PLE_DIM = 256
ATTN_HEADS = 8
HEAD_DIM = 64
ATTN_WIDTH = ATTN_HEADS * HEAD_DIM
SSM_WIDTH = D_MODEL - ATTN_WIDTH
SSM_GROUP_CH = 16
SSM_GROUPS = SSM_WIDTH // SSM_GROUP_CH
SSM_STATE = 64
MIX_WIDTH = ATTN_WIDTH + SSM_WIDTH
IN_COLS = 3 * ATTN_WIDTH + ATTN_HEADS + SSM_WIDTH
D_FF = 2816
Q_BLOCK = 128
EPS = 1e-6

kernel_name = 'hybrid_fox_s5_macaron_ple'


def rmsnorm(x, g):
    xf = x.astype(jnp.float32)
    y = xf * lax.rsqrt(jnp.mean(xf * xf, axis=-1, keepdims=True) + EPS)
    return (y * g.astype(jnp.float32)).astype(x.dtype)


def swiglu(x, w1, w3, w2):
    return (jax.nn.silu(x @ w1) * (x @ w3)) @ w2


def forgetting_attention(q, k, v, log_f):
    B, L, H, hd = q.shape
    scale = 1.0 / math.sqrt(hd)
    q = q.transpose(0, 2, 1, 3)
    k = k.transpose(0, 2, 1, 3)
    v = v.transpose(0, 2, 1, 3)
    c = jnp.cumsum(log_f, axis=1).transpose(0, 2, 1)
    kpos = jnp.arange(L)
    n_blocks = L // Q_BLOCK

    def block(i):
        s0 = i * Q_BLOCK
        qb = lax.dynamic_slice_in_dim(q, s0, Q_BLOCK, axis=2)
        cb = lax.dynamic_slice_in_dim(c, s0, Q_BLOCK, axis=2)
        logits = (jnp.einsum('bhqd,bhkd->bhqk', qb, k).astype(jnp.float32) * scale
                  + cb[..., :, None] - c[..., None, :])
        qpos = s0 + jnp.arange(Q_BLOCK)
        mask = kpos[None, :] <= qpos[:, None]
        w = jax.nn.softmax(jnp.where(mask, logits, -jnp.inf), axis=-1)
        return jnp.einsum('bhqk,bhkd->bhqd', w.astype(v.dtype), v)

    out = lax.map(block, jnp.arange(n_blocks))
    return out.transpose(1, 0, 3, 2, 4).reshape(B, L, H * hd)


def _ssm_combine(e1, e2):
    a1r, a1i, b1r, b1i = e1
    a2r, a2i, b2r, b2i = e2
    ar = a2r * a1r - a2i * a1i
    ai = a2r * a1i + a2i * a1r
    br = a2r * b1r - a2i * b1i + b2r
    bi = a2r * b1i + a2i * b1r + b2i
    return ar, ai, br, bi


def s5_mixer(s, a_re, a_im, log_dt, b_re, b_im, c_re, c_im, d_skip, w_glu, b_glu):
    B, L, _ = s.shape
    f32 = jnp.float32
    u = s.astype(f32).reshape(B, L, SSM_GROUPS, SSM_GROUP_CH)
    ar, ai = a_re.astype(f32), a_im.astype(f32)
    dt = jnp.exp(log_dt.astype(f32))[:, None]
    decay = jnp.exp(dt * ar)
    abar_r = decay * jnp.cos(dt * ai)
    abar_i = decay * jnp.sin(dt * ai)
    nr, ni = abar_r - 1.0, abar_i
    den = ar * ar + ai * ai
    fr = (nr * ar + ni * ai) / den
    fi = (ni * ar - nr * ai) / den
    br, bi = b_re.astype(f32), b_im.astype(f32)
    bbar_r = fr[..., None] * br - fi[..., None] * bi
    bbar_i = fr[..., None] * bi + fi[..., None] * br
    bu_r = jnp.einsum('blgh,gph->blgp', u, bbar_r)
    bu_i = jnp.einsum('blgh,gph->blgp', u, bbar_i)
    a_r_full = jnp.broadcast_to(abar_r, bu_r.shape)
    a_i_full = jnp.broadcast_to(abar_i, bu_i.shape)
    _, _, xr, xi = lax.associative_scan(_ssm_combine, (a_r_full, a_i_full, bu_r, bu_i), axis=1)
    y = (jnp.einsum('blgp,ghp->blgh', xr, c_re.astype(f32))
         - jnp.einsum('blgp,ghp->blgh', xi, c_im.astype(f32))
         + d_skip.astype(f32) * u)
    y = jax.nn.gelu(y.reshape(B, L, SSM_WIDTH)).astype(s.dtype)
    return y * jax.nn.sigmoid(y @ w_glu + b_glu)


def _fwd_setup_inputs(seed: int = 0) -> dict:
    key = jax.random.key(seed)
    ks = iter(jax.random.split(key, 40))
    nrm = lambda shape, scale: jax.random.normal(next(ks), shape, jnp.float32) * scale
    gain = lambda shape: 1.0 + nrm(shape, 0.05)
    Dp, D, F = DEPTH, D_MODEL, D_FF
    G, P, Hc = SSM_GROUPS, SSM_STATE, SSM_GROUP_CH
    inp = {}
    inp['x'] = nrm((BATCH, SEQ, D), 1.0)
    inp['p'] = nrm((DEPTH, BATCH, SEQ, PLE_DIM), 1.0)
    inp['g_ffn1'] = gain((Dp, D))
    inp['w1_a'] = nrm((Dp, D, F), D ** -0.5)
    inp['w3_a'] = nrm((Dp, D, F), D ** -0.5)
    inp['w2_a'] = nrm((Dp, F, D), F ** -0.5)
    inp['g_mix'] = gain((Dp, D))
    inp['w_in'] = nrm((Dp, D, IN_COLS), D ** -0.5)
    inp['b_f'] = jnp.linspace(1.0, 5.0, ATTN_HEADS)[None, :] + nrm((Dp, ATTN_HEADS), 0.1)
    inp['a_re'] = -0.5 + nrm((Dp, G, P), 0.01)
    inp['a_im'] = jnp.pi * jnp.arange(P, dtype=jnp.float32)[None, None, :] + nrm((Dp, G, P), 0.01)
    inp['log_dt'] = jax.random.uniform(next(ks), (Dp, G), jnp.float32, math.log(1e-3), math.log(1e-1))
    inp['b_re'] = nrm((Dp, G, P, Hc), (2.0 * Hc) ** -0.5)
    inp['b_im'] = nrm((Dp, G, P, Hc), (2.0 * Hc) ** -0.5)
    inp['c_re'] = nrm((Dp, G, Hc, P), (2.0 * P) ** -0.5)
    inp['c_im'] = nrm((Dp, G, Hc, P), (2.0 * P) ** -0.5)
    inp['d_skip'] = nrm((Dp, G, Hc), 1.0)
    inp['w_glu'] = nrm((Dp, SSM_WIDTH, SSM_WIDTH), SSM_WIDTH ** -0.5)
    inp['b_glu'] = nrm((Dp, SSM_WIDTH), 0.02)
    inp['g_attn_out'] = gain((Dp, ATTN_WIDTH))
    inp['g_ssm_out'] = gain((Dp, SSM_WIDTH))
    inp['w_out'] = nrm((Dp, MIX_WIDTH, D), MIX_WIDTH ** -0.5)
    inp['g_ffn2'] = gain((Dp, D))
    inp['w1_b'] = nrm((Dp, D, F), D ** -0.5)
    inp['w3_b'] = nrm((Dp, D, F), D ** -0.5)
    inp['w2_b'] = nrm((Dp, F, D), F ** -0.5)
    inp['g_ple'] = gain((Dp, D))
    inp['w_ple_gate'] = nrm((Dp, D, D), D ** -0.5)
    inp['w_ple_proj'] = nrm((Dp, PLE_DIM, D), PLE_DIM ** -0.5)
    inp['g_final'] = gain((D,))
    return inp


def _fwd_reference(x, p, g_ffn1, w1_a, w3_a, w2_a, g_mix, w_in, b_f, a_re, a_im, log_dt,
              b_re, b_im, c_re, c_im, d_skip, w_glu, b_glu, g_attn_out, g_ssm_out, w_out,
              g_ffn2, w1_b, w3_b, w2_b, g_ple, w_ple_gate, w_ple_proj, g_final):
    B, L, _ = x.shape
    h = x
    s_q, s_k, s_v, s_f = ATTN_WIDTH, 2 * ATTN_WIDTH, 3 * ATTN_WIDTH, 3 * ATTN_WIDTH + ATTN_HEADS
    for i in range(DEPTH):
        h = h + 0.5 * swiglu(rmsnorm(h, g_ffn1[i]), w1_a[i], w3_a[i], w2_a[i])
        u = rmsnorm(h, g_mix[i])
        z = u @ w_in[i]
        q = z[..., :s_q].reshape(B, L, ATTN_HEADS, HEAD_DIM)
        k = z[..., s_q:s_k].reshape(B, L, ATTN_HEADS, HEAD_DIM)
        v = z[..., s_k:s_v].reshape(B, L, ATTN_HEADS, HEAD_DIM)
        log_f = jax.nn.log_sigmoid(z[..., s_v:s_f].astype(jnp.float32) + b_f[i].astype(jnp.float32))
        s_in = z[..., s_f:]
        attn = forgetting_attention(q, k, v, log_f)
        ssm = s5_mixer(s_in, a_re[i], a_im[i], log_dt[i], b_re[i], b_im[i], c_re[i], c_im[i],
                       d_skip[i], w_glu[i], b_glu[i])
        mixed = jnp.concatenate([rmsnorm(attn, g_attn_out[i]), rmsnorm(ssm, g_ssm_out[i])], axis=-1)
        h = h + mixed @ w_out[i]
        h = h + 0.5 * swiglu(rmsnorm(h, g_ffn2[i]), w1_b[i], w3_b[i], w2_b[i])
        gate = jax.nn.sigmoid(rmsnorm(h, g_ple[i]) @ w_ple_gate[i])
        h = h + gate * (p[i] @ w_ple_proj[i])
    return rmsnorm(h, g_final)


import jax as _jax
import jax.numpy as _jnp

TWIN_FORMAT = 'train_step'
FWD_PARAMS = ['x', 'p', 'g_ffn1', 'w1_a', 'w3_a', 'w2_a', 'g_mix', 'w_in', 'b_f', 'a_re', 'a_im', 'log_dt', 'b_re', 'b_im', 'c_re', 'c_im', 'd_skip', 'w_glu', 'b_glu', 'g_attn_out', 'g_ssm_out', 'w_out', 'g_ffn2', 'w1_b', 'w3_b', 'w2_b', 'g_ple', 'w_ple_gate', 'w_ple_proj', 'g_final']
TWIN_WEIGHTS = ['g_ffn1', 'w1_a', 'w3_a', 'w2_a', 'g_mix', 'w_in', 'b_f', 'a_re', 'a_im', 'log_dt', 'b_re', 'b_im', 'c_re', 'c_im', 'd_skip', 'w_glu', 'b_glu', 'g_attn_out', 'g_ssm_out', 'w_out', 'g_ffn2', 'w1_b', 'w3_b', 'w2_b', 'g_ple', 'w_ple_gate', 'w_ple_proj', 'g_final']
TWIN_DIFF_INPUT = 'x'
TWIN_INPUTS = ['x', 'p', 'g_ffn1', 'w1_a', 'w3_a', 'w2_a', 'g_mix', 'w_in', 'b_f', 'a_re', 'a_im', 'log_dt', 'b_re', 'b_im', 'c_re', 'c_im', 'd_skip', 'w_glu', 'b_glu', 'g_attn_out', 'g_ssm_out', 'w_out', 'g_ffn2', 'w1_b', 'w3_b', 'w2_b', 'g_ple', 'w_ple_gate', 'w_ple_proj', 'g_final', 'loss_target', 'm_g_ffn1', 'm_w1_a', 'm_w3_a', 'm_w2_a', 'm_g_mix', 'm_w_in', 'm_b_f', 'm_a_re', 'm_a_im', 'm_log_dt', 'm_b_re', 'm_b_im', 'm_c_re', 'm_c_im', 'm_d_skip', 'm_w_glu', 'm_b_glu', 'm_g_attn_out', 'm_g_ssm_out', 'm_w_out', 'm_g_ffn2', 'm_w1_b', 'm_w3_b', 'm_w2_b', 'm_g_ple', 'm_w_ple_gate', 'm_w_ple_proj', 'm_g_final', 'v_g_ffn1', 'v_w1_a', 'v_w3_a', 'v_w2_a', 'v_g_mix', 'v_w_in', 'v_b_f', 'v_a_re', 'v_a_im', 'v_log_dt', 'v_b_re', 'v_b_im', 'v_c_re', 'v_c_im', 'v_d_skip', 'v_w_glu', 'v_b_glu', 'v_g_attn_out', 'v_g_ssm_out', 'v_w_out', 'v_g_ffn2', 'v_w1_b', 'v_w3_b', 'v_w2_b', 'v_g_ple', 'v_w_ple_gate', 'v_w_ple_proj', 'v_g_final']
TWIN_OUTPUTS = ['loss', 'grad_x', 'grad_g_ffn1', 'grad_w1_a', 'grad_w3_a', 'grad_w2_a', 'grad_g_mix', 'grad_w_in', 'grad_b_f', 'grad_a_re', 'grad_a_im', 'grad_log_dt', 'grad_b_re', 'grad_b_im', 'grad_c_re', 'grad_c_im', 'grad_d_skip', 'grad_w_glu', 'grad_b_glu', 'grad_g_attn_out', 'grad_g_ssm_out', 'grad_w_out', 'grad_g_ffn2', 'grad_w1_b', 'grad_w3_b', 'grad_w2_b', 'grad_g_ple', 'grad_w_ple_gate', 'grad_w_ple_proj', 'grad_g_final', 'delta_g_ffn1', 'delta_w1_a', 'delta_w3_a', 'delta_w2_a', 'delta_g_mix', 'delta_w_in', 'delta_b_f', 'delta_a_re', 'delta_a_im', 'delta_log_dt', 'delta_b_re', 'delta_b_im', 'delta_c_re', 'delta_c_im', 'delta_d_skip', 'delta_w_glu', 'delta_b_glu', 'delta_g_attn_out', 'delta_g_ssm_out', 'delta_w_out', 'delta_g_ffn2', 'delta_w1_b', 'delta_w3_b', 'delta_w2_b', 'delta_g_ple', 'delta_w_ple_gate', 'delta_w_ple_proj', 'delta_g_final', 'new_m_g_ffn1', 'new_m_w1_a', 'new_m_w3_a', 'new_m_w2_a', 'new_m_g_mix', 'new_m_w_in', 'new_m_b_f', 'new_m_a_re', 'new_m_a_im', 'new_m_log_dt', 'new_m_b_re', 'new_m_b_im', 'new_m_c_re', 'new_m_c_im', 'new_m_d_skip', 'new_m_w_glu', 'new_m_b_glu', 'new_m_g_attn_out', 'new_m_g_ssm_out', 'new_m_w_out', 'new_m_g_ffn2', 'new_m_w1_b', 'new_m_w3_b', 'new_m_w2_b', 'new_m_g_ple', 'new_m_w_ple_gate', 'new_m_w_ple_proj', 'new_m_g_final', 'new_v_g_ffn1', 'new_v_w1_a', 'new_v_w3_a', 'new_v_w2_a', 'new_v_g_mix', 'new_v_w_in', 'new_v_b_f', 'new_v_a_re', 'new_v_a_im', 'new_v_log_dt', 'new_v_b_re', 'new_v_b_im', 'new_v_c_re', 'new_v_c_im', 'new_v_d_skip', 'new_v_w_glu', 'new_v_b_glu', 'new_v_g_attn_out', 'new_v_g_ssm_out', 'new_v_w_out', 'new_v_g_ffn2', 'new_v_w1_b', 'new_v_w3_b', 'new_v_w2_b', 'new_v_g_ple', 'new_v_w_ple_gate', 'new_v_w_ple_proj', 'new_v_g_final']
TWIN_LEAF_KINDS = {'loss': 'loss', 'grad_x': 'grad_x', 'grad_g_ffn1': 'grad_w', 'grad_w1_a': 'grad_w', 'grad_w3_a': 'grad_w', 'grad_w2_a': 'grad_w', 'grad_g_mix': 'grad_w', 'grad_w_in': 'grad_w', 'grad_b_f': 'grad_w', 'grad_a_re': 'grad_w', 'grad_a_im': 'grad_w', 'grad_log_dt': 'grad_w', 'grad_b_re': 'grad_w', 'grad_b_im': 'grad_w', 'grad_c_re': 'grad_w', 'grad_c_im': 'grad_w', 'grad_d_skip': 'grad_w', 'grad_w_glu': 'grad_w', 'grad_b_glu': 'grad_w', 'grad_g_attn_out': 'grad_w', 'grad_g_ssm_out': 'grad_w', 'grad_w_out': 'grad_w', 'grad_g_ffn2': 'grad_w', 'grad_w1_b': 'grad_w', 'grad_w3_b': 'grad_w', 'grad_w2_b': 'grad_w', 'grad_g_ple': 'grad_w', 'grad_w_ple_gate': 'grad_w', 'grad_w_ple_proj': 'grad_w', 'grad_g_final': 'grad_w', 'delta_g_ffn1': 'delta_w', 'delta_w1_a': 'delta_w', 'delta_w3_a': 'delta_w', 'delta_w2_a': 'delta_w', 'delta_g_mix': 'delta_w', 'delta_w_in': 'delta_w', 'delta_b_f': 'delta_w', 'delta_a_re': 'delta_w', 'delta_a_im': 'delta_w', 'delta_log_dt': 'delta_w', 'delta_b_re': 'delta_w', 'delta_b_im': 'delta_w', 'delta_c_re': 'delta_w', 'delta_c_im': 'delta_w', 'delta_d_skip': 'delta_w', 'delta_w_glu': 'delta_w', 'delta_b_glu': 'delta_w', 'delta_g_attn_out': 'delta_w', 'delta_g_ssm_out': 'delta_w', 'delta_w_out': 'delta_w', 'delta_g_ffn2': 'delta_w', 'delta_w1_b': 'delta_w', 'delta_w3_b': 'delta_w', 'delta_w2_b': 'delta_w', 'delta_g_ple': 'delta_w', 'delta_w_ple_gate': 'delta_w', 'delta_w_ple_proj': 'delta_w', 'delta_g_final': 'delta_w', 'new_m_g_ffn1': 'new_m', 'new_m_w1_a': 'new_m', 'new_m_w3_a': 'new_m', 'new_m_w2_a': 'new_m', 'new_m_g_mix': 'new_m', 'new_m_w_in': 'new_m', 'new_m_b_f': 'new_m', 'new_m_a_re': 'new_m', 'new_m_a_im': 'new_m', 'new_m_log_dt': 'new_m', 'new_m_b_re': 'new_m', 'new_m_b_im': 'new_m', 'new_m_c_re': 'new_m', 'new_m_c_im': 'new_m', 'new_m_d_skip': 'new_m', 'new_m_w_glu': 'new_m', 'new_m_b_glu': 'new_m', 'new_m_g_attn_out': 'new_m', 'new_m_g_ssm_out': 'new_m', 'new_m_w_out': 'new_m', 'new_m_g_ffn2': 'new_m', 'new_m_w1_b': 'new_m', 'new_m_w3_b': 'new_m', 'new_m_w2_b': 'new_m', 'new_m_g_ple': 'new_m', 'new_m_w_ple_gate': 'new_m', 'new_m_w_ple_proj': 'new_m', 'new_m_g_final': 'new_m', 'new_v_g_ffn1': 'new_v', 'new_v_w1_a': 'new_v', 'new_v_w3_a': 'new_v', 'new_v_w2_a': 'new_v', 'new_v_g_mix': 'new_v', 'new_v_w_in': 'new_v', 'new_v_b_f': 'new_v', 'new_v_a_re': 'new_v', 'new_v_a_im': 'new_v', 'new_v_log_dt': 'new_v', 'new_v_b_re': 'new_v', 'new_v_b_im': 'new_v', 'new_v_c_re': 'new_v', 'new_v_c_im': 'new_v', 'new_v_d_skip': 'new_v', 'new_v_w_glu': 'new_v', 'new_v_b_glu': 'new_v', 'new_v_g_attn_out': 'new_v', 'new_v_g_ssm_out': 'new_v', 'new_v_w_out': 'new_v', 'new_v_g_ffn2': 'new_v', 'new_v_w1_b': 'new_v', 'new_v_w3_b': 'new_v', 'new_v_w2_b': 'new_v', 'new_v_g_ple': 'new_v', 'new_v_w_ple_gate': 'new_v', 'new_v_w_ple_proj': 'new_v', 'new_v_g_final': 'new_v'}


def _forward(args):
    return _fwd_reference(*[args[k] for k in FWD_PARAMS])


def _output_shape():
    def fwd():
        inp = _fwd_setup_inputs(0)
        return _fwd_reference(*[inp[k] for k in FWD_PARAMS])
    out = _jax.eval_shape(fwd)
    return out.shape, out.dtype

N_MICROBATCH = 1
ADAM_LR = 0.001
ADAM_B1 = 0.9
ADAM_B2 = 0.999
ADAM_EPS = 1e-08
ADAM_WD = 0.01
ADAM_STEP = 10
PER_EXAMPLE_BATCH_AXIS = {'x': 0, 'p': 1, 'loss_target': 0}
SHARED_INPUTS = []
_WEIGHT_DTYPES = {'g_ffn1': _jnp.float32, 'w1_a': _jnp.float32, 'w3_a': _jnp.float32, 'w2_a': _jnp.float32, 'g_mix': _jnp.float32, 'w_in': _jnp.float32, 'b_f': _jnp.float32, 'a_re': _jnp.float32, 'a_im': _jnp.float32, 'log_dt': _jnp.float32, 'b_re': _jnp.float32, 'b_im': _jnp.float32, 'c_re': _jnp.float32, 'c_im': _jnp.float32, 'd_skip': _jnp.float32, 'w_glu': _jnp.float32, 'b_glu': _jnp.float32, 'g_attn_out': _jnp.float32, 'g_ssm_out': _jnp.float32, 'w_out': _jnp.float32, 'g_ffn2': _jnp.float32, 'w1_b': _jnp.float32, 'w3_b': _jnp.float32, 'w2_b': _jnp.float32, 'g_ple': _jnp.float32, 'w_ple_gate': _jnp.float32, 'w_ple_proj': _jnp.float32, 'g_final': _jnp.float32}
MOMENT_SCALE = {'g_ffn1': 1.192934e-01, 'w1_a': 5.036165e-02, 'w3_a': 4.885732e-02, 'w2_a': 8.091364e-02, 'g_mix': 2.244620e-01, 'w_in': 1.560583e-01, 'b_f': 6.733846e-01, 'a_re': 7.720846e-03, 'a_im': 9.838084e-03, 'log_dt': 7.465336e+00, 'b_re': 6.164728e-03, 'b_im': 6.051419e-03, 'c_re': 1.236302e-02, 'c_im': 1.284116e-02, 'd_skip': 3.192591e-01, 'w_glu': 5.889214e-02, 'b_glu': 1.251427e-01, 'g_attn_out': 1.738540e-01, 'g_ssm_out': 3.024424e-01, 'w_out': 2.424817e-01, 'g_ffn2': 7.185943e-02, 'w1_b': 3.046417e-02, 'w3_b': 3.071334e-02, 'w2_b': 5.074683e-02, 'g_ple': 4.438577e-02, 'w_ple_gate': 4.751987e-02, 'w_ple_proj': 9.007168e-02, 'g_final': 6.414729e+01}


def _to_microbatches(a, axis):
    t = _jnp.moveaxis(a, axis, 0)
    t = t.reshape((N_MICROBATCH, t.shape[0] // N_MICROBATCH) + t.shape[1:])
    return _jnp.moveaxis(t, 1, axis + 1)


def setup_inputs(seed: int = 0) -> dict:
    inp = _fwd_setup_inputs(seed)
    key = _jax.random.fold_in(_jax.random.key(seed), 7919)
    shape, _ = _output_shape()
    out = dict(inp)
    out["loss_target"] = _jax.random.normal(_jax.random.fold_in(key, 0), shape, _jnp.float32)
    for i, name in enumerate(TWIN_WEIGHTS):
        w = inp[name].astype(_jnp.float32)
        if MOMENT_SCALE is None:
            s = _jnp.sqrt(_jnp.mean(_jnp.square(w)) + 1e-30)
        else:
            s = MOMENT_SCALE[name]
        km, kv = _jax.random.split(_jax.random.fold_in(key, i + 1))
        out[name] = w
        out["m_" + name] = s * _jax.random.normal(km, w.shape, _jnp.float32)
        out["v_" + name] = (s * s) * _jax.random.uniform(kv, w.shape, _jnp.float32, 0.5, 1.5)
    if N_MICROBATCH > 1:
        for name, axis in PER_EXAMPLE_BATCH_AXIS.items():
            out[name] = _to_microbatches(out[name], axis)
    return {'x': out['x'], 'p': out['p'], 'g_ffn1': out['g_ffn1'], 'w1_a': out['w1_a'], 'w3_a': out['w3_a'], 'w2_a': out['w2_a'], 'g_mix': out['g_mix'], 'w_in': out['w_in'], 'b_f': out['b_f'], 'a_re': out['a_re'], 'a_im': out['a_im'], 'log_dt': out['log_dt'], 'b_re': out['b_re'], 'b_im': out['b_im'], 'c_re': out['c_re'], 'c_im': out['c_im'], 'd_skip': out['d_skip'], 'w_glu': out['w_glu'], 'b_glu': out['b_glu'], 'g_attn_out': out['g_attn_out'], 'g_ssm_out': out['g_ssm_out'], 'w_out': out['w_out'], 'g_ffn2': out['g_ffn2'], 'w1_b': out['w1_b'], 'w3_b': out['w3_b'], 'w2_b': out['w2_b'], 'g_ple': out['g_ple'], 'w_ple_gate': out['w_ple_gate'], 'w_ple_proj': out['w_ple_proj'], 'g_final': out['g_final'], 'loss_target': out['loss_target'], 'm_g_ffn1': out['m_g_ffn1'], 'm_w1_a': out['m_w1_a'], 'm_w3_a': out['m_w3_a'], 'm_w2_a': out['m_w2_a'], 'm_g_mix': out['m_g_mix'], 'm_w_in': out['m_w_in'], 'm_b_f': out['m_b_f'], 'm_a_re': out['m_a_re'], 'm_a_im': out['m_a_im'], 'm_log_dt': out['m_log_dt'], 'm_b_re': out['m_b_re'], 'm_b_im': out['m_b_im'], 'm_c_re': out['m_c_re'], 'm_c_im': out['m_c_im'], 'm_d_skip': out['m_d_skip'], 'm_w_glu': out['m_w_glu'], 'm_b_glu': out['m_b_glu'], 'm_g_attn_out': out['m_g_attn_out'], 'm_g_ssm_out': out['m_g_ssm_out'], 'm_w_out': out['m_w_out'], 'm_g_ffn2': out['m_g_ffn2'], 'm_w1_b': out['m_w1_b'], 'm_w3_b': out['m_w3_b'], 'm_w2_b': out['m_w2_b'], 'm_g_ple': out['m_g_ple'], 'm_w_ple_gate': out['m_w_ple_gate'], 'm_w_ple_proj': out['m_w_ple_proj'], 'm_g_final': out['m_g_final'], 'v_g_ffn1': out['v_g_ffn1'], 'v_w1_a': out['v_w1_a'], 'v_w3_a': out['v_w3_a'], 'v_w2_a': out['v_w2_a'], 'v_g_mix': out['v_g_mix'], 'v_w_in': out['v_w_in'], 'v_b_f': out['v_b_f'], 'v_a_re': out['v_a_re'], 'v_a_im': out['v_a_im'], 'v_log_dt': out['v_log_dt'], 'v_b_re': out['v_b_re'], 'v_b_im': out['v_b_im'], 'v_c_re': out['v_c_re'], 'v_c_im': out['v_c_im'], 'v_d_skip': out['v_d_skip'], 'v_w_glu': out['v_w_glu'], 'v_b_glu': out['v_b_glu'], 'v_g_attn_out': out['v_g_attn_out'], 'v_g_ssm_out': out['v_g_ssm_out'], 'v_w_out': out['v_w_out'], 'v_g_ffn2': out['v_g_ffn2'], 'v_w1_b': out['v_w1_b'], 'v_w3_b': out['v_w3_b'], 'v_w2_b': out['v_w2_b'], 'v_g_ple': out['v_g_ple'], 'v_w_ple_gate': out['v_w_ple_gate'], 'v_w_ple_proj': out['v_w_ple_proj'], 'v_g_final': out['v_g_final']}


def _loss(weights, diff, rest, loss_target):
    with _jax.named_scope("forward"):
        args = {**rest, TWIN_DIFF_INPUT: diff, **{k: w.astype(_WEIGHT_DTYPES[k]) for k, w in weights.items()}}
        y = _forward(args)
    with _jax.named_scope("loss_head"):
        err = _jnp.square(y.astype(_jnp.float32) - loss_target)
        return 0.5 * _jnp.sum(_jnp.mean(err, axis=-1)) if err.ndim else 0.5 * err


def _adamw(w, g, m, v):
    m = ADAM_B1 * m + (1.0 - ADAM_B1) * g
    v = ADAM_B2 * v + (1.0 - ADAM_B2) * _jnp.square(g)
    m_hat = m / (1.0 - ADAM_B1 ** ADAM_STEP)
    v_hat = v / (1.0 - ADAM_B2 ** ADAM_STEP)
    delta = -ADAM_LR * (m_hat / (_jnp.sqrt(v_hat) + ADAM_EPS) + ADAM_WD * w)
    return delta, m, v


def reference(x, p, g_ffn1, w1_a, w3_a, w2_a, g_mix, w_in, b_f, a_re, a_im, log_dt, b_re, b_im, c_re, c_im, d_skip, w_glu, b_glu, g_attn_out, g_ssm_out, w_out, g_ffn2, w1_b, w3_b, w2_b, g_ple, w_ple_gate, w_ple_proj, g_final, loss_target, m_g_ffn1, m_w1_a, m_w3_a, m_w2_a, m_g_mix, m_w_in, m_b_f, m_a_re, m_a_im, m_log_dt, m_b_re, m_b_im, m_c_re, m_c_im, m_d_skip, m_w_glu, m_b_glu, m_g_attn_out, m_g_ssm_out, m_w_out, m_g_ffn2, m_w1_b, m_w3_b, m_w2_b, m_g_ple, m_w_ple_gate, m_w_ple_proj, m_g_final, v_g_ffn1, v_w1_a, v_w3_a, v_w2_a, v_g_mix, v_w_in, v_b_f, v_a_re, v_a_im, v_log_dt, v_b_re, v_b_im, v_c_re, v_c_im, v_d_skip, v_w_glu, v_b_glu, v_g_attn_out, v_g_ssm_out, v_w_out, v_g_ffn2, v_w1_b, v_w3_b, v_w2_b, v_g_ple, v_w_ple_gate, v_w_ple_proj, v_g_final):
    given = dict(x=x, p=p, g_ffn1=g_ffn1, w1_a=w1_a, w3_a=w3_a, w2_a=w2_a, g_mix=g_mix, w_in=w_in, b_f=b_f, a_re=a_re, a_im=a_im, log_dt=log_dt, b_re=b_re, b_im=b_im, c_re=c_re, c_im=c_im, d_skip=d_skip, w_glu=w_glu, b_glu=b_glu, g_attn_out=g_attn_out, g_ssm_out=g_ssm_out, w_out=w_out, g_ffn2=g_ffn2, w1_b=w1_b, w3_b=w3_b, w2_b=w2_b, g_ple=g_ple, w_ple_gate=w_ple_gate, w_ple_proj=w_ple_proj, g_final=g_final, loss_target=loss_target, m_g_ffn1=m_g_ffn1, m_w1_a=m_w1_a, m_w3_a=m_w3_a, m_w2_a=m_w2_a, m_g_mix=m_g_mix, m_w_in=m_w_in, m_b_f=m_b_f, m_a_re=m_a_re, m_a_im=m_a_im, m_log_dt=m_log_dt, m_b_re=m_b_re, m_b_im=m_b_im, m_c_re=m_c_re, m_c_im=m_c_im, m_d_skip=m_d_skip, m_w_glu=m_w_glu, m_b_glu=m_b_glu, m_g_attn_out=m_g_attn_out, m_g_ssm_out=m_g_ssm_out, m_w_out=m_w_out, m_g_ffn2=m_g_ffn2, m_w1_b=m_w1_b, m_w3_b=m_w3_b, m_w2_b=m_w2_b, m_g_ple=m_g_ple, m_w_ple_gate=m_w_ple_gate, m_w_ple_proj=m_w_ple_proj, m_g_final=m_g_final, v_g_ffn1=v_g_ffn1, v_w1_a=v_w1_a, v_w3_a=v_w3_a, v_w2_a=v_w2_a, v_g_mix=v_g_mix, v_w_in=v_w_in, v_b_f=v_b_f, v_a_re=v_a_re, v_a_im=v_a_im, v_log_dt=v_log_dt, v_b_re=v_b_re, v_b_im=v_b_im, v_c_re=v_c_re, v_c_im=v_c_im, v_d_skip=v_d_skip, v_w_glu=v_w_glu, v_b_glu=v_b_glu, v_g_attn_out=v_g_attn_out, v_g_ssm_out=v_g_ssm_out, v_w_out=v_w_out, v_g_ffn2=v_g_ffn2, v_w1_b=v_w1_b, v_w3_b=v_w3_b, v_w2_b=v_w2_b, v_g_ple=v_g_ple, v_w_ple_gate=v_w_ple_gate, v_w_ple_proj=v_w_ple_proj, v_g_final=v_g_final)
    weights = {n: given[n] for n in TWIN_WEIGHTS}
    shared = {n: given[n] for n in SHARED_INPUTS}
    per_example = {n: given[n] for n in ['x', 'p']}
    grad_fn = _jax.value_and_grad(_loss, argnums=(0, 1))

    def one_microbatch(ex, loss_target):
        ex = dict(ex)
        diff = ex.pop(TWIN_DIFF_INPUT)
        return grad_fn(weights, diff, {**shared, **ex}, loss_target)

    if N_MICROBATCH == 1:
        loss, (grad_w, grad_x) = one_microbatch(per_example, given["loss_target"])
    else:
        def body(carry, xs):
            loss_sum, grad_sum = carry
            l_k, (gw_k, gx_k) = one_microbatch(xs[0], xs[1])
            with _jax.named_scope("update"):
                return (loss_sum + l_k, _jax.tree.map(_jnp.add, grad_sum, gw_k)), gx_k

        init = (_jnp.zeros((), _jnp.float32), _jax.tree.map(_jnp.zeros_like, weights))
        (loss, grad_w), grad_x = _jax.lax.scan(body, init, (per_example, given["loss_target"]))
    with _jax.named_scope("update"):
        delta_w, new_m, new_v = {}, {}, {}
        for n in TWIN_WEIGHTS:
            delta_w[n], new_m[n], new_v[n] = _adamw(weights[n], grad_w[n], given["m_" + n], given["v_" + n])
    return (loss, grad_x, *[grad_w[n] for n in TWIN_WEIGHTS], *[delta_w[n] for n in TWIN_WEIGHTS],
            *[new_m[n] for n in TWIN_WEIGHTS], *[new_v[n] for n in TWIN_WEIGHTS])
```

```python
import functools
import math

import jax
import jax.numpy as jnp
from jax import lax
from jax.experimental import pallas as pl
from jax.experimental.pallas import tpu as pltpu

F32 = jnp.float32
BF16 = jnp.bfloat16

D_MODEL = 1024
D_FF = 2816
N_HEADS = 8
HEAD_DIM = 64
ATTN_W = 512
SSM_W = 512
N_GROUPS = 32
N_STATE = 64
GROUP_CH = 16
STATE_W = N_GROUPS * N_STATE
Z_COLS = 2176
EPS = 1e-6

ADAM_LR = 0.001
ADAM_B1 = 0.9
ADAM_B2 = 0.999
ADAM_EPS = 1e-08
ADAM_WD = 0.01
ADAM_STEP = 10

TOKEN_TILE = 512
FF_TILE = 256
ATTN_TILE = 512
SCAN_STEPS = 32
SCAN_LANES = 512
VMEM_LIMIT = 48 * 1024 * 1024

NT_DIMS = (((1,), (1,)), ((), ()))
TN_DIMS = (((0,), (0,)), ((), ()))
HIGHEST = lax.Precision.HIGHEST
MESH = pl.DeviceIdType.MESH

BIG = (
    ("w1_a", (1024, 704), 1), ("w3_a", (1024, 704), 1), ("w2_a", (704, 1024), 0),
    ("w_in", (1024, 514), 1), ("w_glu", (128, 512), 0), ("w_out", (256, 1024), 0),
    ("w1_b", (1024, 704), 1), ("w3_b", (1024, 704), 1), ("w2_b", (704, 1024), 0),
    ("w_ple_gate", (256, 1024), 0), ("w_ple_proj", (256, 256), 1),
)
PACK_COLS = 1024
PACK_ALIGN = 16
PACK_ROWS = 5408
SMALL = (
    ("g_ffn1", (1, 1024)), ("g_mix", (1, 1024)), ("b_f", (1, 8)), ("a_re", (1, 32, 64)), ("a_im", (1, 32, 64)),
    ("log_dt", (1, 32)), ("b_re", (1, 32, 64, 16)), ("b_im", (1, 32, 64, 16)), ("c_re", (1, 32, 16, 64)),
    ("c_im", (1, 32, 16, 64)), ("d_skip", (1, 32, 16)), ("b_glu", (1, 512)), ("g_attn_out", (1, 512)),
    ("g_ssm_out", (1, 512)), ("g_ffn2", (1, 1024)), ("g_ple", (1, 1024)), ("g_final", (1024,)),
)
SMALL_ROWS = 1120
WEIGHT_ORDER = ("g_ffn1", "w1_a", "w3_a", "w2_a", "g_mix", "w_in", "b_f", "a_re", "a_im", "log_dt", "b_re", "b_im",
                "c_re", "c_im", "d_skip", "w_glu", "b_glu", "g_attn_out", "g_ssm_out", "w_out", "g_ffn2", "w1_b",
                "w3_b", "w2_b", "g_ple", "w_ple_gate", "w_ple_proj", "g_final")


def _params(sem=None):
    kw = dict(vmem_limit_bytes=VMEM_LIMIT)
    if sem is not None:
        kw["dimension_semantics"] = sem
    return pltpu.CompilerParams(**kw)


def _sds(shape, dtype):
    return jax.ShapeDtypeStruct(shape, dtype)


def _tile(n, pref):
    t = min(n, pref)
    assert n % t == 0, (n, pref)
    return t


def _rms_scale(x):
    return lax.rsqrt(jnp.mean(x * x, axis=-1, keepdims=True) + EPS)


def _rms_bwd(dy, x, g):
    r = _rms_scale(x)
    xh = x * r
    dxh = dy * g
    dx = r * (dxh - xh * jnp.mean(dxh * xh, axis=-1, keepdims=True))
    return dx, jnp.sum(dy * xh, axis=0, keepdims=True)


def _dot(a, b):
    return jnp.dot(a, b, preferred_element_type=F32)


def _dot_nt(a, b):
    return lax.dot_general(a, b, NT_DIMS, preferred_element_type=F32)


def _dot_tn(a, b):
    return lax.dot_general(a, b, TN_DIMS, preferred_element_type=F32)


_GELU_C = math.sqrt(2.0 / math.pi)


def _gelu_parts(x):
    t = jnp.tanh(_GELU_C * (x + 0.044715 * x * x * x))
    return 0.5 * x * (1.0 + t), t


def _gelu_grad(x, t):
    return 0.5 * (1.0 + t) + 0.5 * x * (1.0 - t * t) * _GELU_C * (1.0 + 3.0 * 0.044715 * x * x)


def ffn_fwd(h, g, w1, w3, w2, name):
    T = h.shape[0]
    tm, tf = _tile(T, TOKEN_TILE), FF_TILE
    n_f = D_FF // tf

    def body(h_ref, g_ref, w1_ref, w3_ref, w2_ref, ho_ref, a1_ref, a3_ref, n_ref, n_sc, acc):
        j = pl.program_id(1)

        @pl.when(j == 0)
        def _():
            x = h_ref[...]
            nb = (x * _rms_scale(x) * g_ref[...]).astype(BF16)
            n_sc[...] = nb
            n_ref[...] = nb
            acc[...] = jnp.zeros_like(acc)

        n = n_sc[...]
        a1 = _dot(n, w1_ref[...])
        a3 = _dot(n, w3_ref[...])
        a1_ref[...] = a1.astype(BF16)
        a3_ref[...] = a3.astype(BF16)
        act = (a1 * jax.nn.sigmoid(a1) * a3).astype(BF16)
        acc[...] += _dot(act, w2_ref[...])

        @pl.when(j == n_f - 1)
        def _():
            ho_ref[...] = h_ref[...] + 0.5 * acc[...]

    return pl.pallas_call(
        body, name=name, grid=(T // tm, n_f),
        in_specs=[pl.BlockSpec((tm, D_MODEL), lambda i, j: (i, 0)), pl.BlockSpec((1, D_MODEL), lambda i, j: (0, 0)),
                  pl.BlockSpec((D_MODEL, tf), lambda i, j: (0, j)), pl.BlockSpec((D_MODEL, tf), lambda i, j: (0, j)),
                  pl.BlockSpec((tf, D_MODEL), lambda i, j: (j, 0))],
        out_specs=[pl.BlockSpec((tm, D_MODEL), lambda i, j: (i, 0)), pl.BlockSpec((tm, tf), lambda i, j: (i, j)),
                   pl.BlockSpec((tm, tf), lambda i, j: (i, j)), pl.BlockSpec((tm, D_MODEL), lambda i, j: (i, 0))],
        out_shape=[_sds((T, D_MODEL), F32), _sds((T, D_FF), BF16), _sds((T, D_FF), BF16), _sds((T, D_MODEL), BF16)],
        scratch_shapes=[pltpu.VMEM((tm, D_MODEL), BF16), pltpu.VMEM((tm, D_MODEL), F32)],
        compiler_params=_params(("arbitrary", "arbitrary")),
    )(h, g, w1, w3, w2)


def ffn_bwd(h, g, dho, a1, a3, w1, w3, w2, name):
    T = h.shape[0]
    tm, tf = _tile(T, TOKEN_TILE), FF_TILE
    n_f = D_FF // tf

    def body(h_ref, g_ref, dho_ref, a1_ref, a3_ref, w1_ref, w3_ref, w2_ref,
             dhi_ref, da1_ref, da3_ref, act_ref, dg_ref, dhb, dn):
        i, j = pl.program_id(0), pl.program_id(1)

        @pl.when((i == 0) & (j == 0))
        def _():
            dg_ref[...] = jnp.zeros_like(dg_ref)

        @pl.when(j == 0)
        def _():
            dhb[...] = (0.5 * dho_ref[...]).astype(BF16)
            dn[...] = jnp.zeros_like(dn)

        a1v = a1_ref[...].astype(F32)
        a3v = a3_ref[...].astype(F32)
        s = jax.nn.sigmoid(a1v)
        sl = a1v * s
        dact = _dot_nt(dhb[...], w2_ref[...])
        act_ref[...] = (sl * a3v).astype(BF16)
        da1 = (dact * a3v * s * (1.0 + a1v * (1.0 - s))).astype(BF16)
        da3 = (dact * sl).astype(BF16)
        da1_ref[...] = da1
        da3_ref[...] = da3
        dn[...] += _dot_nt(da1, w1_ref[...]) + _dot_nt(da3, w3_ref[...])

        @pl.when(j == n_f - 1)
        def _():
            dx, dg = _rms_bwd(dn[...], h_ref[...], g_ref[...])
            dg_ref[...] += dg
            dhi_ref[...] = dho_ref[...] + dx

    tok = lambda i, j: (i, 0)
    return pl.pallas_call(
        body, name=name, grid=(T // tm, n_f),
        in_specs=[pl.BlockSpec((tm, D_MODEL), tok), pl.BlockSpec((1, D_MODEL), lambda i, j: (0, 0)),
                  pl.BlockSpec((tm, D_MODEL), tok), pl.BlockSpec((tm, tf), lambda i, j: (i, j)),
                  pl.BlockSpec((tm, tf), lambda i, j: (i, j)), pl.BlockSpec((D_MODEL, tf), lambda i, j: (0, j)),
                  pl.BlockSpec((D_MODEL, tf), lambda i, j: (0, j)), pl.BlockSpec((tf, D_MODEL), lambda i, j: (j, 0))],
        out_specs=[pl.BlockSpec((tm, D_MODEL), tok), pl.BlockSpec((tm, tf), lambda i, j: (i, j)),
                   pl.BlockSpec((tm, tf), lambda i, j: (i, j)), pl.BlockSpec((tm, tf), lambda i, j: (i, j)),
                   pl.BlockSpec((1, D_MODEL), lambda i, j: (0, 0))],
        out_shape=[_sds((T, D_MODEL), F32), _sds((T, D_FF), BF16), _sds((T, D_FF), BF16), _sds((T, D_FF), BF16),
                   _sds((1, D_MODEL), F32)],
        scratch_shapes=[pltpu.VMEM((tm, D_MODEL), BF16), pltpu.VMEM((tm, D_MODEL), F32)],
        compiler_params=_params(("arbitrary", "arbitrary")),
    )(h, g, dho, a1, a3, w1, w3, w2)


def mm_tn(a, b, name, scale=1.0):
    T, M = a.shape
    N = b.shape[1]
    bm = 512 if M % 512 == 0 else 256
    bn = N if N in (2176, 1408) else (1408 if N == 2816 else min(N, 1024))
    tk = _tile(T, TOKEN_TILE)
    assert M % bm == 0 and N % bn == 0
    n_k = T // tk

    def body(a_ref, b_ref, o_ref):
        k = pl.program_id(2)

        @pl.when(k == 0)
        def _():
            o_ref[...] = jnp.zeros_like(o_ref)

        o_ref[...] += _dot_tn(a_ref[...].astype(BF16), b_ref[...].astype(BF16))

        if scale != 1.0:
            @pl.when(k == n_k - 1)
            def _():
                o_ref[...] = o_ref[...] * scale

    return pl.pallas_call(
        body, name=name, grid=(M // bm, N // bn, n_k),
        in_specs=[pl.BlockSpec((tk, bm), lambda m, n, k: (k, m)), pl.BlockSpec((tk, bn), lambda m, n, k: (k, n))],
        out_specs=pl.BlockSpec((bm, bn), lambda m, n, k: (m, n)),
        out_shape=_sds((M, N), F32),
        compiler_params=_params(("arbitrary", "arbitrary", "arbitrary")),
    )(a, b)


def mixin_fwd(h1, g, w_in_r, b_f_pad):
    T = h1.shape[0]
    tm = _tile(T, TOKEN_TILE)

    def body(h_ref, g_ref, w_ref, bf_ref, u_ref, qkv_ref, s_ref, fz_ref, c_ref, carry):
        @pl.when(pl.program_id(0) == 0)
        def _():
            carry[...] = jnp.zeros_like(carry)

        x = h_ref[...]
        u = (x * _rms_scale(x) * g_ref[...]).astype(BF16)
        u_ref[...] = u
        z = _dot(u, w_ref[...])
        qkv_ref[...] = z[:, :3 * ATTN_W].astype(BF16)
        s_ref[...] = z[:, 3 * ATTN_W:3 * ATTN_W + SSM_W]
        fz = z[:, 3 * ATTN_W + SSM_W:] + bf_ref[...]
        fz_ref[...] = fz
        lane = lax.broadcasted_iota(jnp.int32, fz.shape, 1)
        logf = jnp.where(lane < N_HEADS, jnp.minimum(fz, 0.0) - jnp.log(1.0 + jnp.exp(-jnp.abs(fz))), 0.0)
        row = lax.broadcasted_iota(jnp.int32, (tm, tm), 0)
        col = lax.broadcasted_iota(jnp.int32, (tm, tm), 1)
        tri = (col <= row).astype(F32)
        cs = jnp.dot(tri, logf, precision=HIGHEST, preferred_element_type=F32) + carry[0:1, :]
        c_ref[...] = cs
        carry[...] = jnp.broadcast_to(cs[tm - 1:tm, :], carry.shape)

    tok = lambda i: (i, 0)
    fix = lambda i: (0, 0)
    return pl.pallas_call(
        body, name="mixin_fwd", grid=(T // tm,),
        in_specs=[pl.BlockSpec((tm, D_MODEL), tok), pl.BlockSpec((1, D_MODEL), fix),
                  pl.BlockSpec((D_MODEL, Z_COLS), fix), pl.BlockSpec((1, 128), fix)],
        out_specs=[pl.BlockSpec((tm, D_MODEL), tok), pl.BlockSpec((tm, 3 * ATTN_W), tok), pl.BlockSpec((tm, SSM_W), tok),
                   pl.BlockSpec((tm, 128), tok), pl.BlockSpec((tm, 128), tok)],
        out_shape=[_sds((T, D_MODEL), BF16), _sds((T, 3 * ATTN_W), BF16), _sds((T, SSM_W), F32),
                   _sds((T, 128), F32), _sds((T, 128), F32)],
        scratch_shapes=[pltpu.VMEM((8, 128), F32)],
        compiler_params=_params(("arbitrary",)),
    )(h1, g, w_in_r, b_f_pad)


def mixin_bwd(dh2, h1, g, w_in_r, dq, dk, dv, ds, dc, fz):
    T = h1.shape[0]
    tm = _tile(T, TOKEN_TILE)
    n_t = T // tm

    def body(dh2_ref, h_ref, g_ref, w_ref, dq_ref, dk_ref, dv_ref, ds_ref, dc_ref, fz_ref,
             dh1_ref, dz_ref, dg_ref, dbf_ref, carry):
        @pl.when(pl.program_id(0) == 0)
        def _():
            carry[...] = jnp.zeros_like(carry)
            dg_ref[...] = jnp.zeros_like(dg_ref)
            dbf_ref[...] = jnp.zeros_like(dbf_ref)

        row = lax.broadcasted_iota(jnp.int32, (tm, tm), 0)
        col = lax.broadcasted_iota(jnp.int32, (tm, tm), 1)
        tri = (col >= row).astype(F32)
        dlogf = jnp.dot(tri, dc_ref[...], precision=HIGHEST, preferred_element_type=F32) + carry[0:1, :]
        carry[...] = jnp.broadcast_to(dlogf[0:1, :], carry.shape)
        dfz = dlogf * jax.nn.sigmoid(-fz_ref[...])
        dbf_ref[...] += jnp.sum(dfz, axis=0, keepdims=True)
        dz = jnp.concatenate([dq_ref[...], dk_ref[...], dv_ref[...], ds_ref[...], dfz], axis=1).astype(BF16)
        dz_ref[...] = dz
        du = _dot_nt(dz, w_ref[...])
        dx, dg = _rms_bwd(du, h_ref[...], g_ref[...])
        dg_ref[...] += dg
        dh1_ref[...] = dh2_ref[...] + dx

    tok = lambda i: (n_t - 1 - i, 0)
    fix = lambda i: (0, 0)
    return pl.pallas_call(
        body, name="mixin_bwd", grid=(n_t,),
        in_specs=[pl.BlockSpec((tm, D_MODEL), tok), pl.BlockSpec((tm, D_MODEL), tok), pl.BlockSpec((1, D_MODEL), fix),
                  pl.BlockSpec((D_MODEL, Z_COLS), fix), pl.BlockSpec((tm, ATTN_W), tok), pl.BlockSpec((tm, ATTN_W), tok),
                  pl.BlockSpec((tm, ATTN_W), tok), pl.BlockSpec((tm, SSM_W), tok), pl.BlockSpec((tm, 128), tok),
                  pl.BlockSpec((tm, 128), tok)],
        out_specs=[pl.BlockSpec((tm, D_MODEL), tok), pl.BlockSpec((tm, Z_COLS), tok), pl.BlockSpec((1, D_MODEL), fix),
                   pl.BlockSpec((1, 128), fix)],
        out_shape=[_sds((T, D_MODEL), F32), _sds((T, Z_COLS), BF16), _sds((1, D_MODEL), F32), _sds((1, 128), F32)],
        scratch_shapes=[pltpu.VMEM((8, 128), F32)],
        compiler_params=_params(("arbitrary",)),
    )(dh2, h1, g, w_in_r, dq, dk, dv, ds, dc, fz)


def attn_fwd(q, k, v, c_col, c_row):
    H, T, hd = q.shape
    tq = _tile(T, ATTN_TILE)
    n = T // tq
    scale = 1.0 / math.sqrt(hd)

    def body(q_ref, k_ref, v_ref, cq_ref, ck_ref, o_ref, lse_ref, m_sc, l_sc, acc):
        qi, kj = pl.program_id(1), pl.program_id(2)

        @pl.when(kj == 0)
        def _():
            m_sc[...] = jnp.full_like(m_sc, -jnp.inf)
            l_sc[...] = jnp.zeros_like(l_sc)
            acc[...] = jnp.zeros_like(acc)

        @pl.when(kj <= qi)
        def _():
            s = _dot_nt(q_ref[0], k_ref[0]) * scale + cq_ref[0] - ck_ref[0]
            rowi = qi * tq + lax.broadcasted_iota(jnp.int32, (tq, tq), 0)
            coli = kj * tq + lax.broadcasted_iota(jnp.int32, (tq, tq), 1)
            s = jnp.where(coli <= rowi, s, -1e30)
            m_new = jnp.maximum(m_sc[...], jnp.max(s, axis=1, keepdims=True))
            alpha = jnp.exp(m_sc[...] - m_new)
            p = jnp.exp(s - m_new)
            l_sc[...] = alpha * l_sc[...] + jnp.sum(p, axis=1, keepdims=True)
            acc[...] = alpha * acc[...] + _dot(p.astype(BF16), v_ref[0])
            m_sc[...] = m_new

        @pl.when(kj == qi)
        def _():
            o_ref[0] = acc[...] / l_sc[...]
            lse_ref[0] = m_sc[...] + jnp.log(l_sc[...])

    qmap = lambda h, i, j: (h, i, 0)
    kmap = lambda h, i, j: (h, jnp.minimum(i, j), 0)
    return pl.pallas_call(
        body, name="attn_fwd", grid=(H, n, n),
        in_specs=[pl.BlockSpec((1, tq, hd), qmap), pl.BlockSpec((1, tq, hd), kmap), pl.BlockSpec((1, tq, hd), kmap),
                  pl.BlockSpec((1, tq, 1), qmap), pl.BlockSpec((1, 1, tq), lambda h, i, j: (h, 0, jnp.minimum(i, j)))],
        out_specs=[pl.BlockSpec((1, tq, hd), qmap), pl.BlockSpec((1, tq, 1), qmap)],
        out_shape=[_sds((H, T, hd), F32), _sds((H, T, 1), F32)],
        scratch_shapes=[pltpu.VMEM((tq, 1), F32), pltpu.VMEM((tq, 1), F32), pltpu.VMEM((tq, hd), F32)],
        compiler_params=_params(("arbitrary", "arbitrary", "arbitrary")),
    )(q, k, v, c_col, c_row)


def attn_bwd(q, k, v, do, lse, delta, c_col, c_row):
    H, T, hd = q.shape
    tq = _tile(T, ATTN_TILE)
    n = T // tq
    scale = 1.0 / math.sqrt(hd)

    def body(q_ref, k_ref, v_ref, do_ref, lse_ref, dl_ref, cq_ref, ck_ref, dq_ref, dk_ref, dv_ref, dc_ref, dr_ref):
        j, i = pl.program_id(1), pl.program_id(2)

        @pl.when((j == 0) & (i == 0))
        def _():
            dq_ref[...] = jnp.zeros_like(dq_ref)
            dr_ref[...] = jnp.zeros_like(dr_ref)

        @pl.when(i == 0)
        def _():
            dk_ref[...] = jnp.zeros_like(dk_ref)
            dv_ref[...] = jnp.zeros_like(dv_ref)
            dc_ref[...] = jnp.zeros_like(dc_ref)

        @pl.when(i >= j)
        def _():
            qv, kv, vv, dov = q_ref[0], k_ref[0], v_ref[0], do_ref[0]
            s = _dot_nt(qv, kv) * scale + cq_ref[0] - ck_ref[0]
            rowi = i * tq + lax.broadcasted_iota(jnp.int32, (tq, tq), 0)
            coli = j * tq + lax.broadcasted_iota(jnp.int32, (tq, tq), 1)
            p = jnp.where(coli <= rowi, jnp.exp(s - lse_ref[0]), 0.0)
            dv_ref[0] += _dot_tn(p.astype(BF16), dov)
            dp = _dot_nt(dov, vv)
            ds = p * (dp - dl_ref[0])
            dsb = ds.astype(BF16)
            rows = pl.ds(pl.multiple_of(i * tq, tq), tq)
            dq_ref[0, rows, :] += _dot(dsb, kv) * scale
            dk_ref[0] += _dot_tn(dsb, qv) * scale
            dc_ref[0] += -jnp.sum(ds, axis=0, keepdims=True)
            dr_ref[0, rows, :] += jnp.sum(ds, axis=1, keepdims=True)

    qmap = lambda h, j, i: (h, jnp.maximum(i, j), 0)
    kmap = lambda h, j, i: (h, j, 0)
    return pl.pallas_call(
        body, name="attn_bwd", grid=(H, n, n),
        in_specs=[pl.BlockSpec((1, tq, hd), qmap), pl.BlockSpec((1, tq, hd), kmap), pl.BlockSpec((1, tq, hd), kmap),
                  pl.BlockSpec((1, tq, hd), qmap), pl.BlockSpec((1, tq, 1), qmap), pl.BlockSpec((1, tq, 1), qmap),
                  pl.BlockSpec((1, tq, 1), qmap), pl.BlockSpec((1, 1, tq), lambda h, j, i: (h, 0, j))],
        out_specs=[pl.BlockSpec((1, T, hd), lambda h, j, i: (h, 0, 0)), pl.BlockSpec((1, tq, hd), kmap),
                   pl.BlockSpec((1, tq, hd), kmap), pl.BlockSpec((1, 1, tq), lambda h, j, i: (h, 0, j)),
                   pl.BlockSpec((1, T, 1), lambda h, j, i: (h, 0, 0))],
        out_shape=[_sds((H, T, hd), F32), _sds((H, T, hd), F32), _sds((H, T, hd), F32), _sds((H, 1, T), F32),
                   _sds((H, T, 1), F32)],
        compiler_params=_params(("arbitrary", "arbitrary", "arbitrary")),
    )(q, k, v, do, lse, delta, c_col, c_row)


def _complex_step(a_r, a_i, cr, ci, br, bi):
    return a_r * cr - a_i * ci + br, a_r * ci + a_i * cr + bi


def ssm_fwd(s_perm, wb, cbd, a_r, a_i, al_r, al_i, dvec):
    T = s_perm.shape[0]
    chunk = T // 8
    ts = _tile(chunk, SCAN_STEPS)
    tr, n_s = ts * 8, chunk // ts
    W, LB = STATE_W, SCAN_LANES

    def body(s_ref, wb_ref, cbd_ref, ar_ref, ai_ref, alr_ref, ali_ref, dv_ref, y_ref, xs_ref, bu, carry):
        ph, i = pl.program_id(0), pl.program_id(1)

        @pl.when((ph == 0) & (i == 0))
        def _():
            carry[...] = jnp.zeros_like(carry)

        bu[...] = _dot(s_ref[...].astype(BF16), wb_ref[...])

        def scan(store):
            for lb in range(W // LB):
                lo = lb * LB
                re, im = slice(lo, lo + LB), slice(W + lo, W + lo + LB)
                ar = jnp.broadcast_to(ar_ref[:, re], (8, LB))
                ai = jnp.broadcast_to(ai_ref[:, re], (8, LB))

                def step(s, c):
                    rows = pl.ds(pl.multiple_of(s * 8, 8), 8)
                    nr, ni = _complex_step(ar, ai, c[0], c[1], bu[rows, re], bu[rows, im])
                    if store:
                        bu[rows, re] = nr
                        bu[rows, im] = ni
                    return nr, ni

                cr, ci = lax.fori_loop(0, ts, step, (carry[:, re], carry[:, im]), unroll=2)
                carry[:, re] = cr
                carry[:, im] = ci

        @pl.when(ph == 0)
        def _():
            scan(False)

            @pl.when(i == n_s - 1)
            def _():
                er, ei = carry[:, :W], carry[:, W:]
                alr = jnp.broadcast_to(alr_ref[...], (8, W))
                ali = jnp.broadcast_to(ali_ref[...], (8, W))
                first = lax.broadcasted_iota(jnp.int32, (8, W), 0) == 0
                sr, si = jnp.zeros((8, W), F32), jnp.zeros((8, W), F32)
                for _ in range(7):
                    vr, vi = _complex_step(alr, ali, sr, si, er, ei)
                    sr = jnp.where(first, 0.0, pltpu.roll(vr, 1, 0))
                    si = jnp.where(first, 0.0, pltpu.roll(vi, 1, 0))
                carry[:, :W] = sr
                carry[:, W:] = si

        @pl.when(ph == 1)
        def _():
            scan(True)
            xb = bu[...].astype(BF16)
            xs_ref[...] = xb
            y_ref[...] = _dot(xb, cbd_ref[...]) + s_ref[...] * dv_ref[...]

    fix = lambda p, i: (0, 0)
    return pl.pallas_call(
        body, name="ssm_fwd", grid=(2, n_s),
        in_specs=[pl.BlockSpec((tr, SSM_W), lambda p, i: (i, 0)), pl.BlockSpec((SSM_W, 2 * W), fix),
                  pl.BlockSpec((2 * W, SSM_W), fix), pl.BlockSpec((1, W), fix), pl.BlockSpec((1, W), fix),
                  pl.BlockSpec((1, W), fix), pl.BlockSpec((1, W), fix), pl.BlockSpec((1, SSM_W), fix)],
        out_specs=[pl.BlockSpec((tr, SSM_W), lambda p, i: (i * p, 0)), pl.BlockSpec((tr, 2 * W), lambda p, i: (i * p, 0))],
        out_shape=[_sds((T, SSM_W), F32), _sds((T, 2 * W), BF16)],
        scratch_shapes=[pltpu.VMEM((tr, 2 * W), F32), pltpu.VMEM((8, 2 * W), F32)],
        compiler_params=_params(("arbitrary", "arbitrary")),
    )(s_perm, wb, cbd, a_r, a_i, al_r, al_i, dvec)


def ssm_bwd(dy_perm, s_perm, xs, cbd_t, wb_t, a_r, a_i, al_r, al_i, dvec):
    T = s_perm.shape[0]
    chunk = T // 8
    ts = _tile(chunk, SCAN_STEPS)
    tr, n_s = ts * 8, chunk // ts
    W, LB = STATE_W, SCAN_LANES

    def body(dy_ref, s_ref, xs_ref, cbt_ref, wbt_ref, ar_ref, ai_ref, alr_ref, ali_ref, dv_ref,
             du_ref, gs_ref, da_ref, dd_ref, gd, x32, carry):
        ph, i = pl.program_id(0), pl.program_id(1)

        @pl.when((ph == 0) & (i == 0))
        def _():
            carry[...] = jnp.zeros_like(carry)
            da_ref[...] = jnp.zeros_like(da_ref)
            dd_ref[...] = jnp.zeros_like(dd_ref)

        gd[...] = _dot(dy_ref[...].astype(BF16), cbt_ref[...])

        def scan(store):
            for lb in range(W // LB):
                lo = lb * LB
                re, im = slice(lo, lo + LB), slice(W + lo, W + lo + LB)
                ar = jnp.broadcast_to(ar_ref[:, re], (8, LB))
                nai = -jnp.broadcast_to(ai_ref[:, re], (8, LB))

                def step(k, c):
                    rows = pl.ds(pl.multiple_of((ts - 1 - k) * 8, 8), 8)
                    cr, ci = c[0], c[1]
                    nr, ni = _complex_step(ar, nai, cr, ci, gd[rows, re], gd[rows, im])
                    if store:
                        xr, xi = x32[rows, re], x32[rows, im]
                        gd[rows, re] = nr
                        gd[rows, im] = ni
                        return nr, ni, c[2] + cr * xr + ci * xi, c[3] + ci * xr - cr * xi
                    return nr, ni

                init = (carry[:, re], carry[:, im])
                if store:
                    init = init + (da_ref[:, re], da_ref[:, im])
                out = lax.fori_loop(0, ts, step, init, unroll=2)
                carry[:, re] = out[0]
                carry[:, im] = out[1]
                if store:
                    da_ref[:, re] = out[2]
                    da_ref[:, im] = out[3]

        @pl.when(ph == 0)
        def _():
            scan(False)

            @pl.when(i == n_s - 1)
            def _():
                er, ei = carry[:, :W], carry[:, W:]
                alr = jnp.broadcast_to(alr_ref[...], (8, W))
                nali = -jnp.broadcast_to(ali_ref[...], (8, W))
                last = lax.broadcasted_iota(jnp.int32, (8, W), 0) == 7
                rr, ri = jnp.zeros((8, W), F32), jnp.zeros((8, W), F32)
                for _ in range(7):
                    vr, vi = _complex_step(alr, nali, rr, ri, er, ei)
                    rr = jnp.where(last, 0.0, pltpu.roll(vr, 7, 0))
                    ri = jnp.where(last, 0.0, pltpu.roll(vi, 7, 0))
                carry[:, :W] = rr
                carry[:, W:] = ri

        @pl.when(ph == 1)
        def _():
            x32[...] = xs_ref[...].astype(F32)
            scan(True)
            gb = gd[...].astype(BF16)
            gs_ref[...] = gb
            dy = dy_ref[...]
            du_ref[...] = _dot(gb, wbt_ref[...]) + dy * dv_ref[...]
            dd_ref[...] += jnp.sum(dy * s_ref[...], axis=0, keepdims=True)

    fix = lambda p, i: (0, 0)
    rev = lambda p, i: (n_s - 1 - i, 0)
    rev_out = lambda p, i: (n_s - 1 - i * p, 0)
    return pl.pallas_call(
        body, name="ssm_bwd", grid=(2, n_s),
        in_specs=[pl.BlockSpec((tr, SSM_W), rev), pl.BlockSpec((tr, SSM_W), rev), pl.BlockSpec((tr, 2 * W), rev),
                  pl.BlockSpec((SSM_W, 2 * W), fix), pl.BlockSpec((2 * W, SSM_W), fix), pl.BlockSpec((1, W), fix),
                  pl.BlockSpec((1, W), fix), pl.BlockSpec((1, W), fix), pl.BlockSpec((1, W), fix),
                  pl.BlockSpec((1, SSM_W), fix)],
        out_specs=[pl.BlockSpec((tr, SSM_W), rev_out), pl.BlockSpec((tr, 2 * W), rev_out),
                   pl.BlockSpec((8, 2 * W), fix), pl.BlockSpec((1, SSM_W), fix)],
        out_shape=[_sds((T, SSM_W), F32), _sds((T, 2 * W), BF16), _sds((8, 2 * W), F32), _sds((1, SSM_W), F32)],
        scratch_shapes=[pltpu.VMEM((tr, 2 * W), F32), pltpu.VMEM((tr, 2 * W), F32), pltpu.VMEM((8, 2 * W), F32)],
        compiler_params=_params(("arbitrary", "arbitrary")),
    )(dy_perm, s_perm, xs, cbd_t, wb_t, a_r, a_i, al_r, al_i, dvec)


def mixout_fwd(h1, attn, ypre, g_a, g_s, w_glu, b_glu, w_out):
    T = h1.shape[0]
    tm = _tile(T, TOKEN_TILE)

    def body(h_ref, at_ref, yp_ref, ga_ref, gs_ref, wg_ref, bg_ref, wo_ref, h2_ref, mixed_ref):
        yg, _ = _gelu_parts(yp_ref[...])
        gl = yg * jax.nn.sigmoid(_dot(yg.astype(BF16), wg_ref[...]) + bg_ref[...])
        at = at_ref[...]
        mixed = jnp.concatenate([at * _rms_scale(at) * ga_ref[...], gl * _rms_scale(gl) * gs_ref[...]], axis=1)
        mixed = mixed.astype(BF16)
        mixed_ref[...] = mixed
        h2_ref[...] = h_ref[...] + _dot(mixed, wo_ref[...])

    tok = lambda i: (i, 0)
    fix = lambda i: (0, 0)
    return pl.pallas_call(
        body, name="mixout_fwd", grid=(T // tm,),
        in_specs=[pl.BlockSpec((tm, D_MODEL), tok), pl.BlockSpec((tm, ATTN_W), tok), pl.BlockSpec((tm, SSM_W), tok),
                  pl.BlockSpec((1, ATTN_W), fix), pl.BlockSpec((1, SSM_W), fix), pl.BlockSpec((SSM_W, SSM_W), fix),
                  pl.BlockSpec((1, SSM_W), fix), pl.BlockSpec((D_MODEL, D_MODEL), fix)],
        out_specs=[pl.BlockSpec((tm, D_MODEL), tok), pl.BlockSpec((tm, D_MODEL), tok)],
        out_shape=[_sds((T, D_MODEL), F32), _sds((T, D_MODEL), BF16)],
        compiler_params=_params(("arbitrary",)),
    )(h1, attn, ypre, g_a, g_s, w_glu, b_glu, w_out)


def mixout_bwd(dh2, attn, ypre, g_a, g_s, w_glu, b_glu, w_out, seg):
    T = dh2.shape[0]
    tm = _tile(T, TOKEN_TILE)

    def body(dh_ref, at_ref, yp_ref, ga_ref, gs_ref, wg_ref, bg_ref, wo_ref, seg_ref,
             dat_ref, dyp_ref, dpre_ref, yg_ref, dl_ref, dga_ref, dgs_ref, dbg_ref):
        @pl.when(pl.program_id(0) == 0)
        def _():
            dga_ref[...] = jnp.zeros_like(dga_ref)
            dgs_ref[...] = jnp.zeros_like(dgs_ref)
            dbg_ref[...] = jnp.zeros_like(dbg_ref)

        dmix = _dot_nt(dh_ref[...].astype(BF16), wo_ref[...])
        at = at_ref[...]
        dat, dga = _rms_bwd(dmix[:, :ATTN_W], at, ga_ref[...])
        dga_ref[...] += dga
        dat_ref[...] = dat
        dl_ref[...] = jnp.dot(dat * at, seg_ref[...], precision=HIGHEST, preferred_element_type=F32)
        yp = yp_ref[...]
        yg, t = _gelu_parts(yp)
        ygb = yg.astype(BF16)
        yg_ref[...] = ygb
        sg = jax.nn.sigmoid(_dot(ygb, wg_ref[...]) + bg_ref[...])
        dgl, dgs = _rms_bwd(dmix[:, ATTN_W:], yg * sg, gs_ref[...])
        dgs_ref[...] += dgs
        dpre = dgl * yg * sg * (1.0 - sg)
        dbg_ref[...] += jnp.sum(dpre, axis=0, keepdims=True)
        dpb = dpre.astype(BF16)
        dpre_ref[...] = dpb
        dyg = dgl * sg + _dot_nt(dpb, wg_ref[...])
        dyp_ref[...] = dyg * _gelu_grad(yp, t)

    tok = lambda i: (i, 0)
    fix = lambda i: (0, 0)
    return pl.pallas_call(
        body, name="mixout_bwd", grid=(T // tm,),
        in_specs=[pl.BlockSpec((tm, D_MODEL), tok), pl.BlockSpec((tm, ATTN_W), tok), pl.BlockSpec((tm, SSM_W), tok),
                  pl.BlockSpec((1, ATTN_W), fix), pl.BlockSpec((1, SSM_W), fix), pl.BlockSpec((SSM_W, SSM_W), fix),
                  pl.BlockSpec((1, SSM_W), fix), pl.BlockSpec((D_MODEL, D_MODEL), fix), pl.BlockSpec((ATTN_W, 128), fix)],
        out_specs=[pl.BlockSpec((tm, ATTN_W), tok), pl.BlockSpec((tm, SSM_W), tok), pl.BlockSpec((tm, SSM_W), tok),
                   pl.BlockSpec((tm, SSM_W), tok), pl.BlockSpec((tm, 128), tok), pl.BlockSpec((1, ATTN_W), fix),
                   pl.BlockSpec((1, SSM_W), fix), pl.BlockSpec((1, SSM_W), fix)],
        out_shape=[_sds((T, ATTN_W), F32), _sds((T, SSM_W), F32), _sds((T, SSM_W), BF16), _sds((T, SSM_W), BF16),
                   _sds((T, 128), F32), _sds((1, ATTN_W), F32), _sds((1, SSM_W), F32), _sds((1, SSM_W), F32)],
        compiler_params=_params(("arbitrary",)),
    )(dh2, attn, ypre, g_a, g_s, w_glu, b_glu, w_out, seg)


def head_fwd_bwd(h3, p, target, g_ple, g_final, w_gate, w_proj):
    T = h3.shape[0]
    tm = _tile(T, TOKEN_TILE)
    pd = p.shape[1]

    def body(h_ref, p_ref, tg_ref, gp_ref, gf_ref, wg_ref, wp_ref,
             dh_ref, n3_ref, dz_ref, dpp_ref, loss_ref, dgp_ref, dgf_ref):
        @pl.when(pl.program_id(0) == 0)
        def _():
            loss_ref[...] = jnp.zeros_like(loss_ref)
            dgp_ref[...] = jnp.zeros_like(dgp_ref)
            dgf_ref[...] = jnp.zeros_like(dgf_ref)

        x = h_ref[...]
        gp, gf = gp_ref[...], gf_ref[...]
        n3 = (x * _rms_scale(x) * gp).astype(BF16)
        n3_ref[...] = n3
        gate = jax.nn.sigmoid(_dot(n3, wg_ref[...]))
        pp = _dot(p_ref[...].astype(BF16), wp_ref[...])
        h4 = x + gate * pp
        y = h4 * _rms_scale(h4) * gf
        e = y - tg_ref[...]
        tile_loss = jnp.sum(jnp.sum(e * e, axis=1, keepdims=True), axis=0, keepdims=True) * (0.5 / D_MODEL)
        loss_ref[...] += jnp.broadcast_to(tile_loss, loss_ref.shape)
        dh4, dgf = _rms_bwd(e * (1.0 / D_MODEL), h4, gf)
        dgf_ref[...] += dgf
        dzg = dh4 * pp * gate * (1.0 - gate)
        dzb = dzg.astype(BF16)
        dz_ref[...] = dzb
        dpp_ref[...] = (dh4 * gate).astype(BF16)
        dx, dgp = _rms_bwd(_dot_nt(dzb, wg_ref[...]), x, gp)
        dgp_ref[...] += dgp
        dh_ref[...] = dh4 + dx

    tok = lambda i: (i, 0)
    fix = lambda i: (0, 0)
    return pl.pallas_call(
        body, name="head_fwd_bwd", grid=(T // tm,),
        in_specs=[pl.BlockSpec((tm, D_MODEL), tok), pl.BlockSpec((tm, pd), tok), pl.BlockSpec((tm, D_MODEL), tok),
                  pl.BlockSpec((1, D_MODEL), fix), pl.BlockSpec((1, D_MODEL), fix), pl.BlockSpec((D_MODEL, D_MODEL), fix),
                  pl.BlockSpec((pd, D_MODEL), fix)],
        out_specs=[pl.BlockSpec((tm, D_MODEL), tok), pl.BlockSpec((tm, D_MODEL), tok), pl.BlockSpec((tm, D_MODEL), tok),
                   pl.BlockSpec((tm, D_MODEL), tok), pl.BlockSpec((8, 128), fix), pl.BlockSpec((1, D_MODEL), fix),
                   pl.BlockSpec((1, D_MODEL), fix)],
        out_shape=[_sds((T, D_MODEL), F32), _sds((T, D_MODEL), BF16), _sds((T, D_MODEL), BF16), _sds((T, D_MODEL), BF16),
                   _sds((8, 128), F32), _sds((1, D_MODEL), F32), _sds((1, D_MODEL), F32)],
        compiler_params=_params(("arbitrary",)),
    )(h3, p, target, g_ple, g_final, w_gate, w_proj)


def _row_tile(rows, cols, n_arrays):
    cap = VMEM_LIMIT // 3 // (2 * n_arrays * cols * 4)
    best = 8
    for t in range(8, min(rows, cap) + 1, 8):
        if rows % t == 0:
            best = t
    return best


def add_arrays(parts, name):
    R, C = parts[0].shape
    tr = _row_tile(R, C, len(parts) + 1)

    def body(*refs):
        acc = refs[0][...]
        for r in refs[1:-1]:
            acc = acc + r[...]
        refs[-1][...] = acc

    spec = pl.BlockSpec((tr, C), lambda i: (i, 0))
    return pl.pallas_call(
        body, name=name, grid=(R // tr,), in_specs=[spec] * len(parts), out_specs=spec,
        out_shape=_sds((R, C), F32), compiler_params=_params(("arbitrary",)),
    )(*parts)


def adamw(w, g, m, v, name):
    R, C = w.shape
    tr = _row_tile(R, C, 7)
    c1 = 1.0 - ADAM_B1 ** ADAM_STEP
    c2 = 1.0 - ADAM_B2 ** ADAM_STEP

    def body(w_ref, g_ref, m_ref, v_ref, d_ref, nm_ref, nv_ref):
        gv = g_ref[...]
        nm = ADAM_B1 * m_ref[...] + (1.0 - ADAM_B1) * gv
        nv = ADAM_B2 * v_ref[...] + (1.0 - ADAM_B2) * (gv * gv)
        nm_ref[...] = nm
        nv_ref[...] = nv
        d_ref[...] = -ADAM_LR * ((nm / c1) / (jnp.sqrt(nv / c2) + ADAM_EPS) + ADAM_WD * w_ref[...])

    spec = pl.BlockSpec((tr, C), lambda i: (i, 0))
    return pl.pallas_call(
        body, name=name, grid=(R // tr,), in_specs=[spec] * 4, out_specs=[spec] * 3,
        out_shape=[_sds((R, C), F32)] * 3, compiler_params=_params(("arbitrary",)),
    )(w, g, m, v)


_HBM = pl.BlockSpec(memory_space=pltpu.HBM)


def _place():
    x, y, c = lax.axis_index("x"), lax.axis_index("y"), lax.axis_index("c")
    return x, y, c, [(1 - x, y), (x, 1 - y), (1 - x, 1 - y)]


def _remote(src, dst, send_sem, recv_sem, to):
    return pltpu.make_async_remote_copy(src_ref=src, dst_ref=dst, send_sem=send_sem, recv_sem=recv_sem,
                                        device_id=to, device_id_type=MESH)


def allgather_shards(wp):
    R, C = wp.shape
    rh = R // 2

    def body(w_ref, out_ref, send_sems, recv_sems, pass_send, pass_recv, local_sem):
        x, y, c, chips = _place()
        me = 2 * x + y
        mine = pltpu.make_async_copy(w_ref, out_ref.at[me], local_sem)
        mine.start()
        my_rows, sib_rows = pl.ds(c * rh, rh), pl.ds((1 - c) * rh, rh)
        sends = []
        for k, (cx, cy) in enumerate(chips):
            cp = _remote(w_ref.at[my_rows, :], out_ref.at[me, my_rows, :], send_sems.at[k], recv_sems.at[k], (cx, cy, c))
            cp.start()
            sends.append(cp)
        for k, (cx, cy) in enumerate(chips):
            blk = out_ref.at[2 * cx + cy, my_rows, :]
            _remote(blk, blk, send_sems.at[k], recv_sems.at[k], (cx, cy, c)).wait_recv()
            cp = _remote(blk, blk, pass_send.at[k], pass_recv.at[k], (x, y, 1 - c))
            cp.start()
            sends.append(cp)
        for k, (cx, cy) in enumerate(chips):
            blk = out_ref.at[2 * cx + cy, sib_rows, :]
            _remote(blk, blk, pass_send.at[k], pass_recv.at[k], (x, y, 1 - c)).wait_recv()
        for cp in sends:
            cp.wait_send()
        mine.wait()

    return pl.pallas_call(
        body, name="allgather_shards", in_specs=[_HBM], out_specs=_HBM, out_shape=_sds((4, R, C), wp.dtype),
        scratch_shapes=[pltpu.SemaphoreType.DMA((3,)), pltpu.SemaphoreType.DMA((3,)), pltpu.SemaphoreType.DMA((3,)),
                        pltpu.SemaphoreType.DMA((3,)), pltpu.SemaphoreType.DMA],
    )(wp)


def sibling_split(g):
    n, R, C = g.shape
    rh = R // 2

    def body(g_ref, own_ref, got_ref, send_sem, recv_sem, local_sem):
        x, y, c, _ = _place()
        mine = pltpu.make_async_copy(g_ref.at[:, pl.ds(c * rh, rh), :], own_ref, local_sem)
        mine.start()
        cp = _remote(g_ref.at[:, pl.ds((1 - c) * rh, rh), :], got_ref, send_sem, recv_sem, (x, y, 1 - c))
        cp.start()
        cp.wait()
        mine.wait()

    return pl.pallas_call(
        body, name="sibling_split", in_specs=[_HBM], out_specs=[_HBM, _HBM],
        out_shape=[_sds((n, rh, C), g.dtype), _sds((n, rh, C), g.dtype)],
        scratch_shapes=[pltpu.SemaphoreType.DMA, pltpu.SemaphoreType.DMA, pltpu.SemaphoreType.DMA],
    )(g)


def chip_exchange(p):
    n, R, C = p.shape

    def body(p_ref, buf_ref, send_sems, recv_sems, local_sem):
        x, y, c, chips = _place()
        me = 2 * x + y
        mine = pltpu.make_async_copy(p_ref.at[me], buf_ref.at[me], local_sem)
        mine.start()
        sends = []
        for k, (cx, cy) in enumerate(chips):
            cp = _remote(p_ref.at[2 * cx + cy], buf_ref.at[me], send_sems.at[k], recv_sems.at[k], (cx, cy, c))
            cp.start()
            sends.append(cp)
        for k, (cx, cy) in enumerate(chips):
            blk = buf_ref.at[2 * cx + cy]
            _remote(blk, blk, send_sems.at[k], recv_sems.at[k], (cx, cy, c)).wait_recv()
        for cp in sends:
            cp.wait_send()
        mine.wait()

    return pl.pallas_call(
        body, name="chip_exchange", in_specs=[_HBM], out_specs=_HBM, out_shape=_sds((n, R, C), p.dtype),
        scratch_shapes=[pltpu.SemaphoreType.DMA((3,)), pltpu.SemaphoreType.DMA((3,)), pltpu.SemaphoreType.DMA],
    )(p)


def sibling_join(half):
    rh, C = half.shape

    def body(h_ref, out_ref, send_sem, recv_sem, local_sem):
        x, y, c, _ = _place()
        rows = pl.ds(c * rh, rh)
        mine = pltpu.make_async_copy(h_ref, out_ref.at[rows, :], local_sem)
        mine.start()
        cp = _remote(h_ref, out_ref.at[rows, :], send_sem, recv_sem, (x, y, 1 - c))
        cp.start()
        cp.wait_send()
        other = out_ref.at[pl.ds((1 - c) * rh, rh), :]
        _remote(other, other, send_sem, recv_sem, (x, y, 1 - c)).wait_recv()
        mine.wait()

    return pl.pallas_call(
        body, name="sibling_join", in_specs=[_HBM], out_specs=_HBM, out_shape=_sds((2 * rh, C), half.dtype),
        scratch_shapes=[pltpu.SemaphoreType.DMA, pltpu.SemaphoreType.DMA, pltpu.SemaphoreType.DMA],
    )(half)


def allreduce_small(v):
    R, C = v.shape

    def body(v_ref, out_ref, buf, send_sems, recv_sems):
        x, y, c, _ = _place()
        me = 4 * x + 2 * y + c
        buf[me] = v_ref[...]
        flips = [((k >> 2) & 1, (k >> 1) & 1, k & 1) for k in range(1, 8)]
        sends = []
        for k, (fx, fy, fc) in enumerate(flips):
            to = (1 - x if fx else x, 1 - y if fy else y, 1 - c if fc else c)
            cp = _remote(v_ref, buf.at[me], send_sems.at[k], recv_sems.at[k], to)
            cp.start()
            sends.append(cp)
        for k, (fx, fy, fc) in enumerate(flips):
            px, py, pc = (1 - x if fx else x, 1 - y if fy else y, 1 - c if fc else c)
            blk = buf.at[4 * px + 2 * py + pc]
            _remote(blk, blk, send_sems.at[k], recv_sems.at[k], (px, py, pc)).wait_recv()
        for cp in sends:
            cp.wait_send()
        acc = buf[0]
        for s in range(1, 8):
            acc = acc + buf[s]
        out_ref[...] = acc

    vm = pl.BlockSpec(memory_space=pltpu.VMEM)
    return pl.pallas_call(
        body, name="allreduce_small", in_specs=[vm], out_specs=vm, out_shape=_sds((R, C), F32),
        scratch_shapes=[pltpu.VMEM((8, R, C), F32), pltpu.SemaphoreType.DMA((7,)), pltpu.SemaphoreType.DMA((7,))],
        compiler_params=pltpu.CompilerParams(vmem_limit_bytes=VMEM_LIMIT),
    )(v)


def _rows_of(shape):
    return shape[0] * shape[1] // PACK_COLS


def _slot_rows(shape):
    return -(-_rows_of(shape) // PACK_ALIGN) * PACK_ALIGN


def _pack_shards(shards, dtype):
    parts = []
    for name, shape, _ in BIG:
        part = shards[name].reshape(_rows_of(shape), PACK_COLS).astype(dtype)
        parts.append(jnp.pad(part, ((0, _slot_rows(shape) - part.shape[0]), (0, 0))))
    used = sum(p.shape[0] for p in parts)
    parts.append(jnp.zeros((PACK_ROWS - used, PACK_COLS), dtype))
    return jnp.concatenate(parts, axis=0)


def _unpack_gathered(ag):
    out, off = {}, 0
    for name, shape, axis in BIG:
        r = _rows_of(shape)
        piece = ag[:, off:off + r, :].reshape((4,) + shape)
        off += _slot_rows(shape)
        if axis == 0:
            out[name] = piece.reshape(4 * shape[0], shape[1])
        else:
            out[name] = piece.transpose(1, 0, 2).reshape(shape[0], 4 * shape[1])
    return out


def _pack_full_grads(grads):
    parts = []
    for name, shape, axis in BIG:
        g = grads[name]
        if axis == 0:
            piece = g.reshape((4,) + shape)
        else:
            piece = g.reshape(shape[0], 4, shape[1]).transpose(1, 0, 2)
        piece = piece.reshape(4, _rows_of(shape), PACK_COLS)
        parts.append(jnp.pad(piece, ((0, 0), (0, _slot_rows(shape) - piece.shape[1]), (0, 0))))
    used = sum(p.shape[1] for p in parts)
    parts.append(jnp.zeros((4, PACK_ROWS - used, PACK_COLS), F32))
    return jnp.concatenate(parts, axis=1)


def _unpack_shards(packed):
    out, off = {}, 0
    for name, shape, _ in BIG:
        r = _rows_of(shape)
        out[name] = packed[off:off + r].reshape((1,) + shape)
        off += _slot_rows(shape)
    return out


def _pack_small(vals, extra=None):
    parts = [vals[name].reshape(-1) for name, _ in SMALL]
    used = sum(p.shape[0] for p in parts)
    if extra is not None:
        parts.append(extra.reshape(1))
        used += 1
    parts.append(jnp.zeros((SMALL_ROWS * 128 - used,), F32))
    return jnp.concatenate(parts).reshape(SMALL_ROWS, 128)


def _unpack_small(packed):
    flat = packed.reshape(-1)
    out, off = {}, 0
    for name, shape in SMALL:
        n = math.prod(shape)
        out[name] = flat[off:off + n].reshape(shape)
        off += n
    return out, flat[off]


def _to_heads(a):
    T = a.shape[0]
    return a.reshape(T, N_HEADS, HEAD_DIM).transpose(1, 0, 2)


def _from_heads(a):
    T = a.shape[1]
    return a.transpose(1, 0, 2).reshape(T, N_HEADS * HEAD_DIM)


def _permute_time(a):
    T, n = a.shape
    return a.reshape(8, T // 8, n).transpose(1, 0, 2).reshape(T, n)


def _unpermute_time(a):
    T, n = a.shape
    return a.reshape(T // 8, 8, n).transpose(1, 0, 2).reshape(T, n)


def _discretize(a_re, a_im, log_dt, b_re, b_im):
    dt = jnp.exp(log_dt)[:, None]
    decay = jnp.exp(dt * a_re)
    abar_r = decay * jnp.cos(dt * a_im)
    abar_i = decay * jnp.sin(dt * a_im)
    nr, ni = abar_r - 1.0, abar_i
    den = a_re * a_re + a_im * a_im
    fr = (nr * a_re + ni * a_im) / den
    fi = (ni * a_re - nr * a_im) / den
    bbar_r = fr[..., None] * b_re - fi[..., None] * b_im
    bbar_i = fr[..., None] * b_im + fi[..., None] * b_re
    return abar_r, abar_i, bbar_r, bbar_i


def _input_matrix(bbar_r, bbar_i):
    eye = jnp.eye(N_GROUPS, dtype=F32)
    blk = lambda b: jnp.einsum("ghp,gk->ghkp", b.transpose(0, 2, 1), eye).reshape(SSM_W, STATE_W)
    return jnp.concatenate([blk(bbar_r), blk(bbar_i)], axis=1)


def _output_matrix(c_re, c_im):
    eye = jnp.eye(N_GROUPS, dtype=F32)
    blk = lambda cm: jnp.einsum("ghp,gk->gpkh", cm, eye).reshape(STATE_W, SSM_W)
    return jnp.concatenate([blk(c_re), -blk(c_im)], axis=0)


def _state_power(ar, ai, n):
    steps = int(round(math.log2(n)))
    assert 1 << steps == n
    for _ in range(steps):
        ar, ai = ar * ar - ai * ai, 2.0 * ar * ai
    return ar, ai


def kernel(x, p, g_ffn1, w1_a, w3_a, w2_a, g_mix, w_in, b_f, a_re, a_im, log_dt, b_re, b_im, c_re, c_im, d_skip, w_glu, b_glu, g_attn_out, g_ssm_out, w_out, g_ffn2, w1_b, w3_b, w2_b, g_ple, w_ple_gate, w_ple_proj, g_final, loss_target, m_g_ffn1, m_w1_a, m_w3_a, m_w2_a, m_g_mix, m_w_in, m_b_f, m_a_re, m_a_im, m_log_dt, m_b_re, m_b_im, m_c_re, m_c_im, m_d_skip, m_w_glu, m_b_glu, m_g_attn_out, m_g_ssm_out, m_w_out, m_g_ffn2, m_w1_b, m_w3_b, m_w2_b, m_g_ple, m_w_ple_gate, m_w_ple_proj, m_g_final, v_g_ffn1, v_w1_a, v_w3_a, v_w2_a, v_g_mix, v_w_in, v_b_f, v_a_re, v_a_im, v_log_dt, v_b_re, v_b_im, v_c_re, v_c_im, v_d_skip, v_w_glu, v_b_glu, v_g_attn_out, v_g_ssm_out, v_w_out, v_g_ffn2, v_w1_b, v_w3_b, v_w2_b, v_g_ple, v_w_ple_gate, v_w_ple_proj, v_g_final):
    args = dict(locals())
    weights = {n: args[n] for n in WEIGHT_ORDER}
    moms = {n: args["m_" + n] for n in WEIGHT_ORDER}
    vars_ = {n: args["v_" + n] for n in WEIGHT_ORDER}
    T = x.shape[1]
    x2, p2, tgt = x[0], p[0, 0], loss_target[0]

    full = _unpack_gathered(allgather_shards(_pack_shards({n: weights[n][0] for n, _, _ in BIG}, BF16)))
    loss_part, dx, grads = _local_step(x2, p2, tgt, {n: weights[n] for n, _ in SMALL}, full)
    return _reduce_and_update(weights, moms, vars_, loss_part, dx, grads)


def _local_step(x2, p2, tgt, sm, full):
    T = x2.shape[0]
    (g_ffn1, g_mix, b_f, a_re, a_im, log_dt, b_re, b_im, c_re, c_im, d_skip, b_glu, g_attn_out, g_ssm_out, g_ffn2, g_ple,
     g_final) = (sm[n] for n, _ in SMALL)
    w_in_f = full["w_in"]
    w_in_r = jnp.concatenate([w_in_f[:, :3 * ATTN_W], w_in_f[:, 3 * ATTN_W + N_HEADS:], w_in_f[:, 3 * ATTN_W:3 * ATTN_W + N_HEADS],
                              jnp.zeros((D_MODEL, 128 - N_HEADS), BF16)], axis=1)
    b_f_pad = jnp.pad(b_f, ((0, 0), (0, 128 - N_HEADS)))

    disc_in = (a_re[0], a_im[0], log_dt[0], b_re[0], b_im[0])
    (abar_r, abar_i, bbar_r, bbar_i), disc_vjp = jax.vjp(_discretize, *disc_in)
    wb = _input_matrix(bbar_r, bbar_i)
    cbd = _output_matrix(c_re[0], c_im[0])
    ar, ai = abar_r.reshape(1, STATE_W), abar_i.reshape(1, STATE_W)
    alr, ali = _state_power(ar, ai, T // 8)
    dvec = d_skip.reshape(1, SSM_W)
    wb16, cbd16 = wb.astype(BF16), cbd.astype(BF16)

    h1, a1a, a3a, n1 = ffn_fwd(x2, g_ffn1, full["w1_a"], full["w3_a"], full["w2_a"], "ffn_a_fwd")
    u, qkv, s_in, fz, cum = mixin_fwd(h1, g_mix, w_in_r, b_f_pad)
    qh, kh, vh = (_to_heads(qkv[:, i * ATTN_W:(i + 1) * ATTN_W]) for i in range(3))
    c_col = cum[:, :N_HEADS].T.reshape(N_HEADS, T, 1)
    c_row = cum[:, :N_HEADS].T.reshape(N_HEADS, 1, T)
    o_heads, lse = attn_fwd(qh, kh, vh, c_col, c_row)
    attn = _from_heads(o_heads)
    s_perm = _permute_time(s_in)
    y_perm, xs = ssm_fwd(s_perm, wb16, cbd16, ar, ai, alr, ali, dvec)
    ypre = _unpermute_time(y_perm)
    h2, mixed = mixout_fwd(h1, attn, ypre, g_attn_out, g_ssm_out, full["w_glu"], b_glu, full["w_out"])
    h3, a1b, a3b, n2 = ffn_fwd(h2, g_ffn2, full["w1_b"], full["w3_b"], full["w2_b"], "ffn_b_fwd")

    dh3, n3, dzg, dpp, loss_part, dg_ple, dg_final = head_fwd_bwd(
        h3, p2, tgt, g_ple, g_final.reshape(1, D_MODEL), full["w_ple_gate"], full["w_ple_proj"])
    grads = {"g_ple": dg_ple, "g_final": dg_final.reshape(D_MODEL)}
    grads["w_ple_gate"] = mm_tn(n3, dzg, "dw_ple_gate")
    grads["w_ple_proj"] = mm_tn(p2, dpp, "dw_ple_proj")

    dh2, da1, da3, act, grads["g_ffn2"] = ffn_bwd(h2, g_ffn2, dh3, a1b, a3b, full["w1_b"], full["w3_b"], full["w2_b"], "ffn_b_bwd")
    grads["w1_b"] = mm_tn(n2, da1, "dw1_b")
    grads["w3_b"] = mm_tn(n2, da3, "dw3_b")
    grads["w2_b"] = mm_tn(act, dh3, "dw2_b", scale=0.5)

    seg = (jnp.arange(ATTN_W)[:, None] // HEAD_DIM == jnp.arange(128)[None, :]).astype(F32)
    dattn, dypre, dpre, yg, delta, grads["g_attn_out"], grads["g_ssm_out"], grads["b_glu"] = mixout_bwd(
        dh2, attn, ypre, g_attn_out, g_ssm_out, full["w_glu"], b_glu, full["w_out"], seg)
    grads["w_out"] = mm_tn(mixed, dh2, "dw_out")
    grads["w_glu"] = mm_tn(yg, dpre, "dw_glu")

    delta_h = delta[:, :N_HEADS].T.reshape(N_HEADS, T, 1)
    dqh, dkh, dvh, dck, dcq = attn_bwd(qh, kh, vh, _to_heads(dattn).astype(BF16), lse, delta_h, c_col, c_row)
    dc = jnp.pad((dck.reshape(N_HEADS, T) + dcq.reshape(N_HEADS, T)).T, ((0, 0), (0, 128 - N_HEADS)))

    dy_perm = _permute_time(dypre)
    du_perm, gs, d_a, dd = ssm_bwd(dy_perm, s_perm, xs, cbd16.T, wb16.T, ar, ai, alr, ali, dvec)
    ds_in = _unpermute_time(du_perm)
    d_wb = mm_tn(s_perm, gs, "dw_ssm_in")
    d_cbd = mm_tn(xs, dy_perm, "dw_ssm_out")
    diag_in = lambda m: jnp.einsum("ghgp->ghp", m.reshape(N_GROUPS, GROUP_CH, N_GROUPS, N_STATE)).transpose(0, 2, 1)
    diag_out = lambda m: jnp.einsum("gpgh->gph", m.reshape(N_GROUPS, N_STATE, N_GROUPS, GROUP_CH)).transpose(0, 2, 1)
    d_abar_r = jnp.sum(d_a[:, :STATE_W], axis=0).reshape(N_GROUPS, N_STATE)
    d_abar_i = jnp.sum(d_a[:, STATE_W:], axis=0).reshape(N_GROUPS, N_STATE)
    d_disc = disc_vjp((d_abar_r, d_abar_i, diag_in(d_wb[:, :STATE_W]), diag_in(d_wb[:, STATE_W:])))
    for name, val in zip(("a_re", "a_im", "log_dt", "b_re", "b_im"), d_disc):
        grads[name] = val[None]
    grads["c_re"] = diag_out(d_cbd[:STATE_W])[None]
    grads["c_im"] = -diag_out(d_cbd[STATE_W:])[None]
    grads["d_skip"] = dd.reshape(1, N_GROUPS, GROUP_CH)

    dh1, dz, grads["g_mix"], dbf = mixin_bwd(dh2, h1, g_mix, w_in_r, _from_heads(dqh), _from_heads(dkh), _from_heads(dvh),
                                             ds_in, dc, fz)
    grads["b_f"] = dbf[:, :N_HEADS]
    d_w_in_r = mm_tn(u, dz, "dw_in")
    grads["w_in"] = jnp.concatenate([d_w_in_r[:, :3 * ATTN_W], d_w_in_r[:, 3 * ATTN_W + SSM_W:3 * ATTN_W + SSM_W + N_HEADS],
                                     d_w_in_r[:, 3 * ATTN_W:3 * ATTN_W + SSM_W]], axis=1)

    dx, da1, da3, act, grads["g_ffn1"] = ffn_bwd(x2, g_ffn1, dh1, a1a, a3a, full["w1_a"], full["w3_a"], full["w2_a"], "ffn_a_bwd")
    grads["w1_a"] = mm_tn(n1, da1, "dw1_a")
    grads["w3_a"] = mm_tn(n1, da3, "dw3_a")
    grads["w2_a"] = mm_tn(act, dh1, "dw2_a", scale=0.5)
    return loss_part, dx, grads


def _reduce_and_update(weights, moms, vars_, loss_part, dx, grads):
    own, theirs = sibling_split(_pack_full_grads(grads))
    pair = add_arrays([own.reshape(-1, PACK_COLS), theirs.reshape(-1, PACK_COLS)], "pair_sum")
    got = chip_exchange(pair.reshape(4, PACK_ROWS // 2, PACK_COLS))
    half = add_arrays([got[0], got[1], got[2], got[3]], "chip_sum")
    g_big = sibling_join(half)
    d_big, m_big, v_big = adamw(_pack_shards({n: weights[n][0] for n, _, _ in BIG}, F32), g_big,
                                _pack_shards({n: moms[n][0] for n, _, _ in BIG}, F32),
                                _pack_shards({n: vars_[n][0] for n, _, _ in BIG}, F32), "adamw_big")

    small = allreduce_small(_pack_small({n: grads[n] for n, _ in SMALL}, extra=loss_part[0, 0]))
    d_small, m_small, v_small = adamw(_pack_small(weights), small, _pack_small(moms), _pack_small(vars_), "adamw_small")

    g_out, loss = _unpack_small(small)
    g_out.update(_unpack_shards(g_big))
    outs = []
    for big, sm in ((d_big, d_small), (m_big, m_small), (v_big, v_small)):
        o, _ = _unpack_small(sm)
        o.update(_unpack_shards(big))
        outs.append(o)
    result = [loss, dx[None]] + [g_out[n] for n in WEIGHT_ORDER]
    for o in outs:
        result += [o[n] for n in WEIGHT_ORDER]
    return tuple(result)
```

```python
import functools
import math

import jax
import jax.numpy as jnp
from jax import lax
from jax.experimental import pallas as pl
from jax.experimental.pallas import tpu as pltpu

F32 = jnp.float32
BF16 = jnp.bfloat16

D_MODEL = 1024
D_FF = 2816
N_HEADS = 8
HEAD_DIM = 64
ATTN_W = 512
SSM_W = 512
N_GROUPS = 32
N_STATE = 64
GROUP_CH = 16
STATE_W = N_GROUPS * N_STATE
Z_COLS = 2176
QK_SCALE = 0.125
EPS = 1e-6

ADAM_LR = 0.001
ADAM_B1 = 0.9
ADAM_B2 = 0.999
ADAM_EPS = 1e-08
ADAM_WD = 0.01
ADAM_STEP = 10

TOKEN_TILE = 512
FF_TILE = 256
ATTN_TILE = 512
SCAN_STEPS = 32
SCAN_LANES = 512
VMEM_LIMIT = 48 * 1024 * 1024
COPY_CHUNKS = 4

NT_DIMS = (((1,), (1,)), ((), ()))
TN_DIMS = (((0,), (0,)), ((), ()))
HIGHEST = lax.Precision.HIGHEST
MESH = pl.DeviceIdType.MESH

BIG = (
    ("w1_a", (1024, 704), 1), ("w3_a", (1024, 704), 1), ("w2_a", (704, 1024), 0),
    ("w_in", (1024, 514), 1), ("w_glu", (128, 512), 0), ("w_out", (256, 1024), 0),
    ("w1_b", (1024, 704), 1), ("w3_b", (1024, 704), 1), ("w2_b", (704, 1024), 0),
    ("w_ple_gate", (256, 1024), 0), ("w_ple_proj", (256, 256), 1),
)
PACK_COLS = 1024
PACK_ALIGN = 16
PACK_ROWS = 5408
SMALL = (
    ("g_ffn1", (1, 1024)), ("g_mix", (1, 1024)), ("b_f", (1, 8)), ("a_re", (1, 32, 64)), ("a_im", (1, 32, 64)),
    ("log_dt", (1, 32)), ("b_re", (1, 32, 64, 16)), ("b_im", (1, 32, 64, 16)), ("c_re", (1, 32, 16, 64)),
    ("c_im", (1, 32, 16, 64)), ("d_skip", (1, 32, 16)), ("b_glu", (1, 512)), ("g_attn_out", (1, 512)),
    ("g_ssm_out", (1, 512)), ("g_ffn2", (1, 1024)), ("g_ple", (1, 1024)), ("g_final", (1024,)),
)
SMALL_ROWS = 1120
WEIGHT_ORDER = ("g_ffn1", "w1_a", "w3_a", "w2_a", "g_mix", "w_in", "b_f", "a_re", "a_im", "log_dt", "b_re", "b_im",
                "c_re", "c_im", "d_skip", "w_glu", "b_glu", "g_attn_out", "g_ssm_out", "w_out", "g_ffn2", "w1_b",
                "w3_b", "w2_b", "g_ple", "w_ple_gate", "w_ple_proj", "g_final")


def _params(sem=None):
    kw = dict(vmem_limit_bytes=VMEM_LIMIT)
    if sem is not None:
        kw["dimension_semantics"] = sem
    return pltpu.CompilerParams(**kw)


def _sds(shape, dtype):
    return jax.ShapeDtypeStruct(shape, dtype)


def _tile(n, pref):
    t = min(n, pref)
    assert n % t == 0, (n, pref)
    return t


def _rms_scale(x):
    return lax.rsqrt(jnp.mean(x * x, axis=-1, keepdims=True) + EPS)


def _rms_bwd(dy, x, g):
    r = _rms_scale(x)
    xh = x * r
    dxh = dy * g
    dx = r * (dxh - xh * jnp.mean(dxh * xh, axis=-1, keepdims=True))
    return dx, jnp.sum(dy * xh, axis=0, keepdims=True)


def _dot(a, b):
    return jnp.dot(a, b, preferred_element_type=F32)


def _dot_nt(a, b):
    return lax.dot_general(a, b, NT_DIMS, preferred_element_type=F32)


def _dot_tn(a, b):
    return lax.dot_general(a, b, TN_DIMS, preferred_element_type=F32)


_GELU_C = math.sqrt(2.0 / math.pi)


def _gelu_parts(x):
    t = jnp.tanh(_GELU_C * (x + 0.044715 * x * x * x))
    return 0.5 * x * (1.0 + t), t


def _gelu_grad(x, t):
    return 0.5 * (1.0 + t) + 0.5 * x * (1.0 - t * t) * _GELU_C * (1.0 + 3.0 * 0.044715 * x * x)


def ffn_fwd(h, g, w1, w3, w2, name):
    T = h.shape[0]
    tm, tf = _tile(T, TOKEN_TILE), FF_TILE
    n_f = D_FF // tf

    def body(h_ref, g_ref, w1_ref, w3_ref, w2_ref, ho_ref, a1_ref, a3_ref, n_ref, n_sc, acc):
        j = pl.program_id(1)

        @pl.when(j == 0)
        def _():
            x = h_ref[...]
            nb = (x * _rms_scale(x) * g_ref[...]).astype(BF16)
            n_sc[...] = nb
            n_ref[...] = nb
            acc[...] = jnp.zeros_like(acc)

        n = n_sc[...]
        a1 = _dot(n, w1_ref[...])
        a3 = _dot(n, w3_ref[...])
        a1_ref[...] = a1.astype(BF16)
        a3_ref[...] = a3.astype(BF16)
        act = (a1 * jax.nn.sigmoid(a1) * a3).astype(BF16)
        acc[...] += _dot(act, w2_ref[...])

        @pl.when(j == n_f - 1)
        def _():
            ho_ref[...] = h_ref[...] + 0.5 * acc[...]

    return pl.pallas_call(
        body, name=name, grid=(T // tm, n_f),
        in_specs=[pl.BlockSpec((tm, D_MODEL), lambda i, j: (i, 0)), pl.BlockSpec((1, D_MODEL), lambda i, j: (0, 0)),
                  pl.BlockSpec((D_MODEL, tf), lambda i, j: (0, j)), pl.BlockSpec((D_MODEL, tf), lambda i, j: (0, j)),
                  pl.BlockSpec((tf, D_MODEL), lambda i, j: (j, 0))],
        out_specs=[pl.BlockSpec((tm, D_MODEL), lambda i, j: (i, 0)), pl.BlockSpec((tm, tf), lambda i, j: (i, j)),
                   pl.BlockSpec((tm, tf), lambda i, j: (i, j)), pl.BlockSpec((tm, D_MODEL), lambda i, j: (i, 0))],
        out_shape=[_sds((T, D_MODEL), F32), _sds((T, D_FF), BF16), _sds((T, D_FF), BF16), _sds((T, D_MODEL), BF16)],
        scratch_shapes=[pltpu.VMEM((tm, D_MODEL), BF16), pltpu.VMEM((tm, D_MODEL), F32)],
        compiler_params=_params(("arbitrary", "arbitrary")),
    )(h, g, w1, w3, w2)


def ffn_bwd(h, g, dho, a1, a3, w1, w3, w2, name):
    T = h.shape[0]
    tm, tf = _tile(T, TOKEN_TILE), FF_TILE
    n_f = D_FF // tf

    def body(h_ref, g_ref, dho_ref, a1_ref, a3_ref, w1_ref, w3_ref, w2_ref,
             dhi_ref, da1_ref, da3_ref, act_ref, dg_ref, dhb, dn):
        i, j = pl.program_id(0), pl.program_id(1)

        @pl.when((i == 0) & (j == 0))
        def _():
            dg_ref[...] = jnp.zeros_like(dg_ref)

        @pl.when(j == 0)
        def _():
            dhb[...] = (0.5 * dho_ref[...]).astype(BF16)
            dn[...] = jnp.zeros_like(dn)

        a1v = a1_ref[...].astype(F32)
        a3v = a3_ref[...].astype(F32)
        s = jax.nn.sigmoid(a1v)
        sl = a1v * s
        dact = _dot_nt(dhb[...], w2_ref[...])
        act_ref[...] = (sl * a3v).astype(BF16)
        da1 = (dact * a3v * s * (1.0 + a1v * (1.0 - s))).astype(BF16)
        da3 = (dact * sl).astype(BF16)
        da1_ref[...] = da1
        da3_ref[...] = da3
        dn[...] += _dot_nt(da1, w1_ref[...]) + _dot_nt(da3, w3_ref[...])

        @pl.when(j == n_f - 1)
        def _():
            dx, dg = _rms_bwd(dn[...], h_ref[...], g_ref[...])
            dg_ref[...] += dg
            dhi_ref[...] = dho_ref[...] + dx

    tok = lambda i, j: (i, 0)
    return pl.pallas_call(
        body, name=name, grid=(T // tm, n_f),
        in_specs=[pl.BlockSpec((tm, D_MODEL), tok), pl.BlockSpec((1, D_MODEL), lambda i, j: (0, 0)),
                  pl.BlockSpec((tm, D_MODEL), tok), pl.BlockSpec((tm, tf), lambda i, j: (i, j)),
                  pl.BlockSpec((tm, tf), lambda i, j: (i, j)), pl.BlockSpec((D_MODEL, tf), lambda i, j: (0, j)),
                  pl.BlockSpec((D_MODEL, tf), lambda i, j: (0, j)), pl.BlockSpec((tf, D_MODEL), lambda i, j: (j, 0))],
        out_specs=[pl.BlockSpec((tm, D_MODEL), tok), pl.BlockSpec((tm, tf), lambda i, j: (i, j)),
                   pl.BlockSpec((tm, tf), lambda i, j: (i, j)), pl.BlockSpec((tm, tf), lambda i, j: (i, j)),
                   pl.BlockSpec((1, D_MODEL), lambda i, j: (0, 0))],
        out_shape=[_sds((T, D_MODEL), F32), _sds((T, D_FF), BF16), _sds((T, D_FF), BF16), _sds((T, D_FF), BF16),
                   _sds((1, D_MODEL), F32)],
        scratch_shapes=[pltpu.VMEM((tm, D_MODEL), BF16), pltpu.VMEM((tm, D_MODEL), F32)],
        compiler_params=_params(("arbitrary", "arbitrary")),
    )(h, g, dho, a1, a3, w1, w3, w2)


def mm_tn(a, b, name, scale=1.0):
    T, M = a.shape
    N = b.shape[1]
    bm = 512 if M % 512 == 0 else (1408 if M == 2816 else 256)
    bn = N if N in (2176, 1408) else (1408 if N == 2816 else min(N, 1024))
    tk = _tile(T, TOKEN_TILE)
    assert M % bm == 0 and N % bn == 0
    n_k = T // tk

    def body(a_ref, b_ref, o_ref):
        k = pl.program_id(2)

        @pl.when(k == 0)
        def _():
            o_ref[...] = jnp.zeros_like(o_ref)

        o_ref[...] += _dot_tn(a_ref[...].astype(BF16), b_ref[...].astype(BF16))

        if scale != 1.0:
            @pl.when(k == n_k - 1)
            def _():
                o_ref[...] = o_ref[...] * scale

    return pl.pallas_call(
        body, name=name, grid=(M // bm, N // bn, n_k),
        in_specs=[pl.BlockSpec((tk, bm), lambda m, n, k: (k, m)), pl.BlockSpec((tk, bn), lambda m, n, k: (k, n))],
        out_specs=pl.BlockSpec((bm, bn), lambda m, n, k: (m, n)),
        out_shape=_sds((M, N), F32),
        compiler_params=_params(("arbitrary", "arbitrary", "arbitrary")),
    )(a, b)


def mixin_fwd(h1, g, w_in_r, b_f_pad):
    T = h1.shape[0]
    tm = _tile(T, TOKEN_TILE)

    def body(h_ref, g_ref, w_ref, bf_ref, u_ref, qkv_ref, s_ref, fz_ref, c_ref, carry):
        @pl.when(pl.program_id(0) == 0)
        def _():
            carry[...] = jnp.zeros_like(carry)

        x = h_ref[...]
        u = (x * _rms_scale(x) * g_ref[...]).astype(BF16)
        u_ref[...] = u
        z = _dot(u, w_ref[...])
        qkv_ref[...] = z[:, :3 * ATTN_W].astype(BF16)
        s_ref[...] = z[:, 3 * ATTN_W:3 * ATTN_W + SSM_W]
        fz = z[:, 3 * ATTN_W + SSM_W:] + bf_ref[...]
        fz_ref[...] = fz
        lane = lax.broadcasted_iota(jnp.int32, fz.shape, 1)
        logf = jnp.where(lane < N_HEADS, jnp.minimum(fz, 0.0) - jnp.log(1.0 + jnp.exp(-jnp.abs(fz))), 0.0)
        row = lax.broadcasted_iota(jnp.int32, (tm, tm), 0)
        col = lax.broadcasted_iota(jnp.int32, (tm, tm), 1)
        tri = (col <= row).astype(F32)
        cs = jnp.dot(tri, logf, precision=HIGHEST, preferred_element_type=F32) + carry[0:1, :]
        c_ref[...] = cs
        carry[...] = jnp.broadcast_to(cs[tm - 1:tm, :], carry.shape)

    tok = lambda i: (i, 0)
    fix = lambda i: (0, 0)
    return pl.pallas_call(
        body, name="mixin_fwd", grid=(T // tm,),
        in_specs=[pl.BlockSpec((tm, D_MODEL), tok), pl.BlockSpec((1, D_MODEL), fix),
                  pl.BlockSpec((D_MODEL, Z_COLS), fix), pl.BlockSpec((1, 128), fix)],
        out_specs=[pl.BlockSpec((tm, D_MODEL), tok), pl.BlockSpec((tm, 3 * ATTN_W), tok), pl.BlockSpec((tm, SSM_W), tok),
                   pl.BlockSpec((tm, 128), tok), pl.BlockSpec((tm, 128), tok)],
        out_shape=[_sds((T, D_MODEL), BF16), _sds((T, 3 * ATTN_W), BF16), _sds((T, SSM_W), F32),
                   _sds((T, 128), F32), _sds((T, 128), F32)],
        scratch_shapes=[pltpu.VMEM((8, 128), F32)],
        compiler_params=_params(("arbitrary",)),
    )(h1, g, w_in_r, b_f_pad)


def mixin_bwd(dh2, h1, g, w_in_r, dq, dk, dv, ds, dc, fz):
    T = h1.shape[0]
    tm = _tile(T, TOKEN_TILE)
    n_t = T // tm

    def body(dh2_ref, h_ref, g_ref, w_ref, dq_ref, dk_ref, dv_ref, ds_ref, dc_ref, fz_ref,
             dh1_ref, dz_ref, dg_ref, dbf_ref, carry):
        @pl.when(pl.program_id(0) == 0)
        def _():
            carry[...] = jnp.zeros_like(carry)
            dg_ref[...] = jnp.zeros_like(dg_ref)
            dbf_ref[...] = jnp.zeros_like(dbf_ref)

        row = lax.broadcasted_iota(jnp.int32, (tm, tm), 0)
        col = lax.broadcasted_iota(jnp.int32, (tm, tm), 1)
        tri = (col >= row).astype(F32)
        dlogf = jnp.dot(tri, dc_ref[...], precision=HIGHEST, preferred_element_type=F32) + carry[0:1, :]
        carry[...] = jnp.broadcast_to(dlogf[0:1, :], carry.shape)
        dfz = dlogf * jax.nn.sigmoid(-fz_ref[...])
        dbf_ref[...] += jnp.sum(dfz, axis=0, keepdims=True)
        dz = jnp.concatenate([dq_ref[...], dk_ref[...], dv_ref[...], ds_ref[...], dfz], axis=1).astype(BF16)
        dz_ref[...] = dz
        du = _dot_nt(dz, w_ref[...])
        dx, dg = _rms_bwd(du, h_ref[...], g_ref[...])
        dg_ref[...] += dg
        dh1_ref[...] = dh2_ref[...] + dx

    tok = lambda i: (n_t - 1 - i, 0)
    fix = lambda i: (0, 0)
    return pl.pallas_call(
        body, name="mixin_bwd", grid=(n_t,),
        in_specs=[pl.BlockSpec((tm, D_MODEL), tok), pl.BlockSpec((tm, D_MODEL), tok), pl.BlockSpec((1, D_MODEL), fix),
                  pl.BlockSpec((D_MODEL, Z_COLS), fix), pl.BlockSpec((tm, ATTN_W), tok), pl.BlockSpec((tm, ATTN_W), tok),
                  pl.BlockSpec((tm, ATTN_W), tok), pl.BlockSpec((tm, SSM_W), tok), pl.BlockSpec((tm, 128), tok),
                  pl.BlockSpec((tm, 128), tok)],
        out_specs=[pl.BlockSpec((tm, D_MODEL), tok), pl.BlockSpec((tm, Z_COLS), tok), pl.BlockSpec((1, D_MODEL), fix),
                   pl.BlockSpec((1, 128), fix)],
        out_shape=[_sds((T, D_MODEL), F32), _sds((T, Z_COLS), BF16), _sds((1, D_MODEL), F32), _sds((1, 128), F32)],
        scratch_shapes=[pltpu.VMEM((8, 128), F32)],
        compiler_params=_params(("arbitrary",)),
    )(dh2, h1, g, w_in_r, dq, dk, dv, ds, dc, fz)


def attn_fwd(q, k, v_aug, ck_rows):
    H, T, hd = q.shape
    tq = _tile(T, ATTN_TILE)
    n = T // tq

    def body(q_ref, k_ref, v_ref, ck_ref, o_ref, lse_ref, m_sc, acc):
        qi = pl.program_id(1)
        qv = q_ref[0]
        m_sc[...] = jnp.full_like(m_sc, -jnp.inf)
        acc[...] = jnp.zeros_like(acc)

        def tile(j, masked):
            rows = pl.ds(pl.multiple_of(j * tq, tq), tq)
            s = _dot_nt(qv, k_ref[0, rows, :]) - ck_ref[0, pl.ds(j, 1), :]
            if masked:
                keep = lax.broadcasted_iota(jnp.int32, (tq, tq), 1) <= lax.broadcasted_iota(jnp.int32, (tq, tq), 0)
                s = jnp.where(keep, s, -1e30)
            m_old = m_sc[...]
            m_new = jnp.maximum(m_old, jnp.max(s, axis=1, keepdims=True))
            p = jnp.exp(s - m_new).astype(BF16)
            acc[...] = jnp.exp(m_old - m_new) * acc[...] + _dot(p, v_ref[0, rows, :])
            m_sc[...] = m_new

        def off_diagonal(j, carry):
            tile(j, False)
            return carry

        lax.fori_loop(0, qi, off_diagonal, 0)
        tile(qi, True)
        a = acc[...]
        total = a[:, hd:hd + 1]
        o_ref[0] = a[:, :hd] / total
        lse_ref[0] = m_sc[...] + jnp.log(total)

    qmap = lambda h, i: (h, i, 0)
    head = lambda h, i: (h, 0, 0)
    return pl.pallas_call(
        body, name="attn_fwd", grid=(H, n),
        in_specs=[pl.BlockSpec((1, tq, hd), qmap), pl.BlockSpec((1, T, hd), head), pl.BlockSpec((1, T, 2 * hd), head),
                  pl.BlockSpec((1, n, tq), head)],
        out_specs=[pl.BlockSpec((1, tq, hd), qmap), pl.BlockSpec((1, tq, 1), qmap)],
        out_shape=[_sds((H, T, hd), F32), _sds((H, T, 1), F32)],
        scratch_shapes=[pltpu.VMEM((tq, 1), F32), pltpu.VMEM((tq, 2 * hd), F32)],
        compiler_params=_params(("arbitrary", "arbitrary")),
    )(q, k, v_aug, ck_rows)


def attn_bwd(q, k, v, do, lse_rows, delta_rows, ck_col):
    H, T, hd = q.shape
    tq = _tile(T, ATTN_TILE)
    n = T // tq

    def body(q_ref, do_ref, lse_ref, dl_ref, k_ref, v_ref, ck_ref, dq_ref, dk_ref, dv_ref, dck_ref, dcq_ref):
        j = pl.program_id(1)

        @pl.when(j == 0)
        def _():
            dq_ref[...] = jnp.zeros_like(dq_ref)
            dcq_ref[...] = jnp.zeros_like(dcq_ref)

        dk_ref[...] = jnp.zeros_like(dk_ref)
        dv_ref[...] = jnp.zeros_like(dv_ref)
        dck_ref[...] = jnp.zeros_like(dck_ref)
        kv, vv, ck = k_ref[0], v_ref[0], ck_ref[0]

        def tile(i, masked):
            rows = pl.ds(pl.multiple_of(i * tq, tq), tq)
            qv, dov = q_ref[0, rows, :], do_ref[0, rows, :]
            pt = jnp.exp(_dot_nt(kv, qv) - ck - lse_ref[0, pl.ds(i, 1), :])
            if masked:
                keep = lax.broadcasted_iota(jnp.int32, (tq, tq), 0) <= lax.broadcasted_iota(jnp.int32, (tq, tq), 1)
                pt = jnp.where(keep, pt, 0.0)
            dv_ref[0] += _dot(pt.astype(BF16), dov)
            dst = pt * (_dot_nt(vv, dov) - dl_ref[0, pl.ds(i, 1), :])
            dsb = dst.astype(BF16)
            dk_ref[0] += _dot(dsb, qv)
            dq_ref[0, rows, :] += _dot_tn(dsb, kv)
            dck_ref[0] += -jnp.sum(dst, axis=1, keepdims=True)
            dcq_ref[0, pl.ds(i, 1), :] += jnp.sum(dst, axis=0, keepdims=True)

        def off_diagonal(i, carry):
            tile(i, False)
            return carry

        tile(j, True)
        lax.fori_loop(j + 1, n, off_diagonal, 0)

    head = lambda h, j: (h, 0, 0)
    kmap = lambda h, j: (h, j, 0)
    return pl.pallas_call(
        body, name="attn_bwd", grid=(H, n),
        in_specs=[pl.BlockSpec((1, T, hd), head), pl.BlockSpec((1, T, hd), head), pl.BlockSpec((1, n, tq), head),
                  pl.BlockSpec((1, n, tq), head), pl.BlockSpec((1, tq, hd), kmap), pl.BlockSpec((1, tq, hd), kmap),
                  pl.BlockSpec((1, tq, 1), kmap)],
        out_specs=[pl.BlockSpec((1, T, hd), head), pl.BlockSpec((1, tq, hd), kmap), pl.BlockSpec((1, tq, hd), kmap),
                   pl.BlockSpec((1, tq, 1), kmap), pl.BlockSpec((1, n, tq), head)],
        out_shape=[_sds((H, T, hd), F32), _sds((H, T, hd), F32), _sds((H, T, hd), F32), _sds((H, T, 1), F32),
                   _sds((H, n, tq), F32)],
        compiler_params=_params(("arbitrary", "arbitrary")),
    )(q, do, lse_rows, delta_rows, k, v, ck_col)


def _complex_step(a_r, a_i, cr, ci, br, bi):
    return a_r * cr - a_i * ci + br, a_r * ci + a_i * cr + bi


def ssm_fwd(s_perm, wb, cbd, a_r, a_i, al_r, al_i, dvec):
    T = s_perm.shape[0]
    chunk = T // 8
    ts = _tile(chunk, SCAN_STEPS)
    tr, n_s = ts * 8, chunk // ts
    W, LB = STATE_W, SCAN_LANES

    def body(s_ref, wb_ref, cbd_ref, ar_ref, ai_ref, alr_ref, ali_ref, dv_ref, y_ref, xs_ref, bu, carry):
        ph, i = pl.program_id(0), pl.program_id(1)

        @pl.when((ph == 0) & (i == 0))
        def _():
            carry[...] = jnp.zeros_like(carry)

        bu[...] = _dot(s_ref[...].astype(BF16), wb_ref[...])

        def scan(store):
            for lb in range(W // LB):
                lo = lb * LB
                re, im = slice(lo, lo + LB), slice(W + lo, W + lo + LB)
                ar = jnp.broadcast_to(ar_ref[:, re], (8, LB))
                ai = jnp.broadcast_to(ai_ref[:, re], (8, LB))

                def step(s, c):
                    rows = pl.ds(pl.multiple_of(s * 8, 8), 8)
                    nr, ni = _complex_step(ar, ai, c[0], c[1], bu[rows, re], bu[rows, im])
                    if store:
                        bu[rows, re] = nr
                        bu[rows, im] = ni
                    return nr, ni

                cr, ci = lax.fori_loop(0, ts, step, (carry[:, re], carry[:, im]), unroll=2)
                carry[:, re] = cr
                carry[:, im] = ci

        @pl.when(ph == 0)
        def _():
            scan(False)

            @pl.when(i == n_s - 1)
            def _():
                er, ei = carry[:, :W], carry[:, W:]
                alr = jnp.broadcast_to(alr_ref[...], (8, W))
                ali = jnp.broadcast_to(ali_ref[...], (8, W))
                first = lax.broadcasted_iota(jnp.int32, (8, W), 0) == 0
                sr, si = jnp.zeros((8, W), F32), jnp.zeros((8, W), F32)
                for _ in range(7):
                    vr, vi = _complex_step(alr, ali, sr, si, er, ei)
                    sr = jnp.where(first, 0.0, pltpu.roll(vr, 1, 0))
                    si = jnp.where(first, 0.0, pltpu.roll(vi, 1, 0))
                carry[:, :W] = sr
                carry[:, W:] = si

        @pl.when(ph == 1)
        def _():
            scan(True)
            xb = bu[...].astype(BF16)
            xs_ref[...] = xb
            y_ref[...] = _dot(xb, cbd_ref[...]) + s_ref[...] * dv_ref[...]

    fix = lambda p, i: (0, 0)
    return pl.pallas_call(
        body, name="ssm_fwd", grid=(2, n_s),
        in_specs=[pl.BlockSpec((tr, SSM_W), lambda p, i: (i, 0)), pl.BlockSpec((SSM_W, 2 * W), fix),
                  pl.BlockSpec((2 * W, SSM_W), fix), pl.BlockSpec((1, W), fix), pl.BlockSpec((1, W), fix),
                  pl.BlockSpec((1, W), fix), pl.BlockSpec((1, W), fix), pl.BlockSpec((1, SSM_W), fix)],
        out_specs=[pl.BlockSpec((tr, SSM_W), lambda p, i: (i * p, 0)), pl.BlockSpec((tr, 2 * W), lambda p, i: (i * p, 0))],
        out_shape=[_sds((T, SSM_W), F32), _sds((T, 2 * W), BF16)],
        scratch_shapes=[pltpu.VMEM((tr, 2 * W), F32), pltpu.VMEM((8, 2 * W), F32)],
        compiler_params=_params(("arbitrary", "arbitrary")),
    )(s_perm, wb, cbd, a_r, a_i, al_r, al_i, dvec)


def ssm_bwd(dy_perm, s_perm, xs, cbd_t, wb_t, a_r, a_i, al_r, al_i, dvec):
    T = s_perm.shape[0]
    chunk = T // 8
    ts = _tile(chunk, SCAN_STEPS)
    tr, n_s = ts * 8, chunk // ts
    W, LB = STATE_W, SCAN_LANES

    def body(dy_ref, s_ref, xs_ref, cbt_ref, wbt_ref, ar_ref, ai_ref, alr_ref, ali_ref, dv_ref,
             du_ref, gs_ref, da_ref, dd_ref, gd, x32, carry):
        ph, i = pl.program_id(0), pl.program_id(1)

        @pl.when((ph == 0) & (i == 0))
        def _():
            carry[...] = jnp.zeros_like(carry)
            da_ref[...] = jnp.zeros_like(da_ref)
            dd_ref[...] = jnp.zeros_like(dd_ref)

        gd[...] = _dot(dy_ref[...].astype(BF16), cbt_ref[...])

        def scan(store):
            for lb in range(W // LB):
                lo = lb * LB
                re, im = slice(lo, lo + LB), slice(W + lo, W + lo + LB)
                ar = jnp.broadcast_to(ar_ref[:, re], (8, LB))
                nai = -jnp.broadcast_to(ai_ref[:, re], (8, LB))

                def step(k, c):
                    rows = pl.ds(pl.multiple_of((ts - 1 - k) * 8, 8), 8)
                    cr, ci = c[0], c[1]
                    nr, ni = _complex_step(ar, nai, cr, ci, gd[rows, re], gd[rows, im])
                    if store:
                        xr, xi = x32[rows, re], x32[rows, im]
                        gd[rows, re] = nr
                        gd[rows, im] = ni
                        return nr, ni, c[2] + cr * xr + ci * xi, c[3] + ci * xr - cr * xi
                    return nr, ni

                init = (carry[:, re], carry[:, im])
                if store:
                    init = init + (da_ref[:, re], da_ref[:, im])
                out = lax.fori_loop(0, ts, step, init, unroll=2)
                carry[:, re] = out[0]
                carry[:, im] = out[1]
                if store:
                    da_ref[:, re] = out[2]
                    da_ref[:, im] = out[3]

        @pl.when(ph == 0)
        def _():
            scan(False)

            @pl.when(i == n_s - 1)
            def _():
                er, ei = carry[:, :W], carry[:, W:]
                alr = jnp.broadcast_to(alr_ref[...], (8, W))
                nali = -jnp.broadcast_to(ali_ref[...], (8, W))
                last = lax.broadcasted_iota(jnp.int32, (8, W), 0) == 7
                rr, ri = jnp.zeros((8, W), F32), jnp.zeros((8, W), F32)
                for _ in range(7):
                    vr, vi = _complex_step(alr, nali, rr, ri, er, ei)
                    rr = jnp.where(last, 0.0, pltpu.roll(vr, 7, 0))
                    ri = jnp.where(last, 0.0, pltpu.roll(vi, 7, 0))
                carry[:, :W] = rr
                carry[:, W:] = ri

        @pl.when(ph == 1)
        def _():
            x32[...] = xs_ref[...].astype(F32)
            scan(True)
            gb = gd[...].astype(BF16)
            gs_ref[...] = gb
            dy = dy_ref[...]
            du_ref[...] = _dot(gb, wbt_ref[...]) + dy * dv_ref[...]
            dd_ref[...] += jnp.sum(dy * s_ref[...], axis=0, keepdims=True)

    fix = lambda p, i: (0, 0)
    rev = lambda p, i: (n_s - 1 - i, 0)
    rev_out = lambda p, i: (n_s - 1 - i * p, 0)
    return pl.pallas_call(
        body, name="ssm_bwd", grid=(2, n_s),
        in_specs=[pl.BlockSpec((tr, SSM_W), rev), pl.BlockSpec((tr, SSM_W), rev), pl.BlockSpec((tr, 2 * W), rev),
                  pl.BlockSpec((SSM_W, 2 * W), fix), pl.BlockSpec((2 * W, SSM_W), fix), pl.BlockSpec((1, W), fix),
                  pl.BlockSpec((1, W), fix), pl.BlockSpec((1, W), fix), pl.BlockSpec((1, W), fix),
                  pl.BlockSpec((1, SSM_W), fix)],
        out_specs=[pl.BlockSpec((tr, SSM_W), rev_out), pl.BlockSpec((tr, 2 * W), rev_out),
                   pl.BlockSpec((8, 2 * W), fix), pl.BlockSpec((1, SSM_W), fix)],
        out_shape=[_sds((T, SSM_W), F32), _sds((T, 2 * W), BF16), _sds((8, 2 * W), F32), _sds((1, SSM_W), F32)],
        scratch_shapes=[pltpu.VMEM((tr, 2 * W), F32), pltpu.VMEM((tr, 2 * W), F32), pltpu.VMEM((8, 2 * W), F32)],
        compiler_params=_params(("arbitrary", "arbitrary")),
    )(dy_perm, s_perm, xs, cbd_t, wb_t, a_r, a_i, al_r, al_i, dvec)


def mixout_fwd(h1, attn, ypre, g_a, g_s, w_glu, b_glu, w_out):
    T = h1.shape[0]
    tm = _tile(T, TOKEN_TILE)

    def body(h_ref, at_ref, yp_ref, ga_ref, gs_ref, wg_ref, bg_ref, wo_ref, h2_ref, mixed_ref):
        yg, _ = _gelu_parts(yp_ref[...])
        gl = yg * jax.nn.sigmoid(_dot(yg.astype(BF16), wg_ref[...]) + bg_ref[...])
        at = at_ref[...]
        mixed = jnp.concatenate([at * _rms_scale(at) * ga_ref[...], gl * _rms_scale(gl) * gs_ref[...]], axis=1)
        mixed = mixed.astype(BF16)
        mixed_ref[...] = mixed
        h2_ref[...] = h_ref[...] + _dot(mixed, wo_ref[...])

    tok = lambda i: (i, 0)
    fix = lambda i: (0, 0)
    return pl.pallas_call(
        body, name="mixout_fwd", grid=(T // tm,),
        in_specs=[pl.BlockSpec((tm, D_MODEL), tok), pl.BlockSpec((tm, ATTN_W), tok), pl.BlockSpec((tm, SSM_W), tok),
                  pl.BlockSpec((1, ATTN_W), fix), pl.BlockSpec((1, SSM_W), fix), pl.BlockSpec((SSM_W, SSM_W), fix),
                  pl.BlockSpec((1, SSM_W), fix), pl.BlockSpec((D_MODEL, D_MODEL), fix)],
        out_specs=[pl.BlockSpec((tm, D_MODEL), tok), pl.BlockSpec((tm, D_MODEL), tok)],
        out_shape=[_sds((T, D_MODEL), F32), _sds((T, D_MODEL), BF16)],
        compiler_params=_params(("arbitrary",)),
    )(h1, attn, ypre, g_a, g_s, w_glu, b_glu, w_out)


def mixout_bwd(dh2, attn, ypre, g_a, g_s, w_glu, b_glu, w_out, seg):
    T = dh2.shape[0]
    tm = _tile(T, TOKEN_TILE)

    def body(dh_ref, at_ref, yp_ref, ga_ref, gs_ref, wg_ref, bg_ref, wo_ref, seg_ref,
             dat_ref, dyp_ref, dpre_ref, yg_ref, dl_ref, dga_ref, dgs_ref, dbg_ref):
        @pl.when(pl.program_id(0) == 0)
        def _():
            dga_ref[...] = jnp.zeros_like(dga_ref)
            dgs_ref[...] = jnp.zeros_like(dgs_ref)
            dbg_ref[...] = jnp.zeros_like(dbg_ref)

        dmix = _dot_nt(dh_ref[...].astype(BF16), wo_ref[...])
        at = at_ref[...]
        dat, dga = _rms_bwd(dmix[:, :ATTN_W], at, ga_ref[...])
        dga_ref[...] += dga
        dat_ref[...] = dat
        dl_ref[...] = jnp.dot(dat * at, seg_ref[...], precision=HIGHEST, preferred_element_type=F32)
        yp = yp_ref[...]
        yg, t = _gelu_parts(yp)
        ygb = yg.astype(BF16)
        yg_ref[...] = ygb
        sg = jax.nn.sigmoid(_dot(ygb, wg_ref[...]) + bg_ref[...])
        dgl, dgs = _rms_bwd(dmix[:, ATTN_W:], yg * sg, gs_ref[...])
        dgs_ref[...] += dgs
        dpre = dgl * yg * sg * (1.0 - sg)
        dbg_ref[...] += jnp.sum(dpre, axis=0, keepdims=True)
        dpb = dpre.astype(BF16)
        dpre_ref[...] = dpb
        dyg = dgl * sg + _dot_nt(dpb, wg_ref[...])
        dyp_ref[...] = dyg * _gelu_grad(yp, t)

    tok = lambda i: (i, 0)
    fix = lambda i: (0, 0)
    return pl.pallas_call(
        body, name="mixout_bwd", grid=(T // tm,),
        in_specs=[pl.BlockSpec((tm, D_MODEL), tok), pl.BlockSpec((tm, ATTN_W), tok), pl.BlockSpec((tm, SSM_W), tok),
                  pl.BlockSpec((1, ATTN_W), fix), pl.BlockSpec((1, SSM_W), fix), pl.BlockSpec((SSM_W, SSM_W), fix),
                  pl.BlockSpec((1, SSM_W), fix), pl.BlockSpec((D_MODEL, D_MODEL), fix), pl.BlockSpec((ATTN_W, 128), fix)],
        out_specs=[pl.BlockSpec((tm, ATTN_W), tok), pl.BlockSpec((tm, SSM_W), tok), pl.BlockSpec((tm, SSM_W), tok),
                   pl.BlockSpec((tm, SSM_W), tok), pl.BlockSpec((tm, 128), tok), pl.BlockSpec((1, ATTN_W), fix),
                   pl.BlockSpec((1, SSM_W), fix), pl.BlockSpec((1, SSM_W), fix)],
        out_shape=[_sds((T, ATTN_W), F32), _sds((T, SSM_W), F32), _sds((T, SSM_W), BF16), _sds((T, SSM_W), BF16),
                   _sds((T, 128), F32), _sds((1, ATTN_W), F32), _sds((1, SSM_W), F32), _sds((1, SSM_W), F32)],
        compiler_params=_params(("arbitrary",)),
    )(dh2, attn, ypre, g_a, g_s, w_glu, b_glu, w_out, seg)


def head_fwd_bwd(h3, p, target, g_ple, g_final, w_gate, w_proj):
    T = h3.shape[0]
    tm = _tile(T, TOKEN_TILE)
    pd = p.shape[1]

    def body(h_ref, p_ref, tg_ref, gp_ref, gf_ref, wg_ref, wp_ref,
             dh_ref, n3_ref, dz_ref, dpp_ref, loss_ref, dgp_ref, dgf_ref):
        @pl.when(pl.program_id(0) == 0)
        def _():
            loss_ref[...] = jnp.zeros_like(loss_ref)
            dgp_ref[...] = jnp.zeros_like(dgp_ref)
            dgf_ref[...] = jnp.zeros_like(dgf_ref)

        x = h_ref[...]
        gp, gf = gp_ref[...], gf_ref[...]
        n3 = (x * _rms_scale(x) * gp).astype(BF16)
        n3_ref[...] = n3
        gate = jax.nn.sigmoid(_dot(n3, wg_ref[...]))
        pp = _dot(p_ref[...].astype(BF16), wp_ref[...])
        h4 = x + gate * pp
        y = h4 * _rms_scale(h4) * gf
        e = y - tg_ref[...]
        tile_loss = jnp.sum(jnp.sum(e * e, axis=1, keepdims=True), axis=0, keepdims=True) * (0.5 / D_MODEL)
        loss_ref[...] += jnp.broadcast_to(tile_loss, loss_ref.shape)
        dh4, dgf = _rms_bwd(e * (1.0 / D_MODEL), h4, gf)
        dgf_ref[...] += dgf
        dzg = dh4 * pp * gate * (1.0 - gate)
        dzb = dzg.astype(BF16)
        dz_ref[...] = dzb
        dpp_ref[...] = (dh4 * gate).astype(BF16)
        dx, dgp = _rms_bwd(_dot_nt(dzb, wg_ref[...]), x, gp)
        dgp_ref[...] += dgp
        dh_ref[...] = dh4 + dx

    tok = lambda i: (i, 0)
    fix = lambda i: (0, 0)
    return pl.pallas_call(
        body, name="head_fwd_bwd", grid=(T // tm,),
        in_specs=[pl.BlockSpec((tm, D_MODEL), tok), pl.BlockSpec((tm, pd), tok), pl.BlockSpec((tm, D_MODEL), tok),
                  pl.BlockSpec((1, D_MODEL), fix), pl.BlockSpec((1, D_MODEL), fix), pl.BlockSpec((D_MODEL, D_MODEL), fix),
                  pl.BlockSpec((pd, D_MODEL), fix)],
        out_specs=[pl.BlockSpec((tm, D_MODEL), tok), pl.BlockSpec((tm, D_MODEL), tok), pl.BlockSpec((tm, D_MODEL), tok),
                   pl.BlockSpec((tm, D_MODEL), tok), pl.BlockSpec((8, 128), fix), pl.BlockSpec((1, D_MODEL), fix),
                   pl.BlockSpec((1, D_MODEL), fix)],
        out_shape=[_sds((T, D_MODEL), F32), _sds((T, D_MODEL), BF16), _sds((T, D_MODEL), BF16), _sds((T, D_MODEL), BF16),
                   _sds((8, 128), F32), _sds((1, D_MODEL), F32), _sds((1, D_MODEL), F32)],
        compiler_params=_params(("arbitrary",)),
    )(h3, p, target, g_ple, g_final, w_gate, w_proj)


def _row_tile(rows, cols, n_arrays):
    cap = VMEM_LIMIT // 3 // (2 * n_arrays * cols * 4)
    best = 8
    for t in range(8, min(rows, cap) + 1, 8):
        if rows % t == 0:
            best = t
    return best


def _adamw_math(w, g, m, v):
    nm = ADAM_B1 * m + (1.0 - ADAM_B1) * g
    nv = ADAM_B2 * v + (1.0 - ADAM_B2) * (g * g)
    c1 = 1.0 - ADAM_B1 ** ADAM_STEP
    c2 = 1.0 - ADAM_B2 ** ADAM_STEP
    return -ADAM_LR * ((nm / c1) / (jnp.sqrt(nv / c2) + ADAM_EPS) + ADAM_WD * w), nm, nv


def adamw(w, g, m, v, name):
    R, C = w.shape
    tr = _row_tile(R, C, 7)

    def body(w_ref, g_ref, m_ref, v_ref, d_ref, nm_ref, nv_ref):
        d_ref[...], nm_ref[...], nv_ref[...] = _adamw_math(w_ref[...], g_ref[...], m_ref[...], v_ref[...])

    spec = pl.BlockSpec((tr, C), lambda i: (i, 0))
    return pl.pallas_call(
        body, name=name, grid=(R // tr,), in_specs=[spec] * 4, out_specs=[spec] * 3,
        out_shape=[_sds((R, C), F32)] * 3, compiler_params=_params(("arbitrary",)),
    )(w, g, m, v)


def adamw_halves(w, g_mine, g_other, m, v, core, name):
    R, C = w.shape
    rh = R // 2
    tr = _row_tile(rh, C, 9)
    nb = rh // tr

    def body(c_ref, w_ref, gm_ref, go_ref, m_ref, v_ref, g_ref, d_ref, nm_ref, nv_ref):
        mine = (pl.program_id(0) // nb) == c_ref[0]
        g = jnp.where(mine, gm_ref[...], go_ref[...])
        g_ref[...] = g
        d_ref[...], nm_ref[...], nv_ref[...] = _adamw_math(w_ref[...], g, m_ref[...], v_ref[...])

    full = pl.BlockSpec((tr, C), lambda i, c: (i, 0))
    half = pl.BlockSpec((tr, C), lambda i, c: (i % nb, 0))
    return pl.pallas_call(
        body, name=name,
        grid_spec=pltpu.PrefetchScalarGridSpec(num_scalar_prefetch=1, grid=(R // tr,), in_specs=[full, half, half, full, full],
                                               out_specs=[full] * 4),
        out_shape=[_sds((R, C), F32)] * 4, compiler_params=_params(("arbitrary",)),
    )(core, w, g_mine, g_other, m, v)


def pair_sum(g, theirs, core):
    n, R, C = g.shape
    rh = R // 2
    tr = _row_tile(rh, C, 3)
    nb = rh // tr

    def body(c_ref, g_ref, t_ref, o_ref):
        o_ref[...] = g_ref[...] + t_ref[...]

    here = pl.BlockSpec((1, tr, C), lambda j, i, c: (j, i, 0))
    return pl.pallas_call(
        body, name="pair_sum",
        grid_spec=pltpu.PrefetchScalarGridSpec(
            num_scalar_prefetch=1, grid=(n, nb),
            in_specs=[pl.BlockSpec((1, tr, C), lambda j, i, c: (j, c[0] * nb + i, 0)), here], out_specs=here),
        out_shape=_sds((n, rh, C), F32), compiler_params=_params(("arbitrary", "arbitrary")),
    )(core, g, theirs)


def chip_sum(pair, got, chip):
    _, R, C = pair.shape
    tr = _row_tile(R, C, 5)

    def body(c_ref, p_ref, g0_ref, g1_ref, g2_ref, o_ref):
        o_ref[...] = ((p_ref[0] + g0_ref[0]) + g1_ref[0]) + g2_ref[0]

    slot = lambda k: pl.BlockSpec((1, tr, C), lambda i, c: (k, i, 0))
    return pl.pallas_call(
        body, name="chip_sum",
        grid_spec=pltpu.PrefetchScalarGridSpec(
            num_scalar_prefetch=1, grid=(R // tr,),
            in_specs=[pl.BlockSpec((1, tr, C), lambda i, c: (c[0], i, 0)), slot(0), slot(1), slot(2)],
            out_specs=pl.BlockSpec((tr, C), lambda i, c: (i, 0))),
        out_shape=_sds((R, C), F32), compiler_params=_params(("arbitrary",)),
    )(chip, pair, got, got, got)


_HBM = pl.BlockSpec(memory_space=pltpu.HBM)


def _place():
    x, y, c = lax.axis_index("x"), lax.axis_index("y"), lax.axis_index("c")
    return x, y, c, [(1 - x, y), (x, 1 - y), (1 - x, 1 - y)]


def _spans(rows, n):
    assert rows % PACK_ALIGN == 0
    tiles = rows // PACK_ALIGN
    n = min(n, tiles)
    cuts = [tiles * q // n for q in range(n + 1)]
    return [(cuts[q] * PACK_ALIGN, (cuts[q + 1] - cuts[q]) * PACK_ALIGN) for q in range(n)]


def _remote(src, dst, send_sem, recv_sem, to):
    return pltpu.make_async_remote_copy(src_ref=src, dst_ref=dst, send_sem=send_sem, recv_sem=recv_sem,
                                        device_id=to, device_id_type=MESH)


def allgather_shards(wp):
    R, C = wp.shape
    rh = R // 2
    spans = _spans(rh, COPY_CHUNKS)
    n_sp = len(spans)
    local_spans = _spans(R, 2 * COPY_CHUNKS)

    def body(w_ref, out_ref, send_sems, recv_sems, pass_send, pass_recv, local_sems):
        x, y, c, chips = _place()
        me = 2 * x + y
        local = []
        for q, (o, n) in enumerate(local_spans):
            cp = pltpu.make_async_copy(w_ref.at[pl.ds(o, n), :], out_ref.at[me, pl.ds(o, n), :], local_sems.at[q])
            cp.start()
            local.append(cp)
        sends = []
        for k, (cx, cy) in enumerate(chips):
            for q, (o, n) in enumerate(spans):
                rows = pl.ds(c * rh + o, n)
                cp = _remote(w_ref.at[rows, :], out_ref.at[me, rows, :], send_sems.at[k * n_sp + q],
                             recv_sems.at[k * n_sp + q], (cx, cy, c))
                cp.start()
                sends.append(cp)
        for k, (cx, cy) in enumerate(chips):
            for q, (o, n) in enumerate(spans):
                blk = out_ref.at[2 * cx + cy, pl.ds(c * rh + o, n), :]
                _remote(blk, blk, send_sems.at[k * n_sp + q], recv_sems.at[k * n_sp + q], (cx, cy, c)).wait_recv()
                cp = _remote(blk, blk, pass_send.at[k * n_sp + q], pass_recv.at[k * n_sp + q], (x, y, 1 - c))
                cp.start()
                sends.append(cp)
        for k, (cx, cy) in enumerate(chips):
            for q, (o, n) in enumerate(spans):
                blk = out_ref.at[2 * cx + cy, pl.ds((1 - c) * rh + o, n), :]
                _remote(blk, blk, pass_send.at[k * n_sp + q], pass_recv.at[k * n_sp + q], (x, y, 1 - c)).wait_recv()
        for cp in sends:
            cp.wait_send()
        for cp in local:
            cp.wait()

    sems = pltpu.SemaphoreType.DMA((3 * n_sp,))
    return pl.pallas_call(
        body, name="allgather_shards", in_specs=[_HBM], out_specs=_HBM, out_shape=_sds((4, R, C), wp.dtype),
        scratch_shapes=[sems, sems, sems, sems, pltpu.SemaphoreType.DMA((len(local_spans),))],
    )(wp)


def sibling_split(g):
    n_sl, R, C = g.shape
    rh = R // 2
    spans = _spans(rh, COPY_CHUNKS)
    n_sp = len(spans)

    def body(g_ref, got_ref, send_sems, recv_sems):
        x, y, c, _ = _place()
        copies = []
        for j in range(n_sl):
            for q, (o, n) in enumerate(spans):
                cp = _remote(g_ref.at[j, pl.ds((1 - c) * rh + o, n), :], got_ref.at[j, pl.ds(o, n), :],
                             send_sems.at[j * n_sp + q], recv_sems.at[j * n_sp + q], (x, y, 1 - c))
                cp.start()
                copies.append(cp)
        for cp in copies:
            cp.wait()

    sems = pltpu.SemaphoreType.DMA((n_sl * n_sp,))
    return pl.pallas_call(
        body, name="sibling_split", in_specs=[_HBM], out_specs=_HBM, out_shape=_sds((n_sl, rh, C), g.dtype),
        scratch_shapes=[sems, sems],
    )(g)


def chip_exchange(p):
    _, R, C = p.shape
    spans = _spans(R, COPY_CHUNKS)
    n_sp = len(spans)

    def body(p_ref, buf_ref, send_sems, recv_sems):
        x, y, c, chips = _place()
        sends = []
        for k, (cx, cy) in enumerate(chips):
            for q, (o, n) in enumerate(spans):
                cp = _remote(p_ref.at[2 * cx + cy, pl.ds(o, n), :], buf_ref.at[k, pl.ds(o, n), :],
                             send_sems.at[k * n_sp + q], recv_sems.at[k * n_sp + q], (cx, cy, c))
                cp.start()
                sends.append(cp)
        for cp in sends:
            cp.wait()

    sems = pltpu.SemaphoreType.DMA((3 * n_sp,))
    return pl.pallas_call(
        body, name="chip_exchange", in_specs=[_HBM], out_specs=_HBM, out_shape=_sds((3, R, C), p.dtype),
        scratch_shapes=[sems, sems],
    )(p)


def sibling_swap(half):
    R, C = half.shape
    spans = _spans(R, COPY_CHUNKS)

    def body(h_ref, got_ref, send_sems, recv_sems):
        x, y, c, _ = _place()
        copies = []
        for q, (o, n) in enumerate(spans):
            cp = _remote(h_ref.at[pl.ds(o, n), :], got_ref.at[pl.ds(o, n), :], send_sems.at[q], recv_sems.at[q], (x, y, 1 - c))
            cp.start()
            copies.append(cp)
        for cp in copies:
            cp.wait()

    sems = pltpu.SemaphoreType.DMA((len(spans),))
    return pl.pallas_call(
        body, name="sibling_swap", in_specs=[_HBM], out_specs=_HBM, out_shape=_sds((R, C), half.dtype),
        scratch_shapes=[sems, sems],
    )(half)


def allreduce_small(v):
    R, C = v.shape

    def body(v_ref, out_ref, buf, send_sems, recv_sems):
        x, y, c, _ = _place()
        me = 4 * x + 2 * y + c
        buf[me] = v_ref[...]
        flips = [((k >> 2) & 1, (k >> 1) & 1, k & 1) for k in range(1, 8)]
        sends = []
        for k, (fx, fy, fc) in enumerate(flips):
            to = (1 - x if fx else x, 1 - y if fy else y, 1 - c if fc else c)
            cp = _remote(v_ref, buf.at[me], send_sems.at[k], recv_sems.at[k], to)
            cp.start()
            sends.append(cp)
        for k, (fx, fy, fc) in enumerate(flips):
            px, py, pc = (1 - x if fx else x, 1 - y if fy else y, 1 - c if fc else c)
            blk = buf.at[4 * px + 2 * py + pc]
            _remote(blk, blk, send_sems.at[k], recv_sems.at[k], (px, py, pc)).wait_recv()
        for cp in sends:
            cp.wait_send()
        acc = buf[0]
        for s in range(1, 8):
            acc = acc + buf[s]
        out_ref[...] = acc

    vm = pl.BlockSpec(memory_space=pltpu.VMEM)
    return pl.pallas_call(
        body, name="allreduce_small", in_specs=[vm], out_specs=vm, out_shape=_sds((R, C), F32),
        scratch_shapes=[pltpu.VMEM((8, R, C), F32), pltpu.SemaphoreType.DMA((7,)), pltpu.SemaphoreType.DMA((7,))],
        compiler_params=pltpu.CompilerParams(vmem_limit_bytes=VMEM_LIMIT),
    )(v)


def _rows_of(shape):
    return shape[0] * shape[1] // PACK_COLS


def _slot_rows(shape):
    return -(-_rows_of(shape) // PACK_ALIGN) * PACK_ALIGN


def _pack_shards(shards, dtype):
    parts = []
    for name, shape, _ in BIG:
        part = shards[name].reshape(_rows_of(shape), PACK_COLS).astype(dtype)
        parts.append(jnp.pad(part, ((0, _slot_rows(shape) - part.shape[0]), (0, 0))))
    used = sum(p.shape[0] for p in parts)
    parts.append(jnp.zeros((PACK_ROWS - used, PACK_COLS), dtype))
    return jnp.concatenate(parts, axis=0)


def _unpack_gathered(ag):
    out, off = {}, 0
    for name, shape, axis in BIG:
        r = _rows_of(shape)
        piece = ag[:, off:off + r, :].reshape((4,) + shape)
        off += _slot_rows(shape)
        if axis == 0:
            out[name] = piece.reshape(4 * shape[0], shape[1])
        else:
            out[name] = piece.transpose(1, 0, 2).reshape(shape[0], 4 * shape[1])
    return out


def _pack_full_grads(grads):
    parts = []
    for name, shape, axis in BIG:
        g = grads[name]
        if axis == 0:
            piece = g.reshape((4,) + shape)
        else:
            piece = g.reshape(shape[0], 4, shape[1]).transpose(1, 0, 2)
        piece = piece.reshape(4, _rows_of(shape), PACK_COLS)
        parts.append(jnp.pad(piece, ((0, 0), (0, _slot_rows(shape) - piece.shape[1]), (0, 0))))
    used = sum(p.shape[1] for p in parts)
    parts.append(jnp.zeros((4, PACK_ROWS - used, PACK_COLS), F32))
    return jnp.concatenate(parts, axis=1)


def _unpack_shards(packed):
    out, off = {}, 0
    for name, shape, _ in BIG:
        r = _rows_of(shape)
        out[name] = packed[off:off + r].reshape((1,) + shape)
        off += _slot_rows(shape)
    return out


def _pack_small(vals, extra=None):
    parts = [vals[name].reshape(-1) for name, _ in SMALL]
    used = sum(p.shape[0] for p in parts)
    if extra is not None:
        parts.append(extra.reshape(1))
        used += 1
    parts.append(jnp.zeros((SMALL_ROWS * 128 - used,), F32))
    return jnp.concatenate(parts).reshape(SMALL_ROWS, 128)


def _unpack_small(packed):
    flat = packed.reshape(-1)
    out, off = {}, 0
    for name, shape in SMALL:
        n = math.prod(shape)
        out[name] = flat[off:off + n].reshape(shape)
        off += n
    return out, flat[off]


def _to_heads(a):
    T = a.shape[0]
    return a.reshape(T, N_HEADS, HEAD_DIM).transpose(1, 0, 2)


def _from_heads(a):
    T = a.shape[1]
    return a.transpose(1, 0, 2).reshape(T, N_HEADS * HEAD_DIM)


def _permute_time(a):
    T, n = a.shape
    return a.reshape(8, T // 8, n).transpose(1, 0, 2).reshape(T, n)


def _unpermute_time(a):
    T, n = a.shape
    return a.reshape(T // 8, 8, n).transpose(1, 0, 2).reshape(T, n)


def _discretize(a_re, a_im, log_dt, b_re, b_im):
    dt = jnp.exp(log_dt)[:, None]
    decay = jnp.exp(dt * a_re)
    abar_r = decay * jnp.cos(dt * a_im)
    abar_i = decay * jnp.sin(dt * a_im)
    nr, ni = abar_r - 1.0, abar_i
    den = a_re * a_re + a_im * a_im
    fr = (nr * a_re + ni * a_im) / den
    fi = (ni * a_re - nr * a_im) / den
    bbar_r = fr[..., None] * b_re - fi[..., None] * b_im
    bbar_i = fr[..., None] * b_im + fi[..., None] * b_re
    return abar_r, abar_i, bbar_r, bbar_i


def _input_matrix(bbar_r, bbar_i):
    eye = jnp.eye(N_GROUPS, dtype=F32)
    blk = lambda b: jnp.einsum("ghp,gk->ghkp", b.transpose(0, 2, 1), eye).reshape(SSM_W, STATE_W)
    return jnp.concatenate([blk(bbar_r), blk(bbar_i)], axis=1)


def _output_matrix(c_re, c_im):
    eye = jnp.eye(N_GROUPS, dtype=F32)
    blk = lambda cm: jnp.einsum("ghp,gk->gpkh", cm, eye).reshape(STATE_W, SSM_W)
    return jnp.concatenate([blk(c_re), -blk(c_im)], axis=0)


def _state_power(ar, ai, n):
    steps = int(round(math.log2(n)))
    assert 1 << steps == n
    for _ in range(steps):
        ar, ai = ar * ar - ai * ai, 2.0 * ar * ai
    return ar, ai


def kernel(x, p, g_ffn1, w1_a, w3_a, w2_a, g_mix, w_in, b_f, a_re, a_im, log_dt, b_re, b_im, c_re, c_im, d_skip, w_glu, b_glu, g_attn_out, g_ssm_out, w_out, g_ffn2, w1_b, w3_b, w2_b, g_ple, w_ple_gate, w_ple_proj, g_final, loss_target, m_g_ffn1, m_w1_a, m_w3_a, m_w2_a, m_g_mix, m_w_in, m_b_f, m_a_re, m_a_im, m_log_dt, m_b_re, m_b_im, m_c_re, m_c_im, m_d_skip, m_w_glu, m_b_glu, m_g_attn_out, m_g_ssm_out, m_w_out, m_g_ffn2, m_w1_b, m_w3_b, m_w2_b, m_g_ple, m_w_ple_gate, m_w_ple_proj, m_g_final, v_g_ffn1, v_w1_a, v_w3_a, v_w2_a, v_g_mix, v_w_in, v_b_f, v_a_re, v_a_im, v_log_dt, v_b_re, v_b_im, v_c_re, v_c_im, v_d_skip, v_w_glu, v_b_glu, v_g_attn_out, v_g_ssm_out, v_w_out, v_g_ffn2, v_w1_b, v_w3_b, v_w2_b, v_g_ple, v_w_ple_gate, v_w_ple_proj, v_g_final):
    args = dict(locals())
    weights = {n: args[n] for n in WEIGHT_ORDER}
    moms = {n: args["m_" + n] for n in WEIGHT_ORDER}
    vars_ = {n: args["v_" + n] for n in WEIGHT_ORDER}
    T = x.shape[1]
    x2, p2, tgt = x[0], p[0, 0], loss_target[0]

    full = _unpack_gathered(allgather_shards(_pack_shards({n: weights[n][0] for n, _, _ in BIG}, BF16)))
    loss_part, dx, grads = _local_step(x2, p2, tgt, {n: weights[n] for n, _ in SMALL}, full)
    return _reduce_and_update(weights, moms, vars_, loss_part, dx, grads)


def _local_step(x2, p2, tgt, sm, full):
    T = x2.shape[0]
    (g_ffn1, g_mix, b_f, a_re, a_im, log_dt, b_re, b_im, c_re, c_im, d_skip, b_glu, g_attn_out, g_ssm_out, g_ffn2, g_ple,
     g_final) = (sm[n] for n, _ in SMALL)
    w_in_f = full["w_in"]
    w_in_r = jnp.concatenate([w_in_f[:, :ATTN_W] * QK_SCALE, w_in_f[:, ATTN_W:3 * ATTN_W], w_in_f[:, 3 * ATTN_W + N_HEADS:],
                              w_in_f[:, 3 * ATTN_W:3 * ATTN_W + N_HEADS], jnp.zeros((D_MODEL, 128 - N_HEADS), BF16)], axis=1)
    b_f_pad = jnp.pad(b_f, ((0, 0), (0, 128 - N_HEADS)))

    disc_in = (a_re[0], a_im[0], log_dt[0], b_re[0], b_im[0])
    (abar_r, abar_i, bbar_r, bbar_i), disc_vjp = jax.vjp(_discretize, *disc_in)
    wb = _input_matrix(bbar_r, bbar_i)
    cbd = _output_matrix(c_re[0], c_im[0])
    ar, ai = abar_r.reshape(1, STATE_W), abar_i.reshape(1, STATE_W)
    alr, ali = _state_power(ar, ai, T // 8)
    dvec = d_skip.reshape(1, SSM_W)
    wb16, cbd16 = wb.astype(BF16), cbd.astype(BF16)

    h1, a1a, a3a, n1 = ffn_fwd(x2, g_ffn1, full["w1_a"], full["w3_a"], full["w2_a"], "ffn_a_fwd")
    u, qkv, s_in, fz, cum = mixin_fwd(h1, g_mix, w_in_r, b_f_pad)
    qh, kh, vh = (_to_heads(qkv[:, i * ATTN_W:(i + 1) * ATTN_W]) for i in range(3))
    n_at = T // _tile(T, ATTN_TILE)
    c_heads = cum[:, :N_HEADS].T
    v_aug = jnp.concatenate([vh, jnp.ones((N_HEADS, T, 1), BF16), jnp.zeros((N_HEADS, T, HEAD_DIM - 1), BF16)], axis=2)
    o_heads, lse = attn_fwd(qh, kh, v_aug, c_heads.reshape(N_HEADS, n_at, T // n_at))
    attn = _from_heads(o_heads)
    s_perm = _permute_time(s_in)
    y_perm, xs = ssm_fwd(s_perm, wb16, cbd16, ar, ai, alr, ali, dvec)
    ypre = _unpermute_time(y_perm)
    h2, mixed = mixout_fwd(h1, attn, ypre, g_attn_out, g_ssm_out, full["w_glu"], b_glu, full["w_out"])
    h3, a1b, a3b, n2 = ffn_fwd(h2, g_ffn2, full["w1_b"], full["w3_b"], full["w2_b"], "ffn_b_fwd")

    dh3, n3, dzg, dpp, loss_part, dg_ple, dg_final = head_fwd_bwd(
        h3, p2, tgt, g_ple, g_final.reshape(1, D_MODEL), full["w_ple_gate"], full["w_ple_proj"])
    grads = {"g_ple": dg_ple, "g_final": dg_final.reshape(D_MODEL)}
    grads["w_ple_gate"] = mm_tn(n3, dzg, "dw_ple_gate")
    grads["w_ple_proj"] = mm_tn(p2, dpp, "dw_ple_proj")

    dh2, da1, da3, act, grads["g_ffn2"] = ffn_bwd(h2, g_ffn2, dh3, a1b, a3b, full["w1_b"], full["w3_b"], full["w2_b"], "ffn_b_bwd")
    grads["w1_b"] = mm_tn(n2, da1, "dw1_b")
    grads["w3_b"] = mm_tn(n2, da3, "dw3_b")
    grads["w2_b"] = mm_tn(act, dh3, "dw2_b", scale=0.5)

    seg = (jnp.arange(ATTN_W)[:, None] // HEAD_DIM == jnp.arange(128)[None, :]).astype(F32)
    dattn, dypre, dpre, yg, delta, grads["g_attn_out"], grads["g_ssm_out"], grads["b_glu"] = mixout_bwd(
        dh2, attn, ypre, g_attn_out, g_ssm_out, full["w_glu"], b_glu, full["w_out"], seg)
    grads["w_out"] = mm_tn(mixed, dh2, "dw_out")
    grads["w_glu"] = mm_tn(yg, dpre, "dw_glu")

    as_rows = lambda a: a.reshape(N_HEADS, n_at, T // n_at)
    dqh, dkh, dvh, dck, dcq = attn_bwd(qh, kh, vh, _to_heads(dattn).astype(BF16), as_rows(lse), as_rows(delta[:, :N_HEADS].T),
                                       c_heads.reshape(N_HEADS, T, 1))
    dc = jnp.pad((dck.reshape(N_HEADS, T) + dcq.reshape(N_HEADS, T)).T, ((0, 0), (0, 128 - N_HEADS)))

    dy_perm = _permute_time(dypre)
    du_perm, gs, d_a, dd = ssm_bwd(dy_perm, s_perm, xs, cbd16.T, wb16.T, ar, ai, alr, ali, dvec)
    ds_in = _unpermute_time(du_perm)
    d_wb = mm_tn(s_perm, gs, "dw_ssm_in")
    d_cbd = mm_tn(xs, dy_perm, "dw_ssm_out")
    diag_in = lambda m: jnp.einsum("ghgp->ghp", m.reshape(N_GROUPS, GROUP_CH, N_GROUPS, N_STATE)).transpose(0, 2, 1)
    diag_out = lambda m: jnp.einsum("gpgh->gph", m.reshape(N_GROUPS, N_STATE, N_GROUPS, GROUP_CH)).transpose(0, 2, 1)
    d_abar_r = jnp.sum(d_a[:, :STATE_W], axis=0).reshape(N_GROUPS, N_STATE)
    d_abar_i = jnp.sum(d_a[:, STATE_W:], axis=0).reshape(N_GROUPS, N_STATE)
    d_disc = disc_vjp((d_abar_r, d_abar_i, diag_in(d_wb[:, :STATE_W]), diag_in(d_wb[:, STATE_W:])))
    for name, val in zip(("a_re", "a_im", "log_dt", "b_re", "b_im"), d_disc):
        grads[name] = val[None]
    grads["c_re"] = diag_out(d_cbd[:STATE_W])[None]
    grads["c_im"] = -diag_out(d_cbd[STATE_W:])[None]
    grads["d_skip"] = dd.reshape(1, N_GROUPS, GROUP_CH)

    dh1, dz, grads["g_mix"], dbf = mixin_bwd(dh2, h1, g_mix, w_in_r, _from_heads(dqh), _from_heads(dkh), _from_heads(dvh),
                                             ds_in, dc, fz)
    grads["b_f"] = dbf[:, :N_HEADS]
    d_w_in_r = mm_tn(u, dz, "dw_in")
    grads["w_in"] = jnp.concatenate([d_w_in_r[:, :ATTN_W] * QK_SCALE, d_w_in_r[:, ATTN_W:3 * ATTN_W],
                                     d_w_in_r[:, 3 * ATTN_W + SSM_W:3 * ATTN_W + SSM_W + N_HEADS],
                                     d_w_in_r[:, 3 * ATTN_W:3 * ATTN_W + SSM_W]], axis=1)

    dx, da1, da3, act, grads["g_ffn1"] = ffn_bwd(x2, g_ffn1, dh1, a1a, a3a, full["w1_a"], full["w3_a"], full["w2_a"], "ffn_a_bwd")
    grads["w1_a"] = mm_tn(n1, da1, "dw1_a")
    grads["w3_a"] = mm_tn(n1, da3, "dw3_a")
    grads["w2_a"] = mm_tn(act, dh1, "dw2_a", scale=0.5)
    return loss_part, dx, grads


def _reduce_and_update(weights, moms, vars_, loss_part, dx, grads):
    core = lax.axis_index("c").astype(jnp.int32).reshape(1)
    chip = (2 * lax.axis_index("x") + lax.axis_index("y")).astype(jnp.int32).reshape(1)
    packed = _pack_full_grads(grads)
    pair = pair_sum(packed, sibling_split(packed), core)
    half = chip_sum(pair, chip_exchange(pair), chip)
    g_big, d_big, m_big, v_big = adamw_halves(
        _pack_shards({n: weights[n][0] for n, _, _ in BIG}, F32), half, sibling_swap(half),
        _pack_shards({n: moms[n][0] for n, _, _ in BIG}, F32), _pack_shards({n: vars_[n][0] for n, _, _ in BIG}, F32),
        core, "adamw_big")

    small = allreduce_small(_pack_small({n: grads[n] for n, _ in SMALL}, extra=loss_part[0, 0]))
    d_small, m_small, v_small = adamw(_pack_small(weights), small, _pack_small(moms), _pack_small(vars_), "adamw_small")

    g_out, loss = _unpack_small(small)
    g_out.update(_unpack_shards(g_big))
    outs = []
    for big, sm in ((d_big, d_small), (m_big, m_small), (v_big, v_small)):
        o, _ = _unpack_small(sm)
        o.update(_unpack_shards(big))
        outs.append(o)
    result = [loss, dx[None]] + [g_out[n] for n in WEIGHT_ORDER]
    for o in outs:
        result += [o[n] for n in WEIGHT_ORDER]
    return tuple(result)
```

```python
import functools
import math

import jax
import jax.numpy as jnp
from jax import lax
from jax.experimental import pallas as pl
from jax.experimental.pallas import tpu as pltpu

F32 = jnp.float32
BF16 = jnp.bfloat16

D_MODEL = 1024
D_FF = 2816
N_HEADS = 8
HEAD_DIM = 64
ATTN_W = 512
SSM_W = 512
N_GROUPS = 32
N_STATE = 64
GROUP_CH = 16
STATE_W = N_GROUPS * N_STATE
Z_COLS = 2176
QK_SCALE = 0.125
EPS = 1e-6

ADAM_LR = 0.001
ADAM_B1 = 0.9
ADAM_B2 = 0.999
ADAM_EPS = 1e-08
ADAM_WD = 0.01
ADAM_STEP = 10

TOKEN_TILE = 512
FFN_TOKEN_TILE = 256
FF_CHUNK = 1408
MM_K_TILE = 2048
ATTN_TILE = 512
SCAN_STEPS = 32
SCAN_LANES = 512
VMEM_LIMIT = 48 * 1024 * 1024
FFN_VMEM_LIMIT = 56 * 1024 * 1024
COPY_CHUNKS = 4

NT_DIMS = (((1,), (1,)), ((), ()))
TN_DIMS = (((0,), (0,)), ((), ()))
HIGHEST = lax.Precision.HIGHEST
MESH = pl.DeviceIdType.MESH

BIG = (
    ("w1_a", (1024, 704), 1), ("w3_a", (1024, 704), 1), ("w2_a", (704, 1024), 0),
    ("w_in", (1024, 514), 1), ("w_glu", (128, 512), 0), ("w_out", (256, 1024), 0),
    ("w1_b", (1024, 704), 1), ("w3_b", (1024, 704), 1), ("w2_b", (704, 1024), 0),
    ("w_ple_gate", (256, 1024), 0), ("w_ple_proj", (256, 256), 1),
)
PACK_COLS = 1024
PACK_ALIGN = 16
PACK_ROWS = 5408
SMALL = (
    ("g_ffn1", (1, 1024)), ("g_mix", (1, 1024)), ("b_f", (1, 8)), ("a_re", (1, 32, 64)), ("a_im", (1, 32, 64)),
    ("log_dt", (1, 32)), ("b_re", (1, 32, 64, 16)), ("b_im", (1, 32, 64, 16)), ("c_re", (1, 32, 16, 64)),
    ("c_im", (1, 32, 16, 64)), ("d_skip", (1, 32, 16)), ("b_glu", (1, 512)), ("g_attn_out", (1, 512)),
    ("g_ssm_out", (1, 512)), ("g_ffn2", (1, 1024)), ("g_ple", (1, 1024)), ("g_final", (1024,)),
)
SMALL_ROWS = 1120
WEIGHT_ORDER = ("g_ffn1", "w1_a", "w3_a", "w2_a", "g_mix", "w_in", "b_f", "a_re", "a_im", "log_dt", "b_re", "b_im",
                "c_re", "c_im", "d_skip", "w_glu", "b_glu", "g_attn_out", "g_ssm_out", "w_out", "g_ffn2", "w1_b",
                "w3_b", "w2_b", "g_ple", "w_ple_gate", "w_ple_proj", "g_final")


def _params(sem=None, vmem=VMEM_LIMIT):
    kw = dict(vmem_limit_bytes=vmem)
    if sem is not None:
        kw["dimension_semantics"] = sem
    return pltpu.CompilerParams(**kw)


def _sds(shape, dtype):
    return jax.ShapeDtypeStruct(shape, dtype)


def _tile(n, pref):
    t = min(n, pref)
    assert n % t == 0, (n, pref)
    return t


def _rms_scale(x):
    return lax.rsqrt(jnp.mean(x * x, axis=-1, keepdims=True) + EPS)


def _rms_bwd(dy, x, g):
    r = _rms_scale(x)
    xh = x * r
    dxh = dy * g
    dx = r * (dxh - xh * jnp.mean(dxh * xh, axis=-1, keepdims=True))
    return dx, jnp.sum(dy * xh, axis=0, keepdims=True)


def _dot(a, b):
    return jnp.dot(a, b, preferred_element_type=F32)


def _dot_nt(a, b):
    return lax.dot_general(a, b, NT_DIMS, preferred_element_type=F32)


def _dot_tn(a, b):
    return lax.dot_general(a, b, TN_DIMS, preferred_element_type=F32)


_GELU_C = math.sqrt(2.0 / math.pi)


def _gelu_parts(x):
    t = jnp.tanh(_GELU_C * (x + 0.044715 * x * x * x))
    return 0.5 * x * (1.0 + t), t


def _gelu_grad(x, t):
    return 0.5 * (1.0 + t) + 0.5 * x * (1.0 - t * t) * _GELU_C * (1.0 + 3.0 * 0.044715 * x * x)


def _resident(shape):
    return pl.BlockSpec(shape, lambda i: (0,) * len(shape), pipeline_mode=pl.Buffered(1))


def ffn_fwd(h, g, w1, w3, w2, name):
    T = h.shape[0]
    tm = _tile(T, FFN_TOKEN_TILE)

    def body(h_ref, g_ref, w1_ref, w3_ref, w2_ref, ho_ref, a1_ref, a3_ref, n_ref):
        x = h_ref[...]
        n = (x * _rms_scale(x) * g_ref[...]).astype(BF16)
        n_ref[...] = n
        out = x
        for lo in range(0, D_FF, FF_CHUNK):
            cols = slice(lo, lo + FF_CHUNK)
            a1 = _dot(n, w1_ref[:, cols])
            a3 = _dot(n, w3_ref[:, cols])
            a1_ref[:, cols] = a1.astype(BF16)
            a3_ref[:, cols] = a3.astype(BF16)
            act = (a1 * jax.nn.sigmoid(a1) * a3).astype(BF16)
            out = out + 0.5 * _dot(act, w2_ref[cols, :])
        ho_ref[...] = out

    tok = lambda i: (i, 0)
    return pl.pallas_call(
        body, name=name, grid=(T // tm,),
        in_specs=[pl.BlockSpec((tm, D_MODEL), tok), _resident((1, D_MODEL)), _resident((D_MODEL, D_FF)),
                  _resident((D_MODEL, D_FF)), _resident((D_FF, D_MODEL))],
        out_specs=[pl.BlockSpec((tm, D_MODEL), tok), pl.BlockSpec((tm, D_FF), tok), pl.BlockSpec((tm, D_FF), tok),
                   pl.BlockSpec((tm, D_MODEL), tok)],
        out_shape=[_sds((T, D_MODEL), F32), _sds((T, D_FF), BF16), _sds((T, D_FF), BF16), _sds((T, D_MODEL), BF16)],
        compiler_params=_params(("arbitrary",), FFN_VMEM_LIMIT),
    )(h, g, w1, w3, w2)


def ffn_bwd(h, g, dho, a1, a3, w1, w3, w2, name):
    T = h.shape[0]
    tm = _tile(T, FFN_TOKEN_TILE)

    def body(h_ref, g_ref, dho_ref, a1_ref, a3_ref, w1_ref, w3_ref, w2_ref, dhi_ref, da1_ref, da3_ref, act_ref, dg_ref):
        @pl.when(pl.program_id(0) == 0)
        def _():
            dg_ref[...] = jnp.zeros_like(dg_ref)

        dho = dho_ref[...]
        dhb = (0.5 * dho).astype(BF16)
        dn = None
        for lo in range(0, D_FF, FF_CHUNK):
            cols = slice(lo, lo + FF_CHUNK)
            a1v = a1_ref[:, cols].astype(F32)
            a3v = a3_ref[:, cols].astype(F32)
            s = jax.nn.sigmoid(a1v)
            sl = a1v * s
            dact = _dot_nt(dhb, w2_ref[cols, :])
            act_ref[:, cols] = (sl * a3v).astype(BF16)
            da1 = (dact * a3v * s * (1.0 + a1v * (1.0 - s))).astype(BF16)
            da3 = (dact * sl).astype(BF16)
            da1_ref[:, cols] = da1
            da3_ref[:, cols] = da3
            part = _dot_nt(da1, w1_ref[:, cols]) + _dot_nt(da3, w3_ref[:, cols])
            dn = part if dn is None else dn + part
        dx, dg = _rms_bwd(dn, h_ref[...], g_ref[...])
        dg_ref[...] += dg
        dhi_ref[...] = dho + dx

    tok = lambda i: (i, 0)
    return pl.pallas_call(
        body, name=name, grid=(T // tm,),
        in_specs=[pl.BlockSpec((tm, D_MODEL), tok), _resident((1, D_MODEL)), pl.BlockSpec((tm, D_MODEL), tok),
                  pl.BlockSpec((tm, D_FF), tok), pl.BlockSpec((tm, D_FF), tok), _resident((D_MODEL, D_FF)),
                  _resident((D_MODEL, D_FF)), _resident((D_FF, D_MODEL))],
        out_specs=[pl.BlockSpec((tm, D_MODEL), tok), pl.BlockSpec((tm, D_FF), tok), pl.BlockSpec((tm, D_FF), tok),
                   pl.BlockSpec((tm, D_FF), tok), pl.BlockSpec((1, D_MODEL), lambda i: (0, 0))],
        out_shape=[_sds((T, D_MODEL), F32), _sds((T, D_FF), BF16), _sds((T, D_FF), BF16), _sds((T, D_FF), BF16),
                   _sds((1, D_MODEL), F32)],
        compiler_params=_params(("arbitrary",), FFN_VMEM_LIMIT),
    )(h, g, dho, a1, a3, w1, w3, w2)


def mm_tn(a, b, name, scale=1.0):
    T, M = a.shape
    N = b.shape[1]
    bm = 512 if M % 512 == 0 else (1408 if M == 2816 else 256)
    bn = N if N in (2176, 1408) else (1408 if N == 2816 else min(N, 1024))
    tk = _tile(T, MM_K_TILE)
    row_bytes = 2 * (bm * a.dtype.itemsize + bn * b.dtype.itemsize)
    while tk > TOKEN_TILE and tk * row_bytes > VMEM_LIMIT // 3:
        tk //= 2
    assert M % bm == 0 and N % bn == 0 and T % tk == 0
    n_k = T // tk

    def body(a_ref, b_ref, o_ref):
        k = pl.program_id(2)

        @pl.when(k == 0)
        def _():
            o_ref[...] = jnp.zeros_like(o_ref)

        o_ref[...] += _dot_tn(a_ref[...].astype(BF16), b_ref[...].astype(BF16))

        if scale != 1.0:
            @pl.when(k == n_k - 1)
            def _():
                o_ref[...] = o_ref[...] * scale

    return pl.pallas_call(
        body, name=name, grid=(M // bm, N // bn, n_k),
        in_specs=[pl.BlockSpec((tk, bm), lambda m, n, k: (k, m)), pl.BlockSpec((tk, bn), lambda m, n, k: (k, n))],
        out_specs=pl.BlockSpec((bm, bn), lambda m, n, k: (m, n)),
        out_shape=_sds((M, N), F32),
        compiler_params=_params(("arbitrary", "arbitrary", "arbitrary")),
    )(a, b)


def mixin_fwd(h1, g, w_in_r, b_f_pad):
    T = h1.shape[0]
    tm = _tile(T, TOKEN_TILE)

    def body(h_ref, g_ref, w_ref, bf_ref, u_ref, qkv_ref, s_ref, fz_ref, c_ref, carry):
        @pl.when(pl.program_id(0) == 0)
        def _():
            carry[...] = jnp.zeros_like(carry)

        x = h_ref[...]
        u = (x * _rms_scale(x) * g_ref[...]).astype(BF16)
        u_ref[...] = u
        z = _dot(u, w_ref[...])
        qkv_ref[...] = z[:, :3 * ATTN_W].astype(BF16)
        s_ref[...] = z[:, 3 * ATTN_W:3 * ATTN_W + SSM_W]
        fz = z[:, 3 * ATTN_W + SSM_W:] + bf_ref[...]
        fz_ref[...] = fz
        lane = lax.broadcasted_iota(jnp.int32, fz.shape, 1)
        logf = jnp.where(lane < N_HEADS, jnp.minimum(fz, 0.0) - jnp.log(1.0 + jnp.exp(-jnp.abs(fz))), 0.0)
        row = lax.broadcasted_iota(jnp.int32, (tm, tm), 0)
        col = lax.broadcasted_iota(jnp.int32, (tm, tm), 1)
        tri = (col <= row).astype(F32)
        cs = jnp.dot(tri, logf, precision=HIGHEST, preferred_element_type=F32) + carry[0:1, :]
        c_ref[...] = cs
        carry[...] = jnp.broadcast_to(cs[tm - 1:tm, :], carry.shape)

    tok = lambda i: (i, 0)
    fix = lambda i: (0, 0)
    return pl.pallas_call(
        body, name="mixin_fwd", grid=(T // tm,),
        in_specs=[pl.BlockSpec((tm, D_MODEL), tok), pl.BlockSpec((1, D_MODEL), fix),
                  pl.BlockSpec((D_MODEL, Z_COLS), fix), pl.BlockSpec((1, 128), fix)],
        out_specs=[pl.BlockSpec((tm, D_MODEL), tok), pl.BlockSpec((tm, 3 * ATTN_W), tok), pl.BlockSpec((tm, SSM_W), tok),
                   pl.BlockSpec((tm, 128), tok), pl.BlockSpec((tm, 128), tok)],
        out_shape=[_sds((T, D_MODEL), BF16), _sds((T, 3 * ATTN_W), BF16), _sds((T, SSM_W), F32),
                   _sds((T, 128), F32), _sds((T, 128), F32)],
        scratch_shapes=[pltpu.VMEM((8, 128), F32)],
        compiler_params=_params(("arbitrary",)),
    )(h1, g, w_in_r, b_f_pad)


def mixin_bwd(dh2, h1, g, w_in_r, dq, dk, dv, ds, dc, fz):
    T = h1.shape[0]
    tm = _tile(T, TOKEN_TILE)
    n_t = T // tm

    def body(dh2_ref, h_ref, g_ref, w_ref, dq_ref, dk_ref, dv_ref, ds_ref, dc_ref, fz_ref,
             dh1_ref, dz_ref, dg_ref, dbf_ref, carry):
        @pl.when(pl.program_id(0) == 0)
        def _():
            carry[...] = jnp.zeros_like(carry)
            dg_ref[...] = jnp.zeros_like(dg_ref)
            dbf_ref[...] = jnp.zeros_like(dbf_ref)

        row = lax.broadcasted_iota(jnp.int32, (tm, tm), 0)
        col = lax.broadcasted_iota(jnp.int32, (tm, tm), 1)
        tri = (col >= row).astype(F32)
        dlogf = jnp.dot(tri, dc_ref[...], precision=HIGHEST, preferred_element_type=F32) + carry[0:1, :]
        carry[...] = jnp.broadcast_to(dlogf[0:1, :], carry.shape)
        dfz = dlogf * jax.nn.sigmoid(-fz_ref[...])
        dbf_ref[...] += jnp.sum(dfz, axis=0, keepdims=True)
        dz = jnp.concatenate([dq_ref[...], dk_ref[...], dv_ref[...], ds_ref[...], dfz], axis=1).astype(BF16)
        dz_ref[...] = dz
        du = _dot_nt(dz, w_ref[...])
        dx, dg = _rms_bwd(du, h_ref[...], g_ref[...])
        dg_ref[...] += dg
        dh1_ref[...] = dh2_ref[...] + dx

    tok = lambda i: (n_t - 1 - i, 0)
    fix = lambda i: (0, 0)
    return pl.pallas_call(
        body, name="mixin_bwd", grid=(n_t,),
        in_specs=[pl.BlockSpec((tm, D_MODEL), tok), pl.BlockSpec((tm, D_MODEL), tok), pl.BlockSpec((1, D_MODEL), fix),
                  pl.BlockSpec((D_MODEL, Z_COLS), fix), pl.BlockSpec((tm, ATTN_W), tok), pl.BlockSpec((tm, ATTN_W), tok),
                  pl.BlockSpec((tm, ATTN_W), tok), pl.BlockSpec((tm, SSM_W), tok), pl.BlockSpec((tm, 128), tok),
                  pl.BlockSpec((tm, 128), tok)],
        out_specs=[pl.BlockSpec((tm, D_MODEL), tok), pl.BlockSpec((tm, Z_COLS), tok), pl.BlockSpec((1, D_MODEL), fix),
                   pl.BlockSpec((1, 128), fix)],
        out_shape=[_sds((T, D_MODEL), F32), _sds((T, Z_COLS), BF16), _sds((1, D_MODEL), F32), _sds((1, 128), F32)],
        scratch_shapes=[pltpu.VMEM((8, 128), F32)],
        compiler_params=_params(("arbitrary",)),
    )(dh2, h1, g, w_in_r, dq, dk, dv, ds, dc, fz)


def attn_fwd(q, k, v_aug, ck_rows):
    H, T, hd = q.shape
    tq = _tile(T, ATTN_TILE)
    n = T // tq

    def body(q_ref, k_ref, v_ref, ck_ref, o_ref, lse_ref, m_sc, acc):
        qi = pl.program_id(1)
        qv = q_ref[0]
        m_sc[...] = jnp.full_like(m_sc, -jnp.inf)
        acc[...] = jnp.zeros_like(acc)

        def tile(j, masked):
            rows = pl.ds(pl.multiple_of(j * tq, tq), tq)
            s = _dot_nt(qv, k_ref[0, rows, :]) - ck_ref[0, pl.ds(j, 1), :]
            if masked:
                keep = lax.broadcasted_iota(jnp.int32, (tq, tq), 1) <= lax.broadcasted_iota(jnp.int32, (tq, tq), 0)
                s = jnp.where(keep, s, -1e30)
            m_old = m_sc[...]
            m_new = jnp.maximum(m_old, jnp.max(s, axis=1, keepdims=True))
            p = jnp.exp(s - m_new).astype(BF16)
            acc[...] = jnp.exp(m_old - m_new) * acc[...] + _dot(p, v_ref[0, rows, :])
            m_sc[...] = m_new

        def off_diagonal(j, carry):
            tile(j, False)
            return carry

        lax.fori_loop(0, qi, off_diagonal, 0)
        tile(qi, True)
        a = acc[...]
        total = a[:, hd:hd + 1]
        o_ref[0] = a[:, :hd] / total
        lse_ref[0] = m_sc[...] + jnp.log(total)

    qmap = lambda h, i: (h, i, 0)
    head = lambda h, i: (h, 0, 0)
    return pl.pallas_call(
        body, name="attn_fwd", grid=(H, n),
        in_specs=[pl.BlockSpec((1, tq, hd), qmap), pl.BlockSpec((1, T, hd), head), pl.BlockSpec((1, T, 2 * hd), head),
                  pl.BlockSpec((1, n, tq), head)],
        out_specs=[pl.BlockSpec((1, tq, hd), qmap), pl.BlockSpec((1, tq, 1), qmap)],
        out_shape=[_sds((H, T, hd), F32), _sds((H, T, 1), F32)],
        scratch_shapes=[pltpu.VMEM((tq, 1), F32), pltpu.VMEM((tq, 2 * hd), F32)],
        compiler_params=_params(("arbitrary", "arbitrary")),
    )(q, k, v_aug, ck_rows)


def attn_bwd(q, k, v, do, lse_rows, delta_rows, ck_col):
    H, T, hd = q.shape
    tq = _tile(T, ATTN_TILE)
    n = T // tq

    def body(q_ref, do_ref, lse_ref, dl_ref, k_ref, v_ref, ck_ref, dq_ref, dk_ref, dv_ref, dck_ref, dcq_ref):
        j = pl.program_id(1)

        @pl.when(j == 0)
        def _():
            dq_ref[...] = jnp.zeros_like(dq_ref)
            dcq_ref[...] = jnp.zeros_like(dcq_ref)

        dk_ref[...] = jnp.zeros_like(dk_ref)
        dv_ref[...] = jnp.zeros_like(dv_ref)
        dck_ref[...] = jnp.zeros_like(dck_ref)
        kv, vv, ck = k_ref[0], v_ref[0], ck_ref[0]

        def tile(i, masked):
            rows = pl.ds(pl.multiple_of(i * tq, tq), tq)
            qv, dov = q_ref[0, rows, :], do_ref[0, rows, :]
            pt = jnp.exp(_dot_nt(kv, qv) - ck - lse_ref[0, pl.ds(i, 1), :])
            if masked:
                keep = lax.broadcasted_iota(jnp.int32, (tq, tq), 0) <= lax.broadcasted_iota(jnp.int32, (tq, tq), 1)
                pt = jnp.where(keep, pt, 0.0)
            dv_ref[0] += _dot(pt.astype(BF16), dov)
            dst = pt * (_dot_nt(vv, dov) - dl_ref[0, pl.ds(i, 1), :])
            dsb = dst.astype(BF16)
            dk_ref[0] += _dot(dsb, qv)
            dq_ref[0, rows, :] += _dot_tn(dsb, kv)
            dck_ref[0] += -jnp.sum(dst, axis=1, keepdims=True)
            dcq_ref[0, pl.ds(i, 1), :] += jnp.sum(dst, axis=0, keepdims=True)

        def off_diagonal(i, carry):
            tile(i, False)
            return carry

        tile(j, True)
        lax.fori_loop(j + 1, n, off_diagonal, 0)

    head = lambda h, j: (h, 0, 0)
    kmap = lambda h, j: (h, j, 0)
    return pl.pallas_call(
        body, name="attn_bwd", grid=(H, n),
        in_specs=[pl.BlockSpec((1, T, hd), head), pl.BlockSpec((1, T, hd), head), pl.BlockSpec((1, n, tq), head),
                  pl.BlockSpec((1, n, tq), head), pl.BlockSpec((1, tq, hd), kmap), pl.BlockSpec((1, tq, hd), kmap),
                  pl.BlockSpec((1, tq, 1), kmap)],
        out_specs=[pl.BlockSpec((1, T, hd), head), pl.BlockSpec((1, tq, hd), kmap), pl.BlockSpec((1, tq, hd), kmap),
                   pl.BlockSpec((1, tq, 1), kmap), pl.BlockSpec((1, n, tq), head)],
        out_shape=[_sds((H, T, hd), F32), _sds((H, T, hd), F32), _sds((H, T, hd), F32), _sds((H, T, 1), F32),
                   _sds((H, n, tq), F32)],
        compiler_params=_params(("arbitrary", "arbitrary")),
    )(q, do, lse_rows, delta_rows, k, v, ck_col)


def _complex_step(a_r, a_i, cr, ci, br, bi):
    return a_r * cr - a_i * ci + br, a_r * ci + a_i * cr + bi


def ssm_fwd(s_perm, wb, cbd, a_r, a_i, al_r, al_i, dvec):
    T = s_perm.shape[0]
    chunk = T // 8
    ts = _tile(chunk, SCAN_STEPS)
    tr, n_s = ts * 8, chunk // ts
    W, LB = STATE_W, SCAN_LANES

    def body(s_ref, wb_ref, cbd_ref, ar_ref, ai_ref, alr_ref, ali_ref, dv_ref, y_ref, xs_ref, bu, carry):
        ph, i = pl.program_id(0), pl.program_id(1)

        @pl.when((ph == 0) & (i == 0))
        def _():
            carry[...] = jnp.zeros_like(carry)

        bu[...] = _dot(s_ref[...].astype(BF16), wb_ref[...])

        def scan(store):
            for lb in range(W // LB):
                lo = lb * LB
                re, im = slice(lo, lo + LB), slice(W + lo, W + lo + LB)
                ar = jnp.broadcast_to(ar_ref[:, re], (8, LB))
                ai = jnp.broadcast_to(ai_ref[:, re], (8, LB))

                def step(s, c):
                    rows = pl.ds(pl.multiple_of(s * 8, 8), 8)
                    nr, ni = _complex_step(ar, ai, c[0], c[1], bu[rows, re], bu[rows, im])
                    if store:
                        bu[rows, re] = nr
                        bu[rows, im] = ni
                    return nr, ni

                cr, ci = lax.fori_loop(0, ts, step, (carry[:, re], carry[:, im]), unroll=2)
                carry[:, re] = cr
                carry[:, im] = ci

        @pl.when(ph == 0)
        def _():
            scan(False)

            @pl.when(i == n_s - 1)
            def _():
                er, ei = carry[:, :W], carry[:, W:]
                alr = jnp.broadcast_to(alr_ref[...], (8, W))
                ali = jnp.broadcast_to(ali_ref[...], (8, W))
                first = lax.broadcasted_iota(jnp.int32, (8, W), 0) == 0
                sr, si = jnp.zeros((8, W), F32), jnp.zeros((8, W), F32)
                for _ in range(7):
                    vr, vi = _complex_step(alr, ali, sr, si, er, ei)
                    sr = jnp.where(first, 0.0, pltpu.roll(vr, 1, 0))
                    si = jnp.where(first, 0.0, pltpu.roll(vi, 1, 0))
                carry[:, :W] = sr
                carry[:, W:] = si

        @pl.when(ph == 1)
        def _():
            scan(True)
            xb = bu[...].astype(BF16)
            xs_ref[...] = xb
            y_ref[...] = _dot(xb, cbd_ref[...]) + s_ref[...] * dv_ref[...]

    fix = lambda p, i: (0, 0)
    return pl.pallas_call(
        body, name="ssm_fwd", grid=(2, n_s),
        in_specs=[pl.BlockSpec((tr, SSM_W), lambda p, i: (i, 0)), pl.BlockSpec((SSM_W, 2 * W), fix),
                  pl.BlockSpec((2 * W, SSM_W), fix), pl.BlockSpec((1, W), fix), pl.BlockSpec((1, W), fix),
                  pl.BlockSpec((1, W), fix), pl.BlockSpec((1, W), fix), pl.BlockSpec((1, SSM_W), fix)],
        out_specs=[pl.BlockSpec((tr, SSM_W), lambda p, i: (i * p, 0)), pl.BlockSpec((tr, 2 * W), lambda p, i: (i * p, 0))],
        out_shape=[_sds((T, SSM_W), F32), _sds((T, 2 * W), BF16)],
        scratch_shapes=[pltpu.VMEM((tr, 2 * W), F32), pltpu.VMEM((8, 2 * W), F32)],
        compiler_params=_params(("arbitrary", "arbitrary")),
    )(s_perm, wb, cbd, a_r, a_i, al_r, al_i, dvec)


def ssm_bwd(dy_perm, s_perm, xs, cbd_t, wb_t, a_r, a_i, al_r, al_i, dvec):
    T = s_perm.shape[0]
    chunk = T // 8
    ts = _tile(chunk, SCAN_STEPS)
    tr, n_s = ts * 8, chunk // ts
    W, LB = STATE_W, SCAN_LANES

    def body(dy_ref, s_ref, xs_ref, cbt_ref, wbt_ref, ar_ref, ai_ref, alr_ref, ali_ref, dv_ref,
             du_ref, gs_ref, da_ref, dd_ref, gd, x32, carry):
        ph, i = pl.program_id(0), pl.program_id(1)

        @pl.when((ph == 0) & (i == 0))
        def _():
            carry[...] = jnp.zeros_like(carry)
            da_ref[...] = jnp.zeros_like(da_ref)
            dd_ref[...] = jnp.zeros_like(dd_ref)

        gd[...] = _dot(dy_ref[...].astype(BF16), cbt_ref[...])

        def scan(store):
            for lb in range(W // LB):
                lo = lb * LB
                re, im = slice(lo, lo + LB), slice(W + lo, W + lo + LB)
                ar = jnp.broadcast_to(ar_ref[:, re], (8, LB))
                nai = -jnp.broadcast_to(ai_ref[:, re], (8, LB))

                def step(k, c):
                    rows = pl.ds(pl.multiple_of((ts - 1 - k) * 8, 8), 8)
                    cr, ci = c[0], c[1]
                    nr, ni = _complex_step(ar, nai, cr, ci, gd[rows, re], gd[rows, im])
                    if store:
                        xr, xi = x32[rows, re], x32[rows, im]
                        gd[rows, re] = nr
                        gd[rows, im] = ni
                        return nr, ni, c[2] + cr * xr + ci * xi, c[3] + ci * xr - cr * xi
                    return nr, ni

                init = (carry[:, re], carry[:, im])
                if store:
                    init = init + (da_ref[:, re], da_ref[:, im])
                out = lax.fori_loop(0, ts, step, init, unroll=2)
                carry[:, re] = out[0]
                carry[:, im] = out[1]
                if store:
                    da_ref[:, re] = out[2]
                    da_ref[:, im] = out[3]

        @pl.when(ph == 0)
        def _():
            scan(False)

            @pl.when(i == n_s - 1)
            def _():
                er, ei = carry[:, :W], carry[:, W:]
                alr = jnp.broadcast_to(alr_ref[...], (8, W))
                nali = -jnp.broadcast_to(ali_ref[...], (8, W))
                last = lax.broadcasted_iota(jnp.int32, (8, W), 0) == 7
                rr, ri = jnp.zeros((8, W), F32), jnp.zeros((8, W), F32)
                for _ in range(7):
                    vr, vi = _complex_step(alr, nali, rr, ri, er, ei)
                    rr = jnp.where(last, 0.0, pltpu.roll(vr, 7, 0))
                    ri = jnp.where(last, 0.0, pltpu.roll(vi, 7, 0))
                carry[:, :W] = rr
                carry[:, W:] = ri

        @pl.when(ph == 1)
        def _():
            x32[...] = xs_ref[...].astype(F32)
            scan(True)
            gb = gd[...].astype(BF16)
            gs_ref[...] = gb
            dy = dy_ref[...]
            du_ref[...] = _dot(gb, wbt_ref[...]) + dy * dv_ref[...]
            dd_ref[...] += jnp.sum(dy * s_ref[...], axis=0, keepdims=True)

    fix = lambda p, i: (0, 0)
    rev = lambda p, i: (n_s - 1 - i, 0)
    rev_out = lambda p, i: (n_s - 1 - i * p, 0)
    return pl.pallas_call(
        body, name="ssm_bwd", grid=(2, n_s),
        in_specs=[pl.BlockSpec((tr, SSM_W), rev), pl.BlockSpec((tr, SSM_W), rev), pl.BlockSpec((tr, 2 * W), rev),
                  pl.BlockSpec((SSM_W, 2 * W), fix), pl.BlockSpec((2 * W, SSM_W), fix), pl.BlockSpec((1, W), fix),
                  pl.BlockSpec((1, W), fix), pl.BlockSpec((1, W), fix), pl.BlockSpec((1, W), fix),
                  pl.BlockSpec((1, SSM_W), fix)],
        out_specs=[pl.BlockSpec((tr, SSM_W), rev_out), pl.BlockSpec((tr, 2 * W), rev_out),
                   pl.BlockSpec((8, 2 * W), fix), pl.BlockSpec((1, SSM_W), fix)],
        out_shape=[_sds((T, SSM_W), F32), _sds((T, 2 * W), BF16), _sds((8, 2 * W), F32), _sds((1, SSM_W), F32)],
        scratch_shapes=[pltpu.VMEM((tr, 2 * W), F32), pltpu.VMEM((tr, 2 * W), F32), pltpu.VMEM((8, 2 * W), F32)],
        compiler_params=_params(("arbitrary", "arbitrary")),
    )(dy_perm, s_perm, xs, cbd_t, wb_t, a_r, a_i, al_r, al_i, dvec)


def mixout_fwd(h1, attn, ypre, g_a, g_s, w_glu, b_glu, w_out):
    T = h1.shape[0]
    tm = _tile(T, TOKEN_TILE)

    def body(h_ref, at_ref, yp_ref, ga_ref, gs_ref, wg_ref, bg_ref, wo_ref, h2_ref, mixed_ref):
        yg, _ = _gelu_parts(yp_ref[...])
        gl = yg * jax.nn.sigmoid(_dot(yg.astype(BF16), wg_ref[...]) + bg_ref[...])
        at = at_ref[...]
        mixed = jnp.concatenate([at * _rms_scale(at) * ga_ref[...], gl * _rms_scale(gl) * gs_ref[...]], axis=1)
        mixed = mixed.astype(BF16)
        mixed_ref[...] = mixed
        h2_ref[...] = h_ref[...] + _dot(mixed, wo_ref[...])

    tok = lambda i: (i, 0)
    fix = lambda i: (0, 0)
    return pl.pallas_call(
        body, name="mixout_fwd", grid=(T // tm,),
        in_specs=[pl.BlockSpec((tm, D_MODEL), tok), pl.BlockSpec((tm, ATTN_W), tok), pl.BlockSpec((tm, SSM_W), tok),
                  pl.BlockSpec((1, ATTN_W), fix), pl.BlockSpec((1, SSM_W), fix), pl.BlockSpec((SSM_W, SSM_W), fix),
                  pl.BlockSpec((1, SSM_W), fix), pl.BlockSpec((D_MODEL, D_MODEL), fix)],
        out_specs=[pl.BlockSpec((tm, D_MODEL), tok), pl.BlockSpec((tm, D_MODEL), tok)],
        out_shape=[_sds((T, D_MODEL), F32), _sds((T, D_MODEL), BF16)],
        compiler_params=_params(("arbitrary",)),
    )(h1, attn, ypre, g_a, g_s, w_glu, b_glu, w_out)


def mixout_bwd(dh2, attn, ypre, g_a, g_s, w_glu, b_glu, w_out, seg):
    T = dh2.shape[0]
    tm = _tile(T, TOKEN_TILE)

    def body(dh_ref, at_ref, yp_ref, ga_ref, gs_ref, wg_ref, bg_ref, wo_ref, seg_ref,
             dat_ref, dyp_ref, dpre_ref, yg_ref, dl_ref, dga_ref, dgs_ref, dbg_ref):
        @pl.when(pl.program_id(0) == 0)
        def _():
            dga_ref[...] = jnp.zeros_like(dga_ref)
            dgs_ref[...] = jnp.zeros_like(dgs_ref)
            dbg_ref[...] = jnp.zeros_like(dbg_ref)

        dmix = _dot_nt(dh_ref[...].astype(BF16), wo_ref[...])
        at = at_ref[...]
        dat, dga = _rms_bwd(dmix[:, :ATTN_W], at, ga_ref[...])
        dga_ref[...] += dga
        dat_ref[...] = dat
        dl_ref[...] = jnp.dot(dat * at, seg_ref[...], precision=HIGHEST, preferred_element_type=F32)
        yp = yp_ref[...]
        yg, t = _gelu_parts(yp)
        ygb = yg.astype(BF16)
        yg_ref[...] = ygb
        sg = jax.nn.sigmoid(_dot(ygb, wg_ref[...]) + bg_ref[...])
        dgl, dgs = _rms_bwd(dmix[:, ATTN_W:], yg * sg, gs_ref[...])
        dgs_ref[...] += dgs
        dpre = dgl * yg * sg * (1.0 - sg)
        dbg_ref[...] += jnp.sum(dpre, axis=0, keepdims=True)
        dpb = dpre.astype(BF16)
        dpre_ref[...] = dpb
        dyg = dgl * sg + _dot_nt(dpb, wg_ref[...])
        dyp_ref[...] = dyg * _gelu_grad(yp, t)

    tok = lambda i: (i, 0)
    fix = lambda i: (0, 0)
    return pl.pallas_call(
        body, name="mixout_bwd", grid=(T // tm,),
        in_specs=[pl.BlockSpec((tm, D_MODEL), tok), pl.BlockSpec((tm, ATTN_W), tok), pl.BlockSpec((tm, SSM_W), tok),
                  pl.BlockSpec((1, ATTN_W), fix), pl.BlockSpec((1, SSM_W), fix), pl.BlockSpec((SSM_W, SSM_W), fix),
                  pl.BlockSpec((1, SSM_W), fix), pl.BlockSpec((D_MODEL, D_MODEL), fix), pl.BlockSpec((ATTN_W, 128), fix)],
        out_specs=[pl.BlockSpec((tm, ATTN_W), tok), pl.BlockSpec((tm, SSM_W), tok), pl.BlockSpec((tm, SSM_W), tok),
                   pl.BlockSpec((tm, SSM_W), tok), pl.BlockSpec((tm, 128), tok), pl.BlockSpec((1, ATTN_W), fix),
                   pl.BlockSpec((1, SSM_W), fix), pl.BlockSpec((1, SSM_W), fix)],
        out_shape=[_sds((T, ATTN_W), F32), _sds((T, SSM_W), F32), _sds((T, SSM_W), BF16), _sds((T, SSM_W), BF16),
                   _sds((T, 128), F32), _sds((1, ATTN_W), F32), _sds((1, SSM_W), F32), _sds((1, SSM_W), F32)],
        compiler_params=_params(("arbitrary",)),
    )(dh2, attn, ypre, g_a, g_s, w_glu, b_glu, w_out, seg)


def head_fwd_bwd(h3, p, target, g_ple, g_final, w_gate, w_proj):
    T = h3.shape[0]
    tm = _tile(T, TOKEN_TILE)
    pd = p.shape[1]

    def body(h_ref, p_ref, tg_ref, gp_ref, gf_ref, wg_ref, wp_ref,
             dh_ref, n3_ref, dz_ref, dpp_ref, loss_ref, dgp_ref, dgf_ref):
        @pl.when(pl.program_id(0) == 0)
        def _():
            loss_ref[...] = jnp.zeros_like(loss_ref)
            dgp_ref[...] = jnp.zeros_like(dgp_ref)
            dgf_ref[...] = jnp.zeros_like(dgf_ref)

        x = h_ref[...]
        gp, gf = gp_ref[...], gf_ref[...]
        n3 = (x * _rms_scale(x) * gp).astype(BF16)
        n3_ref[...] = n3
        gate = jax.nn.sigmoid(_dot(n3, wg_ref[...]))
        pp = _dot(p_ref[...].astype(BF16), wp_ref[...])
        h4 = x + gate * pp
        y = h4 * _rms_scale(h4) * gf
        e = y - tg_ref[...]
        tile_loss = jnp.sum(jnp.sum(e * e, axis=1, keepdims=True), axis=0, keepdims=True) * (0.5 / D_MODEL)
        loss_ref[...] += jnp.broadcast_to(tile_loss, loss_ref.shape)
        dh4, dgf = _rms_bwd(e * (1.0 / D_MODEL), h4, gf)
        dgf_ref[...] += dgf
        dzg = dh4 * pp * gate * (1.0 - gate)
        dzb = dzg.astype(BF16)
        dz_ref[...] = dzb
        dpp_ref[...] = (dh4 * gate).astype(BF16)
        dx, dgp = _rms_bwd(_dot_nt(dzb, wg_ref[...]), x, gp)
        dgp_ref[...] += dgp
        dh_ref[...] = dh4 + dx

    tok = lambda i: (i, 0)
    fix = lambda i: (0, 0)
    return pl.pallas_call(
        body, name="head_fwd_bwd", grid=(T // tm,),
        in_specs=[pl.BlockSpec((tm, D_MODEL), tok), pl.BlockSpec((tm, pd), tok), pl.BlockSpec((tm, D_MODEL), tok),
                  pl.BlockSpec((1, D_MODEL), fix), pl.BlockSpec((1, D_MODEL), fix), pl.BlockSpec((D_MODEL, D_MODEL), fix),
                  pl.BlockSpec((pd, D_MODEL), fix)],
        out_specs=[pl.BlockSpec((tm, D_MODEL), tok), pl.BlockSpec((tm, D_MODEL), tok), pl.BlockSpec((tm, D_MODEL), tok),
                   pl.BlockSpec((tm, D_MODEL), tok), pl.BlockSpec((8, 128), fix), pl.BlockSpec((1, D_MODEL), fix),
                   pl.BlockSpec((1, D_MODEL), fix)],
        out_shape=[_sds((T, D_MODEL), F32), _sds((T, D_MODEL), BF16), _sds((T, D_MODEL), BF16), _sds((T, D_MODEL), BF16),
                   _sds((8, 128), F32), _sds((1, D_MODEL), F32), _sds((1, D_MODEL), F32)],
        compiler_params=_params(("arbitrary",)),
    )(h3, p, target, g_ple, g_final, w_gate, w_proj)


def _row_tile(rows, cols, n_arrays):
    lanes = -(-cols // 128) * 128
    cap = VMEM_LIMIT // 3 // (2 * n_arrays * lanes * 4)
    best = None
    for t in range(PACK_ALIGN, min(rows, cap) + 1, PACK_ALIGN):
        if rows % t == 0:
            best = t
    assert best is not None, (rows, cols)
    return best


def _adamw_math(w, g, m, v):
    nm = ADAM_B1 * m + (1.0 - ADAM_B1) * g
    nv = ADAM_B2 * v + (1.0 - ADAM_B2) * (g * g)
    c1 = 1.0 - ADAM_B1 ** ADAM_STEP
    c2 = 1.0 - ADAM_B2 ** ADAM_STEP
    return -ADAM_LR * ((nm / c1) / (jnp.sqrt(nv / c2) + ADAM_EPS) + ADAM_WD * w), nm, nv


def adamw(w, g, m, v, name):
    R, C = w.shape
    tr = _row_tile(R, C, 7)

    def body(w_ref, g_ref, m_ref, v_ref, d_ref, nm_ref, nv_ref):
        d_ref[...], nm_ref[...], nv_ref[...] = _adamw_math(w_ref[...], g_ref[...], m_ref[...], v_ref[...])

    spec = pl.BlockSpec((tr, C), lambda i: (i, 0))
    return pl.pallas_call(
        body, name=name, grid=(R // tr,), in_specs=[spec] * 4, out_specs=[spec] * 3,
        out_shape=[_sds((R, C), F32)] * 3, compiler_params=_params(("arbitrary",)),
    )(w, g, m, v)


def join_halves(mine, other, core):
    rh, C = mine.shape
    tr = _row_tile(rh, C, 3)
    nb = rh // tr

    def body(c_ref, m_ref, o_ref, out_ref):
        out_ref[...] = jnp.where((pl.program_id(0) // nb) == c_ref[0], m_ref[...], o_ref[...])

    half = pl.BlockSpec((tr, C), lambda i, c: (i % nb, 0))
    return pl.pallas_call(
        body, name="join_halves",
        grid_spec=pltpu.PrefetchScalarGridSpec(num_scalar_prefetch=1, grid=(2 * nb,), in_specs=[half, half],
                                               out_specs=pl.BlockSpec((tr, C), lambda i, c: (i, 0))),
        out_shape=_sds((2 * rh, C), F32), compiler_params=_params(("arbitrary",)),
    )(core, mine, other)


def pair_sum(g, theirs, core):
    n, R, C = g.shape
    rh = R // 2
    tr = _row_tile(rh, C, 3)
    nb = rh // tr

    def body(c_ref, g_ref, t_ref, o_ref):
        o_ref[...] = (g_ref[...] + t_ref[...]).astype(BF16)

    here = pl.BlockSpec((1, tr, C), lambda j, i, c: (j, i, 0))
    return pl.pallas_call(
        body, name="pair_sum",
        grid_spec=pltpu.PrefetchScalarGridSpec(
            num_scalar_prefetch=1, grid=(n, nb),
            in_specs=[pl.BlockSpec((1, tr, C), lambda j, i, c: (j, c[0] * nb + i, 0)), here], out_specs=here),
        out_shape=_sds((n, rh, C), BF16), compiler_params=_params(("arbitrary", "arbitrary")),
    )(core, g, theirs)


def chip_sum(pair, got, chip):
    _, R, C = pair.shape
    tr = _row_tile(R, C, 5)

    def body(c_ref, p_ref, g0_ref, g1_ref, g2_ref, o_ref):
        f = lambda ref: ref[0].astype(F32)
        o_ref[...] = ((f(p_ref) + f(g0_ref)) + f(g1_ref)) + f(g2_ref)

    slot = lambda k: pl.BlockSpec((1, tr, C), lambda i, c: (k, i, 0))
    return pl.pallas_call(
        body, name="chip_sum",
        grid_spec=pltpu.PrefetchScalarGridSpec(
            num_scalar_prefetch=1, grid=(R // tr,),
            in_specs=[pl.BlockSpec((1, tr, C), lambda i, c: (c[0], i, 0)), slot(0), slot(1), slot(2)],
            out_specs=pl.BlockSpec((tr, C), lambda i, c: (i, 0))),
        out_shape=_sds((R, C), F32), compiler_params=_params(("arbitrary",)),
    )(chip, pair, got, got, got)


_HBM = pl.BlockSpec(memory_space=pltpu.HBM)


def _place():
    x, y, c = lax.axis_index("x"), lax.axis_index("y"), lax.axis_index("c")
    return x, y, c, [(1 - x, y), (x, 1 - y), (1 - x, 1 - y)]


def _spans(rows, n):
    assert rows % PACK_ALIGN == 0
    tiles = rows // PACK_ALIGN
    n = min(n, tiles)
    cuts = [tiles * q // n for q in range(n + 1)]
    return [(cuts[q] * PACK_ALIGN, (cuts[q + 1] - cuts[q]) * PACK_ALIGN) for q in range(n)]


def _remote(src, dst, send_sem, recv_sem, to):
    return pltpu.make_async_remote_copy(src_ref=src, dst_ref=dst, send_sem=send_sem, recv_sem=recv_sem,
                                        device_id=to, device_id_type=MESH)


def allgather_shards(wp):
    R, C = wp.shape
    rh = R // 2
    spans = _spans(rh, COPY_CHUNKS)
    n_sp = len(spans)
    local_spans = _spans(R, 2 * COPY_CHUNKS)

    def body(w_ref, out_ref, send_sems, recv_sems, pass_send, pass_recv, local_sems):
        x, y, c, chips = _place()
        me = 2 * x + y
        local = []
        for q, (o, n) in enumerate(local_spans):
            cp = pltpu.make_async_copy(w_ref.at[pl.ds(o, n), :], out_ref.at[me, pl.ds(o, n), :], local_sems.at[q])
            cp.start()
            local.append(cp)
        sends = []
        for k, (cx, cy) in enumerate(chips):
            for q, (o, n) in enumerate(spans):
                rows = pl.ds(c * rh + o, n)
                cp = _remote(w_ref.at[rows, :], out_ref.at[me, rows, :], send_sems.at[k * n_sp + q],
                             recv_sems.at[k * n_sp + q], (cx, cy, c))
                cp.start()
                sends.append(cp)
        for k, (cx, cy) in enumerate(chips):
            for q, (o, n) in enumerate(spans):
                blk = out_ref.at[2 * cx + cy, pl.ds(c * rh + o, n), :]
                _remote(blk, blk, send_sems.at[k * n_sp + q], recv_sems.at[k * n_sp + q], (cx, cy, c)).wait_recv()
                cp = _remote(blk, blk, pass_send.at[k * n_sp + q], pass_recv.at[k * n_sp + q], (x, y, 1 - c))
                cp.start()
                sends.append(cp)
        for k, (cx, cy) in enumerate(chips):
            for q, (o, n) in enumerate(spans):
                blk = out_ref.at[2 * cx + cy, pl.ds((1 - c) * rh + o, n), :]
                _remote(blk, blk, pass_send.at[k * n_sp + q], pass_recv.at[k * n_sp + q], (x, y, 1 - c)).wait_recv()
        for cp in sends:
            cp.wait_send()
        for cp in local:
            cp.wait()

    sems = pltpu.SemaphoreType.DMA((3 * n_sp,))
    return pl.pallas_call(
        body, name="allgather_shards", in_specs=[_HBM], out_specs=_HBM, out_shape=_sds((4, R, C), wp.dtype),
        scratch_shapes=[sems, sems, sems, sems, pltpu.SemaphoreType.DMA((len(local_spans),))],
    )(wp)


def sibling_split(g):
    n_sl, R, C = g.shape
    rh = R // 2
    spans = _spans(rh, COPY_CHUNKS)
    n_sp = len(spans)

    def body(g_ref, got_ref, send_sems, recv_sems):
        x, y, c, _ = _place()
        copies = []
        for j in range(n_sl):
            for q, (o, n) in enumerate(spans):
                cp = _remote(g_ref.at[j, pl.ds((1 - c) * rh + o, n), :], got_ref.at[j, pl.ds(o, n), :],
                             send_sems.at[j * n_sp + q], recv_sems.at[j * n_sp + q], (x, y, 1 - c))
                cp.start()
                copies.append(cp)
        for cp in copies:
            cp.wait()

    sems = pltpu.SemaphoreType.DMA((n_sl * n_sp,))
    return pl.pallas_call(
        body, name="sibling_split", in_specs=[_HBM], out_specs=_HBM, out_shape=_sds((n_sl, rh, C), g.dtype),
        scratch_shapes=[sems, sems],
    )(g)


def chip_exchange(p):
    _, R, C = p.shape
    spans = _spans(R, COPY_CHUNKS)
    n_sp = len(spans)

    def body(p_ref, buf_ref, send_sems, recv_sems):
        x, y, c, chips = _place()
        sends = []
        for k, (cx, cy) in enumerate(chips):
            for q, (o, n) in enumerate(spans):
                cp = _remote(p_ref.at[2 * cx + cy, pl.ds(o, n), :], buf_ref.at[k, pl.ds(o, n), :],
                             send_sems.at[k * n_sp + q], recv_sems.at[k * n_sp + q], (cx, cy, c))
                cp.start()
                sends.append(cp)
        for cp in sends:
            cp.wait()

    sems = pltpu.SemaphoreType.DMA((3 * n_sp,))
    return pl.pallas_call(
        body, name="chip_exchange", in_specs=[_HBM], out_specs=_HBM, out_shape=_sds((3, R, C), p.dtype),
        scratch_shapes=[sems, sems],
    )(p)


def sibling_swap(half):
    R, C = half.shape
    spans = _spans(R, COPY_CHUNKS)

    def body(h_ref, got_ref, send_sems, recv_sems):
        x, y, c, _ = _place()
        copies = []
        for q, (o, n) in enumerate(spans):
            cp = _remote(h_ref.at[pl.ds(o, n), :], got_ref.at[pl.ds(o, n), :], send_sems.at[q], recv_sems.at[q], (x, y, 1 - c))
            cp.start()
            copies.append(cp)
        for cp in copies:
            cp.wait()

    sems = pltpu.SemaphoreType.DMA((len(spans),))
    return pl.pallas_call(
        body, name="sibling_swap", in_specs=[_HBM], out_specs=_HBM, out_shape=_sds((R, C), half.dtype),
        scratch_shapes=[sems, sems],
    )(half)


def allreduce_small(v):
    R, C = v.shape

    def body(v_ref, out_ref, buf, send_sems, recv_sems):
        x, y, c, _ = _place()
        me = 4 * x + 2 * y + c
        buf[me] = v_ref[...]
        flips = [((k >> 2) & 1, (k >> 1) & 1, k & 1) for k in range(1, 8)]
        sends = []
        for k, (fx, fy, fc) in enumerate(flips):
            to = (1 - x if fx else x, 1 - y if fy else y, 1 - c if fc else c)
            cp = _remote(v_ref, buf.at[me], send_sems.at[k], recv_sems.at[k], to)
            cp.start()
            sends.append(cp)
        for k, (fx, fy, fc) in enumerate(flips):
            px, py, pc = (1 - x if fx else x, 1 - y if fy else y, 1 - c if fc else c)
            blk = buf.at[4 * px + 2 * py + pc]
            _remote(blk, blk, send_sems.at[k], recv_sems.at[k], (px, py, pc)).wait_recv()
        for cp in sends:
            cp.wait_send()
        acc = buf[0]
        for s in range(1, 8):
            acc = acc + buf[s]
        out_ref[...] = acc

    vm = pl.BlockSpec(memory_space=pltpu.VMEM)
    return pl.pallas_call(
        body, name="allreduce_small", in_specs=[vm], out_specs=vm, out_shape=_sds((R, C), F32),
        scratch_shapes=[pltpu.VMEM((8, R, C), F32), pltpu.SemaphoreType.DMA((7,)), pltpu.SemaphoreType.DMA((7,))],
        compiler_params=pltpu.CompilerParams(vmem_limit_bytes=VMEM_LIMIT),
    )(v)


def _rows_of(shape):
    return shape[0] * shape[1] // PACK_COLS


def _slot_rows(shape):
    return -(-_rows_of(shape) // PACK_ALIGN) * PACK_ALIGN


def _pack_shards(shards, dtype):
    parts = []
    for name, shape, _ in BIG:
        part = shards[name].reshape(_rows_of(shape), PACK_COLS).astype(dtype)
        parts.append(jnp.pad(part, ((0, _slot_rows(shape) - part.shape[0]), (0, 0))))
    used = sum(p.shape[0] for p in parts)
    parts.append(jnp.zeros((PACK_ROWS - used, PACK_COLS), dtype))
    return jnp.concatenate(parts, axis=0)


def _unpack_gathered(ag):
    out, off = {}, 0
    for name, shape, axis in BIG:
        r = _rows_of(shape)
        piece = ag[:, off:off + r, :].reshape((4,) + shape)
        off += _slot_rows(shape)
        if axis == 0:
            out[name] = piece.reshape(4 * shape[0], shape[1])
        else:
            out[name] = piece.transpose(1, 0, 2).reshape(shape[0], 4 * shape[1])
    return out


def _pack_full_grads(grads):
    parts = []
    for name, shape, axis in BIG:
        g = grads[name]
        if axis == 0:
            piece = g.reshape((4,) + shape)
        else:
            piece = g.reshape(shape[0], 4, shape[1]).transpose(1, 0, 2)
        piece = piece.reshape(4, _rows_of(shape), PACK_COLS)
        parts.append(jnp.pad(piece, ((0, 0), (0, _slot_rows(shape) - piece.shape[1]), (0, 0))))
    used = sum(p.shape[1] for p in parts)
    parts.append(jnp.zeros((4, PACK_ROWS - used, PACK_COLS), F32))
    return jnp.concatenate(parts, axis=1)


def _unpack_shards(packed):
    out, off = {}, 0
    for name, shape, _ in BIG:
        r = _rows_of(shape)
        out[name] = packed[off:off + r].reshape((1,) + shape)
        off += _slot_rows(shape)
    return out


def _pack_small(vals, extra=None):
    parts = [vals[name].reshape(-1) for name, _ in SMALL]
    used = sum(p.shape[0] for p in parts)
    if extra is not None:
        parts.append(extra.reshape(1))
        used += 1
    parts.append(jnp.zeros((SMALL_ROWS * 128 - used,), F32))
    return jnp.concatenate(parts).reshape(SMALL_ROWS, 128)


def _unpack_small(packed):
    flat = packed.reshape(-1)
    out, off = {}, 0
    for name, shape in SMALL:
        n = math.prod(shape)
        out[name] = flat[off:off + n].reshape(shape)
        off += n
    return out, flat[off]


def _to_heads(a):
    T = a.shape[0]
    return a.reshape(T, N_HEADS, HEAD_DIM).transpose(1, 0, 2)


def _from_heads(a):
    T = a.shape[1]
    return a.transpose(1, 0, 2).reshape(T, N_HEADS * HEAD_DIM)


def _permute_time(a):
    T, n = a.shape
    return a.reshape(8, T // 8, n).transpose(1, 0, 2).reshape(T, n)


def _unpermute_time(a):
    T, n = a.shape
    return a.reshape(T // 8, 8, n).transpose(1, 0, 2).reshape(T, n)


def _discretize(a_re, a_im, log_dt, b_re, b_im):
    dt = jnp.exp(log_dt)[:, None]
    decay = jnp.exp(dt * a_re)
    abar_r = decay * jnp.cos(dt * a_im)
    abar_i = decay * jnp.sin(dt * a_im)
    nr, ni = abar_r - 1.0, abar_i
    den = a_re * a_re + a_im * a_im
    fr = (nr * a_re + ni * a_im) / den
    fi = (ni * a_re - nr * a_im) / den
    bbar_r = fr[..., None] * b_re - fi[..., None] * b_im
    bbar_i = fr[..., None] * b_im + fi[..., None] * b_re
    return abar_r, abar_i, bbar_r, bbar_i


def _input_matrix(bbar_r, bbar_i):
    eye = jnp.eye(N_GROUPS, dtype=F32)
    blk = lambda b: jnp.einsum("ghp,gk->ghkp", b.transpose(0, 2, 1), eye).reshape(SSM_W, STATE_W)
    return jnp.concatenate([blk(bbar_r), blk(bbar_i)], axis=1)


def _output_matrix(c_re, c_im):
    eye = jnp.eye(N_GROUPS, dtype=F32)
    blk = lambda cm: jnp.einsum("ghp,gk->gpkh", cm, eye).reshape(STATE_W, SSM_W)
    return jnp.concatenate([blk(c_re), -blk(c_im)], axis=0)


def _state_power(ar, ai, n):
    steps = int(round(math.log2(n)))
    assert 1 << steps == n
    for _ in range(steps):
        ar, ai = ar * ar - ai * ai, 2.0 * ar * ai
    return ar, ai


def kernel(x, p, g_ffn1, w1_a, w3_a, w2_a, g_mix, w_in, b_f, a_re, a_im, log_dt, b_re, b_im, c_re, c_im, d_skip, w_glu, b_glu, g_attn_out, g_ssm_out, w_out, g_ffn2, w1_b, w3_b, w2_b, g_ple, w_ple_gate, w_ple_proj, g_final, loss_target, m_g_ffn1, m_w1_a, m_w3_a, m_w2_a, m_g_mix, m_w_in, m_b_f, m_a_re, m_a_im, m_log_dt, m_b_re, m_b_im, m_c_re, m_c_im, m_d_skip, m_w_glu, m_b_glu, m_g_attn_out, m_g_ssm_out, m_w_out, m_g_ffn2, m_w1_b, m_w3_b, m_w2_b, m_g_ple, m_w_ple_gate, m_w_ple_proj, m_g_final, v_g_ffn1, v_w1_a, v_w3_a, v_w2_a, v_g_mix, v_w_in, v_b_f, v_a_re, v_a_im, v_log_dt, v_b_re, v_b_im, v_c_re, v_c_im, v_d_skip, v_w_glu, v_b_glu, v_g_attn_out, v_g_ssm_out, v_w_out, v_g_ffn2, v_w1_b, v_w3_b, v_w2_b, v_g_ple, v_w_ple_gate, v_w_ple_proj, v_g_final):
    args = dict(locals())
    weights = {n: args[n] for n in WEIGHT_ORDER}
    moms = {n: args["m_" + n] for n in WEIGHT_ORDER}
    vars_ = {n: args["v_" + n] for n in WEIGHT_ORDER}
    T = x.shape[1]
    x2, p2, tgt = x[0], p[0, 0], loss_target[0]

    full = _unpack_gathered(allgather_shards(_pack_shards({n: weights[n][0] for n, _, _ in BIG}, BF16)))
    loss_part, dx, grads = _local_step(x2, p2, tgt, {n: weights[n] for n, _ in SMALL}, full)
    return _reduce_and_update(weights, moms, vars_, loss_part, dx, grads)


def _local_step(x2, p2, tgt, sm, full):
    T = x2.shape[0]
    (g_ffn1, g_mix, b_f, a_re, a_im, log_dt, b_re, b_im, c_re, c_im, d_skip, b_glu, g_attn_out, g_ssm_out, g_ffn2, g_ple,
     g_final) = (sm[n] for n, _ in SMALL)
    w_in_f = full["w_in"]
    w_in_r = jnp.concatenate([w_in_f[:, :ATTN_W] * QK_SCALE, w_in_f[:, ATTN_W:3 * ATTN_W], w_in_f[:, 3 * ATTN_W + N_HEADS:],
                              w_in_f[:, 3 * ATTN_W:3 * ATTN_W + N_HEADS], jnp.zeros((D_MODEL, 128 - N_HEADS), BF16)], axis=1)
    b_f_pad = jnp.pad(b_f, ((0, 0), (0, 128 - N_HEADS)))

    disc_in = (a_re[0], a_im[0], log_dt[0], b_re[0], b_im[0])
    (abar_r, abar_i, bbar_r, bbar_i), disc_vjp = jax.vjp(_discretize, *disc_in)
    wb = _input_matrix(bbar_r, bbar_i)
    cbd = _output_matrix(c_re[0], c_im[0])
    ar, ai = abar_r.reshape(1, STATE_W), abar_i.reshape(1, STATE_W)
    alr, ali = _state_power(ar, ai, T // 8)
    dvec = d_skip.reshape(1, SSM_W)
    wb16, cbd16 = wb.astype(BF16), cbd.astype(BF16)

    h1, a1a, a3a, n1 = ffn_fwd(x2, g_ffn1, full["w1_a"], full["w3_a"], full["w2_a"], "ffn_a_fwd")
    u, qkv, s_in, fz, cum = mixin_fwd(h1, g_mix, w_in_r, b_f_pad)
    qh, kh, vh = (_to_heads(qkv[:, i * ATTN_W:(i + 1) * ATTN_W]) for i in range(3))
    n_at = T // _tile(T, ATTN_TILE)
    c_heads = cum[:, :N_HEADS].T
    v_aug = jnp.concatenate([vh, jnp.ones((N_HEADS, T, 1), BF16), jnp.zeros((N_HEADS, T, HEAD_DIM - 1), BF16)], axis=2)
    o_heads, lse = attn_fwd(qh, kh, v_aug, c_heads.reshape(N_HEADS, n_at, T // n_at))
    attn = _from_heads(o_heads)
    s_perm = _permute_time(s_in)
    y_perm, xs = ssm_fwd(s_perm, wb16, cbd16, ar, ai, alr, ali, dvec)
    ypre = _unpermute_time(y_perm)
    h2, mixed = mixout_fwd(h1, attn, ypre, g_attn_out, g_ssm_out, full["w_glu"], b_glu, full["w_out"])
    h3, a1b, a3b, n2 = ffn_fwd(h2, g_ffn2, full["w1_b"], full["w3_b"], full["w2_b"], "ffn_b_fwd")

    dh3, n3, dzg, dpp, loss_part, dg_ple, dg_final = head_fwd_bwd(
        h3, p2, tgt, g_ple, g_final.reshape(1, D_MODEL), full["w_ple_gate"], full["w_ple_proj"])
    grads = {"g_ple": dg_ple, "g_final": dg_final.reshape(D_MODEL)}
    grads["w_ple_gate"] = mm_tn(n3, dzg, "dw_ple_gate")
    grads["w_ple_proj"] = mm_tn(p2, dpp, "dw_ple_proj")

    dh2, da1, da3, act, grads["g_ffn2"] = ffn_bwd(h2, g_ffn2, dh3, a1b, a3b, full["w1_b"], full["w3_b"], full["w2_b"], "ffn_b_bwd")
    grads["w1_b"] = mm_tn(n2, da1, "dw1_b")
    grads["w3_b"] = mm_tn(n2, da3, "dw3_b")
    grads["w2_b"] = mm_tn(act, dh3, "dw2_b", scale=0.5)

    seg = (jnp.arange(ATTN_W)[:, None] // HEAD_DIM == jnp.arange(128)[None, :]).astype(F32)
    dattn, dypre, dpre, yg, delta, grads["g_attn_out"], grads["g_ssm_out"], grads["b_glu"] = mixout_bwd(
        dh2, attn, ypre, g_attn_out, g_ssm_out, full["w_glu"], b_glu, full["w_out"], seg)
    grads["w_out"] = mm_tn(mixed, dh2, "dw_out")
    grads["w_glu"] = mm_tn(yg, dpre, "dw_glu")

    as_rows = lambda a: a.reshape(N_HEADS, n_at, T // n_at)
    dqh, dkh, dvh, dck, dcq = attn_bwd(qh, kh, vh, _to_heads(dattn).astype(BF16), as_rows(lse), as_rows(delta[:, :N_HEADS].T),
                                       c_heads.reshape(N_HEADS, T, 1))
    dc = jnp.pad((dck.reshape(N_HEADS, T) + dcq.reshape(N_HEADS, T)).T, ((0, 0), (0, 128 - N_HEADS)))

    dy_perm = _permute_time(dypre)
    du_perm, gs, d_a, dd = ssm_bwd(dy_perm, s_perm, xs, cbd16.T, wb16.T, ar, ai, alr, ali, dvec)
    ds_in = _unpermute_time(du_perm)
    d_wb = mm_tn(s_perm, gs, "dw_ssm_in")
    d_cbd = mm_tn(xs, dy_perm, "dw_ssm_out")
    diag_in = lambda m: jnp.einsum("ghgp->ghp", m.reshape(N_GROUPS, GROUP_CH, N_GROUPS, N_STATE)).transpose(0, 2, 1)
    diag_out = lambda m: jnp.einsum("gpgh->gph", m.reshape(N_GROUPS, N_STATE, N_GROUPS, GROUP_CH)).transpose(0, 2, 1)
    d_abar_r = jnp.sum(d_a[:, :STATE_W], axis=0).reshape(N_GROUPS, N_STATE)
    d_abar_i = jnp.sum(d_a[:, STATE_W:], axis=0).reshape(N_GROUPS, N_STATE)
    d_disc = disc_vjp((d_abar_r, d_abar_i, diag_in(d_wb[:, :STATE_W]), diag_in(d_wb[:, STATE_W:])))
    for name, val in zip(("a_re", "a_im", "log_dt", "b_re", "b_im"), d_disc):
        grads[name] = val[None]
    grads["c_re"] = diag_out(d_cbd[:STATE_W])[None]
    grads["c_im"] = -diag_out(d_cbd[STATE_W:])[None]
    grads["d_skip"] = dd.reshape(1, N_GROUPS, GROUP_CH)

    dh1, dz, grads["g_mix"], dbf = mixin_bwd(dh2, h1, g_mix, w_in_r, _from_heads(dqh), _from_heads(dkh), _from_heads(dvh),
                                             ds_in, dc, fz)
    grads["b_f"] = dbf[:, :N_HEADS]
    d_w_in_r = mm_tn(u, dz, "dw_in")
    grads["w_in"] = jnp.concatenate([d_w_in_r[:, :ATTN_W] * QK_SCALE, d_w_in_r[:, ATTN_W:3 * ATTN_W],
                                     d_w_in_r[:, 3 * ATTN_W + SSM_W:3 * ATTN_W + SSM_W + N_HEADS],
                                     d_w_in_r[:, 3 * ATTN_W:3 * ATTN_W + SSM_W]], axis=1)

    dx, da1, da3, act, grads["g_ffn1"] = ffn_bwd(x2, g_ffn1, dh1, a1a, a3a, full["w1_a"], full["w3_a"], full["w2_a"], "ffn_a_bwd")
    grads["w1_a"] = mm_tn(n1, da1, "dw1_a")
    grads["w3_a"] = mm_tn(n1, da3, "dw3_a")
    grads["w2_a"] = mm_tn(act, dh1, "dw2_a", scale=0.5)
    return loss_part, dx, grads


def _reduce_and_update(weights, moms, vars_, loss_part, dx, grads):
    core = lax.axis_index("c").astype(jnp.int32).reshape(1)
    chip = (2 * lax.axis_index("x") + lax.axis_index("y")).astype(jnp.int32).reshape(1)
    packed = _pack_full_grads(grads)
    pair = pair_sum(packed, sibling_split(packed), core)
    half = chip_sum(pair, chip_exchange(pair), chip)
    g_out = _unpack_shards(join_halves(half, sibling_swap(half), core))
    d_out, m_out, v_out = {}, {}, {}
    for n, _, _ in BIG:
        d, m, v = adamw(weights[n][0], g_out[n][0], moms[n][0], vars_[n][0], "adamw_" + n)
        d_out[n], m_out[n], v_out[n] = d[None], m[None], v[None]

    small = allreduce_small(_pack_small({n: grads[n] for n, _ in SMALL}, extra=loss_part[0, 0]))
    d_small, m_small, v_small = adamw(_pack_small(weights), small, _pack_small(moms), _pack_small(vars_), "adamw_small")

    g_small, loss = _unpack_small(small)
    g_out.update(g_small)
    outs = []
    for big, sm in ((d_out, d_small), (m_out, m_small), (v_out, v_small)):
        o, _ = _unpack_small(sm)
        o.update(big)
        outs.append(o)
    result = [loss, dx[None]] + [g_out[n] for n in WEIGHT_ORDER]
    for o in outs:
        result += [o[n] for n in WEIGHT_ORDER]
    return tuple(result)
```

```python
import functools
import math

import jax
import jax.numpy as jnp
from jax import lax
from jax.experimental import pallas as pl
from jax.experimental.pallas import tpu as pltpu

F32 = jnp.float32
BF16 = jnp.bfloat16

D_MODEL = 1024
D_FF = 2816
N_HEADS = 8
HEAD_DIM = 64
ATTN_W = 512
SSM_W = 512
N_GROUPS = 32
N_STATE = 64
GROUP_CH = 16
STATE_W = N_GROUPS * N_STATE
Z_COLS = 2176
QK_SCALE = 0.125
EPS = 1e-6

ADAM_LR = 0.001
ADAM_B1 = 0.9
ADAM_B2 = 0.999
ADAM_EPS = 1e-08
ADAM_WD = 0.01
ADAM_STEP = 10

TOKEN_TILE = 512
FFN_TOKEN_TILE = 256
FF_CHUNK = 1408
MM_K_TILE = 2048
ATTN_TILE = 512
SCAN_STEPS = 32
SCAN_LANES = 512
VMEM_LIMIT = 48 * 1024 * 1024
FFN_VMEM_LIMIT = 56 * 1024 * 1024
COPY_CHUNKS = 4

NT_DIMS = (((1,), (1,)), ((), ()))
TN_DIMS = (((0,), (0,)), ((), ()))
HIGHEST = lax.Precision.HIGHEST
MESH = pl.DeviceIdType.MESH

BIG = (
    ("w1_a", (1024, 704), 1), ("w3_a", (1024, 704), 1), ("w2_a", (704, 1024), 0),
    ("w_in", (1024, 514), 1), ("w_glu", (128, 512), 0), ("w_out", (256, 1024), 0),
    ("w1_b", (1024, 704), 1), ("w3_b", (1024, 704), 1), ("w2_b", (704, 1024), 0),
    ("w_ple_gate", (256, 1024), 0), ("w_ple_proj", (256, 256), 1),
)
PACK_COLS = 1024
PACK_ALIGN = 16
PACK_ROWS = 5408
SMALL = (
    ("g_ffn1", (1, 1024)), ("g_mix", (1, 1024)), ("b_f", (1, 8)), ("a_re", (1, 32, 64)), ("a_im", (1, 32, 64)),
    ("log_dt", (1, 32)), ("b_re", (1, 32, 64, 16)), ("b_im", (1, 32, 64, 16)), ("c_re", (1, 32, 16, 64)),
    ("c_im", (1, 32, 16, 64)), ("d_skip", (1, 32, 16)), ("b_glu", (1, 512)), ("g_attn_out", (1, 512)),
    ("g_ssm_out", (1, 512)), ("g_ffn2", (1, 1024)), ("g_ple", (1, 1024)), ("g_final", (1024,)),
)
SMALL_ROWS = 1120
WEIGHT_ORDER = ("g_ffn1", "w1_a", "w3_a", "w2_a", "g_mix", "w_in", "b_f", "a_re", "a_im", "log_dt", "b_re", "b_im",
                "c_re", "c_im", "d_skip", "w_glu", "b_glu", "g_attn_out", "g_ssm_out", "w_out", "g_ffn2", "w1_b",
                "w3_b", "w2_b", "g_ple", "w_ple_gate", "w_ple_proj", "g_final")


def _params(sem=None, vmem=VMEM_LIMIT):
    kw = dict(vmem_limit_bytes=vmem)
    if sem is not None:
        kw["dimension_semantics"] = sem
    return pltpu.CompilerParams(**kw)


def _sds(shape, dtype):
    return jax.ShapeDtypeStruct(shape, dtype)


def _tile(n, pref):
    t = min(n, pref)
    assert n % t == 0, (n, pref)
    return t


def _rms_scale(x):
    return lax.rsqrt(jnp.mean(x * x, axis=-1, keepdims=True) + EPS)


def _rms_bwd(dy, x, g):
    r = _rms_scale(x)
    xh = x * r
    dxh = dy * g
    dx = r * (dxh - xh * jnp.mean(dxh * xh, axis=-1, keepdims=True))
    return dx, jnp.sum(dy * xh, axis=0, keepdims=True)


def _dot(a, b):
    return jnp.dot(a, b, preferred_element_type=F32)


def _dot_nt(a, b):
    return lax.dot_general(a, b, NT_DIMS, preferred_element_type=F32)


def _dot_tn(a, b):
    return lax.dot_general(a, b, TN_DIMS, preferred_element_type=F32)


_GELU_C = math.sqrt(2.0 / math.pi)


def _gelu_parts(x):
    t = jnp.tanh(_GELU_C * (x + 0.044715 * x * x * x))
    return 0.5 * x * (1.0 + t), t


def _gelu_grad(x, t):
    return 0.5 * (1.0 + t) + 0.5 * x * (1.0 - t * t) * _GELU_C * (1.0 + 3.0 * 0.044715 * x * x)


def _resident(shape):
    return pl.BlockSpec(shape, lambda i: (0,) * len(shape), pipeline_mode=pl.Buffered(1))


def ffn_fwd(h, g, w1, w3, w2, name):
    T = h.shape[0]
    tm = _tile(T, FFN_TOKEN_TILE)

    def body(h_ref, g_ref, w1_ref, w3_ref, w2_ref, ho_ref, a1_ref, a3_ref, n_ref):
        x = h_ref[...]
        n = (x * _rms_scale(x) * g_ref[...]).astype(BF16)
        n_ref[...] = n
        out = x
        for lo in range(0, D_FF, FF_CHUNK):
            cols = slice(lo, lo + FF_CHUNK)
            a1 = _dot(n, w1_ref[:, cols])
            a3 = _dot(n, w3_ref[:, cols])
            a1_ref[:, cols] = a1.astype(BF16)
            a3_ref[:, cols] = a3.astype(BF16)
            act = (a1 * jax.nn.sigmoid(a1) * a3).astype(BF16)
            out = out + 0.5 * _dot(act, w2_ref[cols, :])
        ho_ref[...] = out

    tok = lambda i: (i, 0)
    return pl.pallas_call(
        body, name=name, grid=(T // tm,),
        in_specs=[pl.BlockSpec((tm, D_MODEL), tok), _resident((1, D_MODEL)), _resident((D_MODEL, D_FF)),
                  _resident((D_MODEL, D_FF)), _resident((D_FF, D_MODEL))],
        out_specs=[pl.BlockSpec((tm, D_MODEL), tok), pl.BlockSpec((tm, D_FF), tok), pl.BlockSpec((tm, D_FF), tok),
                   pl.BlockSpec((tm, D_MODEL), tok)],
        out_shape=[_sds((T, D_MODEL), F32), _sds((T, D_FF), BF16), _sds((T, D_FF), BF16), _sds((T, D_MODEL), BF16)],
        compiler_params=_params(("arbitrary",), FFN_VMEM_LIMIT),
    )(h, g, w1, w3, w2)


def ffn_bwd(h, g, dho, a1, a3, w1, w3, w2, name):
    T = h.shape[0]
    tm = _tile(T, FFN_TOKEN_TILE)

    def body(h_ref, g_ref, dho_ref, a1_ref, a3_ref, w1_ref, w3_ref, w2_ref, dhi_ref, da1_ref, da3_ref, act_ref, dg_ref):
        @pl.when(pl.program_id(0) == 0)
        def _():
            dg_ref[...] = jnp.zeros_like(dg_ref)

        dho = dho_ref[...]
        dhb = (0.5 * dho).astype(BF16)
        dn = None
        for lo in range(0, D_FF, FF_CHUNK):
            cols = slice(lo, lo + FF_CHUNK)
            a1v = a1_ref[:, cols].astype(F32)
            a3v = a3_ref[:, cols].astype(F32)
            s = jax.nn.sigmoid(a1v)
            sl = a1v * s
            dact = _dot_nt(dhb, w2_ref[cols, :])
            act_ref[:, cols] = (sl * a3v).astype(BF16)
            da1 = (dact * a3v * s * (1.0 + a1v * (1.0 - s))).astype(BF16)
            da3 = (dact * sl).astype(BF16)
            da1_ref[:, cols] = da1
            da3_ref[:, cols] = da3
            part = _dot_nt(da1, w1_ref[:, cols]) + _dot_nt(da3, w3_ref[:, cols])
            dn = part if dn is None else dn + part
        dx, dg = _rms_bwd(dn, h_ref[...], g_ref[...])
        dg_ref[...] += dg
        dhi_ref[...] = dho + dx

    tok = lambda i: (i, 0)
    return pl.pallas_call(
        body, name=name, grid=(T // tm,),
        in_specs=[pl.BlockSpec((tm, D_MODEL), tok), _resident((1, D_MODEL)), pl.BlockSpec((tm, D_MODEL), tok),
                  pl.BlockSpec((tm, D_FF), tok), pl.BlockSpec((tm, D_FF), tok), _resident((D_MODEL, D_FF)),
                  _resident((D_MODEL, D_FF)), _resident((D_FF, D_MODEL))],
        out_specs=[pl.BlockSpec((tm, D_MODEL), tok), pl.BlockSpec((tm, D_FF), tok), pl.BlockSpec((tm, D_FF), tok),
                   pl.BlockSpec((tm, D_FF), tok), pl.BlockSpec((1, D_MODEL), lambda i: (0, 0))],
        out_shape=[_sds((T, D_MODEL), F32), _sds((T, D_FF), BF16), _sds((T, D_FF), BF16), _sds((T, D_FF), BF16),
                   _sds((1, D_MODEL), F32)],
        compiler_params=_params(("arbitrary",), FFN_VMEM_LIMIT),
    )(h, g, dho, a1, a3, w1, w3, w2)


def mm_tn(a, b, name, scale=1.0):
    T, M = a.shape
    N = b.shape[1]
    bm = 512 if M % 512 == 0 else (1408 if M == 2816 else 256)
    bn = N if N in (2176, 1408) else (1408 if N == 2816 else min(N, 1024))
    tk = _tile(T, MM_K_TILE)
    row_bytes = 2 * (bm * a.dtype.itemsize + bn * b.dtype.itemsize)
    while tk > TOKEN_TILE and tk * row_bytes > VMEM_LIMIT // 3:
        tk //= 2
    assert M % bm == 0 and N % bn == 0 and T % tk == 0
    n_k = T // tk

    def body(a_ref, b_ref, o_ref):
        k = pl.program_id(2)

        @pl.when(k == 0)
        def _():
            o_ref[...] = jnp.zeros_like(o_ref)

        o_ref[...] += _dot_tn(a_ref[...].astype(BF16), b_ref[...].astype(BF16))

        if scale != 1.0:
            @pl.when(k == n_k - 1)
            def _():
                o_ref[...] = o_ref[...] * scale

    return pl.pallas_call(
        body, name=name, grid=(M // bm, N // bn, n_k),
        in_specs=[pl.BlockSpec((tk, bm), lambda m, n, k: (k, m)), pl.BlockSpec((tk, bn), lambda m, n, k: (k, n))],
        out_specs=pl.BlockSpec((bm, bn), lambda m, n, k: (m, n)),
        out_shape=_sds((M, N), F32),
        compiler_params=_params(("arbitrary", "arbitrary", "arbitrary")),
    )(a, b)


def mixin_fwd(h1, g, w_in_r, b_f_pad):
    T = h1.shape[0]
    tm = _tile(T, TOKEN_TILE)

    def body(h_ref, g_ref, w_ref, bf_ref, u_ref, qkv_ref, s_ref, fz_ref, c_ref, carry):
        @pl.when(pl.program_id(0) == 0)
        def _():
            carry[...] = jnp.zeros_like(carry)

        x = h_ref[...]
        u = (x * _rms_scale(x) * g_ref[...]).astype(BF16)
        u_ref[...] = u
        z = _dot(u, w_ref[...])
        qkv_ref[...] = z[:, :3 * ATTN_W].astype(BF16)
        s_ref[...] = z[:, 3 * ATTN_W:3 * ATTN_W + SSM_W]
        fz = z[:, 3 * ATTN_W + SSM_W:] + bf_ref[...]
        fz_ref[...] = fz
        lane = lax.broadcasted_iota(jnp.int32, fz.shape, 1)
        logf = jnp.where(lane < N_HEADS, jnp.minimum(fz, 0.0) - jnp.log(1.0 + jnp.exp(-jnp.abs(fz))), 0.0)
        row = lax.broadcasted_iota(jnp.int32, (tm, tm), 0)
        col = lax.broadcasted_iota(jnp.int32, (tm, tm), 1)
        tri = (col <= row).astype(F32)
        cs = jnp.dot(tri, logf, precision=HIGHEST, preferred_element_type=F32) + carry[0:1, :]
        c_ref[...] = cs
        carry[...] = jnp.broadcast_to(cs[tm - 1:tm, :], carry.shape)

    tok = lambda i: (i, 0)
    fix = lambda i: (0, 0)
    return pl.pallas_call(
        body, name="mixin_fwd", grid=(T // tm,),
        in_specs=[pl.BlockSpec((tm, D_MODEL), tok), pl.BlockSpec((1, D_MODEL), fix),
                  pl.BlockSpec((D_MODEL, Z_COLS), fix), pl.BlockSpec((1, 128), fix)],
        out_specs=[pl.BlockSpec((tm, D_MODEL), tok), pl.BlockSpec((tm, 3 * ATTN_W), tok), pl.BlockSpec((tm, SSM_W), tok),
                   pl.BlockSpec((tm, 128), tok), pl.BlockSpec((tm, 128), tok)],
        out_shape=[_sds((T, D_MODEL), BF16), _sds((T, 3 * ATTN_W), BF16), _sds((T, SSM_W), F32),
                   _sds((T, 128), F32), _sds((T, 128), F32)],
        scratch_shapes=[pltpu.VMEM((8, 128), F32)],
        compiler_params=_params(("arbitrary",)),
    )(h1, g, w_in_r, b_f_pad)


def mixin_bwd(dh2, h1, g, w_in_r, dq, dk, dv, ds, dc, fz):
    T = h1.shape[0]
    tm = _tile(T, TOKEN_TILE)
    n_t = T // tm

    def body(dh2_ref, h_ref, g_ref, w_ref, dq_ref, dk_ref, dv_ref, ds_ref, dc_ref, fz_ref,
             dh1_ref, dz_ref, dg_ref, dbf_ref, carry):
        @pl.when(pl.program_id(0) == 0)
        def _():
            carry[...] = jnp.zeros_like(carry)
            dg_ref[...] = jnp.zeros_like(dg_ref)
            dbf_ref[...] = jnp.zeros_like(dbf_ref)

        row = lax.broadcasted_iota(jnp.int32, (tm, tm), 0)
        col = lax.broadcasted_iota(jnp.int32, (tm, tm), 1)
        tri = (col >= row).astype(F32)
        dlogf = jnp.dot(tri, dc_ref[...], precision=HIGHEST, preferred_element_type=F32) + carry[0:1, :]
        carry[...] = jnp.broadcast_to(dlogf[0:1, :], carry.shape)
        dfz = dlogf * jax.nn.sigmoid(-fz_ref[...])
        dbf_ref[...] += jnp.sum(dfz, axis=0, keepdims=True)
        dz = jnp.concatenate([dq_ref[...], dk_ref[...], dv_ref[...], ds_ref[...], dfz], axis=1).astype(BF16)
        dz_ref[...] = dz
        du = _dot_nt(dz, w_ref[...])
        dx, dg = _rms_bwd(du, h_ref[...], g_ref[...])
        dg_ref[...] += dg
        dh1_ref[...] = dh2_ref[...] + dx

    tok = lambda i: (n_t - 1 - i, 0)
    fix = lambda i: (0, 0)
    return pl.pallas_call(
        body, name="mixin_bwd", grid=(n_t,),
        in_specs=[pl.BlockSpec((tm, D_MODEL), tok), pl.BlockSpec((tm, D_MODEL), tok), pl.BlockSpec((1, D_MODEL), fix),
                  pl.BlockSpec((D_MODEL, Z_COLS), fix), pl.BlockSpec((tm, ATTN_W), tok), pl.BlockSpec((tm, ATTN_W), tok),
                  pl.BlockSpec((tm, ATTN_W), tok), pl.BlockSpec((tm, SSM_W), tok), pl.BlockSpec((tm, 128), tok),
                  pl.BlockSpec((tm, 128), tok)],
        out_specs=[pl.BlockSpec((tm, D_MODEL), tok), pl.BlockSpec((tm, Z_COLS), tok), pl.BlockSpec((1, D_MODEL), fix),
                   pl.BlockSpec((1, 128), fix)],
        out_shape=[_sds((T, D_MODEL), F32), _sds((T, Z_COLS), BF16), _sds((1, D_MODEL), F32), _sds((1, 128), F32)],
        scratch_shapes=[pltpu.VMEM((8, 128), F32)],
        compiler_params=_params(("arbitrary",)),
    )(dh2, h1, g, w_in_r, dq, dk, dv, ds, dc, fz)


def attn_fwd(q_aug, k_aug, v_aug):
    H, T, wd = q_aug.shape
    hd = HEAD_DIM
    tq = _tile(T, ATTN_TILE)
    n = T // tq

    def body(q_ref, k_ref, v_ref, o_ref, lse_ref, m_sc, acc):
        qi = pl.program_id(1)
        qv = q_ref[0]
        m_sc[...] = jnp.full_like(m_sc, -jnp.inf)
        acc[...] = jnp.zeros_like(acc)

        def tile(j, masked):
            rows = pl.ds(pl.multiple_of(j * tq, tq), tq)
            st = _dot_nt(k_ref[0, rows, :], qv)
            if masked:
                keep = lax.broadcasted_iota(jnp.int32, (tq, tq), 0) <= lax.broadcasted_iota(jnp.int32, (tq, tq), 1)
                st = jnp.where(keep, st, -1e30)
            m_old = m_sc[...]
            m_new = jnp.maximum(m_old, jnp.max(st, axis=0, keepdims=True))
            pt = jnp.exp(st - m_new).astype(BF16)
            acc[...] = jnp.exp(m_old - m_new) * acc[...] + _dot_tn(v_ref[0, rows, :], pt)
            m_sc[...] = m_new

        def off_diagonal(j, carry):
            tile(j, False)
            return carry

        lax.fori_loop(0, qi, off_diagonal, 0)
        tile(qi, True)
        total = acc[hd:hd + 1, :]
        lse_ref[0, pl.ds(qi, 1), :] = m_sc[...] + jnp.log(total)
        o_ref[0] = (acc[...] / total).T[:, :hd]

    qmap = lambda h, i: (h, i, 0)
    head = lambda h, i: (h, 0, 0)
    return pl.pallas_call(
        body, name="attn_fwd", grid=(H, n),
        in_specs=[pl.BlockSpec((1, tq, wd), qmap), pl.BlockSpec((1, T, wd), head), pl.BlockSpec((1, T, wd), head)],
        out_specs=[pl.BlockSpec((1, tq, hd), qmap), pl.BlockSpec((1, n, tq), head)],
        out_shape=[_sds((H, T, hd), F32), _sds((H, n, tq), F32)],
        scratch_shapes=[pltpu.VMEM((1, tq), F32), pltpu.VMEM((wd, tq), F32)],
        compiler_params=_params(("arbitrary", "arbitrary")),
    )(q_aug, k_aug, v_aug)


def attn_bwd(q_aug, k_aug, v_aug, do_aug):
    H, T, wd = q_aug.shape
    tq = _tile(T, ATTN_TILE)
    n = T // tq

    def body(q_ref, do_ref, k_ref, v_ref, dq_ref, dk_ref, dv_ref, dck_ref, dcq_ref):
        j = pl.program_id(1)

        @pl.when(j == 0)
        def _():
            dq_ref[...] = jnp.zeros_like(dq_ref)
            dcq_ref[...] = jnp.zeros_like(dcq_ref)

        dk_ref[...] = jnp.zeros_like(dk_ref)
        dv_ref[...] = jnp.zeros_like(dv_ref)
        dck_ref[...] = jnp.zeros_like(dck_ref)
        kv, vv = k_ref[0], v_ref[0]

        def tile(i, masked):
            rows = pl.ds(pl.multiple_of(i * tq, tq), tq)
            qv, dov = q_ref[0, rows, :], do_ref[0, rows, :]
            pt = jnp.exp(_dot_nt(kv, qv))
            if masked:
                keep = lax.broadcasted_iota(jnp.int32, (tq, tq), 0) <= lax.broadcasted_iota(jnp.int32, (tq, tq), 1)
                pt = jnp.where(keep, pt, 0.0)
            dv_ref[0] += _dot(pt.astype(BF16), dov)
            dst = pt * _dot_nt(vv, dov)
            dsb = dst.astype(BF16)
            dk_ref[0] += _dot(dsb, qv)
            dq_ref[0, rows, :] += _dot_tn(dsb, kv)
            dck_ref[0] += -jnp.sum(dst, axis=1, keepdims=True)
            dcq_ref[0, pl.ds(i, 1), :] += jnp.sum(dst, axis=0, keepdims=True)

        def off_diagonal(i, carry):
            tile(i, False)
            return carry

        tile(j, True)
        lax.fori_loop(j + 1, n, off_diagonal, 0)

    head = lambda h, j: (h, 0, 0)
    kmap = lambda h, j: (h, j, 0)
    return pl.pallas_call(
        body, name="attn_bwd", grid=(H, n),
        in_specs=[pl.BlockSpec((1, T, wd), head), pl.BlockSpec((1, T, wd), head), pl.BlockSpec((1, tq, wd), kmap),
                  pl.BlockSpec((1, tq, wd), kmap)],
        out_specs=[pl.BlockSpec((1, T, wd), head), pl.BlockSpec((1, tq, wd), kmap), pl.BlockSpec((1, tq, wd), kmap),
                   pl.BlockSpec((1, tq, 1), kmap), pl.BlockSpec((1, n, tq), head)],
        out_shape=[_sds((H, T, wd), F32), _sds((H, T, wd), F32), _sds((H, T, wd), F32), _sds((H, T, 1), F32),
                   _sds((H, n, tq), F32)],
        compiler_params=_params(("arbitrary", "arbitrary")),
    )(q_aug, do_aug, k_aug, v_aug)


def _complex_step(a_r, a_i, cr, ci, br, bi):
    return a_r * cr - a_i * ci + br, a_r * ci + a_i * cr + bi


def ssm_fwd(s_perm, wb, cbd, a_r, a_i, al_r, al_i, dvec):
    T = s_perm.shape[0]
    chunk = T // 8
    ts = _tile(chunk, SCAN_STEPS)
    tr, n_s = ts * 8, chunk // ts
    W, LB = STATE_W, SCAN_LANES

    def body(s_ref, wb_ref, cbd_ref, ar_ref, ai_ref, alr_ref, ali_ref, dv_ref, y_ref, xs_ref, bu, carry):
        ph, i = pl.program_id(0), pl.program_id(1)

        @pl.when((ph == 0) & (i == 0))
        def _():
            carry[...] = jnp.zeros_like(carry)

        bu[...] = _dot(s_ref[...].astype(BF16), wb_ref[...])

        def scan(store):
            for lb in range(W // LB):
                lo = lb * LB
                re, im = slice(lo, lo + LB), slice(W + lo, W + lo + LB)
                ar = jnp.broadcast_to(ar_ref[:, re], (8, LB))
                ai = jnp.broadcast_to(ai_ref[:, re], (8, LB))

                def step(s, c):
                    rows = pl.ds(pl.multiple_of(s * 8, 8), 8)
                    nr, ni = _complex_step(ar, ai, c[0], c[1], bu[rows, re], bu[rows, im])
                    if store:
                        bu[rows, re] = nr
                        bu[rows, im] = ni
                    return nr, ni

                cr, ci = lax.fori_loop(0, ts, step, (carry[:, re], carry[:, im]), unroll=2)
                carry[:, re] = cr
                carry[:, im] = ci

        @pl.when(ph == 0)
        def _():
            scan(False)

            @pl.when(i == n_s - 1)
            def _():
                er, ei = carry[:, :W], carry[:, W:]
                alr = jnp.broadcast_to(alr_ref[...], (8, W))
                ali = jnp.broadcast_to(ali_ref[...], (8, W))
                first = lax.broadcasted_iota(jnp.int32, (8, W), 0) == 0
                sr, si = jnp.zeros((8, W), F32), jnp.zeros((8, W), F32)
                for _ in range(7):
                    vr, vi = _complex_step(alr, ali, sr, si, er, ei)
                    sr = jnp.where(first, 0.0, pltpu.roll(vr, 1, 0))
                    si = jnp.where(first, 0.0, pltpu.roll(vi, 1, 0))
                carry[:, :W] = sr
                carry[:, W:] = si

        @pl.when(ph == 1)
        def _():
            scan(True)
            xb = bu[...].astype(BF16)
            xs_ref[...] = xb
            y_ref[...] = _dot(xb, cbd_ref[...]) + s_ref[...] * dv_ref[...]

    fix = lambda p, i: (0, 0)
    return pl.pallas_call(
        body, name="ssm_fwd", grid=(2, n_s),
        in_specs=[pl.BlockSpec((tr, SSM_W), lambda p, i: (i, 0)), pl.BlockSpec((SSM_W, 2 * W), fix),
                  pl.BlockSpec((2 * W, SSM_W), fix), pl.BlockSpec((1, W), fix), pl.BlockSpec((1, W), fix),
                  pl.BlockSpec((1, W), fix), pl.BlockSpec((1, W), fix), pl.BlockSpec((1, SSM_W), fix)],
        out_specs=[pl.BlockSpec((tr, SSM_W), lambda p, i: (i * p, 0)), pl.BlockSpec((tr, 2 * W), lambda p, i: (i * p, 0))],
        out_shape=[_sds((T, SSM_W), F32), _sds((T, 2 * W), BF16)],
        scratch_shapes=[pltpu.VMEM((tr, 2 * W), F32), pltpu.VMEM((8, 2 * W), F32)],
        compiler_params=_params(("arbitrary", "arbitrary")),
    )(s_perm, wb, cbd, a_r, a_i, al_r, al_i, dvec)


def ssm_bwd(dy_perm, s_perm, xs, cbd_t, wb_t, a_r, a_i, al_r, al_i, dvec):
    T = s_perm.shape[0]
    chunk = T // 8
    ts = _tile(chunk, SCAN_STEPS)
    tr, n_s = ts * 8, chunk // ts
    W, LB = STATE_W, SCAN_LANES

    def body(dy_ref, s_ref, xs_ref, cbt_ref, wbt_ref, ar_ref, ai_ref, alr_ref, ali_ref, dv_ref,
             du_ref, gs_ref, da_ref, dd_ref, gd, x32, carry):
        ph, i = pl.program_id(0), pl.program_id(1)

        @pl.when((ph == 0) & (i == 0))
        def _():
            carry[...] = jnp.zeros_like(carry)
            da_ref[...] = jnp.zeros_like(da_ref)
            dd_ref[...] = jnp.zeros_like(dd_ref)

        gd[...] = _dot(dy_ref[...].astype(BF16), cbt_ref[...])

        def scan(store):
            for lb in range(W // LB):
                lo = lb * LB
                re, im = slice(lo, lo + LB), slice(W + lo, W + lo + LB)
                ar = jnp.broadcast_to(ar_ref[:, re], (8, LB))
                nai = -jnp.broadcast_to(ai_ref[:, re], (8, LB))

                def step(k, c):
                    rows = pl.ds(pl.multiple_of((ts - 1 - k) * 8, 8), 8)
                    cr, ci = c[0], c[1]
                    nr, ni = _complex_step(ar, nai, cr, ci, gd[rows, re], gd[rows, im])
                    if store:
                        xr, xi = x32[rows, re], x32[rows, im]
                        gd[rows, re] = nr
                        gd[rows, im] = ni
                        return nr, ni, c[2] + cr * xr + ci * xi, c[3] + ci * xr - cr * xi
                    return nr, ni

                init = (carry[:, re], carry[:, im])
                if store:
                    init = init + (da_ref[:, re], da_ref[:, im])
                out = lax.fori_loop(0, ts, step, init, unroll=2)
                carry[:, re] = out[0]
                carry[:, im] = out[1]
                if store:
                    da_ref[:, re] = out[2]
                    da_ref[:, im] = out[3]

        @pl.when(ph == 0)
        def _():
            scan(False)

            @pl.when(i == n_s - 1)
            def _():
                er, ei = carry[:, :W], carry[:, W:]
                alr = jnp.broadcast_to(alr_ref[...], (8, W))
                nali = -jnp.broadcast_to(ali_ref[...], (8, W))
                last = lax.broadcasted_iota(jnp.int32, (8, W), 0) == 7
                rr, ri = jnp.zeros((8, W), F32), jnp.zeros((8, W), F32)
                for _ in range(7):
                    vr, vi = _complex_step(alr, nali, rr, ri, er, ei)
                    rr = jnp.where(last, 0.0, pltpu.roll(vr, 7, 0))
                    ri = jnp.where(last, 0.0, pltpu.roll(vi, 7, 0))
                carry[:, :W] = rr
                carry[:, W:] = ri

        @pl.when(ph == 1)
        def _():
            x32[...] = xs_ref[...].astype(F32)
            scan(True)
            gb = gd[...].astype(BF16)
            gs_ref[...] = gb
            dy = dy_ref[...]
            du_ref[...] = _dot(gb, wbt_ref[...]) + dy * dv_ref[...]
            dd_ref[...] += jnp.sum(dy * s_ref[...], axis=0, keepdims=True)

    fix = lambda p, i: (0, 0)
    rev = lambda p, i: (n_s - 1 - i, 0)
    rev_out = lambda p, i: (n_s - 1 - i * p, 0)
    return pl.pallas_call(
        body, name="ssm_bwd", grid=(2, n_s),
        in_specs=[pl.BlockSpec((tr, SSM_W), rev), pl.BlockSpec((tr, SSM_W), rev), pl.BlockSpec((tr, 2 * W), rev),
                  pl.BlockSpec((SSM_W, 2 * W), fix), pl.BlockSpec((2 * W, SSM_W), fix), pl.BlockSpec((1, W), fix),
                  pl.BlockSpec((1, W), fix), pl.BlockSpec((1, W), fix), pl.BlockSpec((1, W), fix),
                  pl.BlockSpec((1, SSM_W), fix)],
        out_specs=[pl.BlockSpec((tr, SSM_W), rev_out), pl.BlockSpec((tr, 2 * W), rev_out),
                   pl.BlockSpec((8, 2 * W), fix), pl.BlockSpec((1, SSM_W), fix)],
        out_shape=[_sds((T, SSM_W), F32), _sds((T, 2 * W), BF16), _sds((8, 2 * W), F32), _sds((1, SSM_W), F32)],
        scratch_shapes=[pltpu.VMEM((tr, 2 * W), F32), pltpu.VMEM((tr, 2 * W), F32), pltpu.VMEM((8, 2 * W), F32)],
        compiler_params=_params(("arbitrary", "arbitrary")),
    )(dy_perm, s_perm, xs, cbd_t, wb_t, a_r, a_i, al_r, al_i, dvec)


def mixout_fwd(h1, attn, ypre, g_a, g_s, w_glu, b_glu, w_out):
    T = h1.shape[0]
    tm = _tile(T, TOKEN_TILE)

    def body(h_ref, at_ref, yp_ref, ga_ref, gs_ref, wg_ref, bg_ref, wo_ref, h2_ref, mixed_ref):
        yg, _ = _gelu_parts(yp_ref[...])
        gl = yg * jax.nn.sigmoid(_dot(yg.astype(BF16), wg_ref[...]) + bg_ref[...])
        at = at_ref[...]
        mixed = jnp.concatenate([at * _rms_scale(at) * ga_ref[...], gl * _rms_scale(gl) * gs_ref[...]], axis=1)
        mixed = mixed.astype(BF16)
        mixed_ref[...] = mixed
        h2_ref[...] = h_ref[...] + _dot(mixed, wo_ref[...])

    tok = lambda i: (i, 0)
    fix = lambda i: (0, 0)
    return pl.pallas_call(
        body, name="mixout_fwd", grid=(T // tm,),
        in_specs=[pl.BlockSpec((tm, D_MODEL), tok), pl.BlockSpec((tm, ATTN_W), tok), pl.BlockSpec((tm, SSM_W), tok),
                  pl.BlockSpec((1, ATTN_W), fix), pl.BlockSpec((1, SSM_W), fix), pl.BlockSpec((SSM_W, SSM_W), fix),
                  pl.BlockSpec((1, SSM_W), fix), pl.BlockSpec((D_MODEL, D_MODEL), fix)],
        out_specs=[pl.BlockSpec((tm, D_MODEL), tok), pl.BlockSpec((tm, D_MODEL), tok)],
        out_shape=[_sds((T, D_MODEL), F32), _sds((T, D_MODEL), BF16)],
        compiler_params=_params(("arbitrary",)),
    )(h1, attn, ypre, g_a, g_s, w_glu, b_glu, w_out)


def mixout_bwd(dh2, attn, ypre, g_a, g_s, w_glu, b_glu, w_out, seg):
    T = dh2.shape[0]
    tm = _tile(T, TOKEN_TILE)

    def body(dh_ref, at_ref, yp_ref, ga_ref, gs_ref, wg_ref, bg_ref, wo_ref, seg_ref,
             dat_ref, dyp_ref, dpre_ref, yg_ref, dl_ref, dga_ref, dgs_ref, dbg_ref):
        @pl.when(pl.program_id(0) == 0)
        def _():
            dga_ref[...] = jnp.zeros_like(dga_ref)
            dgs_ref[...] = jnp.zeros_like(dgs_ref)
            dbg_ref[...] = jnp.zeros_like(dbg_ref)

        dmix = _dot_nt(dh_ref[...].astype(BF16), wo_ref[...])
        at = at_ref[...]
        dat, dga = _rms_bwd(dmix[:, :ATTN_W], at, ga_ref[...])
        dga_ref[...] += dga
        dat_ref[...] = dat
        dl_ref[...] = jnp.dot(dat * at, seg_ref[...], precision=HIGHEST, preferred_element_type=F32)
        yp = yp_ref[...]
        yg, t = _gelu_parts(yp)
        ygb = yg.astype(BF16)
        yg_ref[...] = ygb
        sg = jax.nn.sigmoid(_dot(ygb, wg_ref[...]) + bg_ref[...])
        dgl, dgs = _rms_bwd(dmix[:, ATTN_W:], yg * sg, gs_ref[...])
        dgs_ref[...] += dgs
        dpre = dgl * yg * sg * (1.0 - sg)
        dbg_ref[...] += jnp.sum(dpre, axis=0, keepdims=True)
        dpb = dpre.astype(BF16)
        dpre_ref[...] = dpb
        dyg = dgl * sg + _dot_nt(dpb, wg_ref[...])
        dyp_ref[...] = dyg * _gelu_grad(yp, t)

    tok = lambda i: (i, 0)
    fix = lambda i: (0, 0)
    return pl.pallas_call(
        body, name="mixout_bwd", grid=(T // tm,),
        in_specs=[pl.BlockSpec((tm, D_MODEL), tok), pl.BlockSpec((tm, ATTN_W), tok), pl.BlockSpec((tm, SSM_W), tok),
                  pl.BlockSpec((1, ATTN_W), fix), pl.BlockSpec((1, SSM_W), fix), pl.BlockSpec((SSM_W, SSM_W), fix),
                  pl.BlockSpec((1, SSM_W), fix), pl.BlockSpec((D_MODEL, D_MODEL), fix), pl.BlockSpec((ATTN_W, 128), fix)],
        out_specs=[pl.BlockSpec((tm, ATTN_W), tok), pl.BlockSpec((tm, SSM_W), tok), pl.BlockSpec((tm, SSM_W), tok),
                   pl.BlockSpec((tm, SSM_W), tok), pl.BlockSpec((tm, 128), tok), pl.BlockSpec((1, ATTN_W), fix),
                   pl.BlockSpec((1, SSM_W), fix), pl.BlockSpec((1, SSM_W), fix)],
        out_shape=[_sds((T, ATTN_W), F32), _sds((T, SSM_W), F32), _sds((T, SSM_W), BF16), _sds((T, SSM_W), BF16),
                   _sds((T, 128), F32), _sds((1, ATTN_W), F32), _sds((1, SSM_W), F32), _sds((1, SSM_W), F32)],
        compiler_params=_params(("arbitrary",)),
    )(dh2, attn, ypre, g_a, g_s, w_glu, b_glu, w_out, seg)


def head_fwd_bwd(h3, p, target, g_ple, g_final, w_gate, w_proj):
    T = h3.shape[0]
    tm = _tile(T, TOKEN_TILE)
    pd = p.shape[1]

    def body(h_ref, p_ref, tg_ref, gp_ref, gf_ref, wg_ref, wp_ref,
             dh_ref, n3_ref, dz_ref, dpp_ref, loss_ref, dgp_ref, dgf_ref):
        @pl.when(pl.program_id(0) == 0)
        def _():
            loss_ref[...] = jnp.zeros_like(loss_ref)
            dgp_ref[...] = jnp.zeros_like(dgp_ref)
            dgf_ref[...] = jnp.zeros_like(dgf_ref)

        x = h_ref[...]
        gp, gf = gp_ref[...], gf_ref[...]
        n3 = (x * _rms_scale(x) * gp).astype(BF16)
        n3_ref[...] = n3
        gate = jax.nn.sigmoid(_dot(n3, wg_ref[...]))
        pp = _dot(p_ref[...].astype(BF16), wp_ref[...])
        h4 = x + gate * pp
        y = h4 * _rms_scale(h4) * gf
        e = y - tg_ref[...]
        tile_loss = jnp.sum(jnp.sum(e * e, axis=1, keepdims=True), axis=0, keepdims=True) * (0.5 / D_MODEL)
        loss_ref[...] += jnp.broadcast_to(tile_loss, loss_ref.shape)
        dh4, dgf = _rms_bwd(e * (1.0 / D_MODEL), h4, gf)
        dgf_ref[...] += dgf
        dzg = dh4 * pp * gate * (1.0 - gate)
        dzb = dzg.astype(BF16)
        dz_ref[...] = dzb
        dpp_ref[...] = (dh4 * gate).astype(BF16)
        dx, dgp = _rms_bwd(_dot_nt(dzb, wg_ref[...]), x, gp)
        dgp_ref[...] += dgp
        dh_ref[...] = dh4 + dx

    tok = lambda i: (i, 0)
    fix = lambda i: (0, 0)
    return pl.pallas_call(
        body, name="head_fwd_bwd", grid=(T // tm,),
        in_specs=[pl.BlockSpec((tm, D_MODEL), tok), pl.BlockSpec((tm, pd), tok), pl.BlockSpec((tm, D_MODEL), tok),
                  pl.BlockSpec((1, D_MODEL), fix), pl.BlockSpec((1, D_MODEL), fix), pl.BlockSpec((D_MODEL, D_MODEL), fix),
                  pl.BlockSpec((pd, D_MODEL), fix)],
        out_specs=[pl.BlockSpec((tm, D_MODEL), tok), pl.BlockSpec((tm, D_MODEL), tok), pl.BlockSpec((tm, D_MODEL), tok),
                   pl.BlockSpec((tm, D_MODEL), tok), pl.BlockSpec((8, 128), fix), pl.BlockSpec((1, D_MODEL), fix),
                   pl.BlockSpec((1, D_MODEL), fix)],
        out_shape=[_sds((T, D_MODEL), F32), _sds((T, D_MODEL), BF16), _sds((T, D_MODEL), BF16), _sds((T, D_MODEL), BF16),
                   _sds((8, 128), F32), _sds((1, D_MODEL), F32), _sds((1, D_MODEL), F32)],
        compiler_params=_params(("arbitrary",)),
    )(h3, p, target, g_ple, g_final, w_gate, w_proj)


def _row_tile(rows, cols, n_arrays):
    lanes = -(-cols // 128) * 128
    cap = VMEM_LIMIT // 3 // (2 * n_arrays * lanes * 4)
    best = None
    for t in range(PACK_ALIGN, min(rows, cap) + 1, PACK_ALIGN):
        if rows % t == 0:
            best = t
    assert best is not None, (rows, cols)
    return best


def _adamw_math(w, g, m, v):
    nm = ADAM_B1 * m + (1.0 - ADAM_B1) * g
    nv = ADAM_B2 * v + (1.0 - ADAM_B2) * (g * g)
    c1 = 1.0 - ADAM_B1 ** ADAM_STEP
    c2 = 1.0 - ADAM_B2 ** ADAM_STEP
    return -ADAM_LR * ((nm / c1) / (jnp.sqrt(nv / c2) + ADAM_EPS) + ADAM_WD * w), nm, nv


def adamw(w, g, m, v, name):
    R, C = w.shape
    tr = _row_tile(R, C, 7)

    def body(w_ref, g_ref, m_ref, v_ref, d_ref, nm_ref, nv_ref):
        d_ref[...], nm_ref[...], nv_ref[...] = _adamw_math(w_ref[...], g_ref[...], m_ref[...], v_ref[...])

    spec = pl.BlockSpec((tr, C), lambda i: (i, 0))
    return pl.pallas_call(
        body, name=name, grid=(R // tr,), in_specs=[spec] * 4, out_specs=[spec] * 3,
        out_shape=[_sds((R, C), F32)] * 3, compiler_params=_params(("arbitrary",)),
    )(w, g, m, v)


def join_halves(mine, other, core):
    rh, C = mine.shape
    tr = _row_tile(rh, C, 3)
    nb = rh // tr

    def body(c_ref, m_ref, o_ref, out_ref):
        out_ref[...] = jnp.where((pl.program_id(0) // nb) == c_ref[0], m_ref[...], o_ref[...])

    half = pl.BlockSpec((tr, C), lambda i, c: (i % nb, 0))
    return pl.pallas_call(
        body, name="join_halves",
        grid_spec=pltpu.PrefetchScalarGridSpec(num_scalar_prefetch=1, grid=(2 * nb,), in_specs=[half, half],
                                               out_specs=pl.BlockSpec((tr, C), lambda i, c: (i, 0))),
        out_shape=_sds((2 * rh, C), F32), compiler_params=_params(("arbitrary",)),
    )(core, mine, other)


def pair_sum(g, theirs, core):
    n, R, C = g.shape
    rh = R // 2
    tr = _row_tile(rh, C, 3)
    nb = rh // tr

    def body(c_ref, g_ref, t_ref, o_ref):
        o_ref[...] = (g_ref[...] + t_ref[...]).astype(BF16)

    here = pl.BlockSpec((1, tr, C), lambda j, i, c: (j, i, 0))
    return pl.pallas_call(
        body, name="pair_sum",
        grid_spec=pltpu.PrefetchScalarGridSpec(
            num_scalar_prefetch=1, grid=(n, nb),
            in_specs=[pl.BlockSpec((1, tr, C), lambda j, i, c: (j, c[0] * nb + i, 0)), here], out_specs=here),
        out_shape=_sds((n, rh, C), BF16), compiler_params=_params(("arbitrary", "arbitrary")),
    )(core, g, theirs)


def chip_sum(pair, got, chip):
    _, R, C = pair.shape
    tr = _row_tile(R, C, 5)

    def body(c_ref, p_ref, g0_ref, g1_ref, g2_ref, o_ref):
        f = lambda ref: ref[0].astype(F32)
        o_ref[...] = ((f(p_ref) + f(g0_ref)) + f(g1_ref)) + f(g2_ref)

    slot = lambda k: pl.BlockSpec((1, tr, C), lambda i, c: (k, i, 0))
    return pl.pallas_call(
        body, name="chip_sum",
        grid_spec=pltpu.PrefetchScalarGridSpec(
            num_scalar_prefetch=1, grid=(R // tr,),
            in_specs=[pl.BlockSpec((1, tr, C), lambda i, c: (c[0], i, 0)), slot(0), slot(1), slot(2)],
            out_specs=pl.BlockSpec((tr, C), lambda i, c: (i, 0))),
        out_shape=_sds((R, C), F32), compiler_params=_params(("arbitrary",)),
    )(chip, pair, got, got, got)


_HBM = pl.BlockSpec(memory_space=pltpu.HBM)


def _place():
    x, y, c = lax.axis_index("x"), lax.axis_index("y"), lax.axis_index("c")
    return x, y, c, [(1 - x, y), (x, 1 - y), (1 - x, 1 - y)]


def _spans(rows, n):
    assert rows % PACK_ALIGN == 0
    tiles = rows // PACK_ALIGN
    n = min(n, tiles)
    cuts = [tiles * q // n for q in range(n + 1)]
    return [(cuts[q] * PACK_ALIGN, (cuts[q + 1] - cuts[q]) * PACK_ALIGN) for q in range(n)]


def _remote(src, dst, send_sem, recv_sem, to):
    return pltpu.make_async_remote_copy(src_ref=src, dst_ref=dst, send_sem=send_sem, recv_sem=recv_sem,
                                        device_id=to, device_id_type=MESH)


def allgather_shards(wp):
    R, C = wp.shape
    rh = R // 2
    spans = _spans(rh, COPY_CHUNKS)
    n_sp = len(spans)
    local_spans = _spans(R, 2 * COPY_CHUNKS)

    def body(w_ref, out_ref, send_sems, recv_sems, pass_send, pass_recv, local_sems):
        x, y, c, chips = _place()
        me = 2 * x + y
        local = []
        for q, (o, n) in enumerate(local_spans):
            cp = pltpu.make_async_copy(w_ref.at[pl.ds(o, n), :], out_ref.at[me, pl.ds(o, n), :], local_sems.at[q])
            cp.start()
            local.append(cp)
        sends = []
        for k, (cx, cy) in enumerate(chips):
            for q, (o, n) in enumerate(spans):
                rows = pl.ds(c * rh + o, n)
                cp = _remote(w_ref.at[rows, :], out_ref.at[me, rows, :], send_sems.at[k * n_sp + q],
                             recv_sems.at[k * n_sp + q], (cx, cy, c))
                cp.start()
                sends.append(cp)
        for k, (cx, cy) in enumerate(chips):
            for q, (o, n) in enumerate(spans):
                blk = out_ref.at[2 * cx + cy, pl.ds(c * rh + o, n), :]
                _remote(blk, blk, send_sems.at[k * n_sp + q], recv_sems.at[k * n_sp + q], (cx, cy, c)).wait_recv()
                cp = _remote(blk, blk, pass_send.at[k * n_sp + q], pass_recv.at[k * n_sp + q], (x, y, 1 - c))
                cp.start()
                sends.append(cp)
        for k, (cx, cy) in enumerate(chips):
            for q, (o, n) in enumerate(spans):
                blk = out_ref.at[2 * cx + cy, pl.ds((1 - c) * rh + o, n), :]
                _remote(blk, blk, pass_send.at[k * n_sp + q], pass_recv.at[k * n_sp + q], (x, y, 1 - c)).wait_recv()
        for cp in sends:
            cp.wait_send()
        for cp in local:
            cp.wait()

    sems = pltpu.SemaphoreType.DMA((3 * n_sp,))
    return pl.pallas_call(
        body, name="allgather_shards", in_specs=[_HBM], out_specs=_HBM, out_shape=_sds((4, R, C), wp.dtype),
        scratch_shapes=[sems, sems, sems, sems, pltpu.SemaphoreType.DMA((len(local_spans),))],
    )(wp)


def sibling_split(g):
    n_sl, R, C = g.shape
    rh = R // 2
    spans = _spans(rh, COPY_CHUNKS)
    n_sp = len(spans)

    def body(g_ref, got_ref, send_sems, recv_sems):
        x, y, c, _ = _place()
        copies = []
        for j in range(n_sl):
            for q, (o, n) in enumerate(spans):
                cp = _remote(g_ref.at[j, pl.ds((1 - c) * rh + o, n), :], got_ref.at[j, pl.ds(o, n), :],
                             send_sems.at[j * n_sp + q], recv_sems.at[j * n_sp + q], (x, y, 1 - c))
                cp.start()
                copies.append(cp)
        for cp in copies:
            cp.wait()

    sems = pltpu.SemaphoreType.DMA((n_sl * n_sp,))
    return pl.pallas_call(
        body, name="sibling_split", in_specs=[_HBM], out_specs=_HBM, out_shape=_sds((n_sl, rh, C), g.dtype),
        scratch_shapes=[sems, sems],
    )(g)


def chip_exchange(p):
    _, R, C = p.shape
    spans = _spans(R, COPY_CHUNKS)
    n_sp = len(spans)

    def body(p_ref, buf_ref, send_sems, recv_sems):
        x, y, c, chips = _place()
        sends = []
        for k, (cx, cy) in enumerate(chips):
            for q, (o, n) in enumerate(spans):
                cp = _remote(p_ref.at[2 * cx + cy, pl.ds(o, n), :], buf_ref.at[k, pl.ds(o, n), :],
                             send_sems.at[k * n_sp + q], recv_sems.at[k * n_sp + q], (cx, cy, c))
                cp.start()
                sends.append(cp)
        for cp in sends:
            cp.wait()

    sems = pltpu.SemaphoreType.DMA((3 * n_sp,))
    return pl.pallas_call(
        body, name="chip_exchange", in_specs=[_HBM], out_specs=_HBM, out_shape=_sds((3, R, C), p.dtype),
        scratch_shapes=[sems, sems],
    )(p)


def sibling_swap(half):
    R, C = half.shape
    spans = _spans(R, COPY_CHUNKS)

    def body(h_ref, got_ref, send_sems, recv_sems):
        x, y, c, _ = _place()
        copies = []
        for q, (o, n) in enumerate(spans):
            cp = _remote(h_ref.at[pl.ds(o, n), :], got_ref.at[pl.ds(o, n), :], send_sems.at[q], recv_sems.at[q], (x, y, 1 - c))
            cp.start()
            copies.append(cp)
        for cp in copies:
            cp.wait()

    sems = pltpu.SemaphoreType.DMA((len(spans),))
    return pl.pallas_call(
        body, name="sibling_swap", in_specs=[_HBM], out_specs=_HBM, out_shape=_sds((R, C), half.dtype),
        scratch_shapes=[sems, sems],
    )(half)


def allreduce_small(v):
    R, C = v.shape

    def body(v_ref, out_ref, buf, send_sems, recv_sems):
        x, y, c, _ = _place()
        me = 4 * x + 2 * y + c
        buf[me] = v_ref[...]
        flips = [((k >> 2) & 1, (k >> 1) & 1, k & 1) for k in range(1, 8)]
        sends = []
        for k, (fx, fy, fc) in enumerate(flips):
            to = (1 - x if fx else x, 1 - y if fy else y, 1 - c if fc else c)
            cp = _remote(v_ref, buf.at[me], send_sems.at[k], recv_sems.at[k], to)
            cp.start()
            sends.append(cp)
        for k, (fx, fy, fc) in enumerate(flips):
            px, py, pc = (1 - x if fx else x, 1 - y if fy else y, 1 - c if fc else c)
            blk = buf.at[4 * px + 2 * py + pc]
            _remote(blk, blk, send_sems.at[k], recv_sems.at[k], (px, py, pc)).wait_recv()
        for cp in sends:
            cp.wait_send()
        acc = buf[0]
        for s in range(1, 8):
            acc = acc + buf[s]
        out_ref[...] = acc

    vm = pl.BlockSpec(memory_space=pltpu.VMEM)
    return pl.pallas_call(
        body, name="allreduce_small", in_specs=[vm], out_specs=vm, out_shape=_sds((R, C), F32),
        scratch_shapes=[pltpu.VMEM((8, R, C), F32), pltpu.SemaphoreType.DMA((7,)), pltpu.SemaphoreType.DMA((7,))],
        compiler_params=pltpu.CompilerParams(vmem_limit_bytes=VMEM_LIMIT),
    )(v)


def _rows_of(shape):
    return shape[0] * shape[1] // PACK_COLS


def _slot_rows(shape):
    return -(-_rows_of(shape) // PACK_ALIGN) * PACK_ALIGN


def _pack_shards(shards, dtype):
    parts = []
    for name, shape, _ in BIG:
        part = shards[name].reshape(_rows_of(shape), PACK_COLS).astype(dtype)
        parts.append(jnp.pad(part, ((0, _slot_rows(shape) - part.shape[0]), (0, 0))))
    used = sum(p.shape[0] for p in parts)
    parts.append(jnp.zeros((PACK_ROWS - used, PACK_COLS), dtype))
    return jnp.concatenate(parts, axis=0)


def _unpack_gathered(ag):
    out, off = {}, 0
    for name, shape, axis in BIG:
        r = _rows_of(shape)
        piece = ag[:, off:off + r, :].reshape((4,) + shape)
        off += _slot_rows(shape)
        if axis == 0:
            out[name] = piece.reshape(4 * shape[0], shape[1])
        else:
            out[name] = piece.transpose(1, 0, 2).reshape(shape[0], 4 * shape[1])
    return out


def _pack_full_grads(grads):
    parts = []
    for name, shape, axis in BIG:
        g = grads[name]
        if axis == 0:
            piece = g.reshape((4,) + shape)
        else:
            piece = g.reshape(shape[0], 4, shape[1]).transpose(1, 0, 2)
        piece = piece.reshape(4, _rows_of(shape), PACK_COLS)
        parts.append(jnp.pad(piece, ((0, 0), (0, _slot_rows(shape) - piece.shape[1]), (0, 0))))
    used = sum(p.shape[1] for p in parts)
    parts.append(jnp.zeros((4, PACK_ROWS - used, PACK_COLS), F32))
    return jnp.concatenate(parts, axis=1)


def _unpack_shards(packed):
    out, off = {}, 0
    for name, shape, _ in BIG:
        r = _rows_of(shape)
        out[name] = packed[off:off + r].reshape((1,) + shape)
        off += _slot_rows(shape)
    return out


def _pack_small(vals, extra=None):
    parts = [vals[name].reshape(-1) for name, _ in SMALL]
    used = sum(p.shape[0] for p in parts)
    if extra is not None:
        parts.append(extra.reshape(1))
        used += 1
    parts.append(jnp.zeros((SMALL_ROWS * 128 - used,), F32))
    return jnp.concatenate(parts).reshape(SMALL_ROWS, 128)


def _unpack_small(packed):
    flat = packed.reshape(-1)
    out, off = {}, 0
    for name, shape in SMALL:
        n = math.prod(shape)
        out[name] = flat[off:off + n].reshape(shape)
        off += n
    return out, flat[off]


def _to_heads(a):
    T = a.shape[0]
    return a.reshape(T, N_HEADS, HEAD_DIM).transpose(1, 0, 2)


def _split3(a):
    as_bf16 = lambda t: lax.reduce_precision(t, exponent_bits=8, mantissa_bits=7)
    hi = as_bf16(a)
    mid = as_bf16(a - hi)
    lo = as_bf16(a - hi - mid)
    return jnp.stack([hi, mid, lo], axis=-1).astype(BF16)


def _augment(feat, *extras):
    parts = [feat, *extras]
    used = sum(p.shape[-1] for p in parts)
    parts.append(jnp.zeros(feat.shape[:-1] + (2 * HEAD_DIM - used,), BF16))
    return jnp.concatenate(parts, axis=-1)


def _from_heads(a):
    T = a.shape[1]
    return a.transpose(1, 0, 2).reshape(T, N_HEADS * HEAD_DIM)


def _permute_time(a):
    T, n = a.shape
    return a.reshape(8, T // 8, n).transpose(1, 0, 2).reshape(T, n)


def _unpermute_time(a):
    T, n = a.shape
    return a.reshape(T // 8, 8, n).transpose(1, 0, 2).reshape(T, n)


def _discretize(a_re, a_im, log_dt, b_re, b_im):
    dt = jnp.exp(log_dt)[:, None]
    decay = jnp.exp(dt * a_re)
    abar_r = decay * jnp.cos(dt * a_im)
    abar_i = decay * jnp.sin(dt * a_im)
    nr, ni = abar_r - 1.0, abar_i
    den = a_re * a_re + a_im * a_im
    fr = (nr * a_re + ni * a_im) / den
    fi = (ni * a_re - nr * a_im) / den
    bbar_r = fr[..., None] * b_re - fi[..., None] * b_im
    bbar_i = fr[..., None] * b_im + fi[..., None] * b_re
    return abar_r, abar_i, bbar_r, bbar_i


def _input_matrix(bbar_r, bbar_i):
    eye = jnp.eye(N_GROUPS, dtype=F32)
    blk = lambda b: jnp.einsum("ghp,gk->ghkp", b.transpose(0, 2, 1), eye).reshape(SSM_W, STATE_W)
    return jnp.concatenate([blk(bbar_r), blk(bbar_i)], axis=1)


def _output_matrix(c_re, c_im):
    eye = jnp.eye(N_GROUPS, dtype=F32)
    blk = lambda cm: jnp.einsum("ghp,gk->gpkh", cm, eye).reshape(STATE_W, SSM_W)
    return jnp.concatenate([blk(c_re), -blk(c_im)], axis=0)


def _state_power(ar, ai, n):
    steps = int(round(math.log2(n)))
    assert 1 << steps == n
    for _ in range(steps):
        ar, ai = ar * ar - ai * ai, 2.0 * ar * ai
    return ar, ai


def kernel(x, p, g_ffn1, w1_a, w3_a, w2_a, g_mix, w_in, b_f, a_re, a_im, log_dt, b_re, b_im, c_re, c_im, d_skip, w_glu, b_glu, g_attn_out, g_ssm_out, w_out, g_ffn2, w1_b, w3_b, w2_b, g_ple, w_ple_gate, w_ple_proj, g_final, loss_target, m_g_ffn1, m_w1_a, m_w3_a, m_w2_a, m_g_mix, m_w_in, m_b_f, m_a_re, m_a_im, m_log_dt, m_b_re, m_b_im, m_c_re, m_c_im, m_d_skip, m_w_glu, m_b_glu, m_g_attn_out, m_g_ssm_out, m_w_out, m_g_ffn2, m_w1_b, m_w3_b, m_w2_b, m_g_ple, m_w_ple_gate, m_w_ple_proj, m_g_final, v_g_ffn1, v_w1_a, v_w3_a, v_w2_a, v_g_mix, v_w_in, v_b_f, v_a_re, v_a_im, v_log_dt, v_b_re, v_b_im, v_c_re, v_c_im, v_d_skip, v_w_glu, v_b_glu, v_g_attn_out, v_g_ssm_out, v_w_out, v_g_ffn2, v_w1_b, v_w3_b, v_w2_b, v_g_ple, v_w_ple_gate, v_w_ple_proj, v_g_final):
    args = dict(locals())
    weights = {n: args[n] for n in WEIGHT_ORDER}
    moms = {n: args["m_" + n] for n in WEIGHT_ORDER}
    vars_ = {n: args["v_" + n] for n in WEIGHT_ORDER}
    T = x.shape[1]
    x2, p2, tgt = x[0], p[0, 0], loss_target[0]

    full = _unpack_gathered(allgather_shards(_pack_shards({n: weights[n][0] for n, _, _ in BIG}, BF16)))
    loss_part, dx, grads = _local_step(x2, p2, tgt, {n: weights[n] for n, _ in SMALL}, full)
    return _reduce_and_update(weights, moms, vars_, loss_part, dx, grads)


def _local_step(x2, p2, tgt, sm, full):
    T = x2.shape[0]
    (g_ffn1, g_mix, b_f, a_re, a_im, log_dt, b_re, b_im, c_re, c_im, d_skip, b_glu, g_attn_out, g_ssm_out, g_ffn2, g_ple,
     g_final) = (sm[n] for n, _ in SMALL)
    w_in_f = full["w_in"]
    w_in_r = jnp.concatenate([w_in_f[:, :ATTN_W] * QK_SCALE, w_in_f[:, ATTN_W:3 * ATTN_W], w_in_f[:, 3 * ATTN_W + N_HEADS:],
                              w_in_f[:, 3 * ATTN_W:3 * ATTN_W + N_HEADS], jnp.zeros((D_MODEL, 128 - N_HEADS), BF16)], axis=1)
    b_f_pad = jnp.pad(b_f, ((0, 0), (0, 128 - N_HEADS)))

    disc_in = (a_re[0], a_im[0], log_dt[0], b_re[0], b_im[0])
    (abar_r, abar_i, bbar_r, bbar_i), disc_vjp = jax.vjp(_discretize, *disc_in)
    wb = _input_matrix(bbar_r, bbar_i)
    cbd = _output_matrix(c_re[0], c_im[0])
    ar, ai = abar_r.reshape(1, STATE_W), abar_i.reshape(1, STATE_W)
    alr, ali = _state_power(ar, ai, T // 8)
    dvec = d_skip.reshape(1, SSM_W)
    wb16, cbd16 = wb.astype(BF16), cbd.astype(BF16)

    h1, a1a, a3a, n1 = ffn_fwd(x2, g_ffn1, full["w1_a"], full["w3_a"], full["w2_a"], "ffn_a_fwd")
    u, qkv, s_in, fz, cum = mixin_fwd(h1, g_mix, w_in_r, b_f_pad)
    qh, kh, vh = (_to_heads(qkv[:, i * ATTN_W:(i + 1) * ATTN_W]) for i in range(3))
    ones3 = jnp.ones((N_HEADS, T, 3), BF16)
    k_aug = _augment(kh, _split3(cum[:, :N_HEADS].T), ones3)
    v_aug = _augment(vh, ones3)
    o_heads, lse = attn_fwd(_augment(qh, -ones3), k_aug, v_aug)
    attn = _from_heads(o_heads)
    s_perm = _permute_time(s_in)
    y_perm, xs = ssm_fwd(s_perm, wb16, cbd16, ar, ai, alr, ali, dvec)
    ypre = _unpermute_time(y_perm)
    h2, mixed = mixout_fwd(h1, attn, ypre, g_attn_out, g_ssm_out, full["w_glu"], b_glu, full["w_out"])
    h3, a1b, a3b, n2 = ffn_fwd(h2, g_ffn2, full["w1_b"], full["w3_b"], full["w2_b"], "ffn_b_fwd")

    dh3, n3, dzg, dpp, loss_part, dg_ple, dg_final = head_fwd_bwd(
        h3, p2, tgt, g_ple, g_final.reshape(1, D_MODEL), full["w_ple_gate"], full["w_ple_proj"])
    grads = {"g_ple": dg_ple, "g_final": dg_final.reshape(D_MODEL)}
    grads["w_ple_gate"] = mm_tn(n3, dzg, "dw_ple_gate")
    grads["w_ple_proj"] = mm_tn(p2, dpp, "dw_ple_proj")

    dh2, da1, da3, act, grads["g_ffn2"] = ffn_bwd(h2, g_ffn2, dh3, a1b, a3b, full["w1_b"], full["w3_b"], full["w2_b"], "ffn_b_bwd")
    grads["w1_b"] = mm_tn(n2, da1, "dw1_b")
    grads["w3_b"] = mm_tn(n2, da3, "dw3_b")
    grads["w2_b"] = mm_tn(act, dh3, "dw2_b", scale=0.5)

    seg = (jnp.arange(ATTN_W)[:, None] // HEAD_DIM == jnp.arange(128)[None, :]).astype(F32)
    dattn, dypre, dpre, yg, delta, grads["g_attn_out"], grads["g_ssm_out"], grads["b_glu"] = mixout_bwd(
        dh2, attn, ypre, g_attn_out, g_ssm_out, full["w_glu"], b_glu, full["w_out"], seg)
    grads["w_out"] = mm_tn(mixed, dh2, "dw_out")
    grads["w_glu"] = mm_tn(yg, dpre, "dw_glu")

    q_bwd = _augment(qh, -ones3, _split3(-lse.reshape(N_HEADS, T)))
    do_aug = _augment(_to_heads(dattn).astype(BF16), _split3(-delta[:, :N_HEADS].T))
    dq_aug, dk_aug, dv_aug, dck, dcq = attn_bwd(q_bwd, k_aug, v_aug, do_aug)
    dqh, dkh, dvh = (a[:, :, :HEAD_DIM] for a in (dq_aug, dk_aug, dv_aug))
    dc = jnp.pad((dck.reshape(N_HEADS, T) + dcq.reshape(N_HEADS, T)).T, ((0, 0), (0, 128 - N_HEADS)))

    dy_perm = _permute_time(dypre)
    du_perm, gs, d_a, dd = ssm_bwd(dy_perm, s_perm, xs, cbd16.T, wb16.T, ar, ai, alr, ali, dvec)
    ds_in = _unpermute_time(du_perm)
    d_wb = mm_tn(s_perm, gs, "dw_ssm_in")
    d_cbd = mm_tn(xs, dy_perm, "dw_ssm_out")
    diag_in = lambda m: jnp.einsum("ghgp->ghp", m.reshape(N_GROUPS, GROUP_CH, N_GROUPS, N_STATE)).transpose(0, 2, 1)
    diag_out = lambda m: jnp.einsum("gpgh->gph", m.reshape(N_GROUPS, N_STATE, N_GROUPS, GROUP_CH)).transpose(0, 2, 1)
    d_abar_r = jnp.sum(d_a[:, :STATE_W], axis=0).reshape(N_GROUPS, N_STATE)
    d_abar_i = jnp.sum(d_a[:, STATE_W:], axis=0).reshape(N_GROUPS, N_STATE)
    d_disc = disc_vjp((d_abar_r, d_abar_i, diag_in(d_wb[:, :STATE_W]), diag_in(d_wb[:, STATE_W:])))
    for name, val in zip(("a_re", "a_im", "log_dt", "b_re", "b_im"), d_disc):
        grads[name] = val[None]
    grads["c_re"] = diag_out(d_cbd[:STATE_W])[None]
    grads["c_im"] = -diag_out(d_cbd[STATE_W:])[None]
    grads["d_skip"] = dd.reshape(1, N_GROUPS, GROUP_CH)

    dh1, dz, grads["g_mix"], dbf = mixin_bwd(dh2, h1, g_mix, w_in_r, _from_heads(dqh), _from_heads(dkh), _from_heads(dvh),
                                             ds_in, dc, fz)
    grads["b_f"] = dbf[:, :N_HEADS]
    d_w_in_r = mm_tn(u, dz, "dw_in")
    grads["w_in"] = jnp.concatenate([d_w_in_r[:, :ATTN_W] * QK_SCALE, d_w_in_r[:, ATTN_W:3 * ATTN_W],
                                     d_w_in_r[:, 3 * ATTN_W + SSM_W:3 * ATTN_W + SSM_W + N_HEADS],
                                     d_w_in_r[:, 3 * ATTN_W:3 * ATTN_W + SSM_W]], axis=1)

    dx, da1, da3, act, grads["g_ffn1"] = ffn_bwd(x2, g_ffn1, dh1, a1a, a3a, full["w1_a"], full["w3_a"], full["w2_a"], "ffn_a_bwd")
    grads["w1_a"] = mm_tn(n1, da1, "dw1_a")
    grads["w3_a"] = mm_tn(n1, da3, "dw3_a")
    grads["w2_a"] = mm_tn(act, dh1, "dw2_a", scale=0.5)
    return loss_part, dx, grads


def _reduce_and_update(weights, moms, vars_, loss_part, dx, grads):
    core = lax.axis_index("c").astype(jnp.int32).reshape(1)
    chip = (2 * lax.axis_index("x") + lax.axis_index("y")).astype(jnp.int32).reshape(1)
    packed = _pack_full_grads(grads)
    pair = pair_sum(packed, sibling_split(packed), core)
    half = chip_sum(pair, chip_exchange(pair), chip)
    g_out = _unpack_shards(join_halves(half, sibling_swap(half), core))
    d_out, m_out, v_out = {}, {}, {}
    for n, _, _ in BIG:
        d, m, v = adamw(weights[n][0], g_out[n][0], moms[n][0], vars_[n][0], "adamw_" + n)
        d_out[n], m_out[n], v_out[n] = d[None], m[None], v[None]

    small = allreduce_small(_pack_small({n: grads[n] for n, _ in SMALL}, extra=loss_part[0, 0]))
    d_small, m_small, v_small = adamw(_pack_small(weights), small, _pack_small(moms), _pack_small(vars_), "adamw_small")

    g_small, loss = _unpack_small(small)
    g_out.update(g_small)
    outs = []
    for big, sm in ((d_out, d_small), (m_out, m_small), (v_out, v_small)):
        o, _ = _unpack_small(sm)
        o.update(big)
        outs.append(o)
    result = [loss, dx[None]] + [g_out[n] for n in WEIGHT_ORDER]
    for o in outs:
        result += [o[n] for n in WEIGHT_ORDER]
    return tuple(result)
```

```python
import functools
import math

import jax
import jax.numpy as jnp
from jax import lax
from jax.experimental import pallas as pl
from jax.experimental.pallas import tpu as pltpu

F32 = jnp.float32
BF16 = jnp.bfloat16

D_MODEL = 1024
D_FF = 2816
N_HEADS = 8
HEAD_DIM = 64
ATTN_W = 512
SSM_W = 512
N_GROUPS = 32
N_STATE = 64
GROUP_CH = 16
STATE_W = N_GROUPS * N_STATE
Z_COLS = 2176
QK_SCALE = 0.125
EPS = 1e-6

ADAM_LR = 0.001
ADAM_B1 = 0.9
ADAM_B2 = 0.999
ADAM_EPS = 1e-08
ADAM_WD = 0.01
ADAM_STEP = 10

TOKEN_TILE = 512
FFN_TOKEN_TILE = 256
FF_CHUNK = 1408
MM_K_TILE = 2048
ATTN_TILE = 512
SCAN_STEPS = 32
SCAN_LANES = 512
VMEM_LIMIT = 48 * 1024 * 1024
FFN_VMEM_LIMIT = 56 * 1024 * 1024
COPY_CHUNKS = 4

NT_DIMS = (((1,), (1,)), ((), ()))
TN_DIMS = (((0,), (0,)), ((), ()))
HIGHEST = lax.Precision.HIGHEST
MESH = pl.DeviceIdType.MESH

BIG = (
    ("w1_a", (1024, 704), 1), ("w3_a", (1024, 704), 1), ("w2_a", (704, 1024), 0),
    ("w_in", (1024, 514), 1), ("w_glu", (128, 512), 0), ("w_out", (256, 1024), 0),
    ("w1_b", (1024, 704), 1), ("w3_b", (1024, 704), 1), ("w2_b", (704, 1024), 0),
    ("w_ple_gate", (256, 1024), 0), ("w_ple_proj", (256, 256), 1),
)
PACK_COLS = 1024
PACK_ALIGN = 16
PACK_ROWS = 5408
SMALL = (
    ("g_ffn1", (1, 1024)), ("g_mix", (1, 1024)), ("b_f", (1, 8)), ("a_re", (1, 32, 64)), ("a_im", (1, 32, 64)),
    ("log_dt", (1, 32)), ("b_re", (1, 32, 64, 16)), ("b_im", (1, 32, 64, 16)), ("c_re", (1, 32, 16, 64)),
    ("c_im", (1, 32, 16, 64)), ("d_skip", (1, 32, 16)), ("b_glu", (1, 512)), ("g_attn_out", (1, 512)),
    ("g_ssm_out", (1, 512)), ("g_ffn2", (1, 1024)), ("g_ple", (1, 1024)), ("g_final", (1024,)),
)
SMALL_ROWS = 1120
WEIGHT_ORDER = ("g_ffn1", "w1_a", "w3_a", "w2_a", "g_mix", "w_in", "b_f", "a_re", "a_im", "log_dt", "b_re", "b_im",
                "c_re", "c_im", "d_skip", "w_glu", "b_glu", "g_attn_out", "g_ssm_out", "w_out", "g_ffn2", "w1_b",
                "w3_b", "w2_b", "g_ple", "w_ple_gate", "w_ple_proj", "g_final")


def _params(sem=None, vmem=VMEM_LIMIT):
    kw = dict(vmem_limit_bytes=vmem)
    if sem is not None:
        kw["dimension_semantics"] = sem
    return pltpu.CompilerParams(**kw)


def _sds(shape, dtype):
    return jax.ShapeDtypeStruct(shape, dtype)


def _tile(n, pref):
    t = min(n, pref)
    assert n % t == 0, (n, pref)
    return t


def _rms_scale(x):
    return lax.rsqrt(jnp.mean(x * x, axis=-1, keepdims=True) + EPS)


def _rms_bwd(dy, x, g):
    r = _rms_scale(x)
    xh = x * r
    dxh = dy * g
    dx = r * (dxh - xh * jnp.mean(dxh * xh, axis=-1, keepdims=True))
    return dx, jnp.sum(dy * xh, axis=0, keepdims=True)


def _dot(a, b):
    return jnp.dot(a, b, preferred_element_type=F32)


def _dot_nt(a, b):
    return lax.dot_general(a, b, NT_DIMS, preferred_element_type=F32)


def _dot_tn(a, b):
    return lax.dot_general(a, b, TN_DIMS, preferred_element_type=F32)


_GELU_C = math.sqrt(2.0 / math.pi)


def _gelu_parts(x):
    t = jnp.tanh(_GELU_C * (x + 0.044715 * x * x * x))
    return 0.5 * x * (1.0 + t), t


def _gelu_grad(x, t):
    return 0.5 * (1.0 + t) + 0.5 * x * (1.0 - t * t) * _GELU_C * (1.0 + 3.0 * 0.044715 * x * x)


def _resident(shape):
    return pl.BlockSpec(shape, lambda i: (0,) * len(shape), pipeline_mode=pl.Buffered(1))


def ffn_fwd(h, g, w1, w3, w2, name):
    T = h.shape[0]
    tm = _tile(T, FFN_TOKEN_TILE)

    def body(h_ref, g_ref, w1_ref, w3_ref, w2_ref, ho_ref, a1_ref, a3_ref, n_ref):
        x = h_ref[...]
        n = (x * _rms_scale(x) * g_ref[...]).astype(BF16)
        n_ref[...] = n
        out = x
        for lo in range(0, D_FF, FF_CHUNK):
            cols = slice(lo, lo + FF_CHUNK)
            a1 = _dot(n, w1_ref[:, cols])
            a3 = _dot(n, w3_ref[:, cols])
            a1_ref[:, cols] = a1.astype(BF16)
            a3_ref[:, cols] = a3.astype(BF16)
            act = (a1 * jax.nn.sigmoid(a1) * a3).astype(BF16)
            out = out + 0.5 * _dot(act, w2_ref[cols, :])
        ho_ref[...] = out

    tok = lambda i: (i, 0)
    return pl.pallas_call(
        body, name=name, grid=(T // tm,),
        in_specs=[pl.BlockSpec((tm, D_MODEL), tok), _resident((1, D_MODEL)), _resident((D_MODEL, D_FF)),
                  _resident((D_MODEL, D_FF)), _resident((D_FF, D_MODEL))],
        out_specs=[pl.BlockSpec((tm, D_MODEL), tok), pl.BlockSpec((tm, D_FF), tok), pl.BlockSpec((tm, D_FF), tok),
                   pl.BlockSpec((tm, D_MODEL), tok)],
        out_shape=[_sds((T, D_MODEL), F32), _sds((T, D_FF), BF16), _sds((T, D_FF), BF16), _sds((T, D_MODEL), BF16)],
        compiler_params=_params(("arbitrary",), FFN_VMEM_LIMIT),
    )(h, g, w1, w3, w2)


def ffn_bwd(h, g, dho, a1, a3, w1, w3, w2, name):
    T = h.shape[0]
    tm = _tile(T, FFN_TOKEN_TILE)

    def body(h_ref, g_ref, dho_ref, a1_ref, a3_ref, w1_ref, w3_ref, w2_ref, dhi_ref, da1_ref, da3_ref, act_ref, dg_ref):
        @pl.when(pl.program_id(0) == 0)
        def _():
            dg_ref[...] = jnp.zeros_like(dg_ref)

        dho = dho_ref[...]
        dhb = (0.5 * dho).astype(BF16)
        dn = None
        for lo in range(0, D_FF, FF_CHUNK):
            cols = slice(lo, lo + FF_CHUNK)
            a1v = a1_ref[:, cols].astype(F32)
            a3v = a3_ref[:, cols].astype(F32)
            s = jax.nn.sigmoid(a1v)
            sl = a1v * s
            dact = _dot_nt(dhb, w2_ref[cols, :])
            act_ref[:, cols] = (sl * a3v).astype(BF16)
            da1 = (dact * a3v * s * (1.0 + a1v * (1.0 - s))).astype(BF16)
            da3 = (dact * sl).astype(BF16)
            da1_ref[:, cols] = da1
            da3_ref[:, cols] = da3
            part = _dot_nt(da1, w1_ref[:, cols]) + _dot_nt(da3, w3_ref[:, cols])
            dn = part if dn is None else dn + part
        dx, dg = _rms_bwd(dn, h_ref[...], g_ref[...])
        dg_ref[...] += dg
        dhi_ref[...] = dho + dx

    tok = lambda i: (i, 0)
    return pl.pallas_call(
        body, name=name, grid=(T // tm,),
        in_specs=[pl.BlockSpec((tm, D_MODEL), tok), _resident((1, D_MODEL)), pl.BlockSpec((tm, D_MODEL), tok),
                  pl.BlockSpec((tm, D_FF), tok), pl.BlockSpec((tm, D_FF), tok), _resident((D_MODEL, D_FF)),
                  _resident((D_MODEL, D_FF)), _resident((D_FF, D_MODEL))],
        out_specs=[pl.BlockSpec((tm, D_MODEL), tok), pl.BlockSpec((tm, D_FF), tok), pl.BlockSpec((tm, D_FF), tok),
                   pl.BlockSpec((tm, D_FF), tok), pl.BlockSpec((1, D_MODEL), lambda i: (0, 0))],
        out_shape=[_sds((T, D_MODEL), F32), _sds((T, D_FF), BF16), _sds((T, D_FF), BF16), _sds((T, D_FF), BF16),
                   _sds((1, D_MODEL), F32)],
        compiler_params=_params(("arbitrary",), FFN_VMEM_LIMIT),
    )(h, g, dho, a1, a3, w1, w3, w2)


def mm_tn(a, b, name, scale=1.0):
    T, M = a.shape
    N = b.shape[1]
    bm = 512 if M % 512 == 0 else (1408 if M == 2816 else 256)
    bn = N if N in (2176, 1408) else (1408 if N == 2816 else min(N, 1024))
    tk = _tile(T, MM_K_TILE)
    row_bytes = 2 * (bm * a.dtype.itemsize + bn * b.dtype.itemsize)
    while tk > TOKEN_TILE and tk * row_bytes > VMEM_LIMIT // 3:
        tk //= 2
    assert M % bm == 0 and N % bn == 0 and T % tk == 0
    n_k = T // tk

    def body(a_ref, b_ref, o_ref):
        k = pl.program_id(2)

        @pl.when(k == 0)
        def _():
            o_ref[...] = jnp.zeros_like(o_ref)

        o_ref[...] += _dot_tn(a_ref[...].astype(BF16), b_ref[...].astype(BF16))

        if scale != 1.0:
            @pl.when(k == n_k - 1)
            def _():
                o_ref[...] = o_ref[...] * scale

    return pl.pallas_call(
        body, name=name, grid=(M // bm, N // bn, n_k),
        in_specs=[pl.BlockSpec((tk, bm), lambda m, n, k: (k, m)), pl.BlockSpec((tk, bn), lambda m, n, k: (k, n))],
        out_specs=pl.BlockSpec((bm, bn), lambda m, n, k: (m, n)),
        out_shape=_sds((M, N), F32),
        compiler_params=_params(("arbitrary", "arbitrary", "arbitrary")),
    )(a, b)


def mixin_fwd(h1, g, w_in_r, b_f_pad):
    T = h1.shape[0]
    tm = _tile(T, TOKEN_TILE)

    def body(h_ref, g_ref, w_ref, bf_ref, u_ref, qkv_ref, s_ref, fz_ref, c_ref, carry):
        @pl.when(pl.program_id(0) == 0)
        def _():
            carry[...] = jnp.zeros_like(carry)

        x = h_ref[...]
        u = (x * _rms_scale(x) * g_ref[...]).astype(BF16)
        u_ref[...] = u
        z = _dot(u, w_ref[...])
        qkv_ref[...] = z[:, :3 * ATTN_W].astype(BF16)
        s_ref[...] = z[:, 3 * ATTN_W:3 * ATTN_W + SSM_W]
        fz = z[:, 3 * ATTN_W + SSM_W:] + bf_ref[...]
        fz_ref[...] = fz
        lane = lax.broadcasted_iota(jnp.int32, fz.shape, 1)
        logf = jnp.where(lane < N_HEADS, jnp.minimum(fz, 0.0) - jnp.log(1.0 + jnp.exp(-jnp.abs(fz))), 0.0)
        row = lax.broadcasted_iota(jnp.int32, (tm, tm), 0)
        col = lax.broadcasted_iota(jnp.int32, (tm, tm), 1)
        tri = (col <= row).astype(F32)
        cs = jnp.dot(tri, logf, precision=HIGHEST, preferred_element_type=F32) + carry[0:1, :]
        c_ref[...] = cs
        carry[...] = jnp.broadcast_to(cs[tm - 1:tm, :], carry.shape)

    tok = lambda i: (i, 0)
    fix = lambda i: (0, 0)
    return pl.pallas_call(
        body, name="mixin_fwd", grid=(T // tm,),
        in_specs=[pl.BlockSpec((tm, D_MODEL), tok), pl.BlockSpec((1, D_MODEL), fix),
                  pl.BlockSpec((D_MODEL, Z_COLS), fix), pl.BlockSpec((1, 128), fix)],
        out_specs=[pl.BlockSpec((tm, D_MODEL), tok), pl.BlockSpec((tm, 3 * ATTN_W), tok), pl.BlockSpec((tm, SSM_W), tok),
                   pl.BlockSpec((tm, 128), tok), pl.BlockSpec((tm, 128), tok)],
        out_shape=[_sds((T, D_MODEL), BF16), _sds((T, 3 * ATTN_W), BF16), _sds((T, SSM_W), F32),
                   _sds((T, 128), F32), _sds((T, 128), F32)],
        scratch_shapes=[pltpu.VMEM((8, 128), F32)],
        compiler_params=_params(("arbitrary",)),
    )(h1, g, w_in_r, b_f_pad)


def mixin_bwd(dh2, h1, g, w_in_r, dq, dk, dv, ds, dc, fz):
    T = h1.shape[0]
    tm = _tile(T, TOKEN_TILE)
    n_t = T // tm

    def body(dh2_ref, h_ref, g_ref, w_ref, dq_ref, dk_ref, dv_ref, ds_ref, dc_ref, fz_ref,
             dh1_ref, dz_ref, dg_ref, dbf_ref, carry):
        @pl.when(pl.program_id(0) == 0)
        def _():
            carry[...] = jnp.zeros_like(carry)
            dg_ref[...] = jnp.zeros_like(dg_ref)
            dbf_ref[...] = jnp.zeros_like(dbf_ref)

        row = lax.broadcasted_iota(jnp.int32, (tm, tm), 0)
        col = lax.broadcasted_iota(jnp.int32, (tm, tm), 1)
        tri = (col >= row).astype(F32)
        dlogf = jnp.dot(tri, dc_ref[...], precision=HIGHEST, preferred_element_type=F32) + carry[0:1, :]
        carry[...] = jnp.broadcast_to(dlogf[0:1, :], carry.shape)
        dfz = dlogf * jax.nn.sigmoid(-fz_ref[...])
        dbf_ref[...] += jnp.sum(dfz, axis=0, keepdims=True)
        dz = jnp.concatenate([_join_heads(dq_ref, BF16), _join_heads(dk_ref, BF16), _join_heads(dv_ref, BF16),
                              ds_ref[...], dfz], axis=1).astype(BF16)
        dz_ref[...] = dz
        du = _dot_nt(dz, w_ref[...])
        dx, dg = _rms_bwd(du, h_ref[...], g_ref[...])
        dg_ref[...] += dg
        dh1_ref[...] = dh2_ref[...] + dx

    tok = lambda i: (n_t - 1 - i, 0)
    fix = lambda i: (0, 0)
    heads = pl.BlockSpec((N_HEADS, tm, 128), lambda i: (0, n_t - 1 - i, 0))
    return pl.pallas_call(
        body, name="mixin_bwd", grid=(n_t,),
        in_specs=[pl.BlockSpec((tm, D_MODEL), tok), pl.BlockSpec((tm, D_MODEL), tok), pl.BlockSpec((1, D_MODEL), fix),
                  pl.BlockSpec((D_MODEL, Z_COLS), fix), heads, heads, heads, pl.BlockSpec((tm, SSM_W), tok),
                  pl.BlockSpec((tm, 128), tok), pl.BlockSpec((tm, 128), tok)],
        out_specs=[pl.BlockSpec((tm, D_MODEL), tok), pl.BlockSpec((tm, Z_COLS), tok), pl.BlockSpec((1, D_MODEL), fix),
                   pl.BlockSpec((1, 128), fix)],
        out_shape=[_sds((T, D_MODEL), F32), _sds((T, Z_COLS), BF16), _sds((1, D_MODEL), F32), _sds((1, 128), F32)],
        scratch_shapes=[pltpu.VMEM((8, 128), F32)],
        compiler_params=_params(("arbitrary",)),
    )(dh2, h1, g, w_in_r, dq, dk, dv, ds, dc, fz)


def _lane_move(src_lo, dst_lo, width, dtype):
    r = lax.broadcasted_iota(jnp.int32, (128, 128), 0)
    c = lax.broadcasted_iota(jnp.int32, (128, 128), 1)
    return ((c - dst_lo == r - src_lo) & (r >= src_lo) & (r < src_lo + width)).astype(dtype)


def _lane_const(lo, width, value):
    lane = lax.broadcasted_iota(jnp.int32, (1, 128), 1)
    return jnp.where((lane >= lo) & (lane < lo + width), value, 0.0).astype(F32)


def _pieces(a):
    hi = a.astype(BF16)
    rest = a - hi.astype(F32)
    mid = rest.astype(BF16)
    return hi, mid, (rest - mid.astype(F32)).astype(BF16)


def _head_features(pair_block, e):
    return _dot(pair_block, _lane_move(HEAD_DIM * e, 0, HEAD_DIM, BF16))


def _helper_columns(pieces, head, sign):
    out = None
    for k, piece in enumerate(pieces):
        term = _dot(piece, _lane_move(head, HEAD_DIM + k, 1, BF16))
        out = term if out is None else out + term
    return sign * out


def heads_in(qkv, cum):
    T = qkv.shape[0]
    tm = _tile(T, TOKEN_TILE)

    def body(qkv_ref, c_ref, q_ref, k_ref, v_ref):
        c = _pieces(c_ref[...])
        for h in range(N_HEADS):
            p, e = divmod(h, 2)
            blk = lambda base: qkv_ref[:, base + 128 * p:base + 128 * (p + 1)]
            q_ref[h] = (_head_features(blk(0), e) + _lane_const(HEAD_DIM, 3, -1.0)).astype(BF16)
            k_ref[h] = (_head_features(blk(ATTN_W), e) + _helper_columns(c, h, 1.0)
                        + _lane_const(HEAD_DIM + 3, 3, 1.0)).astype(BF16)
            v_ref[h] = (_head_features(blk(2 * ATTN_W), e) + _lane_const(HEAD_DIM, 3, 1.0)).astype(BF16)

    tok = lambda i: (i, 0)
    heads = pl.BlockSpec((N_HEADS, tm, 128), lambda i: (0, i, 0))
    return pl.pallas_call(
        body, name="heads_in", grid=(T // tm,),
        in_specs=[pl.BlockSpec((tm, 3 * ATTN_W), tok), pl.BlockSpec((tm, 128), tok)],
        out_specs=[heads] * 3, out_shape=[_sds((N_HEADS, T, 128), BF16)] * 3,
        compiler_params=_params(("arbitrary",)),
    )(qkv, cum)


def attn_fwd(q_aug, k_aug, v_aug):
    H, T, wd = q_aug.shape
    hd = HEAD_DIM
    tq = _tile(T, ATTN_TILE)
    n = T // tq

    def body(q_ref, k_ref, v_ref, o_ref, qb_ref, m_sc, acc):
        qi = pl.program_id(1)
        qv = q_ref[0]
        m_sc[...] = jnp.full_like(m_sc, -jnp.inf)
        acc[...] = jnp.zeros_like(acc)

        def tile(j, masked):
            rows = pl.ds(pl.multiple_of(j * tq, tq), tq)
            st = _dot_nt(k_ref[0, rows, :], qv)
            if masked:
                keep = lax.broadcasted_iota(jnp.int32, (tq, tq), 0) <= lax.broadcasted_iota(jnp.int32, (tq, tq), 1)
                st = jnp.where(keep, st, -1e30)
            m_old = m_sc[...]
            m_new = jnp.maximum(m_old, jnp.max(st, axis=0, keepdims=True))
            pt = jnp.exp(st - m_new).astype(BF16)
            acc[...] = jnp.exp(m_old - m_new) * acc[...] + _dot_tn(v_ref[0, rows, :], pt)
            m_sc[...] = m_new

        def off_diagonal(j, carry):
            tile(j, False)
            return carry

        lax.fori_loop(0, qi, off_diagonal, 0)
        tile(qi, True)
        total = acc[hd:hd + 1, :]
        o_ref[0] = (acc[...] / total).T
        hi, mid, lo = (t.astype(F32) for t in _pieces(-(m_sc[...] + jnp.log(total))))
        row = lax.broadcasted_iota(jnp.int32, (wd, tq), 0)
        lse_rows = jnp.where(row == hd + 3, hi, jnp.where(row == hd + 4, mid, jnp.where(row == hd + 5, lo, 0.0)))
        qb_ref[0] = (qv.astype(F32) + lse_rows.T).astype(BF16)

    qmap = lambda h, i: (h, i, 0)
    head = lambda h, i: (h, 0, 0)
    return pl.pallas_call(
        body, name="attn_fwd", grid=(H, n),
        in_specs=[pl.BlockSpec((1, tq, wd), qmap), pl.BlockSpec((1, T, wd), head), pl.BlockSpec((1, T, wd), head)],
        out_specs=[pl.BlockSpec((1, tq, wd), qmap), pl.BlockSpec((1, tq, wd), qmap)],
        out_shape=[_sds((H, T, wd), F32), _sds((H, T, wd), BF16)],
        scratch_shapes=[pltpu.VMEM((1, tq), F32), pltpu.VMEM((wd, tq), F32)],
        compiler_params=_params(("arbitrary", "arbitrary")),
    )(q_aug, k_aug, v_aug)


def attn_bwd(q_aug, k_aug, v_aug, do_aug):
    H, T, wd = q_aug.shape
    tq = _tile(T, ATTN_TILE)
    n = T // tq

    def body(q_ref, do_ref, k_ref, v_ref, dq_ref, dk_ref, dv_ref, dck_ref, dcq_ref):
        j = pl.program_id(1)

        @pl.when(j == 0)
        def _():
            dq_ref[...] = jnp.zeros_like(dq_ref)
            dcq_ref[...] = jnp.zeros_like(dcq_ref)

        dk_ref[...] = jnp.zeros_like(dk_ref)
        dv_ref[...] = jnp.zeros_like(dv_ref)
        dck_ref[...] = jnp.zeros_like(dck_ref)
        kv, vv = k_ref[0], v_ref[0]

        def tile(i, masked):
            rows = pl.ds(pl.multiple_of(i * tq, tq), tq)
            qv, dov = q_ref[0, rows, :], do_ref[0, rows, :]
            pt = jnp.exp(_dot_nt(kv, qv))
            if masked:
                keep = lax.broadcasted_iota(jnp.int32, (tq, tq), 0) <= lax.broadcasted_iota(jnp.int32, (tq, tq), 1)
                pt = jnp.where(keep, pt, 0.0)
            dv_ref[0] += _dot(pt.astype(BF16), dov)
            dst = pt * _dot_nt(vv, dov)
            dsb = dst.astype(BF16)
            dk_ref[0] += _dot(dsb, qv)
            dq_ref[0, rows, :] += _dot_tn(dsb, kv)
            dck_ref[0] += -jnp.sum(dst, axis=1, keepdims=True)
            dcq_ref[0, pl.ds(i, 1), :] += jnp.sum(dst, axis=0, keepdims=True)

        def off_diagonal(i, carry):
            tile(i, False)
            return carry

        tile(j, True)
        lax.fori_loop(j + 1, n, off_diagonal, 0)

    head = lambda h, j: (h, 0, 0)
    kmap = lambda h, j: (h, j, 0)
    return pl.pallas_call(
        body, name="attn_bwd", grid=(H, n),
        in_specs=[pl.BlockSpec((1, T, wd), head), pl.BlockSpec((1, T, wd), head), pl.BlockSpec((1, tq, wd), kmap),
                  pl.BlockSpec((1, tq, wd), kmap)],
        out_specs=[pl.BlockSpec((1, T, wd), head), pl.BlockSpec((1, tq, wd), kmap), pl.BlockSpec((1, tq, wd), kmap),
                   pl.BlockSpec((1, tq, 1), kmap), pl.BlockSpec((1, n, tq), head)],
        out_shape=[_sds((H, T, wd), F32), _sds((H, T, wd), F32), _sds((H, T, wd), F32), _sds((H, T, 1), F32),
                   _sds((H, n, tq), F32)],
        compiler_params=_params(("arbitrary", "arbitrary")),
    )(q_aug, do_aug, k_aug, v_aug)


def _complex_step(a_r, a_i, cr, ci, br, bi):
    return a_r * cr - a_i * ci + br, a_r * ci + a_i * cr + bi


def ssm_fwd(s_perm, wb, cbd, a_r, a_i, al_r, al_i, dvec):
    T = s_perm.shape[0]
    chunk = T // 8
    ts = _tile(chunk, SCAN_STEPS)
    tr, n_s = ts * 8, chunk // ts
    W, LB = STATE_W, SCAN_LANES

    def body(s_ref, wb_ref, cbd_ref, ar_ref, ai_ref, alr_ref, ali_ref, dv_ref, y_ref, xs_ref, bu, carry):
        ph, i = pl.program_id(0), pl.program_id(1)

        @pl.when((ph == 0) & (i == 0))
        def _():
            carry[...] = jnp.zeros_like(carry)

        bu[...] = _dot(s_ref[...].astype(BF16), wb_ref[...])

        def scan(store):
            for lb in range(W // LB):
                lo = lb * LB
                re, im = slice(lo, lo + LB), slice(W + lo, W + lo + LB)
                ar = jnp.broadcast_to(ar_ref[:, re], (8, LB))
                ai = jnp.broadcast_to(ai_ref[:, re], (8, LB))

                def step(s, c):
                    rows = pl.ds(pl.multiple_of(s * 8, 8), 8)
                    nr, ni = _complex_step(ar, ai, c[0], c[1], bu[rows, re], bu[rows, im])
                    if store:
                        bu[rows, re] = nr
                        bu[rows, im] = ni
                    return nr, ni

                cr, ci = lax.fori_loop(0, ts, step, (carry[:, re], carry[:, im]), unroll=2)
                carry[:, re] = cr
                carry[:, im] = ci

        @pl.when(ph == 0)
        def _():
            scan(False)

            @pl.when(i == n_s - 1)
            def _():
                er, ei = carry[:, :W], carry[:, W:]
                alr = jnp.broadcast_to(alr_ref[...], (8, W))
                ali = jnp.broadcast_to(ali_ref[...], (8, W))
                first = lax.broadcasted_iota(jnp.int32, (8, W), 0) == 0
                sr, si = jnp.zeros((8, W), F32), jnp.zeros((8, W), F32)
                for _ in range(7):
                    vr, vi = _complex_step(alr, ali, sr, si, er, ei)
                    sr = jnp.where(first, 0.0, pltpu.roll(vr, 1, 0))
                    si = jnp.where(first, 0.0, pltpu.roll(vi, 1, 0))
                carry[:, :W] = sr
                carry[:, W:] = si

        @pl.when(ph == 1)
        def _():
            scan(True)
            xb = bu[...].astype(BF16)
            xs_ref[...] = xb
            y_ref[...] = _dot(xb, cbd_ref[...]) + s_ref[...] * dv_ref[...]

    fix = lambda p, i: (0, 0)
    return pl.pallas_call(
        body, name="ssm_fwd", grid=(2, n_s),
        in_specs=[pl.BlockSpec((tr, SSM_W), lambda p, i: (i, 0)), pl.BlockSpec((SSM_W, 2 * W), fix),
                  pl.BlockSpec((2 * W, SSM_W), fix), pl.BlockSpec((1, W), fix), pl.BlockSpec((1, W), fix),
                  pl.BlockSpec((1, W), fix), pl.BlockSpec((1, W), fix), pl.BlockSpec((1, SSM_W), fix)],
        out_specs=[pl.BlockSpec((tr, SSM_W), lambda p, i: (i * p, 0)), pl.BlockSpec((tr, 2 * W), lambda p, i: (i * p, 0))],
        out_shape=[_sds((T, SSM_W), F32), _sds((T, 2 * W), BF16)],
        scratch_shapes=[pltpu.VMEM((tr, 2 * W), F32), pltpu.VMEM((8, 2 * W), F32)],
        compiler_params=_params(("arbitrary", "arbitrary")),
    )(s_perm, wb, cbd, a_r, a_i, al_r, al_i, dvec)


def ssm_bwd(dy_perm, s_perm, xs, cbd_t, wb_t, a_r, a_i, al_r, al_i, dvec):
    T = s_perm.shape[0]
    chunk = T // 8
    ts = _tile(chunk, SCAN_STEPS)
    tr, n_s = ts * 8, chunk // ts
    W, LB = STATE_W, SCAN_LANES

    def body(dy_ref, s_ref, xs_ref, cbt_ref, wbt_ref, ar_ref, ai_ref, alr_ref, ali_ref, dv_ref,
             du_ref, gs_ref, da_ref, dd_ref, gd, x32, carry):
        ph, i = pl.program_id(0), pl.program_id(1)

        @pl.when((ph == 0) & (i == 0))
        def _():
            carry[...] = jnp.zeros_like(carry)
            da_ref[...] = jnp.zeros_like(da_ref)
            dd_ref[...] = jnp.zeros_like(dd_ref)

        gd[...] = _dot(dy_ref[...].astype(BF16), cbt_ref[...])

        def scan(store):
            for lb in range(W // LB):
                lo = lb * LB
                re, im = slice(lo, lo + LB), slice(W + lo, W + lo + LB)
                ar = jnp.broadcast_to(ar_ref[:, re], (8, LB))
                nai = -jnp.broadcast_to(ai_ref[:, re], (8, LB))

                def step(k, c):
                    rows = pl.ds(pl.multiple_of((ts - 1 - k) * 8, 8), 8)
                    cr, ci = c[0], c[1]
                    nr, ni = _complex_step(ar, nai, cr, ci, gd[rows, re], gd[rows, im])
                    if store:
                        xr, xi = x32[rows, re], x32[rows, im]
                        gd[rows, re] = nr
                        gd[rows, im] = ni
                        return nr, ni, c[2] + cr * xr + ci * xi, c[3] + ci * xr - cr * xi
                    return nr, ni

                init = (carry[:, re], carry[:, im])
                if store:
                    init = init + (da_ref[:, re], da_ref[:, im])
                out = lax.fori_loop(0, ts, step, init, unroll=2)
                carry[:, re] = out[0]
                carry[:, im] = out[1]
                if store:
                    da_ref[:, re] = out[2]
                    da_ref[:, im] = out[3]

        @pl.when(ph == 0)
        def _():
            scan(False)

            @pl.when(i == n_s - 1)
            def _():
                er, ei = carry[:, :W], carry[:, W:]
                alr = jnp.broadcast_to(alr_ref[...], (8, W))
                nali = -jnp.broadcast_to(ali_ref[...], (8, W))
                last = lax.broadcasted_iota(jnp.int32, (8, W), 0) == 7
                rr, ri = jnp.zeros((8, W), F32), jnp.zeros((8, W), F32)
                for _ in range(7):
                    vr, vi = _complex_step(alr, nali, rr, ri, er, ei)
                    rr = jnp.where(last, 0.0, pltpu.roll(vr, 7, 0))
                    ri = jnp.where(last, 0.0, pltpu.roll(vi, 7, 0))
                carry[:, :W] = rr
                carry[:, W:] = ri

        @pl.when(ph == 1)
        def _():
            x32[...] = xs_ref[...].astype(F32)
            scan(True)
            gb = gd[...].astype(BF16)
            gs_ref[...] = gb
            dy = dy_ref[...]
            du_ref[...] = _dot(gb, wbt_ref[...]) + dy * dv_ref[...]
            dd_ref[...] += jnp.sum(dy * s_ref[...], axis=0, keepdims=True)

    fix = lambda p, i: (0, 0)
    rev = lambda p, i: (n_s - 1 - i, 0)
    rev_out = lambda p, i: (n_s - 1 - i * p, 0)
    return pl.pallas_call(
        body, name="ssm_bwd", grid=(2, n_s),
        in_specs=[pl.BlockSpec((tr, SSM_W), rev), pl.BlockSpec((tr, SSM_W), rev), pl.BlockSpec((tr, 2 * W), rev),
                  pl.BlockSpec((SSM_W, 2 * W), fix), pl.BlockSpec((2 * W, SSM_W), fix), pl.BlockSpec((1, W), fix),
                  pl.BlockSpec((1, W), fix), pl.BlockSpec((1, W), fix), pl.BlockSpec((1, W), fix),
                  pl.BlockSpec((1, SSM_W), fix)],
        out_specs=[pl.BlockSpec((tr, SSM_W), rev_out), pl.BlockSpec((tr, 2 * W), rev_out),
                   pl.BlockSpec((8, 2 * W), fix), pl.BlockSpec((1, SSM_W), fix)],
        out_shape=[_sds((T, SSM_W), F32), _sds((T, 2 * W), BF16), _sds((8, 2 * W), F32), _sds((1, SSM_W), F32)],
        scratch_shapes=[pltpu.VMEM((tr, 2 * W), F32), pltpu.VMEM((tr, 2 * W), F32), pltpu.VMEM((8, 2 * W), F32)],
        compiler_params=_params(("arbitrary", "arbitrary")),
    )(dy_perm, s_perm, xs, cbd_t, wb_t, a_r, a_i, al_r, al_i, dvec)


def _join_heads(ref, dtype):
    precision = HIGHEST if dtype == F32 else None
    move = lambda h, dst: jnp.dot(ref[h].astype(dtype), _lane_move(0, dst, HEAD_DIM, dtype), precision=precision,
                                  preferred_element_type=F32)
    return jnp.concatenate([move(2 * p, 0) + move(2 * p + 1, HEAD_DIM) for p in range(N_HEADS // 2)], axis=1)


def mixout_fwd(h1, o_heads, ypre, g_a, g_s, w_glu, b_glu, w_out):
    T = h1.shape[0]
    tm = _tile(T, TOKEN_TILE)

    def body(h_ref, at_ref, yp_ref, ga_ref, gs_ref, wg_ref, bg_ref, wo_ref, h2_ref, mixed_ref):
        yg, _ = _gelu_parts(yp_ref[...])
        gl = yg * jax.nn.sigmoid(_dot(yg.astype(BF16), wg_ref[...]) + bg_ref[...])
        at = _join_heads(at_ref, F32)
        mixed = jnp.concatenate([at * _rms_scale(at) * ga_ref[...], gl * _rms_scale(gl) * gs_ref[...]], axis=1)
        mixed = mixed.astype(BF16)
        mixed_ref[...] = mixed
        h2_ref[...] = h_ref[...] + _dot(mixed, wo_ref[...])

    tok = lambda i: (i, 0)
    fix = lambda i: (0, 0)
    return pl.pallas_call(
        body, name="mixout_fwd", grid=(T // tm,),
        in_specs=[pl.BlockSpec((tm, D_MODEL), tok), pl.BlockSpec((N_HEADS, tm, 128), lambda i: (0, i, 0)),
                  pl.BlockSpec((tm, SSM_W), tok),
                  pl.BlockSpec((1, ATTN_W), fix), pl.BlockSpec((1, SSM_W), fix), pl.BlockSpec((SSM_W, SSM_W), fix),
                  pl.BlockSpec((1, SSM_W), fix), pl.BlockSpec((D_MODEL, D_MODEL), fix)],
        out_specs=[pl.BlockSpec((tm, D_MODEL), tok), pl.BlockSpec((tm, D_MODEL), tok)],
        out_shape=[_sds((T, D_MODEL), F32), _sds((T, D_MODEL), BF16)],
        compiler_params=_params(("arbitrary",)),
    )(h1, o_heads, ypre, g_a, g_s, w_glu, b_glu, w_out)


def mixout_bwd(dh2, o_heads, ypre, g_a, g_s, w_glu, b_glu, w_out, seg):
    T = dh2.shape[0]
    tm = _tile(T, TOKEN_TILE)

    def body(dh_ref, at_ref, yp_ref, ga_ref, gs_ref, wg_ref, bg_ref, wo_ref, seg_ref,
             do_ref, dyp_ref, dpre_ref, yg_ref, dga_ref, dgs_ref, dbg_ref):
        @pl.when(pl.program_id(0) == 0)
        def _():
            dga_ref[...] = jnp.zeros_like(dga_ref)
            dgs_ref[...] = jnp.zeros_like(dgs_ref)
            dbg_ref[...] = jnp.zeros_like(dbg_ref)

        dmix = _dot_nt(dh_ref[...].astype(BF16), wo_ref[...])
        at = _join_heads(at_ref, F32)
        dat, dga = _rms_bwd(dmix[:, :ATTN_W], at, ga_ref[...])
        dga_ref[...] += dga
        delta = _pieces(jnp.dot(dat * at, seg_ref[...], precision=HIGHEST, preferred_element_type=F32))
        datb = dat.astype(BF16)
        for h in range(N_HEADS):
            p, e = divmod(h, 2)
            do_ref[h] = (_head_features(datb[:, 128 * p:128 * (p + 1)], e) + _helper_columns(delta, h, -1.0)).astype(BF16)
        yp = yp_ref[...]
        yg, t = _gelu_parts(yp)
        ygb = yg.astype(BF16)
        yg_ref[...] = ygb
        sg = jax.nn.sigmoid(_dot(ygb, wg_ref[...]) + bg_ref[...])
        dgl, dgs = _rms_bwd(dmix[:, ATTN_W:], yg * sg, gs_ref[...])
        dgs_ref[...] += dgs
        dpre = dgl * yg * sg * (1.0 - sg)
        dbg_ref[...] += jnp.sum(dpre, axis=0, keepdims=True)
        dpb = dpre.astype(BF16)
        dpre_ref[...] = dpb
        dyg = dgl * sg + _dot_nt(dpb, wg_ref[...])
        dyp_ref[...] = dyg * _gelu_grad(yp, t)

    tok = lambda i: (i, 0)
    fix = lambda i: (0, 0)
    heads = pl.BlockSpec((N_HEADS, tm, 128), lambda i: (0, i, 0))
    return pl.pallas_call(
        body, name="mixout_bwd", grid=(T // tm,),
        in_specs=[pl.BlockSpec((tm, D_MODEL), tok), heads, pl.BlockSpec((tm, SSM_W), tok),
                  pl.BlockSpec((1, ATTN_W), fix), pl.BlockSpec((1, SSM_W), fix), pl.BlockSpec((SSM_W, SSM_W), fix),
                  pl.BlockSpec((1, SSM_W), fix), pl.BlockSpec((D_MODEL, D_MODEL), fix), pl.BlockSpec((ATTN_W, 128), fix)],
        out_specs=[heads, pl.BlockSpec((tm, SSM_W), tok), pl.BlockSpec((tm, SSM_W), tok),
                   pl.BlockSpec((tm, SSM_W), tok), pl.BlockSpec((1, ATTN_W), fix),
                   pl.BlockSpec((1, SSM_W), fix), pl.BlockSpec((1, SSM_W), fix)],
        out_shape=[_sds((N_HEADS, T, 128), BF16), _sds((T, SSM_W), F32), _sds((T, SSM_W), BF16), _sds((T, SSM_W), BF16),
                   _sds((1, ATTN_W), F32), _sds((1, SSM_W), F32), _sds((1, SSM_W), F32)],
        compiler_params=_params(("arbitrary",)),
    )(dh2, o_heads, ypre, g_a, g_s, w_glu, b_glu, w_out, seg)


def head_fwd_bwd(h3, p, target, g_ple, g_final, w_gate, w_proj):
    T = h3.shape[0]
    tm = _tile(T, TOKEN_TILE)
    pd = p.shape[1]

    def body(h_ref, p_ref, tg_ref, gp_ref, gf_ref, wg_ref, wp_ref,
             dh_ref, n3_ref, dz_ref, dpp_ref, loss_ref, dgp_ref, dgf_ref):
        @pl.when(pl.program_id(0) == 0)
        def _():
            loss_ref[...] = jnp.zeros_like(loss_ref)
            dgp_ref[...] = jnp.zeros_like(dgp_ref)
            dgf_ref[...] = jnp.zeros_like(dgf_ref)

        x = h_ref[...]
        gp, gf = gp_ref[...], gf_ref[...]
        n3 = (x * _rms_scale(x) * gp).astype(BF16)
        n3_ref[...] = n3
        gate = jax.nn.sigmoid(_dot(n3, wg_ref[...]))
        pp = _dot(p_ref[...].astype(BF16), wp_ref[...])
        h4 = x + gate * pp
        y = h4 * _rms_scale(h4) * gf
        e = y - tg_ref[...]
        tile_loss = jnp.sum(jnp.sum(e * e, axis=1, keepdims=True), axis=0, keepdims=True) * (0.5 / D_MODEL)
        loss_ref[...] += jnp.broadcast_to(tile_loss, loss_ref.shape)
        dh4, dgf = _rms_bwd(e * (1.0 / D_MODEL), h4, gf)
        dgf_ref[...] += dgf
        dzg = dh4 * pp * gate * (1.0 - gate)
        dzb = dzg.astype(BF16)
        dz_ref[...] = dzb
        dpp_ref[...] = (dh4 * gate).astype(BF16)
        dx, dgp = _rms_bwd(_dot_nt(dzb, wg_ref[...]), x, gp)
        dgp_ref[...] += dgp
        dh_ref[...] = dh4 + dx

    tok = lambda i: (i, 0)
    fix = lambda i: (0, 0)
    return pl.pallas_call(
        body, name="head_fwd_bwd", grid=(T // tm,),
        in_specs=[pl.BlockSpec((tm, D_MODEL), tok), pl.BlockSpec((tm, pd), tok), pl.BlockSpec((tm, D_MODEL), tok),
                  pl.BlockSpec((1, D_MODEL), fix), pl.BlockSpec((1, D_MODEL), fix), pl.BlockSpec((D_MODEL, D_MODEL), fix),
                  pl.BlockSpec((pd, D_MODEL), fix)],
        out_specs=[pl.BlockSpec((tm, D_MODEL), tok), pl.BlockSpec((tm, D_MODEL), tok), pl.BlockSpec((tm, D_MODEL), tok),
                   pl.BlockSpec((tm, D_MODEL), tok), pl.BlockSpec((8, 128), fix), pl.BlockSpec((1, D_MODEL), fix),
                   pl.BlockSpec((1, D_MODEL), fix)],
        out_shape=[_sds((T, D_MODEL), F32), _sds((T, D_MODEL), BF16), _sds((T, D_MODEL), BF16), _sds((T, D_MODEL), BF16),
                   _sds((8, 128), F32), _sds((1, D_MODEL), F32), _sds((1, D_MODEL), F32)],
        compiler_params=_params(("arbitrary",)),
    )(h3, p, target, g_ple, g_final, w_gate, w_proj)


def _row_tile(rows, cols, n_arrays):
    lanes = -(-cols // 128) * 128
    cap = VMEM_LIMIT // 3 // (2 * n_arrays * lanes * 4)
    best = None
    for t in range(PACK_ALIGN, min(rows, cap) + 1, PACK_ALIGN):
        if rows % t == 0:
            best = t
    assert best is not None, (rows, cols)
    return best


def _adamw_math(w, g, m, v):
    nm = ADAM_B1 * m + (1.0 - ADAM_B1) * g
    nv = ADAM_B2 * v + (1.0 - ADAM_B2) * (g * g)
    c1 = 1.0 - ADAM_B1 ** ADAM_STEP
    c2 = 1.0 - ADAM_B2 ** ADAM_STEP
    return -ADAM_LR * ((nm / c1) / (jnp.sqrt(nv / c2) + ADAM_EPS) + ADAM_WD * w), nm, nv


def adamw(w, g, m, v, name):
    R, C = w.shape
    tr = _row_tile(R, C, 7)

    def body(w_ref, g_ref, m_ref, v_ref, d_ref, nm_ref, nv_ref):
        d_ref[...], nm_ref[...], nv_ref[...] = _adamw_math(w_ref[...], g_ref[...], m_ref[...], v_ref[...])

    spec = pl.BlockSpec((tr, C), lambda i: (i, 0))
    return pl.pallas_call(
        body, name=name, grid=(R // tr,), in_specs=[spec] * 4, out_specs=[spec] * 3,
        out_shape=[_sds((R, C), F32)] * 3, compiler_params=_params(("arbitrary",)),
    )(w, g, m, v)


def join_halves(mine, other, core):
    rh, C = mine.shape
    tr = _row_tile(rh, C, 3)
    nb = rh // tr

    def body(c_ref, m_ref, o_ref, out_ref):
        out_ref[...] = jnp.where((pl.program_id(0) // nb) == c_ref[0], m_ref[...], o_ref[...])

    half = pl.BlockSpec((tr, C), lambda i, c: (i % nb, 0))
    return pl.pallas_call(
        body, name="join_halves",
        grid_spec=pltpu.PrefetchScalarGridSpec(num_scalar_prefetch=1, grid=(2 * nb,), in_specs=[half, half],
                                               out_specs=pl.BlockSpec((tr, C), lambda i, c: (i, 0))),
        out_shape=_sds((2 * rh, C), F32), compiler_params=_params(("arbitrary",)),
    )(core, mine, other)


def pair_sum(g, theirs, core):
    n, R, C = g.shape
    rh = R // 2
    tr = _row_tile(rh, C, 3)
    nb = rh // tr

    def body(c_ref, g_ref, t_ref, o_ref):
        o_ref[...] = (g_ref[...] + t_ref[...]).astype(BF16)

    here = pl.BlockSpec((1, tr, C), lambda j, i, c: (j, i, 0))
    return pl.pallas_call(
        body, name="pair_sum",
        grid_spec=pltpu.PrefetchScalarGridSpec(
            num_scalar_prefetch=1, grid=(n, nb),
            in_specs=[pl.BlockSpec((1, tr, C), lambda j, i, c: (j, c[0] * nb + i, 0)), here], out_specs=here),
        out_shape=_sds((n, rh, C), BF16), compiler_params=_params(("arbitrary", "arbitrary")),
    )(core, g, theirs)


def chip_sum(pair, got, chip):
    _, R, C = pair.shape
    tr = _row_tile(R, C, 5)

    def body(c_ref, p_ref, g0_ref, g1_ref, g2_ref, o_ref):
        f = lambda ref: ref[0].astype(F32)
        o_ref[...] = ((f(p_ref) + f(g0_ref)) + f(g1_ref)) + f(g2_ref)

    slot = lambda k: pl.BlockSpec((1, tr, C), lambda i, c: (k, i, 0))
    return pl.pallas_call(
        body, name="chip_sum",
        grid_spec=pltpu.PrefetchScalarGridSpec(
            num_scalar_prefetch=1, grid=(R // tr,),
            in_specs=[pl.BlockSpec((1, tr, C), lambda i, c: (c[0], i, 0)), slot(0), slot(1), slot(2)],
            out_specs=pl.BlockSpec((tr, C), lambda i, c: (i, 0))),
        out_shape=_sds((R, C), F32), compiler_params=_params(("arbitrary",)),
    )(chip, pair, got, got, got)


_HBM = pl.BlockSpec(memory_space=pltpu.HBM)


def _place():
    x, y, c = lax.axis_index("x"), lax.axis_index("y"), lax.axis_index("c")
    return x, y, c, [(1 - x, y), (x, 1 - y), (1 - x, 1 - y)]


def _spans(rows, n):
    assert rows % PACK_ALIGN == 0
    tiles = rows // PACK_ALIGN
    n = min(n, tiles)
    cuts = [tiles * q // n for q in range(n + 1)]
    return [(cuts[q] * PACK_ALIGN, (cuts[q + 1] - cuts[q]) * PACK_ALIGN) for q in range(n)]


def _remote(src, dst, send_sem, recv_sem, to):
    return pltpu.make_async_remote_copy(src_ref=src, dst_ref=dst, send_sem=send_sem, recv_sem=recv_sem,
                                        device_id=to, device_id_type=MESH)


def allgather_shards(wp):
    R, C = wp.shape
    rh = R // 2
    spans = _spans(rh, COPY_CHUNKS)
    n_sp = len(spans)
    local_spans = _spans(R, 2 * COPY_CHUNKS)

    def body(w_ref, out_ref, send_sems, recv_sems, pass_send, pass_recv, local_sems):
        x, y, c, chips = _place()
        me = 2 * x + y
        local = []
        for q, (o, n) in enumerate(local_spans):
            cp = pltpu.make_async_copy(w_ref.at[pl.ds(o, n), :], out_ref.at[me, pl.ds(o, n), :], local_sems.at[q])
            cp.start()
            local.append(cp)
        sends = []
        for k, (cx, cy) in enumerate(chips):
            for q, (o, n) in enumerate(spans):
                rows = pl.ds(c * rh + o, n)
                cp = _remote(w_ref.at[rows, :], out_ref.at[me, rows, :], send_sems.at[k * n_sp + q],
                             recv_sems.at[k * n_sp + q], (cx, cy, c))
                cp.start()
                sends.append(cp)
        for k, (cx, cy) in enumerate(chips):
            for q, (o, n) in enumerate(spans):
                blk = out_ref.at[2 * cx + cy, pl.ds(c * rh + o, n), :]
                _remote(blk, blk, send_sems.at[k * n_sp + q], recv_sems.at[k * n_sp + q], (cx, cy, c)).wait_recv()
                cp = _remote(blk, blk, pass_send.at[k * n_sp + q], pass_recv.at[k * n_sp + q], (x, y, 1 - c))
                cp.start()
                sends.append(cp)
        for k, (cx, cy) in enumerate(chips):
            for q, (o, n) in enumerate(spans):
                blk = out_ref.at[2 * cx + cy, pl.ds((1 - c) * rh + o, n), :]
                _remote(blk, blk, pass_send.at[k * n_sp + q], pass_recv.at[k * n_sp + q], (x, y, 1 - c)).wait_recv()
        for cp in sends:
            cp.wait_send()
        for cp in local:
            cp.wait()

    sems = pltpu.SemaphoreType.DMA((3 * n_sp,))
    return pl.pallas_call(
        body, name="allgather_shards", in_specs=[_HBM], out_specs=_HBM, out_shape=_sds((4, R, C), wp.dtype),
        scratch_shapes=[sems, sems, sems, sems, pltpu.SemaphoreType.DMA((len(local_spans),))],
    )(wp)


def sibling_split(g):
    n_sl, R, C = g.shape
    rh = R // 2
    spans = _spans(rh, COPY_CHUNKS)
    n_sp = len(spans)

    def body(g_ref, got_ref, send_sems, recv_sems):
        x, y, c, _ = _place()
        copies = []
        for j in range(n_sl):
            for q, (o, n) in enumerate(spans):
                cp = _remote(g_ref.at[j, pl.ds((1 - c) * rh + o, n), :], got_ref.at[j, pl.ds(o, n), :],
                             send_sems.at[j * n_sp + q], recv_sems.at[j * n_sp + q], (x, y, 1 - c))
                cp.start()
                copies.append(cp)
        for cp in copies:
            cp.wait()

    sems = pltpu.SemaphoreType.DMA((n_sl * n_sp,))
    return pl.pallas_call(
        body, name="sibling_split", in_specs=[_HBM], out_specs=_HBM, out_shape=_sds((n_sl, rh, C), g.dtype),
        scratch_shapes=[sems, sems],
    )(g)


def chip_exchange(p):
    _, R, C = p.shape
    spans = _spans(R, COPY_CHUNKS)
    n_sp = len(spans)

    def body(p_ref, buf_ref, send_sems, recv_sems):
        x, y, c, chips = _place()
        sends = []
        for k, (cx, cy) in enumerate(chips):
            for q, (o, n) in enumerate(spans):
                cp = _remote(p_ref.at[2 * cx + cy, pl.ds(o, n), :], buf_ref.at[k, pl.ds(o, n), :],
                             send_sems.at[k * n_sp + q], recv_sems.at[k * n_sp + q], (cx, cy, c))
                cp.start()
                sends.append(cp)
        for cp in sends:
            cp.wait()

    sems = pltpu.SemaphoreType.DMA((3 * n_sp,))
    return pl.pallas_call(
        body, name="chip_exchange", in_specs=[_HBM], out_specs=_HBM, out_shape=_sds((3, R, C), p.dtype),
        scratch_shapes=[sems, sems],
    )(p)


def sibling_swap(half):
    R, C = half.shape
    spans = _spans(R, COPY_CHUNKS)

    def body(h_ref, got_ref, send_sems, recv_sems):
        x, y, c, _ = _place()
        copies = []
        for q, (o, n) in enumerate(spans):
            cp = _remote(h_ref.at[pl.ds(o, n), :], got_ref.at[pl.ds(o, n), :], send_sems.at[q], recv_sems.at[q], (x, y, 1 - c))
            cp.start()
            copies.append(cp)
        for cp in copies:
            cp.wait()

    sems = pltpu.SemaphoreType.DMA((len(spans),))
    return pl.pallas_call(
        body, name="sibling_swap", in_specs=[_HBM], out_specs=_HBM, out_shape=_sds((R, C), half.dtype),
        scratch_shapes=[sems, sems],
    )(half)


def allreduce_small(v):
    R, C = v.shape

    def body(v_ref, out_ref, buf, send_sems, recv_sems):
        x, y, c, _ = _place()
        me = 4 * x + 2 * y + c
        buf[me] = v_ref[...]
        flips = [((k >> 2) & 1, (k >> 1) & 1, k & 1) for k in range(1, 8)]
        sends = []
        for k, (fx, fy, fc) in enumerate(flips):
            to = (1 - x if fx else x, 1 - y if fy else y, 1 - c if fc else c)
            cp = _remote(v_ref, buf.at[me], send_sems.at[k], recv_sems.at[k], to)
            cp.start()
            sends.append(cp)
        for k, (fx, fy, fc) in enumerate(flips):
            px, py, pc = (1 - x if fx else x, 1 - y if fy else y, 1 - c if fc else c)
            blk = buf.at[4 * px + 2 * py + pc]
            _remote(blk, blk, send_sems.at[k], recv_sems.at[k], (px, py, pc)).wait_recv()
        for cp in sends:
            cp.wait_send()
        acc = buf[0]
        for s in range(1, 8):
            acc = acc + buf[s]
        out_ref[...] = acc

    vm = pl.BlockSpec(memory_space=pltpu.VMEM)
    return pl.pallas_call(
        body, name="allreduce_small", in_specs=[vm], out_specs=vm, out_shape=_sds((R, C), F32),
        scratch_shapes=[pltpu.VMEM((8, R, C), F32), pltpu.SemaphoreType.DMA((7,)), pltpu.SemaphoreType.DMA((7,))],
        compiler_params=pltpu.CompilerParams(vmem_limit_bytes=VMEM_LIMIT),
    )(v)


def _rows_of(shape):
    return shape[0] * shape[1] // PACK_COLS


def _slot_rows(shape):
    return -(-_rows_of(shape) // PACK_ALIGN) * PACK_ALIGN


def _pack_shards(shards, dtype):
    parts = []
    for name, shape, _ in BIG:
        part = shards[name].reshape(_rows_of(shape), PACK_COLS).astype(dtype)
        parts.append(jnp.pad(part, ((0, _slot_rows(shape) - part.shape[0]), (0, 0))))
    used = sum(p.shape[0] for p in parts)
    parts.append(jnp.zeros((PACK_ROWS - used, PACK_COLS), dtype))
    return jnp.concatenate(parts, axis=0)


def _unpack_gathered(ag):
    out, off = {}, 0
    for name, shape, axis in BIG:
        r = _rows_of(shape)
        piece = ag[:, off:off + r, :].reshape((4,) + shape)
        off += _slot_rows(shape)
        if axis == 0:
            out[name] = piece.reshape(4 * shape[0], shape[1])
        else:
            out[name] = piece.transpose(1, 0, 2).reshape(shape[0], 4 * shape[1])
    return out


def _pack_full_grads(grads):
    parts = []
    for name, shape, axis in BIG:
        g = grads[name]
        if axis == 0:
            piece = g.reshape((4,) + shape)
        else:
            piece = g.reshape(shape[0], 4, shape[1]).transpose(1, 0, 2)
        piece = piece.reshape(4, _rows_of(shape), PACK_COLS)
        parts.append(jnp.pad(piece, ((0, 0), (0, _slot_rows(shape) - piece.shape[1]), (0, 0))))
    used = sum(p.shape[1] for p in parts)
    parts.append(jnp.zeros((4, PACK_ROWS - used, PACK_COLS), F32))
    return jnp.concatenate(parts, axis=1)


def _unpack_shards(packed):
    out, off = {}, 0
    for name, shape, _ in BIG:
        r = _rows_of(shape)
        out[name] = packed[off:off + r].reshape((1,) + shape)
        off += _slot_rows(shape)
    return out


def _pack_small(vals, extra=None):
    parts = [vals[name].reshape(-1) for name, _ in SMALL]
    used = sum(p.shape[0] for p in parts)
    if extra is not None:
        parts.append(extra.reshape(1))
        used += 1
    parts.append(jnp.zeros((SMALL_ROWS * 128 - used,), F32))
    return jnp.concatenate(parts).reshape(SMALL_ROWS, 128)


def _unpack_small(packed):
    flat = packed.reshape(-1)
    out, off = {}, 0
    for name, shape in SMALL:
        n = math.prod(shape)
        out[name] = flat[off:off + n].reshape(shape)
        off += n
    return out, flat[off]


def _permute_time(a):
    T, n = a.shape
    return a.reshape(8, T // 8, n).transpose(1, 0, 2).reshape(T, n)


def _unpermute_time(a):
    T, n = a.shape
    return a.reshape(T // 8, 8, n).transpose(1, 0, 2).reshape(T, n)


def _discretize(a_re, a_im, log_dt, b_re, b_im):
    dt = jnp.exp(log_dt)[:, None]
    decay = jnp.exp(dt * a_re)
    abar_r = decay * jnp.cos(dt * a_im)
    abar_i = decay * jnp.sin(dt * a_im)
    nr, ni = abar_r - 1.0, abar_i
    den = a_re * a_re + a_im * a_im
    fr = (nr * a_re + ni * a_im) / den
    fi = (ni * a_re - nr * a_im) / den
    bbar_r = fr[..., None] * b_re - fi[..., None] * b_im
    bbar_i = fr[..., None] * b_im + fi[..., None] * b_re
    return abar_r, abar_i, bbar_r, bbar_i


def _input_matrix(bbar_r, bbar_i):
    eye = jnp.eye(N_GROUPS, dtype=F32)
    blk = lambda b: jnp.einsum("ghp,gk->ghkp", b.transpose(0, 2, 1), eye).reshape(SSM_W, STATE_W)
    return jnp.concatenate([blk(bbar_r), blk(bbar_i)], axis=1)


def _output_matrix(c_re, c_im):
    eye = jnp.eye(N_GROUPS, dtype=F32)
    blk = lambda cm: jnp.einsum("ghp,gk->gpkh", cm, eye).reshape(STATE_W, SSM_W)
    return jnp.concatenate([blk(c_re), -blk(c_im)], axis=0)


def _state_power(ar, ai, n):
    steps = int(round(math.log2(n)))
    assert 1 << steps == n
    for _ in range(steps):
        ar, ai = ar * ar - ai * ai, 2.0 * ar * ai
    return ar, ai


def kernel(x, p, g_ffn1, w1_a, w3_a, w2_a, g_mix, w_in, b_f, a_re, a_im, log_dt, b_re, b_im, c_re, c_im, d_skip, w_glu, b_glu, g_attn_out, g_ssm_out, w_out, g_ffn2, w1_b, w3_b, w2_b, g_ple, w_ple_gate, w_ple_proj, g_final, loss_target, m_g_ffn1, m_w1_a, m_w3_a, m_w2_a, m_g_mix, m_w_in, m_b_f, m_a_re, m_a_im, m_log_dt, m_b_re, m_b_im, m_c_re, m_c_im, m_d_skip, m_w_glu, m_b_glu, m_g_attn_out, m_g_ssm_out, m_w_out, m_g_ffn2, m_w1_b, m_w3_b, m_w2_b, m_g_ple, m_w_ple_gate, m_w_ple_proj, m_g_final, v_g_ffn1, v_w1_a, v_w3_a, v_w2_a, v_g_mix, v_w_in, v_b_f, v_a_re, v_a_im, v_log_dt, v_b_re, v_b_im, v_c_re, v_c_im, v_d_skip, v_w_glu, v_b_glu, v_g_attn_out, v_g_ssm_out, v_w_out, v_g_ffn2, v_w1_b, v_w3_b, v_w2_b, v_g_ple, v_w_ple_gate, v_w_ple_proj, v_g_final):
    args = dict(locals())
    weights = {n: args[n] for n in WEIGHT_ORDER}
    moms = {n: args["m_" + n] for n in WEIGHT_ORDER}
    vars_ = {n: args["v_" + n] for n in WEIGHT_ORDER}
    T = x.shape[1]
    x2, p2, tgt = x[0], p[0, 0], loss_target[0]

    full = _unpack_gathered(allgather_shards(_pack_shards({n: weights[n][0] for n, _, _ in BIG}, BF16)))
    loss_part, dx, grads = _local_step(x2, p2, tgt, {n: weights[n] for n, _ in SMALL}, full)
    return _reduce_and_update(weights, moms, vars_, loss_part, dx, grads)


def _local_step(x2, p2, tgt, sm, full):
    T = x2.shape[0]
    (g_ffn1, g_mix, b_f, a_re, a_im, log_dt, b_re, b_im, c_re, c_im, d_skip, b_glu, g_attn_out, g_ssm_out, g_ffn2, g_ple,
     g_final) = (sm[n] for n, _ in SMALL)
    w_in_f = full["w_in"]
    w_in_r = jnp.concatenate([w_in_f[:, :ATTN_W] * QK_SCALE, w_in_f[:, ATTN_W:3 * ATTN_W], w_in_f[:, 3 * ATTN_W + N_HEADS:],
                              w_in_f[:, 3 * ATTN_W:3 * ATTN_W + N_HEADS], jnp.zeros((D_MODEL, 128 - N_HEADS), BF16)], axis=1)
    b_f_pad = jnp.pad(b_f, ((0, 0), (0, 128 - N_HEADS)))

    disc_in = (a_re[0], a_im[0], log_dt[0], b_re[0], b_im[0])
    (abar_r, abar_i, bbar_r, bbar_i), disc_vjp = jax.vjp(_discretize, *disc_in)
    wb = _input_matrix(bbar_r, bbar_i)
    cbd = _output_matrix(c_re[0], c_im[0])
    ar, ai = abar_r.reshape(1, STATE_W), abar_i.reshape(1, STATE_W)
    alr, ali = _state_power(ar, ai, T // 8)
    dvec = d_skip.reshape(1, SSM_W)
    wb16, cbd16 = wb.astype(BF16), cbd.astype(BF16)

    h1, a1a, a3a, n1 = ffn_fwd(x2, g_ffn1, full["w1_a"], full["w3_a"], full["w2_a"], "ffn_a_fwd")
    u, qkv, s_in, fz, cum = mixin_fwd(h1, g_mix, w_in_r, b_f_pad)
    q_aug, k_aug, v_aug = heads_in(qkv, cum)
    o_heads, q_bwd = attn_fwd(q_aug, k_aug, v_aug)
    s_perm = _permute_time(s_in)
    y_perm, xs = ssm_fwd(s_perm, wb16, cbd16, ar, ai, alr, ali, dvec)
    ypre = _unpermute_time(y_perm)
    h2, mixed = mixout_fwd(h1, o_heads, ypre, g_attn_out, g_ssm_out, full["w_glu"], b_glu, full["w_out"])
    h3, a1b, a3b, n2 = ffn_fwd(h2, g_ffn2, full["w1_b"], full["w3_b"], full["w2_b"], "ffn_b_fwd")

    dh3, n3, dzg, dpp, loss_part, dg_ple, dg_final = head_fwd_bwd(
        h3, p2, tgt, g_ple, g_final.reshape(1, D_MODEL), full["w_ple_gate"], full["w_ple_proj"])
    grads = {"g_ple": dg_ple, "g_final": dg_final.reshape(D_MODEL)}
    grads["w_ple_gate"] = mm_tn(n3, dzg, "dw_ple_gate")
    grads["w_ple_proj"] = mm_tn(p2, dpp, "dw_ple_proj")

    dh2, da1, da3, act, grads["g_ffn2"] = ffn_bwd(h2, g_ffn2, dh3, a1b, a3b, full["w1_b"], full["w3_b"], full["w2_b"], "ffn_b_bwd")
    grads["w1_b"] = mm_tn(n2, da1, "dw1_b")
    grads["w3_b"] = mm_tn(n2, da3, "dw3_b")
    grads["w2_b"] = mm_tn(act, dh3, "dw2_b", scale=0.5)

    seg = (jnp.arange(ATTN_W)[:, None] // HEAD_DIM == jnp.arange(128)[None, :]).astype(F32)
    do_aug, dypre, dpre, yg, grads["g_attn_out"], grads["g_ssm_out"], grads["b_glu"] = mixout_bwd(
        dh2, o_heads, ypre, g_attn_out, g_ssm_out, full["w_glu"], b_glu, full["w_out"], seg)
    grads["w_out"] = mm_tn(mixed, dh2, "dw_out")
    grads["w_glu"] = mm_tn(yg, dpre, "dw_glu")

    dq_aug, dk_aug, dv_aug, dck, dcq = attn_bwd(q_bwd, k_aug, v_aug, do_aug)
    dc = jnp.pad((dck.reshape(N_HEADS, T) + dcq.reshape(N_HEADS, T)).T, ((0, 0), (0, 128 - N_HEADS)))

    dy_perm = _permute_time(dypre)
    du_perm, gs, d_a, dd = ssm_bwd(dy_perm, s_perm, xs, cbd16.T, wb16.T, ar, ai, alr, ali, dvec)
    ds_in = _unpermute_time(du_perm)
    d_wb = mm_tn(s_perm, gs, "dw_ssm_in")
    d_cbd = mm_tn(xs, dy_perm, "dw_ssm_out")
    diag_in = lambda m: jnp.einsum("ghgp->ghp", m.reshape(N_GROUPS, GROUP_CH, N_GROUPS, N_STATE)).transpose(0, 2, 1)
    diag_out = lambda m: jnp.einsum("gpgh->gph", m.reshape(N_GROUPS, N_STATE, N_GROUPS, GROUP_CH)).transpose(0, 2, 1)
    d_abar_r = jnp.sum(d_a[:, :STATE_W], axis=0).reshape(N_GROUPS, N_STATE)
    d_abar_i = jnp.sum(d_a[:, STATE_W:], axis=0).reshape(N_GROUPS, N_STATE)
    d_disc = disc_vjp((d_abar_r, d_abar_i, diag_in(d_wb[:, :STATE_W]), diag_in(d_wb[:, STATE_W:])))
    for name, val in zip(("a_re", "a_im", "log_dt", "b_re", "b_im"), d_disc):
        grads[name] = val[None]
    grads["c_re"] = diag_out(d_cbd[:STATE_W])[None]
    grads["c_im"] = -diag_out(d_cbd[STATE_W:])[None]
    grads["d_skip"] = dd.reshape(1, N_GROUPS, GROUP_CH)

    dh1, dz, grads["g_mix"], dbf = mixin_bwd(dh2, h1, g_mix, w_in_r, dq_aug, dk_aug, dv_aug, ds_in, dc, fz)
    grads["b_f"] = dbf[:, :N_HEADS]
    d_w_in_r = mm_tn(u, dz, "dw_in")
    grads["w_in"] = jnp.concatenate([d_w_in_r[:, :ATTN_W] * QK_SCALE, d_w_in_r[:, ATTN_W:3 * ATTN_W],
                                     d_w_in_r[:, 3 * ATTN_W + SSM_W:3 * ATTN_W + SSM_W + N_HEADS],
                                     d_w_in_r[:, 3 * ATTN_W:3 * ATTN_W + SSM_W]], axis=1)

    dx, da1, da3, act, grads["g_ffn1"] = ffn_bwd(x2, g_ffn1, dh1, a1a, a3a, full["w1_a"], full["w3_a"], full["w2_a"], "ffn_a_bwd")
    grads["w1_a"] = mm_tn(n1, da1, "dw1_a")
    grads["w3_a"] = mm_tn(n1, da3, "dw3_a")
    grads["w2_a"] = mm_tn(act, dh1, "dw2_a", scale=0.5)
    return loss_part, dx, grads


def _reduce_and_update(weights, moms, vars_, loss_part, dx, grads):
    core = lax.axis_index("c").astype(jnp.int32).reshape(1)
    chip = (2 * lax.axis_index("x") + lax.axis_index("y")).astype(jnp.int32).reshape(1)
    packed = _pack_full_grads(grads)
    pair = pair_sum(packed, sibling_split(packed), core)
    half = chip_sum(pair, chip_exchange(pair), chip)
    g_out = _unpack_shards(join_halves(half, sibling_swap(half), core))
    d_out, m_out, v_out = {}, {}, {}
    for n, _, _ in BIG:
        d, m, v = adamw(weights[n][0], g_out[n][0], moms[n][0], vars_[n][0], "adamw_" + n)
        d_out[n], m_out[n], v_out[n] = d[None], m[None], v[None]

    small = allreduce_small(_pack_small({n: grads[n] for n, _ in SMALL}, extra=loss_part[0, 0]))
    d_small, m_small, v_small = adamw(_pack_small(weights), small, _pack_small(moms), _pack_small(vars_), "adamw_small")

    g_small, loss = _unpack_small(small)
    g_out.update(g_small)
    outs = []
    for big, sm in ((d_out, d_small), (m_out, m_small), (v_out, v_small)):
        o, _ = _unpack_small(sm)
        o.update(big)
        outs.append(o)
    result = [loss, dx[None]] + [g_out[n] for n in WEIGHT_ORDER]
    for o in outs:
        result += [o[n] for n in WEIGHT_ORDER]
    return tuple(result)
```

```python
import functools
import math

import jax
import jax.numpy as jnp
from jax import lax
from jax.experimental import pallas as pl
from jax.experimental.pallas import tpu as pltpu

F32 = jnp.float32
BF16 = jnp.bfloat16

D_MODEL = 1024
D_FF = 2816
N_HEADS = 8
HEAD_DIM = 64
ATTN_W = 512
SSM_W = 512
N_GROUPS = 32
N_STATE = 64
GROUP_CH = 16
STATE_W = N_GROUPS * N_STATE
Z_COLS = 2176
QK_SCALE = 0.125
EPS = 1e-6

ADAM_LR = 0.001
ADAM_B1 = 0.9
ADAM_B2 = 0.999
ADAM_EPS = 1e-08
ADAM_WD = 0.01
ADAM_STEP = 10

TOKEN_TILE = 512
FFN_TOKEN_TILE = 256
FF_CHUNK = 1408
MM_K_TILE = 2048
ATTN_TILE = 512
ATTN_QUERY_GROUP = 512
SCAN_STEPS = 32
SCAN_LANES = 512
VMEM_LIMIT = 48 * 1024 * 1024
FFN_VMEM_LIMIT = 56 * 1024 * 1024
COPY_CHUNKS = 4

NT_DIMS = (((1,), (1,)), ((), ()))
TN_DIMS = (((0,), (0,)), ((), ()))
HIGHEST = lax.Precision.HIGHEST
MESH = pl.DeviceIdType.MESH

BIG = (
    ("w1_a", (1024, 704), 1), ("w3_a", (1024, 704), 1), ("w2_a", (704, 1024), 0),
    ("w_in", (1024, 514), 1), ("w_glu", (128, 512), 0), ("w_out", (256, 1024), 0),
    ("w1_b", (1024, 704), 1), ("w3_b", (1024, 704), 1), ("w2_b", (704, 1024), 0),
    ("w_ple_gate", (256, 1024), 0), ("w_ple_proj", (256, 256), 1),
)
PACK_COLS = 1024
PACK_ALIGN = 16
PACK_ROWS = 5408
SMALL = (
    ("g_ffn1", (1, 1024)), ("g_mix", (1, 1024)), ("b_f", (1, 8)), ("a_re", (1, 32, 64)), ("a_im", (1, 32, 64)),
    ("log_dt", (1, 32)), ("b_re", (1, 32, 64, 16)), ("b_im", (1, 32, 64, 16)), ("c_re", (1, 32, 16, 64)),
    ("c_im", (1, 32, 16, 64)), ("d_skip", (1, 32, 16)), ("b_glu", (1, 512)), ("g_attn_out", (1, 512)),
    ("g_ssm_out", (1, 512)), ("g_ffn2", (1, 1024)), ("g_ple", (1, 1024)), ("g_final", (1024,)),
)
SMALL_ROWS = 1120
WEIGHT_ORDER = ("g_ffn1", "w1_a", "w3_a", "w2_a", "g_mix", "w_in", "b_f", "a_re", "a_im", "log_dt", "b_re", "b_im",
                "c_re", "c_im", "d_skip", "w_glu", "b_glu", "g_attn_out", "g_ssm_out", "w_out", "g_ffn2", "w1_b",
                "w3_b", "w2_b", "g_ple", "w_ple_gate", "w_ple_proj", "g_final")


def _params(sem=None, vmem=VMEM_LIMIT):
    kw = dict(vmem_limit_bytes=vmem)
    if sem is not None:
        kw["dimension_semantics"] = sem
    return pltpu.CompilerParams(**kw)


def _sds(shape, dtype):
    return jax.ShapeDtypeStruct(shape, dtype)


def _tile(n, pref):
    t = min(n, pref)
    assert n % t == 0, (n, pref)
    return t


def _rms_scale(x):
    return lax.rsqrt(jnp.mean(x * x, axis=-1, keepdims=True) + EPS)


def _rms_bwd(dy, x, g):
    r = _rms_scale(x)
    xh = x * r
    dxh = dy * g
    dx = r * (dxh - xh * jnp.mean(dxh * xh, axis=-1, keepdims=True))
    return dx, jnp.sum(dy * xh, axis=0, keepdims=True)


def _dot(a, b):
    return jnp.dot(a, b, preferred_element_type=F32)


def _dot_nt(a, b):
    return lax.dot_general(a, b, NT_DIMS, preferred_element_type=F32)


def _dot_tn(a, b):
    return lax.dot_general(a, b, TN_DIMS, preferred_element_type=F32)


_GELU_C = math.sqrt(2.0 / math.pi)


def _gelu_parts(x):
    t = jnp.tanh(_GELU_C * (x + 0.044715 * x * x * x))
    return 0.5 * x * (1.0 + t), t


def _gelu_grad(x, t):
    return 0.5 * (1.0 + t) + 0.5 * x * (1.0 - t * t) * _GELU_C * (1.0 + 3.0 * 0.044715 * x * x)


def _resident(shape):
    return pl.BlockSpec(shape, lambda i: (0,) * len(shape), pipeline_mode=pl.Buffered(1))


def ffn_fwd(h, g, w1, w3, w2, name):
    T = h.shape[0]
    tm = _tile(T, FFN_TOKEN_TILE)

    def body(h_ref, g_ref, w1_ref, w3_ref, w2_ref, ho_ref, a1_ref, a3_ref, n_ref):
        x = h_ref[...]
        n = (x * _rms_scale(x) * g_ref[...]).astype(BF16)
        n_ref[...] = n
        out = x
        for lo in range(0, D_FF, FF_CHUNK):
            cols = slice(lo, lo + FF_CHUNK)
            a1 = _dot(n, w1_ref[:, cols])
            a3 = _dot(n, w3_ref[:, cols])
            a1_ref[:, cols] = a1.astype(BF16)
            a3_ref[:, cols] = a3.astype(BF16)
            act = (a1 * jax.nn.sigmoid(a1) * a3).astype(BF16)
            out = out + 0.5 * _dot(act, w2_ref[cols, :])
        ho_ref[...] = out

    tok = lambda i: (i, 0)
    return pl.pallas_call(
        body, name=name, grid=(T // tm,),
        in_specs=[pl.BlockSpec((tm, D_MODEL), tok), _resident((1, D_MODEL)), _resident((D_MODEL, D_FF)),
                  _resident((D_MODEL, D_FF)), _resident((D_FF, D_MODEL))],
        out_specs=[pl.BlockSpec((tm, D_MODEL), tok), pl.BlockSpec((tm, D_FF), tok), pl.BlockSpec((tm, D_FF), tok),
                   pl.BlockSpec((tm, D_MODEL), tok)],
        out_shape=[_sds((T, D_MODEL), F32), _sds((T, D_FF), BF16), _sds((T, D_FF), BF16), _sds((T, D_MODEL), BF16)],
        compiler_params=_params(("arbitrary",), FFN_VMEM_LIMIT),
    )(h, g, w1, w3, w2)


def ffn_bwd(h, g, dho, a1, a3, w1, w3, w2, name):
    T = h.shape[0]
    tm = _tile(T, FFN_TOKEN_TILE)

    def body(h_ref, g_ref, dho_ref, a1_ref, a3_ref, w1_ref, w3_ref, w2_ref, dhi_ref, da1_ref, da3_ref, act_ref, dg_ref):
        @pl.when(pl.program_id(0) == 0)
        def _():
            dg_ref[...] = jnp.zeros_like(dg_ref)

        dho = dho_ref[...]
        dhb = (0.5 * dho).astype(BF16)
        dn = None
        for lo in range(0, D_FF, FF_CHUNK):
            cols = slice(lo, lo + FF_CHUNK)
            a1v = a1_ref[:, cols].astype(F32)
            a3v = a3_ref[:, cols].astype(F32)
            s = jax.nn.sigmoid(a1v)
            sl = a1v * s
            dact = _dot_nt(dhb, w2_ref[cols, :])
            act_ref[:, cols] = (sl * a3v).astype(BF16)
            da1 = (dact * a3v * s * (1.0 + a1v * (1.0 - s))).astype(BF16)
            da3 = (dact * sl).astype(BF16)
            da1_ref[:, cols] = da1
            da3_ref[:, cols] = da3
            part = _dot_nt(da1, w1_ref[:, cols]) + _dot_nt(da3, w3_ref[:, cols])
            dn = part if dn is None else dn + part
        dx, dg = _rms_bwd(dn, h_ref[...], g_ref[...])
        dg_ref[...] += dg
        dhi_ref[...] = dho + dx

    tok = lambda i: (i, 0)
    return pl.pallas_call(
        body, name=name, grid=(T // tm,),
        in_specs=[pl.BlockSpec((tm, D_MODEL), tok), _resident((1, D_MODEL)), pl.BlockSpec((tm, D_MODEL), tok),
                  pl.BlockSpec((tm, D_FF), tok), pl.BlockSpec((tm, D_FF), tok), _resident((D_MODEL, D_FF)),
                  _resident((D_MODEL, D_FF)), _resident((D_FF, D_MODEL))],
        out_specs=[pl.BlockSpec((tm, D_MODEL), tok), pl.BlockSpec((tm, D_FF), tok), pl.BlockSpec((tm, D_FF), tok),
                   pl.BlockSpec((tm, D_FF), tok), pl.BlockSpec((1, D_MODEL), lambda i: (0, 0))],
        out_shape=[_sds((T, D_MODEL), F32), _sds((T, D_FF), BF16), _sds((T, D_FF), BF16), _sds((T, D_FF), BF16),
                   _sds((1, D_MODEL), F32)],
        compiler_params=_params(("arbitrary",), FFN_VMEM_LIMIT),
    )(h, g, dho, a1, a3, w1, w3, w2)


def mm_tn(a, b, name, scale=1.0, a_cols=None, b_cols=None):
    T = a.shape[0]
    a_off, M = a_cols or (0, a.shape[1])
    b_off, N = b_cols or (0, b.shape[1])
    bm = 512 if M % 512 == 0 else (1408 if M == 2816 else 256)
    bn = N if N in (2176, 1408) else (1408 if N == 2816 else min(N, 1024))
    tk = _tile(T, MM_K_TILE)
    row_bytes = 2 * (bm * a.dtype.itemsize + bn * b.dtype.itemsize)
    while tk > TOKEN_TILE and tk * row_bytes > VMEM_LIMIT // 3:
        tk //= 2
    assert M % bm == 0 and N % bn == 0 and T % tk == 0 and a_off % bm == 0 and b_off % bn == 0
    n_k = T // tk
    m0, n0 = a_off // bm, b_off // bn

    def body(a_ref, b_ref, o_ref):
        k = pl.program_id(2)

        @pl.when(k == 0)
        def _():
            o_ref[...] = jnp.zeros_like(o_ref)

        o_ref[...] += _dot_tn(a_ref[...].astype(BF16), b_ref[...].astype(BF16))

        if scale != 1.0:
            @pl.when(k == n_k - 1)
            def _():
                o_ref[...] = o_ref[...] * scale

    return pl.pallas_call(
        body, name=name, grid=(M // bm, N // bn, n_k),
        in_specs=[pl.BlockSpec((tk, bm), lambda m, n, k: (k, m0 + m)), pl.BlockSpec((tk, bn), lambda m, n, k: (k, n0 + n))],
        out_specs=pl.BlockSpec((bm, bn), lambda m, n, k: (m, n)),
        out_shape=_sds((M, N), F32),
        compiler_params=_params(("arbitrary", "arbitrary", "arbitrary")),
    )(a, b)


def mixin_fwd(h1, g, w_in_r, b_f_pad):
    T = h1.shape[0]
    tm = _tile(T, TOKEN_TILE)

    def body(h_ref, g_ref, w_ref, bf_ref, u_ref, qkv_ref, s_ref, fz_ref, c_ref, carry):
        @pl.when(pl.program_id(0) == 0)
        def _():
            carry[...] = jnp.zeros_like(carry)

        x = h_ref[...]
        u = (x * _rms_scale(x) * g_ref[...]).astype(BF16)
        u_ref[...] = u
        z = _dot(u, w_ref[...])
        qkv_ref[...] = z[:, :3 * ATTN_W].astype(BF16)
        s_ref[...] = z[:, 3 * ATTN_W:3 * ATTN_W + SSM_W]
        fz = z[:, 3 * ATTN_W + SSM_W:] + bf_ref[...]
        fz_ref[...] = fz
        lane = lax.broadcasted_iota(jnp.int32, fz.shape, 1)
        logf = jnp.where(lane < N_HEADS, jnp.minimum(fz, 0.0) - jnp.log(1.0 + jnp.exp(-jnp.abs(fz))), 0.0)
        row = lax.broadcasted_iota(jnp.int32, (tm, tm), 0)
        col = lax.broadcasted_iota(jnp.int32, (tm, tm), 1)
        tri = (col <= row).astype(F32)
        cs = jnp.dot(tri, logf, precision=HIGHEST, preferred_element_type=F32) + carry[0:1, :]
        c_ref[...] = cs
        carry[...] = jnp.broadcast_to(cs[tm - 1:tm, :], carry.shape)

    tok = lambda i: (i, 0)
    fix = lambda i: (0, 0)
    return pl.pallas_call(
        body, name="mixin_fwd", grid=(T // tm,),
        in_specs=[pl.BlockSpec((tm, D_MODEL), tok), pl.BlockSpec((1, D_MODEL), fix),
                  pl.BlockSpec((D_MODEL, Z_COLS), fix), pl.BlockSpec((1, 128), fix)],
        out_specs=[pl.BlockSpec((tm, D_MODEL), tok), pl.BlockSpec((tm, 3 * ATTN_W), tok), pl.BlockSpec((tm, SSM_W), tok),
                   pl.BlockSpec((tm, 128), tok), pl.BlockSpec((tm, 128), tok)],
        out_shape=[_sds((T, D_MODEL), BF16), _sds((T, 3 * ATTN_W), BF16), _sds((T, SSM_W), F32),
                   _sds((T, 128), F32), _sds((T, 128), F32)],
        scratch_shapes=[pltpu.VMEM((8, 128), F32)],
        compiler_params=_params(("arbitrary",)),
    )(h1, g, w_in_r, b_f_pad)


def mixin_bwd(dh2, h1, g, w_in_r, dq, dk, dv, ds, dc, fz):
    T = h1.shape[0]
    tm = _tile(T, TOKEN_TILE)
    n_t = T // tm

    def body(dh2_ref, h_ref, g_ref, w_ref, dq_ref, dk_ref, dv_ref, ds_ref, dc_ref, fz_ref,
             dh1_ref, dz_ref, dg_ref, dbf_ref, carry):
        @pl.when(pl.program_id(0) == 0)
        def _():
            carry[...] = jnp.zeros_like(carry)
            dg_ref[...] = jnp.zeros_like(dg_ref)
            dbf_ref[...] = jnp.zeros_like(dbf_ref)

        row = lax.broadcasted_iota(jnp.int32, (tm, tm), 0)
        col = lax.broadcasted_iota(jnp.int32, (tm, tm), 1)
        tri = (col >= row).astype(F32)
        dlogf = jnp.dot(tri, dc_ref[...], precision=HIGHEST, preferred_element_type=F32) + carry[0:1, :]
        carry[...] = jnp.broadcast_to(dlogf[0:1, :], carry.shape)
        dfz = dlogf * jax.nn.sigmoid(-fz_ref[...])
        dbf_ref[...] += jnp.sum(dfz, axis=0, keepdims=True)
        dz = jnp.concatenate([_join_heads(dq_ref, BF16), _join_heads(dk_ref, BF16), _join_heads(dv_ref, BF16),
                              ds_ref[...], dfz], axis=1).astype(BF16)
        dz_ref[...] = dz
        du = _dot_nt(dz, w_ref[...])
        dx, dg = _rms_bwd(du, h_ref[...], g_ref[...])
        dg_ref[...] += dg
        dh1_ref[...] = dh2_ref[...] + dx

    tok = lambda i: (n_t - 1 - i, 0)
    fix = lambda i: (0, 0)
    heads = pl.BlockSpec((N_HEADS, tm, 128), lambda i: (0, n_t - 1 - i, 0))
    return pl.pallas_call(
        body, name="mixin_bwd", grid=(n_t,),
        in_specs=[pl.BlockSpec((tm, D_MODEL), tok), pl.BlockSpec((tm, D_MODEL), tok), pl.BlockSpec((1, D_MODEL), fix),
                  pl.BlockSpec((D_MODEL, Z_COLS), fix), heads, heads, heads, pl.BlockSpec((tm, SSM_W), tok),
                  pl.BlockSpec((tm, 128), tok), pl.BlockSpec((tm, 128), tok)],
        out_specs=[pl.BlockSpec((tm, D_MODEL), tok), pl.BlockSpec((tm, Z_COLS), tok), pl.BlockSpec((1, D_MODEL), fix),
                   pl.BlockSpec((1, 128), fix)],
        out_shape=[_sds((T, D_MODEL), F32), _sds((T, Z_COLS), BF16), _sds((1, D_MODEL), F32), _sds((1, 128), F32)],
        scratch_shapes=[pltpu.VMEM((8, 128), F32)],
        compiler_params=_params(("arbitrary",)),
    )(dh2, h1, g, w_in_r, dq, dk, dv, ds, dc, fz)


def _lane_move(src_lo, dst_lo, width, dtype):
    r = lax.broadcasted_iota(jnp.int32, (128, 128), 0)
    c = lax.broadcasted_iota(jnp.int32, (128, 128), 1)
    return ((c - dst_lo == r - src_lo) & (r >= src_lo) & (r < src_lo + width)).astype(dtype)


def _lane_const(lo, width, value):
    lane = lax.broadcasted_iota(jnp.int32, (1, 128), 1)
    return jnp.where((lane >= lo) & (lane < lo + width), value, 0.0).astype(F32)


def _pieces(a):
    hi = a.astype(BF16)
    rest = a - hi.astype(F32)
    mid = rest.astype(BF16)
    return hi, mid, (rest - mid.astype(F32)).astype(BF16)


def _head_features(pair_block, e):
    return _dot(pair_block, _lane_move(HEAD_DIM * e, 0, HEAD_DIM, BF16))


def _helper_columns(pieces, head, sign):
    out = None
    for k, piece in enumerate(pieces):
        term = _dot(piece, _lane_move(head, HEAD_DIM + k, 1, BF16))
        out = term if out is None else out + term
    return sign * out


def heads_in(qkv, cum):
    T = qkv.shape[0]
    tm = _tile(T, TOKEN_TILE)

    def body(qkv_ref, c_ref, q_ref, k_ref, v_ref):
        c = _pieces(c_ref[...])
        for h in range(N_HEADS):
            p, e = divmod(h, 2)
            blk = lambda base: qkv_ref[:, base + 128 * p:base + 128 * (p + 1)]
            q_ref[h] = (_head_features(blk(0), e) + _lane_const(HEAD_DIM, 3, -1.0)).astype(BF16)
            k_ref[h] = (_head_features(blk(ATTN_W), e) + _helper_columns(c, h, 1.0)
                        + _lane_const(HEAD_DIM + 3, 3, 1.0)).astype(BF16)
            v_ref[h] = (_head_features(blk(2 * ATTN_W), e) + _lane_const(HEAD_DIM, 3, 1.0)).astype(BF16)

    tok = lambda i: (i, 0)
    heads = pl.BlockSpec((N_HEADS, tm, 128), lambda i: (0, i, 0))
    return pl.pallas_call(
        body, name="heads_in", grid=(T // tm,),
        in_specs=[pl.BlockSpec((tm, 3 * ATTN_W), tok), pl.BlockSpec((tm, 128), tok)],
        out_specs=[heads] * 3, out_shape=[_sds((N_HEADS, T, 128), BF16)] * 3,
        compiler_params=_params(("arbitrary",)),
    )(qkv, cum)


def attn_fwd(q_aug, k_aug, v_aug):
    H, T, wd = q_aug.shape
    hd = HEAD_DIM
    tq = _tile(T, ATTN_TILE)
    n = T // tq

    qg = min(tq, ATTN_QUERY_GROUP)

    def body(q_ref, k_ref, v_ref, o_ref, qb_ref, m_sc, acc):
        qi = pl.program_id(1)
        qv = q_ref[0]
        m_sc[...] = jnp.full_like(m_sc, -jnp.inf)
        acc[...] = jnp.zeros_like(acc)

        def tile(j, masked):
            rows = pl.ds(pl.multiple_of(j * tq, tq), tq)
            kv, vv = k_ref[0, rows, :], v_ref[0, rows, :]
            for lo in range(0, tq, qg):
                lanes = slice(lo, lo + qg)
                st = _dot_nt(kv, qv[lanes, :])
                if masked:
                    keep = (lax.broadcasted_iota(jnp.int32, (tq, qg), 0)
                            <= lo + lax.broadcasted_iota(jnp.int32, (tq, qg), 1))
                    st = jnp.where(keep, st, -1e30)
                m_old = m_sc[:, lanes]
                m_new = jnp.maximum(m_old, jnp.max(st, axis=0, keepdims=True))
                pt = jnp.exp(st - m_new).astype(BF16)
                acc[:, lanes] = jnp.exp(m_old - m_new) * acc[:, lanes] + _dot_tn(vv, pt)
                m_sc[:, lanes] = m_new

        def off_diagonal(j, carry):
            tile(j, False)
            return carry

        lax.fori_loop(0, qi, off_diagonal, 0)
        tile(qi, True)
        total = acc[hd:hd + 1, :]
        o_ref[0] = (acc[...] / total).T
        hi, mid, lo = (t.astype(F32) for t in _pieces(-(m_sc[...] + jnp.log(total))))
        row = lax.broadcasted_iota(jnp.int32, (wd, tq), 0)
        lse_rows = jnp.where(row == hd + 3, hi, jnp.where(row == hd + 4, mid, jnp.where(row == hd + 5, lo, 0.0)))
        qb_ref[0] = (qv.astype(F32) + lse_rows.T).astype(BF16)

    qmap = lambda h, i: (h, i, 0)
    head = lambda h, i: (h, 0, 0)
    return pl.pallas_call(
        body, name="attn_fwd", grid=(H, n),
        in_specs=[pl.BlockSpec((1, tq, wd), qmap), pl.BlockSpec((1, T, wd), head), pl.BlockSpec((1, T, wd), head)],
        out_specs=[pl.BlockSpec((1, tq, wd), qmap), pl.BlockSpec((1, tq, wd), qmap)],
        out_shape=[_sds((H, T, wd), F32), _sds((H, T, wd), BF16)],
        scratch_shapes=[pltpu.VMEM((1, tq), F32), pltpu.VMEM((wd, tq), F32)],
        compiler_params=_params(("arbitrary", "arbitrary")),
    )(q_aug, k_aug, v_aug)


def attn_bwd(q_aug, k_aug, v_aug, do_aug):
    H, T, wd = q_aug.shape
    tq = _tile(T, ATTN_TILE)
    n = T // tq

    def body(q_ref, do_ref, k_ref, v_ref, dq_ref, dk_ref, dv_ref, dck_ref, dcq_ref):
        j = pl.program_id(1)

        @pl.when(j == 0)
        def _():
            dq_ref[...] = jnp.zeros_like(dq_ref)
            dcq_ref[...] = jnp.zeros_like(dcq_ref)

        dk_ref[...] = jnp.zeros_like(dk_ref)
        dv_ref[...] = jnp.zeros_like(dv_ref)
        dck_ref[...] = jnp.zeros_like(dck_ref)
        kv, vv = k_ref[0], v_ref[0]

        def tile(i, masked):
            rows = pl.ds(pl.multiple_of(i * tq, tq), tq)
            qv, dov = q_ref[0, rows, :], do_ref[0, rows, :]
            pt = jnp.exp(_dot_nt(kv, qv))
            if masked:
                keep = lax.broadcasted_iota(jnp.int32, (tq, tq), 0) <= lax.broadcasted_iota(jnp.int32, (tq, tq), 1)
                pt = jnp.where(keep, pt, 0.0)
            dv_ref[0] += _dot(pt.astype(BF16), dov)
            dst = pt * _dot_nt(vv, dov)
            dsb = dst.astype(BF16)
            dk_ref[0] += _dot(dsb, qv)
            dq_ref[0, rows, :] += _dot_tn(dsb, kv)
            dck_ref[0] += -jnp.sum(dst, axis=1, keepdims=True)
            dcq_ref[0, pl.ds(i, 1), :] += jnp.sum(dst, axis=0, keepdims=True)

        def off_diagonal(i, carry):
            tile(i, False)
            return carry

        tile(j, True)
        lax.fori_loop(j + 1, n, off_diagonal, 0)

    head = lambda h, j: (h, 0, 0)
    kmap = lambda h, j: (h, j, 0)
    return pl.pallas_call(
        body, name="attn_bwd", grid=(H, n),
        in_specs=[pl.BlockSpec((1, T, wd), head), pl.BlockSpec((1, T, wd), head), pl.BlockSpec((1, tq, wd), kmap),
                  pl.BlockSpec((1, tq, wd), kmap)],
        out_specs=[pl.BlockSpec((1, T, wd), head), pl.BlockSpec((1, tq, wd), kmap), pl.BlockSpec((1, tq, wd), kmap),
                   pl.BlockSpec((1, tq, 1), kmap), pl.BlockSpec((1, n, tq), head)],
        out_shape=[_sds((H, T, wd), F32), _sds((H, T, wd), F32), _sds((H, T, wd), F32), _sds((H, T, 1), F32),
                   _sds((H, n, tq), F32)],
        compiler_params=_params(("arbitrary", "arbitrary")),
    )(q_aug, do_aug, k_aug, v_aug)


def _complex_step(a_r, a_i, cr, ci, br, bi):
    return a_r * cr - a_i * ci + br, a_r * ci + a_i * cr + bi


_HALF_CH = SSM_W // 2
_HALF_ST = STATE_W // 2


def _state_cols(part, half):
    lo = part * STATE_W + half * _HALF_ST
    return slice(lo, lo + _HALF_ST)


def _channels_to_states(x, w_ref, out_ref):
    for half in range(2):
        ch = slice(half * _HALF_CH, (half + 1) * _HALF_CH)
        for part in range(2):
            cols = _state_cols(part, half)
            out_ref[:, cols] = _dot(x[:, ch], w_ref[ch, cols])


def _states_to_channels(x, w_ref):
    halves = []
    for half in range(2):
        ch = slice(half * _HALF_CH, (half + 1) * _HALF_CH)
        halves.append(_dot(x[:, _state_cols(0, half)], w_ref[_state_cols(0, half), ch])
                      + _dot(x[:, _state_cols(1, half)], w_ref[_state_cols(1, half), ch]))
    return jnp.concatenate(halves, axis=1)


def ssm_fwd(s_perm, wb, cbd, a_r, a_i, al_r, al_i, dvec):
    T = s_perm.shape[0]
    chunk = T // 8
    ts = _tile(chunk, SCAN_STEPS)
    tr, n_s = ts * 8, chunk // ts
    W, LB = STATE_W, SCAN_LANES

    def body(s_ref, wb_ref, cbd_ref, ar_ref, ai_ref, alr_ref, ali_ref, dv_ref, y_ref, xs_ref, bu, carry):
        ph, i = pl.program_id(0), pl.program_id(1)

        @pl.when((ph == 0) & (i == 0))
        def _():
            carry[...] = jnp.zeros_like(carry)

        _channels_to_states(s_ref[...].astype(BF16), wb_ref, bu)

        def scan(store):
            for lb in range(W // LB):
                lo = lb * LB
                re, im = slice(lo, lo + LB), slice(W + lo, W + lo + LB)
                ar = jnp.broadcast_to(ar_ref[:, re], (8, LB))
                ai = jnp.broadcast_to(ai_ref[:, re], (8, LB))

                def step(s, c):
                    rows = pl.ds(pl.multiple_of(s * 8, 8), 8)
                    nr, ni = _complex_step(ar, ai, c[0], c[1], bu[rows, re], bu[rows, im])
                    if store:
                        bu[rows, re] = nr
                        bu[rows, im] = ni
                    return nr, ni

                cr, ci = lax.fori_loop(0, ts, step, (carry[:, re], carry[:, im]), unroll=2)
                carry[:, re] = cr
                carry[:, im] = ci

        @pl.when(ph == 0)
        def _():
            scan(False)

            @pl.when(i == n_s - 1)
            def _():
                er, ei = carry[:, :W], carry[:, W:]
                alr = jnp.broadcast_to(alr_ref[...], (8, W))
                ali = jnp.broadcast_to(ali_ref[...], (8, W))
                first = lax.broadcasted_iota(jnp.int32, (8, W), 0) == 0
                sr, si = jnp.zeros((8, W), F32), jnp.zeros((8, W), F32)
                for _ in range(7):
                    vr, vi = _complex_step(alr, ali, sr, si, er, ei)
                    sr = jnp.where(first, 0.0, pltpu.roll(vr, 1, 0))
                    si = jnp.where(first, 0.0, pltpu.roll(vi, 1, 0))
                carry[:, :W] = sr
                carry[:, W:] = si

        @pl.when(ph == 1)
        def _():
            scan(True)
            xb = bu[...].astype(BF16)
            xs_ref[...] = xb
            y_ref[...] = _states_to_channels(xb, cbd_ref) + s_ref[...] * dv_ref[...]

    fix = lambda p, i: (0, 0)
    return pl.pallas_call(
        body, name="ssm_fwd", grid=(2, n_s),
        in_specs=[pl.BlockSpec((tr, SSM_W), lambda p, i: (i, 0)), pl.BlockSpec((SSM_W, 2 * W), fix),
                  pl.BlockSpec((2 * W, SSM_W), fix), pl.BlockSpec((1, W), fix), pl.BlockSpec((1, W), fix),
                  pl.BlockSpec((1, W), fix), pl.BlockSpec((1, W), fix), pl.BlockSpec((1, SSM_W), fix)],
        out_specs=[pl.BlockSpec((tr, SSM_W), lambda p, i: (i * p, 0)), pl.BlockSpec((tr, 2 * W), lambda p, i: (i * p, 0))],
        out_shape=[_sds((T, SSM_W), F32), _sds((T, 2 * W), BF16)],
        scratch_shapes=[pltpu.VMEM((tr, 2 * W), F32), pltpu.VMEM((8, 2 * W), F32)],
        compiler_params=_params(("arbitrary", "arbitrary")),
    )(s_perm, wb, cbd, a_r, a_i, al_r, al_i, dvec)


def ssm_bwd(dy_perm, s_perm, xs, cbd_t, wb_t, a_r, a_i, al_r, al_i, dvec):
    T = s_perm.shape[0]
    chunk = T // 8
    ts = _tile(chunk, SCAN_STEPS)
    tr, n_s = ts * 8, chunk // ts
    W, LB = STATE_W, SCAN_LANES

    def body(dy_ref, s_ref, xs_ref, cbt_ref, wbt_ref, ar_ref, ai_ref, alr_ref, ali_ref, dv_ref,
             du_ref, gs_ref, da_ref, dd_ref, gd, x32, carry):
        ph, i = pl.program_id(0), pl.program_id(1)

        @pl.when((ph == 0) & (i == 0))
        def _():
            carry[...] = jnp.zeros_like(carry)
            da_ref[...] = jnp.zeros_like(da_ref)
            dd_ref[...] = jnp.zeros_like(dd_ref)

        _channels_to_states(dy_ref[...].astype(BF16), cbt_ref, gd)

        def scan(store):
            for lb in range(W // LB):
                lo = lb * LB
                re, im = slice(lo, lo + LB), slice(W + lo, W + lo + LB)
                ar = jnp.broadcast_to(ar_ref[:, re], (8, LB))
                nai = -jnp.broadcast_to(ai_ref[:, re], (8, LB))

                def step(k, c):
                    rows = pl.ds(pl.multiple_of((ts - 1 - k) * 8, 8), 8)
                    cr, ci = c[0], c[1]
                    nr, ni = _complex_step(ar, nai, cr, ci, gd[rows, re], gd[rows, im])
                    if store:
                        xr, xi = x32[rows, re], x32[rows, im]
                        gd[rows, re] = nr
                        gd[rows, im] = ni
                        return nr, ni, c[2] + cr * xr + ci * xi, c[3] + ci * xr - cr * xi
                    return nr, ni

                init = (carry[:, re], carry[:, im])
                if store:
                    init = init + (da_ref[:, re], da_ref[:, im])
                out = lax.fori_loop(0, ts, step, init, unroll=2)
                carry[:, re] = out[0]
                carry[:, im] = out[1]
                if store:
                    da_ref[:, re] = out[2]
                    da_ref[:, im] = out[3]

        @pl.when(ph == 0)
        def _():
            scan(False)

            @pl.when(i == n_s - 1)
            def _():
                er, ei = carry[:, :W], carry[:, W:]
                alr = jnp.broadcast_to(alr_ref[...], (8, W))
                nali = -jnp.broadcast_to(ali_ref[...], (8, W))
                last = lax.broadcasted_iota(jnp.int32, (8, W), 0) == 7
                rr, ri = jnp.zeros((8, W), F32), jnp.zeros((8, W), F32)
                for _ in range(7):
                    vr, vi = _complex_step(alr, nali, rr, ri, er, ei)
                    rr = jnp.where(last, 0.0, pltpu.roll(vr, 7, 0))
                    ri = jnp.where(last, 0.0, pltpu.roll(vi, 7, 0))
                carry[:, :W] = rr
                carry[:, W:] = ri

        @pl.when(ph == 1)
        def _():
            x32[...] = xs_ref[...].astype(F32)
            scan(True)
            gb = gd[...].astype(BF16)
            gs_ref[...] = gb
            dy = dy_ref[...]
            du_ref[...] = _states_to_channels(gb, wbt_ref) + dy * dv_ref[...]
            dd_ref[...] += jnp.sum(dy * s_ref[...], axis=0, keepdims=True)

    fix = lambda p, i: (0, 0)
    rev = lambda p, i: (n_s - 1 - i, 0)
    rev_out = lambda p, i: (n_s - 1 - i * p, 0)
    return pl.pallas_call(
        body, name="ssm_bwd", grid=(2, n_s),
        in_specs=[pl.BlockSpec((tr, SSM_W), rev), pl.BlockSpec((tr, SSM_W), rev), pl.BlockSpec((tr, 2 * W), rev),
                  pl.BlockSpec((SSM_W, 2 * W), fix), pl.BlockSpec((2 * W, SSM_W), fix), pl.BlockSpec((1, W), fix),
                  pl.BlockSpec((1, W), fix), pl.BlockSpec((1, W), fix), pl.BlockSpec((1, W), fix),
                  pl.BlockSpec((1, SSM_W), fix)],
        out_specs=[pl.BlockSpec((tr, SSM_W), rev_out), pl.BlockSpec((tr, 2 * W), rev_out),
                   pl.BlockSpec((8, 2 * W), fix), pl.BlockSpec((1, SSM_W), fix)],
        out_shape=[_sds((T, SSM_W), F32), _sds((T, 2 * W), BF16), _sds((8, 2 * W), F32), _sds((1, SSM_W), F32)],
        scratch_shapes=[pltpu.VMEM((tr, 2 * W), F32), pltpu.VMEM((tr, 2 * W), F32), pltpu.VMEM((8, 2 * W), F32)],
        compiler_params=_params(("arbitrary", "arbitrary")),
    )(dy_perm, s_perm, xs, cbd_t, wb_t, a_r, a_i, al_r, al_i, dvec)


def _join_heads(ref, dtype):
    def move(h, dst):
        x = ref[h]
        pieces = _pieces(x) if dtype == F32 else (x.astype(BF16),)
        out = None
        for piece in pieces:
            term = _dot(piece, _lane_move(0, dst, HEAD_DIM, BF16))
            out = term if out is None else out + term
        return out

    return jnp.concatenate([move(2 * p, 0) + move(2 * p + 1, HEAD_DIM) for p in range(N_HEADS // 2)], axis=1)


def mixout_fwd(h1, o_heads, ypre, g_a, g_s, w_glu, b_glu, w_out):
    T = h1.shape[0]
    tm = _tile(T, TOKEN_TILE)

    def body(h_ref, at_ref, yp_ref, ga_ref, gs_ref, wg_ref, bg_ref, wo_ref, h2_ref, mixed_ref):
        yg, _ = _gelu_parts(yp_ref[...])
        gl = yg * jax.nn.sigmoid(_dot(yg.astype(BF16), wg_ref[...]) + bg_ref[...])
        at = _join_heads(at_ref, F32)
        mixed = jnp.concatenate([at * _rms_scale(at) * ga_ref[...], gl * _rms_scale(gl) * gs_ref[...]], axis=1)
        mixed = mixed.astype(BF16)
        mixed_ref[...] = mixed
        h2_ref[...] = h_ref[...] + _dot(mixed, wo_ref[...])

    tok = lambda i: (i, 0)
    fix = lambda i: (0, 0)
    return pl.pallas_call(
        body, name="mixout_fwd", grid=(T // tm,),
        in_specs=[pl.BlockSpec((tm, D_MODEL), tok), pl.BlockSpec((N_HEADS, tm, 128), lambda i: (0, i, 0)),
                  pl.BlockSpec((tm, SSM_W), tok),
                  pl.BlockSpec((1, ATTN_W), fix), pl.BlockSpec((1, SSM_W), fix), pl.BlockSpec((SSM_W, SSM_W), fix),
                  pl.BlockSpec((1, SSM_W), fix), pl.BlockSpec((D_MODEL, D_MODEL), fix)],
        out_specs=[pl.BlockSpec((tm, D_MODEL), tok), pl.BlockSpec((tm, D_MODEL), tok)],
        out_shape=[_sds((T, D_MODEL), F32), _sds((T, D_MODEL), BF16)],
        compiler_params=_params(("arbitrary",)),
    )(h1, o_heads, ypre, g_a, g_s, w_glu, b_glu, w_out)


def mixout_bwd(dh2, o_heads, ypre, g_a, g_s, w_glu, b_glu, w_out, seg):
    T = dh2.shape[0]
    tm = _tile(T, TOKEN_TILE)

    def body(dh_ref, at_ref, yp_ref, ga_ref, gs_ref, wg_ref, bg_ref, wo_ref, seg_ref,
             do_ref, dyp_ref, dpre_ref, yg_ref, dga_ref, dgs_ref, dbg_ref):
        @pl.when(pl.program_id(0) == 0)
        def _():
            dga_ref[...] = jnp.zeros_like(dga_ref)
            dgs_ref[...] = jnp.zeros_like(dgs_ref)
            dbg_ref[...] = jnp.zeros_like(dbg_ref)

        dmix = _dot_nt(dh_ref[...].astype(BF16), wo_ref[...])
        at = _join_heads(at_ref, F32)
        dat, dga = _rms_bwd(dmix[:, :ATTN_W], at, ga_ref[...])
        dga_ref[...] += dga
        delta = _pieces(jnp.dot(dat * at, seg_ref[...], precision=HIGHEST, preferred_element_type=F32))
        datb = dat.astype(BF16)
        for h in range(N_HEADS):
            p, e = divmod(h, 2)
            do_ref[h] = (_head_features(datb[:, 128 * p:128 * (p + 1)], e) + _helper_columns(delta, h, -1.0)).astype(BF16)
        yp = yp_ref[...]
        yg, t = _gelu_parts(yp)
        ygb = yg.astype(BF16)
        yg_ref[...] = ygb
        sg = jax.nn.sigmoid(_dot(ygb, wg_ref[...]) + bg_ref[...])
        dgl, dgs = _rms_bwd(dmix[:, ATTN_W:], yg * sg, gs_ref[...])
        dgs_ref[...] += dgs
        dpre = dgl * yg * sg * (1.0 - sg)
        dbg_ref[...] += jnp.sum(dpre, axis=0, keepdims=True)
        dpb = dpre.astype(BF16)
        dpre_ref[...] = dpb
        dyg = dgl * sg + _dot_nt(dpb, wg_ref[...])
        dyp_ref[...] = dyg * _gelu_grad(yp, t)

    tok = lambda i: (i, 0)
    fix = lambda i: (0, 0)
    heads = pl.BlockSpec((N_HEADS, tm, 128), lambda i: (0, i, 0))
    return pl.pallas_call(
        body, name="mixout_bwd", grid=(T // tm,),
        in_specs=[pl.BlockSpec((tm, D_MODEL), tok), heads, pl.BlockSpec((tm, SSM_W), tok),
                  pl.BlockSpec((1, ATTN_W), fix), pl.BlockSpec((1, SSM_W), fix), pl.BlockSpec((SSM_W, SSM_W), fix),
                  pl.BlockSpec((1, SSM_W), fix), pl.BlockSpec((D_MODEL, D_MODEL), fix), pl.BlockSpec((ATTN_W, 128), fix)],
        out_specs=[heads, pl.BlockSpec((tm, SSM_W), tok), pl.BlockSpec((tm, SSM_W), tok),
                   pl.BlockSpec((tm, SSM_W), tok), pl.BlockSpec((1, ATTN_W), fix),
                   pl.BlockSpec((1, SSM_W), fix), pl.BlockSpec((1, SSM_W), fix)],
        out_shape=[_sds((N_HEADS, T, 128), BF16), _sds((T, SSM_W), F32), _sds((T, SSM_W), BF16), _sds((T, SSM_W), BF16),
                   _sds((1, ATTN_W), F32), _sds((1, SSM_W), F32), _sds((1, SSM_W), F32)],
        compiler_params=_params(("arbitrary",)),
    )(dh2, o_heads, ypre, g_a, g_s, w_glu, b_glu, w_out, seg)


def head_fwd_bwd(h3, p, target, g_ple, g_final, w_gate, w_proj):
    T = h3.shape[0]
    tm = _tile(T, TOKEN_TILE)
    pd = p.shape[1]

    def body(h_ref, p_ref, tg_ref, gp_ref, gf_ref, wg_ref, wp_ref,
             dh_ref, n3_ref, dz_ref, dpp_ref, loss_ref, dgp_ref, dgf_ref):
        @pl.when(pl.program_id(0) == 0)
        def _():
            loss_ref[...] = jnp.zeros_like(loss_ref)
            dgp_ref[...] = jnp.zeros_like(dgp_ref)
            dgf_ref[...] = jnp.zeros_like(dgf_ref)

        x = h_ref[...]
        gp, gf = gp_ref[...], gf_ref[...]
        n3 = (x * _rms_scale(x) * gp).astype(BF16)
        n3_ref[...] = n3
        gate = jax.nn.sigmoid(_dot(n3, wg_ref[...]))
        pp = _dot(p_ref[...].astype(BF16), wp_ref[...])
        h4 = x + gate * pp
        y = h4 * _rms_scale(h4) * gf
        e = y - tg_ref[...]
        tile_loss = jnp.sum(jnp.sum(e * e, axis=1, keepdims=True), axis=0, keepdims=True) * (0.5 / D_MODEL)
        loss_ref[...] += jnp.broadcast_to(tile_loss, loss_ref.shape)
        dh4, dgf = _rms_bwd(e * (1.0 / D_MODEL), h4, gf)
        dgf_ref[...] += dgf
        dzg = dh4 * pp * gate * (1.0 - gate)
        dzb = dzg.astype(BF16)
        dz_ref[...] = dzb
        dpp_ref[...] = (dh4 * gate).astype(BF16)
        dx, dgp = _rms_bwd(_dot_nt(dzb, wg_ref[...]), x, gp)
        dgp_ref[...] += dgp
        dh_ref[...] = dh4 + dx

    tok = lambda i: (i, 0)
    fix = lambda i: (0, 0)
    return pl.pallas_call(
        body, name="head_fwd_bwd", grid=(T // tm,),
        in_specs=[pl.BlockSpec((tm, D_MODEL), tok), pl.BlockSpec((tm, pd), tok), pl.BlockSpec((tm, D_MODEL), tok),
                  pl.BlockSpec((1, D_MODEL), fix), pl.BlockSpec((1, D_MODEL), fix), pl.BlockSpec((D_MODEL, D_MODEL), fix),
                  pl.BlockSpec((pd, D_MODEL), fix)],
        out_specs=[pl.BlockSpec((tm, D_MODEL), tok), pl.BlockSpec((tm, D_MODEL), tok), pl.BlockSpec((tm, D_MODEL), tok),
                   pl.BlockSpec((tm, D_MODEL), tok), pl.BlockSpec((8, 128), fix), pl.BlockSpec((1, D_MODEL), fix),
                   pl.BlockSpec((1, D_MODEL), fix)],
        out_shape=[_sds((T, D_MODEL), F32), _sds((T, D_MODEL), BF16), _sds((T, D_MODEL), BF16), _sds((T, D_MODEL), BF16),
                   _sds((8, 128), F32), _sds((1, D_MODEL), F32), _sds((1, D_MODEL), F32)],
        compiler_params=_params(("arbitrary",)),
    )(h3, p, target, g_ple, g_final, w_gate, w_proj)


def _row_tile(rows, cols, n_arrays):
    lanes = -(-cols // 128) * 128
    cap = VMEM_LIMIT // 3 // (2 * n_arrays * lanes * 4)
    best = None
    for t in range(PACK_ALIGN, min(rows, cap) + 1, PACK_ALIGN):
        if rows % t == 0:
            best = t
    assert best is not None, (rows, cols)
    return best


def _adamw_math(w, g, m, v):
    nm = ADAM_B1 * m + (1.0 - ADAM_B1) * g
    nv = ADAM_B2 * v + (1.0 - ADAM_B2) * (g * g)
    c1 = 1.0 - ADAM_B1 ** ADAM_STEP
    c2 = 1.0 - ADAM_B2 ** ADAM_STEP
    return -ADAM_LR * ((nm / c1) / (jnp.sqrt(nv / c2) + ADAM_EPS) + ADAM_WD * w), nm, nv


def adamw(w, g, m, v, name):
    R, C = w.shape
    tr = _row_tile(R, C, 7)

    def body(w_ref, g_ref, m_ref, v_ref, d_ref, nm_ref, nv_ref):
        d_ref[...], nm_ref[...], nv_ref[...] = _adamw_math(w_ref[...], g_ref[...], m_ref[...], v_ref[...])

    spec = pl.BlockSpec((tr, C), lambda i: (i, 0))
    return pl.pallas_call(
        body, name=name, grid=(R // tr,), in_specs=[spec] * 4, out_specs=[spec] * 3,
        out_shape=[_sds((R, C), F32)] * 3, compiler_params=_params(("arbitrary",)),
    )(w, g, m, v)


def join_halves(mine, other, core):
    rh, C = mine.shape
    tr = _row_tile(rh, C, 3)
    nb = rh // tr

    def body(c_ref, m_ref, o_ref, out_ref):
        out_ref[...] = jnp.where((pl.program_id(0) // nb) == c_ref[0], m_ref[...], o_ref[...])

    half = pl.BlockSpec((tr, C), lambda i, c: (i % nb, 0))
    return pl.pallas_call(
        body, name="join_halves",
        grid_spec=pltpu.PrefetchScalarGridSpec(num_scalar_prefetch=1, grid=(2 * nb,), in_specs=[half, half],
                                               out_specs=pl.BlockSpec((tr, C), lambda i, c: (i, 0))),
        out_shape=_sds((2 * rh, C), F32), compiler_params=_params(("arbitrary",)),
    )(core, mine, other)


def pair_sum(g, theirs, core):
    n, R, C = g.shape
    rh = R // 2
    tr = _row_tile(rh, C, 3)
    nb = rh // tr

    def body(c_ref, g_ref, t_ref, o_ref):
        o_ref[...] = (g_ref[...] + t_ref[...]).astype(BF16)

    here = pl.BlockSpec((1, tr, C), lambda j, i, c: (j, i, 0))
    return pl.pallas_call(
        body, name="pair_sum",
        grid_spec=pltpu.PrefetchScalarGridSpec(
            num_scalar_prefetch=1, grid=(n, nb),
            in_specs=[pl.BlockSpec((1, tr, C), lambda j, i, c: (j, c[0] * nb + i, 0)), here], out_specs=here),
        out_shape=_sds((n, rh, C), BF16), compiler_params=_params(("arbitrary", "arbitrary")),
    )(core, g, theirs)


def chip_sum(pair, got, chip):
    _, R, C = pair.shape
    tr = _row_tile(R, C, 5)

    def body(c_ref, p_ref, g0_ref, g1_ref, g2_ref, o_ref):
        f = lambda ref: ref[0].astype(F32)
        o_ref[...] = ((f(p_ref) + f(g0_ref)) + f(g1_ref)) + f(g2_ref)

    slot = lambda k: pl.BlockSpec((1, tr, C), lambda i, c: (k, i, 0))
    return pl.pallas_call(
        body, name="chip_sum",
        grid_spec=pltpu.PrefetchScalarGridSpec(
            num_scalar_prefetch=1, grid=(R // tr,),
            in_specs=[pl.BlockSpec((1, tr, C), lambda i, c: (c[0], i, 0)), slot(0), slot(1), slot(2)],
            out_specs=pl.BlockSpec((tr, C), lambda i, c: (i, 0))),
        out_shape=_sds((R, C), F32), compiler_params=_params(("arbitrary",)),
    )(chip, pair, got, got, got)


_HBM = pl.BlockSpec(memory_space=pltpu.HBM)


def _place():
    x, y, c = lax.axis_index("x"), lax.axis_index("y"), lax.axis_index("c")
    return x, y, c, [(1 - x, y), (x, 1 - y), (1 - x, 1 - y)]


def _spans(rows, n):
    assert rows % PACK_ALIGN == 0
    tiles = rows // PACK_ALIGN
    n = min(n, tiles)
    cuts = [tiles * q // n for q in range(n + 1)]
    return [(cuts[q] * PACK_ALIGN, (cuts[q + 1] - cuts[q]) * PACK_ALIGN) for q in range(n)]


def _remote(src, dst, send_sem, recv_sem, to):
    return pltpu.make_async_remote_copy(src_ref=src, dst_ref=dst, send_sem=send_sem, recv_sem=recv_sem,
                                        device_id=to, device_id_type=MESH)


def allgather_shards(wp):
    R, C = wp.shape
    rh = R // 2
    spans = _spans(rh, COPY_CHUNKS)
    n_sp = len(spans)
    local_spans = _spans(R, 2 * COPY_CHUNKS)

    def body(w_ref, out_ref, send_sems, recv_sems, pass_send, pass_recv, local_sems):
        x, y, c, chips = _place()
        me = 2 * x + y
        local = []
        for q, (o, n) in enumerate(local_spans):
            cp = pltpu.make_async_copy(w_ref.at[pl.ds(o, n), :], out_ref.at[me, pl.ds(o, n), :], local_sems.at[q])
            cp.start()
            local.append(cp)
        sends = []
        for k, (cx, cy) in enumerate(chips):
            for q, (o, n) in enumerate(spans):
                rows = pl.ds(c * rh + o, n)
                cp = _remote(w_ref.at[rows, :], out_ref.at[me, rows, :], send_sems.at[k * n_sp + q],
                             recv_sems.at[k * n_sp + q], (cx, cy, c))
                cp.start()
                sends.append(cp)
        for k, (cx, cy) in enumerate(chips):
            for q, (o, n) in enumerate(spans):
                blk = out_ref.at[2 * cx + cy, pl.ds(c * rh + o, n), :]
                _remote(blk, blk, send_sems.at[k * n_sp + q], recv_sems.at[k * n_sp + q], (cx, cy, c)).wait_recv()
                cp = _remote(blk, blk, pass_send.at[k * n_sp + q], pass_recv.at[k * n_sp + q], (x, y, 1 - c))
                cp.start()
                sends.append(cp)
        for k, (cx, cy) in enumerate(chips):
            for q, (o, n) in enumerate(spans):
                blk = out_ref.at[2 * cx + cy, pl.ds((1 - c) * rh + o, n), :]
                _remote(blk, blk, pass_send.at[k * n_sp + q], pass_recv.at[k * n_sp + q], (x, y, 1 - c)).wait_recv()
        for cp in sends:
            cp.wait_send()
        for cp in local:
            cp.wait()

    sems = pltpu.SemaphoreType.DMA((3 * n_sp,))
    return pl.pallas_call(
        body, name="allgather_shards", in_specs=[_HBM], out_specs=_HBM, out_shape=_sds((4, R, C), wp.dtype),
        scratch_shapes=[sems, sems, sems, sems, pltpu.SemaphoreType.DMA((len(local_spans),))],
    )(wp)


def sibling_split(g):
    n_sl, R, C = g.shape
    rh = R // 2
    spans = _spans(rh, COPY_CHUNKS)
    n_sp = len(spans)

    def body(g_ref, got_ref, send_sems, recv_sems):
        x, y, c, _ = _place()
        copies = []
        for j in range(n_sl):
            for q, (o, n) in enumerate(spans):
                cp = _remote(g_ref.at[j, pl.ds((1 - c) * rh + o, n), :], got_ref.at[j, pl.ds(o, n), :],
                             send_sems.at[j * n_sp + q], recv_sems.at[j * n_sp + q], (x, y, 1 - c))
                cp.start()
                copies.append(cp)
        for cp in copies:
            cp.wait()

    sems = pltpu.SemaphoreType.DMA((n_sl * n_sp,))
    return pl.pallas_call(
        body, name="sibling_split", in_specs=[_HBM], out_specs=_HBM, out_shape=_sds((n_sl, rh, C), g.dtype),
        scratch_shapes=[sems, sems],
    )(g)


def chip_exchange(p):
    _, R, C = p.shape
    spans = _spans(R, COPY_CHUNKS)
    n_sp = len(spans)

    def body(p_ref, buf_ref, send_sems, recv_sems):
        x, y, c, chips = _place()
        sends = []
        for k, (cx, cy) in enumerate(chips):
            for q, (o, n) in enumerate(spans):
                cp = _remote(p_ref.at[2 * cx + cy, pl.ds(o, n), :], buf_ref.at[k, pl.ds(o, n), :],
                             send_sems.at[k * n_sp + q], recv_sems.at[k * n_sp + q], (cx, cy, c))
                cp.start()
                sends.append(cp)
        for cp in sends:
            cp.wait()

    sems = pltpu.SemaphoreType.DMA((3 * n_sp,))
    return pl.pallas_call(
        body, name="chip_exchange", in_specs=[_HBM], out_specs=_HBM, out_shape=_sds((3, R, C), p.dtype),
        scratch_shapes=[sems, sems],
    )(p)


def sibling_swap(half):
    R, C = half.shape
    spans = _spans(R, COPY_CHUNKS)

    def body(h_ref, got_ref, send_sems, recv_sems):
        x, y, c, _ = _place()
        copies = []
        for q, (o, n) in enumerate(spans):
            cp = _remote(h_ref.at[pl.ds(o, n), :], got_ref.at[pl.ds(o, n), :], send_sems.at[q], recv_sems.at[q], (x, y, 1 - c))
            cp.start()
            copies.append(cp)
        for cp in copies:
            cp.wait()

    sems = pltpu.SemaphoreType.DMA((len(spans),))
    return pl.pallas_call(
        body, name="sibling_swap", in_specs=[_HBM], out_specs=_HBM, out_shape=_sds((R, C), half.dtype),
        scratch_shapes=[sems, sems],
    )(half)


def allreduce_small(v):
    R, C = v.shape

    def body(v_ref, out_ref, buf, send_sems, recv_sems):
        x, y, c, _ = _place()
        me = 4 * x + 2 * y + c
        buf[me] = v_ref[...]
        flips = [((k >> 2) & 1, (k >> 1) & 1, k & 1) for k in range(1, 8)]
        sends = []
        for k, (fx, fy, fc) in enumerate(flips):
            to = (1 - x if fx else x, 1 - y if fy else y, 1 - c if fc else c)
            cp = _remote(v_ref, buf.at[me], send_sems.at[k], recv_sems.at[k], to)
            cp.start()
            sends.append(cp)
        for k, (fx, fy, fc) in enumerate(flips):
            px, py, pc = (1 - x if fx else x, 1 - y if fy else y, 1 - c if fc else c)
            blk = buf.at[4 * px + 2 * py + pc]
            _remote(blk, blk, send_sems.at[k], recv_sems.at[k], (px, py, pc)).wait_recv()
        for cp in sends:
            cp.wait_send()
        acc = buf[0]
        for s in range(1, 8):
            acc = acc + buf[s]
        out_ref[...] = acc

    vm = pl.BlockSpec(memory_space=pltpu.VMEM)
    return pl.pallas_call(
        body, name="allreduce_small", in_specs=[vm], out_specs=vm, out_shape=_sds((R, C), F32),
        scratch_shapes=[pltpu.VMEM((8, R, C), F32), pltpu.SemaphoreType.DMA((7,)), pltpu.SemaphoreType.DMA((7,))],
        compiler_params=pltpu.CompilerParams(vmem_limit_bytes=VMEM_LIMIT),
    )(v)


def _rows_of(shape):
    return shape[0] * shape[1] // PACK_COLS


def _slot_rows(shape):
    return -(-_rows_of(shape) // PACK_ALIGN) * PACK_ALIGN


def _pack_shards(shards, dtype):
    parts = []
    for name, shape, _ in BIG:
        part = shards[name].reshape(_rows_of(shape), PACK_COLS).astype(dtype)
        parts.append(jnp.pad(part, ((0, _slot_rows(shape) - part.shape[0]), (0, 0))))
    used = sum(p.shape[0] for p in parts)
    parts.append(jnp.zeros((PACK_ROWS - used, PACK_COLS), dtype))
    return jnp.concatenate(parts, axis=0)


def _unpack_gathered(ag):
    out, off = {}, 0
    for name, shape, axis in BIG:
        r = _rows_of(shape)
        piece = ag[:, off:off + r, :].reshape((4,) + shape)
        off += _slot_rows(shape)
        if axis == 0:
            out[name] = piece.reshape(4 * shape[0], shape[1])
        else:
            out[name] = piece.transpose(1, 0, 2).reshape(shape[0], 4 * shape[1])
    return out


def _pack_full_grads(grads):
    parts = []
    for name, shape, axis in BIG:
        g = grads[name]
        if axis == 0:
            piece = g.reshape((4,) + shape)
        else:
            piece = g.reshape(shape[0], 4, shape[1]).transpose(1, 0, 2)
        piece = piece.reshape(4, _rows_of(shape), PACK_COLS)
        parts.append(jnp.pad(piece, ((0, 0), (0, _slot_rows(shape) - piece.shape[1]), (0, 0))))
    used = sum(p.shape[1] for p in parts)
    parts.append(jnp.zeros((4, PACK_ROWS - used, PACK_COLS), F32))
    return jnp.concatenate(parts, axis=1)


def _unpack_shards(packed):
    out, off = {}, 0
    for name, shape, _ in BIG:
        r = _rows_of(shape)
        out[name] = packed[off:off + r].reshape((1,) + shape)
        off += _slot_rows(shape)
    return out


def _pack_small(vals, extra=None):
    parts = [vals[name].reshape(-1) for name, _ in SMALL]
    used = sum(p.shape[0] for p in parts)
    if extra is not None:
        parts.append(extra.reshape(1))
        used += 1
    parts.append(jnp.zeros((SMALL_ROWS * 128 - used,), F32))
    return jnp.concatenate(parts).reshape(SMALL_ROWS, 128)


def _unpack_small(packed):
    flat = packed.reshape(-1)
    out, off = {}, 0
    for name, shape in SMALL:
        n = math.prod(shape)
        out[name] = flat[off:off + n].reshape(shape)
        off += n
    return out, flat[off]


def _permute_time(a):
    T, n = a.shape
    return a.reshape(8, T // 8, n).transpose(1, 0, 2).reshape(T, n)


def _unpermute_time(a):
    T, n = a.shape
    return a.reshape(T // 8, 8, n).transpose(1, 0, 2).reshape(T, n)


def _discretize(a_re, a_im, log_dt, b_re, b_im):
    dt = jnp.exp(log_dt)[:, None]
    decay = jnp.exp(dt * a_re)
    abar_r = decay * jnp.cos(dt * a_im)
    abar_i = decay * jnp.sin(dt * a_im)
    nr, ni = abar_r - 1.0, abar_i
    den = a_re * a_re + a_im * a_im
    fr = (nr * a_re + ni * a_im) / den
    fi = (ni * a_re - nr * a_im) / den
    bbar_r = fr[..., None] * b_re - fi[..., None] * b_im
    bbar_i = fr[..., None] * b_im + fi[..., None] * b_re
    return abar_r, abar_i, bbar_r, bbar_i


def _input_matrix(bbar_r, bbar_i):
    eye = jnp.eye(N_GROUPS, dtype=F32)
    blk = lambda b: jnp.einsum("ghp,gk->ghkp", b.transpose(0, 2, 1), eye).reshape(SSM_W, STATE_W)
    return jnp.concatenate([blk(bbar_r), blk(bbar_i)], axis=1)


def _output_matrix(c_re, c_im):
    eye = jnp.eye(N_GROUPS, dtype=F32)
    blk = lambda cm: jnp.einsum("ghp,gk->gpkh", cm, eye).reshape(STATE_W, SSM_W)
    return jnp.concatenate([blk(c_re), -blk(c_im)], axis=0)


def _state_power(ar, ai, n):
    steps = int(round(math.log2(n)))
    assert 1 << steps == n
    for _ in range(steps):
        ar, ai = ar * ar - ai * ai, 2.0 * ar * ai
    return ar, ai


def kernel(x, p, g_ffn1, w1_a, w3_a, w2_a, g_mix, w_in, b_f, a_re, a_im, log_dt, b_re, b_im, c_re, c_im, d_skip, w_glu, b_glu, g_attn_out, g_ssm_out, w_out, g_ffn2, w1_b, w3_b, w2_b, g_ple, w_ple_gate, w_ple_proj, g_final, loss_target, m_g_ffn1, m_w1_a, m_w3_a, m_w2_a, m_g_mix, m_w_in, m_b_f, m_a_re, m_a_im, m_log_dt, m_b_re, m_b_im, m_c_re, m_c_im, m_d_skip, m_w_glu, m_b_glu, m_g_attn_out, m_g_ssm_out, m_w_out, m_g_ffn2, m_w1_b, m_w3_b, m_w2_b, m_g_ple, m_w_ple_gate, m_w_ple_proj, m_g_final, v_g_ffn1, v_w1_a, v_w3_a, v_w2_a, v_g_mix, v_w_in, v_b_f, v_a_re, v_a_im, v_log_dt, v_b_re, v_b_im, v_c_re, v_c_im, v_d_skip, v_w_glu, v_b_glu, v_g_attn_out, v_g_ssm_out, v_w_out, v_g_ffn2, v_w1_b, v_w3_b, v_w2_b, v_g_ple, v_w_ple_gate, v_w_ple_proj, v_g_final):
    args = dict(locals())
    weights = {n: args[n] for n in WEIGHT_ORDER}
    moms = {n: args["m_" + n] for n in WEIGHT_ORDER}
    vars_ = {n: args["v_" + n] for n in WEIGHT_ORDER}
    T = x.shape[1]
    x2, p2, tgt = x[0], p[0, 0], loss_target[0]

    full = _unpack_gathered(allgather_shards(_pack_shards({n: weights[n][0] for n, _, _ in BIG}, BF16)))
    loss_part, dx, grads = _local_step(x2, p2, tgt, {n: weights[n] for n, _ in SMALL}, full)
    return _reduce_and_update(weights, moms, vars_, loss_part, dx, grads)


def _local_step(x2, p2, tgt, sm, full):
    T = x2.shape[0]
    (g_ffn1, g_mix, b_f, a_re, a_im, log_dt, b_re, b_im, c_re, c_im, d_skip, b_glu, g_attn_out, g_ssm_out, g_ffn2, g_ple,
     g_final) = (sm[n] for n, _ in SMALL)
    w_in_f = full["w_in"]
    w_in_r = jnp.concatenate([w_in_f[:, :ATTN_W] * QK_SCALE, w_in_f[:, ATTN_W:3 * ATTN_W], w_in_f[:, 3 * ATTN_W + N_HEADS:],
                              w_in_f[:, 3 * ATTN_W:3 * ATTN_W + N_HEADS], jnp.zeros((D_MODEL, 128 - N_HEADS), BF16)], axis=1)
    b_f_pad = jnp.pad(b_f, ((0, 0), (0, 128 - N_HEADS)))

    disc_in = (a_re[0], a_im[0], log_dt[0], b_re[0], b_im[0])
    (abar_r, abar_i, bbar_r, bbar_i), disc_vjp = jax.vjp(_discretize, *disc_in)
    wb = _input_matrix(bbar_r, bbar_i)
    cbd = _output_matrix(c_re[0], c_im[0])
    ar, ai = abar_r.reshape(1, STATE_W), abar_i.reshape(1, STATE_W)
    alr, ali = _state_power(ar, ai, T // 8)
    dvec = d_skip.reshape(1, SSM_W)
    wb16, cbd16 = wb.astype(BF16), cbd.astype(BF16)

    h1, a1a, a3a, n1 = ffn_fwd(x2, g_ffn1, full["w1_a"], full["w3_a"], full["w2_a"], "ffn_a_fwd")
    u, qkv, s_in, fz, cum = mixin_fwd(h1, g_mix, w_in_r, b_f_pad)
    q_aug, k_aug, v_aug = heads_in(qkv, cum)
    o_heads, q_bwd = attn_fwd(q_aug, k_aug, v_aug)
    s_perm = _permute_time(s_in)
    y_perm, xs = ssm_fwd(s_perm, wb16, cbd16, ar, ai, alr, ali, dvec)
    ypre = _unpermute_time(y_perm)
    h2, mixed = mixout_fwd(h1, o_heads, ypre, g_attn_out, g_ssm_out, full["w_glu"], b_glu, full["w_out"])
    h3, a1b, a3b, n2 = ffn_fwd(h2, g_ffn2, full["w1_b"], full["w3_b"], full["w2_b"], "ffn_b_fwd")

    dh3, n3, dzg, dpp, loss_part, dg_ple, dg_final = head_fwd_bwd(
        h3, p2, tgt, g_ple, g_final.reshape(1, D_MODEL), full["w_ple_gate"], full["w_ple_proj"])
    grads = {"g_ple": dg_ple, "g_final": dg_final.reshape(D_MODEL)}
    grads["w_ple_gate"] = mm_tn(n3, dzg, "dw_ple_gate")
    grads["w_ple_proj"] = mm_tn(p2, dpp, "dw_ple_proj")

    dh2, da1, da3, act, grads["g_ffn2"] = ffn_bwd(h2, g_ffn2, dh3, a1b, a3b, full["w1_b"], full["w3_b"], full["w2_b"], "ffn_b_bwd")
    grads["w1_b"] = mm_tn(n2, da1, "dw1_b")
    grads["w3_b"] = mm_tn(n2, da3, "dw3_b")
    grads["w2_b"] = mm_tn(act, dh3, "dw2_b", scale=0.5)

    seg = (jnp.arange(ATTN_W)[:, None] // HEAD_DIM == jnp.arange(128)[None, :]).astype(F32)
    do_aug, dypre, dpre, yg, grads["g_attn_out"], grads["g_ssm_out"], grads["b_glu"] = mixout_bwd(
        dh2, o_heads, ypre, g_attn_out, g_ssm_out, full["w_glu"], b_glu, full["w_out"], seg)
    grads["w_out"] = mm_tn(mixed, dh2, "dw_out")
    grads["w_glu"] = mm_tn(yg, dpre, "dw_glu")

    dq_aug, dk_aug, dv_aug, dck, dcq = attn_bwd(q_bwd, k_aug, v_aug, do_aug)
    dc = jnp.pad((dck.reshape(N_HEADS, T) + dcq.reshape(N_HEADS, T)).T, ((0, 0), (0, 128 - N_HEADS)))

    dy_perm = _permute_time(dypre)
    du_perm, gs, d_a, dd = ssm_bwd(dy_perm, s_perm, xs, cbd16.T, wb16.T, ar, ai, alr, ali, dvec)
    ds_in = _unpermute_time(du_perm)
    hg = N_GROUPS // 2
    d_in, d_out = [], []
    for part in range(2):
        ins, outs = [], []
        for half in range(2):
            states = (part * STATE_W + half * _HALF_ST, _HALF_ST)
            chans = (half * _HALF_CH, _HALF_CH)
            blk = mm_tn(s_perm, gs, f"dw_ssm_in_{part}{half}", a_cols=chans, b_cols=states)
            ins.append(jnp.einsum("ghgp->ghp", blk.reshape(hg, GROUP_CH, hg, N_STATE)))
            blk = mm_tn(xs, dy_perm, f"dw_ssm_out_{part}{half}", a_cols=states, b_cols=chans)
            outs.append(jnp.einsum("gpgh->gph", blk.reshape(hg, N_STATE, hg, GROUP_CH)))
        d_in.append(jnp.concatenate(ins, axis=0).transpose(0, 2, 1))
        d_out.append(jnp.concatenate(outs, axis=0).transpose(0, 2, 1))
    d_abar_r = jnp.sum(d_a[:, :STATE_W], axis=0).reshape(N_GROUPS, N_STATE)
    d_abar_i = jnp.sum(d_a[:, STATE_W:], axis=0).reshape(N_GROUPS, N_STATE)
    d_disc = disc_vjp((d_abar_r, d_abar_i, d_in[0], d_in[1]))
    for name, val in zip(("a_re", "a_im", "log_dt", "b_re", "b_im"), d_disc):
        grads[name] = val[None]
    grads["c_re"] = d_out[0][None]
    grads["c_im"] = -d_out[1][None]
    grads["d_skip"] = dd.reshape(1, N_GROUPS, GROUP_CH)

    dh1, dz, grads["g_mix"], dbf = mixin_bwd(dh2, h1, g_mix, w_in_r, dq_aug, dk_aug, dv_aug, ds_in, dc, fz)
    grads["b_f"] = dbf[:, :N_HEADS]
    d_w_in_r = mm_tn(u, dz, "dw_in")
    grads["w_in"] = jnp.concatenate([d_w_in_r[:, :ATTN_W] * QK_SCALE, d_w_in_r[:, ATTN_W:3 * ATTN_W],
                                     d_w_in_r[:, 3 * ATTN_W + SSM_W:3 * ATTN_W + SSM_W + N_HEADS],
                                     d_w_in_r[:, 3 * ATTN_W:3 * ATTN_W + SSM_W]], axis=1)

    dx, da1, da3, act, grads["g_ffn1"] = ffn_bwd(x2, g_ffn1, dh1, a1a, a3a, full["w1_a"], full["w3_a"], full["w2_a"], "ffn_a_bwd")
    grads["w1_a"] = mm_tn(n1, da1, "dw1_a")
    grads["w3_a"] = mm_tn(n1, da3, "dw3_a")
    grads["w2_a"] = mm_tn(act, dh1, "dw2_a", scale=0.5)
    return loss_part, dx, grads


def _reduce_and_update(weights, moms, vars_, loss_part, dx, grads):
    core = lax.axis_index("c").astype(jnp.int32).reshape(1)
    chip = (2 * lax.axis_index("x") + lax.axis_index("y")).astype(jnp.int32).reshape(1)
    packed = _pack_full_grads(grads)
    pair = pair_sum(packed, sibling_split(packed), core)
    half = chip_sum(pair, chip_exchange(pair), chip)
    g_out = _unpack_shards(join_halves(half, sibling_swap(half), core))
    d_out, m_out, v_out = {}, {}, {}
    for n, _, _ in BIG:
        d, m, v = adamw(weights[n][0], g_out[n][0], moms[n][0], vars_[n][0], "adamw_" + n)
        d_out[n], m_out[n], v_out[n] = d[None], m[None], v[None]

    small = allreduce_small(_pack_small({n: grads[n] for n, _ in SMALL}, extra=loss_part[0, 0]))
    d_small, m_small, v_small = adamw(_pack_small(weights), small, _pack_small(moms), _pack_small(vars_), "adamw_small")

    g_small, loss = _unpack_small(small)
    g_out.update(g_small)
    outs = []
    for big, sm in ((d_out, d_small), (m_out, m_small), (v_out, v_small)):
        o, _ = _unpack_small(sm)
        o.update(big)
        outs.append(o)
    result = [loss, dx[None]] + [g_out[n] for n in WEIGHT_ORDER]
    for o in outs:
        result += [o[n] for n in WEIGHT_ORDER]
    return tuple(result)
```

```python
import functools
import math

import jax
import jax.numpy as jnp
from jax import lax
from jax.experimental import pallas as pl
from jax.experimental.pallas import tpu as pltpu

F32 = jnp.float32
BF16 = jnp.bfloat16

D_MODEL = 1024
D_FF = 2816
N_HEADS = 8
HEAD_DIM = 64
ATTN_W = 512
SSM_W = 512
N_GROUPS = 32
N_STATE = 64
GROUP_CH = 16
STATE_W = N_GROUPS * N_STATE
Z_COLS = 2176
QK_SCALE = 0.125
EPS = 1e-6

ADAM_LR = 0.001
ADAM_B1 = 0.9
ADAM_B2 = 0.999
ADAM_EPS = 1e-08
ADAM_WD = 0.01
ADAM_STEP = 10

TOKEN_TILE = 512
FFN_TOKEN_TILE = 256
FF_CHUNK = 1408
MM_K_TILE = 2048
ATTN_TILE = 512
ATTN_QUERY_GROUP = 512
SCAN_STEPS = 32
SCAN_LANES = 512
VMEM_LIMIT = 48 * 1024 * 1024
FFN_VMEM_LIMIT = 56 * 1024 * 1024
COPY_CHUNKS = 4

NT_DIMS = (((1,), (1,)), ((), ()))
TN_DIMS = (((0,), (0,)), ((), ()))
HIGHEST = lax.Precision.HIGHEST
MESH = pl.DeviceIdType.MESH

BIG = (
    ("w1_a", (1024, 704), 1), ("w3_a", (1024, 704), 1), ("w2_a", (704, 1024), 0),
    ("w_in", (1024, 514), 1), ("w_glu", (128, 512), 0), ("w_out", (256, 1024), 0),
    ("w1_b", (1024, 704), 1), ("w3_b", (1024, 704), 1), ("w2_b", (704, 1024), 0),
    ("w_ple_gate", (256, 1024), 0), ("w_ple_proj", (256, 256), 1),
)
PACK_COLS = 1024
PACK_ALIGN = 16
PACK_ROWS = 5408
SMALL = (
    ("g_ffn1", (1, 1024)), ("g_mix", (1, 1024)), ("b_f", (1, 8)), ("a_re", (1, 32, 64)), ("a_im", (1, 32, 64)),
    ("log_dt", (1, 32)), ("b_re", (1, 32, 64, 16)), ("b_im", (1, 32, 64, 16)), ("c_re", (1, 32, 16, 64)),
    ("c_im", (1, 32, 16, 64)), ("d_skip", (1, 32, 16)), ("b_glu", (1, 512)), ("g_attn_out", (1, 512)),
    ("g_ssm_out", (1, 512)), ("g_ffn2", (1, 1024)), ("g_ple", (1, 1024)), ("g_final", (1024,)),
)
SMALL_ROWS = 1120
WEIGHT_ORDER = ("g_ffn1", "w1_a", "w3_a", "w2_a", "g_mix", "w_in", "b_f", "a_re", "a_im", "log_dt", "b_re", "b_im",
                "c_re", "c_im", "d_skip", "w_glu", "b_glu", "g_attn_out", "g_ssm_out", "w_out", "g_ffn2", "w1_b",
                "w3_b", "w2_b", "g_ple", "w_ple_gate", "w_ple_proj", "g_final")


def _params(sem=None, vmem=VMEM_LIMIT):
    kw = dict(vmem_limit_bytes=vmem)
    if sem is not None:
        kw["dimension_semantics"] = sem
    return pltpu.CompilerParams(**kw)


def _sds(shape, dtype):
    return jax.ShapeDtypeStruct(shape, dtype)


def _tile(n, pref):
    t = min(n, pref)
    assert n % t == 0, (n, pref)
    return t


def _rms_scale(x):
    return lax.rsqrt(jnp.mean(x * x, axis=-1, keepdims=True) + EPS)


def _rms_bwd(dy, x, g):
    r = _rms_scale(x)
    xh = x * r
    dxh = dy * g
    dx = r * (dxh - xh * jnp.mean(dxh * xh, axis=-1, keepdims=True))
    return dx, jnp.sum(dy * xh, axis=0, keepdims=True)


def _dot(a, b):
    return jnp.dot(a, b, preferred_element_type=F32)


def _dot_nt(a, b):
    return lax.dot_general(a, b, NT_DIMS, preferred_element_type=F32)


def _dot_tn(a, b):
    return lax.dot_general(a, b, TN_DIMS, preferred_element_type=F32)


_GELU_C = math.sqrt(2.0 / math.pi)


def _gelu_parts(x):
    t = jnp.tanh(_GELU_C * (x + 0.044715 * x * x * x))
    return 0.5 * x * (1.0 + t), t


def _gelu_grad(x, t):
    return 0.5 * (1.0 + t) + 0.5 * x * (1.0 - t * t) * _GELU_C * (1.0 + 3.0 * 0.044715 * x * x)


def _resident(shape):
    return pl.BlockSpec(shape, lambda i: (0,) * len(shape), pipeline_mode=pl.Buffered(1))


def ffn_fwd(h, g, w1, w3, w2, name):
    T = h.shape[0]
    tm = _tile(T, FFN_TOKEN_TILE)

    def body(h_ref, g_ref, w1_ref, w3_ref, w2_ref, ho_ref, a1_ref, a3_ref, n_ref):
        x = h_ref[...]
        n = (x * _rms_scale(x) * g_ref[...]).astype(BF16)
        n_ref[...] = n
        out = x
        for lo in range(0, D_FF, FF_CHUNK):
            cols = slice(lo, lo + FF_CHUNK)
            a1 = _dot_nt(n, w1_ref[cols, :])
            a3 = _dot_nt(n, w3_ref[cols, :])
            a1_ref[:, cols] = a1.astype(BF16)
            a3_ref[:, cols] = a3.astype(BF16)
            act = (a1 * jax.nn.sigmoid(a1) * a3).astype(BF16)
            out = out + 0.5 * _dot(act, w2_ref[cols, :])
        ho_ref[...] = out

    tok = lambda i: (i, 0)
    return pl.pallas_call(
        body, name=name, grid=(T // tm,),
        in_specs=[pl.BlockSpec((tm, D_MODEL), tok), _resident((1, D_MODEL)), _resident((D_FF, D_MODEL)),
                  _resident((D_FF, D_MODEL)), _resident((D_FF, D_MODEL))],
        out_specs=[pl.BlockSpec((tm, D_MODEL), tok), pl.BlockSpec((tm, D_FF), tok), pl.BlockSpec((tm, D_FF), tok),
                   pl.BlockSpec((tm, D_MODEL), tok)],
        out_shape=[_sds((T, D_MODEL), F32), _sds((T, D_FF), BF16), _sds((T, D_FF), BF16), _sds((T, D_MODEL), BF16)],
        compiler_params=_params(("arbitrary",), FFN_VMEM_LIMIT),
    )(h, g, w1, w3, w2)


def ffn_bwd(h, g, dho, a1, a3, w1, w3, w2, name):
    T = h.shape[0]
    tm = _tile(T, FFN_TOKEN_TILE)

    def body(h_ref, g_ref, dho_ref, a1_ref, a3_ref, w1_ref, w3_ref, w2_ref, dhi_ref, da1_ref, da3_ref, act_ref, dg_ref):
        @pl.when(pl.program_id(0) == 0)
        def _():
            dg_ref[...] = jnp.zeros_like(dg_ref)

        dho = dho_ref[...]
        dhb = (0.5 * dho).astype(BF16)
        dn = None
        for lo in range(0, D_FF, FF_CHUNK):
            cols = slice(lo, lo + FF_CHUNK)
            a1v = a1_ref[:, cols].astype(F32)
            a3v = a3_ref[:, cols].astype(F32)
            s = jax.nn.sigmoid(a1v)
            sl = a1v * s
            dact = _dot_nt(dhb, w2_ref[cols, :])
            act_ref[:, cols] = (sl * a3v).astype(BF16)
            da1 = (dact * a3v * s * (1.0 + a1v * (1.0 - s))).astype(BF16)
            da3 = (dact * sl).astype(BF16)
            da1_ref[:, cols] = da1
            da3_ref[:, cols] = da3
            part = _dot(da1, w1_ref[cols, :]) + _dot(da3, w3_ref[cols, :])
            dn = part if dn is None else dn + part
        dx, dg = _rms_bwd(dn, h_ref[...], g_ref[...])
        dg_ref[...] += dg
        dhi_ref[...] = dho + dx

    tok = lambda i: (i, 0)
    return pl.pallas_call(
        body, name=name, grid=(T // tm,),
        in_specs=[pl.BlockSpec((tm, D_MODEL), tok), _resident((1, D_MODEL)), pl.BlockSpec((tm, D_MODEL), tok),
                  pl.BlockSpec((tm, D_FF), tok), pl.BlockSpec((tm, D_FF), tok), _resident((D_FF, D_MODEL)),
                  _resident((D_FF, D_MODEL)), _resident((D_FF, D_MODEL))],
        out_specs=[pl.BlockSpec((tm, D_MODEL), tok), pl.BlockSpec((tm, D_FF), tok), pl.BlockSpec((tm, D_FF), tok),
                   pl.BlockSpec((tm, D_FF), tok), pl.BlockSpec((1, D_MODEL), lambda i: (0, 0))],
        out_shape=[_sds((T, D_MODEL), F32), _sds((T, D_FF), BF16), _sds((T, D_FF), BF16), _sds((T, D_FF), BF16),
                   _sds((1, D_MODEL), F32)],
        compiler_params=_params(("arbitrary",), FFN_VMEM_LIMIT),
    )(h, g, dho, a1, a3, w1, w3, w2)


def mm_tn(a, b, name, scale=1.0, a_cols=None, b_cols=None):
    T = a.shape[0]
    a_off, M = a_cols or (0, a.shape[1])
    b_off, N = b_cols or (0, b.shape[1])
    bm = 512 if M % 512 == 0 else (1408 if M == 2816 else 256)
    bn = N if N in (2176, 1408) else (1408 if N == 2816 else min(N, 1024))
    tk = _tile(T, MM_K_TILE)
    row_bytes = 2 * (bm * a.dtype.itemsize + bn * b.dtype.itemsize)
    while tk > TOKEN_TILE and tk * row_bytes > VMEM_LIMIT // 3:
        tk //= 2
    assert M % bm == 0 and N % bn == 0 and T % tk == 0 and a_off % bm == 0 and b_off % bn == 0
    n_k = T // tk
    m0, n0 = a_off // bm, b_off // bn

    def body(a_ref, b_ref, o_ref):
        k = pl.program_id(2)

        @pl.when(k == 0)
        def _():
            o_ref[...] = jnp.zeros_like(o_ref)

        o_ref[...] += _dot_tn(a_ref[...].astype(BF16), b_ref[...].astype(BF16))

        if scale != 1.0:
            @pl.when(k == n_k - 1)
            def _():
                o_ref[...] = o_ref[...] * scale

    return pl.pallas_call(
        body, name=name, grid=(M // bm, N // bn, n_k),
        in_specs=[pl.BlockSpec((tk, bm), lambda m, n, k: (k, m0 + m)), pl.BlockSpec((tk, bn), lambda m, n, k: (k, n0 + n))],
        out_specs=pl.BlockSpec((bm, bn), lambda m, n, k: (m, n)),
        out_shape=_sds((M, N), F32),
        compiler_params=_params(("arbitrary", "arbitrary", "arbitrary")),
    )(a, b)


def mixin_fwd(h1, g, w_in_r, b_f_pad):
    T = h1.shape[0]
    tm = _tile(T, TOKEN_TILE)

    def body(h_ref, g_ref, w_ref, bf_ref, u_ref, qkv_ref, s_ref, fz_ref, c_ref, carry):
        @pl.when(pl.program_id(0) == 0)
        def _():
            carry[...] = jnp.zeros_like(carry)

        x = h_ref[...]
        u = (x * _rms_scale(x) * g_ref[...]).astype(BF16)
        u_ref[...] = u
        z = _dot(u, w_ref[...])
        qkv_ref[...] = z[:, :3 * ATTN_W].astype(BF16)
        s_ref[...] = z[:, 3 * ATTN_W:3 * ATTN_W + SSM_W]
        fz = z[:, 3 * ATTN_W + SSM_W:] + bf_ref[...]
        fz_ref[...] = fz
        lane = lax.broadcasted_iota(jnp.int32, fz.shape, 1)
        logf = jnp.where(lane < N_HEADS, jnp.minimum(fz, 0.0) - jnp.log(1.0 + jnp.exp(-jnp.abs(fz))), 0.0)
        row = lax.broadcasted_iota(jnp.int32, (tm, tm), 0)
        col = lax.broadcasted_iota(jnp.int32, (tm, tm), 1)
        tri = (col <= row).astype(F32)
        cs = jnp.dot(tri, logf, precision=HIGHEST, preferred_element_type=F32) + carry[0:1, :]
        c_ref[...] = cs
        carry[...] = jnp.broadcast_to(cs[tm - 1:tm, :], carry.shape)

    tok = lambda i: (i, 0)
    fix = lambda i: (0, 0)
    return pl.pallas_call(
        body, name="mixin_fwd", grid=(T // tm,),
        in_specs=[pl.BlockSpec((tm, D_MODEL), tok), pl.BlockSpec((1, D_MODEL), fix),
                  pl.BlockSpec((D_MODEL, Z_COLS), fix), pl.BlockSpec((1, 128), fix)],
        out_specs=[pl.BlockSpec((tm, D_MODEL), tok), pl.BlockSpec((tm, 3 * ATTN_W), tok), pl.BlockSpec((tm, SSM_W), tok),
                   pl.BlockSpec((tm, 128), tok), pl.BlockSpec((tm, 128), tok)],
        out_shape=[_sds((T, D_MODEL), BF16), _sds((T, 3 * ATTN_W), BF16), _sds((T, SSM_W), F32),
                   _sds((T, 128), F32), _sds((T, 128), F32)],
        scratch_shapes=[pltpu.VMEM((8, 128), F32)],
        compiler_params=_params(("arbitrary",)),
    )(h1, g, w_in_r, b_f_pad)


def mixin_bwd(dh2, h1, g, w_in_r, dq, dk, dv, ds, dc, fz):
    T = h1.shape[0]
    tm = _tile(T, TOKEN_TILE)
    n_t = T // tm

    def body(dh2_ref, h_ref, g_ref, w_ref, dq_ref, dk_ref, dv_ref, ds_ref, dc_ref, fz_ref,
             dh1_ref, dz_ref, dg_ref, dbf_ref, carry):
        @pl.when(pl.program_id(0) == 0)
        def _():
            carry[...] = jnp.zeros_like(carry)
            dg_ref[...] = jnp.zeros_like(dg_ref)
            dbf_ref[...] = jnp.zeros_like(dbf_ref)

        row = lax.broadcasted_iota(jnp.int32, (tm, tm), 0)
        col = lax.broadcasted_iota(jnp.int32, (tm, tm), 1)
        tri = (col >= row).astype(F32)
        dlogf = jnp.dot(tri, dc_ref[...], precision=HIGHEST, preferred_element_type=F32) + carry[0:1, :]
        carry[...] = jnp.broadcast_to(dlogf[0:1, :], carry.shape)
        dfz = dlogf * jax.nn.sigmoid(-fz_ref[...])
        dbf_ref[...] += jnp.sum(dfz, axis=0, keepdims=True)
        dz = jnp.concatenate([_join_heads(dq_ref, BF16), _join_heads(dk_ref, BF16), _join_heads(dv_ref, BF16),
                              ds_ref[...], dfz], axis=1).astype(BF16)
        dz_ref[...] = dz
        du = _dot_nt(dz, w_ref[...])
        dx, dg = _rms_bwd(du, h_ref[...], g_ref[...])
        dg_ref[...] += dg
        dh1_ref[...] = dh2_ref[...] + dx

    tok = lambda i: (n_t - 1 - i, 0)
    fix = lambda i: (0, 0)
    heads = pl.BlockSpec((N_HEADS, tm, 128), lambda i: (0, n_t - 1 - i, 0))
    return pl.pallas_call(
        body, name="mixin_bwd", grid=(n_t,),
        in_specs=[pl.BlockSpec((tm, D_MODEL), tok), pl.BlockSpec((tm, D_MODEL), tok), pl.BlockSpec((1, D_MODEL), fix),
                  pl.BlockSpec((D_MODEL, Z_COLS), fix), heads, heads, heads, pl.BlockSpec((tm, SSM_W), tok),
                  pl.BlockSpec((tm, 128), tok), pl.BlockSpec((tm, 128), tok)],
        out_specs=[pl.BlockSpec((tm, D_MODEL), tok), pl.BlockSpec((tm, Z_COLS), tok), pl.BlockSpec((1, D_MODEL), fix),
                   pl.BlockSpec((1, 128), fix)],
        out_shape=[_sds((T, D_MODEL), F32), _sds((T, Z_COLS), BF16), _sds((1, D_MODEL), F32), _sds((1, 128), F32)],
        scratch_shapes=[pltpu.VMEM((8, 128), F32)],
        compiler_params=_params(("arbitrary",)),
    )(dh2, h1, g, w_in_r, dq, dk, dv, ds, dc, fz)


def _lane_move(src_lo, dst_lo, width, dtype):
    r = lax.broadcasted_iota(jnp.int32, (128, 128), 0)
    c = lax.broadcasted_iota(jnp.int32, (128, 128), 1)
    return ((c - dst_lo == r - src_lo) & (r >= src_lo) & (r < src_lo + width)).astype(dtype)


def _lane_const(lo, width, value):
    lane = lax.broadcasted_iota(jnp.int32, (1, 128), 1)
    return jnp.where((lane >= lo) & (lane < lo + width), value, 0.0).astype(F32)


def _pieces(a):
    hi = a.astype(BF16)
    rest = a - hi.astype(F32)
    mid = rest.astype(BF16)
    return hi, mid, (rest - mid.astype(F32)).astype(BF16)


def _head_features(pair_block, e):
    return _dot(pair_block, _lane_move(HEAD_DIM * e, 0, HEAD_DIM, BF16))


def _helper_columns(pieces, head, sign):
    out = None
    for k, piece in enumerate(pieces):
        term = _dot(piece, _lane_move(head, HEAD_DIM + k, 1, BF16))
        out = term if out is None else out + term
    return sign * out


def heads_in(qkv, cum):
    T = qkv.shape[0]
    tm = _tile(T, TOKEN_TILE)

    def body(qkv_ref, c_ref, q_ref, k_ref, v_ref):
        c = _pieces(c_ref[...])
        for h in range(N_HEADS):
            p, e = divmod(h, 2)
            blk = lambda base: qkv_ref[:, base + 128 * p:base + 128 * (p + 1)]
            q_ref[h] = (_head_features(blk(0), e) + _lane_const(HEAD_DIM, 3, -1.0)).astype(BF16)
            k_ref[h] = (_head_features(blk(ATTN_W), e) + _helper_columns(c, h, 1.0)
                        + _lane_const(HEAD_DIM + 3, 3, 1.0)).astype(BF16)
            v_ref[h] = (_head_features(blk(2 * ATTN_W), e) + _lane_const(HEAD_DIM, 3, 1.0)).astype(BF16)

    tok = lambda i: (i, 0)
    heads = pl.BlockSpec((N_HEADS, tm, 128), lambda i: (0, i, 0))
    return pl.pallas_call(
        body, name="heads_in", grid=(T // tm,),
        in_specs=[pl.BlockSpec((tm, 3 * ATTN_W), tok), pl.BlockSpec((tm, 128), tok)],
        out_specs=[heads] * 3, out_shape=[_sds((N_HEADS, T, 128), BF16)] * 3,
        compiler_params=_params(("arbitrary",)),
    )(qkv, cum)


def attn_fwd(q_aug, k_aug, v_aug):
    H, T, wd = q_aug.shape
    hd = HEAD_DIM
    tq = _tile(T, ATTN_TILE)
    n = T // tq

    qg = min(tq, ATTN_QUERY_GROUP)

    def body(q_ref, k_ref, v_ref, o_ref, qb_ref, m_sc, acc):
        qi = pl.program_id(1)
        qv = q_ref[0]
        m_sc[...] = jnp.full_like(m_sc, -jnp.inf)
        acc[...] = jnp.zeros_like(acc)

        def tile(j, masked):
            rows = pl.ds(pl.multiple_of(j * tq, tq), tq)
            kv, vv = k_ref[0, rows, :], v_ref[0, rows, :]
            for lo in range(0, tq, qg):
                lanes = slice(lo, lo + qg)
                st = _dot_nt(kv, qv[lanes, :])
                if masked:
                    keep = (lax.broadcasted_iota(jnp.int32, (tq, qg), 0)
                            <= lo + lax.broadcasted_iota(jnp.int32, (tq, qg), 1))
                    st = jnp.where(keep, st, -1e30)
                m_old = m_sc[:, lanes]
                m_new = jnp.maximum(m_old, jnp.max(st, axis=0, keepdims=True))
                pt = jnp.exp(st - m_new).astype(BF16)
                acc[:, lanes] = jnp.exp(m_old - m_new) * acc[:, lanes] + _dot_tn(vv, pt)
                m_sc[:, lanes] = m_new

        def off_diagonal(j, carry):
            tile(j, False)
            return carry

        lax.fori_loop(0, qi, off_diagonal, 0)
        tile(qi, True)
        total = acc[hd:hd + 1, :]
        o_ref[0] = (acc[...] / total).T
        hi, mid, lo = (t.astype(F32) for t in _pieces(-(m_sc[...] + jnp.log(total))))
        row = lax.broadcasted_iota(jnp.int32, (wd, tq), 0)
        lse_rows = jnp.where(row == hd + 3, hi, jnp.where(row == hd + 4, mid, jnp.where(row == hd + 5, lo, 0.0)))
        qb_ref[0] = (qv.astype(F32) + lse_rows.T).astype(BF16)

    qmap = lambda h, i: (h, i, 0)
    head = lambda h, i: (h, 0, 0)
    return pl.pallas_call(
        body, name="attn_fwd", grid=(H, n),
        in_specs=[pl.BlockSpec((1, tq, wd), qmap), pl.BlockSpec((1, T, wd), head), pl.BlockSpec((1, T, wd), head)],
        out_specs=[pl.BlockSpec((1, tq, wd), qmap), pl.BlockSpec((1, tq, wd), qmap)],
        out_shape=[_sds((H, T, wd), F32), _sds((H, T, wd), BF16)],
        scratch_shapes=[pltpu.VMEM((1, tq), F32), pltpu.VMEM((wd, tq), F32)],
        compiler_params=_params(("arbitrary", "arbitrary")),
    )(q_aug, k_aug, v_aug)


def attn_bwd(q_aug, k_aug, v_aug, do_aug):
    H, T, wd = q_aug.shape
    tq = _tile(T, ATTN_TILE)
    n = T // tq

    def body(q_ref, do_ref, k_ref, v_ref, dq_ref, dk_ref, dv_ref, dc_ref, dck):
        j = pl.program_id(1)

        @pl.when(j == 0)
        def _():
            dq_ref[...] = jnp.zeros_like(dq_ref)
            dc_ref[...] = jnp.zeros_like(dc_ref)

        dk_ref[...] = jnp.zeros_like(dk_ref)
        dv_ref[...] = jnp.zeros_like(dv_ref)
        dck[...] = jnp.zeros_like(dck)
        kv, vv = k_ref[0], v_ref[0]

        def tile(i, masked):
            rows = pl.ds(pl.multiple_of(i * tq, tq), tq)
            qv, dov = q_ref[0, rows, :], do_ref[0, rows, :]
            pt = jnp.exp(_dot_nt(kv, qv))
            if masked:
                keep = lax.broadcasted_iota(jnp.int32, (tq, tq), 0) <= lax.broadcasted_iota(jnp.int32, (tq, tq), 1)
                pt = jnp.where(keep, pt, 0.0)
            dv_ref[0] += _dot(pt.astype(BF16), dov)
            dst = pt * _dot_nt(vv, dov)
            dsb = dst.astype(BF16)
            dk_ref[0] += _dot(dsb, qv)
            dq_ref[0, rows, :] += _dot_tn(dsb, kv)
            dck[...] += jnp.sum(dst, axis=1, keepdims=True)
            dc_ref[0, pl.ds(i, 1), :] += jnp.sum(dst, axis=0, keepdims=True)

        def off_diagonal(i, carry):
            tile(i, False)
            return carry

        tile(j, True)
        lax.fori_loop(j + 1, n, off_diagonal, 0)
        dc_ref[0, pl.ds(j, 1), :] -= jnp.broadcast_to(dck[...], (tq, 128)).T[0:1, :]

    head = lambda h, j: (h, 0, 0)
    kmap = lambda h, j: (h, j, 0)
    return pl.pallas_call(
        body, name="attn_bwd", grid=(H, n),
        in_specs=[pl.BlockSpec((1, T, wd), head), pl.BlockSpec((1, T, wd), head), pl.BlockSpec((1, tq, wd), kmap),
                  pl.BlockSpec((1, tq, wd), kmap)],
        out_specs=[pl.BlockSpec((1, T, wd), head), pl.BlockSpec((1, tq, wd), kmap), pl.BlockSpec((1, tq, wd), kmap),
                   pl.BlockSpec((1, n, tq), head)],
        out_shape=[_sds((H, T, wd), F32), _sds((H, T, wd), F32), _sds((H, T, wd), F32), _sds((H, n, tq), F32)],
        scratch_shapes=[pltpu.VMEM((tq, 1), F32)],
        compiler_params=_params(("arbitrary", "arbitrary")),
    )(q_aug, do_aug, k_aug, v_aug)


def _complex_step(a_r, a_i, cr, ci, br, bi):
    return a_r * cr - a_i * ci + br, a_r * ci + a_i * cr + bi


_HALF_CH = SSM_W // 2
_HALF_ST = STATE_W // 2


def _state_cols(part, half):
    lo = part * STATE_W + half * _HALF_ST
    return slice(lo, lo + _HALF_ST)


def _channels_to_states(x, w_ref, out_ref):
    for half in range(2):
        ch = slice(half * _HALF_CH, (half + 1) * _HALF_CH)
        for part in range(2):
            cols = _state_cols(part, half)
            out_ref[:, cols] = _dot(x[:, ch], w_ref[ch, cols])


def _states_to_channels(x, w_ref):
    halves = []
    for half in range(2):
        ch = slice(half * _HALF_CH, (half + 1) * _HALF_CH)
        halves.append(_dot(x[:, _state_cols(0, half)], w_ref[_state_cols(0, half), ch])
                      + _dot(x[:, _state_cols(1, half)], w_ref[_state_cols(1, half), ch]))
    return jnp.concatenate(halves, axis=1)


def ssm_fwd(s_perm, wb, cbd, a_r, a_i, al_r, al_i, dvec):
    T = s_perm.shape[0]
    chunk = T // 8
    ts = _tile(chunk, SCAN_STEPS)
    tr, n_s = ts * 8, chunk // ts
    W, LB = STATE_W, SCAN_LANES

    def body(s_ref, wb_ref, cbd_ref, ar_ref, ai_ref, alr_ref, ali_ref, dv_ref, y_ref, xs_ref, bu, carry):
        ph, i = pl.program_id(0), pl.program_id(1)

        @pl.when((ph == 0) & (i == 0))
        def _():
            carry[...] = jnp.zeros_like(carry)

        _channels_to_states(s_ref[...].astype(BF16), wb_ref, bu)

        def scan(store):
            for lb in range(W // LB):
                lo = lb * LB
                re, im = slice(lo, lo + LB), slice(W + lo, W + lo + LB)
                ar = jnp.broadcast_to(ar_ref[:, re], (8, LB))
                ai = jnp.broadcast_to(ai_ref[:, re], (8, LB))

                def step(s, c):
                    rows = pl.ds(pl.multiple_of(s * 8, 8), 8)
                    nr, ni = _complex_step(ar, ai, c[0], c[1], bu[rows, re], bu[rows, im])
                    if store:
                        bu[rows, re] = nr
                        bu[rows, im] = ni
                    return nr, ni

                cr, ci = lax.fori_loop(0, ts, step, (carry[:, re], carry[:, im]), unroll=2)
                carry[:, re] = cr
                carry[:, im] = ci

        @pl.when(ph == 0)
        def _():
            scan(False)

            @pl.when(i == n_s - 1)
            def _():
                er, ei = carry[:, :W], carry[:, W:]
                alr = jnp.broadcast_to(alr_ref[...], (8, W))
                ali = jnp.broadcast_to(ali_ref[...], (8, W))
                first = lax.broadcasted_iota(jnp.int32, (8, W), 0) == 0
                sr, si = jnp.zeros((8, W), F32), jnp.zeros((8, W), F32)
                for _ in range(7):
                    vr, vi = _complex_step(alr, ali, sr, si, er, ei)
                    sr = jnp.where(first, 0.0, pltpu.roll(vr, 1, 0))
                    si = jnp.where(first, 0.0, pltpu.roll(vi, 1, 0))
                carry[:, :W] = sr
                carry[:, W:] = si

        @pl.when(ph == 1)
        def _():
            scan(True)
            xb = bu[...].astype(BF16)
            xs_ref[...] = xb
            y_ref[...] = _states_to_channels(xb, cbd_ref) + s_ref[...] * dv_ref[...]

    fix = lambda p, i: (0, 0)
    return pl.pallas_call(
        body, name="ssm_fwd", grid=(2, n_s),
        in_specs=[pl.BlockSpec((tr, SSM_W), lambda p, i: (i, 0)), pl.BlockSpec((SSM_W, 2 * W), fix),
                  pl.BlockSpec((2 * W, SSM_W), fix), pl.BlockSpec((1, W), fix), pl.BlockSpec((1, W), fix),
                  pl.BlockSpec((1, W), fix), pl.BlockSpec((1, W), fix), pl.BlockSpec((1, SSM_W), fix)],
        out_specs=[pl.BlockSpec((tr, SSM_W), lambda p, i: (i * p, 0)), pl.BlockSpec((tr, 2 * W), lambda p, i: (i * p, 0))],
        out_shape=[_sds((T, SSM_W), F32), _sds((T, 2 * W), BF16)],
        scratch_shapes=[pltpu.VMEM((tr, 2 * W), F32), pltpu.VMEM((8, 2 * W), F32)],
        compiler_params=_params(("arbitrary", "arbitrary")),
    )(s_perm, wb, cbd, a_r, a_i, al_r, al_i, dvec)


def ssm_bwd(dy_perm, s_perm, xs, cbd_t, wb_t, a_r, a_i, al_r, al_i, dvec):
    T = s_perm.shape[0]
    chunk = T // 8
    ts = _tile(chunk, SCAN_STEPS)
    tr, n_s = ts * 8, chunk // ts
    W, LB = STATE_W, SCAN_LANES

    def body(dy_ref, s_ref, xs_ref, cbt_ref, wbt_ref, ar_ref, ai_ref, alr_ref, ali_ref, dv_ref,
             du_ref, gs_ref, da_ref, dd_ref, gd, x32, carry):
        ph, i = pl.program_id(0), pl.program_id(1)

        @pl.when((ph == 0) & (i == 0))
        def _():
            carry[...] = jnp.zeros_like(carry)
            da_ref[...] = jnp.zeros_like(da_ref)
            dd_ref[...] = jnp.zeros_like(dd_ref)

        _channels_to_states(dy_ref[...].astype(BF16), cbt_ref, gd)

        def scan(store):
            for lb in range(W // LB):
                lo = lb * LB
                re, im = slice(lo, lo + LB), slice(W + lo, W + lo + LB)
                ar = jnp.broadcast_to(ar_ref[:, re], (8, LB))
                nai = -jnp.broadcast_to(ai_ref[:, re], (8, LB))

                def step(k, c):
                    rows = pl.ds(pl.multiple_of((ts - 1 - k) * 8, 8), 8)
                    cr, ci = c[0], c[1]
                    nr, ni = _complex_step(ar, nai, cr, ci, gd[rows, re], gd[rows, im])
                    if store:
                        xr, xi = x32[rows, re], x32[rows, im]
                        gd[rows, re] = nr
                        gd[rows, im] = ni
                        return nr, ni, c[2] + cr * xr + ci * xi, c[3] + ci * xr - cr * xi
                    return nr, ni

                init = (carry[:, re], carry[:, im])
                if store:
                    init = init + (da_ref[:, re], da_ref[:, im])
                out = lax.fori_loop(0, ts, step, init, unroll=2)
                carry[:, re] = out[0]
                carry[:, im] = out[1]
                if store:
                    da_ref[:, re] = out[2]
                    da_ref[:, im] = out[3]

        @pl.when(ph == 0)
        def _():
            scan(False)

            @pl.when(i == n_s - 1)
            def _():
                er, ei = carry[:, :W], carry[:, W:]
                alr = jnp.broadcast_to(alr_ref[...], (8, W))
                nali = -jnp.broadcast_to(ali_ref[...], (8, W))
                last = lax.broadcasted_iota(jnp.int32, (8, W), 0) == 7
                rr, ri = jnp.zeros((8, W), F32), jnp.zeros((8, W), F32)
                for _ in range(7):
                    vr, vi = _complex_step(alr, nali, rr, ri, er, ei)
                    rr = jnp.where(last, 0.0, pltpu.roll(vr, 7, 0))
                    ri = jnp.where(last, 0.0, pltpu.roll(vi, 7, 0))
                carry[:, :W] = rr
                carry[:, W:] = ri

        @pl.when(ph == 1)
        def _():
            x32[...] = xs_ref[...].astype(F32)
            scan(True)
            gb = gd[...].astype(BF16)
            gs_ref[...] = gb
            dy = dy_ref[...]
            du_ref[...] = _states_to_channels(gb, wbt_ref) + dy * dv_ref[...]
            dd_ref[...] += jnp.sum(dy * s_ref[...], axis=0, keepdims=True)

    fix = lambda p, i: (0, 0)
    rev = lambda p, i: (n_s - 1 - i, 0)
    rev_out = lambda p, i: (n_s - 1 - i * p, 0)
    return pl.pallas_call(
        body, name="ssm_bwd", grid=(2, n_s),
        in_specs=[pl.BlockSpec((tr, SSM_W), rev), pl.BlockSpec((tr, SSM_W), rev), pl.BlockSpec((tr, 2 * W), rev),
                  pl.BlockSpec((SSM_W, 2 * W), fix), pl.BlockSpec((2 * W, SSM_W), fix), pl.BlockSpec((1, W), fix),
                  pl.BlockSpec((1, W), fix), pl.BlockSpec((1, W), fix), pl.BlockSpec((1, W), fix),
                  pl.BlockSpec((1, SSM_W), fix)],
        out_specs=[pl.BlockSpec((tr, SSM_W), rev_out), pl.BlockSpec((tr, 2 * W), rev_out),
                   pl.BlockSpec((8, 2 * W), fix), pl.BlockSpec((1, SSM_W), fix)],
        out_shape=[_sds((T, SSM_W), F32), _sds((T, 2 * W), BF16), _sds((8, 2 * W), F32), _sds((1, SSM_W), F32)],
        scratch_shapes=[pltpu.VMEM((tr, 2 * W), F32), pltpu.VMEM((tr, 2 * W), F32), pltpu.VMEM((8, 2 * W), F32)],
        compiler_params=_params(("arbitrary", "arbitrary")),
    )(dy_perm, s_perm, xs, cbd_t, wb_t, a_r, a_i, al_r, al_i, dvec)


def _join_heads(ref, dtype):
    def move(h, dst):
        x = ref[h]
        pieces = _pieces(x) if dtype == F32 else (x.astype(BF16),)
        out = None
        for piece in pieces:
            term = _dot(piece, _lane_move(0, dst, HEAD_DIM, BF16))
            out = term if out is None else out + term
        return out

    return jnp.concatenate([move(2 * p, 0) + move(2 * p + 1, HEAD_DIM) for p in range(N_HEADS // 2)], axis=1)


def mixout_fwd(h1, o_heads, ypre, g_a, g_s, w_glu, b_glu, w_out):
    T = h1.shape[0]
    tm = _tile(T, TOKEN_TILE)

    def body(h_ref, at_ref, yp_ref, ga_ref, gs_ref, wg_ref, bg_ref, wo_ref, h2_ref, mixed_ref):
        yg, _ = _gelu_parts(yp_ref[...])
        gl = yg * jax.nn.sigmoid(_dot(yg.astype(BF16), wg_ref[...]) + bg_ref[...])
        at = _join_heads(at_ref, F32)
        mixed = jnp.concatenate([at * _rms_scale(at) * ga_ref[...], gl * _rms_scale(gl) * gs_ref[...]], axis=1)
        mixed = mixed.astype(BF16)
        mixed_ref[...] = mixed
        h2_ref[...] = h_ref[...] + _dot(mixed, wo_ref[...])

    tok = lambda i: (i, 0)
    fix = lambda i: (0, 0)
    return pl.pallas_call(
        body, name="mixout_fwd", grid=(T // tm,),
        in_specs=[pl.BlockSpec((tm, D_MODEL), tok), pl.BlockSpec((N_HEADS, tm, 128), lambda i: (0, i, 0)),
                  pl.BlockSpec((tm, SSM_W), tok),
                  pl.BlockSpec((1, ATTN_W), fix), pl.BlockSpec((1, SSM_W), fix), pl.BlockSpec((SSM_W, SSM_W), fix),
                  pl.BlockSpec((1, SSM_W), fix), pl.BlockSpec((D_MODEL, D_MODEL), fix)],
        out_specs=[pl.BlockSpec((tm, D_MODEL), tok), pl.BlockSpec((tm, D_MODEL), tok)],
        out_shape=[_sds((T, D_MODEL), F32), _sds((T, D_MODEL), BF16)],
        compiler_params=_params(("arbitrary",)),
    )(h1, o_heads, ypre, g_a, g_s, w_glu, b_glu, w_out)


def mixout_bwd(dh2, o_heads, ypre, g_a, g_s, w_glu, b_glu, w_out, seg):
    T = dh2.shape[0]
    tm = _tile(T, TOKEN_TILE)

    def body(dh_ref, at_ref, yp_ref, ga_ref, gs_ref, wg_ref, bg_ref, wo_ref, seg_ref,
             do_ref, dyp_ref, dpre_ref, yg_ref, dga_ref, dgs_ref, dbg_ref):
        @pl.when(pl.program_id(0) == 0)
        def _():
            dga_ref[...] = jnp.zeros_like(dga_ref)
            dgs_ref[...] = jnp.zeros_like(dgs_ref)
            dbg_ref[...] = jnp.zeros_like(dbg_ref)

        dmix = _dot_nt(dh_ref[...].astype(BF16), wo_ref[...])
        at = _join_heads(at_ref, F32)
        dat, dga = _rms_bwd(dmix[:, :ATTN_W], at, ga_ref[...])
        dga_ref[...] += dga
        delta = _pieces(jnp.dot(dat * at, seg_ref[...], precision=HIGHEST, preferred_element_type=F32))
        datb = dat.astype(BF16)
        for h in range(N_HEADS):
            p, e = divmod(h, 2)
            do_ref[h] = (_head_features(datb[:, 128 * p:128 * (p + 1)], e) + _helper_columns(delta, h, -1.0)).astype(BF16)
        yp = yp_ref[...]
        yg, t = _gelu_parts(yp)
        ygb = yg.astype(BF16)
        yg_ref[...] = ygb
        sg = jax.nn.sigmoid(_dot(ygb, wg_ref[...]) + bg_ref[...])
        dgl, dgs = _rms_bwd(dmix[:, ATTN_W:], yg * sg, gs_ref[...])
        dgs_ref[...] += dgs
        dpre = dgl * yg * sg * (1.0 - sg)
        dbg_ref[...] += jnp.sum(dpre, axis=0, keepdims=True)
        dpb = dpre.astype(BF16)
        dpre_ref[...] = dpb
        dyg = dgl * sg + _dot_nt(dpb, wg_ref[...])
        dyp_ref[...] = dyg * _gelu_grad(yp, t)

    tok = lambda i: (i, 0)
    fix = lambda i: (0, 0)
    heads = pl.BlockSpec((N_HEADS, tm, 128), lambda i: (0, i, 0))
    return pl.pallas_call(
        body, name="mixout_bwd", grid=(T // tm,),
        in_specs=[pl.BlockSpec((tm, D_MODEL), tok), heads, pl.BlockSpec((tm, SSM_W), tok),
                  pl.BlockSpec((1, ATTN_W), fix), pl.BlockSpec((1, SSM_W), fix), pl.BlockSpec((SSM_W, SSM_W), fix),
                  pl.BlockSpec((1, SSM_W), fix), pl.BlockSpec((D_MODEL, D_MODEL), fix), pl.BlockSpec((ATTN_W, 128), fix)],
        out_specs=[heads, pl.BlockSpec((tm, SSM_W), tok), pl.BlockSpec((tm, SSM_W), tok),
                   pl.BlockSpec((tm, SSM_W), tok), pl.BlockSpec((1, ATTN_W), fix),
                   pl.BlockSpec((1, SSM_W), fix), pl.BlockSpec((1, SSM_W), fix)],
        out_shape=[_sds((N_HEADS, T, 128), BF16), _sds((T, SSM_W), F32), _sds((T, SSM_W), BF16), _sds((T, SSM_W), BF16),
                   _sds((1, ATTN_W), F32), _sds((1, SSM_W), F32), _sds((1, SSM_W), F32)],
        compiler_params=_params(("arbitrary",)),
    )(dh2, o_heads, ypre, g_a, g_s, w_glu, b_glu, w_out, seg)


def head_fwd_bwd(h3, p, target, g_ple, g_final, w_gate, w_proj):
    T = h3.shape[0]
    tm = _tile(T, TOKEN_TILE)
    pd = p.shape[1]

    def body(h_ref, p_ref, tg_ref, gp_ref, gf_ref, wg_ref, wp_ref,
             dh_ref, n3_ref, dz_ref, dpp_ref, loss_ref, dgp_ref, dgf_ref):
        @pl.when(pl.program_id(0) == 0)
        def _():
            loss_ref[...] = jnp.zeros_like(loss_ref)
            dgp_ref[...] = jnp.zeros_like(dgp_ref)
            dgf_ref[...] = jnp.zeros_like(dgf_ref)

        x = h_ref[...]
        gp, gf = gp_ref[...], gf_ref[...]
        n3 = (x * _rms_scale(x) * gp).astype(BF16)
        n3_ref[...] = n3
        gate = jax.nn.sigmoid(_dot(n3, wg_ref[...]))
        pp = _dot(p_ref[...].astype(BF16), wp_ref[...])
        h4 = x + gate * pp
        y = h4 * _rms_scale(h4) * gf
        e = y - tg_ref[...]
        tile_loss = jnp.sum(jnp.sum(e * e, axis=1, keepdims=True), axis=0, keepdims=True) * (0.5 / D_MODEL)
        loss_ref[...] += jnp.broadcast_to(tile_loss, loss_ref.shape)
        dh4, dgf = _rms_bwd(e * (1.0 / D_MODEL), h4, gf)
        dgf_ref[...] += dgf
        dzg = dh4 * pp * gate * (1.0 - gate)
        dzb = dzg.astype(BF16)
        dz_ref[...] = dzb
        dpp_ref[...] = (dh4 * gate).astype(BF16)
        dx, dgp = _rms_bwd(_dot_nt(dzb, wg_ref[...]), x, gp)
        dgp_ref[...] += dgp
        dh_ref[...] = dh4 + dx

    tok = lambda i: (i, 0)
    fix = lambda i: (0, 0)
    return pl.pallas_call(
        body, name="head_fwd_bwd", grid=(T // tm,),
        in_specs=[pl.BlockSpec((tm, D_MODEL), tok), pl.BlockSpec((tm, pd), tok), pl.BlockSpec((tm, D_MODEL), tok),
                  pl.BlockSpec((1, D_MODEL), fix), pl.BlockSpec((1, D_MODEL), fix), pl.BlockSpec((D_MODEL, D_MODEL), fix),
                  pl.BlockSpec((pd, D_MODEL), fix)],
        out_specs=[pl.BlockSpec((tm, D_MODEL), tok), pl.BlockSpec((tm, D_MODEL), tok), pl.BlockSpec((tm, D_MODEL), tok),
                   pl.BlockSpec((tm, D_MODEL), tok), pl.BlockSpec((8, 128), fix), pl.BlockSpec((1, D_MODEL), fix),
                   pl.BlockSpec((1, D_MODEL), fix)],
        out_shape=[_sds((T, D_MODEL), F32), _sds((T, D_MODEL), BF16), _sds((T, D_MODEL), BF16), _sds((T, D_MODEL), BF16),
                   _sds((8, 128), F32), _sds((1, D_MODEL), F32), _sds((1, D_MODEL), F32)],
        compiler_params=_params(("arbitrary",)),
    )(h3, p, target, g_ple, g_final, w_gate, w_proj)


def _row_tile(rows, cols, n_arrays):
    lanes = -(-cols // 128) * 128
    cap = VMEM_LIMIT // 3 // (2 * n_arrays * lanes * 4)
    best = None
    for t in range(PACK_ALIGN, min(rows, cap) + 1, PACK_ALIGN):
        if rows % t == 0:
            best = t
    assert best is not None, (rows, cols)
    return best


def _adamw_math(w, g, m, v):
    nm = ADAM_B1 * m + (1.0 - ADAM_B1) * g
    nv = ADAM_B2 * v + (1.0 - ADAM_B2) * (g * g)
    c1 = 1.0 - ADAM_B1 ** ADAM_STEP
    c2 = 1.0 - ADAM_B2 ** ADAM_STEP
    return -ADAM_LR * ((nm / c1) / (jnp.sqrt(nv / c2) + ADAM_EPS) + ADAM_WD * w), nm, nv


def adamw(w, g, m, v, name):
    R, C = w.shape
    tr = _row_tile(R, C, 7)

    def body(w_ref, g_ref, m_ref, v_ref, d_ref, nm_ref, nv_ref):
        d_ref[...], nm_ref[...], nv_ref[...] = _adamw_math(w_ref[...], g_ref[...], m_ref[...], v_ref[...])

    spec = pl.BlockSpec((tr, C), lambda i: (i, 0))
    return pl.pallas_call(
        body, name=name, grid=(R // tr,), in_specs=[spec] * 4, out_specs=[spec] * 3,
        out_shape=[_sds((R, C), F32)] * 3, compiler_params=_params(("arbitrary",)),
    )(w, g, m, v)


def join_halves(mine, other, core):
    rh, C = mine.shape
    tr = _row_tile(rh, C, 3)
    nb = rh // tr

    def body(c_ref, m_ref, o_ref, out_ref):
        out_ref[...] = jnp.where((pl.program_id(0) // nb) == c_ref[0], m_ref[...], o_ref[...])

    half = pl.BlockSpec((tr, C), lambda i, c: (i % nb, 0))
    return pl.pallas_call(
        body, name="join_halves",
        grid_spec=pltpu.PrefetchScalarGridSpec(num_scalar_prefetch=1, grid=(2 * nb,), in_specs=[half, half],
                                               out_specs=pl.BlockSpec((tr, C), lambda i, c: (i, 0))),
        out_shape=_sds((2 * rh, C), F32), compiler_params=_params(("arbitrary",)),
    )(core, mine, other)


def pair_sum(g, theirs, core):
    n, R, C = g.shape
    rh = R // 2
    tr = _row_tile(rh, C, 3)
    nb = rh // tr

    def body(c_ref, g_ref, t_ref, o_ref):
        o_ref[...] = (g_ref[...] + t_ref[...]).astype(BF16)

    here = pl.BlockSpec((1, tr, C), lambda j, i, c: (j, i, 0))
    return pl.pallas_call(
        body, name="pair_sum",
        grid_spec=pltpu.PrefetchScalarGridSpec(
            num_scalar_prefetch=1, grid=(n, nb),
            in_specs=[pl.BlockSpec((1, tr, C), lambda j, i, c: (j, c[0] * nb + i, 0)), here], out_specs=here),
        out_shape=_sds((n, rh, C), BF16), compiler_params=_params(("arbitrary", "arbitrary")),
    )(core, g, theirs)


def chip_sum(pair, got, chip):
    _, R, C = pair.shape
    tr = _row_tile(R, C, 5)

    def body(c_ref, p_ref, g0_ref, g1_ref, g2_ref, o_ref):
        f = lambda ref: ref[0].astype(F32)
        o_ref[...] = ((f(p_ref) + f(g0_ref)) + f(g1_ref)) + f(g2_ref)

    slot = lambda k: pl.BlockSpec((1, tr, C), lambda i, c: (k, i, 0))
    return pl.pallas_call(
        body, name="chip_sum",
        grid_spec=pltpu.PrefetchScalarGridSpec(
            num_scalar_prefetch=1, grid=(R // tr,),
            in_specs=[pl.BlockSpec((1, tr, C), lambda i, c: (c[0], i, 0)), slot(0), slot(1), slot(2)],
            out_specs=pl.BlockSpec((tr, C), lambda i, c: (i, 0))),
        out_shape=_sds((R, C), F32), compiler_params=_params(("arbitrary",)),
    )(chip, pair, got, got, got)


_HBM = pl.BlockSpec(memory_space=pltpu.HBM)


def _place():
    x, y, c = lax.axis_index("x"), lax.axis_index("y"), lax.axis_index("c")
    return x, y, c, [(1 - x, y), (x, 1 - y), (1 - x, 1 - y)]


def _spans(rows, n):
    assert rows % PACK_ALIGN == 0
    tiles = rows // PACK_ALIGN
    n = min(n, tiles)
    cuts = [tiles * q // n for q in range(n + 1)]
    return [(cuts[q] * PACK_ALIGN, (cuts[q + 1] - cuts[q]) * PACK_ALIGN) for q in range(n)]


def _remote(src, dst, send_sem, recv_sem, to):
    return pltpu.make_async_remote_copy(src_ref=src, dst_ref=dst, send_sem=send_sem, recv_sem=recv_sem,
                                        device_id=to, device_id_type=MESH)


def allgather_shards(wp):
    R, C = wp.shape
    rh = R // 2
    spans = _spans(rh, COPY_CHUNKS)
    n_sp = len(spans)
    local_spans = _spans(R, 2 * COPY_CHUNKS)

    def body(w_ref, out_ref, send_sems, recv_sems, pass_send, pass_recv, local_sems):
        x, y, c, chips = _place()
        me = 2 * x + y
        local = []
        for q, (o, n) in enumerate(local_spans):
            cp = pltpu.make_async_copy(w_ref.at[pl.ds(o, n), :], out_ref.at[me, pl.ds(o, n), :], local_sems.at[q])
            cp.start()
            local.append(cp)
        sends = []
        for k, (cx, cy) in enumerate(chips):
            for q, (o, n) in enumerate(spans):
                rows = pl.ds(c * rh + o, n)
                cp = _remote(w_ref.at[rows, :], out_ref.at[me, rows, :], send_sems.at[k * n_sp + q],
                             recv_sems.at[k * n_sp + q], (cx, cy, c))
                cp.start()
                sends.append(cp)
        for k, (cx, cy) in enumerate(chips):
            for q, (o, n) in enumerate(spans):
                blk = out_ref.at[2 * cx + cy, pl.ds(c * rh + o, n), :]
                _remote(blk, blk, send_sems.at[k * n_sp + q], recv_sems.at[k * n_sp + q], (cx, cy, c)).wait_recv()
                cp = _remote(blk, blk, pass_send.at[k * n_sp + q], pass_recv.at[k * n_sp + q], (x, y, 1 - c))
                cp.start()
                sends.append(cp)
        for k, (cx, cy) in enumerate(chips):
            for q, (o, n) in enumerate(spans):
                blk = out_ref.at[2 * cx + cy, pl.ds((1 - c) * rh + o, n), :]
                _remote(blk, blk, pass_send.at[k * n_sp + q], pass_recv.at[k * n_sp + q], (x, y, 1 - c)).wait_recv()
        for cp in sends:
            cp.wait_send()
        for cp in local:
            cp.wait()

    sems = pltpu.SemaphoreType.DMA((3 * n_sp,))
    return pl.pallas_call(
        body, name="allgather_shards", in_specs=[_HBM], out_specs=_HBM, out_shape=_sds((4, R, C), wp.dtype),
        scratch_shapes=[sems, sems, sems, sems, pltpu.SemaphoreType.DMA((len(local_spans),))],
    )(wp)


def sibling_split(g):
    n_sl, R, C = g.shape
    rh = R // 2
    spans = _spans(rh, COPY_CHUNKS)
    n_sp = len(spans)

    def body(g_ref, got_ref, send_sems, recv_sems):
        x, y, c, _ = _place()
        copies = []
        for j in range(n_sl):
            for q, (o, n) in enumerate(spans):
                cp = _remote(g_ref.at[j, pl.ds((1 - c) * rh + o, n), :], got_ref.at[j, pl.ds(o, n), :],
                             send_sems.at[j * n_sp + q], recv_sems.at[j * n_sp + q], (x, y, 1 - c))
                cp.start()
                copies.append(cp)
        for cp in copies:
            cp.wait()

    sems = pltpu.SemaphoreType.DMA((n_sl * n_sp,))
    return pl.pallas_call(
        body, name="sibling_split", in_specs=[_HBM], out_specs=_HBM, out_shape=_sds((n_sl, rh, C), g.dtype),
        scratch_shapes=[sems, sems],
    )(g)


def chip_exchange(p):
    _, R, C = p.shape
    spans = _spans(R, COPY_CHUNKS)
    n_sp = len(spans)

    def body(p_ref, buf_ref, send_sems, recv_sems):
        x, y, c, chips = _place()
        sends = []
        for k, (cx, cy) in enumerate(chips):
            for q, (o, n) in enumerate(spans):
                cp = _remote(p_ref.at[2 * cx + cy, pl.ds(o, n), :], buf_ref.at[k, pl.ds(o, n), :],
                             send_sems.at[k * n_sp + q], recv_sems.at[k * n_sp + q], (cx, cy, c))
                cp.start()
                sends.append(cp)
        for cp in sends:
            cp.wait()

    sems = pltpu.SemaphoreType.DMA((3 * n_sp,))
    return pl.pallas_call(
        body, name="chip_exchange", in_specs=[_HBM], out_specs=_HBM, out_shape=_sds((3, R, C), p.dtype),
        scratch_shapes=[sems, sems],
    )(p)


def sibling_swap(half):
    R, C = half.shape
    spans = _spans(R, COPY_CHUNKS)

    def body(h_ref, got_ref, send_sems, recv_sems):
        x, y, c, _ = _place()
        copies = []
        for q, (o, n) in enumerate(spans):
            cp = _remote(h_ref.at[pl.ds(o, n), :], got_ref.at[pl.ds(o, n), :], send_sems.at[q], recv_sems.at[q], (x, y, 1 - c))
            cp.start()
            copies.append(cp)
        for cp in copies:
            cp.wait()

    sems = pltpu.SemaphoreType.DMA((len(spans),))
    return pl.pallas_call(
        body, name="sibling_swap", in_specs=[_HBM], out_specs=_HBM, out_shape=_sds((R, C), half.dtype),
        scratch_shapes=[sems, sems],
    )(half)


def allreduce_small(v):
    R, C = v.shape

    def body(v_ref, out_ref, buf, send_sems, recv_sems):
        x, y, c, _ = _place()
        me = 4 * x + 2 * y + c
        buf[me] = v_ref[...]
        flips = [((k >> 2) & 1, (k >> 1) & 1, k & 1) for k in range(1, 8)]
        sends = []
        for k, (fx, fy, fc) in enumerate(flips):
            to = (1 - x if fx else x, 1 - y if fy else y, 1 - c if fc else c)
            cp = _remote(v_ref, buf.at[me], send_sems.at[k], recv_sems.at[k], to)
            cp.start()
            sends.append(cp)
        for k, (fx, fy, fc) in enumerate(flips):
            px, py, pc = (1 - x if fx else x, 1 - y if fy else y, 1 - c if fc else c)
            blk = buf.at[4 * px + 2 * py + pc]
            _remote(blk, blk, send_sems.at[k], recv_sems.at[k], (px, py, pc)).wait_recv()
        for cp in sends:
            cp.wait_send()
        acc = buf[0]
        for s in range(1, 8):
            acc = acc + buf[s]
        out_ref[...] = acc

    vm = pl.BlockSpec(memory_space=pltpu.VMEM)
    return pl.pallas_call(
        body, name="allreduce_small", in_specs=[vm], out_specs=vm, out_shape=_sds((R, C), F32),
        scratch_shapes=[pltpu.VMEM((8, R, C), F32), pltpu.SemaphoreType.DMA((7,)), pltpu.SemaphoreType.DMA((7,))],
        compiler_params=pltpu.CompilerParams(vmem_limit_bytes=VMEM_LIMIT),
    )(v)


def _rows_of(shape):
    return shape[0] * shape[1] // PACK_COLS


def _slot_rows(shape):
    return -(-_rows_of(shape) // PACK_ALIGN) * PACK_ALIGN


TRANSPOSED = ("w1_a", "w3_a", "w1_b", "w3_b")


def _stored(name, shard):
    return shard[0].T if name in TRANSPOSED else shard[0]


def _restored(name, stored):
    return stored.T[None] if name in TRANSPOSED else stored[None]


def _pack_shards(shards, dtype):
    parts = []
    for name, shape, _ in BIG:
        part = _stored(name, shards[name]).reshape(_rows_of(shape), PACK_COLS).astype(dtype)
        parts.append(jnp.pad(part, ((0, _slot_rows(shape) - part.shape[0]), (0, 0))))
    used = sum(p.shape[0] for p in parts)
    parts.append(jnp.zeros((PACK_ROWS - used, PACK_COLS), dtype))
    return jnp.concatenate(parts, axis=0)


def _unpack_gathered(ag):
    out, off = {}, 0
    for name, shape, axis in BIG:
        r = _rows_of(shape)
        piece = ag[:, off:off + r, :]
        off += _slot_rows(shape)
        if name in TRANSPOSED:
            out[name] = piece.reshape(4 * r, PACK_COLS)
        elif axis == 0:
            out[name] = piece.reshape(4 * shape[0], shape[1])
        else:
            out[name] = piece.reshape((4,) + shape).transpose(1, 0, 2).reshape(shape[0], 4 * shape[1])
    return out


def _pack_full_grads(grads):
    parts = []
    for name, shape, axis in BIG:
        g = grads[name]
        if name in TRANSPOSED or axis == 0:
            piece = g.reshape(4, _rows_of(shape), PACK_COLS)
        else:
            piece = g.reshape(shape[0], 4, shape[1]).transpose(1, 0, 2).reshape(4, _rows_of(shape), PACK_COLS)
        parts.append(jnp.pad(piece, ((0, 0), (0, _slot_rows(shape) - piece.shape[1]), (0, 0))))
    used = sum(p.shape[1] for p in parts)
    parts.append(jnp.zeros((4, PACK_ROWS - used, PACK_COLS), F32))
    return jnp.concatenate(parts, axis=1)


def _unpack_shards(packed):
    out, off = {}, 0
    for name, shape, _ in BIG:
        r = _rows_of(shape)
        out[name] = packed[off:off + r] if name in TRANSPOSED else packed[off:off + r].reshape(shape)
        off += _slot_rows(shape)
    return out


def _pack_small(vals, extra=None):
    parts = [vals[name].reshape(-1) for name, _ in SMALL]
    used = sum(p.shape[0] for p in parts)
    if extra is not None:
        parts.append(extra.reshape(1))
        used += 1
    parts.append(jnp.zeros((SMALL_ROWS * 128 - used,), F32))
    return jnp.concatenate(parts).reshape(SMALL_ROWS, 128)


def _unpack_small(packed):
    flat = packed.reshape(-1)
    out, off = {}, 0
    for name, shape in SMALL:
        n = math.prod(shape)
        out[name] = flat[off:off + n].reshape(shape)
        off += n
    return out, flat[off]


def _permute_time(a):
    T, n = a.shape
    return a.reshape(8, T // 8, n).transpose(1, 0, 2).reshape(T, n)


def _unpermute_time(a):
    T, n = a.shape
    return a.reshape(T // 8, 8, n).transpose(1, 0, 2).reshape(T, n)


def _discretize(a_re, a_im, log_dt, b_re, b_im):
    dt = jnp.exp(log_dt)[:, None]
    decay = jnp.exp(dt * a_re)
    abar_r = decay * jnp.cos(dt * a_im)
    abar_i = decay * jnp.sin(dt * a_im)
    nr, ni = abar_r - 1.0, abar_i
    den = a_re * a_re + a_im * a_im
    fr = (nr * a_re + ni * a_im) / den
    fi = (ni * a_re - nr * a_im) / den
    bbar_r = fr[..., None] * b_re - fi[..., None] * b_im
    bbar_i = fr[..., None] * b_im + fi[..., None] * b_re
    return abar_r, abar_i, bbar_r, bbar_i


def _input_matrix(bbar_r, bbar_i):
    eye = jnp.eye(N_GROUPS, dtype=F32)
    blk = lambda b: jnp.einsum("ghp,gk->ghkp", b.transpose(0, 2, 1), eye).reshape(SSM_W, STATE_W)
    return jnp.concatenate([blk(bbar_r), blk(bbar_i)], axis=1)


def _output_matrix(c_re, c_im):
    eye = jnp.eye(N_GROUPS, dtype=F32)
    blk = lambda cm: jnp.einsum("ghp,gk->gpkh", cm, eye).reshape(STATE_W, SSM_W)
    return jnp.concatenate([blk(c_re), -blk(c_im)], axis=0)


def _state_power(ar, ai, n):
    steps = int(round(math.log2(n)))
    assert 1 << steps == n
    for _ in range(steps):
        ar, ai = ar * ar - ai * ai, 2.0 * ar * ai
    return ar, ai


def kernel(x, p, g_ffn1, w1_a, w3_a, w2_a, g_mix, w_in, b_f, a_re, a_im, log_dt, b_re, b_im, c_re, c_im, d_skip, w_glu, b_glu, g_attn_out, g_ssm_out, w_out, g_ffn2, w1_b, w3_b, w2_b, g_ple, w_ple_gate, w_ple_proj, g_final, loss_target, m_g_ffn1, m_w1_a, m_w3_a, m_w2_a, m_g_mix, m_w_in, m_b_f, m_a_re, m_a_im, m_log_dt, m_b_re, m_b_im, m_c_re, m_c_im, m_d_skip, m_w_glu, m_b_glu, m_g_attn_out, m_g_ssm_out, m_w_out, m_g_ffn2, m_w1_b, m_w3_b, m_w2_b, m_g_ple, m_w_ple_gate, m_w_ple_proj, m_g_final, v_g_ffn1, v_w1_a, v_w3_a, v_w2_a, v_g_mix, v_w_in, v_b_f, v_a_re, v_a_im, v_log_dt, v_b_re, v_b_im, v_c_re, v_c_im, v_d_skip, v_w_glu, v_b_glu, v_g_attn_out, v_g_ssm_out, v_w_out, v_g_ffn2, v_w1_b, v_w3_b, v_w2_b, v_g_ple, v_w_ple_gate, v_w_ple_proj, v_g_final):
    args = dict(locals())
    weights = {n: args[n] for n in WEIGHT_ORDER}
    moms = {n: args["m_" + n] for n in WEIGHT_ORDER}
    vars_ = {n: args["v_" + n] for n in WEIGHT_ORDER}
    T = x.shape[1]
    x2, p2, tgt = x[0], p[0, 0], loss_target[0]

    full = _unpack_gathered(allgather_shards(_pack_shards(weights, BF16)))
    loss_part, dx, grads = _local_step(x2, p2, tgt, {n: weights[n] for n, _ in SMALL}, full)
    return _reduce_and_update(weights, moms, vars_, loss_part, dx, grads)


def _local_step(x2, p2, tgt, sm, full):
    T = x2.shape[0]
    (g_ffn1, g_mix, b_f, a_re, a_im, log_dt, b_re, b_im, c_re, c_im, d_skip, b_glu, g_attn_out, g_ssm_out, g_ffn2, g_ple,
     g_final) = (sm[n] for n, _ in SMALL)
    w_in_f = full["w_in"]
    w_in_r = jnp.concatenate([w_in_f[:, :ATTN_W] * QK_SCALE, w_in_f[:, ATTN_W:3 * ATTN_W], w_in_f[:, 3 * ATTN_W + N_HEADS:],
                              w_in_f[:, 3 * ATTN_W:3 * ATTN_W + N_HEADS], jnp.zeros((D_MODEL, 128 - N_HEADS), BF16)], axis=1)
    b_f_pad = jnp.pad(b_f, ((0, 0), (0, 128 - N_HEADS)))

    disc_in = (a_re[0], a_im[0], log_dt[0], b_re[0], b_im[0])
    (abar_r, abar_i, bbar_r, bbar_i), disc_vjp = jax.vjp(_discretize, *disc_in)
    wb = _input_matrix(bbar_r, bbar_i)
    cbd = _output_matrix(c_re[0], c_im[0])
    ar, ai = abar_r.reshape(1, STATE_W), abar_i.reshape(1, STATE_W)
    alr, ali = _state_power(ar, ai, T // 8)
    dvec = d_skip.reshape(1, SSM_W)
    wb16, cbd16 = wb.astype(BF16), cbd.astype(BF16)

    h1, a1a, a3a, n1 = ffn_fwd(x2, g_ffn1, full["w1_a"], full["w3_a"], full["w2_a"], "ffn_a_fwd")
    u, qkv, s_in, fz, cum = mixin_fwd(h1, g_mix, w_in_r, b_f_pad)
    q_aug, k_aug, v_aug = heads_in(qkv, cum)
    o_heads, q_bwd = attn_fwd(q_aug, k_aug, v_aug)
    s_perm = _permute_time(s_in)
    y_perm, xs = ssm_fwd(s_perm, wb16, cbd16, ar, ai, alr, ali, dvec)
    ypre = _unpermute_time(y_perm)
    h2, mixed = mixout_fwd(h1, o_heads, ypre, g_attn_out, g_ssm_out, full["w_glu"], b_glu, full["w_out"])
    h3, a1b, a3b, n2 = ffn_fwd(h2, g_ffn2, full["w1_b"], full["w3_b"], full["w2_b"], "ffn_b_fwd")

    dh3, n3, dzg, dpp, loss_part, dg_ple, dg_final = head_fwd_bwd(
        h3, p2, tgt, g_ple, g_final.reshape(1, D_MODEL), full["w_ple_gate"], full["w_ple_proj"])
    grads = {"g_ple": dg_ple, "g_final": dg_final.reshape(D_MODEL)}
    grads["w_ple_gate"] = mm_tn(n3, dzg, "dw_ple_gate")
    grads["w_ple_proj"] = mm_tn(p2, dpp, "dw_ple_proj")

    dh2, da1, da3, act, grads["g_ffn2"] = ffn_bwd(h2, g_ffn2, dh3, a1b, a3b, full["w1_b"], full["w3_b"], full["w2_b"], "ffn_b_bwd")
    grads["w1_b"] = mm_tn(da1, n2, "dw1_b")
    grads["w3_b"] = mm_tn(da3, n2, "dw3_b")
    grads["w2_b"] = mm_tn(act, dh3, "dw2_b", scale=0.5)

    seg = (jnp.arange(ATTN_W)[:, None] // HEAD_DIM == jnp.arange(128)[None, :]).astype(F32)
    do_aug, dypre, dpre, yg, grads["g_attn_out"], grads["g_ssm_out"], grads["b_glu"] = mixout_bwd(
        dh2, o_heads, ypre, g_attn_out, g_ssm_out, full["w_glu"], b_glu, full["w_out"], seg)
    grads["w_out"] = mm_tn(mixed, dh2, "dw_out")
    grads["w_glu"] = mm_tn(yg, dpre, "dw_glu")

    dq_aug, dk_aug, dv_aug, dc_rows = attn_bwd(q_bwd, k_aug, v_aug, do_aug)
    dc = jnp.pad(dc_rows.reshape(N_HEADS, T).T, ((0, 0), (0, 128 - N_HEADS)))

    dy_perm = _permute_time(dypre)
    du_perm, gs, d_a, dd = ssm_bwd(dy_perm, s_perm, xs, cbd16.T, wb16.T, ar, ai, alr, ali, dvec)
    ds_in = _unpermute_time(du_perm)
    hg = N_GROUPS // 2
    d_in, d_out = [], []
    for part in range(2):
        ins, outs = [], []
        for half in range(2):
            states = (part * STATE_W + half * _HALF_ST, _HALF_ST)
            chans = (half * _HALF_CH, _HALF_CH)
            blk = mm_tn(s_perm, gs, f"dw_ssm_in_{part}{half}", a_cols=chans, b_cols=states)
            ins.append(jnp.einsum("ghgp->ghp", blk.reshape(hg, GROUP_CH, hg, N_STATE)))
            blk = mm_tn(xs, dy_perm, f"dw_ssm_out_{part}{half}", a_cols=states, b_cols=chans)
            outs.append(jnp.einsum("gpgh->gph", blk.reshape(hg, N_STATE, hg, GROUP_CH)))
        d_in.append(jnp.concatenate(ins, axis=0).transpose(0, 2, 1))
        d_out.append(jnp.concatenate(outs, axis=0).transpose(0, 2, 1))
    d_abar_r = jnp.sum(d_a[:, :STATE_W], axis=0).reshape(N_GROUPS, N_STATE)
    d_abar_i = jnp.sum(d_a[:, STATE_W:], axis=0).reshape(N_GROUPS, N_STATE)
    d_disc = disc_vjp((d_abar_r, d_abar_i, d_in[0], d_in[1]))
    for name, val in zip(("a_re", "a_im", "log_dt", "b_re", "b_im"), d_disc):
        grads[name] = val[None]
    grads["c_re"] = d_out[0][None]
    grads["c_im"] = -d_out[1][None]
    grads["d_skip"] = dd.reshape(1, N_GROUPS, GROUP_CH)

    dh1, dz, grads["g_mix"], dbf = mixin_bwd(dh2, h1, g_mix, w_in_r, dq_aug, dk_aug, dv_aug, ds_in, dc, fz)
    grads["b_f"] = dbf[:, :N_HEADS]
    d_w_in_r = mm_tn(u, dz, "dw_in")
    grads["w_in"] = jnp.concatenate([d_w_in_r[:, :ATTN_W] * QK_SCALE, d_w_in_r[:, ATTN_W:3 * ATTN_W],
                                     d_w_in_r[:, 3 * ATTN_W + SSM_W:3 * ATTN_W + SSM_W + N_HEADS],
                                     d_w_in_r[:, 3 * ATTN_W:3 * ATTN_W + SSM_W]], axis=1)

    dx, da1, da3, act, grads["g_ffn1"] = ffn_bwd(x2, g_ffn1, dh1, a1a, a3a, full["w1_a"], full["w3_a"], full["w2_a"], "ffn_a_bwd")
    grads["w1_a"] = mm_tn(da1, n1, "dw1_a")
    grads["w3_a"] = mm_tn(da3, n1, "dw3_a")
    grads["w2_a"] = mm_tn(act, dh1, "dw2_a", scale=0.5)
    return loss_part, dx, grads


def _reduce_and_update(weights, moms, vars_, loss_part, dx, grads):
    core = lax.axis_index("c").astype(jnp.int32).reshape(1)
    chip = (2 * lax.axis_index("x") + lax.axis_index("y")).astype(jnp.int32).reshape(1)
    packed = _pack_full_grads(grads)
    pair = pair_sum(packed, sibling_split(packed), core)
    half = chip_sum(pair, chip_exchange(pair), chip)
    g_stored = _unpack_shards(join_halves(half, sibling_swap(half), core))
    g_out, d_out, m_out, v_out = {}, {}, {}, {}
    for n, _, _ in BIG:
        d, m, v = adamw(_stored(n, weights[n]), g_stored[n], _stored(n, moms[n]), _stored(n, vars_[n]), "adamw_" + n)
        g_out[n], d_out[n], m_out[n], v_out[n] = (_restored(n, a) for a in (g_stored[n], d, m, v))

    small = allreduce_small(_pack_small({n: grads[n] for n, _ in SMALL}, extra=loss_part[0, 0]))
    d_small, m_small, v_small = adamw(_pack_small(weights), small, _pack_small(moms), _pack_small(vars_), "adamw_small")

    g_small, loss = _unpack_small(small)
    g_out.update(g_small)
    outs = []
    for big, sm in ((d_out, d_small), (m_out, m_small), (v_out, v_small)):
        o, _ = _unpack_small(sm)
        o.update(big)
        outs.append(o)
    result = [loss, dx[None]] + [g_out[n] for n in WEIGHT_ORDER]
    for o in outs:
        result += [o[n] for n in WEIGHT_ORDER]
    return tuple(result)
```

```python
import functools
import math

import jax
import jax.numpy as jnp
from jax import lax
from jax.experimental import pallas as pl
from jax.experimental.pallas import tpu as pltpu

F32 = jnp.float32
BF16 = jnp.bfloat16

D_MODEL = 1024
D_FF = 2816
N_HEADS = 8
HEAD_DIM = 64
ATTN_W = 512
SSM_W = 512
N_GROUPS = 32
N_STATE = 64
GROUP_CH = 16
STATE_W = N_GROUPS * N_STATE
Z_COLS = 2176
QK_SCALE = 0.125
EPS = 1e-6

ADAM_LR = 0.001
ADAM_B1 = 0.9
ADAM_B2 = 0.999
ADAM_EPS = 1e-08
ADAM_WD = 0.01
ADAM_STEP = 10

TOKEN_TILE = 512
FFN_TOKEN_TILE = 256
FF_CHUNK = 1408
MM_K_TILE = 2048
ATTN_TILE = 512
SCAN_STEPS = 32
SCAN_LANES = 512
VMEM_LIMIT = 48 * 1024 * 1024
FFN_VMEM_LIMIT = 56 * 1024 * 1024
COPY_CHUNKS = 4

NT_DIMS = (((1,), (1,)), ((), ()))
TN_DIMS = (((0,), (0,)), ((), ()))
HIGHEST = lax.Precision.HIGHEST
MESH = pl.DeviceIdType.MESH

BIG = (
    ("w1_a", (1024, 704), 1), ("w3_a", (1024, 704), 1), ("w2_a", (704, 1024), 0),
    ("w_in", (1024, 514), 1), ("w_glu", (128, 512), 0), ("w_out", (256, 1024), 0),
    ("w1_b", (1024, 704), 1), ("w3_b", (1024, 704), 1), ("w2_b", (704, 1024), 0),
    ("w_ple_gate", (256, 1024), 0), ("w_ple_proj", (256, 256), 1),
)
PACK_COLS = 1024
PACK_ALIGN = 16
PACK_ROWS = 5408
SMALL = (
    ("g_ffn1", (1, 1024)), ("g_mix", (1, 1024)), ("b_f", (1, 8)), ("a_re", (1, 32, 64)), ("a_im", (1, 32, 64)),
    ("log_dt", (1, 32)), ("b_re", (1, 32, 64, 16)), ("b_im", (1, 32, 64, 16)), ("c_re", (1, 32, 16, 64)),
    ("c_im", (1, 32, 16, 64)), ("d_skip", (1, 32, 16)), ("b_glu", (1, 512)), ("g_attn_out", (1, 512)),
    ("g_ssm_out", (1, 512)), ("g_ffn2", (1, 1024)), ("g_ple", (1, 1024)), ("g_final", (1024,)),
)
SMALL_ROWS = 1120
WEIGHT_ORDER = ("g_ffn1", "w1_a", "w3_a", "w2_a", "g_mix", "w_in", "b_f", "a_re", "a_im", "log_dt", "b_re", "b_im",
                "c_re", "c_im", "d_skip", "w_glu", "b_glu", "g_attn_out", "g_ssm_out", "w_out", "g_ffn2", "w1_b",
                "w3_b", "w2_b", "g_ple", "w_ple_gate", "w_ple_proj", "g_final")


def _params(sem=None, vmem=VMEM_LIMIT):
    kw = dict(vmem_limit_bytes=vmem)
    if sem is not None:
        kw["dimension_semantics"] = sem
    return pltpu.CompilerParams(**kw)


def _sds(shape, dtype):
    return jax.ShapeDtypeStruct(shape, dtype)


def _tile(n, pref):
    t = min(n, pref)
    assert n % t == 0, (n, pref)
    return t


def _rms_scale(x):
    return lax.rsqrt(jnp.mean(x * x, axis=-1, keepdims=True) + EPS)


def _rms_bwd(dy, x, g):
    r = _rms_scale(x)
    xh = x * r
    dxh = dy * g
    dx = r * (dxh - xh * jnp.mean(dxh * xh, axis=-1, keepdims=True))
    return dx, jnp.sum(dy * xh, axis=0, keepdims=True)


def _dot(a, b):
    return jnp.dot(a, b, preferred_element_type=F32)


def _dot_nt(a, b):
    return lax.dot_general(a, b, NT_DIMS, preferred_element_type=F32)


def _dot_tn(a, b):
    return lax.dot_general(a, b, TN_DIMS, preferred_element_type=F32)


_GELU_C = math.sqrt(2.0 / math.pi)


def _gelu_parts(x):
    t = jnp.tanh(_GELU_C * (x + 0.044715 * x * x * x))
    return 0.5 * x * (1.0 + t), t


def _gelu_grad(x, t):
    return 0.5 * (1.0 + t) + 0.5 * x * (1.0 - t * t) * _GELU_C * (1.0 + 3.0 * 0.044715 * x * x)


def _resident(shape):
    return pl.BlockSpec(shape, lambda i: (0,) * len(shape), pipeline_mode=pl.Buffered(1))


def ffn_fwd(h, g, w1, w3, w2, name):
    T = h.shape[0]
    tm = _tile(T, FFN_TOKEN_TILE)

    def body(h_ref, g_ref, w1_ref, w3_ref, w2_ref, ho_ref, a1_ref, a3_ref, n_ref):
        x = h_ref[...]
        n = (x * _rms_scale(x) * g_ref[...]).astype(BF16)
        n_ref[...] = n
        out = x
        for lo in range(0, D_FF, FF_CHUNK):
            cols = slice(lo, lo + FF_CHUNK)
            a1 = _dot_nt(n, w1_ref[cols, :])
            a3 = _dot_nt(n, w3_ref[cols, :])
            a1_ref[:, cols] = a1.astype(BF16)
            a3_ref[:, cols] = a3.astype(BF16)
            act = (a1 * jax.nn.sigmoid(a1) * a3).astype(BF16)
            out = out + 0.5 * _dot(act, w2_ref[cols, :])
        ho_ref[...] = out

    tok = lambda i: (i, 0)
    return pl.pallas_call(
        body, name=name, grid=(T // tm,),
        in_specs=[pl.BlockSpec((tm, D_MODEL), tok), _resident((1, D_MODEL)), _resident((D_FF, D_MODEL)),
                  _resident((D_FF, D_MODEL)), _resident((D_FF, D_MODEL))],
        out_specs=[pl.BlockSpec((tm, D_MODEL), tok), pl.BlockSpec((tm, D_FF), tok), pl.BlockSpec((tm, D_FF), tok),
                   pl.BlockSpec((tm, D_MODEL), tok)],
        out_shape=[_sds((T, D_MODEL), F32), _sds((T, D_FF), BF16), _sds((T, D_FF), BF16), _sds((T, D_MODEL), BF16)],
        compiler_params=_params(("arbitrary",), FFN_VMEM_LIMIT),
    )(h, g, w1, w3, w2)


def ffn_bwd(h, g, dho, a1, a3, w1, w3, w2, name):
    T = h.shape[0]
    tm = _tile(T, FFN_TOKEN_TILE)

    def body(h_ref, g_ref, dho_ref, a1_ref, a3_ref, w1_ref, w3_ref, w2_ref, dhi_ref, da1_ref, da3_ref, act_ref, dg_ref):
        @pl.when(pl.program_id(0) == 0)
        def _():
            dg_ref[...] = jnp.zeros_like(dg_ref)

        dho = dho_ref[...]
        dhb = (0.5 * dho).astype(BF16)
        dn = None
        for lo in range(0, D_FF, FF_CHUNK):
            cols = slice(lo, lo + FF_CHUNK)
            a1v = a1_ref[:, cols].astype(F32)
            a3v = a3_ref[:, cols].astype(F32)
            s = jax.nn.sigmoid(a1v)
            sl = a1v * s
            dact = _dot_nt(dhb, w2_ref[cols, :])
            act_ref[:, cols] = (sl * a3v).astype(BF16)
            da1 = (dact * a3v * s * (1.0 + a1v * (1.0 - s))).astype(BF16)
            da3 = (dact * sl).astype(BF16)
            da1_ref[:, cols] = da1
            da3_ref[:, cols] = da3
            part = _dot(da1, w1_ref[cols, :]) + _dot(da3, w3_ref[cols, :])
            dn = part if dn is None else dn + part
        dx, dg = _rms_bwd(dn, h_ref[...], g_ref[...])
        dg_ref[...] += dg
        dhi_ref[...] = dho + dx

    tok = lambda i: (i, 0)
    return pl.pallas_call(
        body, name=name, grid=(T // tm,),
        in_specs=[pl.BlockSpec((tm, D_MODEL), tok), _resident((1, D_MODEL)), pl.BlockSpec((tm, D_MODEL), tok),
                  pl.BlockSpec((tm, D_FF), tok), pl.BlockSpec((tm, D_FF), tok), _resident((D_FF, D_MODEL)),
                  _resident((D_FF, D_MODEL)), _resident((D_FF, D_MODEL))],
        out_specs=[pl.BlockSpec((tm, D_MODEL), tok), pl.BlockSpec((tm, D_FF), tok), pl.BlockSpec((tm, D_FF), tok),
                   pl.BlockSpec((tm, D_FF), tok), pl.BlockSpec((1, D_MODEL), lambda i: (0, 0))],
        out_shape=[_sds((T, D_MODEL), F32), _sds((T, D_FF), BF16), _sds((T, D_FF), BF16), _sds((T, D_FF), BF16),
                   _sds((1, D_MODEL), F32)],
        compiler_params=_params(("arbitrary",), FFN_VMEM_LIMIT),
    )(h, g, dho, a1, a3, w1, w3, w2)


def mm_tn(a, b, name, scale=1.0, a_cols=None, b_cols=None):
    T = a.shape[0]
    a_off, M = a_cols or (0, a.shape[1])
    b_off, N = b_cols or (0, b.shape[1])
    bm = 512 if M % 512 == 0 else (1408 if M == 2816 else 256)
    bn = N if N in (2176, 1408) else (1408 if N == 2816 else min(N, 1024))
    tk = _tile(T, MM_K_TILE)
    row_bytes = 2 * (bm * a.dtype.itemsize + bn * b.dtype.itemsize)
    while tk > TOKEN_TILE and tk * row_bytes > VMEM_LIMIT // 3:
        tk //= 2
    assert M % bm == 0 and N % bn == 0 and T % tk == 0 and a_off % bm == 0 and b_off % bn == 0
    n_k = T // tk
    m0, n0 = a_off // bm, b_off // bn

    def body(a_ref, b_ref, o_ref):
        k = pl.program_id(2)

        @pl.when(k == 0)
        def _():
            o_ref[...] = jnp.zeros_like(o_ref)

        o_ref[...] += _dot_tn(a_ref[...].astype(BF16), b_ref[...].astype(BF16))

        if scale != 1.0:
            @pl.when(k == n_k - 1)
            def _():
                o_ref[...] = o_ref[...] * scale

    return pl.pallas_call(
        body, name=name, grid=(M // bm, N // bn, n_k),
        in_specs=[pl.BlockSpec((tk, bm), lambda m, n, k: (k, m0 + m)), pl.BlockSpec((tk, bn), lambda m, n, k: (k, n0 + n))],
        out_specs=pl.BlockSpec((bm, bn), lambda m, n, k: (m, n)),
        out_shape=_sds((M, N), F32),
        compiler_params=_params(("arbitrary", "arbitrary", "arbitrary")),
    )(a, b)


def mixin_fwd(h1, g, w_in_r, b_f_pad):
    T = h1.shape[0]
    tm = _tile(T, TOKEN_TILE)

    def body(h_ref, g_ref, w_ref, bf_ref, u_ref, qkv_ref, s_ref, fz_ref, c_ref, carry):
        @pl.when(pl.program_id(0) == 0)
        def _():
            carry[...] = jnp.zeros_like(carry)

        x = h_ref[...]
        u = (x * _rms_scale(x) * g_ref[...]).astype(BF16)
        u_ref[...] = u
        z = _dot(u, w_ref[...])
        qkv_ref[...] = z[:, :3 * ATTN_W].astype(BF16)
        s_ref[...] = z[:, 3 * ATTN_W:3 * ATTN_W + SSM_W]
        fz = z[:, 3 * ATTN_W + SSM_W:] + bf_ref[...]
        fz_ref[...] = fz
        lane = lax.broadcasted_iota(jnp.int32, fz.shape, 1)
        logf = jnp.where(lane < N_HEADS, jnp.minimum(fz, 0.0) - jnp.log(1.0 + jnp.exp(-jnp.abs(fz))), 0.0)
        row = lax.broadcasted_iota(jnp.int32, (tm, tm), 0)
        col = lax.broadcasted_iota(jnp.int32, (tm, tm), 1)
        tri = (col <= row).astype(F32)
        cs = jnp.dot(tri, logf, precision=HIGHEST, preferred_element_type=F32) + carry[0:1, :]
        c_ref[...] = cs
        carry[...] = jnp.broadcast_to(cs[tm - 1:tm, :], carry.shape)

    tok = lambda i: (i, 0)
    fix = lambda i: (0, 0)
    return pl.pallas_call(
        body, name="mixin_fwd", grid=(T // tm,),
        in_specs=[pl.BlockSpec((tm, D_MODEL), tok), pl.BlockSpec((1, D_MODEL), fix),
                  pl.BlockSpec((D_MODEL, Z_COLS), fix), pl.BlockSpec((1, 128), fix)],
        out_specs=[pl.BlockSpec((tm, D_MODEL), tok), pl.BlockSpec((tm, 3 * ATTN_W), tok), pl.BlockSpec((tm, SSM_W), tok),
                   pl.BlockSpec((tm, 128), tok), pl.BlockSpec((tm, 128), tok)],
        out_shape=[_sds((T, D_MODEL), BF16), _sds((T, 3 * ATTN_W), BF16), _sds((T, SSM_W), F32),
                   _sds((T, 128), F32), _sds((T, 128), F32)],
        scratch_shapes=[pltpu.VMEM((8, 128), F32)],
        compiler_params=_params(("arbitrary",)),
    )(h1, g, w_in_r, b_f_pad)


def mixin_bwd(dh2, h1, g, w_in_r, dq, dk, dv, ds, dc, fz):
    T = h1.shape[0]
    tm = _tile(T, TOKEN_TILE)
    n_t = T // tm

    def body(dh2_ref, h_ref, g_ref, w_ref, dq_ref, dk_ref, dv_ref, ds_ref, dc_ref, fz_ref,
             dh1_ref, dz_ref, dg_ref, dbf_ref, carry):
        @pl.when(pl.program_id(0) == 0)
        def _():
            carry[...] = jnp.zeros_like(carry)
            dg_ref[...] = jnp.zeros_like(dg_ref)
            dbf_ref[...] = jnp.zeros_like(dbf_ref)

        row = lax.broadcasted_iota(jnp.int32, (tm, tm), 0)
        col = lax.broadcasted_iota(jnp.int32, (tm, tm), 1)
        tri = (col >= row).astype(F32)
        dlogf = jnp.dot(tri, dc_ref[...], precision=HIGHEST, preferred_element_type=F32) + carry[0:1, :]
        carry[...] = jnp.broadcast_to(dlogf[0:1, :], carry.shape)
        dfz = dlogf * jax.nn.sigmoid(-fz_ref[...])
        dbf_ref[...] += jnp.sum(dfz, axis=0, keepdims=True)
        dz = jnp.concatenate([_join_heads(dq_ref, BF16), _join_heads(dk_ref, BF16), _join_heads(dv_ref, BF16),
                              ds_ref[...], dfz], axis=1).astype(BF16)
        dz_ref[...] = dz
        du = _dot_nt(dz, w_ref[...])
        dx, dg = _rms_bwd(du, h_ref[...], g_ref[...])
        dg_ref[...] += dg
        dh1_ref[...] = dh2_ref[...] + dx

    tok = lambda i: (n_t - 1 - i, 0)
    fix = lambda i: (0, 0)
    heads = pl.BlockSpec((N_HEADS, tm, 128), lambda i: (0, n_t - 1 - i, 0))
    return pl.pallas_call(
        body, name="mixin_bwd", grid=(n_t,),
        in_specs=[pl.BlockSpec((tm, D_MODEL), tok), pl.BlockSpec((tm, D_MODEL), tok), pl.BlockSpec((1, D_MODEL), fix),
                  pl.BlockSpec((D_MODEL, Z_COLS), fix), heads, heads, heads, pl.BlockSpec((tm, SSM_W), tok),
                  pl.BlockSpec((tm, 128), tok), pl.BlockSpec((tm, 128), tok)],
        out_specs=[pl.BlockSpec((tm, D_MODEL), tok), pl.BlockSpec((tm, Z_COLS), tok), pl.BlockSpec((1, D_MODEL), fix),
                   pl.BlockSpec((1, 128), fix)],
        out_shape=[_sds((T, D_MODEL), F32), _sds((T, Z_COLS), BF16), _sds((1, D_MODEL), F32), _sds((1, 128), F32)],
        scratch_shapes=[pltpu.VMEM((8, 128), F32)],
        compiler_params=_params(("arbitrary",)),
    )(dh2, h1, g, w_in_r, dq, dk, dv, ds, dc, fz)


def _lane_move(src_lo, dst_lo, width, dtype):
    r = lax.broadcasted_iota(jnp.int32, (128, 128), 0)
    c = lax.broadcasted_iota(jnp.int32, (128, 128), 1)
    return ((c - dst_lo == r - src_lo) & (r >= src_lo) & (r < src_lo + width)).astype(dtype)


def _lane_const(lo, width, value):
    lane = lax.broadcasted_iota(jnp.int32, (1, 128), 1)
    return jnp.where((lane >= lo) & (lane < lo + width), value, 0.0).astype(F32)


def _pieces(a):
    hi = a.astype(BF16)
    rest = a - hi.astype(F32)
    mid = rest.astype(BF16)
    return hi, mid, (rest - mid.astype(F32)).astype(BF16)


def _head_features(pair_block, e):
    return _dot(pair_block, _lane_move(HEAD_DIM * e, 0, HEAD_DIM, BF16))


def _helper_columns(pieces, head, sign):
    out = None
    for k, piece in enumerate(pieces):
        term = _dot(piece, _lane_move(head, HEAD_DIM + k, 1, BF16))
        out = term if out is None else out + term
    return sign * out


def heads_in(qkv, cum):
    T = qkv.shape[0]
    tm = _tile(T, TOKEN_TILE)

    def body(qkv_ref, c_ref, q_ref, k_ref, v_ref):
        c = _pieces(c_ref[...])
        for h in range(N_HEADS):
            p, e = divmod(h, 2)
            blk = lambda base: qkv_ref[:, base + 128 * p:base + 128 * (p + 1)]
            q_ref[h] = (_head_features(blk(0), e) + _lane_const(HEAD_DIM, 3, -1.0)).astype(BF16)
            k_ref[h] = (_head_features(blk(ATTN_W), e) + _helper_columns(c, h, 1.0)
                        + _lane_const(HEAD_DIM + 3, 3, 1.0)).astype(BF16)
            v_ref[h] = (_head_features(blk(2 * ATTN_W), e) + _lane_const(HEAD_DIM, 3, 1.0)).astype(BF16)

    tok = lambda i: (i, 0)
    heads = pl.BlockSpec((N_HEADS, tm, 128), lambda i: (0, i, 0))
    return pl.pallas_call(
        body, name="heads_in", grid=(T // tm,),
        in_specs=[pl.BlockSpec((tm, 3 * ATTN_W), tok), pl.BlockSpec((tm, 128), tok)],
        out_specs=[heads] * 3, out_shape=[_sds((N_HEADS, T, 128), BF16)] * 3,
        compiler_params=_params(("arbitrary",)),
    )(qkv, cum)


def attn_fwd(q_aug, k_aug, v_aug):
    H, T, wd = q_aug.shape
    hd = HEAD_DIM
    tq = _tile(T, ATTN_TILE)
    n = T // tq

    def body(q_ref, k_ref, v_ref, o_ref, qb_ref, m_sc, acc, s_even, s_odd):
        qi = pl.program_id(1)
        qv = q_ref[0]
        m_sc[...] = jnp.full_like(m_sc, -jnp.inf)
        acc[...] = jnp.zeros_like(acc)

        def key_rows(j):
            return pl.ds(pl.multiple_of(jnp.minimum(j, qi) * tq, tq), tq)

        def logits(j, buf):
            buf[...] = _dot_nt(k_ref[0, key_rows(j), :], qv)

        def update(j, buf, masked):
            st = buf[...]
            if masked:
                keep = lax.broadcasted_iota(jnp.int32, (tq, tq), 0) <= lax.broadcasted_iota(jnp.int32, (tq, tq), 1)
                st = jnp.where(keep, st, -1e30)
            m_old = m_sc[...]
            m_new = jnp.maximum(m_old, jnp.max(st, axis=0, keepdims=True))
            pt = jnp.exp(st - m_new).astype(BF16)
            acc[...] = jnp.exp(m_old - m_new) * acc[...] + _dot_tn(v_ref[0, key_rows(j), :], pt)
            m_sc[...] = m_new

        logits(0, s_even)

        def two_tiles(p, carry):
            j = 2 * p
            logits(j + 1, s_odd)
            update(j, s_even, False)
            logits(j + 2, s_even)
            update(j + 1, s_odd, False)
            return carry

        lax.fori_loop(0, qi // 2, two_tiles, 0)

        @pl.when(qi % 2 == 0)
        def _():
            update(qi, s_even, True)

        @pl.when(qi % 2 == 1)
        def _():
            logits(qi, s_odd)
            update(qi - 1, s_even, False)
            update(qi, s_odd, True)

        total = acc[hd:hd + 1, :]
        o_ref[0] = (acc[...] / total).T
        hi, mid, lo = (t.astype(F32) for t in _pieces(-(m_sc[...] + jnp.log(total))))
        row = lax.broadcasted_iota(jnp.int32, (wd, tq), 0)
        lse_rows = jnp.where(row == hd + 3, hi, jnp.where(row == hd + 4, mid, jnp.where(row == hd + 5, lo, 0.0)))
        qb_ref[0] = (qv.astype(F32) + lse_rows.T).astype(BF16)

    qmap = lambda h, i: (h, i, 0)
    head = lambda h, i: (h, 0, 0)
    return pl.pallas_call(
        body, name="attn_fwd", grid=(H, n),
        in_specs=[pl.BlockSpec((1, tq, wd), qmap), pl.BlockSpec((1, T, wd), head), pl.BlockSpec((1, T, wd), head)],
        out_specs=[pl.BlockSpec((1, tq, wd), qmap), pl.BlockSpec((1, tq, wd), qmap)],
        out_shape=[_sds((H, T, wd), F32), _sds((H, T, wd), BF16)],
        scratch_shapes=[pltpu.VMEM((1, tq), F32), pltpu.VMEM((wd, tq), F32), pltpu.VMEM((tq, tq), F32),
                        pltpu.VMEM((tq, tq), F32)],
        compiler_params=_params(("arbitrary", "arbitrary")),
    )(q_aug, k_aug, v_aug)


def attn_bwd(q_aug, k_aug, v_aug, do_aug):
    H, T, wd = q_aug.shape
    tq = _tile(T, ATTN_TILE)
    n = T // tq

    def body(q_ref, do_ref, k_ref, v_ref, dq_ref, dk_ref, dv_ref, dc_ref, dck):
        j = pl.program_id(1)

        @pl.when(j == 0)
        def _():
            dq_ref[...] = jnp.zeros_like(dq_ref)
            dc_ref[...] = jnp.zeros_like(dc_ref)

        dk_ref[...] = jnp.zeros_like(dk_ref)
        dv_ref[...] = jnp.zeros_like(dv_ref)
        dck[...] = jnp.zeros_like(dck)
        kv, vv = k_ref[0], v_ref[0]

        def tile(i, masked):
            rows = pl.ds(pl.multiple_of(i * tq, tq), tq)
            qv, dov = q_ref[0, rows, :], do_ref[0, rows, :]
            pt = jnp.exp(_dot_nt(kv, qv))
            if masked:
                keep = lax.broadcasted_iota(jnp.int32, (tq, tq), 0) <= lax.broadcasted_iota(jnp.int32, (tq, tq), 1)
                pt = jnp.where(keep, pt, 0.0)
            dv_ref[0] += _dot(pt.astype(BF16), dov)
            dst = pt * _dot_nt(vv, dov)
            dsb = dst.astype(BF16)
            dk_ref[0] += _dot(dsb, qv)
            dq_ref[0, rows, :] += _dot_tn(dsb, kv)
            dck[...] += jnp.sum(dst, axis=1, keepdims=True)
            dc_ref[0, pl.ds(i, 1), :] += jnp.sum(dst, axis=0, keepdims=True)

        def off_diagonal(i, carry):
            tile(i, False)
            return carry

        tile(j, True)
        lax.fori_loop(j + 1, n, off_diagonal, 0)
        dc_ref[0, pl.ds(j, 1), :] -= jnp.broadcast_to(dck[...], (tq, 128)).T[0:1, :]

    head = lambda h, j: (h, 0, 0)
    kmap = lambda h, j: (h, j, 0)
    return pl.pallas_call(
        body, name="attn_bwd", grid=(H, n),
        in_specs=[pl.BlockSpec((1, T, wd), head), pl.BlockSpec((1, T, wd), head), pl.BlockSpec((1, tq, wd), kmap),
                  pl.BlockSpec((1, tq, wd), kmap)],
        out_specs=[pl.BlockSpec((1, T, wd), head), pl.BlockSpec((1, tq, wd), kmap), pl.BlockSpec((1, tq, wd), kmap),
                   pl.BlockSpec((1, n, tq), head)],
        out_shape=[_sds((H, T, wd), F32), _sds((H, T, wd), F32), _sds((H, T, wd), F32), _sds((H, n, tq), F32)],
        scratch_shapes=[pltpu.VMEM((tq, 1), F32)],
        compiler_params=_params(("arbitrary", "arbitrary")),
    )(q_aug, do_aug, k_aug, v_aug)


def _complex_step(a_r, a_i, cr, ci, br, bi):
    return a_r * cr - a_i * ci + br, a_r * ci + a_i * cr + bi


_HALF_CH = SSM_W // 2
_HALF_ST = STATE_W // 2


def _state_cols(part, half):
    lo = part * STATE_W + half * _HALF_ST
    return slice(lo, lo + _HALF_ST)


def _channels_to_states(x, w_ref, out_ref):
    for half in range(2):
        ch = slice(half * _HALF_CH, (half + 1) * _HALF_CH)
        for part in range(2):
            cols = _state_cols(part, half)
            out_ref[:, cols] = _dot(x[:, ch], w_ref[ch, cols])


def _states_to_channels(x, w_ref):
    halves = []
    for half in range(2):
        ch = slice(half * _HALF_CH, (half + 1) * _HALF_CH)
        halves.append(_dot(x[:, _state_cols(0, half)], w_ref[_state_cols(0, half), ch])
                      + _dot(x[:, _state_cols(1, half)], w_ref[_state_cols(1, half), ch]))
    return jnp.concatenate(halves, axis=1)


def ssm_fwd(s_perm, wb, cbd, a_r, a_i, al_r, al_i, dvec):
    T = s_perm.shape[0]
    chunk = T // 8
    ts = _tile(chunk, SCAN_STEPS)
    tr, n_s = ts * 8, chunk // ts
    W, LB = STATE_W, SCAN_LANES

    def body(s_ref, wb_ref, cbd_ref, ar_ref, ai_ref, alr_ref, ali_ref, dv_ref, y_ref, xs_ref, bu, carry):
        ph, i = pl.program_id(0), pl.program_id(1)

        @pl.when((ph == 0) & (i == 0))
        def _():
            carry[...] = jnp.zeros_like(carry)

        _channels_to_states(s_ref[...].astype(BF16), wb_ref, bu)

        def scan(store):
            for lb in range(W // LB):
                lo = lb * LB
                re, im = slice(lo, lo + LB), slice(W + lo, W + lo + LB)
                ar = jnp.broadcast_to(ar_ref[:, re], (8, LB))
                ai = jnp.broadcast_to(ai_ref[:, re], (8, LB))

                def step(s, c):
                    rows = pl.ds(pl.multiple_of(s * 8, 8), 8)
                    nr, ni = _complex_step(ar, ai, c[0], c[1], bu[rows, re], bu[rows, im])
                    if store:
                        bu[rows, re] = nr
                        bu[rows, im] = ni
                    return nr, ni

                cr, ci = lax.fori_loop(0, ts, step, (carry[:, re], carry[:, im]), unroll=2)
                carry[:, re] = cr
                carry[:, im] = ci

        @pl.when(ph == 0)
        def _():
            scan(False)

            @pl.when(i == n_s - 1)
            def _():
                er, ei = carry[:, :W], carry[:, W:]
                alr = jnp.broadcast_to(alr_ref[...], (8, W))
                ali = jnp.broadcast_to(ali_ref[...], (8, W))
                first = lax.broadcasted_iota(jnp.int32, (8, W), 0) == 0
                sr, si = jnp.zeros((8, W), F32), jnp.zeros((8, W), F32)
                for _ in range(7):
                    vr, vi = _complex_step(alr, ali, sr, si, er, ei)
                    sr = jnp.where(first, 0.0, pltpu.roll(vr, 1, 0))
                    si = jnp.where(first, 0.0, pltpu.roll(vi, 1, 0))
                carry[:, :W] = sr
                carry[:, W:] = si

        @pl.when(ph == 1)
        def _():
            scan(True)
            xb = bu[...].astype(BF16)
            xs_ref[...] = xb
            y_ref[...] = _states_to_channels(xb, cbd_ref) + s_ref[...] * dv_ref[...]

    fix = lambda p, i: (0, 0)
    return pl.pallas_call(
        body, name="ssm_fwd", grid=(2, n_s),
        in_specs=[pl.BlockSpec((tr, SSM_W), lambda p, i: (i, 0)), pl.BlockSpec((SSM_W, 2 * W), fix),
                  pl.BlockSpec((2 * W, SSM_W), fix), pl.BlockSpec((1, W), fix), pl.BlockSpec((1, W), fix),
                  pl.BlockSpec((1, W), fix), pl.BlockSpec((1, W), fix), pl.BlockSpec((1, SSM_W), fix)],
        out_specs=[pl.BlockSpec((tr, SSM_W), lambda p, i: (i * p, 0)), pl.BlockSpec((tr, 2 * W), lambda p, i: (i * p, 0))],
        out_shape=[_sds((T, SSM_W), F32), _sds((T, 2 * W), BF16)],
        scratch_shapes=[pltpu.VMEM((tr, 2 * W), F32), pltpu.VMEM((8, 2 * W), F32)],
        compiler_params=_params(("arbitrary", "arbitrary")),
    )(s_perm, wb, cbd, a_r, a_i, al_r, al_i, dvec)


def ssm_bwd(dy_perm, s_perm, xs, cbd_t, wb_t, a_r, a_i, al_r, al_i, dvec):
    T = s_perm.shape[0]
    chunk = T // 8
    ts = _tile(chunk, SCAN_STEPS)
    tr, n_s = ts * 8, chunk // ts
    W, LB = STATE_W, SCAN_LANES

    def body(dy_ref, s_ref, xs_ref, cbt_ref, wbt_ref, ar_ref, ai_ref, alr_ref, ali_ref, dv_ref,
             du_ref, gs_ref, da_ref, dd_ref, gd, x32, carry):
        ph, i = pl.program_id(0), pl.program_id(1)

        @pl.when((ph == 0) & (i == 0))
        def _():
            carry[...] = jnp.zeros_like(carry)
            da_ref[...] = jnp.zeros_like(da_ref)
            dd_ref[...] = jnp.zeros_like(dd_ref)

        _channels_to_states(dy_ref[...].astype(BF16), cbt_ref, gd)

        def scan(store):
            for lb in range(W // LB):
                lo = lb * LB
                re, im = slice(lo, lo + LB), slice(W + lo, W + lo + LB)
                ar = jnp.broadcast_to(ar_ref[:, re], (8, LB))
                nai = -jnp.broadcast_to(ai_ref[:, re], (8, LB))

                def step(k, c):
                    rows = pl.ds(pl.multiple_of((ts - 1 - k) * 8, 8), 8)
                    cr, ci = c[0], c[1]
                    nr, ni = _complex_step(ar, nai, cr, ci, gd[rows, re], gd[rows, im])
                    if store:
                        xr, xi = x32[rows, re], x32[rows, im]
                        gd[rows, re] = nr
                        gd[rows, im] = ni
                        return nr, ni, c[2] + cr * xr + ci * xi, c[3] + ci * xr - cr * xi
                    return nr, ni

                init = (carry[:, re], carry[:, im])
                if store:
                    init = init + (da_ref[:, re], da_ref[:, im])
                out = lax.fori_loop(0, ts, step, init, unroll=2)
                carry[:, re] = out[0]
                carry[:, im] = out[1]
                if store:
                    da_ref[:, re] = out[2]
                    da_ref[:, im] = out[3]

        @pl.when(ph == 0)
        def _():
            scan(False)

            @pl.when(i == n_s - 1)
            def _():
                er, ei = carry[:, :W], carry[:, W:]
                alr = jnp.broadcast_to(alr_ref[...], (8, W))
                nali = -jnp.broadcast_to(ali_ref[...], (8, W))
                last = lax.broadcasted_iota(jnp.int32, (8, W), 0) == 7
                rr, ri = jnp.zeros((8, W), F32), jnp.zeros((8, W), F32)
                for _ in range(7):
                    vr, vi = _complex_step(alr, nali, rr, ri, er, ei)
                    rr = jnp.where(last, 0.0, pltpu.roll(vr, 7, 0))
                    ri = jnp.where(last, 0.0, pltpu.roll(vi, 7, 0))
                carry[:, :W] = rr
                carry[:, W:] = ri

        @pl.when(ph == 1)
        def _():
            x32[...] = xs_ref[...].astype(F32)
            scan(True)
            gb = gd[...].astype(BF16)
            gs_ref[...] = gb
            dy = dy_ref[...]
            du_ref[...] = _states_to_channels(gb, wbt_ref) + dy * dv_ref[...]
            dd_ref[...] += jnp.sum(dy * s_ref[...], axis=0, keepdims=True)

    fix = lambda p, i: (0, 0)
    rev = lambda p, i: (n_s - 1 - i, 0)
    rev_out = lambda p, i: (n_s - 1 - i * p, 0)
    return pl.pallas_call(
        body, name="ssm_bwd", grid=(2, n_s),
        in_specs=[pl.BlockSpec((tr, SSM_W), rev), pl.BlockSpec((tr, SSM_W), rev), pl.BlockSpec((tr, 2 * W), rev),
                  pl.BlockSpec((SSM_W, 2 * W), fix), pl.BlockSpec((2 * W, SSM_W), fix), pl.BlockSpec((1, W), fix),
                  pl.BlockSpec((1, W), fix), pl.BlockSpec((1, W), fix), pl.BlockSpec((1, W), fix),
                  pl.BlockSpec((1, SSM_W), fix)],
        out_specs=[pl.BlockSpec((tr, SSM_W), rev_out), pl.BlockSpec((tr, 2 * W), rev_out),
                   pl.BlockSpec((8, 2 * W), fix), pl.BlockSpec((1, SSM_W), fix)],
        out_shape=[_sds((T, SSM_W), F32), _sds((T, 2 * W), BF16), _sds((8, 2 * W), F32), _sds((1, SSM_W), F32)],
        scratch_shapes=[pltpu.VMEM((tr, 2 * W), F32), pltpu.VMEM((tr, 2 * W), F32), pltpu.VMEM((8, 2 * W), F32)],
        compiler_params=_params(("arbitrary", "arbitrary")),
    )(dy_perm, s_perm, xs, cbd_t, wb_t, a_r, a_i, al_r, al_i, dvec)


def _join_heads(ref, dtype):
    def move(h, dst):
        x = ref[h]
        pieces = _pieces(x) if dtype == F32 else (x.astype(BF16),)
        out = None
        for piece in pieces:
            term = _dot(piece, _lane_move(0, dst, HEAD_DIM, BF16))
            out = term if out is None else out + term
        return out

    return jnp.concatenate([move(2 * p, 0) + move(2 * p + 1, HEAD_DIM) for p in range(N_HEADS // 2)], axis=1)


def mixout_fwd(h1, o_heads, ypre, g_a, g_s, w_glu, b_glu, w_out):
    T = h1.shape[0]
    tm = _tile(T, TOKEN_TILE)

    def body(h_ref, at_ref, yp_ref, ga_ref, gs_ref, wg_ref, bg_ref, wo_ref, h2_ref, mixed_ref):
        yg, _ = _gelu_parts(yp_ref[...])
        gl = yg * jax.nn.sigmoid(_dot(yg.astype(BF16), wg_ref[...]) + bg_ref[...])
        at = _join_heads(at_ref, F32)
        mixed = jnp.concatenate([at * _rms_scale(at) * ga_ref[...], gl * _rms_scale(gl) * gs_ref[...]], axis=1)
        mixed = mixed.astype(BF16)
        mixed_ref[...] = mixed
        h2_ref[...] = h_ref[...] + _dot(mixed, wo_ref[...])

    tok = lambda i: (i, 0)
    fix = lambda i: (0, 0)
    return pl.pallas_call(
        body, name="mixout_fwd", grid=(T // tm,),
        in_specs=[pl.BlockSpec((tm, D_MODEL), tok), pl.BlockSpec((N_HEADS, tm, 128), lambda i: (0, i, 0)),
                  pl.BlockSpec((tm, SSM_W), tok),
                  pl.BlockSpec((1, ATTN_W), fix), pl.BlockSpec((1, SSM_W), fix), pl.BlockSpec((SSM_W, SSM_W), fix),
                  pl.BlockSpec((1, SSM_W), fix), pl.BlockSpec((D_MODEL, D_MODEL), fix)],
        out_specs=[pl.BlockSpec((tm, D_MODEL), tok), pl.BlockSpec((tm, D_MODEL), tok)],
        out_shape=[_sds((T, D_MODEL), F32), _sds((T, D_MODEL), BF16)],
        compiler_params=_params(("arbitrary",)),
    )(h1, o_heads, ypre, g_a, g_s, w_glu, b_glu, w_out)


def mixout_bwd(dh2, o_heads, ypre, g_a, g_s, w_glu, b_glu, w_out, seg):
    T = dh2.shape[0]
    tm = _tile(T, TOKEN_TILE)

    def body(dh_ref, at_ref, yp_ref, ga_ref, gs_ref, wg_ref, bg_ref, wo_ref, seg_ref,
             do_ref, dyp_ref, dpre_ref, yg_ref, dga_ref, dgs_ref, dbg_ref):
        @pl.when(pl.program_id(0) == 0)
        def _():
            dga_ref[...] = jnp.zeros_like(dga_ref)
            dgs_ref[...] = jnp.zeros_like(dgs_ref)
            dbg_ref[...] = jnp.zeros_like(dbg_ref)

        dmix = _dot_nt(dh_ref[...].astype(BF16), wo_ref[...])
        at = _join_heads(at_ref, F32)
        dat, dga = _rms_bwd(dmix[:, :ATTN_W], at, ga_ref[...])
        dga_ref[...] += dga
        delta = _pieces(jnp.dot(dat * at, seg_ref[...], precision=HIGHEST, preferred_element_type=F32))
        datb = dat.astype(BF16)
        for h in range(N_HEADS):
            p, e = divmod(h, 2)
            do_ref[h] = (_head_features(datb[:, 128 * p:128 * (p + 1)], e) + _helper_columns(delta, h, -1.0)).astype(BF16)
        yp = yp_ref[...]
        yg, t = _gelu_parts(yp)
        ygb = yg.astype(BF16)
        yg_ref[...] = ygb
        sg = jax.nn.sigmoid(_dot(ygb, wg_ref[...]) + bg_ref[...])
        dgl, dgs = _rms_bwd(dmix[:, ATTN_W:], yg * sg, gs_ref[...])
        dgs_ref[...] += dgs
        dpre = dgl * yg * sg * (1.0 - sg)
        dbg_ref[...] += jnp.sum(dpre, axis=0, keepdims=True)
        dpb = dpre.astype(BF16)
        dpre_ref[...] = dpb
        dyg = dgl * sg + _dot_nt(dpb, wg_ref[...])
        dyp_ref[...] = dyg * _gelu_grad(yp, t)

    tok = lambda i: (i, 0)
    fix = lambda i: (0, 0)
    heads = pl.BlockSpec((N_HEADS, tm, 128), lambda i: (0, i, 0))
    return pl.pallas_call(
        body, name="mixout_bwd", grid=(T // tm,),
        in_specs=[pl.BlockSpec((tm, D_MODEL), tok), heads, pl.BlockSpec((tm, SSM_W), tok),
                  pl.BlockSpec((1, ATTN_W), fix), pl.BlockSpec((1, SSM_W), fix), pl.BlockSpec((SSM_W, SSM_W), fix),
                  pl.BlockSpec((1, SSM_W), fix), pl.BlockSpec((D_MODEL, D_MODEL), fix), pl.BlockSpec((ATTN_W, 128), fix)],
        out_specs=[heads, pl.BlockSpec((tm, SSM_W), tok), pl.BlockSpec((tm, SSM_W), tok),
                   pl.BlockSpec((tm, SSM_W), tok), pl.BlockSpec((1, ATTN_W), fix),
                   pl.BlockSpec((1, SSM_W), fix), pl.BlockSpec((1, SSM_W), fix)],
        out_shape=[_sds((N_HEADS, T, 128), BF16), _sds((T, SSM_W), F32), _sds((T, SSM_W), BF16), _sds((T, SSM_W), BF16),
                   _sds((1, ATTN_W), F32), _sds((1, SSM_W), F32), _sds((1, SSM_W), F32)],
        compiler_params=_params(("arbitrary",)),
    )(dh2, o_heads, ypre, g_a, g_s, w_glu, b_glu, w_out, seg)


def head_fwd_bwd(h3, p, target, g_ple, g_final, w_gate, w_proj):
    T = h3.shape[0]
    tm = _tile(T, TOKEN_TILE)
    pd = p.shape[1]

    def body(h_ref, p_ref, tg_ref, gp_ref, gf_ref, wg_ref, wp_ref,
             dh_ref, n3_ref, dz_ref, dpp_ref, loss_ref, dgp_ref, dgf_ref):
        @pl.when(pl.program_id(0) == 0)
        def _():
            loss_ref[...] = jnp.zeros_like(loss_ref)
            dgp_ref[...] = jnp.zeros_like(dgp_ref)
            dgf_ref[...] = jnp.zeros_like(dgf_ref)

        x = h_ref[...]
        gp, gf = gp_ref[...], gf_ref[...]
        n3 = (x * _rms_scale(x) * gp).astype(BF16)
        n3_ref[...] = n3
        gate = jax.nn.sigmoid(_dot(n3, wg_ref[...]))
        pp = _dot(p_ref[...].astype(BF16), wp_ref[...])
        h4 = x + gate * pp
        y = h4 * _rms_scale(h4) * gf
        e = y - tg_ref[...]
        tile_loss = jnp.sum(jnp.sum(e * e, axis=1, keepdims=True), axis=0, keepdims=True) * (0.5 / D_MODEL)
        loss_ref[...] += jnp.broadcast_to(tile_loss, loss_ref.shape)
        dh4, dgf = _rms_bwd(e * (1.0 / D_MODEL), h4, gf)
        dgf_ref[...] += dgf
        dzg = dh4 * pp * gate * (1.0 - gate)
        dzb = dzg.astype(BF16)
        dz_ref[...] = dzb
        dpp_ref[...] = (dh4 * gate).astype(BF16)
        dx, dgp = _rms_bwd(_dot_nt(dzb, wg_ref[...]), x, gp)
        dgp_ref[...] += dgp
        dh_ref[...] = dh4 + dx

    tok = lambda i: (i, 0)
    fix = lambda i: (0, 0)
    return pl.pallas_call(
        body, name="head_fwd_bwd", grid=(T // tm,),
        in_specs=[pl.BlockSpec((tm, D_MODEL), tok), pl.BlockSpec((tm, pd), tok), pl.BlockSpec((tm, D_MODEL), tok),
                  pl.BlockSpec((1, D_MODEL), fix), pl.BlockSpec((1, D_MODEL), fix), pl.BlockSpec((D_MODEL, D_MODEL), fix),
                  pl.BlockSpec((pd, D_MODEL), fix)],
        out_specs=[pl.BlockSpec((tm, D_MODEL), tok), pl.BlockSpec((tm, D_MODEL), tok), pl.BlockSpec((tm, D_MODEL), tok),
                   pl.BlockSpec((tm, D_MODEL), tok), pl.BlockSpec((8, 128), fix), pl.BlockSpec((1, D_MODEL), fix),
                   pl.BlockSpec((1, D_MODEL), fix)],
        out_shape=[_sds((T, D_MODEL), F32), _sds((T, D_MODEL), BF16), _sds((T, D_MODEL), BF16), _sds((T, D_MODEL), BF16),
                   _sds((8, 128), F32), _sds((1, D_MODEL), F32), _sds((1, D_MODEL), F32)],
        compiler_params=_params(("arbitrary",)),
    )(h3, p, target, g_ple, g_final, w_gate, w_proj)


def _row_tile(rows, cols, n_arrays):
    lanes = -(-cols // 128) * 128
    cap = VMEM_LIMIT // 3 // (2 * n_arrays * lanes * 4)
    best = None
    for t in range(PACK_ALIGN, min(rows, cap) + 1, PACK_ALIGN):
        if rows % t == 0:
            best = t
    assert best is not None, (rows, cols)
    return best


def _adamw_math(w, g, m, v):
    nm = ADAM_B1 * m + (1.0 - ADAM_B1) * g
    nv = ADAM_B2 * v + (1.0 - ADAM_B2) * (g * g)
    c1 = 1.0 - ADAM_B1 ** ADAM_STEP
    c2 = 1.0 - ADAM_B2 ** ADAM_STEP
    return -ADAM_LR * ((nm / c1) / (jnp.sqrt(nv / c2) + ADAM_EPS) + ADAM_WD * w), nm, nv


def adamw(w, g, m, v, name):
    R, C = w.shape
    tr = _row_tile(R, C, 7)

    def body(w_ref, g_ref, m_ref, v_ref, d_ref, nm_ref, nv_ref):
        d_ref[...], nm_ref[...], nv_ref[...] = _adamw_math(w_ref[...], g_ref[...], m_ref[...], v_ref[...])

    spec = pl.BlockSpec((tr, C), lambda i: (i, 0))
    return pl.pallas_call(
        body, name=name, grid=(R // tr,), in_specs=[spec] * 4, out_specs=[spec] * 3,
        out_shape=[_sds((R, C), F32)] * 3, compiler_params=_params(("arbitrary",)),
    )(w, g, m, v)


def join_halves(mine, other, core):
    rh, C = mine.shape
    tr = _row_tile(rh, C, 3)
    nb = rh // tr

    def body(c_ref, m_ref, o_ref, out_ref):
        out_ref[...] = jnp.where((pl.program_id(0) // nb) == c_ref[0], m_ref[...], o_ref[...])

    half = pl.BlockSpec((tr, C), lambda i, c: (i % nb, 0))
    return pl.pallas_call(
        body, name="join_halves",
        grid_spec=pltpu.PrefetchScalarGridSpec(num_scalar_prefetch=1, grid=(2 * nb,), in_specs=[half, half],
                                               out_specs=pl.BlockSpec((tr, C), lambda i, c: (i, 0))),
        out_shape=_sds((2 * rh, C), F32), compiler_params=_params(("arbitrary",)),
    )(core, mine, other)


def pair_sum(g, theirs, core):
    n, R, C = g.shape
    rh = R // 2
    tr = _row_tile(rh, C, 3)
    nb = rh // tr

    def body(c_ref, g_ref, t_ref, o_ref):
        o_ref[...] = (g_ref[...] + t_ref[...]).astype(BF16)

    here = pl.BlockSpec((1, tr, C), lambda j, i, c: (j, i, 0))
    return pl.pallas_call(
        body, name="pair_sum",
        grid_spec=pltpu.PrefetchScalarGridSpec(
            num_scalar_prefetch=1, grid=(n, nb),
            in_specs=[pl.BlockSpec((1, tr, C), lambda j, i, c: (j, c[0] * nb + i, 0)), here], out_specs=here),
        out_shape=_sds((n, rh, C), BF16), compiler_params=_params(("arbitrary", "arbitrary")),
    )(core, g, theirs)


def chip_sum(pair, got, chip):
    _, R, C = pair.shape
    tr = _row_tile(R, C, 5)

    def body(c_ref, p_ref, g0_ref, g1_ref, g2_ref, o_ref):
        f = lambda ref: ref[0].astype(F32)
        o_ref[...] = ((f(p_ref) + f(g0_ref)) + f(g1_ref)) + f(g2_ref)

    slot = lambda k: pl.BlockSpec((1, tr, C), lambda i, c: (k, i, 0))
    return pl.pallas_call(
        body, name="chip_sum",
        grid_spec=pltpu.PrefetchScalarGridSpec(
            num_scalar_prefetch=1, grid=(R // tr,),
            in_specs=[pl.BlockSpec((1, tr, C), lambda i, c: (c[0], i, 0)), slot(0), slot(1), slot(2)],
            out_specs=pl.BlockSpec((tr, C), lambda i, c: (i, 0))),
        out_shape=_sds((R, C), F32), compiler_params=_params(("arbitrary",)),
    )(chip, pair, got, got, got)


_HBM = pl.BlockSpec(memory_space=pltpu.HBM)


def _place():
    x, y, c = lax.axis_index("x"), lax.axis_index("y"), lax.axis_index("c")
    return x, y, c, [(1 - x, y), (x, 1 - y), (1 - x, 1 - y)]


def _spans(rows, n):
    assert rows % PACK_ALIGN == 0
    tiles = rows // PACK_ALIGN
    n = min(n, tiles)
    cuts = [tiles * q // n for q in range(n + 1)]
    return [(cuts[q] * PACK_ALIGN, (cuts[q + 1] - cuts[q]) * PACK_ALIGN) for q in range(n)]


def _remote(src, dst, send_sem, recv_sem, to):
    return pltpu.make_async_remote_copy(src_ref=src, dst_ref=dst, send_sem=send_sem, recv_sem=recv_sem,
                                        device_id=to, device_id_type=MESH)


def allgather_shards(wp):
    R, C = wp.shape
    rh = R // 2
    spans = _spans(rh, COPY_CHUNKS)
    n_sp = len(spans)
    local_spans = _spans(R, 2 * COPY_CHUNKS)

    def body(w_ref, out_ref, send_sems, recv_sems, pass_send, pass_recv, local_sems):
        x, y, c, chips = _place()
        me = 2 * x + y
        local = []
        for q, (o, n) in enumerate(local_spans):
            cp = pltpu.make_async_copy(w_ref.at[pl.ds(o, n), :], out_ref.at[me, pl.ds(o, n), :], local_sems.at[q])
            cp.start()
            local.append(cp)
        sends = []
        for k, (cx, cy) in enumerate(chips):
            for q, (o, n) in enumerate(spans):
                rows = pl.ds(c * rh + o, n)
                cp = _remote(w_ref.at[rows, :], out_ref.at[me, rows, :], send_sems.at[k * n_sp + q],
                             recv_sems.at[k * n_sp + q], (cx, cy, c))
                cp.start()
                sends.append(cp)
        for k, (cx, cy) in enumerate(chips):
            for q, (o, n) in enumerate(spans):
                blk = out_ref.at[2 * cx + cy, pl.ds(c * rh + o, n), :]
                _remote(blk, blk, send_sems.at[k * n_sp + q], recv_sems.at[k * n_sp + q], (cx, cy, c)).wait_recv()
                cp = _remote(blk, blk, pass_send.at[k * n_sp + q], pass_recv.at[k * n_sp + q], (x, y, 1 - c))
                cp.start()
                sends.append(cp)
        for k, (cx, cy) in enumerate(chips):
            for q, (o, n) in enumerate(spans):
                blk = out_ref.at[2 * cx + cy, pl.ds((1 - c) * rh + o, n), :]
                _remote(blk, blk, pass_send.at[k * n_sp + q], pass_recv.at[k * n_sp + q], (x, y, 1 - c)).wait_recv()
        for cp in sends:
            cp.wait_send()
        for cp in local:
            cp.wait()

    sems = pltpu.SemaphoreType.DMA((3 * n_sp,))
    return pl.pallas_call(
        body, name="allgather_shards", in_specs=[_HBM], out_specs=_HBM, out_shape=_sds((4, R, C), wp.dtype),
        scratch_shapes=[sems, sems, sems, sems, pltpu.SemaphoreType.DMA((len(local_spans),))],
    )(wp)


def sibling_split(g):
    n_sl, R, C = g.shape
    rh = R // 2
    spans = _spans(rh, COPY_CHUNKS)
    n_sp = len(spans)

    def body(g_ref, got_ref, send_sems, recv_sems):
        x, y, c, _ = _place()
        copies = []
        for j in range(n_sl):
            for q, (o, n) in enumerate(spans):
                cp = _remote(g_ref.at[j, pl.ds((1 - c) * rh + o, n), :], got_ref.at[j, pl.ds(o, n), :],
                             send_sems.at[j * n_sp + q], recv_sems.at[j * n_sp + q], (x, y, 1 - c))
                cp.start()
                copies.append(cp)
        for cp in copies:
            cp.wait()

    sems = pltpu.SemaphoreType.DMA((n_sl * n_sp,))
    return pl.pallas_call(
        body, name="sibling_split", in_specs=[_HBM], out_specs=_HBM, out_shape=_sds((n_sl, rh, C), g.dtype),
        scratch_shapes=[sems, sems],
    )(g)


def chip_exchange(p):
    _, R, C = p.shape
    spans = _spans(R, COPY_CHUNKS)
    n_sp = len(spans)

    def body(p_ref, buf_ref, send_sems, recv_sems):
        x, y, c, chips = _place()
        sends = []
        for k, (cx, cy) in enumerate(chips):
            for q, (o, n) in enumerate(spans):
                cp = _remote(p_ref.at[2 * cx + cy, pl.ds(o, n), :], buf_ref.at[k, pl.ds(o, n), :],
                             send_sems.at[k * n_sp + q], recv_sems.at[k * n_sp + q], (cx, cy, c))
                cp.start()
                sends.append(cp)
        for cp in sends:
            cp.wait()

    sems = pltpu.SemaphoreType.DMA((3 * n_sp,))
    return pl.pallas_call(
        body, name="chip_exchange", in_specs=[_HBM], out_specs=_HBM, out_shape=_sds((3, R, C), p.dtype),
        scratch_shapes=[sems, sems],
    )(p)


def sibling_swap(half):
    R, C = half.shape
    spans = _spans(R, COPY_CHUNKS)

    def body(h_ref, got_ref, send_sems, recv_sems):
        x, y, c, _ = _place()
        copies = []
        for q, (o, n) in enumerate(spans):
            cp = _remote(h_ref.at[pl.ds(o, n), :], got_ref.at[pl.ds(o, n), :], send_sems.at[q], recv_sems.at[q], (x, y, 1 - c))
            cp.start()
            copies.append(cp)
        for cp in copies:
            cp.wait()

    sems = pltpu.SemaphoreType.DMA((len(spans),))
    return pl.pallas_call(
        body, name="sibling_swap", in_specs=[_HBM], out_specs=_HBM, out_shape=_sds((R, C), half.dtype),
        scratch_shapes=[sems, sems],
    )(half)


def allreduce_small(v):
    R, C = v.shape

    def body(v_ref, out_ref, buf, send_sems, recv_sems):
        x, y, c, _ = _place()
        me = 4 * x + 2 * y + c
        buf[me] = v_ref[...]
        flips = [((k >> 2) & 1, (k >> 1) & 1, k & 1) for k in range(1, 8)]
        sends = []
        for k, (fx, fy, fc) in enumerate(flips):
            to = (1 - x if fx else x, 1 - y if fy else y, 1 - c if fc else c)
            cp = _remote(v_ref, buf.at[me], send_sems.at[k], recv_sems.at[k], to)
            cp.start()
            sends.append(cp)
        for k, (fx, fy, fc) in enumerate(flips):
            px, py, pc = (1 - x if fx else x, 1 - y if fy else y, 1 - c if fc else c)
            blk = buf.at[4 * px + 2 * py + pc]
            _remote(blk, blk, send_sems.at[k], recv_sems.at[k], (px, py, pc)).wait_recv()
        for cp in sends:
            cp.wait_send()
        acc = buf[0]
        for s in range(1, 8):
            acc = acc + buf[s]
        out_ref[...] = acc

    vm = pl.BlockSpec(memory_space=pltpu.VMEM)
    return pl.pallas_call(
        body, name="allreduce_small", in_specs=[vm], out_specs=vm, out_shape=_sds((R, C), F32),
        scratch_shapes=[pltpu.VMEM((8, R, C), F32), pltpu.SemaphoreType.DMA((7,)), pltpu.SemaphoreType.DMA((7,))],
        compiler_params=pltpu.CompilerParams(vmem_limit_bytes=VMEM_LIMIT),
    )(v)


def _rows_of(shape):
    return shape[0] * shape[1] // PACK_COLS


def _slot_rows(shape):
    return -(-_rows_of(shape) // PACK_ALIGN) * PACK_ALIGN


TRANSPOSED = ("w1_a", "w3_a", "w1_b", "w3_b")


def _stored(name, shard):
    return shard[0].T if name in TRANSPOSED else shard[0]


def _restored(name, stored):
    return stored.T[None] if name in TRANSPOSED else stored[None]


def _pack_shards(shards, dtype):
    parts = []
    for name, shape, _ in BIG:
        part = _stored(name, shards[name]).reshape(_rows_of(shape), PACK_COLS).astype(dtype)
        parts.append(jnp.pad(part, ((0, _slot_rows(shape) - part.shape[0]), (0, 0))))
    used = sum(p.shape[0] for p in parts)
    parts.append(jnp.zeros((PACK_ROWS - used, PACK_COLS), dtype))
    return jnp.concatenate(parts, axis=0)


def _unpack_gathered(ag):
    out, off = {}, 0
    for name, shape, axis in BIG:
        r = _rows_of(shape)
        piece = ag[:, off:off + r, :]
        off += _slot_rows(shape)
        if name in TRANSPOSED:
            out[name] = piece.reshape(4 * r, PACK_COLS)
        elif axis == 0:
            out[name] = piece.reshape(4 * shape[0], shape[1])
        else:
            out[name] = piece.reshape((4,) + shape).transpose(1, 0, 2).reshape(shape[0], 4 * shape[1])
    return out


def _pack_full_grads(grads):
    parts = []
    for name, shape, axis in BIG:
        g = grads[name]
        if name in TRANSPOSED or axis == 0:
            piece = g.reshape(4, _rows_of(shape), PACK_COLS)
        else:
            piece = g.reshape(shape[0], 4, shape[1]).transpose(1, 0, 2).reshape(4, _rows_of(shape), PACK_COLS)
        parts.append(jnp.pad(piece, ((0, 0), (0, _slot_rows(shape) - piece.shape[1]), (0, 0))))
    used = sum(p.shape[1] for p in parts)
    parts.append(jnp.zeros((4, PACK_ROWS - used, PACK_COLS), F32))
    return jnp.concatenate(parts, axis=1)


def _unpack_shards(packed):
    out, off = {}, 0
    for name, shape, _ in BIG:
        r = _rows_of(shape)
        out[name] = packed[off:off + r] if name in TRANSPOSED else packed[off:off + r].reshape(shape)
        off += _slot_rows(shape)
    return out


def _pack_small(vals, extra=None):
    parts = [vals[name].reshape(-1) for name, _ in SMALL]
    used = sum(p.shape[0] for p in parts)
    if extra is not None:
        parts.append(extra.reshape(1))
        used += 1
    parts.append(jnp.zeros((SMALL_ROWS * 128 - used,), F32))
    return jnp.concatenate(parts).reshape(SMALL_ROWS, 128)


def _unpack_small(packed):
    flat = packed.reshape(-1)
    out, off = {}, 0
    for name, shape in SMALL:
        n = math.prod(shape)
        out[name] = flat[off:off + n].reshape(shape)
        off += n
    return out, flat[off]


def _permute_time(a):
    T, n = a.shape
    return a.reshape(8, T // 8, n).transpose(1, 0, 2).reshape(T, n)


def _unpermute_time(a):
    T, n = a.shape
    return a.reshape(T // 8, 8, n).transpose(1, 0, 2).reshape(T, n)


def _discretize(a_re, a_im, log_dt, b_re, b_im):
    dt = jnp.exp(log_dt)[:, None]
    decay = jnp.exp(dt * a_re)
    abar_r = decay * jnp.cos(dt * a_im)
    abar_i = decay * jnp.sin(dt * a_im)
    nr, ni = abar_r - 1.0, abar_i
    den = a_re * a_re + a_im * a_im
    fr = (nr * a_re + ni * a_im) / den
    fi = (ni * a_re - nr * a_im) / den
    bbar_r = fr[..., None] * b_re - fi[..., None] * b_im
    bbar_i = fr[..., None] * b_im + fi[..., None] * b_re
    return abar_r, abar_i, bbar_r, bbar_i


def _input_matrix(bbar_r, bbar_i):
    eye = jnp.eye(N_GROUPS, dtype=F32)
    blk = lambda b: jnp.einsum("ghp,gk->ghkp", b.transpose(0, 2, 1), eye).reshape(SSM_W, STATE_W)
    return jnp.concatenate([blk(bbar_r), blk(bbar_i)], axis=1)


def _output_matrix(c_re, c_im):
    eye = jnp.eye(N_GROUPS, dtype=F32)
    blk = lambda cm: jnp.einsum("ghp,gk->gpkh", cm, eye).reshape(STATE_W, SSM_W)
    return jnp.concatenate([blk(c_re), -blk(c_im)], axis=0)


def _state_power(ar, ai, n):
    steps = int(round(math.log2(n)))
    assert 1 << steps == n
    for _ in range(steps):
        ar, ai = ar * ar - ai * ai, 2.0 * ar * ai
    return ar, ai


def kernel(x, p, g_ffn1, w1_a, w3_a, w2_a, g_mix, w_in, b_f, a_re, a_im, log_dt, b_re, b_im, c_re, c_im, d_skip, w_glu, b_glu, g_attn_out, g_ssm_out, w_out, g_ffn2, w1_b, w3_b, w2_b, g_ple, w_ple_gate, w_ple_proj, g_final, loss_target, m_g_ffn1, m_w1_a, m_w3_a, m_w2_a, m_g_mix, m_w_in, m_b_f, m_a_re, m_a_im, m_log_dt, m_b_re, m_b_im, m_c_re, m_c_im, m_d_skip, m_w_glu, m_b_glu, m_g_attn_out, m_g_ssm_out, m_w_out, m_g_ffn2, m_w1_b, m_w3_b, m_w2_b, m_g_ple, m_w_ple_gate, m_w_ple_proj, m_g_final, v_g_ffn1, v_w1_a, v_w3_a, v_w2_a, v_g_mix, v_w_in, v_b_f, v_a_re, v_a_im, v_log_dt, v_b_re, v_b_im, v_c_re, v_c_im, v_d_skip, v_w_glu, v_b_glu, v_g_attn_out, v_g_ssm_out, v_w_out, v_g_ffn2, v_w1_b, v_w3_b, v_w2_b, v_g_ple, v_w_ple_gate, v_w_ple_proj, v_g_final):
    args = dict(locals())
    weights = {n: args[n] for n in WEIGHT_ORDER}
    moms = {n: args["m_" + n] for n in WEIGHT_ORDER}
    vars_ = {n: args["v_" + n] for n in WEIGHT_ORDER}
    T = x.shape[1]
    x2, p2, tgt = x[0], p[0, 0], loss_target[0]

    full = _unpack_gathered(allgather_shards(_pack_shards(weights, BF16)))
    loss_part, dx, grads = _local_step(x2, p2, tgt, {n: weights[n] for n, _ in SMALL}, full)
    return _reduce_and_update(weights, moms, vars_, loss_part, dx, grads)


def _local_step(x2, p2, tgt, sm, full):
    T = x2.shape[0]
    (g_ffn1, g_mix, b_f, a_re, a_im, log_dt, b_re, b_im, c_re, c_im, d_skip, b_glu, g_attn_out, g_ssm_out, g_ffn2, g_ple,
     g_final) = (sm[n] for n, _ in SMALL)
    w_in_f = full["w_in"]
    w_in_r = jnp.concatenate([w_in_f[:, :ATTN_W] * QK_SCALE, w_in_f[:, ATTN_W:3 * ATTN_W], w_in_f[:, 3 * ATTN_W + N_HEADS:],
                              w_in_f[:, 3 * ATTN_W:3 * ATTN_W + N_HEADS], jnp.zeros((D_MODEL, 128 - N_HEADS), BF16)], axis=1)
    b_f_pad = jnp.pad(b_f, ((0, 0), (0, 128 - N_HEADS)))

    disc_in = (a_re[0], a_im[0], log_dt[0], b_re[0], b_im[0])
    (abar_r, abar_i, bbar_r, bbar_i), disc_vjp = jax.vjp(_discretize, *disc_in)
    wb = _input_matrix(bbar_r, bbar_i)
    cbd = _output_matrix(c_re[0], c_im[0])
    ar, ai = abar_r.reshape(1, STATE_W), abar_i.reshape(1, STATE_W)
    alr, ali = _state_power(ar, ai, T // 8)
    dvec = d_skip.reshape(1, SSM_W)
    wb16, cbd16 = wb.astype(BF16), cbd.astype(BF16)

    h1, a1a, a3a, n1 = ffn_fwd(x2, g_ffn1, full["w1_a"], full["w3_a"], full["w2_a"], "ffn_a_fwd")
    u, qkv, s_in, fz, cum = mixin_fwd(h1, g_mix, w_in_r, b_f_pad)
    q_aug, k_aug, v_aug = heads_in(qkv, cum)
    o_heads, q_bwd = attn_fwd(q_aug, k_aug, v_aug)
    s_perm = _permute_time(s_in)
    y_perm, xs = ssm_fwd(s_perm, wb16, cbd16, ar, ai, alr, ali, dvec)
    ypre = _unpermute_time(y_perm)
    h2, mixed = mixout_fwd(h1, o_heads, ypre, g_attn_out, g_ssm_out, full["w_glu"], b_glu, full["w_out"])
    h3, a1b, a3b, n2 = ffn_fwd(h2, g_ffn2, full["w1_b"], full["w3_b"], full["w2_b"], "ffn_b_fwd")

    dh3, n3, dzg, dpp, loss_part, dg_ple, dg_final = head_fwd_bwd(
        h3, p2, tgt, g_ple, g_final.reshape(1, D_MODEL), full["w_ple_gate"], full["w_ple_proj"])
    grads = {"g_ple": dg_ple, "g_final": dg_final.reshape(D_MODEL)}
    grads["w_ple_gate"] = mm_tn(n3, dzg, "dw_ple_gate")
    grads["w_ple_proj"] = mm_tn(p2, dpp, "dw_ple_proj")

    dh2, da1, da3, act, grads["g_ffn2"] = ffn_bwd(h2, g_ffn2, dh3, a1b, a3b, full["w1_b"], full["w3_b"], full["w2_b"], "ffn_b_bwd")
    grads["w1_b"] = mm_tn(da1, n2, "dw1_b")
    grads["w3_b"] = mm_tn(da3, n2, "dw3_b")
    grads["w2_b"] = mm_tn(act, dh3, "dw2_b", scale=0.5)

    seg = (jnp.arange(ATTN_W)[:, None] // HEAD_DIM == jnp.arange(128)[None, :]).astype(F32)
    do_aug, dypre, dpre, yg, grads["g_attn_out"], grads["g_ssm_out"], grads["b_glu"] = mixout_bwd(
        dh2, o_heads, ypre, g_attn_out, g_ssm_out, full["w_glu"], b_glu, full["w_out"], seg)
    grads["w_out"] = mm_tn(mixed, dh2, "dw_out")
    grads["w_glu"] = mm_tn(yg, dpre, "dw_glu")

    dq_aug, dk_aug, dv_aug, dc_rows = attn_bwd(q_bwd, k_aug, v_aug, do_aug)
    dc = jnp.pad(dc_rows.reshape(N_HEADS, T).T, ((0, 0), (0, 128 - N_HEADS)))

    dy_perm = _permute_time(dypre)
    du_perm, gs, d_a, dd = ssm_bwd(dy_perm, s_perm, xs, cbd16.T, wb16.T, ar, ai, alr, ali, dvec)
    ds_in = _unpermute_time(du_perm)
    hg = N_GROUPS // 2
    d_in, d_out = [], []
    for part in range(2):
        ins, outs = [], []
        for half in range(2):
            states = (part * STATE_W + half * _HALF_ST, _HALF_ST)
            chans = (half * _HALF_CH, _HALF_CH)
            blk = mm_tn(s_perm, gs, f"dw_ssm_in_{part}{half}", a_cols=chans, b_cols=states)
            ins.append(jnp.einsum("ghgp->ghp", blk.reshape(hg, GROUP_CH, hg, N_STATE)))
            blk = mm_tn(xs, dy_perm, f"dw_ssm_out_{part}{half}", a_cols=states, b_cols=chans)
            outs.append(jnp.einsum("gpgh->gph", blk.reshape(hg, N_STATE, hg, GROUP_CH)))
        d_in.append(jnp.concatenate(ins, axis=0).transpose(0, 2, 1))
        d_out.append(jnp.concatenate(outs, axis=0).transpose(0, 2, 1))
    d_abar_r = jnp.sum(d_a[:, :STATE_W], axis=0).reshape(N_GROUPS, N_STATE)
    d_abar_i = jnp.sum(d_a[:, STATE_W:], axis=0).reshape(N_GROUPS, N_STATE)
    d_disc = disc_vjp((d_abar_r, d_abar_i, d_in[0], d_in[1]))
    for name, val in zip(("a_re", "a_im", "log_dt", "b_re", "b_im"), d_disc):
        grads[name] = val[None]
    grads["c_re"] = d_out[0][None]
    grads["c_im"] = -d_out[1][None]
    grads["d_skip"] = dd.reshape(1, N_GROUPS, GROUP_CH)

    dh1, dz, grads["g_mix"], dbf = mixin_bwd(dh2, h1, g_mix, w_in_r, dq_aug, dk_aug, dv_aug, ds_in, dc, fz)
    grads["b_f"] = dbf[:, :N_HEADS]
    d_w_in_r = mm_tn(u, dz, "dw_in")
    grads["w_in"] = jnp.concatenate([d_w_in_r[:, :ATTN_W] * QK_SCALE, d_w_in_r[:, ATTN_W:3 * ATTN_W],
                                     d_w_in_r[:, 3 * ATTN_W + SSM_W:3 * ATTN_W + SSM_W + N_HEADS],
                                     d_w_in_r[:, 3 * ATTN_W:3 * ATTN_W + SSM_W]], axis=1)

    dx, da1, da3, act, grads["g_ffn1"] = ffn_bwd(x2, g_ffn1, dh1, a1a, a3a, full["w1_a"], full["w3_a"], full["w2_a"], "ffn_a_bwd")
    grads["w1_a"] = mm_tn(da1, n1, "dw1_a")
    grads["w3_a"] = mm_tn(da3, n1, "dw3_a")
    grads["w2_a"] = mm_tn(act, dh1, "dw2_a", scale=0.5)
    return loss_part, dx, grads


def _reduce_and_update(weights, moms, vars_, loss_part, dx, grads):
    core = lax.axis_index("c").astype(jnp.int32).reshape(1)
    chip = (2 * lax.axis_index("x") + lax.axis_index("y")).astype(jnp.int32).reshape(1)
    packed = _pack_full_grads(grads)
    pair = pair_sum(packed, sibling_split(packed), core)
    half = chip_sum(pair, chip_exchange(pair), chip)
    g_stored = _unpack_shards(join_halves(half, sibling_swap(half), core))
    g_out, d_out, m_out, v_out = {}, {}, {}, {}
    for n, _, _ in BIG:
        d, m, v = adamw(_stored(n, weights[n]), g_stored[n], _stored(n, moms[n]), _stored(n, vars_[n]), "adamw_" + n)
        g_out[n], d_out[n], m_out[n], v_out[n] = (_restored(n, a) for a in (g_stored[n], d, m, v))

    small = allreduce_small(_pack_small({n: grads[n] for n, _ in SMALL}, extra=loss_part[0, 0]))
    d_small, m_small, v_small = adamw(_pack_small(weights), small, _pack_small(moms), _pack_small(vars_), "adamw_small")

    g_small, loss = _unpack_small(small)
    g_out.update(g_small)
    outs = []
    for big, sm in ((d_out, d_small), (m_out, m_small), (v_out, v_small)):
        o, _ = _unpack_small(sm)
        o.update(big)
        outs.append(o)
    result = [loss, dx[None]] + [g_out[n] for n in WEIGHT_ORDER]
    for o in outs:
        result += [o[n] for n in WEIGHT_ORDER]
    return tuple(result)
```

```python
import functools
import math

import jax
import jax.numpy as jnp
from jax import lax
from jax.experimental import pallas as pl
from jax.experimental.pallas import tpu as pltpu

F32 = jnp.float32
BF16 = jnp.bfloat16

D_MODEL = 1024
D_FF = 2816
N_HEADS = 8
HEAD_DIM = 64
ATTN_W = 512
SSM_W = 512
N_GROUPS = 32
N_STATE = 64
GROUP_CH = 16
STATE_W = N_GROUPS * N_STATE
Z_COLS = 2176
QK_SCALE = 0.125
EPS = 1e-6

ADAM_LR = 0.001
ADAM_B1 = 0.9
ADAM_B2 = 0.999
ADAM_EPS = 1e-08
ADAM_WD = 0.01
ADAM_STEP = 10

TOKEN_TILE = 512
FFN_TOKEN_TILE = 256
FF_CHUNK = 1408
MM_K_TILE = 2048
ATTN_TILE = 512
SCAN_STEPS = 32
SCAN_LANES = 512
VMEM_LIMIT = 48 * 1024 * 1024
FFN_VMEM_LIMIT = 56 * 1024 * 1024
COPY_CHUNKS = 4

NT_DIMS = (((1,), (1,)), ((), ()))
TN_DIMS = (((0,), (0,)), ((), ()))
HIGHEST = lax.Precision.HIGHEST
MESH = pl.DeviceIdType.MESH

BIG = (
    ("w1_a", (1024, 704), 1), ("w3_a", (1024, 704), 1), ("w2_a", (704, 1024), 0),
    ("w_in", (1024, 514), 1), ("w_glu", (128, 512), 0), ("w_out", (256, 1024), 0),
    ("w1_b", (1024, 704), 1), ("w3_b", (1024, 704), 1), ("w2_b", (704, 1024), 0),
    ("w_ple_gate", (256, 1024), 0), ("w_ple_proj", (256, 256), 1),
)
PACK_COLS = 1024
PACK_ALIGN = 16
PACK_ROWS = 5408
SMALL = (
    ("g_ffn1", (1, 1024)), ("g_mix", (1, 1024)), ("b_f", (1, 8)), ("a_re", (1, 32, 64)), ("a_im", (1, 32, 64)),
    ("log_dt", (1, 32)), ("b_re", (1, 32, 64, 16)), ("b_im", (1, 32, 64, 16)), ("c_re", (1, 32, 16, 64)),
    ("c_im", (1, 32, 16, 64)), ("d_skip", (1, 32, 16)), ("b_glu", (1, 512)), ("g_attn_out", (1, 512)),
    ("g_ssm_out", (1, 512)), ("g_ffn2", (1, 1024)), ("g_ple", (1, 1024)), ("g_final", (1024,)),
)
SMALL_ROWS = 1152
WEIGHT_ORDER = ("g_ffn1", "w1_a", "w3_a", "w2_a", "g_mix", "w_in", "b_f", "a_re", "a_im", "log_dt", "b_re", "b_im",
                "c_re", "c_im", "d_skip", "w_glu", "b_glu", "g_attn_out", "g_ssm_out", "w_out", "g_ffn2", "w1_b",
                "w3_b", "w2_b", "g_ple", "w_ple_gate", "w_ple_proj", "g_final")


def _params(sem=None, vmem=VMEM_LIMIT):
    kw = dict(vmem_limit_bytes=vmem)
    if sem is not None:
        kw["dimension_semantics"] = sem
    return pltpu.CompilerParams(**kw)


def _sds(shape, dtype):
    return jax.ShapeDtypeStruct(shape, dtype)


def _tile(n, pref):
    t = min(n, pref)
    assert n % t == 0, (n, pref)
    return t


def _rms_scale(x):
    return lax.rsqrt(jnp.mean(x * x, axis=-1, keepdims=True) + EPS)


def _rms_bwd(dy, x, g):
    r = _rms_scale(x)
    xh = x * r
    dxh = dy * g
    dx = r * (dxh - xh * jnp.mean(dxh * xh, axis=-1, keepdims=True))
    return dx, jnp.sum(dy * xh, axis=0, keepdims=True)


def _dot(a, b):
    return jnp.dot(a, b, preferred_element_type=F32)


def _dot_nt(a, b):
    return lax.dot_general(a, b, NT_DIMS, preferred_element_type=F32)


def _dot_tn(a, b):
    return lax.dot_general(a, b, TN_DIMS, preferred_element_type=F32)


_GELU_C = math.sqrt(2.0 / math.pi)


def _gelu_parts(x):
    t = jnp.tanh(_GELU_C * (x + 0.044715 * x * x * x))
    return 0.5 * x * (1.0 + t), t


def _gelu_grad(x, t):
    return 0.5 * (1.0 + t) + 0.5 * x * (1.0 - t * t) * _GELU_C * (1.0 + 3.0 * 0.044715 * x * x)


def _resident(shape):
    return pl.BlockSpec(shape, lambda i: (0,) * len(shape), pipeline_mode=pl.Buffered(1))


def ffn_fwd(h, g, w1, w3, w2, name):
    T = h.shape[0]
    tm = _tile(T, FFN_TOKEN_TILE)

    def body(h_ref, g_ref, w1_ref, w3_ref, w2_ref, ho_ref, a1_ref, a3_ref, n_ref):
        x = h_ref[...]
        n = (x * _rms_scale(x) * g_ref[...]).astype(BF16)
        n_ref[...] = n
        out = x
        for lo in range(0, D_FF, FF_CHUNK):
            cols = slice(lo, lo + FF_CHUNK)
            a1 = _dot_nt(n, w1_ref[cols, :])
            a3 = _dot_nt(n, w3_ref[cols, :])
            a1_ref[:, cols] = a1.astype(BF16)
            a3_ref[:, cols] = a3.astype(BF16)
            act = (a1 * jax.nn.sigmoid(a1) * a3).astype(BF16)
            out = out + 0.5 * _dot(act, w2_ref[cols, :])
        ho_ref[...] = out

    tok = lambda i: (i, 0)
    return pl.pallas_call(
        body, name=name, grid=(T // tm,),
        in_specs=[pl.BlockSpec((tm, D_MODEL), tok), _resident((1, D_MODEL)), _resident((D_FF, D_MODEL)),
                  _resident((D_FF, D_MODEL)), _resident((D_FF, D_MODEL))],
        out_specs=[pl.BlockSpec((tm, D_MODEL), tok), pl.BlockSpec((tm, D_FF), tok), pl.BlockSpec((tm, D_FF), tok),
                   pl.BlockSpec((tm, D_MODEL), tok)],
        out_shape=[_sds((T, D_MODEL), F32), _sds((T, D_FF), BF16), _sds((T, D_FF), BF16), _sds((T, D_MODEL), BF16)],
        compiler_params=_params(("arbitrary",), FFN_VMEM_LIMIT),
    )(h, g, w1, w3, w2)


def ffn_bwd(h, g, dho, a1, a3, w1, w3, w2, name):
    T = h.shape[0]
    tm = _tile(T, FFN_TOKEN_TILE)

    def body(h_ref, g_ref, dho_ref, a1_ref, a3_ref, w1_ref, w3_ref, w2_ref, dhi_ref, da1_ref, da3_ref, act_ref, dg_ref):
        @pl.when(pl.program_id(0) == 0)
        def _():
            dg_ref[...] = jnp.zeros_like(dg_ref)

        dho = dho_ref[...]
        dhb = (0.5 * dho).astype(BF16)
        dn = None
        for lo in range(0, D_FF, FF_CHUNK):
            cols = slice(lo, lo + FF_CHUNK)
            a1v = a1_ref[:, cols].astype(F32)
            a3v = a3_ref[:, cols].astype(F32)
            s = jax.nn.sigmoid(a1v)
            sl = a1v * s
            dact = _dot_nt(dhb, w2_ref[cols, :])
            act_ref[:, cols] = (sl * a3v).astype(BF16)
            da1 = (dact * a3v * s * (1.0 + a1v * (1.0 - s))).astype(BF16)
            da3 = (dact * sl).astype(BF16)
            da1_ref[:, cols] = da1
            da3_ref[:, cols] = da3
            part = _dot(da1, w1_ref[cols, :]) + _dot(da3, w3_ref[cols, :])
            dn = part if dn is None else dn + part
        dx, dg = _rms_bwd(dn, h_ref[...], g_ref[...])
        dg_ref[...] += dg
        dhi_ref[...] = dho + dx

    tok = lambda i: (i, 0)
    return pl.pallas_call(
        body, name=name, grid=(T // tm,),
        in_specs=[pl.BlockSpec((tm, D_MODEL), tok), _resident((1, D_MODEL)), pl.BlockSpec((tm, D_MODEL), tok),
                  pl.BlockSpec((tm, D_FF), tok), pl.BlockSpec((tm, D_FF), tok), _resident((D_FF, D_MODEL)),
                  _resident((D_FF, D_MODEL)), _resident((D_FF, D_MODEL))],
        out_specs=[pl.BlockSpec((tm, D_MODEL), tok), pl.BlockSpec((tm, D_FF), tok), pl.BlockSpec((tm, D_FF), tok),
                   pl.BlockSpec((tm, D_FF), tok), pl.BlockSpec((1, D_MODEL), lambda i: (0, 0))],
        out_shape=[_sds((T, D_MODEL), F32), _sds((T, D_FF), BF16), _sds((T, D_FF), BF16), _sds((T, D_FF), BF16),
                   _sds((1, D_MODEL), F32)],
        compiler_params=_params(("arbitrary",), FFN_VMEM_LIMIT),
    )(h, g, dho, a1, a3, w1, w3, w2)


def mm_tn(a, b, name, scale=1.0, a_cols=None, b_cols=None):
    T = a.shape[0]
    a_off, M = a_cols or (0, a.shape[1])
    b_off, N = b_cols or (0, b.shape[1])
    bm = 512 if M % 512 == 0 else (1408 if M == 2816 else 256)
    bn = N if N in (2176, 1408) else (1408 if N == 2816 else min(N, 1024))
    tk = _tile(T, MM_K_TILE)
    row_bytes = 2 * (bm * a.dtype.itemsize + bn * b.dtype.itemsize)
    while tk > TOKEN_TILE and tk * row_bytes > VMEM_LIMIT // 3:
        tk //= 2
    assert M % bm == 0 and N % bn == 0 and T % tk == 0 and a_off % bm == 0 and b_off % bn == 0
    n_k = T // tk
    m0, n0 = a_off // bm, b_off // bn

    def body(a_ref, b_ref, o_ref):
        k = pl.program_id(2)

        @pl.when(k == 0)
        def _():
            o_ref[...] = jnp.zeros_like(o_ref)

        o_ref[...] += _dot_tn(a_ref[...].astype(BF16), b_ref[...].astype(BF16))

        if scale != 1.0:
            @pl.when(k == n_k - 1)
            def _():
                o_ref[...] = o_ref[...] * scale

    return pl.pallas_call(
        body, name=name, grid=(M // bm, N // bn, n_k),
        in_specs=[pl.BlockSpec((tk, bm), lambda m, n, k: (k, m0 + m)), pl.BlockSpec((tk, bn), lambda m, n, k: (k, n0 + n))],
        out_specs=pl.BlockSpec((bm, bn), lambda m, n, k: (m, n)),
        out_shape=_sds((M, N), F32),
        compiler_params=_params(("arbitrary", "arbitrary", "arbitrary")),
    )(a, b)


def mixin_fwd(h1, g, w_in_r, b_f_pad):
    T = h1.shape[0]
    tm = _tile(T, TOKEN_TILE)

    def body(h_ref, g_ref, w_ref, bf_ref, u_ref, qkv_ref, s_ref, fz_ref, c_ref, carry):
        @pl.when(pl.program_id(0) == 0)
        def _():
            carry[...] = jnp.zeros_like(carry)

        x = h_ref[...]
        u = (x * _rms_scale(x) * g_ref[...]).astype(BF16)
        u_ref[...] = u
        z = _dot(u, w_ref[...])
        qkv_ref[...] = z[:, :3 * ATTN_W].astype(BF16)
        s_ref[...] = z[:, 3 * ATTN_W:3 * ATTN_W + SSM_W]
        fz = z[:, 3 * ATTN_W + SSM_W:] + bf_ref[...]
        fz_ref[...] = fz
        lane = lax.broadcasted_iota(jnp.int32, fz.shape, 1)
        logf = jnp.where(lane < N_HEADS, jnp.minimum(fz, 0.0) - jnp.log(1.0 + jnp.exp(-jnp.abs(fz))), 0.0)
        row = lax.broadcasted_iota(jnp.int32, (tm, tm), 0)
        col = lax.broadcasted_iota(jnp.int32, (tm, tm), 1)
        tri = (col <= row).astype(F32)
        cs = jnp.dot(tri, logf, precision=HIGHEST, preferred_element_type=F32) + carry[0:1, :]
        c_ref[...] = cs
        carry[...] = jnp.broadcast_to(cs[tm - 1:tm, :], carry.shape)

    tok = lambda i: (i, 0)
    fix = lambda i: (0, 0)
    return pl.pallas_call(
        body, name="mixin_fwd", grid=(T // tm,),
        in_specs=[pl.BlockSpec((tm, D_MODEL), tok), pl.BlockSpec((1, D_MODEL), fix),
                  pl.BlockSpec((D_MODEL, Z_COLS), fix), pl.BlockSpec((1, 128), fix)],
        out_specs=[pl.BlockSpec((tm, D_MODEL), tok), pl.BlockSpec((tm, 3 * ATTN_W), tok), pl.BlockSpec((tm, SSM_W), tok),
                   pl.BlockSpec((tm, 128), tok), pl.BlockSpec((tm, 128), tok)],
        out_shape=[_sds((T, D_MODEL), BF16), _sds((T, 3 * ATTN_W), BF16), _sds((T, SSM_W), F32),
                   _sds((T, 128), F32), _sds((T, 128), F32)],
        scratch_shapes=[pltpu.VMEM((8, 128), F32)],
        compiler_params=_params(("arbitrary",)),
    )(h1, g, w_in_r, b_f_pad)


def mixin_bwd(dh2, h1, g, w_in_r, dq, dk, dv, ds, dc, fz):
    T = h1.shape[0]
    tm = _tile(T, TOKEN_TILE)
    n_t = T // tm

    def body(dh2_ref, h_ref, g_ref, w_ref, dq_ref, dk_ref, dv_ref, ds_ref, dc_ref, fz_ref,
             dh1_ref, dz_ref, dg_ref, dbf_ref, carry):
        @pl.when(pl.program_id(0) == 0)
        def _():
            carry[...] = jnp.zeros_like(carry)
            dg_ref[...] = jnp.zeros_like(dg_ref)
            dbf_ref[...] = jnp.zeros_like(dbf_ref)

        row = lax.broadcasted_iota(jnp.int32, (tm, tm), 0)
        col = lax.broadcasted_iota(jnp.int32, (tm, tm), 1)
        tri = (col >= row).astype(F32)
        dlogf = jnp.dot(tri, dc_ref[...], precision=HIGHEST, preferred_element_type=F32) + carry[0:1, :]
        carry[...] = jnp.broadcast_to(dlogf[0:1, :], carry.shape)
        dfz = dlogf * jax.nn.sigmoid(-fz_ref[...])
        dbf_ref[...] += jnp.sum(dfz, axis=0, keepdims=True)
        dz = jnp.concatenate([_join_heads(dq_ref, BF16), _join_heads(dk_ref, BF16), _join_heads(dv_ref, BF16),
                              ds_ref[...], dfz], axis=1).astype(BF16)
        dz_ref[...] = dz
        du = _dot_nt(dz, w_ref[...])
        dx, dg = _rms_bwd(du, h_ref[...], g_ref[...])
        dg_ref[...] += dg
        dh1_ref[...] = dh2_ref[...] + dx

    tok = lambda i: (n_t - 1 - i, 0)
    fix = lambda i: (0, 0)
    heads = pl.BlockSpec((N_HEADS, tm, 128), lambda i: (0, n_t - 1 - i, 0))
    return pl.pallas_call(
        body, name="mixin_bwd", grid=(n_t,),
        in_specs=[pl.BlockSpec((tm, D_MODEL), tok), pl.BlockSpec((tm, D_MODEL), tok), pl.BlockSpec((1, D_MODEL), fix),
                  pl.BlockSpec((D_MODEL, Z_COLS), fix), heads, heads, heads, pl.BlockSpec((tm, SSM_W), tok),
                  pl.BlockSpec((tm, 128), tok), pl.BlockSpec((tm, 128), tok)],
        out_specs=[pl.BlockSpec((tm, D_MODEL), tok), pl.BlockSpec((tm, Z_COLS), tok), pl.BlockSpec((1, D_MODEL), fix),
                   pl.BlockSpec((1, 128), fix)],
        out_shape=[_sds((T, D_MODEL), F32), _sds((T, Z_COLS), BF16), _sds((1, D_MODEL), F32), _sds((1, 128), F32)],
        scratch_shapes=[pltpu.VMEM((8, 128), F32)],
        compiler_params=_params(("arbitrary",)),
    )(dh2, h1, g, w_in_r, dq, dk, dv, ds, dc, fz)


def _lane_move(src_lo, dst_lo, width, dtype):
    r = lax.broadcasted_iota(jnp.int32, (128, 128), 0)
    c = lax.broadcasted_iota(jnp.int32, (128, 128), 1)
    return ((c - dst_lo == r - src_lo) & (r >= src_lo) & (r < src_lo + width)).astype(dtype)


def _lane_const(lo, width, value):
    lane = lax.broadcasted_iota(jnp.int32, (1, 128), 1)
    return jnp.where((lane >= lo) & (lane < lo + width), value, 0.0).astype(F32)


def _pieces(a):
    hi = a.astype(BF16)
    rest = a - hi.astype(F32)
    mid = rest.astype(BF16)
    return hi, mid, (rest - mid.astype(F32)).astype(BF16)


def _head_features(pair_block, e):
    return _dot(pair_block, _lane_move(HEAD_DIM * e, 0, HEAD_DIM, BF16))


def _helper_columns(pieces, head, sign):
    out = None
    for k, piece in enumerate(pieces):
        term = _dot(piece, _lane_move(head, HEAD_DIM + k, 1, BF16))
        out = term if out is None else out + term
    return sign * out


def heads_in(qkv, cum):
    T = qkv.shape[0]
    tm = _tile(T, TOKEN_TILE)

    def body(qkv_ref, c_ref, q_ref, k_ref, v_ref):
        c = _pieces(c_ref[...])
        for h in range(N_HEADS):
            p, e = divmod(h, 2)
            blk = lambda base: qkv_ref[:, base + 128 * p:base + 128 * (p + 1)]
            q_ref[h] = (_head_features(blk(0), e) + _lane_const(HEAD_DIM, 3, -1.0)).astype(BF16)
            k_ref[h] = (_head_features(blk(ATTN_W), e) + _helper_columns(c, h, 1.0)
                        + _lane_const(HEAD_DIM + 3, 3, 1.0)).astype(BF16)
            v_ref[h] = (_head_features(blk(2 * ATTN_W), e) + _lane_const(HEAD_DIM, 3, 1.0)).astype(BF16)

    tok = lambda i: (i, 0)
    heads = pl.BlockSpec((N_HEADS, tm, 128), lambda i: (0, i, 0))
    return pl.pallas_call(
        body, name="heads_in", grid=(T // tm,),
        in_specs=[pl.BlockSpec((tm, 3 * ATTN_W), tok), pl.BlockSpec((tm, 128), tok)],
        out_specs=[heads] * 3, out_shape=[_sds((N_HEADS, T, 128), BF16)] * 3,
        compiler_params=_params(("arbitrary",)),
    )(qkv, cum)


def attn_fwd(q_aug, k_aug, v_aug):
    H, T, wd = q_aug.shape
    hd = HEAD_DIM
    tq = _tile(T, ATTN_TILE)
    n = T // tq

    def body(q_ref, k_ref, v_ref, o_ref, qb_ref, m_sc, acc, s_even, s_odd):
        qi = pl.program_id(1)
        qv = q_ref[0]
        m_sc[...] = jnp.full_like(m_sc, -jnp.inf)
        acc[...] = jnp.zeros_like(acc)

        def key_rows(j):
            return pl.ds(pl.multiple_of(jnp.minimum(j, qi) * tq, tq), tq)

        def logits(j, buf):
            buf[...] = _dot_nt(k_ref[0, key_rows(j), :], qv)

        def update(j, buf, masked):
            st = buf[...]
            if masked:
                keep = lax.broadcasted_iota(jnp.int32, (tq, tq), 0) <= lax.broadcasted_iota(jnp.int32, (tq, tq), 1)
                st = jnp.where(keep, st, -1e30)
            m_old = m_sc[...]
            m_new = jnp.maximum(m_old, jnp.max(st, axis=0, keepdims=True))
            pt = jnp.exp(st - m_new).astype(BF16)
            acc[...] = jnp.exp(m_old - m_new) * acc[...] + _dot_tn(v_ref[0, key_rows(j), :], pt)
            m_sc[...] = m_new

        logits(0, s_even)

        def two_tiles(p, carry):
            j = 2 * p
            logits(j + 1, s_odd)
            update(j, s_even, False)
            logits(j + 2, s_even)
            update(j + 1, s_odd, False)
            return carry

        lax.fori_loop(0, qi // 2, two_tiles, 0)

        @pl.when(qi % 2 == 0)
        def _():
            update(qi, s_even, True)

        @pl.when(qi % 2 == 1)
        def _():
            logits(qi, s_odd)
            update(qi - 1, s_even, False)
            update(qi, s_odd, True)

        total = acc[hd:hd + 1, :]
        o_ref[0] = (acc[...] / total).T
        hi, mid, lo = (t.astype(F32) for t in _pieces(-(m_sc[...] + jnp.log(total))))
        row = lax.broadcasted_iota(jnp.int32, (wd, tq), 0)
        lse_rows = jnp.where(row == hd + 3, hi, jnp.where(row == hd + 4, mid, jnp.where(row == hd + 5, lo, 0.0)))
        qb_ref[0] = (qv.astype(F32) + lse_rows.T).astype(BF16)

    qmap = lambda h, i: (h, i, 0)
    head = lambda h, i: (h, 0, 0)
    return pl.pallas_call(
        body, name="attn_fwd", grid=(H, n),
        in_specs=[pl.BlockSpec((1, tq, wd), qmap), pl.BlockSpec((1, T, wd), head), pl.BlockSpec((1, T, wd), head)],
        out_specs=[pl.BlockSpec((1, tq, wd), qmap), pl.BlockSpec((1, tq, wd), qmap)],
        out_shape=[_sds((H, T, wd), F32), _sds((H, T, wd), BF16)],
        scratch_shapes=[pltpu.VMEM((1, tq), F32), pltpu.VMEM((wd, tq), F32), pltpu.VMEM((tq, tq), F32),
                        pltpu.VMEM((tq, tq), F32)],
        compiler_params=_params(("arbitrary", "arbitrary")),
    )(q_aug, k_aug, v_aug)


def attn_bwd(q_aug, k_aug, v_aug, do_aug):
    H, T, wd = q_aug.shape
    tq = _tile(T, ATTN_TILE)
    n = T // tq

    def body(q_ref, do_ref, k_ref, v_ref, dq_ref, dk_ref, dv_ref, dc_ref, dck, s_a, d_a, s_b, d_b):
        j = pl.program_id(1)

        @pl.when(j == 0)
        def _():
            dq_ref[...] = jnp.zeros_like(dq_ref)
            dc_ref[...] = jnp.zeros_like(dc_ref)

        dk_ref[...] = jnp.zeros_like(dk_ref)
        dv_ref[...] = jnp.zeros_like(dv_ref)
        dck[...] = jnp.zeros_like(dck)
        kv, vv = k_ref[0], v_ref[0]

        def query_rows(i):
            return pl.ds(pl.multiple_of(jnp.minimum(i, n - 1) * tq, tq), tq)

        def products(i, s_buf, d_buf):
            rows = query_rows(i)
            s_buf[...] = _dot_nt(kv, q_ref[0, rows, :])
            d_buf[...] = _dot_nt(vv, do_ref[0, rows, :])

        def update(i, s_buf, d_buf, masked):
            rows = query_rows(i)
            qv, dov = q_ref[0, rows, :], do_ref[0, rows, :]
            pt = jnp.exp(s_buf[...])
            if masked:
                keep = lax.broadcasted_iota(jnp.int32, (tq, tq), 0) <= lax.broadcasted_iota(jnp.int32, (tq, tq), 1)
                pt = jnp.where(keep, pt, 0.0)
            dv_ref[0] += _dot(pt.astype(BF16), dov)
            dst = pt * d_buf[...]
            dsb = dst.astype(BF16)
            dk_ref[0] += _dot(dsb, qv)
            dq_ref[0, rows, :] += _dot_tn(dsb, kv)
            dck[...] += jnp.sum(dst, axis=1, keepdims=True)
            dc_ref[0, pl.ds(i, 1), :] += jnp.sum(dst, axis=0, keepdims=True)

        products(j, s_a, d_a)
        products(j + 1, s_b, d_b)
        update(j, s_a, d_a, True)
        left = n - 1 - j

        def two_tiles(p, carry):
            i = j + 1 + 2 * p
            products(i + 1, s_a, d_a)
            update(i, s_b, d_b, False)
            products(i + 2, s_b, d_b)
            update(i + 1, s_a, d_a, False)
            return carry

        lax.fori_loop(0, left // 2, two_tiles, 0)

        @pl.when(left % 2 == 1)
        def _():
            update(n - 1, s_b, d_b, False)

        dc_ref[0, pl.ds(j, 1), :] -= jnp.broadcast_to(dck[...], (tq, 128)).T[0:1, :]

    head = lambda h, j: (h, 0, 0)
    kmap = lambda h, j: (h, j, 0)
    return pl.pallas_call(
        body, name="attn_bwd", grid=(H, n),
        in_specs=[pl.BlockSpec((1, T, wd), head), pl.BlockSpec((1, T, wd), head), pl.BlockSpec((1, tq, wd), kmap),
                  pl.BlockSpec((1, tq, wd), kmap)],
        out_specs=[pl.BlockSpec((1, T, wd), head), pl.BlockSpec((1, tq, wd), kmap), pl.BlockSpec((1, tq, wd), kmap),
                   pl.BlockSpec((1, n, tq), head)],
        out_shape=[_sds((H, T, wd), F32), _sds((H, T, wd), F32), _sds((H, T, wd), F32), _sds((H, n, tq), F32)],
        scratch_shapes=[pltpu.VMEM((tq, 1), F32)] + [pltpu.VMEM((tq, tq), F32)] * 4,
        compiler_params=_params(("arbitrary", "arbitrary")),
    )(q_aug, do_aug, k_aug, v_aug)


def _complex_step(a_r, a_i, cr, ci, br, bi):
    return a_r * cr - a_i * ci + br, a_r * ci + a_i * cr + bi


_HALF_CH = SSM_W // 2
_HALF_ST = STATE_W // 2


def _state_cols(part, half):
    lo = part * STATE_W + half * _HALF_ST
    return slice(lo, lo + _HALF_ST)


def _channels_to_states(x, w_ref, out_ref):
    for half in range(2):
        ch = slice(half * _HALF_CH, (half + 1) * _HALF_CH)
        for part in range(2):
            cols = _state_cols(part, half)
            out_ref[:, cols] = _dot(x[:, ch], w_ref[ch, cols])


def _states_to_channels(x, w_ref):
    halves = []
    for half in range(2):
        ch = slice(half * _HALF_CH, (half + 1) * _HALF_CH)
        halves.append(_dot(x[:, _state_cols(0, half)], w_ref[_state_cols(0, half), ch])
                      + _dot(x[:, _state_cols(1, half)], w_ref[_state_cols(1, half), ch]))
    return jnp.concatenate(halves, axis=1)


def ssm_fwd(s_perm, wb, cbd, a_r, a_i, al_r, al_i, dvec):
    T = s_perm.shape[0]
    chunk = T // 8
    ts = _tile(chunk, SCAN_STEPS)
    tr, n_s = ts * 8, chunk // ts
    W, LB = STATE_W, SCAN_LANES

    def body(s_ref, wb_ref, cbd_ref, ar_ref, ai_ref, alr_ref, ali_ref, dv_ref, y_ref, xs_ref, bu, carry):
        ph, i = pl.program_id(0), pl.program_id(1)

        @pl.when((ph == 0) & (i == 0))
        def _():
            carry[...] = jnp.zeros_like(carry)

        _channels_to_states(s_ref[...].astype(BF16), wb_ref, bu)

        def scan(store):
            for lb in range(W // LB):
                lo = lb * LB
                re, im = slice(lo, lo + LB), slice(W + lo, W + lo + LB)
                ar = jnp.broadcast_to(ar_ref[:, re], (8, LB))
                ai = jnp.broadcast_to(ai_ref[:, re], (8, LB))

                def step(s, c):
                    rows = pl.ds(pl.multiple_of(s * 8, 8), 8)
                    nr, ni = _complex_step(ar, ai, c[0], c[1], bu[rows, re], bu[rows, im])
                    if store:
                        bu[rows, re] = nr
                        bu[rows, im] = ni
                    return nr, ni

                cr, ci = lax.fori_loop(0, ts, step, (carry[:, re], carry[:, im]), unroll=2)
                carry[:, re] = cr
                carry[:, im] = ci

        @pl.when(ph == 0)
        def _():
            scan(False)

            @pl.when(i == n_s - 1)
            def _():
                er, ei = carry[:, :W], carry[:, W:]
                alr = jnp.broadcast_to(alr_ref[...], (8, W))
                ali = jnp.broadcast_to(ali_ref[...], (8, W))
                first = lax.broadcasted_iota(jnp.int32, (8, W), 0) == 0
                sr, si = jnp.zeros((8, W), F32), jnp.zeros((8, W), F32)
                for _ in range(7):
                    vr, vi = _complex_step(alr, ali, sr, si, er, ei)
                    sr = jnp.where(first, 0.0, pltpu.roll(vr, 1, 0))
                    si = jnp.where(first, 0.0, pltpu.roll(vi, 1, 0))
                carry[:, :W] = sr
                carry[:, W:] = si

        @pl.when(ph == 1)
        def _():
            scan(True)
            xb = bu[...].astype(BF16)
            xs_ref[...] = xb
            y_ref[...] = _states_to_channels(xb, cbd_ref) + s_ref[...] * dv_ref[...]

    fix = lambda p, i: (0, 0)
    return pl.pallas_call(
        body, name="ssm_fwd", grid=(2, n_s),
        in_specs=[pl.BlockSpec((tr, SSM_W), lambda p, i: (i, 0)), pl.BlockSpec((SSM_W, 2 * W), fix),
                  pl.BlockSpec((2 * W, SSM_W), fix), pl.BlockSpec((1, W), fix), pl.BlockSpec((1, W), fix),
                  pl.BlockSpec((1, W), fix), pl.BlockSpec((1, W), fix), pl.BlockSpec((1, SSM_W), fix)],
        out_specs=[pl.BlockSpec((tr, SSM_W), lambda p, i: (i * p, 0)), pl.BlockSpec((tr, 2 * W), lambda p, i: (i * p, 0))],
        out_shape=[_sds((T, SSM_W), F32), _sds((T, 2 * W), BF16)],
        scratch_shapes=[pltpu.VMEM((tr, 2 * W), F32), pltpu.VMEM((8, 2 * W), F32)],
        compiler_params=_params(("arbitrary", "arbitrary")),
    )(s_perm, wb, cbd, a_r, a_i, al_r, al_i, dvec)


def ssm_bwd(dy_perm, s_perm, xs, cbd_t, wb_t, a_r, a_i, al_r, al_i, dvec):
    T = s_perm.shape[0]
    chunk = T // 8
    ts = _tile(chunk, SCAN_STEPS)
    tr, n_s = ts * 8, chunk // ts
    W, LB = STATE_W, SCAN_LANES

    def body(dy_ref, s_ref, xs_ref, cbt_ref, wbt_ref, ar_ref, ai_ref, alr_ref, ali_ref, dv_ref,
             du_ref, gs_ref, da_ref, dd_ref, gd, x32, carry):
        ph, i = pl.program_id(0), pl.program_id(1)

        @pl.when((ph == 0) & (i == 0))
        def _():
            carry[...] = jnp.zeros_like(carry)
            da_ref[...] = jnp.zeros_like(da_ref)
            dd_ref[...] = jnp.zeros_like(dd_ref)

        _channels_to_states(dy_ref[...].astype(BF16), cbt_ref, gd)

        def scan(store):
            for lb in range(W // LB):
                lo = lb * LB
                re, im = slice(lo, lo + LB), slice(W + lo, W + lo + LB)
                ar = jnp.broadcast_to(ar_ref[:, re], (8, LB))
                nai = -jnp.broadcast_to(ai_ref[:, re], (8, LB))

                def step(k, c):
                    rows = pl.ds(pl.multiple_of((ts - 1 - k) * 8, 8), 8)
                    cr, ci = c[0], c[1]
                    nr, ni = _complex_step(ar, nai, cr, ci, gd[rows, re], gd[rows, im])
                    if store:
                        xr, xi = x32[rows, re], x32[rows, im]
                        gd[rows, re] = nr
                        gd[rows, im] = ni
                        return nr, ni, c[2] + cr * xr + ci * xi, c[3] + ci * xr - cr * xi
                    return nr, ni

                init = (carry[:, re], carry[:, im])
                if store:
                    init = init + (da_ref[:, re], da_ref[:, im])
                out = lax.fori_loop(0, ts, step, init, unroll=2)
                carry[:, re] = out[0]
                carry[:, im] = out[1]
                if store:
                    da_ref[:, re] = out[2]
                    da_ref[:, im] = out[3]

        @pl.when(ph == 0)
        def _():
            scan(False)

            @pl.when(i == n_s - 1)
            def _():
                er, ei = carry[:, :W], carry[:, W:]
                alr = jnp.broadcast_to(alr_ref[...], (8, W))
                nali = -jnp.broadcast_to(ali_ref[...], (8, W))
                last = lax.broadcasted_iota(jnp.int32, (8, W), 0) == 7
                rr, ri = jnp.zeros((8, W), F32), jnp.zeros((8, W), F32)
                for _ in range(7):
                    vr, vi = _complex_step(alr, nali, rr, ri, er, ei)
                    rr = jnp.where(last, 0.0, pltpu.roll(vr, 7, 0))
                    ri = jnp.where(last, 0.0, pltpu.roll(vi, 7, 0))
                carry[:, :W] = rr
                carry[:, W:] = ri

        @pl.when(ph == 1)
        def _():
            x32[...] = xs_ref[...].astype(F32)
            scan(True)
            gb = gd[...].astype(BF16)
            gs_ref[...] = gb
            dy = dy_ref[...]
            du_ref[...] = _states_to_channels(gb, wbt_ref) + dy * dv_ref[...]
            dd_ref[...] += jnp.sum(dy * s_ref[...], axis=0, keepdims=True)

    fix = lambda p, i: (0, 0)
    rev = lambda p, i: (n_s - 1 - i, 0)
    rev_out = lambda p, i: (n_s - 1 - i * p, 0)
    return pl.pallas_call(
        body, name="ssm_bwd", grid=(2, n_s),
        in_specs=[pl.BlockSpec((tr, SSM_W), rev), pl.BlockSpec((tr, SSM_W), rev), pl.BlockSpec((tr, 2 * W), rev),
                  pl.BlockSpec((SSM_W, 2 * W), fix), pl.BlockSpec((2 * W, SSM_W), fix), pl.BlockSpec((1, W), fix),
                  pl.BlockSpec((1, W), fix), pl.BlockSpec((1, W), fix), pl.BlockSpec((1, W), fix),
                  pl.BlockSpec((1, SSM_W), fix)],
        out_specs=[pl.BlockSpec((tr, SSM_W), rev_out), pl.BlockSpec((tr, 2 * W), rev_out),
                   pl.BlockSpec((8, 2 * W), fix), pl.BlockSpec((1, SSM_W), fix)],
        out_shape=[_sds((T, SSM_W), F32), _sds((T, 2 * W), BF16), _sds((8, 2 * W), F32), _sds((1, SSM_W), F32)],
        scratch_shapes=[pltpu.VMEM((tr, 2 * W), F32), pltpu.VMEM((tr, 2 * W), F32), pltpu.VMEM((8, 2 * W), F32)],
        compiler_params=_params(("arbitrary", "arbitrary")),
    )(dy_perm, s_perm, xs, cbd_t, wb_t, a_r, a_i, al_r, al_i, dvec)


def _join_heads(ref, dtype):
    def move(h, dst):
        x = ref[h]
        pieces = _pieces(x) if dtype == F32 else (x.astype(BF16),)
        out = None
        for piece in pieces:
            term = _dot(piece, _lane_move(0, dst, HEAD_DIM, BF16))
            out = term if out is None else out + term
        return out

    return jnp.concatenate([move(2 * p, 0) + move(2 * p + 1, HEAD_DIM) for p in range(N_HEADS // 2)], axis=1)


def mixout_fwd(h1, o_heads, ypre, g_a, g_s, w_glu, b_glu, w_out):
    T = h1.shape[0]
    tm = _tile(T, TOKEN_TILE)

    def body(h_ref, at_ref, yp_ref, ga_ref, gs_ref, wg_ref, bg_ref, wo_ref, h2_ref, mixed_ref):
        yg, _ = _gelu_parts(yp_ref[...])
        gl = yg * jax.nn.sigmoid(_dot(yg.astype(BF16), wg_ref[...]) + bg_ref[...])
        at = _join_heads(at_ref, F32)
        mixed = jnp.concatenate([at * _rms_scale(at) * ga_ref[...], gl * _rms_scale(gl) * gs_ref[...]], axis=1)
        mixed = mixed.astype(BF16)
        mixed_ref[...] = mixed
        h2_ref[...] = h_ref[...] + _dot(mixed, wo_ref[...])

    tok = lambda i: (i, 0)
    fix = lambda i: (0, 0)
    return pl.pallas_call(
        body, name="mixout_fwd", grid=(T // tm,),
        in_specs=[pl.BlockSpec((tm, D_MODEL), tok), pl.BlockSpec((N_HEADS, tm, 128), lambda i: (0, i, 0)),
                  pl.BlockSpec((tm, SSM_W), tok),
                  pl.BlockSpec((1, ATTN_W), fix), pl.BlockSpec((1, SSM_W), fix), pl.BlockSpec((SSM_W, SSM_W), fix),
                  pl.BlockSpec((1, SSM_W), fix), pl.BlockSpec((D_MODEL, D_MODEL), fix)],
        out_specs=[pl.BlockSpec((tm, D_MODEL), tok), pl.BlockSpec((tm, D_MODEL), tok)],
        out_shape=[_sds((T, D_MODEL), F32), _sds((T, D_MODEL), BF16)],
        compiler_params=_params(("arbitrary",)),
    )(h1, o_heads, ypre, g_a, g_s, w_glu, b_glu, w_out)


def mixout_bwd(dh2, o_heads, ypre, g_a, g_s, w_glu, b_glu, w_out, seg):
    T = dh2.shape[0]
    tm = _tile(T, TOKEN_TILE)

    def body(dh_ref, at_ref, yp_ref, ga_ref, gs_ref, wg_ref, bg_ref, wo_ref, seg_ref,
             do_ref, dyp_ref, dpre_ref, yg_ref, dga_ref, dgs_ref, dbg_ref):
        @pl.when(pl.program_id(0) == 0)
        def _():
            dga_ref[...] = jnp.zeros_like(dga_ref)
            dgs_ref[...] = jnp.zeros_like(dgs_ref)
            dbg_ref[...] = jnp.zeros_like(dbg_ref)

        dmix = _dot_nt(dh_ref[...].astype(BF16), wo_ref[...])
        at = _join_heads(at_ref, F32)
        dat, dga = _rms_bwd(dmix[:, :ATTN_W], at, ga_ref[...])
        dga_ref[...] += dga
        delta = _pieces(jnp.dot(dat * at, seg_ref[...], precision=HIGHEST, preferred_element_type=F32))
        datb = dat.astype(BF16)
        for h in range(N_HEADS):
            p, e = divmod(h, 2)
            do_ref[h] = (_head_features(datb[:, 128 * p:128 * (p + 1)], e) + _helper_columns(delta, h, -1.0)).astype(BF16)
        yp = yp_ref[...]
        yg, t = _gelu_parts(yp)
        ygb = yg.astype(BF16)
        yg_ref[...] = ygb
        sg = jax.nn.sigmoid(_dot(ygb, wg_ref[...]) + bg_ref[...])
        dgl, dgs = _rms_bwd(dmix[:, ATTN_W:], yg * sg, gs_ref[...])
        dgs_ref[...] += dgs
        dpre = dgl * yg * sg * (1.0 - sg)
        dbg_ref[...] += jnp.sum(dpre, axis=0, keepdims=True)
        dpb = dpre.astype(BF16)
        dpre_ref[...] = dpb
        dyg = dgl * sg + _dot_nt(dpb, wg_ref[...])
        dyp_ref[...] = dyg * _gelu_grad(yp, t)

    tok = lambda i: (i, 0)
    fix = lambda i: (0, 0)
    heads = pl.BlockSpec((N_HEADS, tm, 128), lambda i: (0, i, 0))
    return pl.pallas_call(
        body, name="mixout_bwd", grid=(T // tm,),
        in_specs=[pl.BlockSpec((tm, D_MODEL), tok), heads, pl.BlockSpec((tm, SSM_W), tok),
                  pl.BlockSpec((1, ATTN_W), fix), pl.BlockSpec((1, SSM_W), fix), pl.BlockSpec((SSM_W, SSM_W), fix),
                  pl.BlockSpec((1, SSM_W), fix), pl.BlockSpec((D_MODEL, D_MODEL), fix), pl.BlockSpec((ATTN_W, 128), fix)],
        out_specs=[heads, pl.BlockSpec((tm, SSM_W), tok), pl.BlockSpec((tm, SSM_W), tok),
                   pl.BlockSpec((tm, SSM_W), tok), pl.BlockSpec((1, ATTN_W), fix),
                   pl.BlockSpec((1, SSM_W), fix), pl.BlockSpec((1, SSM_W), fix)],
        out_shape=[_sds((N_HEADS, T, 128), BF16), _sds((T, SSM_W), F32), _sds((T, SSM_W), BF16), _sds((T, SSM_W), BF16),
                   _sds((1, ATTN_W), F32), _sds((1, SSM_W), F32), _sds((1, SSM_W), F32)],
        compiler_params=_params(("arbitrary",)),
    )(dh2, o_heads, ypre, g_a, g_s, w_glu, b_glu, w_out, seg)


def head_fwd_bwd(h3, p, target, g_ple, g_final, w_gate, w_proj):
    T = h3.shape[0]
    tm = _tile(T, TOKEN_TILE)
    pd = p.shape[1]

    def body(h_ref, p_ref, tg_ref, gp_ref, gf_ref, wg_ref, wp_ref,
             dh_ref, n3_ref, dz_ref, dpp_ref, loss_ref, dgp_ref, dgf_ref):
        @pl.when(pl.program_id(0) == 0)
        def _():
            loss_ref[...] = jnp.zeros_like(loss_ref)
            dgp_ref[...] = jnp.zeros_like(dgp_ref)
            dgf_ref[...] = jnp.zeros_like(dgf_ref)

        x = h_ref[...]
        gp, gf = gp_ref[...], gf_ref[...]
        n3 = (x * _rms_scale(x) * gp).astype(BF16)
        n3_ref[...] = n3
        gate = jax.nn.sigmoid(_dot(n3, wg_ref[...]))
        pp = _dot(p_ref[...].astype(BF16), wp_ref[...])
        h4 = x + gate * pp
        y = h4 * _rms_scale(h4) * gf
        e = y - tg_ref[...]
        tile_loss = jnp.sum(jnp.sum(e * e, axis=1, keepdims=True), axis=0, keepdims=True) * (0.5 / D_MODEL)
        loss_ref[...] += jnp.broadcast_to(tile_loss, loss_ref.shape)
        dh4, dgf = _rms_bwd(e * (1.0 / D_MODEL), h4, gf)
        dgf_ref[...] += dgf
        dzg = dh4 * pp * gate * (1.0 - gate)
        dzb = dzg.astype(BF16)
        dz_ref[...] = dzb
        dpp_ref[...] = (dh4 * gate).astype(BF16)
        dx, dgp = _rms_bwd(_dot_nt(dzb, wg_ref[...]), x, gp)
        dgp_ref[...] += dgp
        dh_ref[...] = dh4 + dx

    tok = lambda i: (i, 0)
    fix = lambda i: (0, 0)
    return pl.pallas_call(
        body, name="head_fwd_bwd", grid=(T // tm,),
        in_specs=[pl.BlockSpec((tm, D_MODEL), tok), pl.BlockSpec((tm, pd), tok), pl.BlockSpec((tm, D_MODEL), tok),
                  pl.BlockSpec((1, D_MODEL), fix), pl.BlockSpec((1, D_MODEL), fix), pl.BlockSpec((D_MODEL, D_MODEL), fix),
                  pl.BlockSpec((pd, D_MODEL), fix)],
        out_specs=[pl.BlockSpec((tm, D_MODEL), tok), pl.BlockSpec((tm, D_MODEL), tok), pl.BlockSpec((tm, D_MODEL), tok),
                   pl.BlockSpec((tm, D_MODEL), tok), pl.BlockSpec((8, 128), fix), pl.BlockSpec((1, D_MODEL), fix),
                   pl.BlockSpec((1, D_MODEL), fix)],
        out_shape=[_sds((T, D_MODEL), F32), _sds((T, D_MODEL), BF16), _sds((T, D_MODEL), BF16), _sds((T, D_MODEL), BF16),
                   _sds((8, 128), F32), _sds((1, D_MODEL), F32), _sds((1, D_MODEL), F32)],
        compiler_params=_params(("arbitrary",)),
    )(h3, p, target, g_ple, g_final, w_gate, w_proj)


def _row_tile(rows, cols, n_arrays):
    lanes = -(-cols // 128) * 128
    cap = VMEM_LIMIT // 3 // (2 * n_arrays * lanes * 4)
    best = None
    for t in range(PACK_ALIGN, min(rows, cap) + 1, PACK_ALIGN):
        if rows % t == 0:
            best = t
    assert best is not None, (rows, cols)
    return best


def _adamw_math(w, g, m, v):
    nm = ADAM_B1 * m + (1.0 - ADAM_B1) * g
    nv = ADAM_B2 * v + (1.0 - ADAM_B2) * (g * g)
    c1 = 1.0 - ADAM_B1 ** ADAM_STEP
    c2 = 1.0 - ADAM_B2 ** ADAM_STEP
    return -ADAM_LR * ((nm / c1) / (jnp.sqrt(nv / c2) + ADAM_EPS) + ADAM_WD * w), nm, nv


def adamw(w, g, m, v, name):
    R, C = w.shape
    tr = _row_tile(R, C, 7)

    def body(w_ref, g_ref, m_ref, v_ref, d_ref, nm_ref, nv_ref):
        d_ref[...], nm_ref[...], nv_ref[...] = _adamw_math(w_ref[...], g_ref[...], m_ref[...], v_ref[...])

    spec = pl.BlockSpec((tr, C), lambda i: (i, 0))
    return pl.pallas_call(
        body, name=name, grid=(R // tr,), in_specs=[spec] * 4, out_specs=[spec] * 3,
        out_shape=[_sds((R, C), F32)] * 3, compiler_params=_params(("arbitrary",)),
    )(w, g, m, v)


def join_halves(mine, other, core):
    rh, C = mine.shape
    tr = _row_tile(rh, C, 3)
    nb = rh // tr

    def body(c_ref, m_ref, o_ref, out_ref):
        out_ref[...] = jnp.where((pl.program_id(0) // nb) == c_ref[0], m_ref[...], o_ref[...])

    half = pl.BlockSpec((tr, C), lambda i, c: (i % nb, 0))
    return pl.pallas_call(
        body, name="join_halves",
        grid_spec=pltpu.PrefetchScalarGridSpec(num_scalar_prefetch=1, grid=(2 * nb,), in_specs=[half, half],
                                               out_specs=pl.BlockSpec((tr, C), lambda i, c: (i, 0))),
        out_shape=_sds((2 * rh, C), F32), compiler_params=_params(("arbitrary",)),
    )(core, mine, other)


def pair_sum(g, theirs, core):
    n, R, C = g.shape
    rh = R // 2
    tr = _row_tile(rh, C, 3)
    nb = rh // tr

    def body(c_ref, g_ref, t_ref, o_ref):
        o_ref[...] = (g_ref[...] + t_ref[...]).astype(BF16)

    here = pl.BlockSpec((1, tr, C), lambda j, i, c: (j, i, 0))
    return pl.pallas_call(
        body, name="pair_sum",
        grid_spec=pltpu.PrefetchScalarGridSpec(
            num_scalar_prefetch=1, grid=(n, nb),
            in_specs=[pl.BlockSpec((1, tr, C), lambda j, i, c: (j, c[0] * nb + i, 0)), here], out_specs=here),
        out_shape=_sds((n, rh, C), BF16), compiler_params=_params(("arbitrary", "arbitrary")),
    )(core, g, theirs)


def chip_sum(pair, got, chip):
    _, R, C = pair.shape
    tr = _row_tile(R, C, 5)

    def body(c_ref, p_ref, g0_ref, g1_ref, g2_ref, o_ref):
        f = lambda ref: ref[0].astype(F32)
        o_ref[...] = ((f(p_ref) + f(g0_ref)) + f(g1_ref)) + f(g2_ref)

    slot = lambda k: pl.BlockSpec((1, tr, C), lambda i, c: (k, i, 0))
    return pl.pallas_call(
        body, name="chip_sum",
        grid_spec=pltpu.PrefetchScalarGridSpec(
            num_scalar_prefetch=1, grid=(R // tr,),
            in_specs=[pl.BlockSpec((1, tr, C), lambda i, c: (c[0], i, 0)), slot(0), slot(1), slot(2)],
            out_specs=pl.BlockSpec((tr, C), lambda i, c: (i, 0))),
        out_shape=_sds((R, C), F32), compiler_params=_params(("arbitrary",)),
    )(chip, pair, got, got, got)


_HBM = pl.BlockSpec(memory_space=pltpu.HBM)


def _place():
    x, y, c = lax.axis_index("x"), lax.axis_index("y"), lax.axis_index("c")
    return x, y, c, [(1 - x, y), (x, 1 - y), (1 - x, 1 - y)]


def _spans(rows, n):
    assert rows % PACK_ALIGN == 0
    tiles = rows // PACK_ALIGN
    n = min(n, tiles)
    cuts = [tiles * q // n for q in range(n + 1)]
    return [(cuts[q] * PACK_ALIGN, (cuts[q + 1] - cuts[q]) * PACK_ALIGN) for q in range(n)]


def _remote(src, dst, send_sem, recv_sem, to):
    return pltpu.make_async_remote_copy(src_ref=src, dst_ref=dst, send_sem=send_sem, recv_sem=recv_sem,
                                        device_id=to, device_id_type=MESH)


def allgather_shards(wp):
    R, C = wp.shape
    rh = R // 2
    spans = _spans(rh, COPY_CHUNKS)
    n_sp = len(spans)
    local_spans = _spans(R, 2 * COPY_CHUNKS)

    def body(w_ref, out_ref, send_sems, recv_sems, pass_send, pass_recv, local_sems):
        x, y, c, chips = _place()
        me = 2 * x + y
        local = []
        for q, (o, n) in enumerate(local_spans):
            cp = pltpu.make_async_copy(w_ref.at[pl.ds(o, n), :], out_ref.at[me, pl.ds(o, n), :], local_sems.at[q])
            cp.start()
            local.append(cp)
        sends = []
        for k, (cx, cy) in enumerate(chips):
            for q, (o, n) in enumerate(spans):
                rows = pl.ds(c * rh + o, n)
                cp = _remote(w_ref.at[rows, :], out_ref.at[me, rows, :], send_sems.at[k * n_sp + q],
                             recv_sems.at[k * n_sp + q], (cx, cy, c))
                cp.start()
                sends.append(cp)
        for q, (o, n) in enumerate(spans):
            for k, (cx, cy) in enumerate(chips):
                blk = out_ref.at[2 * cx + cy, pl.ds(c * rh + o, n), :]
                _remote(blk, blk, send_sems.at[k * n_sp + q], recv_sems.at[k * n_sp + q], (cx, cy, c)).wait_recv()
                cp = _remote(blk, blk, pass_send.at[k * n_sp + q], pass_recv.at[k * n_sp + q], (x, y, 1 - c))
                cp.start()
                sends.append(cp)
        for k, (cx, cy) in enumerate(chips):
            for q, (o, n) in enumerate(spans):
                blk = out_ref.at[2 * cx + cy, pl.ds((1 - c) * rh + o, n), :]
                _remote(blk, blk, pass_send.at[k * n_sp + q], pass_recv.at[k * n_sp + q], (x, y, 1 - c)).wait_recv()
        for cp in sends:
            cp.wait_send()
        for cp in local:
            cp.wait()

    sems = pltpu.SemaphoreType.DMA((3 * n_sp,))
    return pl.pallas_call(
        body, name="allgather_shards", in_specs=[_HBM], out_specs=_HBM, out_shape=_sds((4, R, C), wp.dtype),
        scratch_shapes=[sems, sems, sems, sems, pltpu.SemaphoreType.DMA((len(local_spans),))],
    )(wp)


def sibling_split(g):
    n_sl, R, C = g.shape
    rh = R // 2
    spans = _spans(rh, COPY_CHUNKS)
    n_sp = len(spans)

    def body(g_ref, got_ref, send_sems, recv_sems):
        x, y, c, _ = _place()
        copies = []
        for j in range(n_sl):
            for q, (o, n) in enumerate(spans):
                cp = _remote(g_ref.at[j, pl.ds((1 - c) * rh + o, n), :], got_ref.at[j, pl.ds(o, n), :],
                             send_sems.at[j * n_sp + q], recv_sems.at[j * n_sp + q], (x, y, 1 - c))
                cp.start()
                copies.append(cp)
        for cp in copies:
            cp.wait()

    sems = pltpu.SemaphoreType.DMA((n_sl * n_sp,))
    return pl.pallas_call(
        body, name="sibling_split", in_specs=[_HBM], out_specs=_HBM, out_shape=_sds((n_sl, rh, C), g.dtype),
        scratch_shapes=[sems, sems],
    )(g)


def chip_exchange(p):
    _, R, C = p.shape
    spans = _spans(R, COPY_CHUNKS)
    n_sp = len(spans)

    def body(p_ref, buf_ref, send_sems, recv_sems):
        x, y, c, chips = _place()
        sends = []
        for k, (cx, cy) in enumerate(chips):
            for q, (o, n) in enumerate(spans):
                cp = _remote(p_ref.at[2 * cx + cy, pl.ds(o, n), :], buf_ref.at[k, pl.ds(o, n), :],
                             send_sems.at[k * n_sp + q], recv_sems.at[k * n_sp + q], (cx, cy, c))
                cp.start()
                sends.append(cp)
        for cp in sends:
            cp.wait()

    sems = pltpu.SemaphoreType.DMA((3 * n_sp,))
    return pl.pallas_call(
        body, name="chip_exchange", in_specs=[_HBM], out_specs=_HBM, out_shape=_sds((3, R, C), p.dtype),
        scratch_shapes=[sems, sems],
    )(p)


def sibling_swap(half):
    R, C = half.shape
    spans = _spans(R, COPY_CHUNKS)

    def body(h_ref, got_ref, send_sems, recv_sems):
        x, y, c, _ = _place()
        copies = []
        for q, (o, n) in enumerate(spans):
            cp = _remote(h_ref.at[pl.ds(o, n), :], got_ref.at[pl.ds(o, n), :], send_sems.at[q], recv_sems.at[q], (x, y, 1 - c))
            cp.start()
            copies.append(cp)
        for cp in copies:
            cp.wait()

    sems = pltpu.SemaphoreType.DMA((len(spans),))
    return pl.pallas_call(
        body, name="sibling_swap", in_specs=[_HBM], out_specs=_HBM, out_shape=_sds((R, C), half.dtype),
        scratch_shapes=[sems, sems],
    )(half)


def allreduce_small(v):
    R, C = v.shape

    def body(v_ref, out_ref, buf, send_sems, recv_sems):
        x, y, c, _ = _place()
        me = 4 * x + 2 * y + c
        buf[me] = v_ref[...]
        flips = [((k >> 2) & 1, (k >> 1) & 1, k & 1) for k in range(1, 8)]
        sends = []
        for k, (fx, fy, fc) in enumerate(flips):
            to = (1 - x if fx else x, 1 - y if fy else y, 1 - c if fc else c)
            cp = _remote(v_ref, buf.at[me], send_sems.at[k], recv_sems.at[k], to)
            cp.start()
            sends.append(cp)
        for k, (fx, fy, fc) in enumerate(flips):
            px, py, pc = (1 - x if fx else x, 1 - y if fy else y, 1 - c if fc else c)
            blk = buf.at[4 * px + 2 * py + pc]
            _remote(blk, blk, send_sems.at[k], recv_sems.at[k], (px, py, pc)).wait_recv()
        for cp in sends:
            cp.wait_send()
        acc = buf[0]
        for s in range(1, 8):
            acc = acc + buf[s]
        out_ref[...] = acc

    vm = pl.BlockSpec(memory_space=pltpu.VMEM)
    return pl.pallas_call(
        body, name="allreduce_small", in_specs=[vm], out_specs=vm, out_shape=_sds((R, C), F32),
        scratch_shapes=[pltpu.VMEM((8, R, C), F32), pltpu.SemaphoreType.DMA((7,)), pltpu.SemaphoreType.DMA((7,))],
        compiler_params=pltpu.CompilerParams(vmem_limit_bytes=VMEM_LIMIT),
    )(v)


def _rows_of(shape):
    return shape[0] * shape[1] // PACK_COLS


def _slot_rows(shape):
    return -(-_rows_of(shape) // PACK_ALIGN) * PACK_ALIGN


TRANSPOSED = ("w1_a", "w3_a", "w1_b", "w3_b")


def _stored(name, shard):
    return shard[0].T if name in TRANSPOSED else shard[0]


def _restored(name, stored):
    return stored.T[None] if name in TRANSPOSED else stored[None]


def _pack_shards(shards, dtype):
    parts = []
    for name, shape, _ in BIG:
        part = _stored(name, shards[name]).reshape(_rows_of(shape), PACK_COLS).astype(dtype)
        parts.append(jnp.pad(part, ((0, _slot_rows(shape) - part.shape[0]), (0, 0))))
    used = sum(p.shape[0] for p in parts)
    parts.append(jnp.zeros((PACK_ROWS - used, PACK_COLS), dtype))
    return jnp.concatenate(parts, axis=0)


def _unpack_gathered(ag):
    out, off = {}, 0
    for name, shape, axis in BIG:
        r = _rows_of(shape)
        piece = ag[:, off:off + r, :]
        off += _slot_rows(shape)
        if name in TRANSPOSED:
            out[name] = piece.reshape(4 * r, PACK_COLS)
        elif axis == 0:
            out[name] = piece.reshape(4 * shape[0], shape[1])
        else:
            out[name] = piece.reshape((4,) + shape).transpose(1, 0, 2).reshape(shape[0], 4 * shape[1])
    return out


def _pack_full_grads(grads):
    parts = []
    for name, shape, axis in BIG:
        g = grads[name]
        if name in TRANSPOSED or axis == 0:
            piece = g.reshape(4, _rows_of(shape), PACK_COLS)
        else:
            piece = g.reshape(shape[0], 4, shape[1]).transpose(1, 0, 2).reshape(4, _rows_of(shape), PACK_COLS)
        parts.append(jnp.pad(piece, ((0, 0), (0, _slot_rows(shape) - piece.shape[1]), (0, 0))))
    used = sum(p.shape[1] for p in parts)
    parts.append(jnp.zeros((4, PACK_ROWS - used, PACK_COLS), F32))
    return jnp.concatenate(parts, axis=1)


def _unpack_shards(packed):
    out, off = {}, 0
    for name, shape, _ in BIG:
        r = _rows_of(shape)
        out[name] = packed[off:off + r] if name in TRANSPOSED else packed[off:off + r].reshape(shape)
        off += _slot_rows(shape)
    return out


def _small_rows(shape):
    return -(-math.prod(shape) // 1024) * 8


def _pack_small(vals, extra=None):
    def slot(v, rows):
        flat = v.reshape(-1)
        return jnp.pad(flat, (0, rows * 128 - flat.shape[0])).reshape(rows, 128)

    parts = [slot(vals[name], _small_rows(shape)) for name, shape in SMALL]
    parts.append(slot(extra if extra is not None else jnp.zeros((1,), F32), 8))
    assert sum(p.shape[0] for p in parts) == SMALL_ROWS
    return jnp.concatenate(parts, axis=0)


def _unpack_small(packed):
    out, off = {}, 0
    for name, shape in SMALL:
        rows = _small_rows(shape)
        out[name] = packed[off:off + rows].reshape(-1)[:math.prod(shape)].reshape(shape)
        off += rows
    return out, packed[off, 0]


def _permute_time(a):
    T, n = a.shape
    return a.reshape(8, T // 8, n).transpose(1, 0, 2).reshape(T, n)


def _unpermute_time(a):
    T, n = a.shape
    return a.reshape(T // 8, 8, n).transpose(1, 0, 2).reshape(T, n)


def _discretize(a_re, a_im, log_dt, b_re, b_im):
    dt = jnp.exp(log_dt)[:, None]
    decay = jnp.exp(dt * a_re)
    abar_r = decay * jnp.cos(dt * a_im)
    abar_i = decay * jnp.sin(dt * a_im)
    nr, ni = abar_r - 1.0, abar_i
    den = a_re * a_re + a_im * a_im
    fr = (nr * a_re + ni * a_im) / den
    fi = (ni * a_re - nr * a_im) / den
    bbar_r = fr[..., None] * b_re - fi[..., None] * b_im
    bbar_i = fr[..., None] * b_im + fi[..., None] * b_re
    return abar_r, abar_i, bbar_r, bbar_i


def _input_matrix(bbar_r, bbar_i):
    eye = jnp.eye(N_GROUPS, dtype=F32)
    blk = lambda b: jnp.einsum("ghp,gk->ghkp", b.transpose(0, 2, 1), eye).reshape(SSM_W, STATE_W)
    return jnp.concatenate([blk(bbar_r), blk(bbar_i)], axis=1)


def _output_matrix(c_re, c_im):
    eye = jnp.eye(N_GROUPS, dtype=F32)
    blk = lambda cm: jnp.einsum("ghp,gk->gpkh", cm, eye).reshape(STATE_W, SSM_W)
    return jnp.concatenate([blk(c_re), -blk(c_im)], axis=0)


def _state_power(ar, ai, n):
    steps = int(round(math.log2(n)))
    assert 1 << steps == n
    for _ in range(steps):
        ar, ai = ar * ar - ai * ai, 2.0 * ar * ai
    return ar, ai


def kernel(x, p, g_ffn1, w1_a, w3_a, w2_a, g_mix, w_in, b_f, a_re, a_im, log_dt, b_re, b_im, c_re, c_im, d_skip, w_glu, b_glu, g_attn_out, g_ssm_out, w_out, g_ffn2, w1_b, w3_b, w2_b, g_ple, w_ple_gate, w_ple_proj, g_final, loss_target, m_g_ffn1, m_w1_a, m_w3_a, m_w2_a, m_g_mix, m_w_in, m_b_f, m_a_re, m_a_im, m_log_dt, m_b_re, m_b_im, m_c_re, m_c_im, m_d_skip, m_w_glu, m_b_glu, m_g_attn_out, m_g_ssm_out, m_w_out, m_g_ffn2, m_w1_b, m_w3_b, m_w2_b, m_g_ple, m_w_ple_gate, m_w_ple_proj, m_g_final, v_g_ffn1, v_w1_a, v_w3_a, v_w2_a, v_g_mix, v_w_in, v_b_f, v_a_re, v_a_im, v_log_dt, v_b_re, v_b_im, v_c_re, v_c_im, v_d_skip, v_w_glu, v_b_glu, v_g_attn_out, v_g_ssm_out, v_w_out, v_g_ffn2, v_w1_b, v_w3_b, v_w2_b, v_g_ple, v_w_ple_gate, v_w_ple_proj, v_g_final):
    args = dict(locals())
    weights = {n: args[n] for n in WEIGHT_ORDER}
    moms = {n: args["m_" + n] for n in WEIGHT_ORDER}
    vars_ = {n: args["v_" + n] for n in WEIGHT_ORDER}
    T = x.shape[1]
    x2, p2, tgt = x[0], p[0, 0], loss_target[0]

    full = _unpack_gathered(allgather_shards(_pack_shards(weights, BF16)))
    loss_part, dx, grads = _local_step(x2, p2, tgt, {n: weights[n] for n, _ in SMALL}, full)
    return _reduce_and_update(weights, moms, vars_, loss_part, dx, grads)


def _local_step(x2, p2, tgt, sm, full):
    T = x2.shape[0]
    (g_ffn1, g_mix, b_f, a_re, a_im, log_dt, b_re, b_im, c_re, c_im, d_skip, b_glu, g_attn_out, g_ssm_out, g_ffn2, g_ple,
     g_final) = (sm[n] for n, _ in SMALL)
    w_in_f = full["w_in"]
    w_in_r = jnp.concatenate([w_in_f[:, :ATTN_W] * QK_SCALE, w_in_f[:, ATTN_W:3 * ATTN_W], w_in_f[:, 3 * ATTN_W + N_HEADS:],
                              w_in_f[:, 3 * ATTN_W:3 * ATTN_W + N_HEADS], jnp.zeros((D_MODEL, 128 - N_HEADS), BF16)], axis=1)
    b_f_pad = jnp.pad(b_f, ((0, 0), (0, 128 - N_HEADS)))

    disc_in = (a_re[0], a_im[0], log_dt[0], b_re[0], b_im[0])
    (abar_r, abar_i, bbar_r, bbar_i), disc_vjp = jax.vjp(_discretize, *disc_in)
    wb = _input_matrix(bbar_r, bbar_i)
    cbd = _output_matrix(c_re[0], c_im[0])
    ar, ai = abar_r.reshape(1, STATE_W), abar_i.reshape(1, STATE_W)
    alr, ali = _state_power(ar, ai, T // 8)
    dvec = d_skip.reshape(1, SSM_W)
    wb16, cbd16 = wb.astype(BF16), cbd.astype(BF16)

    h1, a1a, a3a, n1 = ffn_fwd(x2, g_ffn1, full["w1_a"], full["w3_a"], full["w2_a"], "ffn_a_fwd")
    u, qkv, s_in, fz, cum = mixin_fwd(h1, g_mix, w_in_r, b_f_pad)
    q_aug, k_aug, v_aug = heads_in(qkv, cum)
    o_heads, q_bwd = attn_fwd(q_aug, k_aug, v_aug)
    s_perm = _permute_time(s_in)
    y_perm, xs = ssm_fwd(s_perm, wb16, cbd16, ar, ai, alr, ali, dvec)
    ypre = _unpermute_time(y_perm)
    h2, mixed = mixout_fwd(h1, o_heads, ypre, g_attn_out, g_ssm_out, full["w_glu"], b_glu, full["w_out"])
    h3, a1b, a3b, n2 = ffn_fwd(h2, g_ffn2, full["w1_b"], full["w3_b"], full["w2_b"], "ffn_b_fwd")

    dh3, n3, dzg, dpp, loss_part, dg_ple, dg_final = head_fwd_bwd(
        h3, p2, tgt, g_ple, g_final.reshape(1, D_MODEL), full["w_ple_gate"], full["w_ple_proj"])
    grads = {"g_ple": dg_ple, "g_final": dg_final.reshape(D_MODEL)}
    grads["w_ple_gate"] = mm_tn(n3, dzg, "dw_ple_gate")
    grads["w_ple_proj"] = mm_tn(p2, dpp, "dw_ple_proj")

    dh2, da1, da3, act, grads["g_ffn2"] = ffn_bwd(h2, g_ffn2, dh3, a1b, a3b, full["w1_b"], full["w3_b"], full["w2_b"], "ffn_b_bwd")
    grads["w1_b"] = mm_tn(da1, n2, "dw1_b")
    grads["w3_b"] = mm_tn(da3, n2, "dw3_b")
    grads["w2_b"] = mm_tn(act, dh3, "dw2_b", scale=0.5)

    seg = (jnp.arange(ATTN_W)[:, None] // HEAD_DIM == jnp.arange(128)[None, :]).astype(F32)
    do_aug, dypre, dpre, yg, grads["g_attn_out"], grads["g_ssm_out"], grads["b_glu"] = mixout_bwd(
        dh2, o_heads, ypre, g_attn_out, g_ssm_out, full["w_glu"], b_glu, full["w_out"], seg)
    grads["w_out"] = mm_tn(mixed, dh2, "dw_out")
    grads["w_glu"] = mm_tn(yg, dpre, "dw_glu")

    dq_aug, dk_aug, dv_aug, dc_rows = attn_bwd(q_bwd, k_aug, v_aug, do_aug)
    dc = jnp.pad(dc_rows.reshape(N_HEADS, T).T, ((0, 0), (0, 128 - N_HEADS)))

    dy_perm = _permute_time(dypre)
    du_perm, gs, d_a, dd = ssm_bwd(dy_perm, s_perm, xs, cbd16.T, wb16.T, ar, ai, alr, ali, dvec)
    ds_in = _unpermute_time(du_perm)
    hg = N_GROUPS // 2
    d_in, d_out = [], []
    for part in range(2):
        ins, outs = [], []
        for half in range(2):
            states = (part * STATE_W + half * _HALF_ST, _HALF_ST)
            chans = (half * _HALF_CH, _HALF_CH)
            blk = mm_tn(s_perm, gs, f"dw_ssm_in_{part}{half}", a_cols=chans, b_cols=states)
            ins.append(jnp.einsum("ghgp->ghp", blk.reshape(hg, GROUP_CH, hg, N_STATE)))
            blk = mm_tn(xs, dy_perm, f"dw_ssm_out_{part}{half}", a_cols=states, b_cols=chans)
            outs.append(jnp.einsum("gpgh->gph", blk.reshape(hg, N_STATE, hg, GROUP_CH)))
        d_in.append(jnp.concatenate(ins, axis=0).transpose(0, 2, 1))
        d_out.append(jnp.concatenate(outs, axis=0).transpose(0, 2, 1))
    d_abar_r = jnp.sum(d_a[:, :STATE_W], axis=0).reshape(N_GROUPS, N_STATE)
    d_abar_i = jnp.sum(d_a[:, STATE_W:], axis=0).reshape(N_GROUPS, N_STATE)
    d_disc = disc_vjp((d_abar_r, d_abar_i, d_in[0], d_in[1]))
    for name, val in zip(("a_re", "a_im", "log_dt", "b_re", "b_im"), d_disc):
        grads[name] = val[None]
    grads["c_re"] = d_out[0][None]
    grads["c_im"] = -d_out[1][None]
    grads["d_skip"] = dd.reshape(1, N_GROUPS, GROUP_CH)

    dh1, dz, grads["g_mix"], dbf = mixin_bwd(dh2, h1, g_mix, w_in_r, dq_aug, dk_aug, dv_aug, ds_in, dc, fz)
    grads["b_f"] = dbf[:, :N_HEADS]
    d_w_in_r = mm_tn(u, dz, "dw_in")
    grads["w_in"] = jnp.concatenate([d_w_in_r[:, :ATTN_W] * QK_SCALE, d_w_in_r[:, ATTN_W:3 * ATTN_W],
                                     d_w_in_r[:, 3 * ATTN_W + SSM_W:3 * ATTN_W + SSM_W + N_HEADS],
                                     d_w_in_r[:, 3 * ATTN_W:3 * ATTN_W + SSM_W]], axis=1)

    dx, da1, da3, act, grads["g_ffn1"] = ffn_bwd(x2, g_ffn1, dh1, a1a, a3a, full["w1_a"], full["w3_a"], full["w2_a"], "ffn_a_bwd")
    grads["w1_a"] = mm_tn(da1, n1, "dw1_a")
    grads["w3_a"] = mm_tn(da3, n1, "dw3_a")
    grads["w2_a"] = mm_tn(act, dh1, "dw2_a", scale=0.5)
    return loss_part, dx, grads


def _reduce_and_update(weights, moms, vars_, loss_part, dx, grads):
    core = lax.axis_index("c").astype(jnp.int32).reshape(1)
    chip = (2 * lax.axis_index("x") + lax.axis_index("y")).astype(jnp.int32).reshape(1)
    packed = _pack_full_grads(grads)
    pair = pair_sum(packed, sibling_split(packed), core)
    half = chip_sum(pair, chip_exchange(pair), chip)
    g_stored = _unpack_shards(join_halves(half, sibling_swap(half), core))
    g_out, d_out, m_out, v_out = {}, {}, {}, {}
    for n, _, _ in BIG:
        d, m, v = adamw(_stored(n, weights[n]), g_stored[n], _stored(n, moms[n]), _stored(n, vars_[n]), "adamw_" + n)
        g_out[n], d_out[n], m_out[n], v_out[n] = (_restored(n, a) for a in (g_stored[n], d, m, v))

    small = allreduce_small(_pack_small({n: grads[n] for n, _ in SMALL}, extra=loss_part[0, 0]))
    d_small, m_small, v_small = adamw(_pack_small(weights), small, _pack_small(moms), _pack_small(vars_), "adamw_small")

    g_small, loss = _unpack_small(small)
    g_out.update(g_small)
    outs = []
    for big, sm in ((d_out, d_small), (m_out, m_small), (v_out, v_small)):
        o, _ = _unpack_small(sm)
        o.update(big)
        outs.append(o)
    result = [loss, dx[None]] + [g_out[n] for n in WEIGHT_ORDER]
    for o in outs:
        result += [o[n] for n in WEIGHT_ORDER]
    return tuple(result)
```

```python
import functools
import math

import jax
import jax.numpy as jnp
from jax import lax
from jax.experimental import pallas as pl
from jax.experimental.pallas import tpu as pltpu

F32 = jnp.float32
BF16 = jnp.bfloat16

D_MODEL = 1024
D_FF = 2816
N_HEADS = 8
HEAD_DIM = 64
ATTN_W = 512
SSM_W = 512
N_GROUPS = 32
N_STATE = 64
GROUP_CH = 16
STATE_W = N_GROUPS * N_STATE
Z_COLS = 2176
QK_SCALE = 0.125
EPS = 1e-6

ADAM_LR = 0.001
ADAM_B1 = 0.9
ADAM_B2 = 0.999
ADAM_EPS = 1e-08
ADAM_WD = 0.01
ADAM_STEP = 10

TOKEN_TILE = 512
FFN_TOKEN_TILE = 256
FF_CHUNK = 1408
MM_K_TILE = 2048
ATTN_TILE = 512
SCAN_STEPS = 32
SCAN_LANES = 512
VMEM_LIMIT = 48 * 1024 * 1024
FFN_VMEM_LIMIT = 56 * 1024 * 1024
COPY_CHUNKS = 4

NT_DIMS = (((1,), (1,)), ((), ()))
TN_DIMS = (((0,), (0,)), ((), ()))
HIGHEST = lax.Precision.HIGHEST
MESH = pl.DeviceIdType.MESH

BIG = (
    ("w1_a", (1024, 704), 1), ("w3_a", (1024, 704), 1), ("w2_a", (704, 1024), 0),
    ("w_in", (1024, 514), 1), ("w_glu", (128, 512), 0), ("w_out", (256, 1024), 0),
    ("w1_b", (1024, 704), 1), ("w3_b", (1024, 704), 1), ("w2_b", (704, 1024), 0),
    ("w_ple_gate", (256, 1024), 0), ("w_ple_proj", (256, 256), 1),
)
PACK_COLS = 1024
PACK_ALIGN = 16
PACK_ROWS = 5408
SMALL = (
    ("g_ffn1", (1, 1024)), ("g_mix", (1, 1024)), ("b_f", (1, 8)), ("a_re", (1, 32, 64)), ("a_im", (1, 32, 64)),
    ("log_dt", (1, 32)), ("b_re", (1, 32, 64, 16)), ("b_im", (1, 32, 64, 16)), ("c_re", (1, 32, 16, 64)),
    ("c_im", (1, 32, 16, 64)), ("d_skip", (1, 32, 16)), ("b_glu", (1, 512)), ("g_attn_out", (1, 512)),
    ("g_ssm_out", (1, 512)), ("g_ffn2", (1, 1024)), ("g_ple", (1, 1024)), ("g_final", (1024,)),
)
SMALL_ROWS = 1152
WEIGHT_ORDER = ("g_ffn1", "w1_a", "w3_a", "w2_a", "g_mix", "w_in", "b_f", "a_re", "a_im", "log_dt", "b_re", "b_im",
                "c_re", "c_im", "d_skip", "w_glu", "b_glu", "g_attn_out", "g_ssm_out", "w_out", "g_ffn2", "w1_b",
                "w3_b", "w2_b", "g_ple", "w_ple_gate", "w_ple_proj", "g_final")


def _params(sem=None, vmem=VMEM_LIMIT):
    kw = dict(vmem_limit_bytes=vmem)
    if sem is not None:
        kw["dimension_semantics"] = sem
    return pltpu.CompilerParams(**kw)


def _sds(shape, dtype):
    return jax.ShapeDtypeStruct(shape, dtype)


def _tile(n, pref):
    t = min(n, pref)
    assert n % t == 0, (n, pref)
    return t


def _rms_scale(x):
    return lax.rsqrt(jnp.mean(x * x, axis=-1, keepdims=True) + EPS)


def _rms_bwd(dy, x, g):
    r = _rms_scale(x)
    xh = x * r
    dxh = dy * g
    dx = r * (dxh - xh * jnp.mean(dxh * xh, axis=-1, keepdims=True))
    return dx, jnp.sum(dy * xh, axis=0, keepdims=True)


def _dot(a, b):
    return jnp.dot(a, b, preferred_element_type=F32)


def _dot_nt(a, b):
    return lax.dot_general(a, b, NT_DIMS, preferred_element_type=F32)


def _dot_tn(a, b):
    return lax.dot_general(a, b, TN_DIMS, preferred_element_type=F32)


_GELU_C = math.sqrt(2.0 / math.pi)


def _gelu_parts(x):
    t = jnp.tanh(_GELU_C * (x + 0.044715 * x * x * x))
    return 0.5 * x * (1.0 + t), t


def _gelu_grad(x, t):
    return 0.5 * (1.0 + t) + 0.5 * x * (1.0 - t * t) * _GELU_C * (1.0 + 3.0 * 0.044715 * x * x)


def _resident(shape):
    return pl.BlockSpec(shape, lambda i: (0,) * len(shape), pipeline_mode=pl.Buffered(1))


def ffn_fwd(h, g, w1, w3, w2, name):
    T = h.shape[0]
    tm = _tile(T, FFN_TOKEN_TILE)

    def body(h_ref, g_ref, w1_ref, w3_ref, w2_ref, ho_ref, a1_ref, a3_ref, n_ref):
        x = h_ref[...]
        n = (x * _rms_scale(x) * g_ref[...]).astype(BF16)
        n_ref[...] = n
        out = x
        for lo in range(0, D_FF, FF_CHUNK):
            cols = slice(lo, lo + FF_CHUNK)
            a1 = _dot_nt(n, w1_ref[cols, :])
            a3 = _dot_nt(n, w3_ref[cols, :])
            a1_ref[:, cols] = a1.astype(BF16)
            a3_ref[:, cols] = a3.astype(BF16)
            act = (a1 * jax.nn.sigmoid(a1) * a3).astype(BF16)
            out = out + 0.5 * _dot(act, w2_ref[cols, :])
        ho_ref[...] = out

    tok = lambda i: (i, 0)
    return pl.pallas_call(
        body, name=name, grid=(T // tm,),
        in_specs=[pl.BlockSpec((tm, D_MODEL), tok), _resident((1, D_MODEL)), _resident((D_FF, D_MODEL)),
                  _resident((D_FF, D_MODEL)), _resident((D_FF, D_MODEL))],
        out_specs=[pl.BlockSpec((tm, D_MODEL), tok), pl.BlockSpec((tm, D_FF), tok), pl.BlockSpec((tm, D_FF), tok),
                   pl.BlockSpec((tm, D_MODEL), tok)],
        out_shape=[_sds((T, D_MODEL), F32), _sds((T, D_FF), BF16), _sds((T, D_FF), BF16), _sds((T, D_MODEL), BF16)],
        compiler_params=_params(("arbitrary",), FFN_VMEM_LIMIT),
    )(h, g, w1, w3, w2)


def ffn_bwd(h, g, dho, a1, a3, w1, w3, w2, name):
    T = h.shape[0]
    tm = _tile(T, FFN_TOKEN_TILE)

    def body(h_ref, g_ref, dho_ref, a1_ref, a3_ref, w1_ref, w3_ref, w2_ref, dhi_ref, da1_ref, da3_ref, act_ref, dg_ref):
        @pl.when(pl.program_id(0) == 0)
        def _():
            dg_ref[...] = jnp.zeros_like(dg_ref)

        dho = dho_ref[...]
        dhb = (0.5 * dho).astype(BF16)
        dn = None
        for lo in range(0, D_FF, FF_CHUNK):
            cols = slice(lo, lo + FF_CHUNK)
            a1v = a1_ref[:, cols].astype(F32)
            a3v = a3_ref[:, cols].astype(F32)
            s = jax.nn.sigmoid(a1v)
            sl = a1v * s
            dact = _dot_nt(dhb, w2_ref[cols, :])
            act_ref[:, cols] = (sl * a3v).astype(BF16)
            da1 = (dact * a3v * s * (1.0 + a1v * (1.0 - s))).astype(BF16)
            da3 = (dact * sl).astype(BF16)
            da1_ref[:, cols] = da1
            da3_ref[:, cols] = da3
            part = _dot(da1, w1_ref[cols, :]) + _dot(da3, w3_ref[cols, :])
            dn = part if dn is None else dn + part
        dx, dg = _rms_bwd(dn, h_ref[...], g_ref[...])
        dg_ref[...] += dg
        dhi_ref[...] = dho + dx

    tok = lambda i: (i, 0)
    return pl.pallas_call(
        body, name=name, grid=(T // tm,),
        in_specs=[pl.BlockSpec((tm, D_MODEL), tok), _resident((1, D_MODEL)), pl.BlockSpec((tm, D_MODEL), tok),
                  pl.BlockSpec((tm, D_FF), tok), pl.BlockSpec((tm, D_FF), tok), _resident((D_FF, D_MODEL)),
                  _resident((D_FF, D_MODEL)), _resident((D_FF, D_MODEL))],
        out_specs=[pl.BlockSpec((tm, D_MODEL), tok), pl.BlockSpec((tm, D_FF), tok), pl.BlockSpec((tm, D_FF), tok),
                   pl.BlockSpec((tm, D_FF), tok), pl.BlockSpec((1, D_MODEL), lambda i: (0, 0))],
        out_shape=[_sds((T, D_MODEL), F32), _sds((T, D_FF), BF16), _sds((T, D_FF), BF16), _sds((T, D_FF), BF16),
                   _sds((1, D_MODEL), F32)],
        compiler_params=_params(("arbitrary",), FFN_VMEM_LIMIT),
    )(h, g, dho, a1, a3, w1, w3, w2)


def mm_tn(a, b, name, scale=1.0, a_cols=None, b_cols=None):
    T = a.shape[0]
    a_off, M = a_cols or (0, a.shape[1])
    b_off, N = b_cols or (0, b.shape[1])
    bm = 512 if M % 512 == 0 else (1408 if M == 2816 else 256)
    bn = N if N in (2176, 1408) else (1408 if N == 2816 else min(N, 1024))
    tk = _tile(T, MM_K_TILE)
    row_bytes = 2 * (bm * a.dtype.itemsize + bn * b.dtype.itemsize)
    while tk > TOKEN_TILE and tk * row_bytes > VMEM_LIMIT // 3:
        tk //= 2
    assert M % bm == 0 and N % bn == 0 and T % tk == 0 and a_off % bm == 0 and b_off % bn == 0
    n_k = T // tk
    m0, n0 = a_off // bm, b_off // bn

    def body(a_ref, b_ref, o_ref):
        k = pl.program_id(2)

        @pl.when(k == 0)
        def _():
            o_ref[...] = jnp.zeros_like(o_ref)

        o_ref[...] += _dot_tn(a_ref[...].astype(BF16), b_ref[...].astype(BF16))

        if scale != 1.0:
            @pl.when(k == n_k - 1)
            def _():
                o_ref[...] = o_ref[...] * scale

    return pl.pallas_call(
        body, name=name, grid=(M // bm, N // bn, n_k),
        in_specs=[pl.BlockSpec((tk, bm), lambda m, n, k: (k, m0 + m)), pl.BlockSpec((tk, bn), lambda m, n, k: (k, n0 + n))],
        out_specs=pl.BlockSpec((bm, bn), lambda m, n, k: (m, n)),
        out_shape=_sds((M, N), F32),
        compiler_params=_params(("arbitrary", "arbitrary", "arbitrary")),
    )(a, b)


def mixin_fwd(h1, g, w_in_r, b_f_pad):
    T = h1.shape[0]
    tm = _tile(T, TOKEN_TILE)

    def body(h_ref, g_ref, w_ref, bf_ref, u_ref, qkv_ref, s_ref, fz_ref, c_ref, carry):
        @pl.when(pl.program_id(0) == 0)
        def _():
            carry[...] = jnp.zeros_like(carry)

        x = h_ref[...]
        u = (x * _rms_scale(x) * g_ref[...]).astype(BF16)
        u_ref[...] = u
        z = _dot(u, w_ref[...])
        qkv_ref[...] = z[:, :3 * ATTN_W].astype(BF16)
        s_ref[...] = z[:, 3 * ATTN_W:3 * ATTN_W + SSM_W]
        fz = z[:, 3 * ATTN_W + SSM_W:] + bf_ref[...]
        fz_ref[...] = fz
        lane = lax.broadcasted_iota(jnp.int32, fz.shape, 1)
        logf = jnp.where(lane < N_HEADS, jnp.minimum(fz, 0.0) - jnp.log(1.0 + jnp.exp(-jnp.abs(fz))), 0.0)
        row = lax.broadcasted_iota(jnp.int32, (tm, tm), 0)
        col = lax.broadcasted_iota(jnp.int32, (tm, tm), 1)
        tri = (col <= row).astype(F32)
        cs = jnp.dot(tri, logf, precision=HIGHEST, preferred_element_type=F32) + carry[0:1, :]
        c_ref[...] = cs
        carry[...] = jnp.broadcast_to(cs[tm - 1:tm, :], carry.shape)

    tok = lambda i: (i, 0)
    fix = lambda i: (0, 0)
    return pl.pallas_call(
        body, name="mixin_fwd", grid=(T // tm,),
        in_specs=[pl.BlockSpec((tm, D_MODEL), tok), pl.BlockSpec((1, D_MODEL), fix),
                  pl.BlockSpec((D_MODEL, Z_COLS), fix), pl.BlockSpec((1, 128), fix)],
        out_specs=[pl.BlockSpec((tm, D_MODEL), tok), pl.BlockSpec((tm, 3 * ATTN_W), tok), pl.BlockSpec((tm, SSM_W), tok),
                   pl.BlockSpec((tm, 128), tok), pl.BlockSpec((tm, 128), tok)],
        out_shape=[_sds((T, D_MODEL), BF16), _sds((T, 3 * ATTN_W), BF16), _sds((T, SSM_W), F32),
                   _sds((T, 128), F32), _sds((T, 128), F32)],
        scratch_shapes=[pltpu.VMEM((8, 128), F32)],
        compiler_params=_params(("arbitrary",)),
    )(h1, g, w_in_r, b_f_pad)


def mixin_bwd(dh2, h1, g, w_in_r, dq, dk, dv, ds, dc, fz):
    T = h1.shape[0]
    tm = _tile(T, TOKEN_TILE)
    n_t = T // tm

    def body(dh2_ref, h_ref, g_ref, w_ref, dq_ref, dk_ref, dv_ref, ds_ref, dc_ref, fz_ref,
             dh1_ref, dz_ref, dg_ref, dbf_ref, carry):
        @pl.when(pl.program_id(0) == 0)
        def _():
            carry[...] = jnp.zeros_like(carry)
            dg_ref[...] = jnp.zeros_like(dg_ref)
            dbf_ref[...] = jnp.zeros_like(dbf_ref)

        row = lax.broadcasted_iota(jnp.int32, (tm, tm), 0)
        col = lax.broadcasted_iota(jnp.int32, (tm, tm), 1)
        tri = (col >= row).astype(F32)
        dlogf = jnp.dot(tri, dc_ref[...], precision=HIGHEST, preferred_element_type=F32) + carry[0:1, :]
        carry[...] = jnp.broadcast_to(dlogf[0:1, :], carry.shape)
        dfz = dlogf * jax.nn.sigmoid(-fz_ref[...])
        dbf_ref[...] += jnp.sum(dfz, axis=0, keepdims=True)
        dz = jnp.concatenate([_join_heads(dq_ref, BF16), _join_heads(dk_ref, BF16), _join_heads(dv_ref, BF16),
                              ds_ref[...], dfz], axis=1).astype(BF16)
        dz_ref[...] = dz
        du = _dot_nt(dz, w_ref[...])
        dx, dg = _rms_bwd(du, h_ref[...], g_ref[...])
        dg_ref[...] += dg
        dh1_ref[...] = dh2_ref[...] + dx

    tok = lambda i: (n_t - 1 - i, 0)
    fix = lambda i: (0, 0)
    heads = pl.BlockSpec((N_HEADS, tm, 128), lambda i: (0, n_t - 1 - i, 0))
    return pl.pallas_call(
        body, name="mixin_bwd", grid=(n_t,),
        in_specs=[pl.BlockSpec((tm, D_MODEL), tok), pl.BlockSpec((tm, D_MODEL), tok), pl.BlockSpec((1, D_MODEL), fix),
                  pl.BlockSpec((D_MODEL, Z_COLS), fix), heads, heads, heads, pl.BlockSpec((tm, SSM_W), tok),
                  pl.BlockSpec((tm, 128), tok), pl.BlockSpec((tm, 128), tok)],
        out_specs=[pl.BlockSpec((tm, D_MODEL), tok), pl.BlockSpec((tm, Z_COLS), tok), pl.BlockSpec((1, D_MODEL), fix),
                   pl.BlockSpec((1, 128), fix)],
        out_shape=[_sds((T, D_MODEL), F32), _sds((T, Z_COLS), BF16), _sds((1, D_MODEL), F32), _sds((1, 128), F32)],
        scratch_shapes=[pltpu.VMEM((8, 128), F32)],
        compiler_params=_params(("arbitrary",)),
    )(dh2, h1, g, w_in_r, dq, dk, dv, ds, dc, fz)


def _lane_move(src_lo, dst_lo, width, dtype):
    r = lax.broadcasted_iota(jnp.int32, (128, 128), 0)
    c = lax.broadcasted_iota(jnp.int32, (128, 128), 1)
    return ((c - dst_lo == r - src_lo) & (r >= src_lo) & (r < src_lo + width)).astype(dtype)


def _lane_const(lo, width, value):
    lane = lax.broadcasted_iota(jnp.int32, (1, 128), 1)
    return jnp.where((lane >= lo) & (lane < lo + width), value, 0.0).astype(F32)


def _pieces(a):
    hi = a.astype(BF16)
    rest = a - hi.astype(F32)
    mid = rest.astype(BF16)
    return hi, mid, (rest - mid.astype(F32)).astype(BF16)


def _head_features(pair_block, e):
    return _dot(pair_block, _lane_move(HEAD_DIM * e, 0, HEAD_DIM, BF16))


def _helper_columns(pieces, head, sign):
    out = None
    for k, piece in enumerate(pieces):
        term = _dot(piece, _lane_move(head, HEAD_DIM + k, 1, BF16))
        out = term if out is None else out + term
    return sign * out


def heads_in(qkv, cum):
    T = qkv.shape[0]
    tm = _tile(T, TOKEN_TILE)

    def body(qkv_ref, c_ref, q_ref, k_ref, v_ref):
        c = _pieces(c_ref[...])
        for h in range(N_HEADS):
            p, e = divmod(h, 2)
            blk = lambda base: qkv_ref[:, base + 128 * p:base + 128 * (p + 1)]
            q_ref[h] = (_head_features(blk(0), e) + _lane_const(HEAD_DIM, 3, -1.0)).astype(BF16)
            k_ref[h] = (_head_features(blk(ATTN_W), e) + _helper_columns(c, h, 1.0)
                        + _lane_const(HEAD_DIM + 3, 3, 1.0)).astype(BF16)
            v_ref[h] = (_head_features(blk(2 * ATTN_W), e) + _lane_const(HEAD_DIM, 3, 1.0)).astype(BF16)

    tok = lambda i: (i, 0)
    heads = pl.BlockSpec((N_HEADS, tm, 128), lambda i: (0, i, 0))
    return pl.pallas_call(
        body, name="heads_in", grid=(T // tm,),
        in_specs=[pl.BlockSpec((tm, 3 * ATTN_W), tok), pl.BlockSpec((tm, 128), tok)],
        out_specs=[heads] * 3, out_shape=[_sds((N_HEADS, T, 128), BF16)] * 3,
        compiler_params=_params(("arbitrary",)),
    )(qkv, cum)


def attn_fwd(q_aug, k_aug, v_aug):
    H, T, wd = q_aug.shape
    hd = HEAD_DIM
    tq = _tile(T, ATTN_TILE)
    n = T // tq

    def body(q_ref, k_ref, v_ref, o_ref, qb_ref, m_sc, acc, s_even, s_odd):
        qi = pl.program_id(1)
        qv = q_ref[0]
        m_sc[...] = jnp.full_like(m_sc, -jnp.inf)
        acc[...] = jnp.zeros_like(acc)

        def key_rows(j):
            return pl.ds(pl.multiple_of(jnp.minimum(j, qi) * tq, tq), tq)

        def logits(j, buf):
            buf[...] = _dot_nt(k_ref[0, key_rows(j), :], qv)

        def update(j, buf, masked):
            st = buf[...]
            if masked:
                keep = lax.broadcasted_iota(jnp.int32, (tq, tq), 0) <= lax.broadcasted_iota(jnp.int32, (tq, tq), 1)
                st = jnp.where(keep, st, -1e30)
            m_old = m_sc[...]
            m_new = jnp.maximum(m_old, jnp.max(st, axis=0, keepdims=True))
            pt = jnp.exp(st - m_new).astype(BF16)
            acc[...] = jnp.exp(m_old - m_new) * acc[...] + _dot_tn(v_ref[0, key_rows(j), :], pt)
            m_sc[...] = m_new

        logits(0, s_even)

        def two_tiles(p, carry):
            j = 2 * p
            logits(j + 1, s_odd)
            update(j, s_even, False)
            logits(j + 2, s_even)
            update(j + 1, s_odd, False)
            return carry

        lax.fori_loop(0, qi // 2, two_tiles, 0)

        @pl.when(qi % 2 == 0)
        def _():
            update(qi, s_even, True)

        @pl.when(qi % 2 == 1)
        def _():
            logits(qi, s_odd)
            update(qi - 1, s_even, False)
            update(qi, s_odd, True)

        total = acc[hd:hd + 1, :]
        o_ref[0] = (acc[...] / total).T
        hi, mid, lo = (t.astype(F32) for t in _pieces(-(m_sc[...] + jnp.log(total))))
        row = lax.broadcasted_iota(jnp.int32, (wd, tq), 0)
        lse_rows = jnp.where(row == hd + 3, hi, jnp.where(row == hd + 4, mid, jnp.where(row == hd + 5, lo, 0.0)))
        qb_ref[0] = (qv.astype(F32) + lse_rows.T).astype(BF16)

    qmap = lambda h, i: (h, i, 0)
    head = lambda h, i: (h, 0, 0)
    return pl.pallas_call(
        body, name="attn_fwd", grid=(H, n),
        in_specs=[pl.BlockSpec((1, tq, wd), qmap), pl.BlockSpec((1, T, wd), head), pl.BlockSpec((1, T, wd), head)],
        out_specs=[pl.BlockSpec((1, tq, wd), qmap), pl.BlockSpec((1, tq, wd), qmap)],
        out_shape=[_sds((H, T, wd), F32), _sds((H, T, wd), BF16)],
        scratch_shapes=[pltpu.VMEM((1, tq), F32), pltpu.VMEM((wd, tq), F32), pltpu.VMEM((tq, tq), F32),
                        pltpu.VMEM((tq, tq), F32)],
        compiler_params=_params(("arbitrary", "arbitrary")),
    )(q_aug, k_aug, v_aug)


def attn_bwd(q_aug, k_aug, v_aug, do_aug, pair=None):
    H, T, wd = q_aug.shape
    tq = _tile(T, ATTN_TILE)
    n = T // tq

    def compute(q_ref, do_ref, k_ref, v_ref, dq_ref, dk_ref, dv_ref, dc_ref, dck, s_a, d_a, s_b, d_b):
        j = pl.program_id(1)

        @pl.when(j == 0)
        def _():
            dq_ref[...] = jnp.zeros_like(dq_ref)
            dc_ref[...] = jnp.zeros_like(dc_ref)

        dk_ref[...] = jnp.zeros_like(dk_ref)
        dv_ref[...] = jnp.zeros_like(dv_ref)
        dck[...] = jnp.zeros_like(dck)
        kv, vv = k_ref[0], v_ref[0]

        def query_rows(i):
            return pl.ds(pl.multiple_of(jnp.minimum(i, n - 1) * tq, tq), tq)

        def products(i, s_buf, d_buf):
            rows = query_rows(i)
            s_buf[...] = _dot_nt(kv, q_ref[0, rows, :])
            d_buf[...] = _dot_nt(vv, do_ref[0, rows, :])

        def update(i, s_buf, d_buf, masked):
            rows = query_rows(i)
            qv, dov = q_ref[0, rows, :], do_ref[0, rows, :]
            pt = jnp.exp(s_buf[...])
            if masked:
                keep = lax.broadcasted_iota(jnp.int32, (tq, tq), 0) <= lax.broadcasted_iota(jnp.int32, (tq, tq), 1)
                pt = jnp.where(keep, pt, 0.0)
            dv_ref[0] += _dot(pt.astype(BF16), dov)
            dst = pt * d_buf[...]
            dsb = dst.astype(BF16)
            dk_ref[0] += _dot(dsb, qv)
            dq_ref[0, rows, :] += _dot_tn(dsb, kv)
            dck[...] += jnp.sum(dst, axis=1, keepdims=True)
            dc_ref[0, pl.ds(i, 1), :] += jnp.sum(dst, axis=0, keepdims=True)

        products(j, s_a, d_a)
        products(j + 1, s_b, d_b)
        update(j, s_a, d_a, True)
        left = n - 1 - j

        def two_tiles(p, carry):
            i = j + 1 + 2 * p
            products(i + 1, s_a, d_a)
            update(i, s_b, d_b, False)
            products(i + 2, s_b, d_b)
            update(i + 1, s_a, d_a, False)
            return carry

        lax.fori_loop(0, left // 2, two_tiles, 0)

        @pl.when(left % 2 == 1)
        def _():
            update(n - 1, s_b, d_b, False)

        dc_ref[0, pl.ds(j, 1), :] -= jnp.broadcast_to(dck[...], (tq, 128)).T[0:1, :]

    head = lambda h, j: (h, 0, 0)
    kmap = lambda h, j: (h, j, 0)
    in_specs = [pl.BlockSpec((1, T, wd), head), pl.BlockSpec((1, T, wd), head), pl.BlockSpec((1, tq, wd), kmap),
                pl.BlockSpec((1, tq, wd), kmap)]
    out_specs = [pl.BlockSpec((1, T, wd), head), pl.BlockSpec((1, tq, wd), kmap), pl.BlockSpec((1, tq, wd), kmap),
                 pl.BlockSpec((1, n, tq), head)]
    out_shape = [_sds((H, T, wd), F32), _sds((H, T, wd), F32), _sds((H, T, wd), F32), _sds((H, n, tq), F32)]
    scratch = [pltpu.VMEM((tq, 1), F32)] + [pltpu.VMEM((tq, tq), F32)] * 4
    operands = (q_aug, do_aug, k_aug, v_aug)
    if pair is None:
        body = compute
    else:
        def body(q_ref, do_ref, k_ref, v_ref, pair_ref, dq_ref, dk_ref, dv_ref, dc_ref, got_ref,
                 dck, s_a, d_a, s_b, d_b, send_sems, recv_sems):
            h, j = pl.program_id(0), pl.program_id(1)

            @pl.when((h == 0) & (j == 0))
            def _():
                for cp in _chip_copies(pair_ref, got_ref, send_sems, recv_sems):
                    cp.start()

            compute(q_ref, do_ref, k_ref, v_ref, dq_ref, dk_ref, dv_ref, dc_ref, dck, s_a, d_a, s_b, d_b)

            @pl.when((h == H - 1) & (j == n - 1))
            def _():
                for cp in _chip_copies(pair_ref, got_ref, send_sems, recv_sems):
                    cp.wait()

        sems = pltpu.SemaphoreType.DMA((3 * len(_spans(pair.shape[1], COPY_CHUNKS)),))
        in_specs, out_specs = in_specs + [_HBM], out_specs + [_HBM]
        out_shape = out_shape + [_sds((3,) + pair.shape[1:], pair.dtype)]
        scratch, operands = scratch + [sems, sems], operands + (pair,)
    return pl.pallas_call(
        body, name="attn_bwd", grid=(H, n), in_specs=in_specs, out_specs=out_specs, out_shape=out_shape,
        scratch_shapes=scratch, compiler_params=_params(("arbitrary", "arbitrary")),
    )(*operands)


def _complex_step(a_r, a_i, cr, ci, br, bi):
    return a_r * cr - a_i * ci + br, a_r * ci + a_i * cr + bi


_HALF_CH = SSM_W // 2
_HALF_ST = STATE_W // 2


def _state_cols(part, half):
    lo = part * STATE_W + half * _HALF_ST
    return slice(lo, lo + _HALF_ST)


def _channels_to_states(x, w_ref, out_ref):
    for half in range(2):
        ch = slice(half * _HALF_CH, (half + 1) * _HALF_CH)
        for part in range(2):
            cols = _state_cols(part, half)
            out_ref[:, cols] = _dot(x[:, ch], w_ref[ch, cols])


def _states_to_channels(x, w_ref):
    halves = []
    for half in range(2):
        ch = slice(half * _HALF_CH, (half + 1) * _HALF_CH)
        halves.append(_dot(x[:, _state_cols(0, half)], w_ref[_state_cols(0, half), ch])
                      + _dot(x[:, _state_cols(1, half)], w_ref[_state_cols(1, half), ch]))
    return jnp.concatenate(halves, axis=1)


def ssm_fwd(s_perm, wb, cbd, a_r, a_i, al_r, al_i, dvec):
    T = s_perm.shape[0]
    chunk = T // 8
    ts = _tile(chunk, SCAN_STEPS)
    tr, n_s = ts * 8, chunk // ts
    W, LB = STATE_W, SCAN_LANES

    def body(s_ref, wb_ref, cbd_ref, ar_ref, ai_ref, alr_ref, ali_ref, dv_ref, y_ref, xs_ref, bu, carry):
        ph, i = pl.program_id(0), pl.program_id(1)

        @pl.when((ph == 0) & (i == 0))
        def _():
            carry[...] = jnp.zeros_like(carry)

        _channels_to_states(s_ref[...].astype(BF16), wb_ref, bu)

        def scan(store):
            for lb in range(W // LB):
                lo = lb * LB
                re, im = slice(lo, lo + LB), slice(W + lo, W + lo + LB)
                ar = jnp.broadcast_to(ar_ref[:, re], (8, LB))
                ai = jnp.broadcast_to(ai_ref[:, re], (8, LB))

                def step(s, c):
                    rows = pl.ds(pl.multiple_of(s * 8, 8), 8)
                    nr, ni = _complex_step(ar, ai, c[0], c[1], bu[rows, re], bu[rows, im])
                    if store:
                        bu[rows, re] = nr
                        bu[rows, im] = ni
                    return nr, ni

                cr, ci = lax.fori_loop(0, ts, step, (carry[:, re], carry[:, im]), unroll=2)
                carry[:, re] = cr
                carry[:, im] = ci

        @pl.when(ph == 0)
        def _():
            scan(False)

            @pl.when(i == n_s - 1)
            def _():
                er, ei = carry[:, :W], carry[:, W:]
                alr = jnp.broadcast_to(alr_ref[...], (8, W))
                ali = jnp.broadcast_to(ali_ref[...], (8, W))
                first = lax.broadcasted_iota(jnp.int32, (8, W), 0) == 0
                sr, si = jnp.zeros((8, W), F32), jnp.zeros((8, W), F32)
                for _ in range(7):
                    vr, vi = _complex_step(alr, ali, sr, si, er, ei)
                    sr = jnp.where(first, 0.0, pltpu.roll(vr, 1, 0))
                    si = jnp.where(first, 0.0, pltpu.roll(vi, 1, 0))
                carry[:, :W] = sr
                carry[:, W:] = si

        @pl.when(ph == 1)
        def _():
            scan(True)
            xb = bu[...].astype(BF16)
            xs_ref[...] = xb
            y_ref[...] = _states_to_channels(xb, cbd_ref) + s_ref[...] * dv_ref[...]

    fix = lambda p, i: (0, 0)
    return pl.pallas_call(
        body, name="ssm_fwd", grid=(2, n_s),
        in_specs=[pl.BlockSpec((tr, SSM_W), lambda p, i: (i, 0)), pl.BlockSpec((SSM_W, 2 * W), fix),
                  pl.BlockSpec((2 * W, SSM_W), fix), pl.BlockSpec((1, W), fix), pl.BlockSpec((1, W), fix),
                  pl.BlockSpec((1, W), fix), pl.BlockSpec((1, W), fix), pl.BlockSpec((1, SSM_W), fix)],
        out_specs=[pl.BlockSpec((tr, SSM_W), lambda p, i: (i * p, 0)), pl.BlockSpec((tr, 2 * W), lambda p, i: (i * p, 0))],
        out_shape=[_sds((T, SSM_W), F32), _sds((T, 2 * W), BF16)],
        scratch_shapes=[pltpu.VMEM((tr, 2 * W), F32), pltpu.VMEM((8, 2 * W), F32)],
        compiler_params=_params(("arbitrary", "arbitrary")),
    )(s_perm, wb, cbd, a_r, a_i, al_r, al_i, dvec)


def ssm_bwd(dy_perm, s_perm, xs, cbd_t, wb_t, a_r, a_i, al_r, al_i, dvec):
    T = s_perm.shape[0]
    chunk = T // 8
    ts = _tile(chunk, SCAN_STEPS)
    tr, n_s = ts * 8, chunk // ts
    W, LB = STATE_W, SCAN_LANES

    def body(dy_ref, s_ref, xs_ref, cbt_ref, wbt_ref, ar_ref, ai_ref, alr_ref, ali_ref, dv_ref,
             du_ref, gs_ref, da_ref, dd_ref, gd, x32, carry):
        ph, i = pl.program_id(0), pl.program_id(1)

        @pl.when((ph == 0) & (i == 0))
        def _():
            carry[...] = jnp.zeros_like(carry)
            da_ref[...] = jnp.zeros_like(da_ref)
            dd_ref[...] = jnp.zeros_like(dd_ref)

        _channels_to_states(dy_ref[...].astype(BF16), cbt_ref, gd)

        def scan(store):
            for lb in range(W // LB):
                lo = lb * LB
                re, im = slice(lo, lo + LB), slice(W + lo, W + lo + LB)
                ar = jnp.broadcast_to(ar_ref[:, re], (8, LB))
                nai = -jnp.broadcast_to(ai_ref[:, re], (8, LB))

                def step(k, c):
                    rows = pl.ds(pl.multiple_of((ts - 1 - k) * 8, 8), 8)
                    cr, ci = c[0], c[1]
                    nr, ni = _complex_step(ar, nai, cr, ci, gd[rows, re], gd[rows, im])
                    if store:
                        xr, xi = x32[rows, re], x32[rows, im]
                        gd[rows, re] = nr
                        gd[rows, im] = ni
                        return nr, ni, c[2] + cr * xr + ci * xi, c[3] + ci * xr - cr * xi
                    return nr, ni

                init = (carry[:, re], carry[:, im])
                if store:
                    init = init + (da_ref[:, re], da_ref[:, im])
                out = lax.fori_loop(0, ts, step, init, unroll=2)
                carry[:, re] = out[0]
                carry[:, im] = out[1]
                if store:
                    da_ref[:, re] = out[2]
                    da_ref[:, im] = out[3]

        @pl.when(ph == 0)
        def _():
            scan(False)

            @pl.when(i == n_s - 1)
            def _():
                er, ei = carry[:, :W], carry[:, W:]
                alr = jnp.broadcast_to(alr_ref[...], (8, W))
                nali = -jnp.broadcast_to(ali_ref[...], (8, W))
                last = lax.broadcasted_iota(jnp.int32, (8, W), 0) == 7
                rr, ri = jnp.zeros((8, W), F32), jnp.zeros((8, W), F32)
                for _ in range(7):
                    vr, vi = _complex_step(alr, nali, rr, ri, er, ei)
                    rr = jnp.where(last, 0.0, pltpu.roll(vr, 7, 0))
                    ri = jnp.where(last, 0.0, pltpu.roll(vi, 7, 0))
                carry[:, :W] = rr
                carry[:, W:] = ri

        @pl.when(ph == 1)
        def _():
            x32[...] = xs_ref[...].astype(F32)
            scan(True)
            gb = gd[...].astype(BF16)
            gs_ref[...] = gb
            dy = dy_ref[...]
            du_ref[...] = _states_to_channels(gb, wbt_ref) + dy * dv_ref[...]
            dd_ref[...] += jnp.sum(dy * s_ref[...], axis=0, keepdims=True)

    fix = lambda p, i: (0, 0)
    rev = lambda p, i: (n_s - 1 - i, 0)
    rev_out = lambda p, i: (n_s - 1 - i * p, 0)
    return pl.pallas_call(
        body, name="ssm_bwd", grid=(2, n_s),
        in_specs=[pl.BlockSpec((tr, SSM_W), rev), pl.BlockSpec((tr, SSM_W), rev), pl.BlockSpec((tr, 2 * W), rev),
                  pl.BlockSpec((SSM_W, 2 * W), fix), pl.BlockSpec((2 * W, SSM_W), fix), pl.BlockSpec((1, W), fix),
                  pl.BlockSpec((1, W), fix), pl.BlockSpec((1, W), fix), pl.BlockSpec((1, W), fix),
                  pl.BlockSpec((1, SSM_W), fix)],
        out_specs=[pl.BlockSpec((tr, SSM_W), rev_out), pl.BlockSpec((tr, 2 * W), rev_out),
                   pl.BlockSpec((8, 2 * W), fix), pl.BlockSpec((1, SSM_W), fix)],
        out_shape=[_sds((T, SSM_W), F32), _sds((T, 2 * W), BF16), _sds((8, 2 * W), F32), _sds((1, SSM_W), F32)],
        scratch_shapes=[pltpu.VMEM((tr, 2 * W), F32), pltpu.VMEM((tr, 2 * W), F32), pltpu.VMEM((8, 2 * W), F32)],
        compiler_params=_params(("arbitrary", "arbitrary")),
    )(dy_perm, s_perm, xs, cbd_t, wb_t, a_r, a_i, al_r, al_i, dvec)


def _join_heads(ref, dtype):
    def move(h, dst):
        x = ref[h]
        pieces = _pieces(x) if dtype == F32 else (x.astype(BF16),)
        out = None
        for piece in pieces:
            term = _dot(piece, _lane_move(0, dst, HEAD_DIM, BF16))
            out = term if out is None else out + term
        return out

    return jnp.concatenate([move(2 * p, 0) + move(2 * p + 1, HEAD_DIM) for p in range(N_HEADS // 2)], axis=1)


def mixout_fwd(h1, o_heads, ypre, g_a, g_s, w_glu, b_glu, w_out):
    T = h1.shape[0]
    tm = _tile(T, TOKEN_TILE)

    def body(h_ref, at_ref, yp_ref, ga_ref, gs_ref, wg_ref, bg_ref, wo_ref, h2_ref, mixed_ref):
        yg, _ = _gelu_parts(yp_ref[...])
        gl = yg * jax.nn.sigmoid(_dot(yg.astype(BF16), wg_ref[...]) + bg_ref[...])
        at = _join_heads(at_ref, F32)
        mixed = jnp.concatenate([at * _rms_scale(at) * ga_ref[...], gl * _rms_scale(gl) * gs_ref[...]], axis=1)
        mixed = mixed.astype(BF16)
        mixed_ref[...] = mixed
        h2_ref[...] = h_ref[...] + _dot(mixed, wo_ref[...])

    tok = lambda i: (i, 0)
    fix = lambda i: (0, 0)
    return pl.pallas_call(
        body, name="mixout_fwd", grid=(T // tm,),
        in_specs=[pl.BlockSpec((tm, D_MODEL), tok), pl.BlockSpec((N_HEADS, tm, 128), lambda i: (0, i, 0)),
                  pl.BlockSpec((tm, SSM_W), tok),
                  pl.BlockSpec((1, ATTN_W), fix), pl.BlockSpec((1, SSM_W), fix), pl.BlockSpec((SSM_W, SSM_W), fix),
                  pl.BlockSpec((1, SSM_W), fix), pl.BlockSpec((D_MODEL, D_MODEL), fix)],
        out_specs=[pl.BlockSpec((tm, D_MODEL), tok), pl.BlockSpec((tm, D_MODEL), tok)],
        out_shape=[_sds((T, D_MODEL), F32), _sds((T, D_MODEL), BF16)],
        compiler_params=_params(("arbitrary",)),
    )(h1, o_heads, ypre, g_a, g_s, w_glu, b_glu, w_out)


def mixout_bwd(dh2, o_heads, ypre, g_a, g_s, w_glu, b_glu, w_out, seg):
    T = dh2.shape[0]
    tm = _tile(T, TOKEN_TILE)

    def body(dh_ref, at_ref, yp_ref, ga_ref, gs_ref, wg_ref, bg_ref, wo_ref, seg_ref,
             do_ref, dyp_ref, dpre_ref, yg_ref, dga_ref, dgs_ref, dbg_ref):
        @pl.when(pl.program_id(0) == 0)
        def _():
            dga_ref[...] = jnp.zeros_like(dga_ref)
            dgs_ref[...] = jnp.zeros_like(dgs_ref)
            dbg_ref[...] = jnp.zeros_like(dbg_ref)

        dmix = _dot_nt(dh_ref[...].astype(BF16), wo_ref[...])
        at = _join_heads(at_ref, F32)
        dat, dga = _rms_bwd(dmix[:, :ATTN_W], at, ga_ref[...])
        dga_ref[...] += dga
        delta = _pieces(jnp.dot(dat * at, seg_ref[...], precision=HIGHEST, preferred_element_type=F32))
        datb = dat.astype(BF16)
        for h in range(N_HEADS):
            p, e = divmod(h, 2)
            do_ref[h] = (_head_features(datb[:, 128 * p:128 * (p + 1)], e) + _helper_columns(delta, h, -1.0)).astype(BF16)
        yp = yp_ref[...]
        yg, t = _gelu_parts(yp)
        ygb = yg.astype(BF16)
        yg_ref[...] = ygb
        sg = jax.nn.sigmoid(_dot(ygb, wg_ref[...]) + bg_ref[...])
        dgl, dgs = _rms_bwd(dmix[:, ATTN_W:], yg * sg, gs_ref[...])
        dgs_ref[...] += dgs
        dpre = dgl * yg * sg * (1.0 - sg)
        dbg_ref[...] += jnp.sum(dpre, axis=0, keepdims=True)
        dpb = dpre.astype(BF16)
        dpre_ref[...] = dpb
        dyg = dgl * sg + _dot_nt(dpb, wg_ref[...])
        dyp_ref[...] = dyg * _gelu_grad(yp, t)

    tok = lambda i: (i, 0)
    fix = lambda i: (0, 0)
    heads = pl.BlockSpec((N_HEADS, tm, 128), lambda i: (0, i, 0))
    return pl.pallas_call(
        body, name="mixout_bwd", grid=(T // tm,),
        in_specs=[pl.BlockSpec((tm, D_MODEL), tok), heads, pl.BlockSpec((tm, SSM_W), tok),
                  pl.BlockSpec((1, ATTN_W), fix), pl.BlockSpec((1, SSM_W), fix), pl.BlockSpec((SSM_W, SSM_W), fix),
                  pl.BlockSpec((1, SSM_W), fix), pl.BlockSpec((D_MODEL, D_MODEL), fix), pl.BlockSpec((ATTN_W, 128), fix)],
        out_specs=[heads, pl.BlockSpec((tm, SSM_W), tok), pl.BlockSpec((tm, SSM_W), tok),
                   pl.BlockSpec((tm, SSM_W), tok), pl.BlockSpec((1, ATTN_W), fix),
                   pl.BlockSpec((1, SSM_W), fix), pl.BlockSpec((1, SSM_W), fix)],
        out_shape=[_sds((N_HEADS, T, 128), BF16), _sds((T, SSM_W), F32), _sds((T, SSM_W), BF16), _sds((T, SSM_W), BF16),
                   _sds((1, ATTN_W), F32), _sds((1, SSM_W), F32), _sds((1, SSM_W), F32)],
        compiler_params=_params(("arbitrary",)),
    )(dh2, o_heads, ypre, g_a, g_s, w_glu, b_glu, w_out, seg)


def head_fwd_bwd(h3, p, target, g_ple, g_final, w_gate, w_proj):
    T = h3.shape[0]
    tm = _tile(T, TOKEN_TILE)
    pd = p.shape[1]

    def body(h_ref, p_ref, tg_ref, gp_ref, gf_ref, wg_ref, wp_ref,
             dh_ref, n3_ref, dz_ref, dpp_ref, loss_ref, dgp_ref, dgf_ref):
        @pl.when(pl.program_id(0) == 0)
        def _():
            loss_ref[...] = jnp.zeros_like(loss_ref)
            dgp_ref[...] = jnp.zeros_like(dgp_ref)
            dgf_ref[...] = jnp.zeros_like(dgf_ref)

        x = h_ref[...]
        gp, gf = gp_ref[...], gf_ref[...]
        n3 = (x * _rms_scale(x) * gp).astype(BF16)
        n3_ref[...] = n3
        gate = jax.nn.sigmoid(_dot(n3, wg_ref[...]))
        pp = _dot(p_ref[...].astype(BF16), wp_ref[...])
        h4 = x + gate * pp
        y = h4 * _rms_scale(h4) * gf
        e = y - tg_ref[...]
        tile_loss = jnp.sum(jnp.sum(e * e, axis=1, keepdims=True), axis=0, keepdims=True) * (0.5 / D_MODEL)
        loss_ref[...] += jnp.broadcast_to(tile_loss, loss_ref.shape)
        dh4, dgf = _rms_bwd(e * (1.0 / D_MODEL), h4, gf)
        dgf_ref[...] += dgf
        dzg = dh4 * pp * gate * (1.0 - gate)
        dzb = dzg.astype(BF16)
        dz_ref[...] = dzb
        dpp_ref[...] = (dh4 * gate).astype(BF16)
        dx, dgp = _rms_bwd(_dot_nt(dzb, wg_ref[...]), x, gp)
        dgp_ref[...] += dgp
        dh_ref[...] = dh4 + dx

    tok = lambda i: (i, 0)
    fix = lambda i: (0, 0)
    return pl.pallas_call(
        body, name="head_fwd_bwd", grid=(T // tm,),
        in_specs=[pl.BlockSpec((tm, D_MODEL), tok), pl.BlockSpec((tm, pd), tok), pl.BlockSpec((tm, D_MODEL), tok),
                  pl.BlockSpec((1, D_MODEL), fix), pl.BlockSpec((1, D_MODEL), fix), pl.BlockSpec((D_MODEL, D_MODEL), fix),
                  pl.BlockSpec((pd, D_MODEL), fix)],
        out_specs=[pl.BlockSpec((tm, D_MODEL), tok), pl.BlockSpec((tm, D_MODEL), tok), pl.BlockSpec((tm, D_MODEL), tok),
                   pl.BlockSpec((tm, D_MODEL), tok), pl.BlockSpec((8, 128), fix), pl.BlockSpec((1, D_MODEL), fix),
                   pl.BlockSpec((1, D_MODEL), fix)],
        out_shape=[_sds((T, D_MODEL), F32), _sds((T, D_MODEL), BF16), _sds((T, D_MODEL), BF16), _sds((T, D_MODEL), BF16),
                   _sds((8, 128), F32), _sds((1, D_MODEL), F32), _sds((1, D_MODEL), F32)],
        compiler_params=_params(("arbitrary",)),
    )(h3, p, target, g_ple, g_final, w_gate, w_proj)


def _row_tile(rows, cols, n_arrays):
    lanes = -(-cols // 128) * 128
    cap = VMEM_LIMIT // 3 // (2 * n_arrays * lanes * 4)
    best = None
    for t in range(PACK_ALIGN, min(rows, cap) + 1, PACK_ALIGN):
        if rows % t == 0:
            best = t
    assert best is not None, (rows, cols)
    return best


def _adamw_math(w, g, m, v):
    nm = ADAM_B1 * m + (1.0 - ADAM_B1) * g
    nv = ADAM_B2 * v + (1.0 - ADAM_B2) * (g * g)
    c1 = 1.0 - ADAM_B1 ** ADAM_STEP
    c2 = 1.0 - ADAM_B2 ** ADAM_STEP
    return -ADAM_LR * ((nm / c1) / (jnp.sqrt(nv / c2) + ADAM_EPS) + ADAM_WD * w), nm, nv


def adamw(w, g, m, v, name):
    R, C = w.shape
    tr = _row_tile(R, C, 7)

    def body(w_ref, g_ref, m_ref, v_ref, d_ref, nm_ref, nv_ref):
        d_ref[...], nm_ref[...], nv_ref[...] = _adamw_math(w_ref[...], g_ref[...], m_ref[...], v_ref[...])

    spec = pl.BlockSpec((tr, C), lambda i: (i, 0))
    return pl.pallas_call(
        body, name=name, grid=(R // tr,), in_specs=[spec] * 4, out_specs=[spec] * 3,
        out_shape=[_sds((R, C), F32)] * 3, compiler_params=_params(("arbitrary",)),
    )(w, g, m, v)


def join_halves(mine, other, core):
    rh, C = mine.shape
    tr = _row_tile(rh, C, 3)
    nb = rh // tr

    def body(c_ref, m_ref, o_ref, out_ref):
        out_ref[...] = jnp.where((pl.program_id(0) // nb) == c_ref[0], m_ref[...], o_ref[...])

    half = pl.BlockSpec((tr, C), lambda i, c: (i % nb, 0))
    return pl.pallas_call(
        body, name="join_halves",
        grid_spec=pltpu.PrefetchScalarGridSpec(num_scalar_prefetch=1, grid=(2 * nb,), in_specs=[half, half],
                                               out_specs=pl.BlockSpec((tr, C), lambda i, c: (i, 0))),
        out_shape=_sds((2 * rh, C), F32), compiler_params=_params(("arbitrary",)),
    )(core, mine, other)


def pair_sum(g, theirs, core):
    n, R, C = g.shape
    rh = R // 2
    tr = _row_tile(rh, C, 3)
    nb = rh // tr

    def body(c_ref, g_ref, t_ref, o_ref):
        o_ref[...] = (g_ref[...] + t_ref[...]).astype(BF16)

    here = pl.BlockSpec((1, tr, C), lambda j, i, c: (j, i, 0))
    return pl.pallas_call(
        body, name="pair_sum",
        grid_spec=pltpu.PrefetchScalarGridSpec(
            num_scalar_prefetch=1, grid=(n, nb),
            in_specs=[pl.BlockSpec((1, tr, C), lambda j, i, c: (j, c[0] * nb + i, 0)), here], out_specs=here),
        out_shape=_sds((n, rh, C), BF16), compiler_params=_params(("arbitrary", "arbitrary")),
    )(core, g, theirs)


def chip_sum(pair, got, chip):
    _, R, C = pair.shape
    tr = _row_tile(R, C, 5)

    def body(c_ref, p_ref, g0_ref, g1_ref, g2_ref, o_ref):
        f = lambda ref: ref[0].astype(F32)
        o_ref[...] = ((f(p_ref) + f(g0_ref)) + f(g1_ref)) + f(g2_ref)

    slot = lambda k: pl.BlockSpec((1, tr, C), lambda i, c: (k, i, 0))
    return pl.pallas_call(
        body, name="chip_sum",
        grid_spec=pltpu.PrefetchScalarGridSpec(
            num_scalar_prefetch=1, grid=(R // tr,),
            in_specs=[pl.BlockSpec((1, tr, C), lambda i, c: (c[0], i, 0)), slot(0), slot(1), slot(2)],
            out_specs=pl.BlockSpec((tr, C), lambda i, c: (i, 0))),
        out_shape=_sds((R, C), F32), compiler_params=_params(("arbitrary",)),
    )(chip, pair, got, got, got)


_HBM = pl.BlockSpec(memory_space=pltpu.HBM)


def _place():
    x, y, c = lax.axis_index("x"), lax.axis_index("y"), lax.axis_index("c")
    return x, y, c, [(1 - x, y), (x, 1 - y), (1 - x, 1 - y)]


def _spans(rows, n):
    assert rows % PACK_ALIGN == 0
    tiles = rows // PACK_ALIGN
    n = min(n, tiles)
    cuts = [tiles * q // n for q in range(n + 1)]
    return [(cuts[q] * PACK_ALIGN, (cuts[q + 1] - cuts[q]) * PACK_ALIGN) for q in range(n)]


def _remote(src, dst, send_sem, recv_sem, to):
    return pltpu.make_async_remote_copy(src_ref=src, dst_ref=dst, send_sem=send_sem, recv_sem=recv_sem,
                                        device_id=to, device_id_type=MESH)


def allgather_shards(wp):
    R, C = wp.shape
    rh = R // 2
    spans = _spans(rh, COPY_CHUNKS)
    n_sp = len(spans)
    local_spans = _spans(R, 2 * COPY_CHUNKS)

    def body(w_ref, out_ref, send_sems, recv_sems, pass_send, pass_recv, local_sems):
        x, y, c, chips = _place()
        me = 2 * x + y
        local = []
        for q, (o, n) in enumerate(local_spans):
            cp = pltpu.make_async_copy(w_ref.at[pl.ds(o, n), :], out_ref.at[me, pl.ds(o, n), :], local_sems.at[q])
            cp.start()
            local.append(cp)
        sends = []
        for k, (cx, cy) in enumerate(chips):
            for q, (o, n) in enumerate(spans):
                rows = pl.ds(c * rh + o, n)
                cp = _remote(w_ref.at[rows, :], out_ref.at[me, rows, :], send_sems.at[k * n_sp + q],
                             recv_sems.at[k * n_sp + q], (cx, cy, c))
                cp.start()
                sends.append(cp)
        for q, (o, n) in enumerate(spans):
            for k, (cx, cy) in enumerate(chips):
                blk = out_ref.at[2 * cx + cy, pl.ds(c * rh + o, n), :]
                _remote(blk, blk, send_sems.at[k * n_sp + q], recv_sems.at[k * n_sp + q], (cx, cy, c)).wait_recv()
                cp = _remote(blk, blk, pass_send.at[k * n_sp + q], pass_recv.at[k * n_sp + q], (x, y, 1 - c))
                cp.start()
                sends.append(cp)
        for k, (cx, cy) in enumerate(chips):
            for q, (o, n) in enumerate(spans):
                blk = out_ref.at[2 * cx + cy, pl.ds((1 - c) * rh + o, n), :]
                _remote(blk, blk, pass_send.at[k * n_sp + q], pass_recv.at[k * n_sp + q], (x, y, 1 - c)).wait_recv()
        for cp in sends:
            cp.wait_send()
        for cp in local:
            cp.wait()

    sems = pltpu.SemaphoreType.DMA((3 * n_sp,))
    return pl.pallas_call(
        body, name="allgather_shards", in_specs=[_HBM], out_specs=_HBM, out_shape=_sds((4, R, C), wp.dtype),
        scratch_shapes=[sems, sems, sems, sems, pltpu.SemaphoreType.DMA((len(local_spans),))],
    )(wp)


def sibling_split(g):
    n_sl, R, C = g.shape
    rh = R // 2
    spans = _spans(rh, COPY_CHUNKS)
    n_sp = len(spans)

    def body(g_ref, got_ref, send_sems, recv_sems):
        x, y, c, _ = _place()
        copies = []
        for j in range(n_sl):
            for q, (o, n) in enumerate(spans):
                cp = _remote(g_ref.at[j, pl.ds((1 - c) * rh + o, n), :], got_ref.at[j, pl.ds(o, n), :],
                             send_sems.at[j * n_sp + q], recv_sems.at[j * n_sp + q], (x, y, 1 - c))
                cp.start()
                copies.append(cp)
        for cp in copies:
            cp.wait()

    sems = pltpu.SemaphoreType.DMA((n_sl * n_sp,))
    return pl.pallas_call(
        body, name="sibling_split", in_specs=[_HBM], out_specs=_HBM, out_shape=_sds((n_sl, rh, C), g.dtype),
        scratch_shapes=[sems, sems],
    )(g)


def _chip_copies(p_ref, buf_ref, send_sems, recv_sems):
    rows = p_ref.shape[1]
    spans = _spans(rows, COPY_CHUNKS)
    x, y, c, chips = _place()
    copies = []
    for k, (cx, cy) in enumerate(chips):
        for q, (o, n) in enumerate(spans):
            copies.append(_remote(p_ref.at[2 * cx + cy, pl.ds(o, n), :], buf_ref.at[k, pl.ds(o, n), :],
                                  send_sems.at[k * len(spans) + q], recv_sems.at[k * len(spans) + q], (cx, cy, c)))
    return copies


def chip_exchange(p):
    _, R, C = p.shape

    def body(p_ref, buf_ref, send_sems, recv_sems):
        copies = _chip_copies(p_ref, buf_ref, send_sems, recv_sems)
        for cp in copies:
            cp.start()
        for cp in copies:
            cp.wait()

    sems = pltpu.SemaphoreType.DMA((3 * len(_spans(R, COPY_CHUNKS)),))
    return pl.pallas_call(
        body, name="chip_exchange", in_specs=[_HBM], out_specs=_HBM, out_shape=_sds((3, R, C), p.dtype),
        scratch_shapes=[sems, sems],
    )(p)


def sibling_swap(half):
    R, C = half.shape
    spans = _spans(R, COPY_CHUNKS)

    def body(h_ref, got_ref, send_sems, recv_sems):
        x, y, c, _ = _place()
        copies = []
        for q, (o, n) in enumerate(spans):
            cp = _remote(h_ref.at[pl.ds(o, n), :], got_ref.at[pl.ds(o, n), :], send_sems.at[q], recv_sems.at[q], (x, y, 1 - c))
            cp.start()
            copies.append(cp)
        for cp in copies:
            cp.wait()

    sems = pltpu.SemaphoreType.DMA((len(spans),))
    return pl.pallas_call(
        body, name="sibling_swap", in_specs=[_HBM], out_specs=_HBM, out_shape=_sds((R, C), half.dtype),
        scratch_shapes=[sems, sems],
    )(half)


def allreduce_small(v):
    R, C = v.shape

    def body(v_ref, out_ref, buf, send_sems, recv_sems):
        x, y, c, _ = _place()
        me = 4 * x + 2 * y + c
        buf[me] = v_ref[...]
        flips = [((k >> 2) & 1, (k >> 1) & 1, k & 1) for k in range(1, 8)]
        sends = []
        for k, (fx, fy, fc) in enumerate(flips):
            to = (1 - x if fx else x, 1 - y if fy else y, 1 - c if fc else c)
            cp = _remote(v_ref, buf.at[me], send_sems.at[k], recv_sems.at[k], to)
            cp.start()
            sends.append(cp)
        for k, (fx, fy, fc) in enumerate(flips):
            px, py, pc = (1 - x if fx else x, 1 - y if fy else y, 1 - c if fc else c)
            blk = buf.at[4 * px + 2 * py + pc]
            _remote(blk, blk, send_sems.at[k], recv_sems.at[k], (px, py, pc)).wait_recv()
        for cp in sends:
            cp.wait_send()
        acc = buf[0]
        for s in range(1, 8):
            acc = acc + buf[s]
        out_ref[...] = acc

    vm = pl.BlockSpec(memory_space=pltpu.VMEM)
    return pl.pallas_call(
        body, name="allreduce_small", in_specs=[vm], out_specs=vm, out_shape=_sds((R, C), F32),
        scratch_shapes=[pltpu.VMEM((8, R, C), F32), pltpu.SemaphoreType.DMA((7,)), pltpu.SemaphoreType.DMA((7,))],
        compiler_params=pltpu.CompilerParams(vmem_limit_bytes=VMEM_LIMIT),
    )(v)


def _rows_of(shape):
    return shape[0] * shape[1] // PACK_COLS


def _slot_rows(shape):
    return -(-_rows_of(shape) // PACK_ALIGN) * PACK_ALIGN


TRANSPOSED = ("w1_a", "w3_a", "w1_b", "w3_b")


def _stored(name, shard):
    return shard[0].T if name in TRANSPOSED else shard[0]


def _restored(name, stored):
    return stored.T[None] if name in TRANSPOSED else stored[None]


def _pack_shards(shards, dtype):
    parts = []
    for name, shape, _ in BIG:
        part = _stored(name, shards[name]).reshape(_rows_of(shape), PACK_COLS).astype(dtype)
        parts.append(jnp.pad(part, ((0, _slot_rows(shape) - part.shape[0]), (0, 0))))
    used = sum(p.shape[0] for p in parts)
    parts.append(jnp.zeros((PACK_ROWS - used, PACK_COLS), dtype))
    return jnp.concatenate(parts, axis=0)


def _unpack_gathered(ag):
    out, off = {}, 0
    for name, shape, axis in BIG:
        r = _rows_of(shape)
        piece = ag[:, off:off + r, :]
        off += _slot_rows(shape)
        if name in TRANSPOSED:
            out[name] = piece.reshape(4 * r, PACK_COLS)
        elif axis == 0:
            out[name] = piece.reshape(4 * shape[0], shape[1])
        else:
            out[name] = piece.reshape((4,) + shape).transpose(1, 0, 2).reshape(shape[0], 4 * shape[1])
    return out


LATE = ("w_glu", "w_out", "w1_b", "w3_b", "w2_b", "w_ple_gate", "w_ple_proj")
GRAD_GROUPS = (tuple(e for e in BIG if e[0] in LATE), tuple(e for e in BIG if e[0] not in LATE))


def _group_rows(entries):
    used = sum(_slot_rows(shape) for _, shape, _ in entries)
    return -(-used // (2 * PACK_ALIGN)) * 2 * PACK_ALIGN


def _pack_full_grads(grads, entries):
    parts = []
    for name, shape, axis in entries:
        g = grads[name]
        if name in TRANSPOSED or axis == 0:
            piece = g.reshape(4, _rows_of(shape), PACK_COLS)
        else:
            piece = g.reshape(shape[0], 4, shape[1]).transpose(1, 0, 2).reshape(4, _rows_of(shape), PACK_COLS)
        parts.append(jnp.pad(piece, ((0, 0), (0, _slot_rows(shape) - piece.shape[1]), (0, 0))))
    used = sum(p.shape[1] for p in parts)
    if _group_rows(entries) > used:
        parts.append(jnp.zeros((4, _group_rows(entries) - used, PACK_COLS), F32))
    return jnp.concatenate(parts, axis=1)


def _unpack_shards(packed, entries):
    out, off = {}, 0
    for name, shape, _ in entries:
        r = _rows_of(shape)
        out[name] = packed[off:off + r] if name in TRANSPOSED else packed[off:off + r].reshape(shape)
        off += _slot_rows(shape)
    return out


def _small_rows(shape):
    return -(-math.prod(shape) // 1024) * 8


def _pack_small(vals, extra=None):
    def slot(v, rows):
        flat = v.reshape(-1)
        return jnp.pad(flat, (0, rows * 128 - flat.shape[0])).reshape(rows, 128)

    parts = [slot(vals[name], _small_rows(shape)) for name, shape in SMALL]
    parts.append(slot(extra if extra is not None else jnp.zeros((1,), F32), 8))
    assert sum(p.shape[0] for p in parts) == SMALL_ROWS
    return jnp.concatenate(parts, axis=0)


def _unpack_small(packed):
    out, off = {}, 0
    for name, shape in SMALL:
        rows = _small_rows(shape)
        out[name] = packed[off:off + rows].reshape(-1)[:math.prod(shape)].reshape(shape)
        off += rows
    return out, packed[off, 0]


def _permute_time(a):
    T, n = a.shape
    return a.reshape(8, T // 8, n).transpose(1, 0, 2).reshape(T, n)


def _unpermute_time(a):
    T, n = a.shape
    return a.reshape(T // 8, 8, n).transpose(1, 0, 2).reshape(T, n)


def _discretize(a_re, a_im, log_dt, b_re, b_im):
    dt = jnp.exp(log_dt)[:, None]
    decay = jnp.exp(dt * a_re)
    abar_r = decay * jnp.cos(dt * a_im)
    abar_i = decay * jnp.sin(dt * a_im)
    nr, ni = abar_r - 1.0, abar_i
    den = a_re * a_re + a_im * a_im
    fr = (nr * a_re + ni * a_im) / den
    fi = (ni * a_re - nr * a_im) / den
    bbar_r = fr[..., None] * b_re - fi[..., None] * b_im
    bbar_i = fr[..., None] * b_im + fi[..., None] * b_re
    return abar_r, abar_i, bbar_r, bbar_i


def _input_matrix(bbar_r, bbar_i):
    eye = jnp.eye(N_GROUPS, dtype=F32)
    blk = lambda b: jnp.einsum("ghp,gk->ghkp", b.transpose(0, 2, 1), eye).reshape(SSM_W, STATE_W)
    return jnp.concatenate([blk(bbar_r), blk(bbar_i)], axis=1)


def _output_matrix(c_re, c_im):
    eye = jnp.eye(N_GROUPS, dtype=F32)
    blk = lambda cm: jnp.einsum("ghp,gk->gpkh", cm, eye).reshape(STATE_W, SSM_W)
    return jnp.concatenate([blk(c_re), -blk(c_im)], axis=0)


def _state_power(ar, ai, n):
    steps = int(round(math.log2(n)))
    assert 1 << steps == n
    for _ in range(steps):
        ar, ai = ar * ar - ai * ai, 2.0 * ar * ai
    return ar, ai


def kernel(x, p, g_ffn1, w1_a, w3_a, w2_a, g_mix, w_in, b_f, a_re, a_im, log_dt, b_re, b_im, c_re, c_im, d_skip, w_glu, b_glu, g_attn_out, g_ssm_out, w_out, g_ffn2, w1_b, w3_b, w2_b, g_ple, w_ple_gate, w_ple_proj, g_final, loss_target, m_g_ffn1, m_w1_a, m_w3_a, m_w2_a, m_g_mix, m_w_in, m_b_f, m_a_re, m_a_im, m_log_dt, m_b_re, m_b_im, m_c_re, m_c_im, m_d_skip, m_w_glu, m_b_glu, m_g_attn_out, m_g_ssm_out, m_w_out, m_g_ffn2, m_w1_b, m_w3_b, m_w2_b, m_g_ple, m_w_ple_gate, m_w_ple_proj, m_g_final, v_g_ffn1, v_w1_a, v_w3_a, v_w2_a, v_g_mix, v_w_in, v_b_f, v_a_re, v_a_im, v_log_dt, v_b_re, v_b_im, v_c_re, v_c_im, v_d_skip, v_w_glu, v_b_glu, v_g_attn_out, v_g_ssm_out, v_w_out, v_g_ffn2, v_w1_b, v_w3_b, v_w2_b, v_g_ple, v_w_ple_gate, v_w_ple_proj, v_g_final):
    args = dict(locals())
    weights = {n: args[n] for n in WEIGHT_ORDER}
    moms = {n: args["m_" + n] for n in WEIGHT_ORDER}
    vars_ = {n: args["v_" + n] for n in WEIGHT_ORDER}
    T = x.shape[1]
    x2, p2, tgt = x[0], p[0, 0], loss_target[0]

    full = _unpack_gathered(allgather_shards(_pack_shards(weights, BF16)))
    core = lax.axis_index("c").astype(jnp.int32).reshape(1)
    chip = (2 * lax.axis_index("x") + lax.axis_index("y")).astype(jnp.int32).reshape(1)
    loss_part, dx, grads, late = _local_step(x2, p2, tgt, {n: weights[n] for n, _ in SMALL}, full,
                                             early_exchange=lambda g: _pair_of(g, GRAD_GROUPS[0], core))
    return _reduce_and_update(weights, moms, vars_, loss_part, dx, grads, core, chip, late)


def _pair_of(grads, entries, core):
    packed = _pack_full_grads(grads, entries)
    return pair_sum(packed, sibling_split(packed), core)


def _local_step(x2, p2, tgt, sm, full, early_exchange=None):
    T = x2.shape[0]
    (g_ffn1, g_mix, b_f, a_re, a_im, log_dt, b_re, b_im, c_re, c_im, d_skip, b_glu, g_attn_out, g_ssm_out, g_ffn2, g_ple,
     g_final) = (sm[n] for n, _ in SMALL)
    w_in_f = full["w_in"]
    w_in_r = jnp.concatenate([w_in_f[:, :ATTN_W] * QK_SCALE, w_in_f[:, ATTN_W:3 * ATTN_W], w_in_f[:, 3 * ATTN_W + N_HEADS:],
                              w_in_f[:, 3 * ATTN_W:3 * ATTN_W + N_HEADS], jnp.zeros((D_MODEL, 128 - N_HEADS), BF16)], axis=1)
    b_f_pad = jnp.pad(b_f, ((0, 0), (0, 128 - N_HEADS)))

    disc_in = (a_re[0], a_im[0], log_dt[0], b_re[0], b_im[0])
    (abar_r, abar_i, bbar_r, bbar_i), disc_vjp = jax.vjp(_discretize, *disc_in)
    wb = _input_matrix(bbar_r, bbar_i)
    cbd = _output_matrix(c_re[0], c_im[0])
    ar, ai = abar_r.reshape(1, STATE_W), abar_i.reshape(1, STATE_W)
    alr, ali = _state_power(ar, ai, T // 8)
    dvec = d_skip.reshape(1, SSM_W)
    wb16, cbd16 = wb.astype(BF16), cbd.astype(BF16)

    h1, a1a, a3a, n1 = ffn_fwd(x2, g_ffn1, full["w1_a"], full["w3_a"], full["w2_a"], "ffn_a_fwd")
    u, qkv, s_in, fz, cum = mixin_fwd(h1, g_mix, w_in_r, b_f_pad)
    q_aug, k_aug, v_aug = heads_in(qkv, cum)
    o_heads, q_bwd = attn_fwd(q_aug, k_aug, v_aug)
    s_perm = _permute_time(s_in)
    y_perm, xs = ssm_fwd(s_perm, wb16, cbd16, ar, ai, alr, ali, dvec)
    ypre = _unpermute_time(y_perm)
    h2, mixed = mixout_fwd(h1, o_heads, ypre, g_attn_out, g_ssm_out, full["w_glu"], b_glu, full["w_out"])
    h3, a1b, a3b, n2 = ffn_fwd(h2, g_ffn2, full["w1_b"], full["w3_b"], full["w2_b"], "ffn_b_fwd")

    dh3, n3, dzg, dpp, loss_part, dg_ple, dg_final = head_fwd_bwd(
        h3, p2, tgt, g_ple, g_final.reshape(1, D_MODEL), full["w_ple_gate"], full["w_ple_proj"])
    grads = {"g_ple": dg_ple, "g_final": dg_final.reshape(D_MODEL)}
    grads["w_ple_gate"] = mm_tn(n3, dzg, "dw_ple_gate")
    grads["w_ple_proj"] = mm_tn(p2, dpp, "dw_ple_proj")

    dh2, da1, da3, act, grads["g_ffn2"] = ffn_bwd(h2, g_ffn2, dh3, a1b, a3b, full["w1_b"], full["w3_b"], full["w2_b"], "ffn_b_bwd")
    grads["w1_b"] = mm_tn(da1, n2, "dw1_b")
    grads["w3_b"] = mm_tn(da3, n2, "dw3_b")
    grads["w2_b"] = mm_tn(act, dh3, "dw2_b", scale=0.5)

    seg = (jnp.arange(ATTN_W)[:, None] // HEAD_DIM == jnp.arange(128)[None, :]).astype(F32)
    do_aug, dypre, dpre, yg, grads["g_attn_out"], grads["g_ssm_out"], grads["b_glu"] = mixout_bwd(
        dh2, o_heads, ypre, g_attn_out, g_ssm_out, full["w_glu"], b_glu, full["w_out"], seg)
    grads["w_out"] = mm_tn(mixed, dh2, "dw_out")
    grads["w_glu"] = mm_tn(yg, dpre, "dw_glu")

    if early_exchange is None:
        late = None
        dq_aug, dk_aug, dv_aug, dc_rows = attn_bwd(q_bwd, k_aug, v_aug, do_aug)
    else:
        pair_late = early_exchange(grads)
        dq_aug, dk_aug, dv_aug, dc_rows, got_late = attn_bwd(q_bwd, k_aug, v_aug, do_aug, pair=pair_late)
        late = (pair_late, got_late)
    dc = jnp.pad(dc_rows.reshape(N_HEADS, T).T, ((0, 0), (0, 128 - N_HEADS)))

    dy_perm = _permute_time(dypre)
    du_perm, gs, d_a, dd = ssm_bwd(dy_perm, s_perm, xs, cbd16.T, wb16.T, ar, ai, alr, ali, dvec)
    ds_in = _unpermute_time(du_perm)
    hg = N_GROUPS // 2
    d_in, d_out = [], []
    for part in range(2):
        ins, outs = [], []
        for half in range(2):
            states = (part * STATE_W + half * _HALF_ST, _HALF_ST)
            chans = (half * _HALF_CH, _HALF_CH)
            blk = mm_tn(s_perm, gs, f"dw_ssm_in_{part}{half}", a_cols=chans, b_cols=states)
            ins.append(jnp.einsum("ghgp->ghp", blk.reshape(hg, GROUP_CH, hg, N_STATE)))
            blk = mm_tn(xs, dy_perm, f"dw_ssm_out_{part}{half}", a_cols=states, b_cols=chans)
            outs.append(jnp.einsum("gpgh->gph", blk.reshape(hg, N_STATE, hg, GROUP_CH)))
        d_in.append(jnp.concatenate(ins, axis=0).transpose(0, 2, 1))
        d_out.append(jnp.concatenate(outs, axis=0).transpose(0, 2, 1))
    d_abar_r = jnp.sum(d_a[:, :STATE_W], axis=0).reshape(N_GROUPS, N_STATE)
    d_abar_i = jnp.sum(d_a[:, STATE_W:], axis=0).reshape(N_GROUPS, N_STATE)
    d_disc = disc_vjp((d_abar_r, d_abar_i, d_in[0], d_in[1]))
    for name, val in zip(("a_re", "a_im", "log_dt", "b_re", "b_im"), d_disc):
        grads[name] = val[None]
    grads["c_re"] = d_out[0][None]
    grads["c_im"] = -d_out[1][None]
    grads["d_skip"] = dd.reshape(1, N_GROUPS, GROUP_CH)

    dh1, dz, grads["g_mix"], dbf = mixin_bwd(dh2, h1, g_mix, w_in_r, dq_aug, dk_aug, dv_aug, ds_in, dc, fz)
    grads["b_f"] = dbf[:, :N_HEADS]
    d_w_in_r = mm_tn(u, dz, "dw_in")
    grads["w_in"] = jnp.concatenate([d_w_in_r[:, :ATTN_W] * QK_SCALE, d_w_in_r[:, ATTN_W:3 * ATTN_W],
                                     d_w_in_r[:, 3 * ATTN_W + SSM_W:3 * ATTN_W + SSM_W + N_HEADS],
                                     d_w_in_r[:, 3 * ATTN_W:3 * ATTN_W + SSM_W]], axis=1)

    dx, da1, da3, act, grads["g_ffn1"] = ffn_bwd(x2, g_ffn1, dh1, a1a, a3a, full["w1_a"], full["w3_a"], full["w2_a"], "ffn_a_bwd")
    grads["w1_a"] = mm_tn(da1, n1, "dw1_a")
    grads["w3_a"] = mm_tn(da3, n1, "dw3_a")
    grads["w2_a"] = mm_tn(act, dh1, "dw2_a", scale=0.5)
    return loss_part, dx, grads, late


def _reduce_and_update(weights, moms, vars_, loss_part, dx, grads, core, chip, late):
    pair_early = _pair_of(grads, GRAD_GROUPS[1], core)
    g_stored = {}
    for entries, (pair, got) in zip(GRAD_GROUPS, (late, (pair_early, chip_exchange(pair_early)))):
        half = chip_sum(pair, got, chip)
        g_stored.update(_unpack_shards(join_halves(half, sibling_swap(half), core), entries))
    g_out, d_out, m_out, v_out = {}, {}, {}, {}
    for n, _, _ in BIG:
        d, m, v = adamw(_stored(n, weights[n]), g_stored[n], _stored(n, moms[n]), _stored(n, vars_[n]), "adamw_" + n)
        g_out[n], d_out[n], m_out[n], v_out[n] = (_restored(n, a) for a in (g_stored[n], d, m, v))

    small = allreduce_small(_pack_small({n: grads[n] for n, _ in SMALL}, extra=loss_part[0, 0]))
    d_small, m_small, v_small = adamw(_pack_small(weights), small, _pack_small(moms), _pack_small(vars_), "adamw_small")

    g_small, loss = _unpack_small(small)
    g_out.update(g_small)
    outs = []
    for big, sm in ((d_out, d_small), (m_out, m_small), (v_out, v_small)):
        o, _ = _unpack_small(sm)
        o.update(big)
        outs.append(o)
    result = [loss, dx[None]] + [g_out[n] for n in WEIGHT_ORDER]
    for o in outs:
        result += [o[n] for n in WEIGHT_ORDER]
    return tuple(result)
```

```python
import functools
import math

import jax
import jax.numpy as jnp
from jax import lax
from jax.experimental import pallas as pl
from jax.experimental.pallas import tpu as pltpu

F32 = jnp.float32
BF16 = jnp.bfloat16

D_MODEL = 1024
D_FF = 2816
N_HEADS = 8
HEAD_DIM = 64
ATTN_W = 512
SSM_W = 512
N_GROUPS = 32
N_STATE = 64
GROUP_CH = 16
STATE_W = N_GROUPS * N_STATE
Z_COLS = 2176
QK_SCALE = 0.125
EPS = 1e-6

ADAM_LR = 0.001
ADAM_B1 = 0.9
ADAM_B2 = 0.999
ADAM_EPS = 1e-08
ADAM_WD = 0.01
ADAM_STEP = 10

TOKEN_TILE = 512
FFN_TOKEN_TILE = 256
FF_CHUNK = 1408
MM_K_TILE = 2048
ATTN_TILE = 512
SCAN_STEPS = 32
SCAN_LANES = 512
VMEM_LIMIT = 48 * 1024 * 1024
FFN_VMEM_LIMIT = 56 * 1024 * 1024
COPY_CHUNKS = 4

NT_DIMS = (((1,), (1,)), ((), ()))
TN_DIMS = (((0,), (0,)), ((), ()))
HIGHEST = lax.Precision.HIGHEST
MESH = pl.DeviceIdType.MESH

BIG = (
    ("w1_a", (1024, 704), 1), ("w3_a", (1024, 704), 1), ("w2_a", (704, 1024), 0),
    ("w_in", (1024, 514), 1), ("w_glu", (128, 512), 0), ("w_out", (256, 1024), 0),
    ("w1_b", (1024, 704), 1), ("w3_b", (1024, 704), 1), ("w2_b", (704, 1024), 0),
    ("w_ple_gate", (256, 1024), 0), ("w_ple_proj", (256, 256), 1),
)
PACK_COLS = 1024
PACK_ALIGN = 16
PACK_ROWS = 5408
SMALL = (
    ("g_ffn1", (1, 1024)), ("g_mix", (1, 1024)), ("b_f", (1, 8)), ("a_re", (1, 32, 64)), ("a_im", (1, 32, 64)),
    ("log_dt", (1, 32)), ("b_re", (1, 32, 64, 16)), ("b_im", (1, 32, 64, 16)), ("c_re", (1, 32, 16, 64)),
    ("c_im", (1, 32, 16, 64)), ("d_skip", (1, 32, 16)), ("b_glu", (1, 512)), ("g_attn_out", (1, 512)),
    ("g_ssm_out", (1, 512)), ("g_ffn2", (1, 1024)), ("g_ple", (1, 1024)), ("g_final", (1024,)),
)
SMALL_ROWS = 1152
WEIGHT_ORDER = ("g_ffn1", "w1_a", "w3_a", "w2_a", "g_mix", "w_in", "b_f", "a_re", "a_im", "log_dt", "b_re", "b_im",
                "c_re", "c_im", "d_skip", "w_glu", "b_glu", "g_attn_out", "g_ssm_out", "w_out", "g_ffn2", "w1_b",
                "w3_b", "w2_b", "g_ple", "w_ple_gate", "w_ple_proj", "g_final")


def _params(sem=None, vmem=VMEM_LIMIT):
    kw = dict(vmem_limit_bytes=vmem)
    if sem is not None:
        kw["dimension_semantics"] = sem
    return pltpu.CompilerParams(**kw)


def _sds(shape, dtype):
    return jax.ShapeDtypeStruct(shape, dtype)


def _tile(n, pref):
    t = min(n, pref)
    assert n % t == 0, (n, pref)
    return t


def _rms_scale(x):
    return lax.rsqrt(jnp.mean(x * x, axis=-1, keepdims=True) + EPS)


def _rms_bwd(dy, x, g):
    r = _rms_scale(x)
    xh = x * r
    dxh = dy * g
    dx = r * (dxh - xh * jnp.mean(dxh * xh, axis=-1, keepdims=True))
    return dx, jnp.sum(dy * xh, axis=0, keepdims=True)


def _dot(a, b):
    return jnp.dot(a, b, preferred_element_type=F32)


def _dot_nt(a, b):
    return lax.dot_general(a, b, NT_DIMS, preferred_element_type=F32)


def _dot_tn(a, b):
    return lax.dot_general(a, b, TN_DIMS, preferred_element_type=F32)


_GELU_C = math.sqrt(2.0 / math.pi)


def _gelu_parts(x):
    t = jnp.tanh(_GELU_C * (x + 0.044715 * x * x * x))
    return 0.5 * x * (1.0 + t), t


def _gelu_grad(x, t):
    return 0.5 * (1.0 + t) + 0.5 * x * (1.0 - t * t) * _GELU_C * (1.0 + 3.0 * 0.044715 * x * x)


def _resident(shape):
    return pl.BlockSpec(shape, lambda i: (0,) * len(shape), pipeline_mode=pl.Buffered(1))


def ffn_fwd(h, g, w1, w3, w2, name):
    T = h.shape[0]
    tm = _tile(T, FFN_TOKEN_TILE)

    def body(h_ref, g_ref, w1_ref, w3_ref, w2_ref, ho_ref, a1_ref, a3_ref, n_ref):
        x = h_ref[...]
        n = (x * _rms_scale(x) * g_ref[...]).astype(BF16)
        n_ref[...] = n
        out = x
        for lo in range(0, D_FF, FF_CHUNK):
            cols = slice(lo, lo + FF_CHUNK)
            a1 = _dot_nt(n, w1_ref[cols, :])
            a3 = _dot_nt(n, w3_ref[cols, :])
            a1_ref[:, cols] = a1.astype(BF16)
            a3_ref[:, cols] = a3.astype(BF16)
            act = (a1 * jax.nn.sigmoid(a1) * a3).astype(BF16)
            out = out + 0.5 * _dot(act, w2_ref[cols, :])
        ho_ref[...] = out

    tok = lambda i: (i, 0)
    return pl.pallas_call(
        body, name=name, grid=(T // tm,),
        in_specs=[pl.BlockSpec((tm, D_MODEL), tok), _resident((1, D_MODEL)), _resident((D_FF, D_MODEL)),
                  _resident((D_FF, D_MODEL)), _resident((D_FF, D_MODEL))],
        out_specs=[pl.BlockSpec((tm, D_MODEL), tok), pl.BlockSpec((tm, D_FF), tok), pl.BlockSpec((tm, D_FF), tok),
                   pl.BlockSpec((tm, D_MODEL), tok)],
        out_shape=[_sds((T, D_MODEL), F32), _sds((T, D_FF), BF16), _sds((T, D_FF), BF16), _sds((T, D_MODEL), BF16)],
        compiler_params=_params(("arbitrary",), FFN_VMEM_LIMIT),
    )(h, g, w1, w3, w2)


def ffn_bwd(h, g, dho, a1, a3, w1, w3, w2, name):
    T = h.shape[0]
    tm = _tile(T, FFN_TOKEN_TILE)

    def body(h_ref, g_ref, dho_ref, a1_ref, a3_ref, w1_ref, w3_ref, w2_ref, dhi_ref, da1_ref, da3_ref, act_ref, dg_ref):
        @pl.when(pl.program_id(0) == 0)
        def _():
            dg_ref[...] = jnp.zeros_like(dg_ref)

        dho = dho_ref[...]
        dhb = (0.5 * dho).astype(BF16)
        dn = None
        for lo in range(0, D_FF, FF_CHUNK):
            cols = slice(lo, lo + FF_CHUNK)
            a1v = a1_ref[:, cols].astype(F32)
            a3v = a3_ref[:, cols].astype(F32)
            s = jax.nn.sigmoid(a1v)
            sl = a1v * s
            dact = _dot_nt(dhb, w2_ref[cols, :])
            act_ref[:, cols] = (sl * a3v).astype(BF16)
            da1 = (dact * a3v * s * (1.0 + a1v * (1.0 - s))).astype(BF16)
            da3 = (dact * sl).astype(BF16)
            da1_ref[:, cols] = da1
            da3_ref[:, cols] = da3
            part = _dot(da1, w1_ref[cols, :]) + _dot(da3, w3_ref[cols, :])
            dn = part if dn is None else dn + part
        dx, dg = _rms_bwd(dn, h_ref[...], g_ref[...])
        dg_ref[...] += dg
        dhi_ref[...] = dho + dx

    tok = lambda i: (i, 0)
    return pl.pallas_call(
        body, name=name, grid=(T // tm,),
        in_specs=[pl.BlockSpec((tm, D_MODEL), tok), _resident((1, D_MODEL)), pl.BlockSpec((tm, D_MODEL), tok),
                  pl.BlockSpec((tm, D_FF), tok), pl.BlockSpec((tm, D_FF), tok), _resident((D_FF, D_MODEL)),
                  _resident((D_FF, D_MODEL)), _resident((D_FF, D_MODEL))],
        out_specs=[pl.BlockSpec((tm, D_MODEL), tok), pl.BlockSpec((tm, D_FF), tok), pl.BlockSpec((tm, D_FF), tok),
                   pl.BlockSpec((tm, D_FF), tok), pl.BlockSpec((1, D_MODEL), lambda i: (0, 0))],
        out_shape=[_sds((T, D_MODEL), F32), _sds((T, D_FF), BF16), _sds((T, D_FF), BF16), _sds((T, D_FF), BF16),
                   _sds((1, D_MODEL), F32)],
        compiler_params=_params(("arbitrary",), FFN_VMEM_LIMIT),
    )(h, g, dho, a1, a3, w1, w3, w2)


def mm_tn(a, b, name, scale=1.0, a_cols=None, b_cols=None):
    T = a.shape[0]
    a_off, M = a_cols or (0, a.shape[1])
    b_off, N = b_cols or (0, b.shape[1])
    bm = 512 if M % 512 == 0 else (1408 if M == 2816 else 256)
    bn = N if N in (2176, 1408) else (1408 if N == 2816 else min(N, 1024))
    tk = _tile(T, MM_K_TILE)
    row_bytes = 2 * (bm * a.dtype.itemsize + bn * b.dtype.itemsize)
    while tk > TOKEN_TILE and tk * row_bytes > VMEM_LIMIT // 3:
        tk //= 2
    assert M % bm == 0 and N % bn == 0 and T % tk == 0 and a_off % bm == 0 and b_off % bn == 0
    n_k = T // tk
    m0, n0 = a_off // bm, b_off // bn

    def body(a_ref, b_ref, o_ref):
        k = pl.program_id(2)

        @pl.when(k == 0)
        def _():
            o_ref[...] = jnp.zeros_like(o_ref)

        o_ref[...] += _dot_tn(a_ref[...].astype(BF16), b_ref[...].astype(BF16))

        if scale != 1.0:
            @pl.when(k == n_k - 1)
            def _():
                o_ref[...] = o_ref[...] * scale

    return pl.pallas_call(
        body, name=name, grid=(M // bm, N // bn, n_k),
        in_specs=[pl.BlockSpec((tk, bm), lambda m, n, k: (k, m0 + m)), pl.BlockSpec((tk, bn), lambda m, n, k: (k, n0 + n))],
        out_specs=pl.BlockSpec((bm, bn), lambda m, n, k: (m, n)),
        out_shape=_sds((M, N), F32),
        compiler_params=_params(("arbitrary", "arbitrary", "arbitrary")),
    )(a, b)


def mixin_fwd(h1, g, w_in_r, b_f_pad):
    T = h1.shape[0]
    tm = _tile(T, TOKEN_TILE)

    def body(h_ref, g_ref, w_ref, bf_ref, u_ref, qkv_ref, s_ref, fz_ref, c_ref, carry):
        @pl.when(pl.program_id(0) == 0)
        def _():
            carry[...] = jnp.zeros_like(carry)

        x = h_ref[...]
        u = (x * _rms_scale(x) * g_ref[...]).astype(BF16)
        u_ref[...] = u
        z = _dot(u, w_ref[...])
        qkv_ref[...] = z[:, :3 * ATTN_W].astype(BF16)
        s_ref[...] = z[:, 3 * ATTN_W:3 * ATTN_W + SSM_W]
        fz = z[:, 3 * ATTN_W + SSM_W:] + bf_ref[...]
        fz_ref[...] = fz
        lane = lax.broadcasted_iota(jnp.int32, fz.shape, 1)
        logf = jnp.where(lane < N_HEADS, jnp.minimum(fz, 0.0) - jnp.log(1.0 + jnp.exp(-jnp.abs(fz))), 0.0)
        row = lax.broadcasted_iota(jnp.int32, (tm, tm), 0)
        col = lax.broadcasted_iota(jnp.int32, (tm, tm), 1)
        tri = (col <= row).astype(F32)
        cs = jnp.dot(tri, logf, precision=HIGHEST, preferred_element_type=F32) + carry[0:1, :]
        c_ref[...] = cs
        carry[...] = jnp.broadcast_to(cs[tm - 1:tm, :], carry.shape)

    tok = lambda i: (i, 0)
    fix = lambda i: (0, 0)
    return pl.pallas_call(
        body, name="mixin_fwd", grid=(T // tm,),
        in_specs=[pl.BlockSpec((tm, D_MODEL), tok), pl.BlockSpec((1, D_MODEL), fix),
                  pl.BlockSpec((D_MODEL, Z_COLS), fix), pl.BlockSpec((1, 128), fix)],
        out_specs=[pl.BlockSpec((tm, D_MODEL), tok), pl.BlockSpec((tm, 3 * ATTN_W), tok), pl.BlockSpec((tm, SSM_W), tok),
                   pl.BlockSpec((tm, 128), tok), pl.BlockSpec((tm, 128), tok)],
        out_shape=[_sds((T, D_MODEL), BF16), _sds((T, 3 * ATTN_W), BF16), _sds((T, SSM_W), F32),
                   _sds((T, 128), F32), _sds((T, 128), F32)],
        scratch_shapes=[pltpu.VMEM((8, 128), F32)],
        compiler_params=_params(("arbitrary",)),
    )(h1, g, w_in_r, b_f_pad)


def mixin_bwd(dh2, h1, g, w_in_r, dq, dk, dv, ds, dc, fz):
    T = h1.shape[0]
    tm = _tile(T, TOKEN_TILE)
    n_t = T // tm

    def body(dh2_ref, h_ref, g_ref, w_ref, dq_ref, dk_ref, dv_ref, ds_ref, dc_ref, fz_ref,
             dh1_ref, dz_ref, dg_ref, dbf_ref, carry):
        @pl.when(pl.program_id(0) == 0)
        def _():
            carry[...] = jnp.zeros_like(carry)
            dg_ref[...] = jnp.zeros_like(dg_ref)
            dbf_ref[...] = jnp.zeros_like(dbf_ref)

        row = lax.broadcasted_iota(jnp.int32, (tm, tm), 0)
        col = lax.broadcasted_iota(jnp.int32, (tm, tm), 1)
        tri = (col >= row).astype(F32)
        dlogf = jnp.dot(tri, dc_ref[...], precision=HIGHEST, preferred_element_type=F32) + carry[0:1, :]
        carry[...] = jnp.broadcast_to(dlogf[0:1, :], carry.shape)
        dfz = dlogf * jax.nn.sigmoid(-fz_ref[...])
        dbf_ref[...] += jnp.sum(dfz, axis=0, keepdims=True)
        dz = jnp.concatenate([_join_heads(dq_ref, BF16), _join_heads(dk_ref, BF16), _join_heads(dv_ref, BF16),
                              ds_ref[...], dfz], axis=1).astype(BF16)
        dz_ref[...] = dz
        du = _dot_nt(dz, w_ref[...])
        dx, dg = _rms_bwd(du, h_ref[...], g_ref[...])
        dg_ref[...] += dg
        dh1_ref[...] = dh2_ref[...] + dx

    tok = lambda i: (n_t - 1 - i, 0)
    fix = lambda i: (0, 0)
    heads = pl.BlockSpec((N_HEADS, tm, 128), lambda i: (0, n_t - 1 - i, 0))
    return pl.pallas_call(
        body, name="mixin_bwd", grid=(n_t,),
        in_specs=[pl.BlockSpec((tm, D_MODEL), tok), pl.BlockSpec((tm, D_MODEL), tok), pl.BlockSpec((1, D_MODEL), fix),
                  pl.BlockSpec((D_MODEL, Z_COLS), fix), heads, heads, heads, pl.BlockSpec((tm, SSM_W), tok),
                  pl.BlockSpec((tm, 128), tok), pl.BlockSpec((tm, 128), tok)],
        out_specs=[pl.BlockSpec((tm, D_MODEL), tok), pl.BlockSpec((tm, Z_COLS), tok), pl.BlockSpec((1, D_MODEL), fix),
                   pl.BlockSpec((1, 128), fix)],
        out_shape=[_sds((T, D_MODEL), F32), _sds((T, Z_COLS), BF16), _sds((1, D_MODEL), F32), _sds((1, 128), F32)],
        scratch_shapes=[pltpu.VMEM((8, 128), F32)],
        compiler_params=_params(("arbitrary",)),
    )(dh2, h1, g, w_in_r, dq, dk, dv, ds, dc, fz)


def _lane_move(src_lo, dst_lo, width, dtype):
    r = lax.broadcasted_iota(jnp.int32, (128, 128), 0)
    c = lax.broadcasted_iota(jnp.int32, (128, 128), 1)
    return ((c - dst_lo == r - src_lo) & (r >= src_lo) & (r < src_lo + width)).astype(dtype)


def _lane_const(lo, width, value):
    lane = lax.broadcasted_iota(jnp.int32, (1, 128), 1)
    return jnp.where((lane >= lo) & (lane < lo + width), value, 0.0).astype(F32)


def _pieces(a):
    hi = a.astype(BF16)
    rest = a - hi.astype(F32)
    mid = rest.astype(BF16)
    return hi, mid, (rest - mid.astype(F32)).astype(BF16)


def _head_features(pair_block, e):
    return _dot(pair_block, _lane_move(HEAD_DIM * e, 0, HEAD_DIM, BF16))


def _helper_columns(pieces, head, sign):
    out = None
    for k, piece in enumerate(pieces):
        term = _dot(piece, _lane_move(head, HEAD_DIM + k, 1, BF16))
        out = term if out is None else out + term
    return sign * out


def heads_in(qkv, cum):
    T = qkv.shape[0]
    tm = _tile(T, TOKEN_TILE)

    def body(qkv_ref, c_ref, q_ref, k_ref, v_ref):
        c = _pieces(c_ref[...])
        for h in range(N_HEADS):
            p, e = divmod(h, 2)
            blk = lambda base: qkv_ref[:, base + 128 * p:base + 128 * (p + 1)]
            q_ref[h] = (_head_features(blk(0), e) + _lane_const(HEAD_DIM, 3, -1.0)).astype(BF16)
            k_ref[h] = (_head_features(blk(ATTN_W), e) + _helper_columns(c, h, 1.0)
                        + _lane_const(HEAD_DIM + 3, 3, 1.0)).astype(BF16)
            v_ref[h] = (_head_features(blk(2 * ATTN_W), e) + _lane_const(HEAD_DIM, 3, 1.0)).astype(BF16)

    tok = lambda i: (i, 0)
    heads = pl.BlockSpec((N_HEADS, tm, 128), lambda i: (0, i, 0))
    return pl.pallas_call(
        body, name="heads_in", grid=(T // tm,),
        in_specs=[pl.BlockSpec((tm, 3 * ATTN_W), tok), pl.BlockSpec((tm, 128), tok)],
        out_specs=[heads] * 3, out_shape=[_sds((N_HEADS, T, 128), BF16)] * 3,
        compiler_params=_params(("arbitrary",)),
    )(qkv, cum)


def attn_fwd(q_aug, k_aug, v_aug):
    H, T, wd = q_aug.shape
    hd = HEAD_DIM
    tq = _tile(T, ATTN_TILE)
    n = T // tq

    def body(q_ref, k_ref, v_ref, o_ref, qb_ref, m_sc, acc, s_even, s_odd):
        qi = pl.program_id(1)
        qv = q_ref[0]
        m_sc[...] = jnp.full_like(m_sc, -jnp.inf)
        acc[...] = jnp.zeros_like(acc)

        def key_rows(j):
            return pl.ds(pl.multiple_of(jnp.minimum(j, qi) * tq, tq), tq)

        def logits(j, buf):
            buf[...] = _dot_nt(k_ref[0, key_rows(j), :], qv)

        def update(j, buf, masked):
            st = buf[...]
            if masked:
                keep = lax.broadcasted_iota(jnp.int32, (tq, tq), 0) <= lax.broadcasted_iota(jnp.int32, (tq, tq), 1)
                st = jnp.where(keep, st, -1e30)
            m_old = m_sc[...]
            m_new = jnp.maximum(m_old, jnp.max(st, axis=0, keepdims=True))
            pt = jnp.exp(st - m_new).astype(BF16)
            acc[...] = jnp.exp(m_old - m_new) * acc[...] + _dot_tn(v_ref[0, key_rows(j), :], pt)
            m_sc[...] = m_new

        logits(0, s_even)

        def two_tiles(p, carry):
            j = 2 * p
            logits(j + 1, s_odd)
            update(j, s_even, False)
            logits(j + 2, s_even)
            update(j + 1, s_odd, False)
            return carry

        lax.fori_loop(0, qi // 2, two_tiles, 0)

        @pl.when(qi % 2 == 0)
        def _():
            update(qi, s_even, True)

        @pl.when(qi % 2 == 1)
        def _():
            logits(qi, s_odd)
            update(qi - 1, s_even, False)
            update(qi, s_odd, True)

        total = acc[hd:hd + 1, :]
        o_ref[0] = (acc[...] / total).T
        hi, mid, lo = (t.astype(F32) for t in _pieces(-(m_sc[...] + jnp.log(total))))
        row = lax.broadcasted_iota(jnp.int32, (wd, tq), 0)
        lse_rows = jnp.where(row == hd + 3, hi, jnp.where(row == hd + 4, mid, jnp.where(row == hd + 5, lo, 0.0)))
        qb_ref[0] = (qv.astype(F32) + lse_rows.T).astype(BF16)

    qmap = lambda h, i: (h, i, 0)
    head = lambda h, i: (h, 0, 0)
    return pl.pallas_call(
        body, name="attn_fwd", grid=(H, n),
        in_specs=[pl.BlockSpec((1, tq, wd), qmap), pl.BlockSpec((1, T, wd), head), pl.BlockSpec((1, T, wd), head)],
        out_specs=[pl.BlockSpec((1, tq, wd), qmap), pl.BlockSpec((1, tq, wd), qmap)],
        out_shape=[_sds((H, T, wd), F32), _sds((H, T, wd), BF16)],
        scratch_shapes=[pltpu.VMEM((1, tq), F32), pltpu.VMEM((wd, tq), F32), pltpu.VMEM((tq, tq), F32),
                        pltpu.VMEM((tq, tq), F32)],
        compiler_params=_params(("arbitrary", "arbitrary")),
    )(q_aug, k_aug, v_aug)


def attn_bwd(q_aug, k_aug, v_aug, do_aug, pair=None):
    H, T, wd = q_aug.shape
    tq = _tile(T, ATTN_TILE)
    n = T // tq

    def compute(q_ref, do_ref, k_ref, v_ref, dq_ref, dk_ref, dv_ref, dc_ref, dck, s_a, d_a, s_b, d_b):
        j = pl.program_id(1)

        @pl.when(j == 0)
        def _():
            dq_ref[...] = jnp.zeros_like(dq_ref)
            dc_ref[...] = jnp.zeros_like(dc_ref)

        dk_ref[...] = jnp.zeros_like(dk_ref)
        dv_ref[...] = jnp.zeros_like(dv_ref)
        dck[...] = jnp.zeros_like(dck)
        kv, vv = k_ref[0], v_ref[0]

        def query_rows(i):
            return pl.ds(pl.multiple_of(jnp.minimum(i, n - 1) * tq, tq), tq)

        def products(i, s_buf, d_buf):
            rows = query_rows(i)
            s_buf[...] = _dot_nt(kv, q_ref[0, rows, :])
            d_buf[...] = _dot_nt(vv, do_ref[0, rows, :])

        def update(i, s_buf, d_buf, masked):
            rows = query_rows(i)
            qv, dov = q_ref[0, rows, :], do_ref[0, rows, :]
            pt = jnp.exp(s_buf[...])
            if masked:
                keep = lax.broadcasted_iota(jnp.int32, (tq, tq), 0) <= lax.broadcasted_iota(jnp.int32, (tq, tq), 1)
                pt = jnp.where(keep, pt, 0.0)
            dv_ref[0] += _dot(pt.astype(BF16), dov)
            dst = pt * d_buf[...]
            dsb = dst.astype(BF16)
            dk_ref[0] += _dot(dsb, qv)
            dq_ref[0, rows, :] += _dot_tn(dsb, kv)
            dck[...] += jnp.sum(dst, axis=1, keepdims=True)
            dc_ref[0, pl.ds(i, 1), :] += jnp.sum(dst, axis=0, keepdims=True)

        products(j, s_a, d_a)
        products(j + 1, s_b, d_b)
        update(j, s_a, d_a, True)
        left = n - 1 - j

        def two_tiles(p, carry):
            i = j + 1 + 2 * p
            products(i + 1, s_a, d_a)
            update(i, s_b, d_b, False)
            products(i + 2, s_b, d_b)
            update(i + 1, s_a, d_a, False)
            return carry

        lax.fori_loop(0, left // 2, two_tiles, 0)

        @pl.when(left % 2 == 1)
        def _():
            update(n - 1, s_b, d_b, False)

        dc_ref[0, pl.ds(j, 1), :] -= jnp.broadcast_to(dck[...], (tq, 128)).T[0:1, :]

    head = lambda h, j: (h, 0, 0)
    kmap = lambda h, j: (h, j, 0)
    in_specs = [pl.BlockSpec((1, T, wd), head), pl.BlockSpec((1, T, wd), head), pl.BlockSpec((1, tq, wd), kmap),
                pl.BlockSpec((1, tq, wd), kmap)]
    out_specs = [pl.BlockSpec((1, T, wd), head), pl.BlockSpec((1, tq, wd), kmap), pl.BlockSpec((1, tq, wd), kmap),
                 pl.BlockSpec((1, n, tq), head)]
    out_shape = [_sds((H, T, wd), F32), _sds((H, T, wd), F32), _sds((H, T, wd), F32), _sds((H, n, tq), F32)]
    scratch = [pltpu.VMEM((tq, 1), F32)] + [pltpu.VMEM((tq, tq), F32)] * 4
    operands = (q_aug, do_aug, k_aug, v_aug)
    if pair is None:
        body = compute
    else:
        def body(q_ref, do_ref, k_ref, v_ref, pair_ref, dq_ref, dk_ref, dv_ref, dc_ref, got_ref,
                 dck, s_a, d_a, s_b, d_b, send_sems, recv_sems):
            h, j = pl.program_id(0), pl.program_id(1)

            @pl.when((h == 0) & (j == 0))
            def _():
                for cp in _chip_copies(pair_ref, got_ref, send_sems, recv_sems):
                    cp.start()

            compute(q_ref, do_ref, k_ref, v_ref, dq_ref, dk_ref, dv_ref, dc_ref, dck, s_a, d_a, s_b, d_b)

            @pl.when((h == H - 1) & (j == n - 1))
            def _():
                for cp in _chip_copies(pair_ref, got_ref, send_sems, recv_sems):
                    cp.wait()

        sems = pltpu.SemaphoreType.DMA((3 * len(_spans(pair.shape[1], COPY_CHUNKS)),))
        in_specs, out_specs = in_specs + [_HBM], out_specs + [_HBM]
        out_shape = out_shape + [_sds((3,) + pair.shape[1:], pair.dtype)]
        scratch, operands = scratch + [sems, sems], operands + (pair,)
    return pl.pallas_call(
        body, name="attn_bwd", grid=(H, n), in_specs=in_specs, out_specs=out_specs, out_shape=out_shape,
        scratch_shapes=scratch, compiler_params=_params(("arbitrary", "arbitrary")),
    )(*operands)


def _complex_step(a_r, a_i, cr, ci, br, bi):
    return a_r * cr - a_i * ci + br, a_r * ci + a_i * cr + bi


_HALF_CH = SSM_W // 2
_HALF_ST = STATE_W // 2


def _state_cols(part, half):
    lo = part * STATE_W + half * _HALF_ST
    return slice(lo, lo + _HALF_ST)


def _channels_to_states(x, w_ref, out_ref):
    for half in range(2):
        ch = slice(half * _HALF_CH, (half + 1) * _HALF_CH)
        for part in range(2):
            cols = _state_cols(part, half)
            out_ref[:, cols] = _dot(x[:, ch], w_ref[ch, cols])


def _states_to_channels(x, w_ref):
    halves = []
    for half in range(2):
        ch = slice(half * _HALF_CH, (half + 1) * _HALF_CH)
        halves.append(_dot(x[:, _state_cols(0, half)], w_ref[_state_cols(0, half), ch])
                      + _dot(x[:, _state_cols(1, half)], w_ref[_state_cols(1, half), ch]))
    return jnp.concatenate(halves, axis=1)


def ssm_fwd(s_perm, wb, cbd, a_r, a_i, al_r, al_i, dvec):
    T = s_perm.shape[0]
    chunk = T // 8
    ts = _tile(chunk, SCAN_STEPS)
    tr, n_s = ts * 8, chunk // ts
    W, LB = STATE_W, SCAN_LANES

    def body(s_ref, wb_ref, cbd_ref, ar_ref, ai_ref, alr_ref, ali_ref, dv_ref, y_ref, xs_ref, bu, carry):
        ph, i = pl.program_id(0), pl.program_id(1)

        @pl.when((ph == 0) & (i == 0))
        def _():
            carry[...] = jnp.zeros_like(carry)

        _channels_to_states(s_ref[...].astype(BF16), wb_ref, bu)

        def scan(store):
            for lb in range(W // LB):
                lo = lb * LB
                re, im = slice(lo, lo + LB), slice(W + lo, W + lo + LB)
                ar = jnp.broadcast_to(ar_ref[:, re], (8, LB))
                ai = jnp.broadcast_to(ai_ref[:, re], (8, LB))

                def step(s, c):
                    rows = pl.ds(pl.multiple_of(s * 8, 8), 8)
                    nr, ni = _complex_step(ar, ai, c[0], c[1], bu[rows, re], bu[rows, im])
                    if store:
                        bu[rows, re] = nr
                        bu[rows, im] = ni
                    return nr, ni

                cr, ci = lax.fori_loop(0, ts, step, (carry[:, re], carry[:, im]), unroll=2)
                carry[:, re] = cr
                carry[:, im] = ci

        @pl.when(ph == 0)
        def _():
            scan(False)

            @pl.when(i == n_s - 1)
            def _():
                er, ei = carry[:, :W], carry[:, W:]
                alr = jnp.broadcast_to(alr_ref[...], (8, W))
                ali = jnp.broadcast_to(ali_ref[...], (8, W))
                first = lax.broadcasted_iota(jnp.int32, (8, W), 0) == 0
                sr, si = jnp.zeros((8, W), F32), jnp.zeros((8, W), F32)
                for _ in range(7):
                    vr, vi = _complex_step(alr, ali, sr, si, er, ei)
                    sr = jnp.where(first, 0.0, pltpu.roll(vr, 1, 0))
                    si = jnp.where(first, 0.0, pltpu.roll(vi, 1, 0))
                carry[:, :W] = sr
                carry[:, W:] = si

        @pl.when(ph == 1)
        def _():
            scan(True)
            xb = bu[...].astype(BF16)
            xs_ref[...] = xb
            y_ref[...] = _states_to_channels(xb, cbd_ref) + s_ref[...] * dv_ref[...]

    fix = lambda p, i: (0, 0)
    return pl.pallas_call(
        body, name="ssm_fwd", grid=(2, n_s),
        in_specs=[pl.BlockSpec((tr, SSM_W), lambda p, i: (i, 0)), pl.BlockSpec((SSM_W, 2 * W), fix),
                  pl.BlockSpec((2 * W, SSM_W), fix), pl.BlockSpec((1, W), fix), pl.BlockSpec((1, W), fix),
                  pl.BlockSpec((1, W), fix), pl.BlockSpec((1, W), fix), pl.BlockSpec((1, SSM_W), fix)],
        out_specs=[pl.BlockSpec((tr, SSM_W), lambda p, i: (i * p, 0)), pl.BlockSpec((tr, 2 * W), lambda p, i: (i * p, 0))],
        out_shape=[_sds((T, SSM_W), F32), _sds((T, 2 * W), BF16)],
        scratch_shapes=[pltpu.VMEM((tr, 2 * W), F32), pltpu.VMEM((8, 2 * W), F32)],
        compiler_params=_params(("arbitrary", "arbitrary")),
    )(s_perm, wb, cbd, a_r, a_i, al_r, al_i, dvec)


def ssm_bwd(dy_perm, s_perm, xs, cbd_t, wb_t, a_r, a_i, al_r, al_i, dvec):
    T = s_perm.shape[0]
    chunk = T // 8
    ts = _tile(chunk, SCAN_STEPS)
    tr, n_s = ts * 8, chunk // ts
    W, LB = STATE_W, SCAN_LANES

    def body(dy_ref, s_ref, xs_ref, cbt_ref, wbt_ref, ar_ref, ai_ref, alr_ref, ali_ref, dv_ref,
             du_ref, gs_ref, da_ref, dd_ref, gd, x32, carry):
        ph, i = pl.program_id(0), pl.program_id(1)

        @pl.when((ph == 0) & (i == 0))
        def _():
            carry[...] = jnp.zeros_like(carry)
            da_ref[...] = jnp.zeros_like(da_ref)
            dd_ref[...] = jnp.zeros_like(dd_ref)

        _channels_to_states(dy_ref[...].astype(BF16), cbt_ref, gd)

        def scan(store):
            for lb in range(W // LB):
                lo = lb * LB
                re, im = slice(lo, lo + LB), slice(W + lo, W + lo + LB)
                ar = jnp.broadcast_to(ar_ref[:, re], (8, LB))
                nai = -jnp.broadcast_to(ai_ref[:, re], (8, LB))

                def step(k, c):
                    rows = pl.ds(pl.multiple_of((ts - 1 - k) * 8, 8), 8)
                    cr, ci = c[0], c[1]
                    nr, ni = _complex_step(ar, nai, cr, ci, gd[rows, re], gd[rows, im])
                    if store:
                        xr, xi = x32[rows, re], x32[rows, im]
                        gd[rows, re] = nr
                        gd[rows, im] = ni
                        return nr, ni, c[2] + cr * xr + ci * xi, c[3] + ci * xr - cr * xi
                    return nr, ni

                init = (carry[:, re], carry[:, im])
                if store:
                    init = init + (da_ref[:, re], da_ref[:, im])
                out = lax.fori_loop(0, ts, step, init, unroll=2)
                carry[:, re] = out[0]
                carry[:, im] = out[1]
                if store:
                    da_ref[:, re] = out[2]
                    da_ref[:, im] = out[3]

        @pl.when(ph == 0)
        def _():
            scan(False)

            @pl.when(i == n_s - 1)
            def _():
                er, ei = carry[:, :W], carry[:, W:]
                alr = jnp.broadcast_to(alr_ref[...], (8, W))
                nali = -jnp.broadcast_to(ali_ref[...], (8, W))
                last = lax.broadcasted_iota(jnp.int32, (8, W), 0) == 7
                rr, ri = jnp.zeros((8, W), F32), jnp.zeros((8, W), F32)
                for _ in range(7):
                    vr, vi = _complex_step(alr, nali, rr, ri, er, ei)
                    rr = jnp.where(last, 0.0, pltpu.roll(vr, 7, 0))
                    ri = jnp.where(last, 0.0, pltpu.roll(vi, 7, 0))
                carry[:, :W] = rr
                carry[:, W:] = ri

        @pl.when(ph == 1)
        def _():
            x32[...] = xs_ref[...].astype(F32)
            scan(True)
            gb = gd[...].astype(BF16)
            gs_ref[...] = gb
            dy = dy_ref[...]
            du_ref[...] = _states_to_channels(gb, wbt_ref) + dy * dv_ref[...]
            dd_ref[...] += jnp.sum(dy * s_ref[...], axis=0, keepdims=True)

    fix = lambda p, i: (0, 0)
    rev = lambda p, i: (n_s - 1 - i, 0)
    rev_out = lambda p, i: (n_s - 1 - i * p, 0)
    return pl.pallas_call(
        body, name="ssm_bwd", grid=(2, n_s),
        in_specs=[pl.BlockSpec((tr, SSM_W), rev), pl.BlockSpec((tr, SSM_W), rev), pl.BlockSpec((tr, 2 * W), rev),
                  pl.BlockSpec((SSM_W, 2 * W), fix), pl.BlockSpec((2 * W, SSM_W), fix), pl.BlockSpec((1, W), fix),
                  pl.BlockSpec((1, W), fix), pl.BlockSpec((1, W), fix), pl.BlockSpec((1, W), fix),
                  pl.BlockSpec((1, SSM_W), fix)],
        out_specs=[pl.BlockSpec((tr, SSM_W), rev_out), pl.BlockSpec((tr, 2 * W), rev_out),
                   pl.BlockSpec((8, 2 * W), fix), pl.BlockSpec((1, SSM_W), fix)],
        out_shape=[_sds((T, SSM_W), F32), _sds((T, 2 * W), BF16), _sds((8, 2 * W), F32), _sds((1, SSM_W), F32)],
        scratch_shapes=[pltpu.VMEM((tr, 2 * W), F32), pltpu.VMEM((tr, 2 * W), F32), pltpu.VMEM((8, 2 * W), F32)],
        compiler_params=_params(("arbitrary", "arbitrary")),
    )(dy_perm, s_perm, xs, cbd_t, wb_t, a_r, a_i, al_r, al_i, dvec)


def _join_heads(ref, dtype):
    def move(h, dst):
        x = ref[h]
        pieces = _pieces(x) if dtype == F32 else (x.astype(BF16),)
        out = None
        for piece in pieces:
            term = _dot(piece, _lane_move(0, dst, HEAD_DIM, BF16))
            out = term if out is None else out + term
        return out

    return jnp.concatenate([move(2 * p, 0) + move(2 * p + 1, HEAD_DIM) for p in range(N_HEADS // 2)], axis=1)


def mixout_fwd(h1, o_heads, ypre, g_a, g_s, w_glu, b_glu, w_out):
    T = h1.shape[0]
    tm = _tile(T, TOKEN_TILE)

    def body(h_ref, at_ref, yp_ref, ga_ref, gs_ref, wg_ref, bg_ref, wo_ref, h2_ref, mixed_ref):
        yg, _ = _gelu_parts(yp_ref[...])
        gl = yg * jax.nn.sigmoid(_dot(yg.astype(BF16), wg_ref[...]) + bg_ref[...])
        at = _join_heads(at_ref, F32)
        mixed = jnp.concatenate([at * _rms_scale(at) * ga_ref[...], gl * _rms_scale(gl) * gs_ref[...]], axis=1)
        mixed = mixed.astype(BF16)
        mixed_ref[...] = mixed
        h2_ref[...] = h_ref[...] + _dot(mixed, wo_ref[...])

    tok = lambda i: (i, 0)
    fix = lambda i: (0, 0)
    return pl.pallas_call(
        body, name="mixout_fwd", grid=(T // tm,),
        in_specs=[pl.BlockSpec((tm, D_MODEL), tok), pl.BlockSpec((N_HEADS, tm, 128), lambda i: (0, i, 0)),
                  pl.BlockSpec((tm, SSM_W), tok),
                  pl.BlockSpec((1, ATTN_W), fix), pl.BlockSpec((1, SSM_W), fix), pl.BlockSpec((SSM_W, SSM_W), fix),
                  pl.BlockSpec((1, SSM_W), fix), pl.BlockSpec((D_MODEL, D_MODEL), fix)],
        out_specs=[pl.BlockSpec((tm, D_MODEL), tok), pl.BlockSpec((tm, D_MODEL), tok)],
        out_shape=[_sds((T, D_MODEL), F32), _sds((T, D_MODEL), BF16)],
        compiler_params=_params(("arbitrary",)),
    )(h1, o_heads, ypre, g_a, g_s, w_glu, b_glu, w_out)


def mixout_bwd(dh2, o_heads, ypre, g_a, g_s, w_glu, b_glu, w_out, seg):
    T = dh2.shape[0]
    tm = _tile(T, TOKEN_TILE)

    def body(dh_ref, at_ref, yp_ref, ga_ref, gs_ref, wg_ref, bg_ref, wo_ref, seg_ref,
             do_ref, dyp_ref, dpre_ref, yg_ref, dga_ref, dgs_ref, dbg_ref):
        @pl.when(pl.program_id(0) == 0)
        def _():
            dga_ref[...] = jnp.zeros_like(dga_ref)
            dgs_ref[...] = jnp.zeros_like(dgs_ref)
            dbg_ref[...] = jnp.zeros_like(dbg_ref)

        dmix = _dot_nt(dh_ref[...].astype(BF16), wo_ref[...])
        at = _join_heads(at_ref, F32)
        dat, dga = _rms_bwd(dmix[:, :ATTN_W], at, ga_ref[...])
        dga_ref[...] += dga
        delta = _pieces(jnp.dot(dat * at, seg_ref[...], precision=HIGHEST, preferred_element_type=F32))
        datb = dat.astype(BF16)
        for h in range(N_HEADS):
            p, e = divmod(h, 2)
            do_ref[h] = (_head_features(datb[:, 128 * p:128 * (p + 1)], e) + _helper_columns(delta, h, -1.0)).astype(BF16)
        yp = yp_ref[...]
        yg, t = _gelu_parts(yp)
        ygb = yg.astype(BF16)
        yg_ref[...] = ygb
        sg = jax.nn.sigmoid(_dot(ygb, wg_ref[...]) + bg_ref[...])
        dgl, dgs = _rms_bwd(dmix[:, ATTN_W:], yg * sg, gs_ref[...])
        dgs_ref[...] += dgs
        dpre = dgl * yg * sg * (1.0 - sg)
        dbg_ref[...] += jnp.sum(dpre, axis=0, keepdims=True)
        dpb = dpre.astype(BF16)
        dpre_ref[...] = dpb
        dyg = dgl * sg + _dot_nt(dpb, wg_ref[...])
        dyp_ref[...] = dyg * _gelu_grad(yp, t)

    tok = lambda i: (i, 0)
    fix = lambda i: (0, 0)
    heads = pl.BlockSpec((N_HEADS, tm, 128), lambda i: (0, i, 0))
    return pl.pallas_call(
        body, name="mixout_bwd", grid=(T // tm,),
        in_specs=[pl.BlockSpec((tm, D_MODEL), tok), heads, pl.BlockSpec((tm, SSM_W), tok),
                  pl.BlockSpec((1, ATTN_W), fix), pl.BlockSpec((1, SSM_W), fix), pl.BlockSpec((SSM_W, SSM_W), fix),
                  pl.BlockSpec((1, SSM_W), fix), pl.BlockSpec((D_MODEL, D_MODEL), fix), pl.BlockSpec((ATTN_W, 128), fix)],
        out_specs=[heads, pl.BlockSpec((tm, SSM_W), tok), pl.BlockSpec((tm, SSM_W), tok),
                   pl.BlockSpec((tm, SSM_W), tok), pl.BlockSpec((1, ATTN_W), fix),
                   pl.BlockSpec((1, SSM_W), fix), pl.BlockSpec((1, SSM_W), fix)],
        out_shape=[_sds((N_HEADS, T, 128), BF16), _sds((T, SSM_W), F32), _sds((T, SSM_W), BF16), _sds((T, SSM_W), BF16),
                   _sds((1, ATTN_W), F32), _sds((1, SSM_W), F32), _sds((1, SSM_W), F32)],
        compiler_params=_params(("arbitrary",)),
    )(dh2, o_heads, ypre, g_a, g_s, w_glu, b_glu, w_out, seg)


def head_fwd_bwd(h3, p, target, g_ple, g_final, w_gate, w_proj):
    T = h3.shape[0]
    tm = _tile(T, TOKEN_TILE)
    pd = p.shape[1]

    def body(h_ref, p_ref, tg_ref, gp_ref, gf_ref, wg_ref, wp_ref,
             dh_ref, n3_ref, dz_ref, dpp_ref, loss_ref, dgp_ref, dgf_ref):
        @pl.when(pl.program_id(0) == 0)
        def _():
            loss_ref[...] = jnp.zeros_like(loss_ref)
            dgp_ref[...] = jnp.zeros_like(dgp_ref)
            dgf_ref[...] = jnp.zeros_like(dgf_ref)

        x = h_ref[...]
        gp, gf = gp_ref[...], gf_ref[...]
        n3 = (x * _rms_scale(x) * gp).astype(BF16)
        n3_ref[...] = n3
        gate = jax.nn.sigmoid(_dot(n3, wg_ref[...]))
        pp = _dot(p_ref[...].astype(BF16), wp_ref[...])
        h4 = x + gate * pp
        y = h4 * _rms_scale(h4) * gf
        e = y - tg_ref[...]
        tile_loss = jnp.sum(jnp.sum(e * e, axis=1, keepdims=True), axis=0, keepdims=True) * (0.5 / D_MODEL)
        loss_ref[...] += jnp.broadcast_to(tile_loss, loss_ref.shape)
        dh4, dgf = _rms_bwd(e * (1.0 / D_MODEL), h4, gf)
        dgf_ref[...] += dgf
        dzg = dh4 * pp * gate * (1.0 - gate)
        dzb = dzg.astype(BF16)
        dz_ref[...] = dzb
        dpp_ref[...] = (dh4 * gate).astype(BF16)
        dx, dgp = _rms_bwd(_dot_nt(dzb, wg_ref[...]), x, gp)
        dgp_ref[...] += dgp
        dh_ref[...] = dh4 + dx

    tok = lambda i: (i, 0)
    fix = lambda i: (0, 0)
    return pl.pallas_call(
        body, name="head_fwd_bwd", grid=(T // tm,),
        in_specs=[pl.BlockSpec((tm, D_MODEL), tok), pl.BlockSpec((tm, pd), tok), pl.BlockSpec((tm, D_MODEL), tok),
                  pl.BlockSpec((1, D_MODEL), fix), pl.BlockSpec((1, D_MODEL), fix), pl.BlockSpec((D_MODEL, D_MODEL), fix),
                  pl.BlockSpec((pd, D_MODEL), fix)],
        out_specs=[pl.BlockSpec((tm, D_MODEL), tok), pl.BlockSpec((tm, D_MODEL), tok), pl.BlockSpec((tm, D_MODEL), tok),
                   pl.BlockSpec((tm, D_MODEL), tok), pl.BlockSpec((8, 128), fix), pl.BlockSpec((1, D_MODEL), fix),
                   pl.BlockSpec((1, D_MODEL), fix)],
        out_shape=[_sds((T, D_MODEL), F32), _sds((T, D_MODEL), BF16), _sds((T, D_MODEL), BF16), _sds((T, D_MODEL), BF16),
                   _sds((8, 128), F32), _sds((1, D_MODEL), F32), _sds((1, D_MODEL), F32)],
        compiler_params=_params(("arbitrary",)),
    )(h3, p, target, g_ple, g_final, w_gate, w_proj)


def _row_tile(rows, cols, n_arrays):
    lanes = -(-cols // 128) * 128
    cap = VMEM_LIMIT // 3 // (2 * n_arrays * lanes * 4)
    best = None
    for t in range(PACK_ALIGN, min(rows, cap) + 1, PACK_ALIGN):
        if rows % t == 0:
            best = t
    assert best is not None, (rows, cols)
    return best


def _adamw_math(w, g, m, v):
    nm = ADAM_B1 * m + (1.0 - ADAM_B1) * g
    nv = ADAM_B2 * v + (1.0 - ADAM_B2) * (g * g)
    c1 = 1.0 - ADAM_B1 ** ADAM_STEP
    c2 = 1.0 - ADAM_B2 ** ADAM_STEP
    return -ADAM_LR * ((nm / c1) / (jnp.sqrt(nv / c2) + ADAM_EPS) + ADAM_WD * w), nm, nv


def adamw(w, g, m, v, name):
    R, C = w.shape
    tr = _row_tile(R, C, 7)

    def body(w_ref, g_ref, m_ref, v_ref, d_ref, nm_ref, nv_ref):
        d_ref[...], nm_ref[...], nv_ref[...] = _adamw_math(w_ref[...], g_ref[...], m_ref[...], v_ref[...])

    spec = pl.BlockSpec((tr, C), lambda i: (i, 0))
    return pl.pallas_call(
        body, name=name, grid=(R // tr,), in_specs=[spec] * 4, out_specs=[spec] * 3,
        out_shape=[_sds((R, C), F32)] * 3, compiler_params=_params(("arbitrary",)),
    )(w, g, m, v)


def join_halves(mine, other, core):
    rh, C = mine.shape
    tr = _row_tile(rh, C, 3)
    nb = rh // tr

    def body(c_ref, m_ref, o_ref, out_ref):
        out_ref[...] = jnp.where((pl.program_id(0) // nb) == c_ref[0], m_ref[...], o_ref[...])

    half = pl.BlockSpec((tr, C), lambda i, c: (i % nb, 0))
    return pl.pallas_call(
        body, name="join_halves",
        grid_spec=pltpu.PrefetchScalarGridSpec(num_scalar_prefetch=1, grid=(2 * nb,), in_specs=[half, half],
                                               out_specs=pl.BlockSpec((tr, C), lambda i, c: (i, 0))),
        out_shape=_sds((2 * rh, C), F32), compiler_params=_params(("arbitrary",)),
    )(core, mine, other)


def pair_sum(g, theirs, core):
    n, R, C = g.shape
    rh = R // 2
    tr = _row_tile(rh, C, 3)
    nb = rh // tr

    def body(c_ref, g_ref, t_ref, o_ref):
        o_ref[...] = (g_ref[...] + t_ref[...]).astype(BF16)

    here = pl.BlockSpec((1, tr, C), lambda j, i, c: (j, i, 0))
    return pl.pallas_call(
        body, name="pair_sum",
        grid_spec=pltpu.PrefetchScalarGridSpec(
            num_scalar_prefetch=1, grid=(n, nb),
            in_specs=[pl.BlockSpec((1, tr, C), lambda j, i, c: (j, c[0] * nb + i, 0)), here], out_specs=here),
        out_shape=_sds((n, rh, C), BF16), compiler_params=_params(("arbitrary", "arbitrary")),
    )(core, g, theirs)


def chip_sum(pair, got, chip):
    _, R, C = pair.shape
    tr = _row_tile(R, C, 5)

    def body(c_ref, p_ref, g0_ref, g1_ref, g2_ref, o_ref):
        f = lambda ref: ref[0].astype(F32)
        o_ref[...] = ((f(p_ref) + f(g0_ref)) + f(g1_ref)) + f(g2_ref)

    slot = lambda k: pl.BlockSpec((1, tr, C), lambda i, c: (k, i, 0))
    return pl.pallas_call(
        body, name="chip_sum",
        grid_spec=pltpu.PrefetchScalarGridSpec(
            num_scalar_prefetch=1, grid=(R // tr,),
            in_specs=[pl.BlockSpec((1, tr, C), lambda i, c: (c[0], i, 0)), slot(0), slot(1), slot(2)],
            out_specs=pl.BlockSpec((tr, C), lambda i, c: (i, 0))),
        out_shape=_sds((R, C), F32), compiler_params=_params(("arbitrary",)),
    )(chip, pair, got, got, got)


_HBM = pl.BlockSpec(memory_space=pltpu.HBM)


def _place():
    x, y, c = lax.axis_index("x"), lax.axis_index("y"), lax.axis_index("c")
    return x, y, c, [(1 - x, y), (x, 1 - y), (1 - x, 1 - y)]


def _spans(rows, n):
    assert rows % PACK_ALIGN == 0
    tiles = rows // PACK_ALIGN
    n = min(n, tiles)
    cuts = [tiles * q // n for q in range(n + 1)]
    return [(cuts[q] * PACK_ALIGN, (cuts[q + 1] - cuts[q]) * PACK_ALIGN) for q in range(n)]


def _remote(src, dst, send_sem, recv_sem, to):
    return pltpu.make_async_remote_copy(src_ref=src, dst_ref=dst, send_sem=send_sem, recv_sem=recv_sem,
                                        device_id=to, device_id_type=MESH)


def allgather_shards(wp):
    R, C = wp.shape
    rh = R // 2
    spans = _spans(rh, COPY_CHUNKS)
    n_sp = len(spans)
    local_spans = _spans(R, 2 * COPY_CHUNKS)

    def body(w_ref, out_ref, send_sems, recv_sems, pass_send, pass_recv, local_sems):
        x, y, c, chips = _place()
        me = 2 * x + y
        local = []
        for q, (o, n) in enumerate(local_spans):
            cp = pltpu.make_async_copy(w_ref.at[pl.ds(o, n), :], out_ref.at[me, pl.ds(o, n), :], local_sems.at[q])
            cp.start()
            local.append(cp)
        sends = []
        for k, (cx, cy) in enumerate(chips):
            for q, (o, n) in enumerate(spans):
                rows = pl.ds(c * rh + o, n)
                cp = _remote(w_ref.at[rows, :], out_ref.at[me, rows, :], send_sems.at[k * n_sp + q],
                             recv_sems.at[k * n_sp + q], (cx, cy, c))
                cp.start()
                sends.append(cp)
        for q, (o, n) in enumerate(spans):
            for k, (cx, cy) in enumerate(chips):
                blk = out_ref.at[2 * cx + cy, pl.ds(c * rh + o, n), :]
                _remote(blk, blk, send_sems.at[k * n_sp + q], recv_sems.at[k * n_sp + q], (cx, cy, c)).wait_recv()
                cp = _remote(blk, blk, pass_send.at[k * n_sp + q], pass_recv.at[k * n_sp + q], (x, y, 1 - c))
                cp.start()
                sends.append(cp)
        for k, (cx, cy) in enumerate(chips):
            for q, (o, n) in enumerate(spans):
                blk = out_ref.at[2 * cx + cy, pl.ds((1 - c) * rh + o, n), :]
                _remote(blk, blk, pass_send.at[k * n_sp + q], pass_recv.at[k * n_sp + q], (x, y, 1 - c)).wait_recv()
        for cp in sends:
            cp.wait_send()
        for cp in local:
            cp.wait()

    sems = pltpu.SemaphoreType.DMA((3 * n_sp,))
    return pl.pallas_call(
        body, name="allgather_shards", in_specs=[_HBM], out_specs=_HBM, out_shape=_sds((4, R, C), wp.dtype),
        scratch_shapes=[sems, sems, sems, sems, pltpu.SemaphoreType.DMA((len(local_spans),))],
    )(wp)


def sibling_split(g):
    n_sl, R, C = g.shape
    rh = R // 2
    spans = _spans(rh, COPY_CHUNKS)
    n_sp = len(spans)

    def body(g_ref, got_ref, send_sems, recv_sems):
        x, y, c, _ = _place()
        copies = []
        for j in range(n_sl):
            for q, (o, n) in enumerate(spans):
                cp = _remote(g_ref.at[j, pl.ds((1 - c) * rh + o, n), :], got_ref.at[j, pl.ds(o, n), :],
                             send_sems.at[j * n_sp + q], recv_sems.at[j * n_sp + q], (x, y, 1 - c))
                cp.start()
                copies.append(cp)
        for cp in copies:
            cp.wait()

    sems = pltpu.SemaphoreType.DMA((n_sl * n_sp,))
    return pl.pallas_call(
        body, name="sibling_split", in_specs=[_HBM], out_specs=_HBM, out_shape=_sds((n_sl, rh, C), g.dtype),
        scratch_shapes=[sems, sems],
    )(g)


def _chip_copies(p_ref, buf_ref, send_sems, recv_sems):
    rows = p_ref.shape[1]
    spans = _spans(rows, COPY_CHUNKS)
    x, y, c, chips = _place()
    copies = []
    for k, (cx, cy) in enumerate(chips):
        for q, (o, n) in enumerate(spans):
            copies.append(_remote(p_ref.at[2 * cx + cy, pl.ds(o, n), :], buf_ref.at[k, pl.ds(o, n), :],
                                  send_sems.at[k * len(spans) + q], recv_sems.at[k * len(spans) + q], (cx, cy, c)))
    return copies


def chip_exchange(p):
    _, R, C = p.shape

    def body(p_ref, buf_ref, send_sems, recv_sems):
        copies = _chip_copies(p_ref, buf_ref, send_sems, recv_sems)
        for cp in copies:
            cp.start()
        for cp in copies:
            cp.wait()

    sems = pltpu.SemaphoreType.DMA((3 * len(_spans(R, COPY_CHUNKS)),))
    return pl.pallas_call(
        body, name="chip_exchange", in_specs=[_HBM], out_specs=_HBM, out_shape=_sds((3, R, C), p.dtype),
        scratch_shapes=[sems, sems],
    )(p)


def sibling_swap(half):
    R, C = half.shape
    spans = _spans(R, COPY_CHUNKS)

    def body(h_ref, got_ref, send_sems, recv_sems):
        x, y, c, _ = _place()
        copies = []
        for q, (o, n) in enumerate(spans):
            cp = _remote(h_ref.at[pl.ds(o, n), :], got_ref.at[pl.ds(o, n), :], send_sems.at[q], recv_sems.at[q], (x, y, 1 - c))
            cp.start()
            copies.append(cp)
        for cp in copies:
            cp.wait()

    sems = pltpu.SemaphoreType.DMA((len(spans),))
    return pl.pallas_call(
        body, name="sibling_swap", in_specs=[_HBM], out_specs=_HBM, out_shape=_sds((R, C), half.dtype),
        scratch_shapes=[sems, sems],
    )(half)


def allreduce_small(v):
    R, C = v.shape

    def body(v_ref, out_ref, buf, send_sems, recv_sems):
        x, y, c, _ = _place()
        me = 4 * x + 2 * y + c
        buf[me] = v_ref[...]
        flips = [((k >> 2) & 1, (k >> 1) & 1, k & 1) for k in range(1, 8)]
        sends = []
        for k, (fx, fy, fc) in enumerate(flips):
            to = (1 - x if fx else x, 1 - y if fy else y, 1 - c if fc else c)
            cp = _remote(v_ref, buf.at[me], send_sems.at[k], recv_sems.at[k], to)
            cp.start()
            sends.append(cp)
        for k, (fx, fy, fc) in enumerate(flips):
            px, py, pc = (1 - x if fx else x, 1 - y if fy else y, 1 - c if fc else c)
            blk = buf.at[4 * px + 2 * py + pc]
            _remote(blk, blk, send_sems.at[k], recv_sems.at[k], (px, py, pc)).wait_recv()
        for cp in sends:
            cp.wait_send()
        acc = buf[0]
        for s in range(1, 8):
            acc = acc + buf[s]
        out_ref[...] = acc

    vm = pl.BlockSpec(memory_space=pltpu.VMEM)
    return pl.pallas_call(
        body, name="allreduce_small", in_specs=[vm], out_specs=vm, out_shape=_sds((R, C), F32),
        scratch_shapes=[pltpu.VMEM((8, R, C), F32), pltpu.SemaphoreType.DMA((7,)), pltpu.SemaphoreType.DMA((7,))],
        compiler_params=pltpu.CompilerParams(vmem_limit_bytes=VMEM_LIMIT),
    )(v)


def _rows_of(shape):
    return shape[0] * shape[1] // PACK_COLS


def _slot_rows(shape):
    return -(-_rows_of(shape) // PACK_ALIGN) * PACK_ALIGN


TRANSPOSED = ("w1_a", "w3_a", "w1_b", "w3_b")


def _stored(name, shard):
    return shard[0].T if name in TRANSPOSED else shard[0]


def _restored(name, stored):
    return stored.T[None] if name in TRANSPOSED else stored[None]


def _pack_shards(shards, dtype):
    parts = []
    for name, shape, _ in BIG:
        part = _stored(name, shards[name]).reshape(_rows_of(shape), PACK_COLS).astype(dtype)
        parts.append(jnp.pad(part, ((0, _slot_rows(shape) - part.shape[0]), (0, 0))))
    used = sum(p.shape[0] for p in parts)
    parts.append(jnp.zeros((PACK_ROWS - used, PACK_COLS), dtype))
    return jnp.concatenate(parts, axis=0)


def _unpack_gathered(ag):
    out, off = {}, 0
    for name, shape, axis in BIG:
        r = _rows_of(shape)
        piece = ag[:, off:off + r, :]
        off += _slot_rows(shape)
        if name in TRANSPOSED:
            out[name] = piece.reshape(4 * r, PACK_COLS)
        elif axis == 0:
            out[name] = piece.reshape(4 * shape[0], shape[1])
        else:
            out[name] = piece.reshape((4,) + shape).transpose(1, 0, 2).reshape(shape[0], 4 * shape[1])
    return out


LATE = ("w_glu", "w_out", "w1_b", "w3_b", "w2_b", "w_ple_gate", "w_ple_proj")
GRAD_GROUPS = (tuple(e for e in BIG if e[0] in LATE), tuple(e for e in BIG if e[0] not in LATE))


GROUP_ROW_UNIT = 2816


def _group_rows(entries):
    used = sum(_slot_rows(shape) for _, shape, _ in entries)
    return -(-used // GROUP_ROW_UNIT) * GROUP_ROW_UNIT


def _pack_full_grads(grads, entries):
    parts = []
    for name, shape, axis in entries:
        g = grads[name]
        if name in TRANSPOSED or axis == 0:
            piece = g.reshape(4, _rows_of(shape), PACK_COLS)
        else:
            piece = g.reshape(shape[0], 4, shape[1]).transpose(1, 0, 2).reshape(4, _rows_of(shape), PACK_COLS)
        parts.append(jnp.pad(piece, ((0, 0), (0, _slot_rows(shape) - piece.shape[1]), (0, 0))))
    used = sum(p.shape[1] for p in parts)
    if _group_rows(entries) > used:
        parts.append(jnp.zeros((4, _group_rows(entries) - used, PACK_COLS), F32))
    return jnp.concatenate(parts, axis=1)


def _unpack_shards(packed, entries):
    out, off = {}, 0
    for name, shape, _ in entries:
        r = _rows_of(shape)
        out[name] = packed[off:off + r] if name in TRANSPOSED else packed[off:off + r].reshape(shape)
        off += _slot_rows(shape)
    return out


def _small_rows(shape):
    return -(-math.prod(shape) // 1024) * 8


def _pack_small(vals, extra=None):
    def slot(v, rows):
        flat = v.reshape(-1)
        return jnp.pad(flat, (0, rows * 128 - flat.shape[0])).reshape(rows, 128)

    parts = [slot(vals[name], _small_rows(shape)) for name, shape in SMALL]
    parts.append(slot(extra if extra is not None else jnp.zeros((1,), F32), 8))
    assert sum(p.shape[0] for p in parts) == SMALL_ROWS
    return jnp.concatenate(parts, axis=0)


def _unpack_small(packed):
    out, off = {}, 0
    for name, shape in SMALL:
        rows = _small_rows(shape)
        out[name] = packed[off:off + rows].reshape(-1)[:math.prod(shape)].reshape(shape)
        off += rows
    return out, packed[off, 0]


def _permute_time(a):
    T, n = a.shape
    return a.reshape(8, T // 8, n).transpose(1, 0, 2).reshape(T, n)


def _unpermute_time(a):
    T, n = a.shape
    return a.reshape(T // 8, 8, n).transpose(1, 0, 2).reshape(T, n)


def _discretize(a_re, a_im, log_dt, b_re, b_im):
    dt = jnp.exp(log_dt)[:, None]
    decay = jnp.exp(dt * a_re)
    abar_r = decay * jnp.cos(dt * a_im)
    abar_i = decay * jnp.sin(dt * a_im)
    nr, ni = abar_r - 1.0, abar_i
    den = a_re * a_re + a_im * a_im
    fr = (nr * a_re + ni * a_im) / den
    fi = (ni * a_re - nr * a_im) / den
    bbar_r = fr[..., None] * b_re - fi[..., None] * b_im
    bbar_i = fr[..., None] * b_im + fi[..., None] * b_re
    return abar_r, abar_i, bbar_r, bbar_i


def _input_matrix(bbar_r, bbar_i):
    eye = jnp.eye(N_GROUPS, dtype=F32)
    blk = lambda b: jnp.einsum("ghp,gk->ghkp", b.transpose(0, 2, 1), eye).reshape(SSM_W, STATE_W)
    return jnp.concatenate([blk(bbar_r), blk(bbar_i)], axis=1)


def _output_matrix(c_re, c_im):
    eye = jnp.eye(N_GROUPS, dtype=F32)
    blk = lambda cm: jnp.einsum("ghp,gk->gpkh", cm, eye).reshape(STATE_W, SSM_W)
    return jnp.concatenate([blk(c_re), -blk(c_im)], axis=0)


def _state_power(ar, ai, n):
    steps = int(round(math.log2(n)))
    assert 1 << steps == n
    for _ in range(steps):
        ar, ai = ar * ar - ai * ai, 2.0 * ar * ai
    return ar, ai


def kernel(x, p, g_ffn1, w1_a, w3_a, w2_a, g_mix, w_in, b_f, a_re, a_im, log_dt, b_re, b_im, c_re, c_im, d_skip, w_glu, b_glu, g_attn_out, g_ssm_out, w_out, g_ffn2, w1_b, w3_b, w2_b, g_ple, w_ple_gate, w_ple_proj, g_final, loss_target, m_g_ffn1, m_w1_a, m_w3_a, m_w2_a, m_g_mix, m_w_in, m_b_f, m_a_re, m_a_im, m_log_dt, m_b_re, m_b_im, m_c_re, m_c_im, m_d_skip, m_w_glu, m_b_glu, m_g_attn_out, m_g_ssm_out, m_w_out, m_g_ffn2, m_w1_b, m_w3_b, m_w2_b, m_g_ple, m_w_ple_gate, m_w_ple_proj, m_g_final, v_g_ffn1, v_w1_a, v_w3_a, v_w2_a, v_g_mix, v_w_in, v_b_f, v_a_re, v_a_im, v_log_dt, v_b_re, v_b_im, v_c_re, v_c_im, v_d_skip, v_w_glu, v_b_glu, v_g_attn_out, v_g_ssm_out, v_w_out, v_g_ffn2, v_w1_b, v_w3_b, v_w2_b, v_g_ple, v_w_ple_gate, v_w_ple_proj, v_g_final):
    args = dict(locals())
    weights = {n: args[n] for n in WEIGHT_ORDER}
    moms = {n: args["m_" + n] for n in WEIGHT_ORDER}
    vars_ = {n: args["v_" + n] for n in WEIGHT_ORDER}
    T = x.shape[1]
    x2, p2, tgt = x[0], p[0, 0], loss_target[0]

    full = _unpack_gathered(allgather_shards(_pack_shards(weights, BF16)))
    core = lax.axis_index("c").astype(jnp.int32).reshape(1)
    chip = (2 * lax.axis_index("x") + lax.axis_index("y")).astype(jnp.int32).reshape(1)
    loss_part, dx, grads, late = _local_step(x2, p2, tgt, {n: weights[n] for n, _ in SMALL}, full,
                                             early_exchange=lambda g: _pair_of(g, GRAD_GROUPS[0], core))
    return _reduce_and_update(weights, moms, vars_, loss_part, dx, grads, core, chip, late)


def _pair_of(grads, entries, core):
    packed = _pack_full_grads(grads, entries)
    return pair_sum(packed, sibling_split(packed), core)


def _local_step(x2, p2, tgt, sm, full, early_exchange=None):
    T = x2.shape[0]
    (g_ffn1, g_mix, b_f, a_re, a_im, log_dt, b_re, b_im, c_re, c_im, d_skip, b_glu, g_attn_out, g_ssm_out, g_ffn2, g_ple,
     g_final) = (sm[n] for n, _ in SMALL)
    w_in_f = full["w_in"]
    w_in_r = jnp.concatenate([w_in_f[:, :ATTN_W] * QK_SCALE, w_in_f[:, ATTN_W:3 * ATTN_W], w_in_f[:, 3 * ATTN_W + N_HEADS:],
                              w_in_f[:, 3 * ATTN_W:3 * ATTN_W + N_HEADS], jnp.zeros((D_MODEL, 128 - N_HEADS), BF16)], axis=1)
    b_f_pad = jnp.pad(b_f, ((0, 0), (0, 128 - N_HEADS)))

    disc_in = (a_re[0], a_im[0], log_dt[0], b_re[0], b_im[0])
    (abar_r, abar_i, bbar_r, bbar_i), disc_vjp = jax.vjp(_discretize, *disc_in)
    wb = _input_matrix(bbar_r, bbar_i)
    cbd = _output_matrix(c_re[0], c_im[0])
    ar, ai = abar_r.reshape(1, STATE_W), abar_i.reshape(1, STATE_W)
    alr, ali = _state_power(ar, ai, T // 8)
    dvec = d_skip.reshape(1, SSM_W)
    wb16, cbd16 = wb.astype(BF16), cbd.astype(BF16)

    h1, a1a, a3a, n1 = ffn_fwd(x2, g_ffn1, full["w1_a"], full["w3_a"], full["w2_a"], "ffn_a_fwd")
    u, qkv, s_in, fz, cum = mixin_fwd(h1, g_mix, w_in_r, b_f_pad)
    q_aug, k_aug, v_aug = heads_in(qkv, cum)
    o_heads, q_bwd = attn_fwd(q_aug, k_aug, v_aug)
    s_perm = _permute_time(s_in)
    y_perm, xs = ssm_fwd(s_perm, wb16, cbd16, ar, ai, alr, ali, dvec)
    ypre = _unpermute_time(y_perm)
    h2, mixed = mixout_fwd(h1, o_heads, ypre, g_attn_out, g_ssm_out, full["w_glu"], b_glu, full["w_out"])
    h3, a1b, a3b, n2 = ffn_fwd(h2, g_ffn2, full["w1_b"], full["w3_b"], full["w2_b"], "ffn_b_fwd")

    dh3, n3, dzg, dpp, loss_part, dg_ple, dg_final = head_fwd_bwd(
        h3, p2, tgt, g_ple, g_final.reshape(1, D_MODEL), full["w_ple_gate"], full["w_ple_proj"])
    grads = {"g_ple": dg_ple, "g_final": dg_final.reshape(D_MODEL)}
    grads["w_ple_gate"] = mm_tn(n3, dzg, "dw_ple_gate")
    grads["w_ple_proj"] = mm_tn(p2, dpp, "dw_ple_proj")

    dh2, da1, da3, act, grads["g_ffn2"] = ffn_bwd(h2, g_ffn2, dh3, a1b, a3b, full["w1_b"], full["w3_b"], full["w2_b"], "ffn_b_bwd")
    grads["w1_b"] = mm_tn(da1, n2, "dw1_b")
    grads["w3_b"] = mm_tn(da3, n2, "dw3_b")
    grads["w2_b"] = mm_tn(act, dh3, "dw2_b", scale=0.5)

    seg = (jnp.arange(ATTN_W)[:, None] // HEAD_DIM == jnp.arange(128)[None, :]).astype(F32)
    do_aug, dypre, dpre, yg, grads["g_attn_out"], grads["g_ssm_out"], grads["b_glu"] = mixout_bwd(
        dh2, o_heads, ypre, g_attn_out, g_ssm_out, full["w_glu"], b_glu, full["w_out"], seg)
    grads["w_out"] = mm_tn(mixed, dh2, "dw_out")
    grads["w_glu"] = mm_tn(yg, dpre, "dw_glu")

    if early_exchange is None:
        late = None
        dq_aug, dk_aug, dv_aug, dc_rows = attn_bwd(q_bwd, k_aug, v_aug, do_aug)
    else:
        pair_late = early_exchange(grads)
        dq_aug, dk_aug, dv_aug, dc_rows, got_late = attn_bwd(q_bwd, k_aug, v_aug, do_aug, pair=pair_late)
        late = (pair_late, got_late)
    dc = jnp.pad(dc_rows.reshape(N_HEADS, T).T, ((0, 0), (0, 128 - N_HEADS)))

    dy_perm = _permute_time(dypre)
    du_perm, gs, d_a, dd = ssm_bwd(dy_perm, s_perm, xs, cbd16.T, wb16.T, ar, ai, alr, ali, dvec)
    ds_in = _unpermute_time(du_perm)
    hg = N_GROUPS // 2
    d_in, d_out = [], []
    for part in range(2):
        ins, outs = [], []
        for half in range(2):
            states = (part * STATE_W + half * _HALF_ST, _HALF_ST)
            chans = (half * _HALF_CH, _HALF_CH)
            blk = mm_tn(s_perm, gs, f"dw_ssm_in_{part}{half}", a_cols=chans, b_cols=states)
            ins.append(jnp.einsum("ghgp->ghp", blk.reshape(hg, GROUP_CH, hg, N_STATE)))
            blk = mm_tn(xs, dy_perm, f"dw_ssm_out_{part}{half}", a_cols=states, b_cols=chans)
            outs.append(jnp.einsum("gpgh->gph", blk.reshape(hg, N_STATE, hg, GROUP_CH)))
        d_in.append(jnp.concatenate(ins, axis=0).transpose(0, 2, 1))
        d_out.append(jnp.concatenate(outs, axis=0).transpose(0, 2, 1))
    d_abar_r = jnp.sum(d_a[:, :STATE_W], axis=0).reshape(N_GROUPS, N_STATE)
    d_abar_i = jnp.sum(d_a[:, STATE_W:], axis=0).reshape(N_GROUPS, N_STATE)
    d_disc = disc_vjp((d_abar_r, d_abar_i, d_in[0], d_in[1]))
    for name, val in zip(("a_re", "a_im", "log_dt", "b_re", "b_im"), d_disc):
        grads[name] = val[None]
    grads["c_re"] = d_out[0][None]
    grads["c_im"] = -d_out[1][None]
    grads["d_skip"] = dd.reshape(1, N_GROUPS, GROUP_CH)

    dh1, dz, grads["g_mix"], dbf = mixin_bwd(dh2, h1, g_mix, w_in_r, dq_aug, dk_aug, dv_aug, ds_in, dc, fz)
    grads["b_f"] = dbf[:, :N_HEADS]
    d_w_in_r = mm_tn(u, dz, "dw_in")
    grads["w_in"] = jnp.concatenate([d_w_in_r[:, :ATTN_W] * QK_SCALE, d_w_in_r[:, ATTN_W:3 * ATTN_W],
                                     d_w_in_r[:, 3 * ATTN_W + SSM_W:3 * ATTN_W + SSM_W + N_HEADS],
                                     d_w_in_r[:, 3 * ATTN_W:3 * ATTN_W + SSM_W]], axis=1)

    dx, da1, da3, act, grads["g_ffn1"] = ffn_bwd(x2, g_ffn1, dh1, a1a, a3a, full["w1_a"], full["w3_a"], full["w2_a"], "ffn_a_bwd")
    grads["w1_a"] = mm_tn(da1, n1, "dw1_a")
    grads["w3_a"] = mm_tn(da3, n1, "dw3_a")
    grads["w2_a"] = mm_tn(act, dh1, "dw2_a", scale=0.5)
    return loss_part, dx, grads, late


def _reduce_and_update(weights, moms, vars_, loss_part, dx, grads, core, chip, late):
    pair_early = _pair_of(grads, GRAD_GROUPS[1], core)
    g_stored = {}
    for entries, (pair, got) in zip(GRAD_GROUPS, (late, (pair_early, chip_exchange(pair_early)))):
        half = chip_sum(pair, got, chip)
        g_stored.update(_unpack_shards(join_halves(half, sibling_swap(half), core), entries))
    g_out, d_out, m_out, v_out = {}, {}, {}, {}
    for n, _, _ in BIG:
        d, m, v = adamw(_stored(n, weights[n]), g_stored[n], _stored(n, moms[n]), _stored(n, vars_[n]), "adamw_" + n)
        g_out[n], d_out[n], m_out[n], v_out[n] = (_restored(n, a) for a in (g_stored[n], d, m, v))

    small = allreduce_small(_pack_small({n: grads[n] for n, _ in SMALL}, extra=loss_part[0, 0]))
    d_small, m_small, v_small = adamw(_pack_small(weights), small, _pack_small(moms), _pack_small(vars_), "adamw_small")

    g_small, loss = _unpack_small(small)
    g_out.update(g_small)
    outs = []
    for big, sm in ((d_out, d_small), (m_out, m_small), (v_out, v_small)):
        o, _ = _unpack_small(sm)
        o.update(big)
        outs.append(o)
    result = [loss, dx[None]] + [g_out[n] for n in WEIGHT_ORDER]
    for o in outs:
        result += [o[n] for n in WEIGHT_ORDER]
    return tuple(result)
```

```python
import functools
import math

import jax
import jax.numpy as jnp
from jax import lax
from jax.experimental import pallas as pl
from jax.experimental.pallas import tpu as pltpu

F32 = jnp.float32
BF16 = jnp.bfloat16

D_MODEL = 1024
D_FF = 2816
N_HEADS = 8
HEAD_DIM = 64
ATTN_W = 512
SSM_W = 512
N_GROUPS = 32
N_STATE = 64
GROUP_CH = 16
STATE_W = N_GROUPS * N_STATE
Z_COLS = 2176
QK_SCALE = 0.125
EPS = 1e-6

ADAM_LR = 0.001
ADAM_B1 = 0.9
ADAM_B2 = 0.999
ADAM_EPS = 1e-08
ADAM_WD = 0.01
ADAM_STEP = 10

TOKEN_TILE = 512
FFN_TOKEN_TILE = 256
FF_CHUNK = 1408
MM_K_TILE = 2048
ATTN_TILE = 512
SCAN_STEPS = 32
SCAN_LANES = 512
VMEM_LIMIT = 48 * 1024 * 1024
FFN_VMEM_LIMIT = 56 * 1024 * 1024
COPY_CHUNKS = 4

NT_DIMS = (((1,), (1,)), ((), ()))
TN_DIMS = (((0,), (0,)), ((), ()))
HIGHEST = lax.Precision.HIGHEST
MESH = pl.DeviceIdType.MESH

BIG = (
    ("w1_a", (1024, 704), 1), ("w3_a", (1024, 704), 1), ("w2_a", (704, 1024), 0),
    ("w_in", (1024, 514), 1), ("w_glu", (128, 512), 0), ("w_out", (256, 1024), 0),
    ("w1_b", (1024, 704), 1), ("w3_b", (1024, 704), 1), ("w2_b", (704, 1024), 0),
    ("w_ple_gate", (256, 1024), 0), ("w_ple_proj", (256, 256), 1),
)
PACK_COLS = 1024
PACK_ALIGN = 16
SMALL = (
    ("g_ffn1", (1, 1024)), ("g_mix", (1, 1024)), ("b_f", (1, 8)), ("a_re", (1, 32, 64)), ("a_im", (1, 32, 64)),
    ("log_dt", (1, 32)), ("b_re", (1, 32, 64, 16)), ("b_im", (1, 32, 64, 16)), ("c_re", (1, 32, 16, 64)),
    ("c_im", (1, 32, 16, 64)), ("d_skip", (1, 32, 16)), ("b_glu", (1, 512)), ("g_attn_out", (1, 512)),
    ("g_ssm_out", (1, 512)), ("g_ffn2", (1, 1024)), ("g_ple", (1, 1024)), ("g_final", (1024,)),
)
SMALL_ROWS = 1152
WEIGHT_ORDER = ("g_ffn1", "w1_a", "w3_a", "w2_a", "g_mix", "w_in", "b_f", "a_re", "a_im", "log_dt", "b_re", "b_im",
                "c_re", "c_im", "d_skip", "w_glu", "b_glu", "g_attn_out", "g_ssm_out", "w_out", "g_ffn2", "w1_b",
                "w3_b", "w2_b", "g_ple", "w_ple_gate", "w_ple_proj", "g_final")


def _params(sem=None, vmem=VMEM_LIMIT):
    kw = dict(vmem_limit_bytes=vmem)
    if sem is not None:
        kw["dimension_semantics"] = sem
    return pltpu.CompilerParams(**kw)


def _sds(shape, dtype):
    return jax.ShapeDtypeStruct(shape, dtype)


def _tile(n, pref):
    t = min(n, pref)
    assert n % t == 0, (n, pref)
    return t


def _rms_scale(x):
    return lax.rsqrt(jnp.mean(x * x, axis=-1, keepdims=True) + EPS)


def _rms_bwd(dy, x, g):
    r = _rms_scale(x)
    xh = x * r
    dxh = dy * g
    dx = r * (dxh - xh * jnp.mean(dxh * xh, axis=-1, keepdims=True))
    return dx, jnp.sum(dy * xh, axis=0, keepdims=True)


def _dot(a, b):
    return jnp.dot(a, b, preferred_element_type=F32)


def _dot_nt(a, b):
    return lax.dot_general(a, b, NT_DIMS, preferred_element_type=F32)


def _dot_tn(a, b):
    return lax.dot_general(a, b, TN_DIMS, preferred_element_type=F32)


_GELU_C = math.sqrt(2.0 / math.pi)


def _gelu_parts(x):
    t = jnp.tanh(_GELU_C * (x + 0.044715 * x * x * x))
    return 0.5 * x * (1.0 + t), t


def _gelu_grad(x, t):
    return 0.5 * (1.0 + t) + 0.5 * x * (1.0 - t * t) * _GELU_C * (1.0 + 3.0 * 0.044715 * x * x)


def _resident(shape):
    return pl.BlockSpec(shape, lambda i: (0,) * len(shape), pipeline_mode=pl.Buffered(1))


def ffn_fwd(h, g, w1, w3, w2, name):
    T = h.shape[0]
    tm = _tile(T, FFN_TOKEN_TILE)

    def body(h_ref, g_ref, w1_ref, w3_ref, w2_ref, ho_ref, a1_ref, a3_ref, n_ref):
        x = h_ref[...]
        n = (x * _rms_scale(x) * g_ref[...]).astype(BF16)
        n_ref[...] = n
        out = x
        for lo in range(0, D_FF, FF_CHUNK):
            cols = slice(lo, lo + FF_CHUNK)
            a1 = _dot_nt(n, w1_ref[cols, :])
            a3 = _dot_nt(n, w3_ref[cols, :])
            a1_ref[:, cols] = a1.astype(BF16)
            a3_ref[:, cols] = a3.astype(BF16)
            act = (a1 * jax.nn.sigmoid(a1) * a3).astype(BF16)
            out = out + 0.5 * _dot(act, w2_ref[cols, :])
        ho_ref[...] = out

    tok = lambda i: (i, 0)
    return pl.pallas_call(
        body, name=name, grid=(T // tm,),
        in_specs=[pl.BlockSpec((tm, D_MODEL), tok), _resident((1, D_MODEL)), _resident((D_FF, D_MODEL)),
                  _resident((D_FF, D_MODEL)), _resident((D_FF, D_MODEL))],
        out_specs=[pl.BlockSpec((tm, D_MODEL), tok), pl.BlockSpec((tm, D_FF), tok), pl.BlockSpec((tm, D_FF), tok),
                   pl.BlockSpec((tm, D_MODEL), tok)],
        out_shape=[_sds((T, D_MODEL), F32), _sds((T, D_FF), BF16), _sds((T, D_FF), BF16), _sds((T, D_MODEL), BF16)],
        compiler_params=_params(("arbitrary",), FFN_VMEM_LIMIT),
    )(h, g, w1, w3, w2)


def ffn_bwd(h, g, dho, a1, a3, w1, w3, w2, name):
    T = h.shape[0]
    tm = _tile(T, FFN_TOKEN_TILE)

    def body(h_ref, g_ref, dho_ref, a1_ref, a3_ref, w1_ref, w3_ref, w2_ref, dhi_ref, da1_ref, da3_ref, act_ref, dg_ref):
        @pl.when(pl.program_id(0) == 0)
        def _():
            dg_ref[...] = jnp.zeros_like(dg_ref)

        dho = dho_ref[...]
        dhb = (0.5 * dho).astype(BF16)
        dn = None
        for lo in range(0, D_FF, FF_CHUNK):
            cols = slice(lo, lo + FF_CHUNK)
            a1v = a1_ref[:, cols].astype(F32)
            a3v = a3_ref[:, cols].astype(F32)
            s = jax.nn.sigmoid(a1v)
            sl = a1v * s
            dact = _dot_nt(dhb, w2_ref[cols, :])
            act_ref[:, cols] = (sl * a3v).astype(BF16)
            da1 = (dact * a3v * s * (1.0 + a1v * (1.0 - s))).astype(BF16)
            da3 = (dact * sl).astype(BF16)
            da1_ref[:, cols] = da1
            da3_ref[:, cols] = da3
            part = _dot(da1, w1_ref[cols, :]) + _dot(da3, w3_ref[cols, :])
            dn = part if dn is None else dn + part
        dx, dg = _rms_bwd(dn, h_ref[...], g_ref[...])
        dg_ref[...] += dg
        dhi_ref[...] = dho + dx

    tok = lambda i: (i, 0)
    return pl.pallas_call(
        body, name=name, grid=(T // tm,),
        in_specs=[pl.BlockSpec((tm, D_MODEL), tok), _resident((1, D_MODEL)), pl.BlockSpec((tm, D_MODEL), tok),
                  pl.BlockSpec((tm, D_FF), tok), pl.BlockSpec((tm, D_FF), tok), _resident((D_FF, D_MODEL)),
                  _resident((D_FF, D_MODEL)), _resident((D_FF, D_MODEL))],
        out_specs=[pl.BlockSpec((tm, D_MODEL), tok), pl.BlockSpec((tm, D_FF), tok), pl.BlockSpec((tm, D_FF), tok),
                   pl.BlockSpec((tm, D_FF), tok), pl.BlockSpec((1, D_MODEL), lambda i: (0, 0))],
        out_shape=[_sds((T, D_MODEL), F32), _sds((T, D_FF), BF16), _sds((T, D_FF), BF16), _sds((T, D_FF), BF16),
                   _sds((1, D_MODEL), F32)],
        compiler_params=_params(("arbitrary",), FFN_VMEM_LIMIT),
    )(h, g, dho, a1, a3, w1, w3, w2)


def mm_tn(a, b, name, scale=1.0, a_cols=None, b_cols=None):
    T = a.shape[0]
    a_off, M = a_cols or (0, a.shape[1])
    b_off, N = b_cols or (0, b.shape[1])
    bm = 512 if M % 512 == 0 else (1408 if M == 2816 else 256)
    bn = N if N in (2176, 1408) else (1408 if N == 2816 else min(N, 1024))
    tk = _tile(T, MM_K_TILE)
    row_bytes = 2 * (bm * a.dtype.itemsize + bn * b.dtype.itemsize)
    while tk > TOKEN_TILE and tk * row_bytes > VMEM_LIMIT // 3:
        tk //= 2
    assert M % bm == 0 and N % bn == 0 and T % tk == 0 and a_off % bm == 0 and b_off % bn == 0
    n_k = T // tk
    m0, n0 = a_off // bm, b_off // bn

    def body(a_ref, b_ref, o_ref):
        k = pl.program_id(2)

        @pl.when(k == 0)
        def _():
            o_ref[...] = jnp.zeros_like(o_ref)

        o_ref[...] += _dot_tn(a_ref[...].astype(BF16), b_ref[...].astype(BF16))

        if scale != 1.0:
            @pl.when(k == n_k - 1)
            def _():
                o_ref[...] = o_ref[...] * scale

    return pl.pallas_call(
        body, name=name, grid=(M // bm, N // bn, n_k),
        in_specs=[pl.BlockSpec((tk, bm), lambda m, n, k: (k, m0 + m)), pl.BlockSpec((tk, bn), lambda m, n, k: (k, n0 + n))],
        out_specs=pl.BlockSpec((bm, bn), lambda m, n, k: (m, n)),
        out_shape=_sds((M, N), F32),
        compiler_params=_params(("arbitrary", "arbitrary", "arbitrary")),
    )(a, b)


def mixin_fwd(h1, g, w_in_r, b_f_pad):
    T = h1.shape[0]
    tm = _tile(T, TOKEN_TILE)

    def body(h_ref, g_ref, w_ref, bf_ref, u_ref, qkv_ref, s_ref, fz_ref, c_ref, carry):
        @pl.when(pl.program_id(0) == 0)
        def _():
            carry[...] = jnp.zeros_like(carry)

        x = h_ref[...]
        u = (x * _rms_scale(x) * g_ref[...]).astype(BF16)
        u_ref[...] = u
        z = _dot(u, w_ref[...])
        qkv_ref[...] = z[:, :3 * ATTN_W].astype(BF16)
        s_ref[...] = z[:, 3 * ATTN_W:3 * ATTN_W + SSM_W]
        fz = z[:, 3 * ATTN_W + SSM_W:] + bf_ref[...]
        fz_ref[...] = fz
        lane = lax.broadcasted_iota(jnp.int32, fz.shape, 1)
        logf = jnp.where(lane < N_HEADS, jnp.minimum(fz, 0.0) - jnp.log(1.0 + jnp.exp(-jnp.abs(fz))), 0.0)
        row = lax.broadcasted_iota(jnp.int32, (tm, tm), 0)
        col = lax.broadcasted_iota(jnp.int32, (tm, tm), 1)
        tri = (col <= row).astype(F32)
        cs = jnp.dot(tri, logf, precision=HIGHEST, preferred_element_type=F32) + carry[0:1, :]
        c_ref[...] = cs
        carry[...] = jnp.broadcast_to(cs[tm - 1:tm, :], carry.shape)

    tok = lambda i: (i, 0)
    fix = lambda i: (0, 0)
    return pl.pallas_call(
        body, name="mixin_fwd", grid=(T // tm,),
        in_specs=[pl.BlockSpec((tm, D_MODEL), tok), pl.BlockSpec((1, D_MODEL), fix),
                  pl.BlockSpec((D_MODEL, Z_COLS), fix), pl.BlockSpec((1, 128), fix)],
        out_specs=[pl.BlockSpec((tm, D_MODEL), tok), pl.BlockSpec((tm, 3 * ATTN_W), tok), pl.BlockSpec((tm, SSM_W), tok),
                   pl.BlockSpec((tm, 128), tok), pl.BlockSpec((tm, 128), tok)],
        out_shape=[_sds((T, D_MODEL), BF16), _sds((T, 3 * ATTN_W), BF16), _sds((T, SSM_W), F32),
                   _sds((T, 128), F32), _sds((T, 128), F32)],
        scratch_shapes=[pltpu.VMEM((8, 128), F32)],
        compiler_params=_params(("arbitrary",)),
    )(h1, g, w_in_r, b_f_pad)


def mixin_bwd(dh2, h1, g, w_in_r, dq, dk, dv, ds, dc, fz):
    T = h1.shape[0]
    tm = _tile(T, TOKEN_TILE)
    n_t = T // tm

    def body(dh2_ref, h_ref, g_ref, w_ref, dq_ref, dk_ref, dv_ref, ds_ref, dc_ref, fz_ref,
             dh1_ref, dz_ref, dg_ref, dbf_ref, carry):
        @pl.when(pl.program_id(0) == 0)
        def _():
            carry[...] = jnp.zeros_like(carry)
            dg_ref[...] = jnp.zeros_like(dg_ref)
            dbf_ref[...] = jnp.zeros_like(dbf_ref)

        row = lax.broadcasted_iota(jnp.int32, (tm, tm), 0)
        col = lax.broadcasted_iota(jnp.int32, (tm, tm), 1)
        tri = (col >= row).astype(F32)
        dlogf = jnp.dot(tri, dc_ref[...], precision=HIGHEST, preferred_element_type=F32) + carry[0:1, :]
        carry[...] = jnp.broadcast_to(dlogf[0:1, :], carry.shape)
        dfz = dlogf * jax.nn.sigmoid(-fz_ref[...])
        dbf_ref[...] += jnp.sum(dfz, axis=0, keepdims=True)
        dz = jnp.concatenate([_join_heads(dq_ref, BF16), _join_heads(dk_ref, BF16), _join_heads(dv_ref, BF16),
                              ds_ref[...], dfz], axis=1).astype(BF16)
        dz_ref[...] = dz
        du = _dot_nt(dz, w_ref[...])
        dx, dg = _rms_bwd(du, h_ref[...], g_ref[...])
        dg_ref[...] += dg
        dh1_ref[...] = dh2_ref[...] + dx

    tok = lambda i: (n_t - 1 - i, 0)
    fix = lambda i: (0, 0)
    heads = pl.BlockSpec((N_HEADS, tm, 128), lambda i: (0, n_t - 1 - i, 0))
    return pl.pallas_call(
        body, name="mixin_bwd", grid=(n_t,),
        in_specs=[pl.BlockSpec((tm, D_MODEL), tok), pl.BlockSpec((tm, D_MODEL), tok), pl.BlockSpec((1, D_MODEL), fix),
                  pl.BlockSpec((D_MODEL, Z_COLS), fix), heads, heads, heads, pl.BlockSpec((tm, SSM_W), tok),
                  pl.BlockSpec((tm, 128), tok), pl.BlockSpec((tm, 128), tok)],
        out_specs=[pl.BlockSpec((tm, D_MODEL), tok), pl.BlockSpec((tm, Z_COLS), tok), pl.BlockSpec((1, D_MODEL), fix),
                   pl.BlockSpec((1, 128), fix)],
        out_shape=[_sds((T, D_MODEL), F32), _sds((T, Z_COLS), BF16), _sds((1, D_MODEL), F32), _sds((1, 128), F32)],
        scratch_shapes=[pltpu.VMEM((8, 128), F32)],
        compiler_params=_params(("arbitrary",)),
    )(dh2, h1, g, w_in_r, dq, dk, dv, ds, dc, fz)


def _lane_move(src_lo, dst_lo, width, dtype):
    r = lax.broadcasted_iota(jnp.int32, (128, 128), 0)
    c = lax.broadcasted_iota(jnp.int32, (128, 128), 1)
    return ((c - dst_lo == r - src_lo) & (r >= src_lo) & (r < src_lo + width)).astype(dtype)


def _lane_const(lo, width, value):
    lane = lax.broadcasted_iota(jnp.int32, (1, 128), 1)
    return jnp.where((lane >= lo) & (lane < lo + width), value, 0.0).astype(F32)


def _pieces(a):
    hi = a.astype(BF16)
    rest = a - hi.astype(F32)
    mid = rest.astype(BF16)
    return hi, mid, (rest - mid.astype(F32)).astype(BF16)


def _head_features(pair_block, e):
    return _dot(pair_block, _lane_move(HEAD_DIM * e, 0, HEAD_DIM, BF16))


def _helper_columns(pieces, head, sign):
    out = None
    for k, piece in enumerate(pieces):
        term = _dot(piece, _lane_move(head, HEAD_DIM + k, 1, BF16))
        out = term if out is None else out + term
    return sign * out


def heads_in(qkv, cum):
    T = qkv.shape[0]
    tm = _tile(T, TOKEN_TILE)

    def body(qkv_ref, c_ref, q_ref, k_ref, v_ref):
        c = _pieces(c_ref[...])
        for h in range(N_HEADS):
            p, e = divmod(h, 2)
            blk = lambda base: qkv_ref[:, base + 128 * p:base + 128 * (p + 1)]
            q_ref[h] = (_head_features(blk(0), e) + _lane_const(HEAD_DIM, 3, -1.0)).astype(BF16)
            k_ref[h] = (_head_features(blk(ATTN_W), e) + _helper_columns(c, h, 1.0)
                        + _lane_const(HEAD_DIM + 3, 3, 1.0)).astype(BF16)
            v_ref[h] = (_head_features(blk(2 * ATTN_W), e) + _lane_const(HEAD_DIM, 3, 1.0)).astype(BF16)

    tok = lambda i: (i, 0)
    heads = pl.BlockSpec((N_HEADS, tm, 128), lambda i: (0, i, 0))
    return pl.pallas_call(
        body, name="heads_in", grid=(T // tm,),
        in_specs=[pl.BlockSpec((tm, 3 * ATTN_W), tok), pl.BlockSpec((tm, 128), tok)],
        out_specs=[heads] * 3, out_shape=[_sds((N_HEADS, T, 128), BF16)] * 3,
        compiler_params=_params(("arbitrary",)),
    )(qkv, cum)


def attn_fwd(q_aug, k_aug, v_aug, send=None):
    H, T, wd = q_aug.shape
    hd = HEAD_DIM
    tq = _tile(T, ATTN_TILE)
    n = T // tq

    def body(q_ref, k_ref, v_ref, o_ref, qb_ref, m_sc, acc, s_even, s_odd):
        qi = pl.program_id(1)
        qv = q_ref[0]
        m_sc[...] = jnp.full_like(m_sc, -jnp.inf)
        acc[...] = jnp.zeros_like(acc)

        def key_rows(j):
            return pl.ds(pl.multiple_of(jnp.minimum(j, qi) * tq, tq), tq)

        def logits(j, buf):
            buf[...] = _dot_nt(k_ref[0, key_rows(j), :], qv)

        def update(j, buf, masked):
            st = buf[...]
            if masked:
                keep = lax.broadcasted_iota(jnp.int32, (tq, tq), 0) <= lax.broadcasted_iota(jnp.int32, (tq, tq), 1)
                st = jnp.where(keep, st, -1e30)
            m_old = m_sc[...]
            m_new = jnp.maximum(m_old, jnp.max(st, axis=0, keepdims=True))
            pt = jnp.exp(st - m_new).astype(BF16)
            acc[...] = jnp.exp(m_old - m_new) * acc[...] + _dot_tn(v_ref[0, key_rows(j), :], pt)
            m_sc[...] = m_new

        logits(0, s_even)

        def two_tiles(p, carry):
            j = 2 * p
            logits(j + 1, s_odd)
            update(j, s_even, False)
            logits(j + 2, s_even)
            update(j + 1, s_odd, False)
            return carry

        lax.fori_loop(0, qi // 2, two_tiles, 0)

        @pl.when(qi % 2 == 0)
        def _():
            update(qi, s_even, True)

        @pl.when(qi % 2 == 1)
        def _():
            logits(qi, s_odd)
            update(qi - 1, s_even, False)
            update(qi, s_odd, True)

        total = acc[hd:hd + 1, :]
        o_ref[0] = (acc[...] / total).T
        hi, mid, lo = (t.astype(F32) for t in _pieces(-(m_sc[...] + jnp.log(total))))
        row = lax.broadcasted_iota(jnp.int32, (wd, tq), 0)
        lse_rows = jnp.where(row == hd + 3, hi, jnp.where(row == hd + 4, mid, jnp.where(row == hd + 5, lo, 0.0)))
        qb_ref[0] = (qv.astype(F32) + lse_rows.T).astype(BF16)

    qmap = lambda h, i: (h, i, 0)
    head = lambda h, i: (h, 0, 0)
    in_specs = [pl.BlockSpec((1, tq, wd), qmap), pl.BlockSpec((1, T, wd), head), pl.BlockSpec((1, T, wd), head)]
    out_specs = [pl.BlockSpec((1, tq, wd), qmap), pl.BlockSpec((1, tq, wd), qmap)]
    out_shape = [_sds((H, T, wd), F32), _sds((H, T, wd), BF16)]
    scratch = [pltpu.VMEM((1, tq), F32), pltpu.VMEM((wd, tq), F32), pltpu.VMEM((tq, tq), F32), pltpu.VMEM((tq, tq), F32)]
    operands = (q_aug, k_aug, v_aug)
    if send is None:
        kernel_body = body
    else:
        def kernel_body(q_ref, k_ref, v_ref, send_ref, o_ref, qb_ref, got_ref, m_sc, acc, s_even, s_odd, send_sems, recv_sems):
            h, i = pl.program_id(0), pl.program_id(1)

            @pl.when((h == 0) & (i == 0))
            def _():
                for cp in _half_copies(send_ref, got_ref, send_sems, recv_sems):
                    cp.start()

            body(q_ref, k_ref, v_ref, o_ref, qb_ref, m_sc, acc, s_even, s_odd)

            @pl.when((h == H - 1) & (i == n - 1))
            def _():
                for cp in _half_copies(send_ref, got_ref, send_sems, recv_sems):
                    cp.wait()

        rh = send.shape[0] // 2
        sems = pltpu.SemaphoreType.DMA((3 * len(_spans(rh, COPY_CHUNKS)),))
        in_specs, out_specs = in_specs + [_HBM], out_specs + [_HBM]
        out_shape = out_shape + [_sds((3, rh, send.shape[1]), send.dtype)]
        scratch, operands = scratch + [sems, sems], operands + (send,)
    return pl.pallas_call(
        kernel_body, name="attn_fwd", grid=(H, n), in_specs=in_specs, out_specs=out_specs, out_shape=out_shape,
        scratch_shapes=scratch, compiler_params=_params(("arbitrary", "arbitrary")),
    )(*operands)


def attn_bwd(q_aug, k_aug, v_aug, do_aug, pair=None):
    H, T, wd = q_aug.shape
    tq = _tile(T, ATTN_TILE)
    n = T // tq

    def compute(q_ref, do_ref, k_ref, v_ref, dq_ref, dk_ref, dv_ref, dc_ref, dck, s_a, d_a, s_b, d_b):
        j = pl.program_id(1)

        @pl.when(j == 0)
        def _():
            dq_ref[...] = jnp.zeros_like(dq_ref)
            dc_ref[...] = jnp.zeros_like(dc_ref)

        dk_ref[...] = jnp.zeros_like(dk_ref)
        dv_ref[...] = jnp.zeros_like(dv_ref)
        dck[...] = jnp.zeros_like(dck)
        kv, vv = k_ref[0], v_ref[0]

        def query_rows(i):
            return pl.ds(pl.multiple_of(jnp.minimum(i, n - 1) * tq, tq), tq)

        def products(i, s_buf, d_buf):
            rows = query_rows(i)
            s_buf[...] = _dot_nt(kv, q_ref[0, rows, :])
            d_buf[...] = _dot_nt(vv, do_ref[0, rows, :])

        def update(i, s_buf, d_buf, masked):
            rows = query_rows(i)
            qv, dov = q_ref[0, rows, :], do_ref[0, rows, :]
            pt = jnp.exp(s_buf[...])
            if masked:
                keep = lax.broadcasted_iota(jnp.int32, (tq, tq), 0) <= lax.broadcasted_iota(jnp.int32, (tq, tq), 1)
                pt = jnp.where(keep, pt, 0.0)
            dv_ref[0] += _dot(pt.astype(BF16), dov)
            dst = pt * d_buf[...]
            dsb = dst.astype(BF16)
            dk_ref[0] += _dot(dsb, qv)
            dq_ref[0, rows, :] += _dot_tn(dsb, kv)
            dck[...] += jnp.sum(dst, axis=1, keepdims=True)
            dc_ref[0, pl.ds(i, 1), :] += jnp.sum(dst, axis=0, keepdims=True)

        products(j, s_a, d_a)
        products(j + 1, s_b, d_b)
        update(j, s_a, d_a, True)
        left = n - 1 - j

        def two_tiles(p, carry):
            i = j + 1 + 2 * p
            products(i + 1, s_a, d_a)
            update(i, s_b, d_b, False)
            products(i + 2, s_b, d_b)
            update(i + 1, s_a, d_a, False)
            return carry

        lax.fori_loop(0, left // 2, two_tiles, 0)

        @pl.when(left % 2 == 1)
        def _():
            update(n - 1, s_b, d_b, False)

        dc_ref[0, pl.ds(j, 1), :] -= jnp.broadcast_to(dck[...], (tq, 128)).T[0:1, :]

    head = lambda h, j: (h, 0, 0)
    kmap = lambda h, j: (h, j, 0)
    in_specs = [pl.BlockSpec((1, T, wd), head), pl.BlockSpec((1, T, wd), head), pl.BlockSpec((1, tq, wd), kmap),
                pl.BlockSpec((1, tq, wd), kmap)]
    out_specs = [pl.BlockSpec((1, T, wd), head), pl.BlockSpec((1, tq, wd), kmap), pl.BlockSpec((1, tq, wd), kmap),
                 pl.BlockSpec((1, n, tq), head)]
    out_shape = [_sds((H, T, wd), F32), _sds((H, T, wd), F32), _sds((H, T, wd), F32), _sds((H, n, tq), F32)]
    scratch = [pltpu.VMEM((tq, 1), F32)] + [pltpu.VMEM((tq, tq), F32)] * 4
    operands = (q_aug, do_aug, k_aug, v_aug)
    if pair is None:
        body = compute
    else:
        def body(q_ref, do_ref, k_ref, v_ref, pair_ref, dq_ref, dk_ref, dv_ref, dc_ref, got_ref,
                 dck, s_a, d_a, s_b, d_b, send_sems, recv_sems):
            h, j = pl.program_id(0), pl.program_id(1)

            @pl.when((h == 0) & (j == 0))
            def _():
                for cp in _chip_copies(pair_ref, got_ref, send_sems, recv_sems):
                    cp.start()

            compute(q_ref, do_ref, k_ref, v_ref, dq_ref, dk_ref, dv_ref, dc_ref, dck, s_a, d_a, s_b, d_b)

            @pl.when((h == H - 1) & (j == n - 1))
            def _():
                for cp in _chip_copies(pair_ref, got_ref, send_sems, recv_sems):
                    cp.wait()

        sems = pltpu.SemaphoreType.DMA((3 * len(_spans(pair.shape[1], COPY_CHUNKS)),))
        in_specs, out_specs = in_specs + [_HBM], out_specs + [_HBM]
        out_shape = out_shape + [_sds((3,) + pair.shape[1:], pair.dtype)]
        scratch, operands = scratch + [sems, sems], operands + (pair,)
    return pl.pallas_call(
        body, name="attn_bwd", grid=(H, n), in_specs=in_specs, out_specs=out_specs, out_shape=out_shape,
        scratch_shapes=scratch, compiler_params=_params(("arbitrary", "arbitrary")),
    )(*operands)


def _complex_step(a_r, a_i, cr, ci, br, bi):
    return a_r * cr - a_i * ci + br, a_r * ci + a_i * cr + bi


_HALF_CH = SSM_W // 2
_HALF_ST = STATE_W // 2


def _state_cols(part, half):
    lo = part * STATE_W + half * _HALF_ST
    return slice(lo, lo + _HALF_ST)


def _channels_to_states(x, w_ref, out_ref):
    for half in range(2):
        ch = slice(half * _HALF_CH, (half + 1) * _HALF_CH)
        for part in range(2):
            cols = _state_cols(part, half)
            out_ref[:, cols] = _dot(x[:, ch], w_ref[ch, cols])


def _states_to_channels(x, w_ref):
    halves = []
    for half in range(2):
        ch = slice(half * _HALF_CH, (half + 1) * _HALF_CH)
        halves.append(_dot(x[:, _state_cols(0, half)], w_ref[_state_cols(0, half), ch])
                      + _dot(x[:, _state_cols(1, half)], w_ref[_state_cols(1, half), ch]))
    return jnp.concatenate(halves, axis=1)


def ssm_fwd(s_perm, wb, cbd, a_r, a_i, al_r, al_i, dvec):
    T = s_perm.shape[0]
    chunk = T // 8
    ts = _tile(chunk, SCAN_STEPS)
    tr, n_s = ts * 8, chunk // ts
    W, LB = STATE_W, SCAN_LANES

    def body(s_ref, wb_ref, cbd_ref, ar_ref, ai_ref, alr_ref, ali_ref, dv_ref, y_ref, xs_ref, bu, carry):
        ph, i = pl.program_id(0), pl.program_id(1)

        @pl.when((ph == 0) & (i == 0))
        def _():
            carry[...] = jnp.zeros_like(carry)

        _channels_to_states(s_ref[...].astype(BF16), wb_ref, bu)

        def scan(store):
            for lb in range(W // LB):
                lo = lb * LB
                re, im = slice(lo, lo + LB), slice(W + lo, W + lo + LB)
                ar = jnp.broadcast_to(ar_ref[:, re], (8, LB))
                ai = jnp.broadcast_to(ai_ref[:, re], (8, LB))

                def step(s, c):
                    rows = pl.ds(pl.multiple_of(s * 8, 8), 8)
                    nr, ni = _complex_step(ar, ai, c[0], c[1], bu[rows, re], bu[rows, im])
                    if store:
                        bu[rows, re] = nr
                        bu[rows, im] = ni
                    return nr, ni

                cr, ci = lax.fori_loop(0, ts, step, (carry[:, re], carry[:, im]), unroll=2)
                carry[:, re] = cr
                carry[:, im] = ci

        @pl.when(ph == 0)
        def _():
            scan(False)

            @pl.when(i == n_s - 1)
            def _():
                er, ei = carry[:, :W], carry[:, W:]
                alr = jnp.broadcast_to(alr_ref[...], (8, W))
                ali = jnp.broadcast_to(ali_ref[...], (8, W))
                first = lax.broadcasted_iota(jnp.int32, (8, W), 0) == 0
                sr, si = jnp.zeros((8, W), F32), jnp.zeros((8, W), F32)
                for _ in range(7):
                    vr, vi = _complex_step(alr, ali, sr, si, er, ei)
                    sr = jnp.where(first, 0.0, pltpu.roll(vr, 1, 0))
                    si = jnp.where(first, 0.0, pltpu.roll(vi, 1, 0))
                carry[:, :W] = sr
                carry[:, W:] = si

        @pl.when(ph == 1)
        def _():
            scan(True)
            xb = bu[...].astype(BF16)
            xs_ref[...] = xb
            y_ref[...] = _states_to_channels(xb, cbd_ref) + s_ref[...] * dv_ref[...]

    fix = lambda p, i: (0, 0)
    return pl.pallas_call(
        body, name="ssm_fwd", grid=(2, n_s),
        in_specs=[pl.BlockSpec((tr, SSM_W), lambda p, i: (i, 0)), pl.BlockSpec((SSM_W, 2 * W), fix),
                  pl.BlockSpec((2 * W, SSM_W), fix), pl.BlockSpec((1, W), fix), pl.BlockSpec((1, W), fix),
                  pl.BlockSpec((1, W), fix), pl.BlockSpec((1, W), fix), pl.BlockSpec((1, SSM_W), fix)],
        out_specs=[pl.BlockSpec((tr, SSM_W), lambda p, i: (i * p, 0)), pl.BlockSpec((tr, 2 * W), lambda p, i: (i * p, 0))],
        out_shape=[_sds((T, SSM_W), F32), _sds((T, 2 * W), BF16)],
        scratch_shapes=[pltpu.VMEM((tr, 2 * W), F32), pltpu.VMEM((8, 2 * W), F32)],
        compiler_params=_params(("arbitrary", "arbitrary")),
    )(s_perm, wb, cbd, a_r, a_i, al_r, al_i, dvec)


def ssm_bwd(dy_perm, s_perm, xs, cbd_t, wb_t, a_r, a_i, al_r, al_i, dvec):
    T = s_perm.shape[0]
    chunk = T // 8
    ts = _tile(chunk, SCAN_STEPS)
    tr, n_s = ts * 8, chunk // ts
    W, LB = STATE_W, SCAN_LANES

    def body(dy_ref, s_ref, xs_ref, cbt_ref, wbt_ref, ar_ref, ai_ref, alr_ref, ali_ref, dv_ref,
             du_ref, gs_ref, da_ref, dd_ref, gd, x32, carry):
        ph, i = pl.program_id(0), pl.program_id(1)

        @pl.when((ph == 0) & (i == 0))
        def _():
            carry[...] = jnp.zeros_like(carry)
            da_ref[...] = jnp.zeros_like(da_ref)
            dd_ref[...] = jnp.zeros_like(dd_ref)

        _channels_to_states(dy_ref[...].astype(BF16), cbt_ref, gd)

        def scan(store):
            for lb in range(W // LB):
                lo = lb * LB
                re, im = slice(lo, lo + LB), slice(W + lo, W + lo + LB)
                ar = jnp.broadcast_to(ar_ref[:, re], (8, LB))
                nai = -jnp.broadcast_to(ai_ref[:, re], (8, LB))

                def step(k, c):
                    rows = pl.ds(pl.multiple_of((ts - 1 - k) * 8, 8), 8)
                    cr, ci = c[0], c[1]
                    nr, ni = _complex_step(ar, nai, cr, ci, gd[rows, re], gd[rows, im])
                    if store:
                        xr, xi = x32[rows, re], x32[rows, im]
                        gd[rows, re] = nr
                        gd[rows, im] = ni
                        return nr, ni, c[2] + cr * xr + ci * xi, c[3] + ci * xr - cr * xi
                    return nr, ni

                init = (carry[:, re], carry[:, im])
                if store:
                    init = init + (da_ref[:, re], da_ref[:, im])
                out = lax.fori_loop(0, ts, step, init, unroll=2)
                carry[:, re] = out[0]
                carry[:, im] = out[1]
                if store:
                    da_ref[:, re] = out[2]
                    da_ref[:, im] = out[3]

        @pl.when(ph == 0)
        def _():
            scan(False)

            @pl.when(i == n_s - 1)
            def _():
                er, ei = carry[:, :W], carry[:, W:]
                alr = jnp.broadcast_to(alr_ref[...], (8, W))
                nali = -jnp.broadcast_to(ali_ref[...], (8, W))
                last = lax.broadcasted_iota(jnp.int32, (8, W), 0) == 7
                rr, ri = jnp.zeros((8, W), F32), jnp.zeros((8, W), F32)
                for _ in range(7):
                    vr, vi = _complex_step(alr, nali, rr, ri, er, ei)
                    rr = jnp.where(last, 0.0, pltpu.roll(vr, 7, 0))
                    ri = jnp.where(last, 0.0, pltpu.roll(vi, 7, 0))
                carry[:, :W] = rr
                carry[:, W:] = ri

        @pl.when(ph == 1)
        def _():
            x32[...] = xs_ref[...].astype(F32)
            scan(True)
            gb = gd[...].astype(BF16)
            gs_ref[...] = gb
            dy = dy_ref[...]
            du_ref[...] = _states_to_channels(gb, wbt_ref) + dy * dv_ref[...]
            dd_ref[...] += jnp.sum(dy * s_ref[...], axis=0, keepdims=True)

    fix = lambda p, i: (0, 0)
    rev = lambda p, i: (n_s - 1 - i, 0)
    rev_out = lambda p, i: (n_s - 1 - i * p, 0)
    return pl.pallas_call(
        body, name="ssm_bwd", grid=(2, n_s),
        in_specs=[pl.BlockSpec((tr, SSM_W), rev), pl.BlockSpec((tr, SSM_W), rev), pl.BlockSpec((tr, 2 * W), rev),
                  pl.BlockSpec((SSM_W, 2 * W), fix), pl.BlockSpec((2 * W, SSM_W), fix), pl.BlockSpec((1, W), fix),
                  pl.BlockSpec((1, W), fix), pl.BlockSpec((1, W), fix), pl.BlockSpec((1, W), fix),
                  pl.BlockSpec((1, SSM_W), fix)],
        out_specs=[pl.BlockSpec((tr, SSM_W), rev_out), pl.BlockSpec((tr, 2 * W), rev_out),
                   pl.BlockSpec((8, 2 * W), fix), pl.BlockSpec((1, SSM_W), fix)],
        out_shape=[_sds((T, SSM_W), F32), _sds((T, 2 * W), BF16), _sds((8, 2 * W), F32), _sds((1, SSM_W), F32)],
        scratch_shapes=[pltpu.VMEM((tr, 2 * W), F32), pltpu.VMEM((tr, 2 * W), F32), pltpu.VMEM((8, 2 * W), F32)],
        compiler_params=_params(("arbitrary", "arbitrary")),
    )(dy_perm, s_perm, xs, cbd_t, wb_t, a_r, a_i, al_r, al_i, dvec)


def _join_heads(ref, dtype):
    def move(h, dst):
        x = ref[h]
        pieces = _pieces(x) if dtype == F32 else (x.astype(BF16),)
        out = None
        for piece in pieces:
            term = _dot(piece, _lane_move(0, dst, HEAD_DIM, BF16))
            out = term if out is None else out + term
        return out

    return jnp.concatenate([move(2 * p, 0) + move(2 * p + 1, HEAD_DIM) for p in range(N_HEADS // 2)], axis=1)


def mixout_fwd(h1, o_heads, ypre, g_a, g_s, w_glu, b_glu, w_out):
    T = h1.shape[0]
    tm = _tile(T, TOKEN_TILE)

    def body(h_ref, at_ref, yp_ref, ga_ref, gs_ref, wg_ref, bg_ref, wo_ref, h2_ref, mixed_ref):
        yg, _ = _gelu_parts(yp_ref[...])
        gl = yg * jax.nn.sigmoid(_dot(yg.astype(BF16), wg_ref[...]) + bg_ref[...])
        at = _join_heads(at_ref, F32)
        mixed = jnp.concatenate([at * _rms_scale(at) * ga_ref[...], gl * _rms_scale(gl) * gs_ref[...]], axis=1)
        mixed = mixed.astype(BF16)
        mixed_ref[...] = mixed
        h2_ref[...] = h_ref[...] + _dot(mixed, wo_ref[...])

    tok = lambda i: (i, 0)
    fix = lambda i: (0, 0)
    return pl.pallas_call(
        body, name="mixout_fwd", grid=(T // tm,),
        in_specs=[pl.BlockSpec((tm, D_MODEL), tok), pl.BlockSpec((N_HEADS, tm, 128), lambda i: (0, i, 0)),
                  pl.BlockSpec((tm, SSM_W), tok),
                  pl.BlockSpec((1, ATTN_W), fix), pl.BlockSpec((1, SSM_W), fix), pl.BlockSpec((SSM_W, SSM_W), fix),
                  pl.BlockSpec((1, SSM_W), fix), pl.BlockSpec((D_MODEL, D_MODEL), fix)],
        out_specs=[pl.BlockSpec((tm, D_MODEL), tok), pl.BlockSpec((tm, D_MODEL), tok)],
        out_shape=[_sds((T, D_MODEL), F32), _sds((T, D_MODEL), BF16)],
        compiler_params=_params(("arbitrary",)),
    )(h1, o_heads, ypre, g_a, g_s, w_glu, b_glu, w_out)


def mixout_bwd(dh2, o_heads, ypre, g_a, g_s, w_glu, b_glu, w_out, seg):
    T = dh2.shape[0]
    tm = _tile(T, TOKEN_TILE)

    def body(dh_ref, at_ref, yp_ref, ga_ref, gs_ref, wg_ref, bg_ref, wo_ref, seg_ref,
             do_ref, dyp_ref, dpre_ref, yg_ref, dga_ref, dgs_ref, dbg_ref):
        @pl.when(pl.program_id(0) == 0)
        def _():
            dga_ref[...] = jnp.zeros_like(dga_ref)
            dgs_ref[...] = jnp.zeros_like(dgs_ref)
            dbg_ref[...] = jnp.zeros_like(dbg_ref)

        dmix = _dot_nt(dh_ref[...].astype(BF16), wo_ref[...])
        at = _join_heads(at_ref, F32)
        dat, dga = _rms_bwd(dmix[:, :ATTN_W], at, ga_ref[...])
        dga_ref[...] += dga
        delta = _pieces(jnp.dot(dat * at, seg_ref[...], precision=HIGHEST, preferred_element_type=F32))
        datb = dat.astype(BF16)
        for h in range(N_HEADS):
            p, e = divmod(h, 2)
            do_ref[h] = (_head_features(datb[:, 128 * p:128 * (p + 1)], e) + _helper_columns(delta, h, -1.0)).astype(BF16)
        yp = yp_ref[...]
        yg, t = _gelu_parts(yp)
        ygb = yg.astype(BF16)
        yg_ref[...] = ygb
        sg = jax.nn.sigmoid(_dot(ygb, wg_ref[...]) + bg_ref[...])
        dgl, dgs = _rms_bwd(dmix[:, ATTN_W:], yg * sg, gs_ref[...])
        dgs_ref[...] += dgs
        dpre = dgl * yg * sg * (1.0 - sg)
        dbg_ref[...] += jnp.sum(dpre, axis=0, keepdims=True)
        dpb = dpre.astype(BF16)
        dpre_ref[...] = dpb
        dyg = dgl * sg + _dot_nt(dpb, wg_ref[...])
        dyp_ref[...] = dyg * _gelu_grad(yp, t)

    tok = lambda i: (i, 0)
    fix = lambda i: (0, 0)
    heads = pl.BlockSpec((N_HEADS, tm, 128), lambda i: (0, i, 0))
    return pl.pallas_call(
        body, name="mixout_bwd", grid=(T // tm,),
        in_specs=[pl.BlockSpec((tm, D_MODEL), tok), heads, pl.BlockSpec((tm, SSM_W), tok),
                  pl.BlockSpec((1, ATTN_W), fix), pl.BlockSpec((1, SSM_W), fix), pl.BlockSpec((SSM_W, SSM_W), fix),
                  pl.BlockSpec((1, SSM_W), fix), pl.BlockSpec((D_MODEL, D_MODEL), fix), pl.BlockSpec((ATTN_W, 128), fix)],
        out_specs=[heads, pl.BlockSpec((tm, SSM_W), tok), pl.BlockSpec((tm, SSM_W), tok),
                   pl.BlockSpec((tm, SSM_W), tok), pl.BlockSpec((1, ATTN_W), fix),
                   pl.BlockSpec((1, SSM_W), fix), pl.BlockSpec((1, SSM_W), fix)],
        out_shape=[_sds((N_HEADS, T, 128), BF16), _sds((T, SSM_W), F32), _sds((T, SSM_W), BF16), _sds((T, SSM_W), BF16),
                   _sds((1, ATTN_W), F32), _sds((1, SSM_W), F32), _sds((1, SSM_W), F32)],
        compiler_params=_params(("arbitrary",)),
    )(dh2, o_heads, ypre, g_a, g_s, w_glu, b_glu, w_out, seg)


def head_fwd_bwd(h3, p, target, g_ple, g_final, w_gate, w_proj):
    T = h3.shape[0]
    tm = _tile(T, TOKEN_TILE)
    pd = p.shape[1]

    def body(h_ref, p_ref, tg_ref, gp_ref, gf_ref, wg_ref, wp_ref,
             dh_ref, n3_ref, dz_ref, dpp_ref, loss_ref, dgp_ref, dgf_ref):
        @pl.when(pl.program_id(0) == 0)
        def _():
            loss_ref[...] = jnp.zeros_like(loss_ref)
            dgp_ref[...] = jnp.zeros_like(dgp_ref)
            dgf_ref[...] = jnp.zeros_like(dgf_ref)

        x = h_ref[...]
        gp, gf = gp_ref[...], gf_ref[...]
        n3 = (x * _rms_scale(x) * gp).astype(BF16)
        n3_ref[...] = n3
        gate = jax.nn.sigmoid(_dot(n3, wg_ref[...]))
        pp = _dot(p_ref[...].astype(BF16), wp_ref[...])
        h4 = x + gate * pp
        y = h4 * _rms_scale(h4) * gf
        e = y - tg_ref[...]
        tile_loss = jnp.sum(jnp.sum(e * e, axis=1, keepdims=True), axis=0, keepdims=True) * (0.5 / D_MODEL)
        loss_ref[...] += jnp.broadcast_to(tile_loss, loss_ref.shape)
        dh4, dgf = _rms_bwd(e * (1.0 / D_MODEL), h4, gf)
        dgf_ref[...] += dgf
        dzg = dh4 * pp * gate * (1.0 - gate)
        dzb = dzg.astype(BF16)
        dz_ref[...] = dzb
        dpp_ref[...] = (dh4 * gate).astype(BF16)
        dx, dgp = _rms_bwd(_dot_nt(dzb, wg_ref[...]), x, gp)
        dgp_ref[...] += dgp
        dh_ref[...] = dh4 + dx

    tok = lambda i: (i, 0)
    fix = lambda i: (0, 0)
    return pl.pallas_call(
        body, name="head_fwd_bwd", grid=(T // tm,),
        in_specs=[pl.BlockSpec((tm, D_MODEL), tok), pl.BlockSpec((tm, pd), tok), pl.BlockSpec((tm, D_MODEL), tok),
                  pl.BlockSpec((1, D_MODEL), fix), pl.BlockSpec((1, D_MODEL), fix), pl.BlockSpec((D_MODEL, D_MODEL), fix),
                  pl.BlockSpec((pd, D_MODEL), fix)],
        out_specs=[pl.BlockSpec((tm, D_MODEL), tok), pl.BlockSpec((tm, D_MODEL), tok), pl.BlockSpec((tm, D_MODEL), tok),
                   pl.BlockSpec((tm, D_MODEL), tok), pl.BlockSpec((8, 128), fix), pl.BlockSpec((1, D_MODEL), fix),
                   pl.BlockSpec((1, D_MODEL), fix)],
        out_shape=[_sds((T, D_MODEL), F32), _sds((T, D_MODEL), BF16), _sds((T, D_MODEL), BF16), _sds((T, D_MODEL), BF16),
                   _sds((8, 128), F32), _sds((1, D_MODEL), F32), _sds((1, D_MODEL), F32)],
        compiler_params=_params(("arbitrary",)),
    )(h3, p, target, g_ple, g_final, w_gate, w_proj)


def _row_tile(rows, cols, n_arrays):
    lanes = -(-cols // 128) * 128
    cap = VMEM_LIMIT // 3 // (2 * n_arrays * lanes * 4)
    best = None
    for t in range(PACK_ALIGN, min(rows, cap) + 1, PACK_ALIGN):
        if rows % t == 0:
            best = t
    assert best is not None, (rows, cols)
    return best


def _adamw_math(w, g, m, v):
    nm = ADAM_B1 * m + (1.0 - ADAM_B1) * g
    nv = ADAM_B2 * v + (1.0 - ADAM_B2) * (g * g)
    c1 = 1.0 - ADAM_B1 ** ADAM_STEP
    c2 = 1.0 - ADAM_B2 ** ADAM_STEP
    return -ADAM_LR * ((nm / c1) / (jnp.sqrt(nv / c2) + ADAM_EPS) + ADAM_WD * w), nm, nv


def adamw(w, g, m, v, name):
    R, C = w.shape
    tr = _row_tile(R, C, 7)

    def body(w_ref, g_ref, m_ref, v_ref, d_ref, nm_ref, nv_ref):
        d_ref[...], nm_ref[...], nv_ref[...] = _adamw_math(w_ref[...], g_ref[...], m_ref[...], v_ref[...])

    spec = pl.BlockSpec((tr, C), lambda i: (i, 0))
    return pl.pallas_call(
        body, name=name, grid=(R // tr,), in_specs=[spec] * 4, out_specs=[spec] * 3,
        out_shape=[_sds((R, C), F32)] * 3, compiler_params=_params(("arbitrary",)),
    )(w, g, m, v)


def join_halves(mine, other, core):
    rh, C = mine.shape
    tr = _row_tile(rh, C, 3)
    nb = rh // tr

    def body(c_ref, m_ref, o_ref, out_ref):
        out_ref[...] = jnp.where((pl.program_id(0) // nb) == c_ref[0], m_ref[...], o_ref[...])

    half = pl.BlockSpec((tr, C), lambda i, c: (i % nb, 0))
    return pl.pallas_call(
        body, name="join_halves",
        grid_spec=pltpu.PrefetchScalarGridSpec(num_scalar_prefetch=1, grid=(2 * nb,), in_specs=[half, half],
                                               out_specs=pl.BlockSpec((tr, C), lambda i, c: (i, 0))),
        out_shape=_sds((2 * rh, C), F32), compiler_params=_params(("arbitrary",)),
    )(core, mine, other)


def pair_sum(g, theirs, core):
    n, R, C = g.shape
    rh = R // 2
    tr = _row_tile(rh, C, 3)
    nb = rh // tr

    def body(c_ref, g_ref, t_ref, o_ref):
        o_ref[...] = (g_ref[...] + t_ref[...]).astype(BF16)

    here = pl.BlockSpec((1, tr, C), lambda j, i, c: (j, i, 0))
    return pl.pallas_call(
        body, name="pair_sum",
        grid_spec=pltpu.PrefetchScalarGridSpec(
            num_scalar_prefetch=1, grid=(n, nb),
            in_specs=[pl.BlockSpec((1, tr, C), lambda j, i, c: (j, c[0] * nb + i, 0)), here], out_specs=here),
        out_shape=_sds((n, rh, C), BF16), compiler_params=_params(("arbitrary", "arbitrary")),
    )(core, g, theirs)


def chip_sum(pair, got, chip):
    _, R, C = pair.shape
    tr = _row_tile(R, C, 5)

    def body(c_ref, p_ref, g0_ref, g1_ref, g2_ref, o_ref):
        f = lambda ref: ref[0].astype(F32)
        o_ref[...] = ((f(p_ref) + f(g0_ref)) + f(g1_ref)) + f(g2_ref)

    slot = lambda k: pl.BlockSpec((1, tr, C), lambda i, c: (k, i, 0))
    return pl.pallas_call(
        body, name="chip_sum",
        grid_spec=pltpu.PrefetchScalarGridSpec(
            num_scalar_prefetch=1, grid=(R // tr,),
            in_specs=[pl.BlockSpec((1, tr, C), lambda i, c: (c[0], i, 0)), slot(0), slot(1), slot(2)],
            out_specs=pl.BlockSpec((tr, C), lambda i, c: (i, 0))),
        out_shape=_sds((R, C), F32), compiler_params=_params(("arbitrary",)),
    )(chip, pair, got, got, got)


_HBM = pl.BlockSpec(memory_space=pltpu.HBM)


def _place():
    x, y, c = lax.axis_index("x"), lax.axis_index("y"), lax.axis_index("c")
    return x, y, c, [(1 - x, y), (x, 1 - y), (1 - x, 1 - y)]


def _spans(rows, n):
    assert rows % PACK_ALIGN == 0
    tiles = rows // PACK_ALIGN
    n = min(n, tiles)
    cuts = [tiles * q // n for q in range(n + 1)]
    return [(cuts[q] * PACK_ALIGN, (cuts[q + 1] - cuts[q]) * PACK_ALIGN) for q in range(n)]


def _remote(src, dst, send_sem, recv_sem, to):
    return pltpu.make_async_remote_copy(src_ref=src, dst_ref=dst, send_sem=send_sem, recv_sem=recv_sem,
                                        device_id=to, device_id_type=MESH)


def allgather_shards(wp):
    R, C = wp.shape
    rh = R // 2
    spans = _spans(rh, COPY_CHUNKS)
    n_sp = len(spans)
    local_spans = _spans(R, 2 * COPY_CHUNKS)

    def body(w_ref, out_ref, send_sems, recv_sems, pass_send, pass_recv, local_sems):
        x, y, c, chips = _place()
        me = 2 * x + y
        local = []
        for q, (o, n) in enumerate(local_spans):
            cp = pltpu.make_async_copy(w_ref.at[pl.ds(o, n), :], out_ref.at[me, pl.ds(o, n), :], local_sems.at[q])
            cp.start()
            local.append(cp)
        sends = []
        for k, (cx, cy) in enumerate(chips):
            for q, (o, n) in enumerate(spans):
                rows = pl.ds(c * rh + o, n)
                cp = _remote(w_ref.at[rows, :], out_ref.at[me, rows, :], send_sems.at[k * n_sp + q],
                             recv_sems.at[k * n_sp + q], (cx, cy, c))
                cp.start()
                sends.append(cp)
        for q, (o, n) in enumerate(spans):
            for k, (cx, cy) in enumerate(chips):
                blk = out_ref.at[2 * cx + cy, pl.ds(c * rh + o, n), :]
                _remote(blk, blk, send_sems.at[k * n_sp + q], recv_sems.at[k * n_sp + q], (cx, cy, c)).wait_recv()
                cp = _remote(blk, blk, pass_send.at[k * n_sp + q], pass_recv.at[k * n_sp + q], (x, y, 1 - c))
                cp.start()
                sends.append(cp)
        for k, (cx, cy) in enumerate(chips):
            for q, (o, n) in enumerate(spans):
                blk = out_ref.at[2 * cx + cy, pl.ds((1 - c) * rh + o, n), :]
                _remote(blk, blk, pass_send.at[k * n_sp + q], pass_recv.at[k * n_sp + q], (x, y, 1 - c)).wait_recv()
        for cp in sends:
            cp.wait_send()
        for cp in local:
            cp.wait()

    sems = pltpu.SemaphoreType.DMA((3 * n_sp,))
    return pl.pallas_call(
        body, name="allgather_shards", in_specs=[_HBM], out_specs=_HBM, out_shape=_sds((4, R, C), wp.dtype),
        scratch_shapes=[sems, sems, sems, sems, pltpu.SemaphoreType.DMA((len(local_spans),))],
    )(wp)


def _half_copies(w_ref, got_ref, send_sems, recv_sems):
    rh = w_ref.shape[0] // 2
    spans = _spans(rh, COPY_CHUNKS)
    x, y, c, chips = _place()
    copies = []
    for k, (cx, cy) in enumerate(chips):
        for q, (o, n) in enumerate(spans):
            copies.append(_remote(w_ref.at[pl.ds(c * rh + o, n), :], got_ref.at[k, pl.ds(o, n), :],
                                  send_sems.at[k * len(spans) + q], recv_sems.at[k * len(spans) + q], (cx, cy, c)))
    return copies


def gather_finish(wp, got):
    R, C = wp.shape
    rh = R // 2
    spans = _spans(rh, COPY_CHUNKS)
    n_sp = len(spans)
    local_spans = _spans(R, 2 * COPY_CHUNKS)

    def body(w_ref, got_ref, out_ref, send_sems, recv_sems, local_sems, place_sems):
        x, y, c, chips = _place()
        me = 2 * x + y
        local, passed = [], []
        for q, (o, n) in enumerate(local_spans):
            local.append(pltpu.make_async_copy(w_ref.at[pl.ds(o, n), :], out_ref.at[me, pl.ds(o, n), :], local_sems.at[q]))
        for k, (cx, cy) in enumerate(chips):
            for q, (o, n) in enumerate(spans):
                src = got_ref.at[k, pl.ds(o, n), :]
                dst = out_ref.at[2 * cx + cy, pl.ds(c * rh + o, n), :]
                local.append(pltpu.make_async_copy(src, dst, place_sems.at[k * n_sp + q]))
                passed.append(_remote(src, dst, send_sems.at[k * n_sp + q], recv_sems.at[k * n_sp + q], (x, y, 1 - c)))
        for cp in local + passed:
            cp.start()
        for cp in passed:
            cp.wait_send()
        for cp in local:
            cp.wait()
        for k, (cx, cy) in enumerate(chips):
            for q, (o, n) in enumerate(spans):
                blk = out_ref.at[2 * cx + cy, pl.ds((1 - c) * rh + o, n), :]
                _remote(blk, blk, send_sems.at[k * n_sp + q], recv_sems.at[k * n_sp + q], (x, y, 1 - c)).wait_recv()

    sems = pltpu.SemaphoreType.DMA((3 * n_sp,))
    return pl.pallas_call(
        body, name="gather_finish", in_specs=[_HBM, _HBM], out_specs=_HBM, out_shape=_sds((4, R, C), wp.dtype),
        scratch_shapes=[sems, sems, pltpu.SemaphoreType.DMA((len(local_spans),)), sems],
    )(wp, got)


def sibling_split(g):
    n_sl, R, C = g.shape
    rh = R // 2
    spans = _spans(rh, COPY_CHUNKS)
    n_sp = len(spans)

    def body(g_ref, got_ref, send_sems, recv_sems):
        x, y, c, _ = _place()
        copies = []
        for j in range(n_sl):
            for q, (o, n) in enumerate(spans):
                cp = _remote(g_ref.at[j, pl.ds((1 - c) * rh + o, n), :], got_ref.at[j, pl.ds(o, n), :],
                             send_sems.at[j * n_sp + q], recv_sems.at[j * n_sp + q], (x, y, 1 - c))
                cp.start()
                copies.append(cp)
        for cp in copies:
            cp.wait()

    sems = pltpu.SemaphoreType.DMA((n_sl * n_sp,))
    return pl.pallas_call(
        body, name="sibling_split", in_specs=[_HBM], out_specs=_HBM, out_shape=_sds((n_sl, rh, C), g.dtype),
        scratch_shapes=[sems, sems],
    )(g)


def _chip_copies(p_ref, buf_ref, send_sems, recv_sems):
    rows = p_ref.shape[1]
    spans = _spans(rows, COPY_CHUNKS)
    x, y, c, chips = _place()
    copies = []
    for k, (cx, cy) in enumerate(chips):
        for q, (o, n) in enumerate(spans):
            copies.append(_remote(p_ref.at[2 * cx + cy, pl.ds(o, n), :], buf_ref.at[k, pl.ds(o, n), :],
                                  send_sems.at[k * len(spans) + q], recv_sems.at[k * len(spans) + q], (cx, cy, c)))
    return copies


def chip_exchange(p):
    _, R, C = p.shape

    def body(p_ref, buf_ref, send_sems, recv_sems):
        copies = _chip_copies(p_ref, buf_ref, send_sems, recv_sems)
        for cp in copies:
            cp.start()
        for cp in copies:
            cp.wait()

    sems = pltpu.SemaphoreType.DMA((3 * len(_spans(R, COPY_CHUNKS)),))
    return pl.pallas_call(
        body, name="chip_exchange", in_specs=[_HBM], out_specs=_HBM, out_shape=_sds((3, R, C), p.dtype),
        scratch_shapes=[sems, sems],
    )(p)


def sibling_swap(half):
    R, C = half.shape
    spans = _spans(R, COPY_CHUNKS)

    def body(h_ref, got_ref, send_sems, recv_sems):
        x, y, c, _ = _place()
        copies = []
        for q, (o, n) in enumerate(spans):
            cp = _remote(h_ref.at[pl.ds(o, n), :], got_ref.at[pl.ds(o, n), :], send_sems.at[q], recv_sems.at[q], (x, y, 1 - c))
            cp.start()
            copies.append(cp)
        for cp in copies:
            cp.wait()

    sems = pltpu.SemaphoreType.DMA((len(spans),))
    return pl.pallas_call(
        body, name="sibling_swap", in_specs=[_HBM], out_specs=_HBM, out_shape=_sds((R, C), half.dtype),
        scratch_shapes=[sems, sems],
    )(half)


def allreduce_small(v):
    R, C = v.shape

    def body(v_ref, out_ref, buf, send_sems, recv_sems):
        x, y, c, _ = _place()
        me = 4 * x + 2 * y + c
        buf[me] = v_ref[...]
        flips = [((k >> 2) & 1, (k >> 1) & 1, k & 1) for k in range(1, 8)]
        sends = []
        for k, (fx, fy, fc) in enumerate(flips):
            to = (1 - x if fx else x, 1 - y if fy else y, 1 - c if fc else c)
            cp = _remote(v_ref, buf.at[me], send_sems.at[k], recv_sems.at[k], to)
            cp.start()
            sends.append(cp)
        for k, (fx, fy, fc) in enumerate(flips):
            px, py, pc = (1 - x if fx else x, 1 - y if fy else y, 1 - c if fc else c)
            blk = buf.at[4 * px + 2 * py + pc]
            _remote(blk, blk, send_sems.at[k], recv_sems.at[k], (px, py, pc)).wait_recv()
        for cp in sends:
            cp.wait_send()
        acc = buf[0]
        for s in range(1, 8):
            acc = acc + buf[s]
        out_ref[...] = acc

    vm = pl.BlockSpec(memory_space=pltpu.VMEM)
    return pl.pallas_call(
        body, name="allreduce_small", in_specs=[vm], out_specs=vm, out_shape=_sds((R, C), F32),
        scratch_shapes=[pltpu.VMEM((8, R, C), F32), pltpu.SemaphoreType.DMA((7,)), pltpu.SemaphoreType.DMA((7,))],
        compiler_params=pltpu.CompilerParams(vmem_limit_bytes=VMEM_LIMIT),
    )(v)


def _rows_of(shape):
    return shape[0] * shape[1] // PACK_COLS


def _slot_rows(shape):
    return -(-_rows_of(shape) // PACK_ALIGN) * PACK_ALIGN


TRANSPOSED = ("w1_a", "w3_a", "w1_b", "w3_b")


def _stored(name, shard):
    return shard[0].T if name in TRANSPOSED else shard[0]


def _restored(name, stored):
    return stored.T[None] if name in TRANSPOSED else stored[None]


def _pack_shards(shards, dtype, entries):
    parts = []
    for name, shape, _ in entries:
        part = _stored(name, shards[name]).reshape(_rows_of(shape), PACK_COLS).astype(dtype)
        parts.append(jnp.pad(part, ((0, _slot_rows(shape) - part.shape[0]), (0, 0))))
    used = sum(p.shape[0] for p in parts)
    parts.append(jnp.zeros((_group_rows(entries) - used, PACK_COLS), dtype))
    return jnp.concatenate(parts, axis=0)


def _unpack_gathered(ag, entries):
    out, off = {}, 0
    for name, shape, axis in entries:
        r = _rows_of(shape)
        piece = ag[:, off:off + r, :]
        off += _slot_rows(shape)
        if name in TRANSPOSED:
            out[name] = piece.reshape(4 * r, PACK_COLS)
        elif axis == 0:
            out[name] = piece.reshape(4 * shape[0], shape[1])
        else:
            out[name] = piece.reshape((4,) + shape).transpose(1, 0, 2).reshape(shape[0], 4 * shape[1])
    return out


LATE = ("w_glu", "w_out", "w1_b", "w3_b", "w2_b", "w_ple_gate", "w_ple_proj")
GRAD_GROUPS = (tuple(e for e in BIG if e[0] in LATE), tuple(e for e in BIG if e[0] not in LATE))


GROUP_ROW_UNIT = 2816


def _group_rows(entries):
    used = sum(_slot_rows(shape) for _, shape, _ in entries)
    return -(-used // GROUP_ROW_UNIT) * GROUP_ROW_UNIT


def _pack_full_grads(grads, entries):
    parts = []
    for name, shape, axis in entries:
        g = grads[name]
        if name in TRANSPOSED or axis == 0:
            piece = g.reshape(4, _rows_of(shape), PACK_COLS)
        else:
            piece = g.reshape(shape[0], 4, shape[1]).transpose(1, 0, 2).reshape(4, _rows_of(shape), PACK_COLS)
        parts.append(jnp.pad(piece, ((0, 0), (0, _slot_rows(shape) - piece.shape[1]), (0, 0))))
    used = sum(p.shape[1] for p in parts)
    if _group_rows(entries) > used:
        parts.append(jnp.zeros((4, _group_rows(entries) - used, PACK_COLS), F32))
    return jnp.concatenate(parts, axis=1)


def _unpack_shards(packed, entries):
    out, off = {}, 0
    for name, shape, _ in entries:
        r = _rows_of(shape)
        out[name] = packed[off:off + r] if name in TRANSPOSED else packed[off:off + r].reshape(shape)
        off += _slot_rows(shape)
    return out


def _small_rows(shape):
    return -(-math.prod(shape) // 1024) * 8


def _pack_small(vals, extra=None):
    def slot(v, rows):
        flat = v.reshape(-1)
        return jnp.pad(flat, (0, rows * 128 - flat.shape[0])).reshape(rows, 128)

    parts = [slot(vals[name], _small_rows(shape)) for name, shape in SMALL]
    parts.append(slot(extra if extra is not None else jnp.zeros((1,), F32), 8))
    assert sum(p.shape[0] for p in parts) == SMALL_ROWS
    return jnp.concatenate(parts, axis=0)


def _unpack_small(packed):
    out, off = {}, 0
    for name, shape in SMALL:
        rows = _small_rows(shape)
        out[name] = packed[off:off + rows].reshape(-1)[:math.prod(shape)].reshape(shape)
        off += rows
    return out, packed[off, 0]


def _permute_time(a):
    T, n = a.shape
    return a.reshape(8, T // 8, n).transpose(1, 0, 2).reshape(T, n)


def _unpermute_time(a):
    T, n = a.shape
    return a.reshape(T // 8, 8, n).transpose(1, 0, 2).reshape(T, n)


def _discretize(a_re, a_im, log_dt, b_re, b_im):
    dt = jnp.exp(log_dt)[:, None]
    decay = jnp.exp(dt * a_re)
    abar_r = decay * jnp.cos(dt * a_im)
    abar_i = decay * jnp.sin(dt * a_im)
    nr, ni = abar_r - 1.0, abar_i
    den = a_re * a_re + a_im * a_im
    fr = (nr * a_re + ni * a_im) / den
    fi = (ni * a_re - nr * a_im) / den
    bbar_r = fr[..., None] * b_re - fi[..., None] * b_im
    bbar_i = fr[..., None] * b_im + fi[..., None] * b_re
    return abar_r, abar_i, bbar_r, bbar_i


def _input_matrix(bbar_r, bbar_i):
    eye = jnp.eye(N_GROUPS, dtype=F32)
    blk = lambda b: jnp.einsum("ghp,gk->ghkp", b.transpose(0, 2, 1), eye).reshape(SSM_W, STATE_W)
    return jnp.concatenate([blk(bbar_r), blk(bbar_i)], axis=1)


def _output_matrix(c_re, c_im):
    eye = jnp.eye(N_GROUPS, dtype=F32)
    blk = lambda cm: jnp.einsum("ghp,gk->gpkh", cm, eye).reshape(STATE_W, SSM_W)
    return jnp.concatenate([blk(c_re), -blk(c_im)], axis=0)


def _state_power(ar, ai, n):
    steps = int(round(math.log2(n)))
    assert 1 << steps == n
    for _ in range(steps):
        ar, ai = ar * ar - ai * ai, 2.0 * ar * ai
    return ar, ai


def kernel(x, p, g_ffn1, w1_a, w3_a, w2_a, g_mix, w_in, b_f, a_re, a_im, log_dt, b_re, b_im, c_re, c_im, d_skip, w_glu, b_glu, g_attn_out, g_ssm_out, w_out, g_ffn2, w1_b, w3_b, w2_b, g_ple, w_ple_gate, w_ple_proj, g_final, loss_target, m_g_ffn1, m_w1_a, m_w3_a, m_w2_a, m_g_mix, m_w_in, m_b_f, m_a_re, m_a_im, m_log_dt, m_b_re, m_b_im, m_c_re, m_c_im, m_d_skip, m_w_glu, m_b_glu, m_g_attn_out, m_g_ssm_out, m_w_out, m_g_ffn2, m_w1_b, m_w3_b, m_w2_b, m_g_ple, m_w_ple_gate, m_w_ple_proj, m_g_final, v_g_ffn1, v_w1_a, v_w3_a, v_w2_a, v_g_mix, v_w_in, v_b_f, v_a_re, v_a_im, v_log_dt, v_b_re, v_b_im, v_c_re, v_c_im, v_d_skip, v_w_glu, v_b_glu, v_g_attn_out, v_g_ssm_out, v_w_out, v_g_ffn2, v_w1_b, v_w3_b, v_w2_b, v_g_ple, v_w_ple_gate, v_w_ple_proj, v_g_final):
    args = dict(locals())
    weights = {n: args[n] for n in WEIGHT_ORDER}
    moms = {n: args["m_" + n] for n in WEIGHT_ORDER}
    vars_ = {n: args["v_" + n] for n in WEIGHT_ORDER}
    T = x.shape[1]
    x2, p2, tgt = x[0], p[0, 0], loss_target[0]

    late_entries, early_entries = GRAD_GROUPS
    full = _unpack_gathered(allgather_shards(_pack_shards(weights, BF16, early_entries)), early_entries)
    core = lax.axis_index("c").astype(jnp.int32).reshape(1)
    chip = (2 * lax.axis_index("x") + lax.axis_index("y")).astype(jnp.int32).reshape(1)
    loss_part, dx, grads, late = _local_step(x2, p2, tgt, {n: weights[n] for n, _ in SMALL}, full,
                                             early_exchange=lambda g: _pair_of(g, late_entries, core),
                                             late_pack=_pack_shards(weights, BF16, late_entries))
    return _reduce_and_update(weights, moms, vars_, loss_part, dx, grads, core, chip, late)


def _pair_of(grads, entries, core):
    packed = _pack_full_grads(grads, entries)
    return pair_sum(packed, sibling_split(packed), core)


def _local_step(x2, p2, tgt, sm, full, early_exchange=None, late_pack=None):
    full = dict(full)
    T = x2.shape[0]
    (g_ffn1, g_mix, b_f, a_re, a_im, log_dt, b_re, b_im, c_re, c_im, d_skip, b_glu, g_attn_out, g_ssm_out, g_ffn2, g_ple,
     g_final) = (sm[n] for n, _ in SMALL)
    w_in_f = full["w_in"]
    w_in_r = jnp.concatenate([w_in_f[:, :ATTN_W] * QK_SCALE, w_in_f[:, ATTN_W:3 * ATTN_W], w_in_f[:, 3 * ATTN_W + N_HEADS:],
                              w_in_f[:, 3 * ATTN_W:3 * ATTN_W + N_HEADS], jnp.zeros((D_MODEL, 128 - N_HEADS), BF16)], axis=1)
    b_f_pad = jnp.pad(b_f, ((0, 0), (0, 128 - N_HEADS)))

    disc_in = (a_re[0], a_im[0], log_dt[0], b_re[0], b_im[0])
    (abar_r, abar_i, bbar_r, bbar_i), disc_vjp = jax.vjp(_discretize, *disc_in)
    wb = _input_matrix(bbar_r, bbar_i)
    cbd = _output_matrix(c_re[0], c_im[0])
    ar, ai = abar_r.reshape(1, STATE_W), abar_i.reshape(1, STATE_W)
    alr, ali = _state_power(ar, ai, T // 8)
    dvec = d_skip.reshape(1, SSM_W)
    wb16, cbd16 = wb.astype(BF16), cbd.astype(BF16)

    h1, a1a, a3a, n1 = ffn_fwd(x2, g_ffn1, full["w1_a"], full["w3_a"], full["w2_a"], "ffn_a_fwd")
    u, qkv, s_in, fz, cum = mixin_fwd(h1, g_mix, w_in_r, b_f_pad)
    q_aug, k_aug, v_aug = heads_in(qkv, cum)
    if late_pack is None:
        o_heads, q_bwd = attn_fwd(q_aug, k_aug, v_aug)
    else:
        o_heads, q_bwd, got = attn_fwd(q_aug, k_aug, v_aug, send=late_pack)
        full.update(_unpack_gathered(gather_finish(late_pack, got), GRAD_GROUPS[0]))
    s_perm = _permute_time(s_in)
    y_perm, xs = ssm_fwd(s_perm, wb16, cbd16, ar, ai, alr, ali, dvec)
    ypre = _unpermute_time(y_perm)
    h2, mixed = mixout_fwd(h1, o_heads, ypre, g_attn_out, g_ssm_out, full["w_glu"], b_glu, full["w_out"])
    h3, a1b, a3b, n2 = ffn_fwd(h2, g_ffn2, full["w1_b"], full["w3_b"], full["w2_b"], "ffn_b_fwd")

    dh3, n3, dzg, dpp, loss_part, dg_ple, dg_final = head_fwd_bwd(
        h3, p2, tgt, g_ple, g_final.reshape(1, D_MODEL), full["w_ple_gate"], full["w_ple_proj"])
    grads = {"g_ple": dg_ple, "g_final": dg_final.reshape(D_MODEL)}
    grads["w_ple_gate"] = mm_tn(n3, dzg, "dw_ple_gate")
    grads["w_ple_proj"] = mm_tn(p2, dpp, "dw_ple_proj")

    dh2, da1, da3, act, grads["g_ffn2"] = ffn_bwd(h2, g_ffn2, dh3, a1b, a3b, full["w1_b"], full["w3_b"], full["w2_b"], "ffn_b_bwd")
    grads["w1_b"] = mm_tn(da1, n2, "dw1_b")
    grads["w3_b"] = mm_tn(da3, n2, "dw3_b")
    grads["w2_b"] = mm_tn(act, dh3, "dw2_b", scale=0.5)

    seg = (jnp.arange(ATTN_W)[:, None] // HEAD_DIM == jnp.arange(128)[None, :]).astype(F32)
    do_aug, dypre, dpre, yg, grads["g_attn_out"], grads["g_ssm_out"], grads["b_glu"] = mixout_bwd(
        dh2, o_heads, ypre, g_attn_out, g_ssm_out, full["w_glu"], b_glu, full["w_out"], seg)
    grads["w_out"] = mm_tn(mixed, dh2, "dw_out")
    grads["w_glu"] = mm_tn(yg, dpre, "dw_glu")

    if early_exchange is None:
        late = None
        dq_aug, dk_aug, dv_aug, dc_rows = attn_bwd(q_bwd, k_aug, v_aug, do_aug)
    else:
        pair_late = early_exchange(grads)
        dq_aug, dk_aug, dv_aug, dc_rows, got_late = attn_bwd(q_bwd, k_aug, v_aug, do_aug, pair=pair_late)
        late = (pair_late, got_late)
    dc = jnp.pad(dc_rows.reshape(N_HEADS, T).T, ((0, 0), (0, 128 - N_HEADS)))

    dy_perm = _permute_time(dypre)
    du_perm, gs, d_a, dd = ssm_bwd(dy_perm, s_perm, xs, cbd16.T, wb16.T, ar, ai, alr, ali, dvec)
    ds_in = _unpermute_time(du_perm)
    hg = N_GROUPS // 2
    d_in, d_out = [], []
    for part in range(2):
        ins, outs = [], []
        for half in range(2):
            states = (part * STATE_W + half * _HALF_ST, _HALF_ST)
            chans = (half * _HALF_CH, _HALF_CH)
            blk = mm_tn(s_perm, gs, f"dw_ssm_in_{part}{half}", a_cols=chans, b_cols=states)
            ins.append(jnp.einsum("ghgp->ghp", blk.reshape(hg, GROUP_CH, hg, N_STATE)))
            blk = mm_tn(xs, dy_perm, f"dw_ssm_out_{part}{half}", a_cols=states, b_cols=chans)
            outs.append(jnp.einsum("gpgh->gph", blk.reshape(hg, N_STATE, hg, GROUP_CH)))
        d_in.append(jnp.concatenate(ins, axis=0).transpose(0, 2, 1))
        d_out.append(jnp.concatenate(outs, axis=0).transpose(0, 2, 1))
    d_abar_r = jnp.sum(d_a[:, :STATE_W], axis=0).reshape(N_GROUPS, N_STATE)
    d_abar_i = jnp.sum(d_a[:, STATE_W:], axis=0).reshape(N_GROUPS, N_STATE)
    d_disc = disc_vjp((d_abar_r, d_abar_i, d_in[0], d_in[1]))
    for name, val in zip(("a_re", "a_im", "log_dt", "b_re", "b_im"), d_disc):
        grads[name] = val[None]
    grads["c_re"] = d_out[0][None]
    grads["c_im"] = -d_out[1][None]
    grads["d_skip"] = dd.reshape(1, N_GROUPS, GROUP_CH)

    dh1, dz, grads["g_mix"], dbf = mixin_bwd(dh2, h1, g_mix, w_in_r, dq_aug, dk_aug, dv_aug, ds_in, dc, fz)
    grads["b_f"] = dbf[:, :N_HEADS]
    d_w_in_r = mm_tn(u, dz, "dw_in")
    grads["w_in"] = jnp.concatenate([d_w_in_r[:, :ATTN_W] * QK_SCALE, d_w_in_r[:, ATTN_W:3 * ATTN_W],
                                     d_w_in_r[:, 3 * ATTN_W + SSM_W:3 * ATTN_W + SSM_W + N_HEADS],
                                     d_w_in_r[:, 3 * ATTN_W:3 * ATTN_W + SSM_W]], axis=1)

    dx, da1, da3, act, grads["g_ffn1"] = ffn_bwd(x2, g_ffn1, dh1, a1a, a3a, full["w1_a"], full["w3_a"], full["w2_a"], "ffn_a_bwd")
    grads["w1_a"] = mm_tn(da1, n1, "dw1_a")
    grads["w3_a"] = mm_tn(da3, n1, "dw3_a")
    grads["w2_a"] = mm_tn(act, dh1, "dw2_a", scale=0.5)
    return loss_part, dx, grads, late


def _reduce_and_update(weights, moms, vars_, loss_part, dx, grads, core, chip, late):
    pair_early = _pair_of(grads, GRAD_GROUPS[1], core)
    g_stored = {}
    for entries, (pair, got) in zip(GRAD_GROUPS, (late, (pair_early, chip_exchange(pair_early)))):
        half = chip_sum(pair, got, chip)
        g_stored.update(_unpack_shards(join_halves(half, sibling_swap(half), core), entries))
    g_out, d_out, m_out, v_out = {}, {}, {}, {}
    for n, _, _ in BIG:
        d, m, v = adamw(_stored(n, weights[n]), g_stored[n], _stored(n, moms[n]), _stored(n, vars_[n]), "adamw_" + n)
        g_out[n], d_out[n], m_out[n], v_out[n] = (_restored(n, a) for a in (g_stored[n], d, m, v))

    small = allreduce_small(_pack_small({n: grads[n] for n, _ in SMALL}, extra=loss_part[0, 0]))
    d_small, m_small, v_small = adamw(_pack_small(weights), small, _pack_small(moms), _pack_small(vars_), "adamw_small")

    g_small, loss = _unpack_small(small)
    g_out.update(g_small)
    outs = []
    for big, sm in ((d_out, d_small), (m_out, m_small), (v_out, v_small)):
        o, _ = _unpack_small(sm)
        o.update(big)
        outs.append(o)
    result = [loss, dx[None]] + [g_out[n] for n in WEIGHT_ORDER]
    for o in outs:
        result += [o[n] for n in WEIGHT_ORDER]
    return tuple(result)
```

```python
import functools
import math

import jax
import jax.numpy as jnp
from jax import lax
from jax.experimental import pallas as pl
from jax.experimental.pallas import tpu as pltpu

F32 = jnp.float32
BF16 = jnp.bfloat16

D_MODEL = 1024
D_FF = 2816
N_HEADS = 8
HEAD_DIM = 64
ATTN_W = 512
SSM_W = 512
N_GROUPS = 32
N_STATE = 64
GROUP_CH = 16
STATE_W = N_GROUPS * N_STATE
Z_COLS = 2176
QK_SCALE = 0.125
EPS = 1e-6

ADAM_LR = 0.001
ADAM_B1 = 0.9
ADAM_B2 = 0.999
ADAM_EPS = 1e-08
ADAM_WD = 0.01
ADAM_STEP = 10

TOKEN_TILE = 512
FFN_TOKEN_TILE = 256
FF_CHUNK = 1408
MM_K_TILE = 2048
ATTN_TILE = 512
SCAN_STEPS = 32
SCAN_LANES = 512
VMEM_LIMIT = 48 * 1024 * 1024
FFN_VMEM_LIMIT = 56 * 1024 * 1024
COPY_CHUNKS = 4

NT_DIMS = (((1,), (1,)), ((), ()))
TN_DIMS = (((0,), (0,)), ((), ()))
HIGHEST = lax.Precision.HIGHEST
MESH = pl.DeviceIdType.MESH

BIG = (
    ("w1_a", (1024, 704), 1), ("w3_a", (1024, 704), 1), ("w2_a", (704, 1024), 0),
    ("w_in", (1024, 514), 1), ("w_glu", (128, 512), 0), ("w_out", (256, 1024), 0),
    ("w1_b", (1024, 704), 1), ("w3_b", (1024, 704), 1), ("w2_b", (704, 1024), 0),
    ("w_ple_gate", (256, 1024), 0), ("w_ple_proj", (256, 256), 1),
)
PACK_COLS = 1024
PACK_ALIGN = 16
SMALL = (
    ("g_ffn1", (1, 1024)), ("g_mix", (1, 1024)), ("b_f", (1, 8)), ("a_re", (1, 32, 64)), ("a_im", (1, 32, 64)),
    ("log_dt", (1, 32)), ("b_re", (1, 32, 64, 16)), ("b_im", (1, 32, 64, 16)), ("c_re", (1, 32, 16, 64)),
    ("c_im", (1, 32, 16, 64)), ("d_skip", (1, 32, 16)), ("b_glu", (1, 512)), ("g_attn_out", (1, 512)),
    ("g_ssm_out", (1, 512)), ("g_ffn2", (1, 1024)), ("g_ple", (1, 1024)), ("g_final", (1024,)),
)
SMALL_ROWS = 1152
WEIGHT_ORDER = ("g_ffn1", "w1_a", "w3_a", "w2_a", "g_mix", "w_in", "b_f", "a_re", "a_im", "log_dt", "b_re", "b_im",
                "c_re", "c_im", "d_skip", "w_glu", "b_glu", "g_attn_out", "g_ssm_out", "w_out", "g_ffn2", "w1_b",
                "w3_b", "w2_b", "g_ple", "w_ple_gate", "w_ple_proj", "g_final")


def _params(sem=None, vmem=VMEM_LIMIT):
    kw = dict(vmem_limit_bytes=vmem)
    if sem is not None:
        kw["dimension_semantics"] = sem
    return pltpu.CompilerParams(**kw)


def _sds(shape, dtype):
    return jax.ShapeDtypeStruct(shape, dtype)


def _tile(n, pref):
    t = min(n, pref)
    assert n % t == 0, (n, pref)
    return t


def _rms_scale(x):
    return lax.rsqrt(jnp.mean(x * x, axis=-1, keepdims=True) + EPS)


def _rms_bwd(dy, x, g):
    r = _rms_scale(x)
    xh = x * r
    dxh = dy * g
    dx = r * (dxh - xh * jnp.mean(dxh * xh, axis=-1, keepdims=True))
    return dx, jnp.sum(dy * xh, axis=0, keepdims=True)


def _dot(a, b):
    return jnp.dot(a, b, preferred_element_type=F32)


def _dot_nt(a, b):
    return lax.dot_general(a, b, NT_DIMS, preferred_element_type=F32)


def _dot_tn(a, b):
    return lax.dot_general(a, b, TN_DIMS, preferred_element_type=F32)


_GELU_C = math.sqrt(2.0 / math.pi)


def _gelu_parts(x):
    t = jnp.tanh(_GELU_C * (x + 0.044715 * x * x * x))
    return 0.5 * x * (1.0 + t), t


def _gelu_grad(x, t):
    return 0.5 * (1.0 + t) + 0.5 * x * (1.0 - t * t) * _GELU_C * (1.0 + 3.0 * 0.044715 * x * x)


def _resident(shape):
    return pl.BlockSpec(shape, lambda i: (0,) * len(shape), pipeline_mode=pl.Buffered(1))


def ffn_fwd(h, g, w1, w3, w2, name):
    T = h.shape[0]
    tm = _tile(T, FFN_TOKEN_TILE)

    def body(h_ref, g_ref, w1_ref, w3_ref, w2_ref, ho_ref, a1_ref, a3_ref, n_ref):
        x = h_ref[...]
        n = (x * _rms_scale(x) * g_ref[...]).astype(BF16)
        n_ref[...] = n
        out = x
        for lo in range(0, D_FF, FF_CHUNK):
            cols = slice(lo, lo + FF_CHUNK)
            a1 = _dot_nt(n, w1_ref[cols, :])
            a3 = _dot_nt(n, w3_ref[cols, :])
            a1_ref[:, cols] = a1.astype(BF16)
            a3_ref[:, cols] = a3.astype(BF16)
            act = (a1 * jax.nn.sigmoid(a1) * a3).astype(BF16)
            out = out + 0.5 * _dot(act, w2_ref[cols, :])
        ho_ref[...] = out

    tok = lambda i: (i, 0)
    return pl.pallas_call(
        body, name=name, grid=(T // tm,),
        in_specs=[pl.BlockSpec((tm, D_MODEL), tok), _resident((1, D_MODEL)), _resident((D_FF, D_MODEL)),
                  _resident((D_FF, D_MODEL)), _resident((D_FF, D_MODEL))],
        out_specs=[pl.BlockSpec((tm, D_MODEL), tok), pl.BlockSpec((tm, D_FF), tok), pl.BlockSpec((tm, D_FF), tok),
                   pl.BlockSpec((tm, D_MODEL), tok)],
        out_shape=[_sds((T, D_MODEL), F32), _sds((T, D_FF), BF16), _sds((T, D_FF), BF16), _sds((T, D_MODEL), BF16)],
        compiler_params=_params(("arbitrary",), FFN_VMEM_LIMIT),
    )(h, g, w1, w3, w2)


def ffn_bwd(h, g, dho, a1, a3, w1, w3, w2, name):
    T = h.shape[0]
    tm = _tile(T, FFN_TOKEN_TILE)

    def body(h_ref, g_ref, dho_ref, a1_ref, a3_ref, w1_ref, w3_ref, w2_ref, dhi_ref, da1_ref, da3_ref, act_ref, dg_ref):
        @pl.when(pl.program_id(0) == 0)
        def _():
            dg_ref[...] = jnp.zeros_like(dg_ref)

        dho = dho_ref[...]
        dhb = (0.5 * dho).astype(BF16)
        dn = None
        for lo in range(0, D_FF, FF_CHUNK):
            cols = slice(lo, lo + FF_CHUNK)
            a1v = a1_ref[:, cols].astype(F32)
            a3v = a3_ref[:, cols].astype(F32)
            s = jax.nn.sigmoid(a1v)
            sl = a1v * s
            dact = _dot_nt(dhb, w2_ref[cols, :])
            act_ref[:, cols] = (sl * a3v).astype(BF16)
            da1 = (dact * a3v * s * (1.0 + a1v * (1.0 - s))).astype(BF16)
            da3 = (dact * sl).astype(BF16)
            da1_ref[:, cols] = da1
            da3_ref[:, cols] = da3
            part = _dot(da1, w1_ref[cols, :]) + _dot(da3, w3_ref[cols, :])
            dn = part if dn is None else dn + part
        dx, dg = _rms_bwd(dn, h_ref[...], g_ref[...])
        dg_ref[...] += dg
        dhi_ref[...] = dho + dx

    tok = lambda i: (i, 0)
    return pl.pallas_call(
        body, name=name, grid=(T // tm,),
        in_specs=[pl.BlockSpec((tm, D_MODEL), tok), _resident((1, D_MODEL)), pl.BlockSpec((tm, D_MODEL), tok),
                  pl.BlockSpec((tm, D_FF), tok), pl.BlockSpec((tm, D_FF), tok), _resident((D_FF, D_MODEL)),
                  _resident((D_FF, D_MODEL)), _resident((D_FF, D_MODEL))],
        out_specs=[pl.BlockSpec((tm, D_MODEL), tok), pl.BlockSpec((tm, D_FF), tok), pl.BlockSpec((tm, D_FF), tok),
                   pl.BlockSpec((tm, D_FF), tok), pl.BlockSpec((1, D_MODEL), lambda i: (0, 0))],
        out_shape=[_sds((T, D_MODEL), F32), _sds((T, D_FF), BF16), _sds((T, D_FF), BF16), _sds((T, D_FF), BF16),
                   _sds((1, D_MODEL), F32)],
        compiler_params=_params(("arbitrary",), FFN_VMEM_LIMIT),
    )(h, g, dho, a1, a3, w1, w3, w2)


def mm_tn(a, b, name, scale=1.0, a_cols=None, b_cols=None):
    T = a.shape[0]
    a_off, M = a_cols or (0, a.shape[1])
    b_off, N = b_cols or (0, b.shape[1])
    bm = 512 if M % 512 == 0 else (1408 if M == 2816 else 256)
    bn = N if N in (2176, 1408) else (1408 if N == 2816 else min(N, 1024))
    tk = _tile(T, MM_K_TILE)
    row_bytes = 2 * (bm * a.dtype.itemsize + bn * b.dtype.itemsize)
    while tk > TOKEN_TILE and tk * row_bytes > VMEM_LIMIT // 3:
        tk //= 2
    assert M % bm == 0 and N % bn == 0 and T % tk == 0 and a_off % bm == 0 and b_off % bn == 0
    n_k = T // tk
    m0, n0 = a_off // bm, b_off // bn

    def body(a_ref, b_ref, o_ref):
        k = pl.program_id(2)

        @pl.when(k == 0)
        def _():
            o_ref[...] = jnp.zeros_like(o_ref)

        o_ref[...] += _dot_tn(a_ref[...].astype(BF16), b_ref[...].astype(BF16))

        if scale != 1.0:
            @pl.when(k == n_k - 1)
            def _():
                o_ref[...] = o_ref[...] * scale

    return pl.pallas_call(
        body, name=name, grid=(M // bm, N // bn, n_k),
        in_specs=[pl.BlockSpec((tk, bm), lambda m, n, k: (k, m0 + m)), pl.BlockSpec((tk, bn), lambda m, n, k: (k, n0 + n))],
        out_specs=pl.BlockSpec((bm, bn), lambda m, n, k: (m, n)),
        out_shape=_sds((M, N), F32),
        compiler_params=_params(("arbitrary", "arbitrary", "arbitrary")),
    )(a, b)


def mixin_fwd(h1, g, w_in_r, b_f_pad):
    T = h1.shape[0]
    tm = _tile(T, TOKEN_TILE)

    def body(h_ref, g_ref, w_ref, bf_ref, u_ref, qkv_ref, s_ref, fz_ref, c_ref, carry):
        @pl.when(pl.program_id(0) == 0)
        def _():
            carry[...] = jnp.zeros_like(carry)

        x = h_ref[...]
        u = (x * _rms_scale(x) * g_ref[...]).astype(BF16)
        u_ref[...] = u
        z = _dot(u, w_ref[...])
        qkv_ref[...] = z[:, :3 * ATTN_W].astype(BF16)
        s_ref[...] = z[:, 3 * ATTN_W:3 * ATTN_W + SSM_W]
        fz = z[:, 3 * ATTN_W + SSM_W:] + bf_ref[...]
        fz_ref[...] = fz
        lane = lax.broadcasted_iota(jnp.int32, fz.shape, 1)
        logf = jnp.where(lane < N_HEADS, jnp.minimum(fz, 0.0) - jnp.log(1.0 + jnp.exp(-jnp.abs(fz))), 0.0)
        row = lax.broadcasted_iota(jnp.int32, (tm, tm), 0)
        col = lax.broadcasted_iota(jnp.int32, (tm, tm), 1)
        tri = (col <= row).astype(F32)
        cs = jnp.dot(tri, logf, precision=HIGHEST, preferred_element_type=F32) + carry[0:1, :]
        c_ref[...] = cs
        carry[...] = jnp.broadcast_to(cs[tm - 1:tm, :], carry.shape)

    tok = lambda i: (i, 0)
    fix = lambda i: (0, 0)
    return pl.pallas_call(
        body, name="mixin_fwd", grid=(T // tm,),
        in_specs=[pl.BlockSpec((tm, D_MODEL), tok), pl.BlockSpec((1, D_MODEL), fix),
                  pl.BlockSpec((D_MODEL, Z_COLS), fix), pl.BlockSpec((1, 128), fix)],
        out_specs=[pl.BlockSpec((tm, D_MODEL), tok), pl.BlockSpec((tm, 3 * ATTN_W), tok), pl.BlockSpec((tm, SSM_W), tok),
                   pl.BlockSpec((tm, 128), tok), pl.BlockSpec((tm, 128), tok)],
        out_shape=[_sds((T, D_MODEL), BF16), _sds((T, 3 * ATTN_W), BF16), _sds((T, SSM_W), F32),
                   _sds((T, 128), F32), _sds((T, 128), F32)],
        scratch_shapes=[pltpu.VMEM((8, 128), F32)],
        compiler_params=_params(("arbitrary",)),
    )(h1, g, w_in_r, b_f_pad)


def mixin_bwd(dh2, h1, g, w_in_r, dq, dk, dv, ds, dc, fz):
    T = h1.shape[0]
    tm = _tile(T, TOKEN_TILE)
    n_t = T // tm

    def body(dh2_ref, h_ref, g_ref, w_ref, dq_ref, dk_ref, dv_ref, ds_ref, dc_ref, fz_ref,
             dh1_ref, dz_ref, dg_ref, dbf_ref, carry):
        @pl.when(pl.program_id(0) == 0)
        def _():
            carry[...] = jnp.zeros_like(carry)
            dg_ref[...] = jnp.zeros_like(dg_ref)
            dbf_ref[...] = jnp.zeros_like(dbf_ref)

        row = lax.broadcasted_iota(jnp.int32, (tm, tm), 0)
        col = lax.broadcasted_iota(jnp.int32, (tm, tm), 1)
        tri = (col >= row).astype(F32)
        dlogf = jnp.dot(tri, dc_ref[...], precision=HIGHEST, preferred_element_type=F32) + carry[0:1, :]
        carry[...] = jnp.broadcast_to(dlogf[0:1, :], carry.shape)
        dfz = dlogf * jax.nn.sigmoid(-fz_ref[...])
        dbf_ref[...] += jnp.sum(dfz, axis=0, keepdims=True)
        dz = jnp.concatenate([_join_heads(dq_ref, BF16), _join_heads(dk_ref, BF16), _join_heads(dv_ref, BF16),
                              ds_ref[...], dfz], axis=1).astype(BF16)
        dz_ref[...] = dz
        du = _dot_nt(dz, w_ref[...])
        dx, dg = _rms_bwd(du, h_ref[...], g_ref[...])
        dg_ref[...] += dg
        dh1_ref[...] = dh2_ref[...] + dx

    tok = lambda i: (n_t - 1 - i, 0)
    fix = lambda i: (0, 0)
    heads = pl.BlockSpec((N_HEADS, tm, 128), lambda i: (0, n_t - 1 - i, 0))
    return pl.pallas_call(
        body, name="mixin_bwd", grid=(n_t,),
        in_specs=[pl.BlockSpec((tm, D_MODEL), tok), pl.BlockSpec((tm, D_MODEL), tok), pl.BlockSpec((1, D_MODEL), fix),
                  pl.BlockSpec((D_MODEL, Z_COLS), fix), heads, heads, heads, pl.BlockSpec((tm, SSM_W), tok),
                  pl.BlockSpec((tm, 128), tok), pl.BlockSpec((tm, 128), tok)],
        out_specs=[pl.BlockSpec((tm, D_MODEL), tok), pl.BlockSpec((tm, Z_COLS), tok), pl.BlockSpec((1, D_MODEL), fix),
                   pl.BlockSpec((1, 128), fix)],
        out_shape=[_sds((T, D_MODEL), F32), _sds((T, Z_COLS), BF16), _sds((1, D_MODEL), F32), _sds((1, 128), F32)],
        scratch_shapes=[pltpu.VMEM((8, 128), F32)],
        compiler_params=_params(("arbitrary",)),
    )(dh2, h1, g, w_in_r, dq, dk, dv, ds, dc, fz)


def _lane_move(src_lo, dst_lo, width, dtype):
    r = lax.broadcasted_iota(jnp.int32, (128, 128), 0)
    c = lax.broadcasted_iota(jnp.int32, (128, 128), 1)
    return ((c - dst_lo == r - src_lo) & (r >= src_lo) & (r < src_lo + width)).astype(dtype)


def _lane_const(lo, width, value):
    lane = lax.broadcasted_iota(jnp.int32, (1, 128), 1)
    return jnp.where((lane >= lo) & (lane < lo + width), value, 0.0).astype(F32)


def _pieces(a):
    hi = a.astype(BF16)
    rest = a - hi.astype(F32)
    mid = rest.astype(BF16)
    return hi, mid, (rest - mid.astype(F32)).astype(BF16)


def _head_features(pair_block, e):
    return _dot(pair_block, _lane_move(HEAD_DIM * e, 0, HEAD_DIM, BF16))


def _helper_columns(pieces, head, sign):
    out = None
    for k, piece in enumerate(pieces):
        term = _dot(piece, _lane_move(head, HEAD_DIM + k, 1, BF16))
        out = term if out is None else out + term
    return sign * out


def heads_in(qkv, cum):
    T = qkv.shape[0]
    tm = _tile(T, TOKEN_TILE)

    def body(qkv_ref, c_ref, q_ref, k_ref, v_ref):
        c = _pieces(c_ref[...])
        for h in range(N_HEADS):
            p, e = divmod(h, 2)
            blk = lambda base: qkv_ref[:, base + 128 * p:base + 128 * (p + 1)]
            q_ref[h] = (_head_features(blk(0), e) + _lane_const(HEAD_DIM, 3, -1.0)).astype(BF16)
            k_ref[h] = (_head_features(blk(ATTN_W), e) + _helper_columns(c, h, 1.0)
                        + _lane_const(HEAD_DIM + 3, 3, 1.0)).astype(BF16)
            v_ref[h] = (_head_features(blk(2 * ATTN_W), e) + _lane_const(HEAD_DIM, 3, 1.0)).astype(BF16)

    tok = lambda i: (i, 0)
    heads = pl.BlockSpec((N_HEADS, tm, 128), lambda i: (0, i, 0))
    return pl.pallas_call(
        body, name="heads_in", grid=(T // tm,),
        in_specs=[pl.BlockSpec((tm, 3 * ATTN_W), tok), pl.BlockSpec((tm, 128), tok)],
        out_specs=[heads] * 3, out_shape=[_sds((N_HEADS, T, 128), BF16)] * 3,
        compiler_params=_params(("arbitrary",)),
    )(qkv, cum)


def attn_fwd(q_aug, k_aug, v_aug, send=None):
    H, T, wd = q_aug.shape
    hd = HEAD_DIM
    tq = _tile(T, ATTN_TILE)
    n = T // tq

    def body(q_ref, k_ref, v_ref, o_ref, qb_ref, m_sc, acc, s_even, s_odd):
        qi = pl.program_id(1)
        qv = q_ref[0]
        m_sc[...] = jnp.full_like(m_sc, -jnp.inf)
        acc[...] = jnp.zeros_like(acc)

        def key_rows(j):
            return pl.ds(pl.multiple_of(jnp.minimum(j, qi) * tq, tq), tq)

        def logits(j, buf):
            buf[...] = _dot_nt(k_ref[0, key_rows(j), :], qv)

        def update(j, buf, masked):
            st = buf[...]
            if masked:
                keep = lax.broadcasted_iota(jnp.int32, (tq, tq), 0) <= lax.broadcasted_iota(jnp.int32, (tq, tq), 1)
                st = jnp.where(keep, st, -1e30)
            m_old = m_sc[...]
            m_new = jnp.maximum(m_old, jnp.max(st, axis=0, keepdims=True))
            pt = jnp.exp(st - m_new).astype(BF16)
            acc[...] = jnp.exp(m_old - m_new) * acc[...] + _dot_tn(v_ref[0, key_rows(j), :], pt)
            m_sc[...] = m_new

        logits(0, s_even)

        def two_tiles(p, carry):
            j = 2 * p
            logits(j + 1, s_odd)
            update(j, s_even, False)
            logits(j + 2, s_even)
            update(j + 1, s_odd, False)
            return carry

        lax.fori_loop(0, qi // 2, two_tiles, 0)

        @pl.when(qi % 2 == 0)
        def _():
            update(qi, s_even, True)

        @pl.when(qi % 2 == 1)
        def _():
            logits(qi, s_odd)
            update(qi - 1, s_even, False)
            update(qi, s_odd, True)

        total = acc[hd:hd + 1, :]
        o_ref[0] = (acc[...] / total).T
        hi, mid, lo = (t.astype(F32) for t in _pieces(-(m_sc[...] + jnp.log(total))))
        row = lax.broadcasted_iota(jnp.int32, (wd, tq), 0)
        lse_rows = jnp.where(row == hd + 3, hi, jnp.where(row == hd + 4, mid, jnp.where(row == hd + 5, lo, 0.0)))
        qb_ref[0] = (qv.astype(F32) + lse_rows.T).astype(BF16)

    qmap = lambda h, i: (h, i, 0)
    head = lambda h, i: (h, 0, 0)
    in_specs = [pl.BlockSpec((1, tq, wd), qmap), pl.BlockSpec((1, T, wd), head), pl.BlockSpec((1, T, wd), head)]
    out_specs = [pl.BlockSpec((1, tq, wd), qmap), pl.BlockSpec((1, tq, wd), qmap)]
    out_shape = [_sds((H, T, wd), F32), _sds((H, T, wd), BF16)]
    scratch = [pltpu.VMEM((1, tq), F32), pltpu.VMEM((wd, tq), F32), pltpu.VMEM((tq, tq), F32), pltpu.VMEM((tq, tq), F32)]
    operands = (q_aug, k_aug, v_aug)
    if send is None:
        kernel_body = body
    else:
        def kernel_body(q_ref, k_ref, v_ref, send_ref, o_ref, qb_ref, got_ref, m_sc, acc, s_even, s_odd, *sems):
            h, i = pl.program_id(0), pl.program_id(1)

            @pl.when((h == 0) & (i == 0))
            def _():
                _gather_start(send_ref, got_ref, *sems)

            body(q_ref, k_ref, v_ref, o_ref, qb_ref, m_sc, acc, s_even, s_odd)

            @pl.when((h == H - 1) & (i == n - 1))
            def _():
                _gather_finish(send_ref, got_ref, *sems)

        in_specs, out_specs = in_specs + [_HBM], out_specs + [_HBM]
        out_shape = out_shape + [_sds((4,) + send.shape, send.dtype)]
        scratch, operands = scratch + _gather_scratch(send.shape[0]), operands + (send,)
    return pl.pallas_call(
        kernel_body, name="attn_fwd", grid=(H, n), in_specs=in_specs, out_specs=out_specs, out_shape=out_shape,
        scratch_shapes=scratch, compiler_params=_params(("arbitrary", "arbitrary")),
    )(*operands)


def attn_bwd(q_aug, k_aug, v_aug, do_aug, pair=None):
    H, T, wd = q_aug.shape
    tq = _tile(T, ATTN_TILE)
    n = T // tq

    def compute(q_ref, do_ref, k_ref, v_ref, dq_ref, dk_ref, dv_ref, dc_ref, dck, s_a, d_a, s_b, d_b):
        j = pl.program_id(1)

        @pl.when(j == 0)
        def _():
            dq_ref[...] = jnp.zeros_like(dq_ref)
            dc_ref[...] = jnp.zeros_like(dc_ref)

        dk_ref[...] = jnp.zeros_like(dk_ref)
        dv_ref[...] = jnp.zeros_like(dv_ref)
        dck[...] = jnp.zeros_like(dck)
        kv, vv = k_ref[0], v_ref[0]

        def query_rows(i):
            return pl.ds(pl.multiple_of(jnp.minimum(i, n - 1) * tq, tq), tq)

        def products(i, s_buf, d_buf):
            rows = query_rows(i)
            s_buf[...] = _dot_nt(kv, q_ref[0, rows, :])
            d_buf[...] = _dot_nt(vv, do_ref[0, rows, :])

        def update(i, s_buf, d_buf, masked):
            rows = query_rows(i)
            qv, dov = q_ref[0, rows, :], do_ref[0, rows, :]
            pt = jnp.exp(s_buf[...])
            if masked:
                keep = lax.broadcasted_iota(jnp.int32, (tq, tq), 0) <= lax.broadcasted_iota(jnp.int32, (tq, tq), 1)
                pt = jnp.where(keep, pt, 0.0)
            dv_ref[0] += _dot(pt.astype(BF16), dov)
            dst = pt * d_buf[...]
            dsb = dst.astype(BF16)
            dk_ref[0] += _dot(dsb, qv)
            dq_ref[0, rows, :] += _dot_tn(dsb, kv)
            dck[...] += jnp.sum(dst, axis=1, keepdims=True)
            dc_ref[0, pl.ds(i, 1), :] += jnp.sum(dst, axis=0, keepdims=True)

        products(j, s_a, d_a)
        products(j + 1, s_b, d_b)
        update(j, s_a, d_a, True)
        left = n - 1 - j

        def two_tiles(p, carry):
            i = j + 1 + 2 * p
            products(i + 1, s_a, d_a)
            update(i, s_b, d_b, False)
            products(i + 2, s_b, d_b)
            update(i + 1, s_a, d_a, False)
            return carry

        lax.fori_loop(0, left // 2, two_tiles, 0)

        @pl.when(left % 2 == 1)
        def _():
            update(n - 1, s_b, d_b, False)

        dc_ref[0, pl.ds(j, 1), :] -= jnp.broadcast_to(dck[...], (tq, 128)).T[0:1, :]

    head = lambda h, j: (h, 0, 0)
    kmap = lambda h, j: (h, j, 0)
    in_specs = [pl.BlockSpec((1, T, wd), head), pl.BlockSpec((1, T, wd), head), pl.BlockSpec((1, tq, wd), kmap),
                pl.BlockSpec((1, tq, wd), kmap)]
    out_specs = [pl.BlockSpec((1, T, wd), head), pl.BlockSpec((1, tq, wd), kmap), pl.BlockSpec((1, tq, wd), kmap),
                 pl.BlockSpec((1, n, tq), head)]
    out_shape = [_sds((H, T, wd), F32), _sds((H, T, wd), F32), _sds((H, T, wd), F32), _sds((H, n, tq), F32)]
    scratch = [pltpu.VMEM((tq, 1), F32)] + [pltpu.VMEM((tq, tq), F32)] * 4
    operands = (q_aug, do_aug, k_aug, v_aug)
    if pair is None:
        body = compute
    else:
        def body(q_ref, do_ref, k_ref, v_ref, pair_ref, dq_ref, dk_ref, dv_ref, dc_ref, got_ref,
                 dck, s_a, d_a, s_b, d_b, send_sems, recv_sems):
            h, j = pl.program_id(0), pl.program_id(1)

            @pl.when((h == 0) & (j == 0))
            def _():
                for cp in _chip_copies(pair_ref, got_ref, send_sems, recv_sems):
                    cp.start()

            compute(q_ref, do_ref, k_ref, v_ref, dq_ref, dk_ref, dv_ref, dc_ref, dck, s_a, d_a, s_b, d_b)

            @pl.when((h == H - 1) & (j == n - 1))
            def _():
                for cp in _chip_copies(pair_ref, got_ref, send_sems, recv_sems):
                    cp.wait()

        sems = pltpu.SemaphoreType.DMA((3 * len(_spans(pair.shape[1], COPY_CHUNKS)),))
        in_specs, out_specs = in_specs + [_HBM], out_specs + [_HBM]
        out_shape = out_shape + [_sds((3,) + pair.shape[1:], pair.dtype)]
        scratch, operands = scratch + [sems, sems], operands + (pair,)
    return pl.pallas_call(
        body, name="attn_bwd", grid=(H, n), in_specs=in_specs, out_specs=out_specs, out_shape=out_shape,
        scratch_shapes=scratch, compiler_params=_params(("arbitrary", "arbitrary")),
    )(*operands)


def _complex_step(a_r, a_i, cr, ci, br, bi):
    return a_r * cr - a_i * ci + br, a_r * ci + a_i * cr + bi


_HALF_CH = SSM_W // 2
_HALF_ST = STATE_W // 2


def _state_cols(part, half):
    lo = part * STATE_W + half * _HALF_ST
    return slice(lo, lo + _HALF_ST)


def _channels_to_states(x, w_ref, out_ref):
    for half in range(2):
        ch = slice(half * _HALF_CH, (half + 1) * _HALF_CH)
        for part in range(2):
            cols = _state_cols(part, half)
            out_ref[:, cols] = _dot(x[:, ch], w_ref[ch, cols])


def _states_to_channels(x, w_ref):
    halves = []
    for half in range(2):
        ch = slice(half * _HALF_CH, (half + 1) * _HALF_CH)
        halves.append(_dot(x[:, _state_cols(0, half)], w_ref[_state_cols(0, half), ch])
                      + _dot(x[:, _state_cols(1, half)], w_ref[_state_cols(1, half), ch]))
    return jnp.concatenate(halves, axis=1)


def ssm_fwd(s_perm, wb, cbd, a_r, a_i, al_r, al_i, dvec):
    T = s_perm.shape[0]
    chunk = T // 8
    ts = _tile(chunk, SCAN_STEPS)
    tr, n_s = ts * 8, chunk // ts
    W, LB = STATE_W, SCAN_LANES

    def body(s_ref, wb_ref, cbd_ref, ar_ref, ai_ref, alr_ref, ali_ref, dv_ref, y_ref, xs_ref, bu, carry):
        ph, i = pl.program_id(0), pl.program_id(1)

        @pl.when((ph == 0) & (i == 0))
        def _():
            carry[...] = jnp.zeros_like(carry)

        _channels_to_states(s_ref[...].astype(BF16), wb_ref, bu)

        def scan(store):
            for lb in range(W // LB):
                lo = lb * LB
                re, im = slice(lo, lo + LB), slice(W + lo, W + lo + LB)
                ar = jnp.broadcast_to(ar_ref[:, re], (8, LB))
                ai = jnp.broadcast_to(ai_ref[:, re], (8, LB))

                def step(s, c):
                    rows = pl.ds(pl.multiple_of(s * 8, 8), 8)
                    nr, ni = _complex_step(ar, ai, c[0], c[1], bu[rows, re], bu[rows, im])
                    if store:
                        bu[rows, re] = nr
                        bu[rows, im] = ni
                    return nr, ni

                cr, ci = lax.fori_loop(0, ts, step, (carry[:, re], carry[:, im]), unroll=2)
                carry[:, re] = cr
                carry[:, im] = ci

        @pl.when(ph == 0)
        def _():
            scan(False)

            @pl.when(i == n_s - 1)
            def _():
                er, ei = carry[:, :W], carry[:, W:]
                alr = jnp.broadcast_to(alr_ref[...], (8, W))
                ali = jnp.broadcast_to(ali_ref[...], (8, W))
                first = lax.broadcasted_iota(jnp.int32, (8, W), 0) == 0
                sr, si = jnp.zeros((8, W), F32), jnp.zeros((8, W), F32)
                for _ in range(7):
                    vr, vi = _complex_step(alr, ali, sr, si, er, ei)
                    sr = jnp.where(first, 0.0, pltpu.roll(vr, 1, 0))
                    si = jnp.where(first, 0.0, pltpu.roll(vi, 1, 0))
                carry[:, :W] = sr
                carry[:, W:] = si

        @pl.when(ph == 1)
        def _():
            scan(True)
            xb = bu[...].astype(BF16)
            xs_ref[...] = xb
            y_ref[...] = _states_to_channels(xb, cbd_ref) + s_ref[...] * dv_ref[...]

    fix = lambda p, i: (0, 0)
    return pl.pallas_call(
        body, name="ssm_fwd", grid=(2, n_s),
        in_specs=[pl.BlockSpec((tr, SSM_W), lambda p, i: (i, 0)), pl.BlockSpec((SSM_W, 2 * W), fix),
                  pl.BlockSpec((2 * W, SSM_W), fix), pl.BlockSpec((1, W), fix), pl.BlockSpec((1, W), fix),
                  pl.BlockSpec((1, W), fix), pl.BlockSpec((1, W), fix), pl.BlockSpec((1, SSM_W), fix)],
        out_specs=[pl.BlockSpec((tr, SSM_W), lambda p, i: (i * p, 0)), pl.BlockSpec((tr, 2 * W), lambda p, i: (i * p, 0))],
        out_shape=[_sds((T, SSM_W), F32), _sds((T, 2 * W), BF16)],
        scratch_shapes=[pltpu.VMEM((tr, 2 * W), F32), pltpu.VMEM((8, 2 * W), F32)],
        compiler_params=_params(("arbitrary", "arbitrary")),
    )(s_perm, wb, cbd, a_r, a_i, al_r, al_i, dvec)


def ssm_bwd(dy_perm, s_perm, xs, cbd_t, wb_t, a_r, a_i, al_r, al_i, dvec):
    T = s_perm.shape[0]
    chunk = T // 8
    ts = _tile(chunk, SCAN_STEPS)
    tr, n_s = ts * 8, chunk // ts
    W, LB = STATE_W, SCAN_LANES

    def body(dy_ref, s_ref, xs_ref, cbt_ref, wbt_ref, ar_ref, ai_ref, alr_ref, ali_ref, dv_ref,
             du_ref, gs_ref, da_ref, dd_ref, gd, x32, carry):
        ph, i = pl.program_id(0), pl.program_id(1)

        @pl.when((ph == 0) & (i == 0))
        def _():
            carry[...] = jnp.zeros_like(carry)
            da_ref[...] = jnp.zeros_like(da_ref)
            dd_ref[...] = jnp.zeros_like(dd_ref)

        _channels_to_states(dy_ref[...].astype(BF16), cbt_ref, gd)

        def scan(store):
            for lb in range(W // LB):
                lo = lb * LB
                re, im = slice(lo, lo + LB), slice(W + lo, W + lo + LB)
                ar = jnp.broadcast_to(ar_ref[:, re], (8, LB))
                nai = -jnp.broadcast_to(ai_ref[:, re], (8, LB))

                def step(k, c):
                    rows = pl.ds(pl.multiple_of((ts - 1 - k) * 8, 8), 8)
                    cr, ci = c[0], c[1]
                    nr, ni = _complex_step(ar, nai, cr, ci, gd[rows, re], gd[rows, im])
                    if store:
                        xr, xi = x32[rows, re], x32[rows, im]
                        gd[rows, re] = nr
                        gd[rows, im] = ni
                        return nr, ni, c[2] + cr * xr + ci * xi, c[3] + ci * xr - cr * xi
                    return nr, ni

                init = (carry[:, re], carry[:, im])
                if store:
                    init = init + (da_ref[:, re], da_ref[:, im])
                out = lax.fori_loop(0, ts, step, init, unroll=2)
                carry[:, re] = out[0]
                carry[:, im] = out[1]
                if store:
                    da_ref[:, re] = out[2]
                    da_ref[:, im] = out[3]

        @pl.when(ph == 0)
        def _():
            scan(False)

            @pl.when(i == n_s - 1)
            def _():
                er, ei = carry[:, :W], carry[:, W:]
                alr = jnp.broadcast_to(alr_ref[...], (8, W))
                nali = -jnp.broadcast_to(ali_ref[...], (8, W))
                last = lax.broadcasted_iota(jnp.int32, (8, W), 0) == 7
                rr, ri = jnp.zeros((8, W), F32), jnp.zeros((8, W), F32)
                for _ in range(7):
                    vr, vi = _complex_step(alr, nali, rr, ri, er, ei)
                    rr = jnp.where(last, 0.0, pltpu.roll(vr, 7, 0))
                    ri = jnp.where(last, 0.0, pltpu.roll(vi, 7, 0))
                carry[:, :W] = rr
                carry[:, W:] = ri

        @pl.when(ph == 1)
        def _():
            x32[...] = xs_ref[...].astype(F32)
            scan(True)
            gb = gd[...].astype(BF16)
            gs_ref[...] = gb
            dy = dy_ref[...]
            du_ref[...] = _states_to_channels(gb, wbt_ref) + dy * dv_ref[...]
            dd_ref[...] += jnp.sum(dy * s_ref[...], axis=0, keepdims=True)

    fix = lambda p, i: (0, 0)
    rev = lambda p, i: (n_s - 1 - i, 0)
    rev_out = lambda p, i: (n_s - 1 - i * p, 0)
    return pl.pallas_call(
        body, name="ssm_bwd", grid=(2, n_s),
        in_specs=[pl.BlockSpec((tr, SSM_W), rev), pl.BlockSpec((tr, SSM_W), rev), pl.BlockSpec((tr, 2 * W), rev),
                  pl.BlockSpec((SSM_W, 2 * W), fix), pl.BlockSpec((2 * W, SSM_W), fix), pl.BlockSpec((1, W), fix),
                  pl.BlockSpec((1, W), fix), pl.BlockSpec((1, W), fix), pl.BlockSpec((1, W), fix),
                  pl.BlockSpec((1, SSM_W), fix)],
        out_specs=[pl.BlockSpec((tr, SSM_W), rev_out), pl.BlockSpec((tr, 2 * W), rev_out),
                   pl.BlockSpec((8, 2 * W), fix), pl.BlockSpec((1, SSM_W), fix)],
        out_shape=[_sds((T, SSM_W), F32), _sds((T, 2 * W), BF16), _sds((8, 2 * W), F32), _sds((1, SSM_W), F32)],
        scratch_shapes=[pltpu.VMEM((tr, 2 * W), F32), pltpu.VMEM((tr, 2 * W), F32), pltpu.VMEM((8, 2 * W), F32)],
        compiler_params=_params(("arbitrary", "arbitrary")),
    )(dy_perm, s_perm, xs, cbd_t, wb_t, a_r, a_i, al_r, al_i, dvec)


def _join_heads(ref, dtype):
    def move(h, dst):
        x = ref[h]
        pieces = _pieces(x) if dtype == F32 else (x.astype(BF16),)
        out = None
        for piece in pieces:
            term = _dot(piece, _lane_move(0, dst, HEAD_DIM, BF16))
            out = term if out is None else out + term
        return out

    return jnp.concatenate([move(2 * p, 0) + move(2 * p + 1, HEAD_DIM) for p in range(N_HEADS // 2)], axis=1)


def mixout_fwd(h1, o_heads, ypre, g_a, g_s, w_glu, b_glu, w_out):
    T = h1.shape[0]
    tm = _tile(T, TOKEN_TILE)

    def body(h_ref, at_ref, yp_ref, ga_ref, gs_ref, wg_ref, bg_ref, wo_ref, h2_ref, mixed_ref):
        yg, _ = _gelu_parts(yp_ref[...])
        gl = yg * jax.nn.sigmoid(_dot(yg.astype(BF16), wg_ref[...]) + bg_ref[...])
        at = _join_heads(at_ref, F32)
        mixed = jnp.concatenate([at * _rms_scale(at) * ga_ref[...], gl * _rms_scale(gl) * gs_ref[...]], axis=1)
        mixed = mixed.astype(BF16)
        mixed_ref[...] = mixed
        h2_ref[...] = h_ref[...] + _dot(mixed, wo_ref[...])

    tok = lambda i: (i, 0)
    fix = lambda i: (0, 0)
    return pl.pallas_call(
        body, name="mixout_fwd", grid=(T // tm,),
        in_specs=[pl.BlockSpec((tm, D_MODEL), tok), pl.BlockSpec((N_HEADS, tm, 128), lambda i: (0, i, 0)),
                  pl.BlockSpec((tm, SSM_W), tok),
                  pl.BlockSpec((1, ATTN_W), fix), pl.BlockSpec((1, SSM_W), fix), pl.BlockSpec((SSM_W, SSM_W), fix),
                  pl.BlockSpec((1, SSM_W), fix), pl.BlockSpec((D_MODEL, D_MODEL), fix)],
        out_specs=[pl.BlockSpec((tm, D_MODEL), tok), pl.BlockSpec((tm, D_MODEL), tok)],
        out_shape=[_sds((T, D_MODEL), F32), _sds((T, D_MODEL), BF16)],
        compiler_params=_params(("arbitrary",)),
    )(h1, o_heads, ypre, g_a, g_s, w_glu, b_glu, w_out)


def mixout_bwd(dh2, o_heads, ypre, g_a, g_s, w_glu, b_glu, w_out, seg):
    T = dh2.shape[0]
    tm = _tile(T, TOKEN_TILE)

    def body(dh_ref, at_ref, yp_ref, ga_ref, gs_ref, wg_ref, bg_ref, wo_ref, seg_ref,
             do_ref, dyp_ref, dpre_ref, yg_ref, dga_ref, dgs_ref, dbg_ref):
        @pl.when(pl.program_id(0) == 0)
        def _():
            dga_ref[...] = jnp.zeros_like(dga_ref)
            dgs_ref[...] = jnp.zeros_like(dgs_ref)
            dbg_ref[...] = jnp.zeros_like(dbg_ref)

        dmix = _dot_nt(dh_ref[...].astype(BF16), wo_ref[...])
        at = _join_heads(at_ref, F32)
        dat, dga = _rms_bwd(dmix[:, :ATTN_W], at, ga_ref[...])
        dga_ref[...] += dga
        delta = _pieces(jnp.dot(dat * at, seg_ref[...], precision=HIGHEST, preferred_element_type=F32))
        datb = dat.astype(BF16)
        for h in range(N_HEADS):
            p, e = divmod(h, 2)
            do_ref[h] = (_head_features(datb[:, 128 * p:128 * (p + 1)], e) + _helper_columns(delta, h, -1.0)).astype(BF16)
        yp = yp_ref[...]
        yg, t = _gelu_parts(yp)
        ygb = yg.astype(BF16)
        yg_ref[...] = ygb
        sg = jax.nn.sigmoid(_dot(ygb, wg_ref[...]) + bg_ref[...])
        dgl, dgs = _rms_bwd(dmix[:, ATTN_W:], yg * sg, gs_ref[...])
        dgs_ref[...] += dgs
        dpre = dgl * yg * sg * (1.0 - sg)
        dbg_ref[...] += jnp.sum(dpre, axis=0, keepdims=True)
        dpb = dpre.astype(BF16)
        dpre_ref[...] = dpb
        dyg = dgl * sg + _dot_nt(dpb, wg_ref[...])
        dyp_ref[...] = dyg * _gelu_grad(yp, t)

    tok = lambda i: (i, 0)
    fix = lambda i: (0, 0)
    heads = pl.BlockSpec((N_HEADS, tm, 128), lambda i: (0, i, 0))
    return pl.pallas_call(
        body, name="mixout_bwd", grid=(T // tm,),
        in_specs=[pl.BlockSpec((tm, D_MODEL), tok), heads, pl.BlockSpec((tm, SSM_W), tok),
                  pl.BlockSpec((1, ATTN_W), fix), pl.BlockSpec((1, SSM_W), fix), pl.BlockSpec((SSM_W, SSM_W), fix),
                  pl.BlockSpec((1, SSM_W), fix), pl.BlockSpec((D_MODEL, D_MODEL), fix), pl.BlockSpec((ATTN_W, 128), fix)],
        out_specs=[heads, pl.BlockSpec((tm, SSM_W), tok), pl.BlockSpec((tm, SSM_W), tok),
                   pl.BlockSpec((tm, SSM_W), tok), pl.BlockSpec((1, ATTN_W), fix),
                   pl.BlockSpec((1, SSM_W), fix), pl.BlockSpec((1, SSM_W), fix)],
        out_shape=[_sds((N_HEADS, T, 128), BF16), _sds((T, SSM_W), F32), _sds((T, SSM_W), BF16), _sds((T, SSM_W), BF16),
                   _sds((1, ATTN_W), F32), _sds((1, SSM_W), F32), _sds((1, SSM_W), F32)],
        compiler_params=_params(("arbitrary",)),
    )(dh2, o_heads, ypre, g_a, g_s, w_glu, b_glu, w_out, seg)


def head_fwd_bwd(h3, p, target, g_ple, g_final, w_gate, w_proj):
    T = h3.shape[0]
    tm = _tile(T, TOKEN_TILE)
    pd = p.shape[1]

    def body(h_ref, p_ref, tg_ref, gp_ref, gf_ref, wg_ref, wp_ref,
             dh_ref, n3_ref, dz_ref, dpp_ref, loss_ref, dgp_ref, dgf_ref):
        @pl.when(pl.program_id(0) == 0)
        def _():
            loss_ref[...] = jnp.zeros_like(loss_ref)
            dgp_ref[...] = jnp.zeros_like(dgp_ref)
            dgf_ref[...] = jnp.zeros_like(dgf_ref)

        x = h_ref[...]
        gp, gf = gp_ref[...], gf_ref[...]
        n3 = (x * _rms_scale(x) * gp).astype(BF16)
        n3_ref[...] = n3
        gate = jax.nn.sigmoid(_dot(n3, wg_ref[...]))
        pp = _dot(p_ref[...].astype(BF16), wp_ref[...])
        h4 = x + gate * pp
        y = h4 * _rms_scale(h4) * gf
        e = y - tg_ref[...]
        tile_loss = jnp.sum(jnp.sum(e * e, axis=1, keepdims=True), axis=0, keepdims=True) * (0.5 / D_MODEL)
        loss_ref[...] += jnp.broadcast_to(tile_loss, loss_ref.shape)
        dh4, dgf = _rms_bwd(e * (1.0 / D_MODEL), h4, gf)
        dgf_ref[...] += dgf
        dzg = dh4 * pp * gate * (1.0 - gate)
        dzb = dzg.astype(BF16)
        dz_ref[...] = dzb
        dpp_ref[...] = (dh4 * gate).astype(BF16)
        dx, dgp = _rms_bwd(_dot_nt(dzb, wg_ref[...]), x, gp)
        dgp_ref[...] += dgp
        dh_ref[...] = dh4 + dx

    tok = lambda i: (i, 0)
    fix = lambda i: (0, 0)
    return pl.pallas_call(
        body, name="head_fwd_bwd", grid=(T // tm,),
        in_specs=[pl.BlockSpec((tm, D_MODEL), tok), pl.BlockSpec((tm, pd), tok), pl.BlockSpec((tm, D_MODEL), tok),
                  pl.BlockSpec((1, D_MODEL), fix), pl.BlockSpec((1, D_MODEL), fix), pl.BlockSpec((D_MODEL, D_MODEL), fix),
                  pl.BlockSpec((pd, D_MODEL), fix)],
        out_specs=[pl.BlockSpec((tm, D_MODEL), tok), pl.BlockSpec((tm, D_MODEL), tok), pl.BlockSpec((tm, D_MODEL), tok),
                   pl.BlockSpec((tm, D_MODEL), tok), pl.BlockSpec((8, 128), fix), pl.BlockSpec((1, D_MODEL), fix),
                   pl.BlockSpec((1, D_MODEL), fix)],
        out_shape=[_sds((T, D_MODEL), F32), _sds((T, D_MODEL), BF16), _sds((T, D_MODEL), BF16), _sds((T, D_MODEL), BF16),
                   _sds((8, 128), F32), _sds((1, D_MODEL), F32), _sds((1, D_MODEL), F32)],
        compiler_params=_params(("arbitrary",)),
    )(h3, p, target, g_ple, g_final, w_gate, w_proj)


def _row_tile(rows, cols, n_arrays):
    lanes = -(-cols // 128) * 128
    cap = VMEM_LIMIT // 3 // (2 * n_arrays * lanes * 4)
    best = None
    for t in range(PACK_ALIGN, min(rows, cap) + 1, PACK_ALIGN):
        if rows % t == 0:
            best = t
    assert best is not None, (rows, cols)
    return best


def _adamw_math(w, g, m, v):
    nm = ADAM_B1 * m + (1.0 - ADAM_B1) * g
    nv = ADAM_B2 * v + (1.0 - ADAM_B2) * (g * g)
    c1 = 1.0 - ADAM_B1 ** ADAM_STEP
    c2 = 1.0 - ADAM_B2 ** ADAM_STEP
    return -ADAM_LR * ((nm / c1) / (jnp.sqrt(nv / c2) + ADAM_EPS) + ADAM_WD * w), nm, nv


def adamw(w, g, m, v, name):
    R, C = w.shape
    tr = _row_tile(R, C, 7)

    def body(w_ref, g_ref, m_ref, v_ref, d_ref, nm_ref, nv_ref):
        d_ref[...], nm_ref[...], nv_ref[...] = _adamw_math(w_ref[...], g_ref[...], m_ref[...], v_ref[...])

    spec = pl.BlockSpec((tr, C), lambda i: (i, 0))
    return pl.pallas_call(
        body, name=name, grid=(R // tr,), in_specs=[spec] * 4, out_specs=[spec] * 3,
        out_shape=[_sds((R, C), F32)] * 3, compiler_params=_params(("arbitrary",)),
    )(w, g, m, v)


def join_halves(mine, other, core):
    rh, C = mine.shape
    tr = _row_tile(rh, C, 3)
    nb = rh // tr

    def body(c_ref, m_ref, o_ref, out_ref):
        out_ref[...] = jnp.where((pl.program_id(0) // nb) == c_ref[0], m_ref[...], o_ref[...])

    half = pl.BlockSpec((tr, C), lambda i, c: (i % nb, 0))
    return pl.pallas_call(
        body, name="join_halves",
        grid_spec=pltpu.PrefetchScalarGridSpec(num_scalar_prefetch=1, grid=(2 * nb,), in_specs=[half, half],
                                               out_specs=pl.BlockSpec((tr, C), lambda i, c: (i, 0))),
        out_shape=_sds((2 * rh, C), F32), compiler_params=_params(("arbitrary",)),
    )(core, mine, other)


def pair_sum(g, theirs, core):
    n, R, C = g.shape
    rh = R // 2
    tr = _row_tile(rh, C, 3)
    nb = rh // tr

    def body(c_ref, g_ref, t_ref, o_ref):
        o_ref[...] = (g_ref[...] + t_ref[...]).astype(BF16)

    here = pl.BlockSpec((1, tr, C), lambda j, i, c: (j, i, 0))
    return pl.pallas_call(
        body, name="pair_sum",
        grid_spec=pltpu.PrefetchScalarGridSpec(
            num_scalar_prefetch=1, grid=(n, nb),
            in_specs=[pl.BlockSpec((1, tr, C), lambda j, i, c: (j, c[0] * nb + i, 0)), here], out_specs=here),
        out_shape=_sds((n, rh, C), BF16), compiler_params=_params(("arbitrary", "arbitrary")),
    )(core, g, theirs)


def chip_sum(pair, got, chip):
    _, R, C = pair.shape
    tr = _row_tile(R, C, 5)

    def body(c_ref, p_ref, g0_ref, g1_ref, g2_ref, o_ref):
        f = lambda ref: ref[0].astype(F32)
        o_ref[...] = ((f(p_ref) + f(g0_ref)) + f(g1_ref)) + f(g2_ref)

    slot = lambda k: pl.BlockSpec((1, tr, C), lambda i, c: (k, i, 0))
    return pl.pallas_call(
        body, name="chip_sum",
        grid_spec=pltpu.PrefetchScalarGridSpec(
            num_scalar_prefetch=1, grid=(R // tr,),
            in_specs=[pl.BlockSpec((1, tr, C), lambda i, c: (c[0], i, 0)), slot(0), slot(1), slot(2)],
            out_specs=pl.BlockSpec((tr, C), lambda i, c: (i, 0))),
        out_shape=_sds((R, C), F32), compiler_params=_params(("arbitrary",)),
    )(chip, pair, got, got, got)


_HBM = pl.BlockSpec(memory_space=pltpu.HBM)


def _place():
    x, y, c = lax.axis_index("x"), lax.axis_index("y"), lax.axis_index("c")
    return x, y, c, [(1 - x, y), (x, 1 - y), (1 - x, 1 - y)]


def _spans(rows, n):
    assert rows % PACK_ALIGN == 0
    tiles = rows // PACK_ALIGN
    n = min(n, tiles)
    cuts = [tiles * q // n for q in range(n + 1)]
    return [(cuts[q] * PACK_ALIGN, (cuts[q + 1] - cuts[q]) * PACK_ALIGN) for q in range(n)]


def _remote(src, dst, send_sem, recv_sem, to):
    return pltpu.make_async_remote_copy(src_ref=src, dst_ref=dst, send_sem=send_sem, recv_sem=recv_sem,
                                        device_id=to, device_id_type=MESH)


GATHER_SEMS = 6


def _gather_scratch(rows):
    ici = pltpu.SemaphoreType.DMA((3 * len(_spans(rows // 2, COPY_CHUNKS)),))
    own = pltpu.SemaphoreType.DMA((len(_spans(rows, 2 * COPY_CHUNKS)),))
    return [ici, ici, own, own, ici, ici]


def _gather_start(w_ref, out_ref, ici_send, ici_recv, own_send, own_recv, pass_send, pass_recv):
    R = w_ref.shape[0]
    rh = R // 2
    spans = _spans(rh, COPY_CHUNKS)
    x, y, c, chips = _place()
    me = 2 * x + y
    for k, (cx, cy) in enumerate(chips):
        for q, (o, n) in enumerate(spans):
            rows = pl.ds(c * rh + o, n)
            _remote(w_ref.at[rows, :], out_ref.at[me, rows, :], ici_send.at[k * len(spans) + q],
                    ici_recv.at[k * len(spans) + q], (cx, cy, c)).start()
    for q, (o, n) in enumerate(_spans(R, 2 * COPY_CHUNKS)):
        rows = pl.ds(o, n)
        _remote(w_ref.at[rows, :], out_ref.at[me, rows, :], own_send.at[q], own_recv.at[q], (x, y, 1 - c)).start()


def _gather_finish(w_ref, out_ref, ici_send, ici_recv, own_send, own_recv, pass_send, pass_recv):
    R = w_ref.shape[0]
    rh = R // 2
    spans = _spans(rh, COPY_CHUNKS)
    n_sp = len(spans)
    x, y, c, chips = _place()
    me = 2 * x + y
    sibling = (x, y, 1 - c)
    passed = []
    for q, (o, n) in enumerate(spans):
        for k, (cx, cy) in enumerate(chips):
            blk = out_ref.at[2 * cx + cy, pl.ds(c * rh + o, n), :]
            _remote(blk, blk, ici_send.at[k * n_sp + q], ici_recv.at[k * n_sp + q], (cx, cy, c)).wait_recv()
            cp = _remote(blk, blk, pass_send.at[k * n_sp + q], pass_recv.at[k * n_sp + q], sibling)
            cp.start()
            passed.append(cp)
    for k, (cx, cy) in enumerate(chips):
        for q, (o, n) in enumerate(spans):
            blk = out_ref.at[2 * cx + cy, pl.ds((1 - c) * rh + o, n), :]
            _remote(blk, blk, pass_send.at[k * n_sp + q], pass_recv.at[k * n_sp + q], sibling).wait_recv()
            rows = pl.ds(c * rh + o, n)
            _remote(w_ref.at[rows, :], out_ref.at[me, rows, :], ici_send.at[k * n_sp + q], ici_recv.at[k * n_sp + q],
                    (cx, cy, c)).wait_send()
    for q, (o, n) in enumerate(_spans(R, 2 * COPY_CHUNKS)):
        rows = pl.ds(o, n)
        _remote(w_ref.at[rows, :], out_ref.at[me, rows, :], own_send.at[q], own_recv.at[q], sibling).wait()
    for cp in passed:
        cp.wait_send()


def gather_shards(wp):
    R, C = wp.shape

    def body(w_ref, out_ref, *sems):
        _gather_start(w_ref, out_ref, *sems)
        _gather_finish(w_ref, out_ref, *sems)

    return pl.pallas_call(
        body, name="gather_shards", in_specs=[_HBM], out_specs=_HBM, out_shape=_sds((4, R, C), wp.dtype),
        scratch_shapes=_gather_scratch(R),
    )(wp)


def sibling_split(g):
    n_sl, R, C = g.shape
    rh = R // 2
    spans = _spans(rh, COPY_CHUNKS)
    n_sp = len(spans)

    def body(g_ref, got_ref, send_sems, recv_sems):
        x, y, c, _ = _place()
        copies = []
        for j in range(n_sl):
            for q, (o, n) in enumerate(spans):
                cp = _remote(g_ref.at[j, pl.ds((1 - c) * rh + o, n), :], got_ref.at[j, pl.ds(o, n), :],
                             send_sems.at[j * n_sp + q], recv_sems.at[j * n_sp + q], (x, y, 1 - c))
                cp.start()
                copies.append(cp)
        for cp in copies:
            cp.wait()

    sems = pltpu.SemaphoreType.DMA((n_sl * n_sp,))
    return pl.pallas_call(
        body, name="sibling_split", in_specs=[_HBM], out_specs=_HBM, out_shape=_sds((n_sl, rh, C), g.dtype),
        scratch_shapes=[sems, sems],
    )(g)


def _chip_copies(p_ref, buf_ref, send_sems, recv_sems):
    rows = p_ref.shape[1]
    spans = _spans(rows, COPY_CHUNKS)
    x, y, c, chips = _place()
    copies = []
    for k, (cx, cy) in enumerate(chips):
        for q, (o, n) in enumerate(spans):
            copies.append(_remote(p_ref.at[2 * cx + cy, pl.ds(o, n), :], buf_ref.at[k, pl.ds(o, n), :],
                                  send_sems.at[k * len(spans) + q], recv_sems.at[k * len(spans) + q], (cx, cy, c)))
    return copies


def chip_exchange(p):
    _, R, C = p.shape

    def body(p_ref, buf_ref, send_sems, recv_sems):
        copies = _chip_copies(p_ref, buf_ref, send_sems, recv_sems)
        for cp in copies:
            cp.start()
        for cp in copies:
            cp.wait()

    sems = pltpu.SemaphoreType.DMA((3 * len(_spans(R, COPY_CHUNKS)),))
    return pl.pallas_call(
        body, name="chip_exchange", in_specs=[_HBM], out_specs=_HBM, out_shape=_sds((3, R, C), p.dtype),
        scratch_shapes=[sems, sems],
    )(p)


def sibling_swap(half):
    R, C = half.shape
    spans = _spans(R, COPY_CHUNKS)

    def body(h_ref, got_ref, send_sems, recv_sems):
        x, y, c, _ = _place()
        copies = []
        for q, (o, n) in enumerate(spans):
            cp = _remote(h_ref.at[pl.ds(o, n), :], got_ref.at[pl.ds(o, n), :], send_sems.at[q], recv_sems.at[q], (x, y, 1 - c))
            cp.start()
            copies.append(cp)
        for cp in copies:
            cp.wait()

    sems = pltpu.SemaphoreType.DMA((len(spans),))
    return pl.pallas_call(
        body, name="sibling_swap", in_specs=[_HBM], out_specs=_HBM, out_shape=_sds((R, C), half.dtype),
        scratch_shapes=[sems, sems],
    )(half)


def allreduce_small(v):
    R, C = v.shape

    def body(v_ref, out_ref, buf, send_sems, recv_sems):
        x, y, c, _ = _place()
        me = 4 * x + 2 * y + c
        buf[me] = v_ref[...]
        flips = [((k >> 2) & 1, (k >> 1) & 1, k & 1) for k in range(1, 8)]
        sends = []
        for k, (fx, fy, fc) in enumerate(flips):
            to = (1 - x if fx else x, 1 - y if fy else y, 1 - c if fc else c)
            cp = _remote(v_ref, buf.at[me], send_sems.at[k], recv_sems.at[k], to)
            cp.start()
            sends.append(cp)
        for k, (fx, fy, fc) in enumerate(flips):
            px, py, pc = (1 - x if fx else x, 1 - y if fy else y, 1 - c if fc else c)
            blk = buf.at[4 * px + 2 * py + pc]
            _remote(blk, blk, send_sems.at[k], recv_sems.at[k], (px, py, pc)).wait_recv()
        for cp in sends:
            cp.wait_send()
        acc = buf[0]
        for s in range(1, 8):
            acc = acc + buf[s]
        out_ref[...] = acc

    vm = pl.BlockSpec(memory_space=pltpu.VMEM)
    return pl.pallas_call(
        body, name="allreduce_small", in_specs=[vm], out_specs=vm, out_shape=_sds((R, C), F32),
        scratch_shapes=[pltpu.VMEM((8, R, C), F32), pltpu.SemaphoreType.DMA((7,)), pltpu.SemaphoreType.DMA((7,))],
        compiler_params=pltpu.CompilerParams(vmem_limit_bytes=VMEM_LIMIT),
    )(v)


def _rows_of(shape):
    return shape[0] * shape[1] // PACK_COLS


def _slot_rows(shape):
    return -(-_rows_of(shape) // PACK_ALIGN) * PACK_ALIGN


TRANSPOSED = ("w1_a", "w3_a", "w1_b", "w3_b")


def _stored(name, shard):
    return shard[0].T if name in TRANSPOSED else shard[0]


def _restored(name, stored):
    return stored.T[None] if name in TRANSPOSED else stored[None]


def _pack_shards(shards, dtype, entries):
    parts = []
    for name, shape, _ in entries:
        part = _stored(name, shards[name]).reshape(_rows_of(shape), PACK_COLS).astype(dtype)
        parts.append(jnp.pad(part, ((0, _slot_rows(shape) - part.shape[0]), (0, 0))))
    used = sum(p.shape[0] for p in parts)
    parts.append(jnp.zeros((_group_rows(entries) - used, PACK_COLS), dtype))
    return jnp.concatenate(parts, axis=0)


def _unpack_gathered(ag, entries):
    out, off = {}, 0
    for name, shape, axis in entries:
        r = _rows_of(shape)
        piece = ag[:, off:off + r, :]
        off += _slot_rows(shape)
        if name in TRANSPOSED:
            out[name] = piece.reshape(4 * r, PACK_COLS)
        elif axis == 0:
            out[name] = piece.reshape(4 * shape[0], shape[1])
        else:
            out[name] = piece.reshape((4,) + shape).transpose(1, 0, 2).reshape(shape[0], 4 * shape[1])
    return out


LATE = ("w_glu", "w_out", "w1_b", "w3_b", "w2_b", "w_ple_gate", "w_ple_proj")
GRAD_GROUPS = (tuple(e for e in BIG if e[0] in LATE), tuple(e for e in BIG if e[0] not in LATE))


GROUP_ROW_UNIT = 2816


def _group_rows(entries):
    used = sum(_slot_rows(shape) for _, shape, _ in entries)
    return -(-used // GROUP_ROW_UNIT) * GROUP_ROW_UNIT


def _pack_full_grads(grads, entries):
    parts = []
    for name, shape, axis in entries:
        g = grads[name]
        if name in TRANSPOSED or axis == 0:
            piece = g.reshape(4, _rows_of(shape), PACK_COLS)
        else:
            piece = g.reshape(shape[0], 4, shape[1]).transpose(1, 0, 2).reshape(4, _rows_of(shape), PACK_COLS)
        parts.append(jnp.pad(piece, ((0, 0), (0, _slot_rows(shape) - piece.shape[1]), (0, 0))))
    used = sum(p.shape[1] for p in parts)
    if _group_rows(entries) > used:
        parts.append(jnp.zeros((4, _group_rows(entries) - used, PACK_COLS), F32))
    return jnp.concatenate(parts, axis=1)


def _unpack_shards(packed, entries):
    out, off = {}, 0
    for name, shape, _ in entries:
        r = _rows_of(shape)
        out[name] = packed[off:off + r] if name in TRANSPOSED else packed[off:off + r].reshape(shape)
        off += _slot_rows(shape)
    return out


def _small_rows(shape):
    return -(-math.prod(shape) // 1024) * 8


def _pack_small(vals, extra=None):
    def slot(v, rows):
        flat = v.reshape(-1)
        return jnp.pad(flat, (0, rows * 128 - flat.shape[0])).reshape(rows, 128)

    parts = [slot(vals[name], _small_rows(shape)) for name, shape in SMALL]
    parts.append(slot(extra if extra is not None else jnp.zeros((1,), F32), 8))
    assert sum(p.shape[0] for p in parts) == SMALL_ROWS
    return jnp.concatenate(parts, axis=0)


def _unpack_small(packed):
    out, off = {}, 0
    for name, shape in SMALL:
        rows = _small_rows(shape)
        out[name] = packed[off:off + rows].reshape(-1)[:math.prod(shape)].reshape(shape)
        off += rows
    return out, packed[off, 0]


def _permute_time(a):
    T, n = a.shape
    return a.reshape(8, T // 8, n).transpose(1, 0, 2).reshape(T, n)


def _unpermute_time(a):
    T, n = a.shape
    return a.reshape(T // 8, 8, n).transpose(1, 0, 2).reshape(T, n)


def _discretize(a_re, a_im, log_dt, b_re, b_im):
    dt = jnp.exp(log_dt)[:, None]
    decay = jnp.exp(dt * a_re)
    abar_r = decay * jnp.cos(dt * a_im)
    abar_i = decay * jnp.sin(dt * a_im)
    nr, ni = abar_r - 1.0, abar_i
    den = a_re * a_re + a_im * a_im
    fr = (nr * a_re + ni * a_im) / den
    fi = (ni * a_re - nr * a_im) / den
    bbar_r = fr[..., None] * b_re - fi[..., None] * b_im
    bbar_i = fr[..., None] * b_im + fi[..., None] * b_re
    return abar_r, abar_i, bbar_r, bbar_i


def _input_matrix(bbar_r, bbar_i):
    eye = jnp.eye(N_GROUPS, dtype=F32)
    blk = lambda b: jnp.einsum("ghp,gk->ghkp", b.transpose(0, 2, 1), eye).reshape(SSM_W, STATE_W)
    return jnp.concatenate([blk(bbar_r), blk(bbar_i)], axis=1)


def _output_matrix(c_re, c_im):
    eye = jnp.eye(N_GROUPS, dtype=F32)
    blk = lambda cm: jnp.einsum("ghp,gk->gpkh", cm, eye).reshape(STATE_W, SSM_W)
    return jnp.concatenate([blk(c_re), -blk(c_im)], axis=0)


def _state_power(ar, ai, n):
    steps = int(round(math.log2(n)))
    assert 1 << steps == n
    for _ in range(steps):
        ar, ai = ar * ar - ai * ai, 2.0 * ar * ai
    return ar, ai


def kernel(x, p, g_ffn1, w1_a, w3_a, w2_a, g_mix, w_in, b_f, a_re, a_im, log_dt, b_re, b_im, c_re, c_im, d_skip, w_glu, b_glu, g_attn_out, g_ssm_out, w_out, g_ffn2, w1_b, w3_b, w2_b, g_ple, w_ple_gate, w_ple_proj, g_final, loss_target, m_g_ffn1, m_w1_a, m_w3_a, m_w2_a, m_g_mix, m_w_in, m_b_f, m_a_re, m_a_im, m_log_dt, m_b_re, m_b_im, m_c_re, m_c_im, m_d_skip, m_w_glu, m_b_glu, m_g_attn_out, m_g_ssm_out, m_w_out, m_g_ffn2, m_w1_b, m_w3_b, m_w2_b, m_g_ple, m_w_ple_gate, m_w_ple_proj, m_g_final, v_g_ffn1, v_w1_a, v_w3_a, v_w2_a, v_g_mix, v_w_in, v_b_f, v_a_re, v_a_im, v_log_dt, v_b_re, v_b_im, v_c_re, v_c_im, v_d_skip, v_w_glu, v_b_glu, v_g_attn_out, v_g_ssm_out, v_w_out, v_g_ffn2, v_w1_b, v_w3_b, v_w2_b, v_g_ple, v_w_ple_gate, v_w_ple_proj, v_g_final):
    args = dict(locals())
    weights = {n: args[n] for n in WEIGHT_ORDER}
    moms = {n: args["m_" + n] for n in WEIGHT_ORDER}
    vars_ = {n: args["v_" + n] for n in WEIGHT_ORDER}
    T = x.shape[1]
    x2, p2, tgt = x[0], p[0, 0], loss_target[0]

    late_entries, early_entries = GRAD_GROUPS
    full = _unpack_gathered(gather_shards(_pack_shards(weights, BF16, early_entries)), early_entries)
    core = lax.axis_index("c").astype(jnp.int32).reshape(1)
    chip = (2 * lax.axis_index("x") + lax.axis_index("y")).astype(jnp.int32).reshape(1)
    loss_part, dx, grads, late = _local_step(x2, p2, tgt, {n: weights[n] for n, _ in SMALL}, full,
                                             early_exchange=lambda g: _pair_of(g, late_entries, core),
                                             late_pack=_pack_shards(weights, BF16, late_entries))
    return _reduce_and_update(weights, moms, vars_, loss_part, dx, grads, core, chip, late)


def _pair_of(grads, entries, core):
    packed = _pack_full_grads(grads, entries)
    return pair_sum(packed, sibling_split(packed), core)


def _local_step(x2, p2, tgt, sm, full, early_exchange=None, late_pack=None):
    full = dict(full)
    T = x2.shape[0]
    (g_ffn1, g_mix, b_f, a_re, a_im, log_dt, b_re, b_im, c_re, c_im, d_skip, b_glu, g_attn_out, g_ssm_out, g_ffn2, g_ple,
     g_final) = (sm[n] for n, _ in SMALL)
    w_in_f = full["w_in"]
    w_in_r = jnp.concatenate([w_in_f[:, :ATTN_W] * QK_SCALE, w_in_f[:, ATTN_W:3 * ATTN_W], w_in_f[:, 3 * ATTN_W + N_HEADS:],
                              w_in_f[:, 3 * ATTN_W:3 * ATTN_W + N_HEADS], jnp.zeros((D_MODEL, 128 - N_HEADS), BF16)], axis=1)
    b_f_pad = jnp.pad(b_f, ((0, 0), (0, 128 - N_HEADS)))

    disc_in = (a_re[0], a_im[0], log_dt[0], b_re[0], b_im[0])
    (abar_r, abar_i, bbar_r, bbar_i), disc_vjp = jax.vjp(_discretize, *disc_in)
    wb = _input_matrix(bbar_r, bbar_i)
    cbd = _output_matrix(c_re[0], c_im[0])
    ar, ai = abar_r.reshape(1, STATE_W), abar_i.reshape(1, STATE_W)
    alr, ali = _state_power(ar, ai, T // 8)
    dvec = d_skip.reshape(1, SSM_W)
    wb16, cbd16 = wb.astype(BF16), cbd.astype(BF16)

    h1, a1a, a3a, n1 = ffn_fwd(x2, g_ffn1, full["w1_a"], full["w3_a"], full["w2_a"], "ffn_a_fwd")
    u, qkv, s_in, fz, cum = mixin_fwd(h1, g_mix, w_in_r, b_f_pad)
    q_aug, k_aug, v_aug = heads_in(qkv, cum)
    if late_pack is None:
        o_heads, q_bwd = attn_fwd(q_aug, k_aug, v_aug)
    else:
        o_heads, q_bwd, gathered = attn_fwd(q_aug, k_aug, v_aug, send=late_pack)
        full.update(_unpack_gathered(gathered, GRAD_GROUPS[0]))
    s_perm = _permute_time(s_in)
    y_perm, xs = ssm_fwd(s_perm, wb16, cbd16, ar, ai, alr, ali, dvec)
    ypre = _unpermute_time(y_perm)
    h2, mixed = mixout_fwd(h1, o_heads, ypre, g_attn_out, g_ssm_out, full["w_glu"], b_glu, full["w_out"])
    h3, a1b, a3b, n2 = ffn_fwd(h2, g_ffn2, full["w1_b"], full["w3_b"], full["w2_b"], "ffn_b_fwd")

    dh3, n3, dzg, dpp, loss_part, dg_ple, dg_final = head_fwd_bwd(
        h3, p2, tgt, g_ple, g_final.reshape(1, D_MODEL), full["w_ple_gate"], full["w_ple_proj"])
    grads = {"g_ple": dg_ple, "g_final": dg_final.reshape(D_MODEL)}
    grads["w_ple_gate"] = mm_tn(n3, dzg, "dw_ple_gate")
    grads["w_ple_proj"] = mm_tn(p2, dpp, "dw_ple_proj")

    dh2, da1, da3, act, grads["g_ffn2"] = ffn_bwd(h2, g_ffn2, dh3, a1b, a3b, full["w1_b"], full["w3_b"], full["w2_b"], "ffn_b_bwd")
    grads["w1_b"] = mm_tn(da1, n2, "dw1_b")
    grads["w3_b"] = mm_tn(da3, n2, "dw3_b")
    grads["w2_b"] = mm_tn(act, dh3, "dw2_b", scale=0.5)

    seg = (jnp.arange(ATTN_W)[:, None] // HEAD_DIM == jnp.arange(128)[None, :]).astype(F32)
    do_aug, dypre, dpre, yg, grads["g_attn_out"], grads["g_ssm_out"], grads["b_glu"] = mixout_bwd(
        dh2, o_heads, ypre, g_attn_out, g_ssm_out, full["w_glu"], b_glu, full["w_out"], seg)
    grads["w_out"] = mm_tn(mixed, dh2, "dw_out")
    grads["w_glu"] = mm_tn(yg, dpre, "dw_glu")

    if early_exchange is None:
        late = None
        dq_aug, dk_aug, dv_aug, dc_rows = attn_bwd(q_bwd, k_aug, v_aug, do_aug)
    else:
        pair_late = early_exchange(grads)
        dq_aug, dk_aug, dv_aug, dc_rows, got_late = attn_bwd(q_bwd, k_aug, v_aug, do_aug, pair=pair_late)
        late = (pair_late, got_late)
    dc = jnp.pad(dc_rows.reshape(N_HEADS, T).T, ((0, 0), (0, 128 - N_HEADS)))

    dy_perm = _permute_time(dypre)
    du_perm, gs, d_a, dd = ssm_bwd(dy_perm, s_perm, xs, cbd16.T, wb16.T, ar, ai, alr, ali, dvec)
    ds_in = _unpermute_time(du_perm)
    hg = N_GROUPS // 2
    d_in, d_out = [], []
    for part in range(2):
        ins, outs = [], []
        for half in range(2):
            states = (part * STATE_W + half * _HALF_ST, _HALF_ST)
            chans = (half * _HALF_CH, _HALF_CH)
            blk = mm_tn(s_perm, gs, f"dw_ssm_in_{part}{half}", a_cols=chans, b_cols=states)
            ins.append(jnp.einsum("ghgp->ghp", blk.reshape(hg, GROUP_CH, hg, N_STATE)))
            blk = mm_tn(xs, dy_perm, f"dw_ssm_out_{part}{half}", a_cols=states, b_cols=chans)
            outs.append(jnp.einsum("gpgh->gph", blk.reshape(hg, N_STATE, hg, GROUP_CH)))
        d_in.append(jnp.concatenate(ins, axis=0).transpose(0, 2, 1))
        d_out.append(jnp.concatenate(outs, axis=0).transpose(0, 2, 1))
    d_abar_r = jnp.sum(d_a[:, :STATE_W], axis=0).reshape(N_GROUPS, N_STATE)
    d_abar_i = jnp.sum(d_a[:, STATE_W:], axis=0).reshape(N_GROUPS, N_STATE)
    d_disc = disc_vjp((d_abar_r, d_abar_i, d_in[0], d_in[1]))
    for name, val in zip(("a_re", "a_im", "log_dt", "b_re", "b_im"), d_disc):
        grads[name] = val[None]
    grads["c_re"] = d_out[0][None]
    grads["c_im"] = -d_out[1][None]
    grads["d_skip"] = dd.reshape(1, N_GROUPS, GROUP_CH)

    dh1, dz, grads["g_mix"], dbf = mixin_bwd(dh2, h1, g_mix, w_in_r, dq_aug, dk_aug, dv_aug, ds_in, dc, fz)
    grads["b_f"] = dbf[:, :N_HEADS]
    d_w_in_r = mm_tn(u, dz, "dw_in")
    grads["w_in"] = jnp.concatenate([d_w_in_r[:, :ATTN_W] * QK_SCALE, d_w_in_r[:, ATTN_W:3 * ATTN_W],
                                     d_w_in_r[:, 3 * ATTN_W + SSM_W:3 * ATTN_W + SSM_W + N_HEADS],
                                     d_w_in_r[:, 3 * ATTN_W:3 * ATTN_W + SSM_W]], axis=1)

    dx, da1, da3, act, grads["g_ffn1"] = ffn_bwd(x2, g_ffn1, dh1, a1a, a3a, full["w1_a"], full["w3_a"], full["w2_a"], "ffn_a_bwd")
    grads["w1_a"] = mm_tn(da1, n1, "dw1_a")
    grads["w3_a"] = mm_tn(da3, n1, "dw3_a")
    grads["w2_a"] = mm_tn(act, dh1, "dw2_a", scale=0.5)
    return loss_part, dx, grads, late


def _reduce_and_update(weights, moms, vars_, loss_part, dx, grads, core, chip, late):
    pair_early = _pair_of(grads, GRAD_GROUPS[1], core)
    g_stored = {}
    for entries, (pair, got) in zip(GRAD_GROUPS, (late, (pair_early, chip_exchange(pair_early)))):
        half = chip_sum(pair, got, chip)
        g_stored.update(_unpack_shards(join_halves(half, sibling_swap(half), core), entries))
    g_out, d_out, m_out, v_out = {}, {}, {}, {}
    for n, _, _ in BIG:
        d, m, v = adamw(_stored(n, weights[n]), g_stored[n], _stored(n, moms[n]), _stored(n, vars_[n]), "adamw_" + n)
        g_out[n], d_out[n], m_out[n], v_out[n] = (_restored(n, a) for a in (g_stored[n], d, m, v))

    small = allreduce_small(_pack_small({n: grads[n] for n, _ in SMALL}, extra=loss_part[0, 0]))
    d_small, m_small, v_small = adamw(_pack_small(weights), small, _pack_small(moms), _pack_small(vars_), "adamw_small")

    g_small, loss = _unpack_small(small)
    g_out.update(g_small)
    outs = []
    for big, sm in ((d_out, d_small), (m_out, m_small), (v_out, v_small)):
        o, _ = _unpack_small(sm)
        o.update(big)
        outs.append(o)
    result = [loss, dx[None]] + [g_out[n] for n in WEIGHT_ORDER]
    for o in outs:
        result += [o[n] for n in WEIGHT_ORDER]
    return tuple(result)
```

```python
import functools
import math

import jax
import jax.numpy as jnp
from jax import lax
from jax.experimental import pallas as pl
from jax.experimental.pallas import tpu as pltpu

F32 = jnp.float32
BF16 = jnp.bfloat16

D_MODEL = 1024
D_FF = 2816
N_HEADS = 8
HEAD_DIM = 64
ATTN_W = 512
SSM_W = 512
N_GROUPS = 32
N_STATE = 64
GROUP_CH = 16
STATE_W = N_GROUPS * N_STATE
Z_COLS = 2176
QK_SCALE = 0.125
EPS = 1e-6

ADAM_LR = 0.001
ADAM_B1 = 0.9
ADAM_B2 = 0.999
ADAM_EPS = 1e-08
ADAM_WD = 0.01
ADAM_STEP = 10

TOKEN_TILE = 512
FFN_TOKEN_TILE = 256
FF_CHUNK = 1408
MM_K_TILE = 2048
ATTN_TILE = 512
SCAN_STEPS = 32
SCAN_LANES = 512
VMEM_LIMIT = 48 * 1024 * 1024
FFN_VMEM_LIMIT = 56 * 1024 * 1024
COPY_CHUNKS = 4

NT_DIMS = (((1,), (1,)), ((), ()))
TN_DIMS = (((0,), (0,)), ((), ()))
HIGHEST = lax.Precision.HIGHEST
MESH = pl.DeviceIdType.MESH

BIG = (
    ("w1_a", (1024, 704), 1), ("w3_a", (1024, 704), 1), ("w2_a", (704, 1024), 0),
    ("w_in", (1024, 514), 1), ("w_glu", (128, 512), 0), ("w_out", (256, 1024), 0),
    ("w1_b", (1024, 704), 1), ("w3_b", (1024, 704), 1), ("w2_b", (704, 1024), 0),
    ("w_ple_gate", (256, 1024), 0), ("w_ple_proj", (256, 256), 1),
)
PACK_COLS = 1024
PACK_ALIGN = 16
SMALL = (
    ("g_ffn1", (1, 1024)), ("g_mix", (1, 1024)), ("b_f", (1, 8)), ("a_re", (1, 32, 64)), ("a_im", (1, 32, 64)),
    ("log_dt", (1, 32)), ("b_re", (1, 32, 64, 16)), ("b_im", (1, 32, 64, 16)), ("c_re", (1, 32, 16, 64)),
    ("c_im", (1, 32, 16, 64)), ("d_skip", (1, 32, 16)), ("b_glu", (1, 512)), ("g_attn_out", (1, 512)),
    ("g_ssm_out", (1, 512)), ("g_ffn2", (1, 1024)), ("g_ple", (1, 1024)), ("g_final", (1024,)),
)
SMALL_ROWS = 1152
WEIGHT_ORDER = ("g_ffn1", "w1_a", "w3_a", "w2_a", "g_mix", "w_in", "b_f", "a_re", "a_im", "log_dt", "b_re", "b_im",
                "c_re", "c_im", "d_skip", "w_glu", "b_glu", "g_attn_out", "g_ssm_out", "w_out", "g_ffn2", "w1_b",
                "w3_b", "w2_b", "g_ple", "w_ple_gate", "w_ple_proj", "g_final")


def _params(sem=None, vmem=VMEM_LIMIT):
    kw = dict(vmem_limit_bytes=vmem)
    if sem is not None:
        kw["dimension_semantics"] = sem
    return pltpu.CompilerParams(**kw)


def _sds(shape, dtype):
    return jax.ShapeDtypeStruct(shape, dtype)


def _tile(n, pref):
    t = min(n, pref)
    assert n % t == 0, (n, pref)
    return t


def _rms_scale(x):
    return lax.rsqrt(jnp.mean(x * x, axis=-1, keepdims=True) + EPS)


def _rms_bwd(dy, x, g):
    r = _rms_scale(x)
    xh = x * r
    dxh = dy * g
    dx = r * (dxh - xh * jnp.mean(dxh * xh, axis=-1, keepdims=True))
    return dx, jnp.sum(dy * xh, axis=0, keepdims=True)


def _dot(a, b):
    return jnp.dot(a, b, preferred_element_type=F32)


def _dot_nt(a, b):
    return lax.dot_general(a, b, NT_DIMS, preferred_element_type=F32)


def _dot_tn(a, b):
    return lax.dot_general(a, b, TN_DIMS, preferred_element_type=F32)


_GELU_C = math.sqrt(2.0 / math.pi)


def _gelu_parts(x):
    t = jnp.tanh(_GELU_C * (x + 0.044715 * x * x * x))
    return 0.5 * x * (1.0 + t), t


def _gelu_grad(x, t):
    return 0.5 * (1.0 + t) + 0.5 * x * (1.0 - t * t) * _GELU_C * (1.0 + 3.0 * 0.044715 * x * x)


def _resident(shape):
    return pl.BlockSpec(shape, lambda i: (0,) * len(shape), pipeline_mode=pl.Buffered(1))


def ffn_fwd(h, g, w1, w3, w2, name):
    T = h.shape[0]
    tm = _tile(T, FFN_TOKEN_TILE)

    def body(h_ref, g_ref, w1_ref, w3_ref, w2_ref, ho_ref, a1_ref, a3_ref, n_ref):
        x = h_ref[...]
        n = (x * _rms_scale(x) * g_ref[...]).astype(BF16)
        n_ref[...] = n
        out = x
        for lo in range(0, D_FF, FF_CHUNK):
            cols = slice(lo, lo + FF_CHUNK)
            a1 = _dot_nt(n, w1_ref[cols, :])
            a3 = _dot_nt(n, w3_ref[cols, :])
            a1_ref[:, cols] = a1.astype(BF16)
            a3_ref[:, cols] = a3.astype(BF16)
            act = (a1 * jax.nn.sigmoid(a1) * a3).astype(BF16)
            out = out + 0.5 * _dot(act, w2_ref[cols, :])
        ho_ref[...] = out

    tok = lambda i: (i, 0)
    return pl.pallas_call(
        body, name=name, grid=(T // tm,),
        in_specs=[pl.BlockSpec((tm, D_MODEL), tok), _resident((1, D_MODEL)), _resident((D_FF, D_MODEL)),
                  _resident((D_FF, D_MODEL)), _resident((D_FF, D_MODEL))],
        out_specs=[pl.BlockSpec((tm, D_MODEL), tok), pl.BlockSpec((tm, D_FF), tok), pl.BlockSpec((tm, D_FF), tok),
                   pl.BlockSpec((tm, D_MODEL), tok)],
        out_shape=[_sds((T, D_MODEL), F32), _sds((T, D_FF), BF16), _sds((T, D_FF), BF16), _sds((T, D_MODEL), BF16)],
        compiler_params=_params(("arbitrary",), FFN_VMEM_LIMIT),
    )(h, g, w1, w3, w2)


def ffn_bwd(h, g, dho, a1, a3, w1, w3, w2, name):
    T = h.shape[0]
    tm = _tile(T, FFN_TOKEN_TILE)

    def body(h_ref, g_ref, dho_ref, a1_ref, a3_ref, w1_ref, w3_ref, w2_ref, dhi_ref, da1_ref, da3_ref, act_ref, dg_ref):
        @pl.when(pl.program_id(0) == 0)
        def _():
            dg_ref[...] = jnp.zeros_like(dg_ref)

        dho = dho_ref[...]
        dhb = (0.5 * dho).astype(BF16)
        dn = None
        for lo in range(0, D_FF, FF_CHUNK):
            cols = slice(lo, lo + FF_CHUNK)
            a1v = a1_ref[:, cols].astype(F32)
            a3v = a3_ref[:, cols].astype(F32)
            s = jax.nn.sigmoid(a1v)
            sl = a1v * s
            dact = _dot_nt(dhb, w2_ref[cols, :])
            act_ref[:, cols] = (sl * a3v).astype(BF16)
            da1 = (dact * a3v * s * (1.0 + a1v * (1.0 - s))).astype(BF16)
            da3 = (dact * sl).astype(BF16)
            da1_ref[:, cols] = da1
            da3_ref[:, cols] = da3
            part = _dot(da1, w1_ref[cols, :]) + _dot(da3, w3_ref[cols, :])
            dn = part if dn is None else dn + part
        dx, dg = _rms_bwd(dn, h_ref[...], g_ref[...])
        dg_ref[...] += dg
        dhi_ref[...] = dho + dx

    tok = lambda i: (i, 0)
    return pl.pallas_call(
        body, name=name, grid=(T // tm,),
        in_specs=[pl.BlockSpec((tm, D_MODEL), tok), _resident((1, D_MODEL)), pl.BlockSpec((tm, D_MODEL), tok),
                  pl.BlockSpec((tm, D_FF), tok), pl.BlockSpec((tm, D_FF), tok), _resident((D_FF, D_MODEL)),
                  _resident((D_FF, D_MODEL)), _resident((D_FF, D_MODEL))],
        out_specs=[pl.BlockSpec((tm, D_MODEL), tok), pl.BlockSpec((tm, D_FF), tok), pl.BlockSpec((tm, D_FF), tok),
                   pl.BlockSpec((tm, D_FF), tok), pl.BlockSpec((1, D_MODEL), lambda i: (0, 0))],
        out_shape=[_sds((T, D_MODEL), F32), _sds((T, D_FF), BF16), _sds((T, D_FF), BF16), _sds((T, D_FF), BF16),
                   _sds((1, D_MODEL), F32)],
        compiler_params=_params(("arbitrary",), FFN_VMEM_LIMIT),
    )(h, g, dho, a1, a3, w1, w3, w2)


def mm_tn(a, b, name, scale=1.0, a_cols=None, b_cols=None):
    T = a.shape[0]
    a_off, M = a_cols or (0, a.shape[1])
    b_off, N = b_cols or (0, b.shape[1])
    bm = 512 if M % 512 == 0 else (1408 if M == 2816 else 256)
    bn = N if N in (2176, 1408) else (1408 if N == 2816 else min(N, 1024))
    tk = _tile(T, MM_K_TILE)
    row_bytes = 2 * (bm * a.dtype.itemsize + bn * b.dtype.itemsize)
    while tk > TOKEN_TILE and tk * row_bytes > VMEM_LIMIT // 3:
        tk //= 2
    assert M % bm == 0 and N % bn == 0 and T % tk == 0 and a_off % bm == 0 and b_off % bn == 0
    n_k = T // tk
    m0, n0 = a_off // bm, b_off // bn

    def body(a_ref, b_ref, o_ref):
        k = pl.program_id(2)

        @pl.when(k == 0)
        def _():
            o_ref[...] = jnp.zeros_like(o_ref)

        o_ref[...] += _dot_tn(a_ref[...].astype(BF16), b_ref[...].astype(BF16))

        if scale != 1.0:
            @pl.when(k == n_k - 1)
            def _():
                o_ref[...] = o_ref[...] * scale

    return pl.pallas_call(
        body, name=name, grid=(M // bm, N // bn, n_k),
        in_specs=[pl.BlockSpec((tk, bm), lambda m, n, k: (k, m0 + m)), pl.BlockSpec((tk, bn), lambda m, n, k: (k, n0 + n))],
        out_specs=pl.BlockSpec((bm, bn), lambda m, n, k: (m, n)),
        out_shape=_sds((M, N), F32),
        compiler_params=_params(("arbitrary", "arbitrary", "arbitrary")),
    )(a, b)


def mixin_fwd(h1, g, w_in_r, b_f_pad):
    T = h1.shape[0]
    tm = _tile(T, TOKEN_TILE)

    def body(h_ref, g_ref, w_ref, bf_ref, u_ref, qkv_ref, s_ref, fz_ref, c_ref, carry):
        @pl.when(pl.program_id(0) == 0)
        def _():
            carry[...] = jnp.zeros_like(carry)

        x = h_ref[...]
        u = (x * _rms_scale(x) * g_ref[...]).astype(BF16)
        u_ref[...] = u
        z = _dot(u, w_ref[...])
        qkv_ref[...] = z[:, :3 * ATTN_W].astype(BF16)
        s_ref[...] = z[:, 3 * ATTN_W:3 * ATTN_W + SSM_W]
        fz = z[:, 3 * ATTN_W + SSM_W:] + bf_ref[...]
        fz_ref[...] = fz
        lane = lax.broadcasted_iota(jnp.int32, fz.shape, 1)
        logf = jnp.where(lane < N_HEADS, jnp.minimum(fz, 0.0) - jnp.log(1.0 + jnp.exp(-jnp.abs(fz))), 0.0)
        row = lax.broadcasted_iota(jnp.int32, (tm, tm), 0)
        col = lax.broadcasted_iota(jnp.int32, (tm, tm), 1)
        tri = (col <= row).astype(F32)
        cs = jnp.dot(tri, logf, precision=HIGHEST, preferred_element_type=F32) + carry[0:1, :]
        c_ref[...] = cs
        carry[...] = jnp.broadcast_to(cs[tm - 1:tm, :], carry.shape)

    tok = lambda i: (i, 0)
    fix = lambda i: (0, 0)
    return pl.pallas_call(
        body, name="mixin_fwd", grid=(T // tm,),
        in_specs=[pl.BlockSpec((tm, D_MODEL), tok), pl.BlockSpec((1, D_MODEL), fix),
                  pl.BlockSpec((D_MODEL, Z_COLS), fix), pl.BlockSpec((1, 128), fix)],
        out_specs=[pl.BlockSpec((tm, D_MODEL), tok), pl.BlockSpec((tm, 3 * ATTN_W), tok), pl.BlockSpec((tm, SSM_W), tok),
                   pl.BlockSpec((tm, 128), tok), pl.BlockSpec((tm, 128), tok)],
        out_shape=[_sds((T, D_MODEL), BF16), _sds((T, 3 * ATTN_W), BF16), _sds((T, SSM_W), F32),
                   _sds((T, 128), F32), _sds((T, 128), F32)],
        scratch_shapes=[pltpu.VMEM((8, 128), F32)],
        compiler_params=_params(("arbitrary",)),
    )(h1, g, w_in_r, b_f_pad)


def mixin_bwd(dh2, h1, g, w_in_r, dq, dk, dv, ds, dc, fz):
    T = h1.shape[0]
    tm = _tile(T, TOKEN_TILE)
    n_t = T // tm
    assert dq.shape[1:] == (n_t, 128, tm), dq.shape

    def body(dh2_ref, h_ref, g_ref, w_ref, dq_ref, dk_ref, dv_ref, ds_ref, dc_ref, fz_ref,
             dh1_ref, dz_ref, dg_ref, dbf_ref, carry):
        @pl.when(pl.program_id(0) == 0)
        def _():
            carry[...] = jnp.zeros_like(carry)
            dg_ref[...] = jnp.zeros_like(dg_ref)
            dbf_ref[...] = jnp.zeros_like(dbf_ref)

        row = lax.broadcasted_iota(jnp.int32, (tm, tm), 0)
        col = lax.broadcasted_iota(jnp.int32, (tm, tm), 1)
        tri = (col >= row).astype(F32)
        dlogf = jnp.dot(tri, dc_ref[...], precision=HIGHEST, preferred_element_type=F32) + carry[0:1, :]
        carry[...] = jnp.broadcast_to(dlogf[0:1, :], carry.shape)
        dfz = dlogf * jax.nn.sigmoid(-fz_ref[...])
        dbf_ref[...] += jnp.sum(dfz, axis=0, keepdims=True)
        dz = jnp.concatenate([_join_heads(dq_ref, BF16, transposed=True), _join_heads(dk_ref, BF16), _join_heads(dv_ref, BF16),
                              ds_ref[...], dfz], axis=1).astype(BF16)
        dz_ref[...] = dz
        du = _dot_nt(dz, w_ref[...])
        dx, dg = _rms_bwd(du, h_ref[...], g_ref[...])
        dg_ref[...] += dg
        dh1_ref[...] = dh2_ref[...] + dx

    tok = lambda i: (n_t - 1 - i, 0)
    fix = lambda i: (0, 0)
    heads = pl.BlockSpec((N_HEADS, tm, 128), lambda i: (0, n_t - 1 - i, 0))
    return pl.pallas_call(
        body, name="mixin_bwd", grid=(n_t,),
        in_specs=[pl.BlockSpec((tm, D_MODEL), tok), pl.BlockSpec((tm, D_MODEL), tok), pl.BlockSpec((1, D_MODEL), fix),
                  pl.BlockSpec((D_MODEL, Z_COLS), fix), pl.BlockSpec((N_HEADS, 1, 128, tm), lambda i: (0, n_t - 1 - i, 0, 0)),
                  heads, heads, pl.BlockSpec((tm, SSM_W), tok),
                  pl.BlockSpec((tm, 128), tok), pl.BlockSpec((tm, 128), tok)],
        out_specs=[pl.BlockSpec((tm, D_MODEL), tok), pl.BlockSpec((tm, Z_COLS), tok), pl.BlockSpec((1, D_MODEL), fix),
                   pl.BlockSpec((1, 128), fix)],
        out_shape=[_sds((T, D_MODEL), F32), _sds((T, Z_COLS), BF16), _sds((1, D_MODEL), F32), _sds((1, 128), F32)],
        scratch_shapes=[pltpu.VMEM((8, 128), F32)],
        compiler_params=_params(("arbitrary",)),
    )(dh2, h1, g, w_in_r, dq, dk, dv, ds, dc, fz)


def _lane_move(src_lo, dst_lo, width, dtype):
    r = lax.broadcasted_iota(jnp.int32, (128, 128), 0)
    c = lax.broadcasted_iota(jnp.int32, (128, 128), 1)
    return ((c - dst_lo == r - src_lo) & (r >= src_lo) & (r < src_lo + width)).astype(dtype)


def _lane_const(lo, width, value):
    lane = lax.broadcasted_iota(jnp.int32, (1, 128), 1)
    return jnp.where((lane >= lo) & (lane < lo + width), value, 0.0).astype(F32)


def _pieces(a):
    hi = a.astype(BF16)
    rest = a - hi.astype(F32)
    mid = rest.astype(BF16)
    return hi, mid, (rest - mid.astype(F32)).astype(BF16)


def _head_features(pair_block, e):
    return _dot(pair_block, _lane_move(HEAD_DIM * e, 0, HEAD_DIM, BF16))


def _helper_columns(pieces, head, sign):
    out = None
    for k, piece in enumerate(pieces):
        term = _dot(piece, _lane_move(head, HEAD_DIM + k, 1, BF16))
        out = term if out is None else out + term
    return sign * out


def heads_in(qkv, cum):
    T = qkv.shape[0]
    tm = _tile(T, TOKEN_TILE)

    def body(qkv_ref, c_ref, q_ref, k_ref, v_ref):
        c = _pieces(c_ref[...])
        for h in range(N_HEADS):
            p, e = divmod(h, 2)
            blk = lambda base: qkv_ref[:, base + 128 * p:base + 128 * (p + 1)]
            q_ref[h] = (_head_features(blk(0), e) + _lane_const(HEAD_DIM, 3, -1.0)).astype(BF16)
            k_ref[h] = (_head_features(blk(ATTN_W), e) + _helper_columns(c, h, 1.0)
                        + _lane_const(HEAD_DIM + 3, 3, 1.0)).astype(BF16)
            v_ref[h] = (_head_features(blk(2 * ATTN_W), e) + _lane_const(HEAD_DIM, 3, 1.0)).astype(BF16)

    tok = lambda i: (i, 0)
    heads = pl.BlockSpec((N_HEADS, tm, 128), lambda i: (0, i, 0))
    return pl.pallas_call(
        body, name="heads_in", grid=(T // tm,),
        in_specs=[pl.BlockSpec((tm, 3 * ATTN_W), tok), pl.BlockSpec((tm, 128), tok)],
        out_specs=[heads] * 3, out_shape=[_sds((N_HEADS, T, 128), BF16)] * 3,
        compiler_params=_params(("arbitrary",)),
    )(qkv, cum)


def attn_fwd(q_aug, k_aug, v_aug, send=None):
    H, T, wd = q_aug.shape
    hd = HEAD_DIM
    tq = _tile(T, ATTN_TILE)
    n = T // tq

    def body(q_ref, k_ref, v_ref, o_ref, qb_ref, m_sc, acc, s_even, s_odd):
        qi = pl.program_id(1)
        qv = q_ref[0]
        m_sc[...] = jnp.full_like(m_sc, -jnp.inf)
        acc[...] = jnp.zeros_like(acc)

        def key_rows(j):
            return pl.ds(pl.multiple_of(jnp.minimum(j, qi) * tq, tq), tq)

        def logits(j, buf):
            buf[...] = _dot_nt(k_ref[0, key_rows(j), :], qv)

        def update(j, buf, masked):
            st = buf[...]
            if masked:
                keep = lax.broadcasted_iota(jnp.int32, (tq, tq), 0) <= lax.broadcasted_iota(jnp.int32, (tq, tq), 1)
                st = jnp.where(keep, st, -1e30)
            m_old = m_sc[...]
            m_new = jnp.maximum(m_old, jnp.max(st, axis=0, keepdims=True))
            pt = jnp.exp(st - m_new).astype(BF16)
            acc[...] = jnp.exp(m_old - m_new) * acc[...] + _dot_tn(v_ref[0, key_rows(j), :], pt)
            m_sc[...] = m_new

        logits(0, s_even)

        def two_tiles(p, carry):
            j = 2 * p
            logits(j + 1, s_odd)
            update(j, s_even, False)
            logits(j + 2, s_even)
            update(j + 1, s_odd, False)
            return carry

        lax.fori_loop(0, qi // 2, two_tiles, 0)

        @pl.when(qi % 2 == 0)
        def _():
            update(qi, s_even, True)

        @pl.when(qi % 2 == 1)
        def _():
            logits(qi, s_odd)
            update(qi - 1, s_even, False)
            update(qi, s_odd, True)

        total = acc[hd:hd + 1, :]
        o_ref[0] = (acc[...] / total).T
        hi, mid, lo = (t.astype(F32) for t in _pieces(-(m_sc[...] + jnp.log(total))))
        row = lax.broadcasted_iota(jnp.int32, (wd, tq), 0)
        lse_rows = jnp.where(row == hd + 3, hi, jnp.where(row == hd + 4, mid, jnp.where(row == hd + 5, lo, 0.0)))
        qb_ref[0] = (qv.astype(F32) + lse_rows.T).astype(BF16)

    qmap = lambda h, i: (h, i, 0)
    head = lambda h, i: (h, 0, 0)
    in_specs = [pl.BlockSpec((1, tq, wd), qmap), pl.BlockSpec((1, T, wd), head), pl.BlockSpec((1, T, wd), head)]
    out_specs = [pl.BlockSpec((1, tq, wd), qmap), pl.BlockSpec((1, tq, wd), qmap)]
    out_shape = [_sds((H, T, wd), F32), _sds((H, T, wd), BF16)]
    scratch = [pltpu.VMEM((1, tq), F32), pltpu.VMEM((wd, tq), F32), pltpu.VMEM((tq, tq), F32), pltpu.VMEM((tq, tq), F32)]
    operands = (q_aug, k_aug, v_aug)
    if send is None:
        kernel_body = body
    else:
        def kernel_body(q_ref, k_ref, v_ref, send_ref, o_ref, qb_ref, got_ref, m_sc, acc, s_even, s_odd, *sems):
            h, i = pl.program_id(0), pl.program_id(1)

            @pl.when((h == 0) & (i == 0))
            def _():
                _gather_start(send_ref, got_ref, *sems)

            body(q_ref, k_ref, v_ref, o_ref, qb_ref, m_sc, acc, s_even, s_odd)

            @pl.when((h == H - 1) & (i == n - 1))
            def _():
                _gather_finish(send_ref, got_ref, *sems)

        in_specs, out_specs = in_specs + [_HBM], out_specs + [_HBM]
        out_shape = out_shape + [_sds((4,) + send.shape, send.dtype)]
        scratch, operands = scratch + _gather_scratch(send.shape[0]), operands + (send,)
    return pl.pallas_call(
        kernel_body, name="attn_fwd", grid=(H, n), in_specs=in_specs, out_specs=out_specs, out_shape=out_shape,
        scratch_shapes=scratch, compiler_params=_params(("arbitrary", "arbitrary")),
    )(*operands)


def attn_bwd(q_aug, k_aug, v_aug, do_aug, pair=None):
    H, T, wd = q_aug.shape
    tq = _tile(T, ATTN_TILE)
    n = T // tq

    def compute(q_ref, do_ref, k_ref, v_ref, dq_ref, dk_ref, dv_ref, dc_ref, dck, s_a, d_a, s_b, d_b):
        j = pl.program_id(1)

        @pl.when(j == 0)
        def _():
            dq_ref[...] = jnp.zeros_like(dq_ref)
            dc_ref[...] = jnp.zeros_like(dc_ref)

        dk_ref[...] = jnp.zeros_like(dk_ref)
        dv_ref[...] = jnp.zeros_like(dv_ref)
        dck[...] = jnp.zeros_like(dck)
        kv, vv = k_ref[0], v_ref[0]

        def query_rows(i):
            return pl.ds(pl.multiple_of(jnp.minimum(i, n - 1) * tq, tq), tq)

        def products(i, s_buf, d_buf):
            rows = query_rows(i)
            s_buf[...] = _dot_nt(kv, q_ref[0, rows, :])
            d_buf[...] = _dot_nt(vv, do_ref[0, rows, :])

        def update(i, s_buf, d_buf, masked):
            rows = query_rows(i)
            qv, dov = q_ref[0, rows, :], do_ref[0, rows, :]
            pt = jnp.exp(s_buf[...])
            if masked:
                keep = lax.broadcasted_iota(jnp.int32, (tq, tq), 0) <= lax.broadcasted_iota(jnp.int32, (tq, tq), 1)
                pt = jnp.where(keep, pt, 0.0)
            dv_ref[0] += _dot(pt.astype(BF16), dov)
            dst = pt * d_buf[...]
            dsb = dst.astype(BF16)
            dk_ref[0] += _dot(dsb, qv)
            dq_ref[0, i] += _dot_tn(kv, dsb)
            dck[...] += jnp.sum(dst, axis=1, keepdims=True)
            dc_ref[0, pl.ds(i, 1), :] += jnp.sum(dst, axis=0, keepdims=True)

        products(j, s_a, d_a)
        products(j + 1, s_b, d_b)
        update(j, s_a, d_a, True)
        left = n - 1 - j

        def two_tiles(p, carry):
            i = j + 1 + 2 * p
            products(i + 1, s_a, d_a)
            update(i, s_b, d_b, False)
            products(i + 2, s_b, d_b)
            update(i + 1, s_a, d_a, False)
            return carry

        lax.fori_loop(0, left // 2, two_tiles, 0)

        @pl.when(left % 2 == 1)
        def _():
            update(n - 1, s_b, d_b, False)

        dc_ref[0, pl.ds(j, 1), :] -= jnp.broadcast_to(dck[...], (tq, 128)).T[0:1, :]

    head = lambda h, j: (h, 0, 0)
    kmap = lambda h, j: (h, j, 0)
    in_specs = [pl.BlockSpec((1, T, wd), head), pl.BlockSpec((1, T, wd), head), pl.BlockSpec((1, tq, wd), kmap),
                pl.BlockSpec((1, tq, wd), kmap)]
    out_specs = [pl.BlockSpec((1, n, wd, tq), lambda h, j: (h, 0, 0, 0)), pl.BlockSpec((1, tq, wd), kmap),
                 pl.BlockSpec((1, tq, wd), kmap), pl.BlockSpec((1, n, tq), head)]
    out_shape = [_sds((H, n, wd, tq), F32), _sds((H, T, wd), F32), _sds((H, T, wd), F32), _sds((H, n, tq), F32)]
    scratch = [pltpu.VMEM((tq, 1), F32)] + [pltpu.VMEM((tq, tq), F32)] * 4
    operands = (q_aug, do_aug, k_aug, v_aug)
    if pair is None:
        body = compute
    else:
        def body(q_ref, do_ref, k_ref, v_ref, pair_ref, dq_ref, dk_ref, dv_ref, dc_ref, got_ref,
                 dck, s_a, d_a, s_b, d_b, send_sems, recv_sems):
            h, j = pl.program_id(0), pl.program_id(1)

            @pl.when((h == 0) & (j == 0))
            def _():
                for cp in _chip_copies(pair_ref, got_ref, send_sems, recv_sems):
                    cp.start()

            compute(q_ref, do_ref, k_ref, v_ref, dq_ref, dk_ref, dv_ref, dc_ref, dck, s_a, d_a, s_b, d_b)

            @pl.when((h == H - 1) & (j == n - 1))
            def _():
                for cp in _chip_copies(pair_ref, got_ref, send_sems, recv_sems):
                    cp.wait()

        sems = pltpu.SemaphoreType.DMA((3 * len(_spans(pair.shape[1], COPY_CHUNKS)),))
        in_specs, out_specs = in_specs + [_HBM], out_specs + [_HBM]
        out_shape = out_shape + [_sds((3,) + pair.shape[1:], pair.dtype)]
        scratch, operands = scratch + [sems, sems], operands + (pair,)
    return pl.pallas_call(
        body, name="attn_bwd", grid=(H, n), in_specs=in_specs, out_specs=out_specs, out_shape=out_shape,
        scratch_shapes=scratch, compiler_params=_params(("arbitrary", "arbitrary")),
    )(*operands)


def _complex_step(a_r, a_i, cr, ci, br, bi):
    return a_r * cr - a_i * ci + br, a_r * ci + a_i * cr + bi


_HALF_CH = SSM_W // 2
_HALF_ST = STATE_W // 2


def _state_cols(part, half):
    lo = part * STATE_W + half * _HALF_ST
    return slice(lo, lo + _HALF_ST)


def _channels_to_states(x, w_ref, out_ref):
    for half in range(2):
        ch = slice(half * _HALF_CH, (half + 1) * _HALF_CH)
        for part in range(2):
            cols = _state_cols(part, half)
            out_ref[:, cols] = _dot(x[:, ch], w_ref[ch, cols])


def _states_to_channels(x, w_ref):
    halves = []
    for half in range(2):
        ch = slice(half * _HALF_CH, (half + 1) * _HALF_CH)
        halves.append(_dot(x[:, _state_cols(0, half)], w_ref[_state_cols(0, half), ch])
                      + _dot(x[:, _state_cols(1, half)], w_ref[_state_cols(1, half), ch]))
    return jnp.concatenate(halves, axis=1)


def ssm_fwd(s_perm, wb, cbd, a_r, a_i, al_r, al_i, dvec):
    T = s_perm.shape[0]
    chunk = T // 8
    ts = _tile(chunk, SCAN_STEPS)
    tr, n_s = ts * 8, chunk // ts
    W, LB = STATE_W, SCAN_LANES

    def body(s_ref, wb_ref, cbd_ref, ar_ref, ai_ref, alr_ref, ali_ref, dv_ref, y_ref, xs_ref, bu, carry):
        ph, i = pl.program_id(0), pl.program_id(1)

        @pl.when((ph == 0) & (i == 0))
        def _():
            carry[...] = jnp.zeros_like(carry)

        _channels_to_states(s_ref[...].astype(BF16), wb_ref, bu)

        def scan(store):
            for lb in range(W // LB):
                lo = lb * LB
                re, im = slice(lo, lo + LB), slice(W + lo, W + lo + LB)
                ar = jnp.broadcast_to(ar_ref[:, re], (8, LB))
                ai = jnp.broadcast_to(ai_ref[:, re], (8, LB))

                def step(s, c):
                    rows = pl.ds(pl.multiple_of(s * 8, 8), 8)
                    nr, ni = _complex_step(ar, ai, c[0], c[1], bu[rows, re], bu[rows, im])
                    if store:
                        bu[rows, re] = nr
                        bu[rows, im] = ni
                    return nr, ni

                cr, ci = lax.fori_loop(0, ts, step, (carry[:, re], carry[:, im]), unroll=2)
                carry[:, re] = cr
                carry[:, im] = ci

        @pl.when(ph == 0)
        def _():
            scan(False)

            @pl.when(i == n_s - 1)
            def _():
                er, ei = carry[:, :W], carry[:, W:]
                alr = jnp.broadcast_to(alr_ref[...], (8, W))
                ali = jnp.broadcast_to(ali_ref[...], (8, W))
                first = lax.broadcasted_iota(jnp.int32, (8, W), 0) == 0
                sr, si = jnp.zeros((8, W), F32), jnp.zeros((8, W), F32)
                for _ in range(7):
                    vr, vi = _complex_step(alr, ali, sr, si, er, ei)
                    sr = jnp.where(first, 0.0, pltpu.roll(vr, 1, 0))
                    si = jnp.where(first, 0.0, pltpu.roll(vi, 1, 0))
                carry[:, :W] = sr
                carry[:, W:] = si

        @pl.when(ph == 1)
        def _():
            scan(True)
            xb = bu[...].astype(BF16)
            xs_ref[...] = xb
            y_ref[...] = _states_to_channels(xb, cbd_ref) + s_ref[...] * dv_ref[...]

    fix = lambda p, i: (0, 0)
    return pl.pallas_call(
        body, name="ssm_fwd", grid=(2, n_s),
        in_specs=[pl.BlockSpec((tr, SSM_W), lambda p, i: (i, 0)), pl.BlockSpec((SSM_W, 2 * W), fix),
                  pl.BlockSpec((2 * W, SSM_W), fix), pl.BlockSpec((1, W), fix), pl.BlockSpec((1, W), fix),
                  pl.BlockSpec((1, W), fix), pl.BlockSpec((1, W), fix), pl.BlockSpec((1, SSM_W), fix)],
        out_specs=[pl.BlockSpec((tr, SSM_W), lambda p, i: (i * p, 0)), pl.BlockSpec((tr, 2 * W), lambda p, i: (i * p, 0))],
        out_shape=[_sds((T, SSM_W), F32), _sds((T, 2 * W), BF16)],
        scratch_shapes=[pltpu.VMEM((tr, 2 * W), F32), pltpu.VMEM((8, 2 * W), F32)],
        compiler_params=_params(("arbitrary", "arbitrary")),
    )(s_perm, wb, cbd, a_r, a_i, al_r, al_i, dvec)


def ssm_bwd(dy_perm, s_perm, xs, cbd_t, wb_t, a_r, a_i, al_r, al_i, dvec):
    T = s_perm.shape[0]
    chunk = T // 8
    ts = _tile(chunk, SCAN_STEPS)
    tr, n_s = ts * 8, chunk // ts
    W, LB = STATE_W, SCAN_LANES

    def body(dy_ref, s_ref, xs_ref, cbt_ref, wbt_ref, ar_ref, ai_ref, alr_ref, ali_ref, dv_ref,
             du_ref, gs_ref, da_ref, dd_ref, gd, x32, carry):
        ph, i = pl.program_id(0), pl.program_id(1)

        @pl.when((ph == 0) & (i == 0))
        def _():
            carry[...] = jnp.zeros_like(carry)
            da_ref[...] = jnp.zeros_like(da_ref)
            dd_ref[...] = jnp.zeros_like(dd_ref)

        _channels_to_states(dy_ref[...].astype(BF16), cbt_ref, gd)

        def scan(store):
            for lb in range(W // LB):
                lo = lb * LB
                re, im = slice(lo, lo + LB), slice(W + lo, W + lo + LB)
                ar = jnp.broadcast_to(ar_ref[:, re], (8, LB))
                nai = -jnp.broadcast_to(ai_ref[:, re], (8, LB))

                def step(k, c):
                    rows = pl.ds(pl.multiple_of((ts - 1 - k) * 8, 8), 8)
                    cr, ci = c[0], c[1]
                    nr, ni = _complex_step(ar, nai, cr, ci, gd[rows, re], gd[rows, im])
                    if store:
                        xr, xi = x32[rows, re], x32[rows, im]
                        gd[rows, re] = nr
                        gd[rows, im] = ni
                        return nr, ni, c[2] + cr * xr + ci * xi, c[3] + ci * xr - cr * xi
                    return nr, ni

                init = (carry[:, re], carry[:, im])
                if store:
                    init = init + (da_ref[:, re], da_ref[:, im])
                out = lax.fori_loop(0, ts, step, init, unroll=2)
                carry[:, re] = out[0]
                carry[:, im] = out[1]
                if store:
                    da_ref[:, re] = out[2]
                    da_ref[:, im] = out[3]

        @pl.when(ph == 0)
        def _():
            scan(False)

            @pl.when(i == n_s - 1)
            def _():
                er, ei = carry[:, :W], carry[:, W:]
                alr = jnp.broadcast_to(alr_ref[...], (8, W))
                nali = -jnp.broadcast_to(ali_ref[...], (8, W))
                last = lax.broadcasted_iota(jnp.int32, (8, W), 0) == 7
                rr, ri = jnp.zeros((8, W), F32), jnp.zeros((8, W), F32)
                for _ in range(7):
                    vr, vi = _complex_step(alr, nali, rr, ri, er, ei)
                    rr = jnp.where(last, 0.0, pltpu.roll(vr, 7, 0))
                    ri = jnp.where(last, 0.0, pltpu.roll(vi, 7, 0))
                carry[:, :W] = rr
                carry[:, W:] = ri

        @pl.when(ph == 1)
        def _():
            x32[...] = xs_ref[...].astype(F32)
            scan(True)
            gb = gd[...].astype(BF16)
            gs_ref[...] = gb
            dy = dy_ref[...]
            du_ref[...] = _states_to_channels(gb, wbt_ref) + dy * dv_ref[...]
            dd_ref[...] += jnp.sum(dy * s_ref[...], axis=0, keepdims=True)

    fix = lambda p, i: (0, 0)
    rev = lambda p, i: (n_s - 1 - i, 0)
    rev_out = lambda p, i: (n_s - 1 - i * p, 0)
    return pl.pallas_call(
        body, name="ssm_bwd", grid=(2, n_s),
        in_specs=[pl.BlockSpec((tr, SSM_W), rev), pl.BlockSpec((tr, SSM_W), rev), pl.BlockSpec((tr, 2 * W), rev),
                  pl.BlockSpec((SSM_W, 2 * W), fix), pl.BlockSpec((2 * W, SSM_W), fix), pl.BlockSpec((1, W), fix),
                  pl.BlockSpec((1, W), fix), pl.BlockSpec((1, W), fix), pl.BlockSpec((1, W), fix),
                  pl.BlockSpec((1, SSM_W), fix)],
        out_specs=[pl.BlockSpec((tr, SSM_W), rev_out), pl.BlockSpec((tr, 2 * W), rev_out),
                   pl.BlockSpec((8, 2 * W), fix), pl.BlockSpec((1, SSM_W), fix)],
        out_shape=[_sds((T, SSM_W), F32), _sds((T, 2 * W), BF16), _sds((8, 2 * W), F32), _sds((1, SSM_W), F32)],
        scratch_shapes=[pltpu.VMEM((tr, 2 * W), F32), pltpu.VMEM((tr, 2 * W), F32), pltpu.VMEM((8, 2 * W), F32)],
        compiler_params=_params(("arbitrary", "arbitrary")),
    )(dy_perm, s_perm, xs, cbd_t, wb_t, a_r, a_i, al_r, al_i, dvec)


def _join_heads(ref, dtype, transposed=False):
    def move(h, dst):
        x = ref[h, 0] if transposed else ref[h]
        pieces = _pieces(x) if dtype == F32 else (x.astype(BF16),)
        out = None
        for piece in pieces:
            place = _lane_move(0, dst, HEAD_DIM, BF16)
            term = _dot_tn(piece, place) if transposed else _dot(piece, place)
            out = term if out is None else out + term
        return out

    return jnp.concatenate([move(2 * p, 0) + move(2 * p + 1, HEAD_DIM) for p in range(N_HEADS // 2)], axis=1)


def mixout_fwd(h1, o_heads, ypre, g_a, g_s, w_glu, b_glu, w_out):
    T = h1.shape[0]
    tm = _tile(T, TOKEN_TILE)

    def body(h_ref, at_ref, yp_ref, ga_ref, gs_ref, wg_ref, bg_ref, wo_ref, h2_ref, mixed_ref):
        yg, _ = _gelu_parts(yp_ref[...])
        gl = yg * jax.nn.sigmoid(_dot(yg.astype(BF16), wg_ref[...]) + bg_ref[...])
        at = _join_heads(at_ref, F32)
        mixed = jnp.concatenate([at * _rms_scale(at) * ga_ref[...], gl * _rms_scale(gl) * gs_ref[...]], axis=1)
        mixed = mixed.astype(BF16)
        mixed_ref[...] = mixed
        h2_ref[...] = h_ref[...] + _dot(mixed, wo_ref[...])

    tok = lambda i: (i, 0)
    fix = lambda i: (0, 0)
    return pl.pallas_call(
        body, name="mixout_fwd", grid=(T // tm,),
        in_specs=[pl.BlockSpec((tm, D_MODEL), tok), pl.BlockSpec((N_HEADS, tm, 128), lambda i: (0, i, 0)),
                  pl.BlockSpec((tm, SSM_W), tok),
                  pl.BlockSpec((1, ATTN_W), fix), pl.BlockSpec((1, SSM_W), fix), pl.BlockSpec((SSM_W, SSM_W), fix),
                  pl.BlockSpec((1, SSM_W), fix), pl.BlockSpec((D_MODEL, D_MODEL), fix)],
        out_specs=[pl.BlockSpec((tm, D_MODEL), tok), pl.BlockSpec((tm, D_MODEL), tok)],
        out_shape=[_sds((T, D_MODEL), F32), _sds((T, D_MODEL), BF16)],
        compiler_params=_params(("arbitrary",)),
    )(h1, o_heads, ypre, g_a, g_s, w_glu, b_glu, w_out)


def mixout_bwd(dh2, o_heads, ypre, g_a, g_s, w_glu, b_glu, w_out, seg):
    T = dh2.shape[0]
    tm = _tile(T, TOKEN_TILE)

    def body(dh_ref, at_ref, yp_ref, ga_ref, gs_ref, wg_ref, bg_ref, wo_ref, seg_ref,
             do_ref, dyp_ref, dpre_ref, yg_ref, dga_ref, dgs_ref, dbg_ref):
        @pl.when(pl.program_id(0) == 0)
        def _():
            dga_ref[...] = jnp.zeros_like(dga_ref)
            dgs_ref[...] = jnp.zeros_like(dgs_ref)
            dbg_ref[...] = jnp.zeros_like(dbg_ref)

        dmix = _dot_nt(dh_ref[...].astype(BF16), wo_ref[...])
        at = _join_heads(at_ref, F32)
        dat, dga = _rms_bwd(dmix[:, :ATTN_W], at, ga_ref[...])
        dga_ref[...] += dga
        delta = _pieces(jnp.dot(dat * at, seg_ref[...], precision=HIGHEST, preferred_element_type=F32))
        datb = dat.astype(BF16)
        for h in range(N_HEADS):
            p, e = divmod(h, 2)
            do_ref[h] = (_head_features(datb[:, 128 * p:128 * (p + 1)], e) + _helper_columns(delta, h, -1.0)).astype(BF16)
        yp = yp_ref[...]
        yg, t = _gelu_parts(yp)
        ygb = yg.astype(BF16)
        yg_ref[...] = ygb
        sg = jax.nn.sigmoid(_dot(ygb, wg_ref[...]) + bg_ref[...])
        dgl, dgs = _rms_bwd(dmix[:, ATTN_W:], yg * sg, gs_ref[...])
        dgs_ref[...] += dgs
        dpre = dgl * yg * sg * (1.0 - sg)
        dbg_ref[...] += jnp.sum(dpre, axis=0, keepdims=True)
        dpb = dpre.astype(BF16)
        dpre_ref[...] = dpb
        dyg = dgl * sg + _dot_nt(dpb, wg_ref[...])
        dyp_ref[...] = dyg * _gelu_grad(yp, t)

    tok = lambda i: (i, 0)
    fix = lambda i: (0, 0)
    heads = pl.BlockSpec((N_HEADS, tm, 128), lambda i: (0, i, 0))
    return pl.pallas_call(
        body, name="mixout_bwd", grid=(T // tm,),
        in_specs=[pl.BlockSpec((tm, D_MODEL), tok), heads, pl.BlockSpec((tm, SSM_W), tok),
                  pl.BlockSpec((1, ATTN_W), fix), pl.BlockSpec((1, SSM_W), fix), pl.BlockSpec((SSM_W, SSM_W), fix),
                  pl.BlockSpec((1, SSM_W), fix), pl.BlockSpec((D_MODEL, D_MODEL), fix), pl.BlockSpec((ATTN_W, 128), fix)],
        out_specs=[heads, pl.BlockSpec((tm, SSM_W), tok), pl.BlockSpec((tm, SSM_W), tok),
                   pl.BlockSpec((tm, SSM_W), tok), pl.BlockSpec((1, ATTN_W), fix),
                   pl.BlockSpec((1, SSM_W), fix), pl.BlockSpec((1, SSM_W), fix)],
        out_shape=[_sds((N_HEADS, T, 128), BF16), _sds((T, SSM_W), F32), _sds((T, SSM_W), BF16), _sds((T, SSM_W), BF16),
                   _sds((1, ATTN_W), F32), _sds((1, SSM_W), F32), _sds((1, SSM_W), F32)],
        compiler_params=_params(("arbitrary",)),
    )(dh2, o_heads, ypre, g_a, g_s, w_glu, b_glu, w_out, seg)


def head_fwd_bwd(h3, p, target, g_ple, g_final, w_gate, w_proj):
    T = h3.shape[0]
    tm = _tile(T, TOKEN_TILE)
    pd = p.shape[1]

    def body(h_ref, p_ref, tg_ref, gp_ref, gf_ref, wg_ref, wp_ref,
             dh_ref, n3_ref, dz_ref, dpp_ref, loss_ref, dgp_ref, dgf_ref):
        @pl.when(pl.program_id(0) == 0)
        def _():
            loss_ref[...] = jnp.zeros_like(loss_ref)
            dgp_ref[...] = jnp.zeros_like(dgp_ref)
            dgf_ref[...] = jnp.zeros_like(dgf_ref)

        x = h_ref[...]
        gp, gf = gp_ref[...], gf_ref[...]
        n3 = (x * _rms_scale(x) * gp).astype(BF16)
        n3_ref[...] = n3
        gate = jax.nn.sigmoid(_dot(n3, wg_ref[...]))
        pp = _dot(p_ref[...].astype(BF16), wp_ref[...])
        h4 = x + gate * pp
        y = h4 * _rms_scale(h4) * gf
        e = y - tg_ref[...]
        tile_loss = jnp.sum(jnp.sum(e * e, axis=1, keepdims=True), axis=0, keepdims=True) * (0.5 / D_MODEL)
        loss_ref[...] += jnp.broadcast_to(tile_loss, loss_ref.shape)
        dh4, dgf = _rms_bwd(e * (1.0 / D_MODEL), h4, gf)
        dgf_ref[...] += dgf
        dzg = dh4 * pp * gate * (1.0 - gate)
        dzb = dzg.astype(BF16)
        dz_ref[...] = dzb
        dpp_ref[...] = (dh4 * gate).astype(BF16)
        dx, dgp = _rms_bwd(_dot_nt(dzb, wg_ref[...]), x, gp)
        dgp_ref[...] += dgp
        dh_ref[...] = dh4 + dx

    tok = lambda i: (i, 0)
    fix = lambda i: (0, 0)
    return pl.pallas_call(
        body, name="head_fwd_bwd", grid=(T // tm,),
        in_specs=[pl.BlockSpec((tm, D_MODEL), tok), pl.BlockSpec((tm, pd), tok), pl.BlockSpec((tm, D_MODEL), tok),
                  pl.BlockSpec((1, D_MODEL), fix), pl.BlockSpec((1, D_MODEL), fix), pl.BlockSpec((D_MODEL, D_MODEL), fix),
                  pl.BlockSpec((pd, D_MODEL), fix)],
        out_specs=[pl.BlockSpec((tm, D_MODEL), tok), pl.BlockSpec((tm, D_MODEL), tok), pl.BlockSpec((tm, D_MODEL), tok),
                   pl.BlockSpec((tm, D_MODEL), tok), pl.BlockSpec((8, 128), fix), pl.BlockSpec((1, D_MODEL), fix),
                   pl.BlockSpec((1, D_MODEL), fix)],
        out_shape=[_sds((T, D_MODEL), F32), _sds((T, D_MODEL), BF16), _sds((T, D_MODEL), BF16), _sds((T, D_MODEL), BF16),
                   _sds((8, 128), F32), _sds((1, D_MODEL), F32), _sds((1, D_MODEL), F32)],
        compiler_params=_params(("arbitrary",)),
    )(h3, p, target, g_ple, g_final, w_gate, w_proj)


def _row_tile(rows, cols, n_arrays):
    lanes = -(-cols // 128) * 128
    cap = VMEM_LIMIT // 3 // (2 * n_arrays * lanes * 4)
    best = None
    for t in range(PACK_ALIGN, min(rows, cap) + 1, PACK_ALIGN):
        if rows % t == 0:
            best = t
    assert best is not None, (rows, cols)
    return best


def _adamw_math(w, g, m, v):
    nm = ADAM_B1 * m + (1.0 - ADAM_B1) * g
    nv = ADAM_B2 * v + (1.0 - ADAM_B2) * (g * g)
    c1 = 1.0 - ADAM_B1 ** ADAM_STEP
    c2 = 1.0 - ADAM_B2 ** ADAM_STEP
    return -ADAM_LR * ((nm / c1) / (jnp.sqrt(nv / c2) + ADAM_EPS) + ADAM_WD * w), nm, nv


def adamw(w, g, m, v, name):
    R, C = w.shape
    tr = _row_tile(R, C, 7)

    def body(w_ref, g_ref, m_ref, v_ref, d_ref, nm_ref, nv_ref):
        d_ref[...], nm_ref[...], nv_ref[...] = _adamw_math(w_ref[...], g_ref[...], m_ref[...], v_ref[...])

    spec = pl.BlockSpec((tr, C), lambda i: (i, 0))
    return pl.pallas_call(
        body, name=name, grid=(R // tr,), in_specs=[spec] * 4, out_specs=[spec] * 3,
        out_shape=[_sds((R, C), F32)] * 3, compiler_params=_params(("arbitrary",)),
    )(w, g, m, v)


def join_halves(mine, other, core):
    rh, C = mine.shape
    tr = _row_tile(rh, C, 3)
    nb = rh // tr

    def body(c_ref, m_ref, o_ref, out_ref):
        out_ref[...] = jnp.where((pl.program_id(0) // nb) == c_ref[0], m_ref[...], o_ref[...])

    half = pl.BlockSpec((tr, C), lambda i, c: (i % nb, 0))
    return pl.pallas_call(
        body, name="join_halves",
        grid_spec=pltpu.PrefetchScalarGridSpec(num_scalar_prefetch=1, grid=(2 * nb,), in_specs=[half, half],
                                               out_specs=pl.BlockSpec((tr, C), lambda i, c: (i, 0))),
        out_shape=_sds((2 * rh, C), F32), compiler_params=_params(("arbitrary",)),
    )(core, mine, other)


def pair_sum(g, theirs, core):
    n, R, C = g.shape
    rh = R // 2
    tr = _row_tile(rh, C, 3)
    nb = rh // tr

    def body(c_ref, g_ref, t_ref, o_ref):
        o_ref[...] = (g_ref[...] + t_ref[...]).astype(BF16)

    here = pl.BlockSpec((1, tr, C), lambda j, i, c: (j, i, 0))
    return pl.pallas_call(
        body, name="pair_sum",
        grid_spec=pltpu.PrefetchScalarGridSpec(
            num_scalar_prefetch=1, grid=(n, nb),
            in_specs=[pl.BlockSpec((1, tr, C), lambda j, i, c: (j, c[0] * nb + i, 0)), here], out_specs=here),
        out_shape=_sds((n, rh, C), BF16), compiler_params=_params(("arbitrary", "arbitrary")),
    )(core, g, theirs)


def chip_sum(pair, got, chip):
    _, R, C = pair.shape
    tr = _row_tile(R, C, 5)

    def body(c_ref, p_ref, g0_ref, g1_ref, g2_ref, o_ref):
        f = lambda ref: ref[0].astype(F32)
        o_ref[...] = ((f(p_ref) + f(g0_ref)) + f(g1_ref)) + f(g2_ref)

    slot = lambda k: pl.BlockSpec((1, tr, C), lambda i, c: (k, i, 0))
    return pl.pallas_call(
        body, name="chip_sum",
        grid_spec=pltpu.PrefetchScalarGridSpec(
            num_scalar_prefetch=1, grid=(R // tr,),
            in_specs=[pl.BlockSpec((1, tr, C), lambda i, c: (c[0], i, 0)), slot(0), slot(1), slot(2)],
            out_specs=pl.BlockSpec((tr, C), lambda i, c: (i, 0))),
        out_shape=_sds((R, C), F32), compiler_params=_params(("arbitrary",)),
    )(chip, pair, got, got, got)


_HBM = pl.BlockSpec(memory_space=pltpu.HBM)


def _place():
    x, y, c = lax.axis_index("x"), lax.axis_index("y"), lax.axis_index("c")
    return x, y, c, [(1 - x, y), (x, 1 - y), (1 - x, 1 - y)]


def _spans(rows, n):
    assert rows % PACK_ALIGN == 0
    tiles = rows // PACK_ALIGN
    n = min(n, tiles)
    cuts = [tiles * q // n for q in range(n + 1)]
    return [(cuts[q] * PACK_ALIGN, (cuts[q + 1] - cuts[q]) * PACK_ALIGN) for q in range(n)]


def _remote(src, dst, send_sem, recv_sem, to):
    return pltpu.make_async_remote_copy(src_ref=src, dst_ref=dst, send_sem=send_sem, recv_sem=recv_sem,
                                        device_id=to, device_id_type=MESH)


GATHER_SEMS = 6


def _gather_scratch(rows):
    ici = pltpu.SemaphoreType.DMA((3 * len(_spans(rows // 2, COPY_CHUNKS)),))
    own = pltpu.SemaphoreType.DMA((len(_spans(rows, 2 * COPY_CHUNKS)),))
    return [ici, ici, own, own, ici, ici]


def _gather_start(w_ref, out_ref, ici_send, ici_recv, own_send, own_recv, pass_send, pass_recv):
    R = w_ref.shape[0]
    rh = R // 2
    spans = _spans(rh, COPY_CHUNKS)
    x, y, c, chips = _place()
    me = 2 * x + y
    for k, (cx, cy) in enumerate(chips):
        for q, (o, n) in enumerate(spans):
            rows = pl.ds(c * rh + o, n)
            _remote(w_ref.at[rows, :], out_ref.at[me, rows, :], ici_send.at[k * len(spans) + q],
                    ici_recv.at[k * len(spans) + q], (cx, cy, c)).start()
    for q, (o, n) in enumerate(_spans(R, 2 * COPY_CHUNKS)):
        rows = pl.ds(o, n)
        _remote(w_ref.at[rows, :], out_ref.at[me, rows, :], own_send.at[q], own_recv.at[q], (x, y, 1 - c)).start()


def _gather_finish(w_ref, out_ref, ici_send, ici_recv, own_send, own_recv, pass_send, pass_recv):
    R = w_ref.shape[0]
    rh = R // 2
    spans = _spans(rh, COPY_CHUNKS)
    n_sp = len(spans)
    x, y, c, chips = _place()
    me = 2 * x + y
    sibling = (x, y, 1 - c)
    passed = []
    for q, (o, n) in enumerate(spans):
        for k, (cx, cy) in enumerate(chips):
            blk = out_ref.at[2 * cx + cy, pl.ds(c * rh + o, n), :]
            _remote(blk, blk, ici_send.at[k * n_sp + q], ici_recv.at[k * n_sp + q], (cx, cy, c)).wait_recv()
            cp = _remote(blk, blk, pass_send.at[k * n_sp + q], pass_recv.at[k * n_sp + q], sibling)
            cp.start()
            passed.append(cp)
    for k, (cx, cy) in enumerate(chips):
        for q, (o, n) in enumerate(spans):
            blk = out_ref.at[2 * cx + cy, pl.ds((1 - c) * rh + o, n), :]
            _remote(blk, blk, pass_send.at[k * n_sp + q], pass_recv.at[k * n_sp + q], sibling).wait_recv()
            rows = pl.ds(c * rh + o, n)
            _remote(w_ref.at[rows, :], out_ref.at[me, rows, :], ici_send.at[k * n_sp + q], ici_recv.at[k * n_sp + q],
                    (cx, cy, c)).wait_send()
    for q, (o, n) in enumerate(_spans(R, 2 * COPY_CHUNKS)):
        rows = pl.ds(o, n)
        _remote(w_ref.at[rows, :], out_ref.at[me, rows, :], own_send.at[q], own_recv.at[q], sibling).wait()
    for cp in passed:
        cp.wait_send()


def gather_shards(wp):
    R, C = wp.shape

    def body(w_ref, out_ref, *sems):
        _gather_start(w_ref, out_ref, *sems)
        _gather_finish(w_ref, out_ref, *sems)

    return pl.pallas_call(
        body, name="gather_shards", in_specs=[_HBM], out_specs=_HBM, out_shape=_sds((4, R, C), wp.dtype),
        scratch_shapes=_gather_scratch(R),
    )(wp)


def sibling_split(g):
    n_sl, R, C = g.shape
    rh = R // 2
    spans = _spans(rh, COPY_CHUNKS)
    n_sp = len(spans)

    def body(g_ref, got_ref, send_sems, recv_sems):
        x, y, c, _ = _place()
        copies = []
        for j in range(n_sl):
            for q, (o, n) in enumerate(spans):
                cp = _remote(g_ref.at[j, pl.ds((1 - c) * rh + o, n), :], got_ref.at[j, pl.ds(o, n), :],
                             send_sems.at[j * n_sp + q], recv_sems.at[j * n_sp + q], (x, y, 1 - c))
                cp.start()
                copies.append(cp)
        for cp in copies:
            cp.wait()

    sems = pltpu.SemaphoreType.DMA((n_sl * n_sp,))
    return pl.pallas_call(
        body, name="sibling_split", in_specs=[_HBM], out_specs=_HBM, out_shape=_sds((n_sl, rh, C), g.dtype),
        scratch_shapes=[sems, sems],
    )(g)


def _chip_copies(p_ref, buf_ref, send_sems, recv_sems):
    rows = p_ref.shape[1]
    spans = _spans(rows, COPY_CHUNKS)
    x, y, c, chips = _place()
    copies = []
    for k, (cx, cy) in enumerate(chips):
        for q, (o, n) in enumerate(spans):
            copies.append(_remote(p_ref.at[2 * cx + cy, pl.ds(o, n), :], buf_ref.at[k, pl.ds(o, n), :],
                                  send_sems.at[k * len(spans) + q], recv_sems.at[k * len(spans) + q], (cx, cy, c)))
    return copies


def chip_exchange(p):
    _, R, C = p.shape

    def body(p_ref, buf_ref, send_sems, recv_sems):
        copies = _chip_copies(p_ref, buf_ref, send_sems, recv_sems)
        for cp in copies:
            cp.start()
        for cp in copies:
            cp.wait()

    sems = pltpu.SemaphoreType.DMA((3 * len(_spans(R, COPY_CHUNKS)),))
    return pl.pallas_call(
        body, name="chip_exchange", in_specs=[_HBM], out_specs=_HBM, out_shape=_sds((3, R, C), p.dtype),
        scratch_shapes=[sems, sems],
    )(p)


def sibling_swap(half):
    R, C = half.shape
    spans = _spans(R, COPY_CHUNKS)

    def body(h_ref, got_ref, send_sems, recv_sems):
        x, y, c, _ = _place()
        copies = []
        for q, (o, n) in enumerate(spans):
            cp = _remote(h_ref.at[pl.ds(o, n), :], got_ref.at[pl.ds(o, n), :], send_sems.at[q], recv_sems.at[q], (x, y, 1 - c))
            cp.start()
            copies.append(cp)
        for cp in copies:
            cp.wait()

    sems = pltpu.SemaphoreType.DMA((len(spans),))
    return pl.pallas_call(
        body, name="sibling_swap", in_specs=[_HBM], out_specs=_HBM, out_shape=_sds((R, C), half.dtype),
        scratch_shapes=[sems, sems],
    )(half)


def allreduce_small(v):
    R, C = v.shape

    def body(v_ref, out_ref, buf, send_sems, recv_sems):
        x, y, c, _ = _place()
        me = 4 * x + 2 * y + c
        buf[me] = v_ref[...]
        flips = [((k >> 2) & 1, (k >> 1) & 1, k & 1) for k in range(1, 8)]
        sends = []
        for k, (fx, fy, fc) in enumerate(flips):
            to = (1 - x if fx else x, 1 - y if fy else y, 1 - c if fc else c)
            cp = _remote(v_ref, buf.at[me], send_sems.at[k], recv_sems.at[k], to)
            cp.start()
            sends.append(cp)
        for k, (fx, fy, fc) in enumerate(flips):
            px, py, pc = (1 - x if fx else x, 1 - y if fy else y, 1 - c if fc else c)
            blk = buf.at[4 * px + 2 * py + pc]
            _remote(blk, blk, send_sems.at[k], recv_sems.at[k], (px, py, pc)).wait_recv()
        for cp in sends:
            cp.wait_send()
        acc = buf[0]
        for s in range(1, 8):
            acc = acc + buf[s]
        out_ref[...] = acc

    vm = pl.BlockSpec(memory_space=pltpu.VMEM)
    return pl.pallas_call(
        body, name="allreduce_small", in_specs=[vm], out_specs=vm, out_shape=_sds((R, C), F32),
        scratch_shapes=[pltpu.VMEM((8, R, C), F32), pltpu.SemaphoreType.DMA((7,)), pltpu.SemaphoreType.DMA((7,))],
        compiler_params=pltpu.CompilerParams(vmem_limit_bytes=VMEM_LIMIT),
    )(v)


def _rows_of(shape):
    return shape[0] * shape[1] // PACK_COLS


def _slot_rows(shape):
    return -(-_rows_of(shape) // PACK_ALIGN) * PACK_ALIGN


TRANSPOSED = ("w1_a", "w3_a", "w1_b", "w3_b")


def _stored(name, shard):
    return shard[0].T if name in TRANSPOSED else shard[0]


def _restored(name, stored):
    return stored.T[None] if name in TRANSPOSED else stored[None]


def _pack_shards(shards, dtype, entries):
    parts = []
    for name, shape, _ in entries:
        part = _stored(name, shards[name]).reshape(_rows_of(shape), PACK_COLS).astype(dtype)
        parts.append(jnp.pad(part, ((0, _slot_rows(shape) - part.shape[0]), (0, 0))))
    used = sum(p.shape[0] for p in parts)
    parts.append(jnp.zeros((_group_rows(entries) - used, PACK_COLS), dtype))
    return jnp.concatenate(parts, axis=0)


def _unpack_gathered(ag, entries):
    out, off = {}, 0
    for name, shape, axis in entries:
        r = _rows_of(shape)
        piece = ag[:, off:off + r, :]
        off += _slot_rows(shape)
        if name in TRANSPOSED:
            out[name] = piece.reshape(4 * r, PACK_COLS)
        elif axis == 0:
            out[name] = piece.reshape(4 * shape[0], shape[1])
        else:
            out[name] = piece.reshape((4,) + shape).transpose(1, 0, 2).reshape(shape[0], 4 * shape[1])
    return out


LATE = ("w_glu", "w_out", "w1_b", "w3_b", "w2_b", "w_ple_gate", "w_ple_proj")
GRAD_GROUPS = (tuple(e for e in BIG if e[0] in LATE), tuple(e for e in BIG if e[0] not in LATE))


GROUP_ROW_UNIT = 2816


def _group_rows(entries):
    used = sum(_slot_rows(shape) for _, shape, _ in entries)
    return -(-used // GROUP_ROW_UNIT) * GROUP_ROW_UNIT


def _pack_full_grads(grads, entries):
    parts = []
    for name, shape, axis in entries:
        g = grads[name]
        if name in TRANSPOSED or axis == 0:
            piece = g.reshape(4, _rows_of(shape), PACK_COLS)
        else:
            piece = g.reshape(shape[0], 4, shape[1]).transpose(1, 0, 2).reshape(4, _rows_of(shape), PACK_COLS)
        parts.append(jnp.pad(piece, ((0, 0), (0, _slot_rows(shape) - piece.shape[1]), (0, 0))))
    used = sum(p.shape[1] for p in parts)
    if _group_rows(entries) > used:
        parts.append(jnp.zeros((4, _group_rows(entries) - used, PACK_COLS), F32))
    return jnp.concatenate(parts, axis=1)


def _unpack_shards(packed, entries):
    out, off = {}, 0
    for name, shape, _ in entries:
        r = _rows_of(shape)
        out[name] = packed[off:off + r] if name in TRANSPOSED else packed[off:off + r].reshape(shape)
        off += _slot_rows(shape)
    return out


def _small_rows(shape):
    return -(-math.prod(shape) // 1024) * 8


def _pack_small(vals, extra=None):
    def slot(v, rows):
        flat = v.reshape(-1)
        return jnp.pad(flat, (0, rows * 128 - flat.shape[0])).reshape(rows, 128)

    parts = [slot(vals[name], _small_rows(shape)) for name, shape in SMALL]
    parts.append(slot(extra if extra is not None else jnp.zeros((1,), F32), 8))
    assert sum(p.shape[0] for p in parts) == SMALL_ROWS
    return jnp.concatenate(parts, axis=0)


def _unpack_small(packed):
    out, off = {}, 0
    for name, shape in SMALL:
        rows = _small_rows(shape)
        out[name] = packed[off:off + rows].reshape(-1)[:math.prod(shape)].reshape(shape)
        off += rows
    return out, packed[off, 0]


def _permute_time(a):
    T, n = a.shape
    return a.reshape(8, T // 8, n).transpose(1, 0, 2).reshape(T, n)


def _unpermute_time(a):
    T, n = a.shape
    return a.reshape(T // 8, 8, n).transpose(1, 0, 2).reshape(T, n)


def _discretize(a_re, a_im, log_dt, b_re, b_im):
    dt = jnp.exp(log_dt)[:, None]
    decay = jnp.exp(dt * a_re)
    abar_r = decay * jnp.cos(dt * a_im)
    abar_i = decay * jnp.sin(dt * a_im)
    nr, ni = abar_r - 1.0, abar_i
    den = a_re * a_re + a_im * a_im
    fr = (nr * a_re + ni * a_im) / den
    fi = (ni * a_re - nr * a_im) / den
    bbar_r = fr[..., None] * b_re - fi[..., None] * b_im
    bbar_i = fr[..., None] * b_im + fi[..., None] * b_re
    return abar_r, abar_i, bbar_r, bbar_i


def _input_matrix(bbar_r, bbar_i):
    eye = jnp.eye(N_GROUPS, dtype=F32)
    blk = lambda b: jnp.einsum("ghp,gk->ghkp", b.transpose(0, 2, 1), eye).reshape(SSM_W, STATE_W)
    return jnp.concatenate([blk(bbar_r), blk(bbar_i)], axis=1)


def _output_matrix(c_re, c_im):
    eye = jnp.eye(N_GROUPS, dtype=F32)
    blk = lambda cm: jnp.einsum("ghp,gk->gpkh", cm, eye).reshape(STATE_W, SSM_W)
    return jnp.concatenate([blk(c_re), -blk(c_im)], axis=0)


def _state_power(ar, ai, n):
    steps = int(round(math.log2(n)))
    assert 1 << steps == n
    for _ in range(steps):
        ar, ai = ar * ar - ai * ai, 2.0 * ar * ai
    return ar, ai


def kernel(x, p, g_ffn1, w1_a, w3_a, w2_a, g_mix, w_in, b_f, a_re, a_im, log_dt, b_re, b_im, c_re, c_im, d_skip, w_glu, b_glu, g_attn_out, g_ssm_out, w_out, g_ffn2, w1_b, w3_b, w2_b, g_ple, w_ple_gate, w_ple_proj, g_final, loss_target, m_g_ffn1, m_w1_a, m_w3_a, m_w2_a, m_g_mix, m_w_in, m_b_f, m_a_re, m_a_im, m_log_dt, m_b_re, m_b_im, m_c_re, m_c_im, m_d_skip, m_w_glu, m_b_glu, m_g_attn_out, m_g_ssm_out, m_w_out, m_g_ffn2, m_w1_b, m_w3_b, m_w2_b, m_g_ple, m_w_ple_gate, m_w_ple_proj, m_g_final, v_g_ffn1, v_w1_a, v_w3_a, v_w2_a, v_g_mix, v_w_in, v_b_f, v_a_re, v_a_im, v_log_dt, v_b_re, v_b_im, v_c_re, v_c_im, v_d_skip, v_w_glu, v_b_glu, v_g_attn_out, v_g_ssm_out, v_w_out, v_g_ffn2, v_w1_b, v_w3_b, v_w2_b, v_g_ple, v_w_ple_gate, v_w_ple_proj, v_g_final):
    args = dict(locals())
    weights = {n: args[n] for n in WEIGHT_ORDER}
    moms = {n: args["m_" + n] for n in WEIGHT_ORDER}
    vars_ = {n: args["v_" + n] for n in WEIGHT_ORDER}
    T = x.shape[1]
    x2, p2, tgt = x[0], p[0, 0], loss_target[0]

    late_entries, early_entries = GRAD_GROUPS
    full = _unpack_gathered(gather_shards(_pack_shards(weights, BF16, early_entries)), early_entries)
    core = lax.axis_index("c").astype(jnp.int32).reshape(1)
    chip = (2 * lax.axis_index("x") + lax.axis_index("y")).astype(jnp.int32).reshape(1)
    loss_part, dx, grads, late = _local_step(x2, p2, tgt, {n: weights[n] for n, _ in SMALL}, full,
                                             early_exchange=lambda g: _pair_of(g, late_entries, core),
                                             late_pack=_pack_shards(weights, BF16, late_entries))
    return _reduce_and_update(weights, moms, vars_, loss_part, dx, grads, core, chip, late)


def _pair_of(grads, entries, core):
    packed = _pack_full_grads(grads, entries)
    return pair_sum(packed, sibling_split(packed), core)


def _local_step(x2, p2, tgt, sm, full, early_exchange=None, late_pack=None):
    full = dict(full)
    T = x2.shape[0]
    (g_ffn1, g_mix, b_f, a_re, a_im, log_dt, b_re, b_im, c_re, c_im, d_skip, b_glu, g_attn_out, g_ssm_out, g_ffn2, g_ple,
     g_final) = (sm[n] for n, _ in SMALL)
    w_in_f = full["w_in"]
    w_in_r = jnp.concatenate([w_in_f[:, :ATTN_W] * QK_SCALE, w_in_f[:, ATTN_W:3 * ATTN_W], w_in_f[:, 3 * ATTN_W + N_HEADS:],
                              w_in_f[:, 3 * ATTN_W:3 * ATTN_W + N_HEADS], jnp.zeros((D_MODEL, 128 - N_HEADS), BF16)], axis=1)
    b_f_pad = jnp.pad(b_f, ((0, 0), (0, 128 - N_HEADS)))

    disc_in = (a_re[0], a_im[0], log_dt[0], b_re[0], b_im[0])
    (abar_r, abar_i, bbar_r, bbar_i), disc_vjp = jax.vjp(_discretize, *disc_in)
    wb = _input_matrix(bbar_r, bbar_i)
    cbd = _output_matrix(c_re[0], c_im[0])
    ar, ai = abar_r.reshape(1, STATE_W), abar_i.reshape(1, STATE_W)
    alr, ali = _state_power(ar, ai, T // 8)
    dvec = d_skip.reshape(1, SSM_W)
    wb16, cbd16 = wb.astype(BF16), cbd.astype(BF16)

    h1, a1a, a3a, n1 = ffn_fwd(x2, g_ffn1, full["w1_a"], full["w3_a"], full["w2_a"], "ffn_a_fwd")
    u, qkv, s_in, fz, cum = mixin_fwd(h1, g_mix, w_in_r, b_f_pad)
    q_aug, k_aug, v_aug = heads_in(qkv, cum)
    if late_pack is None:
        o_heads, q_bwd = attn_fwd(q_aug, k_aug, v_aug)
    else:
        o_heads, q_bwd, gathered = attn_fwd(q_aug, k_aug, v_aug, send=late_pack)
        full.update(_unpack_gathered(gathered, GRAD_GROUPS[0]))
    s_perm = _permute_time(s_in)
    y_perm, xs = ssm_fwd(s_perm, wb16, cbd16, ar, ai, alr, ali, dvec)
    ypre = _unpermute_time(y_perm)
    h2, mixed = mixout_fwd(h1, o_heads, ypre, g_attn_out, g_ssm_out, full["w_glu"], b_glu, full["w_out"])
    h3, a1b, a3b, n2 = ffn_fwd(h2, g_ffn2, full["w1_b"], full["w3_b"], full["w2_b"], "ffn_b_fwd")

    dh3, n3, dzg, dpp, loss_part, dg_ple, dg_final = head_fwd_bwd(
        h3, p2, tgt, g_ple, g_final.reshape(1, D_MODEL), full["w_ple_gate"], full["w_ple_proj"])
    grads = {"g_ple": dg_ple, "g_final": dg_final.reshape(D_MODEL)}
    grads["w_ple_gate"] = mm_tn(n3, dzg, "dw_ple_gate")
    grads["w_ple_proj"] = mm_tn(p2, dpp, "dw_ple_proj")

    dh2, da1, da3, act, grads["g_ffn2"] = ffn_bwd(h2, g_ffn2, dh3, a1b, a3b, full["w1_b"], full["w3_b"], full["w2_b"], "ffn_b_bwd")
    grads["w1_b"] = mm_tn(da1, n2, "dw1_b")
    grads["w3_b"] = mm_tn(da3, n2, "dw3_b")
    grads["w2_b"] = mm_tn(act, dh3, "dw2_b", scale=0.5)

    seg = (jnp.arange(ATTN_W)[:, None] // HEAD_DIM == jnp.arange(128)[None, :]).astype(F32)
    do_aug, dypre, dpre, yg, grads["g_attn_out"], grads["g_ssm_out"], grads["b_glu"] = mixout_bwd(
        dh2, o_heads, ypre, g_attn_out, g_ssm_out, full["w_glu"], b_glu, full["w_out"], seg)
    grads["w_out"] = mm_tn(mixed, dh2, "dw_out")
    grads["w_glu"] = mm_tn(yg, dpre, "dw_glu")

    if early_exchange is None:
        late = None
        dq_aug, dk_aug, dv_aug, dc_rows = attn_bwd(q_bwd, k_aug, v_aug, do_aug)
    else:
        pair_late = early_exchange(grads)
        dq_aug, dk_aug, dv_aug, dc_rows, got_late = attn_bwd(q_bwd, k_aug, v_aug, do_aug, pair=pair_late)
        late = (pair_late, got_late)
    dc = jnp.pad(dc_rows.reshape(N_HEADS, T).T, ((0, 0), (0, 128 - N_HEADS)))

    dy_perm = _permute_time(dypre)
    du_perm, gs, d_a, dd = ssm_bwd(dy_perm, s_perm, xs, cbd16.T, wb16.T, ar, ai, alr, ali, dvec)
    ds_in = _unpermute_time(du_perm)
    hg = N_GROUPS // 2
    d_in, d_out = [], []
    for part in range(2):
        ins, outs = [], []
        for half in range(2):
            states = (part * STATE_W + half * _HALF_ST, _HALF_ST)
            chans = (half * _HALF_CH, _HALF_CH)
            blk = mm_tn(s_perm, gs, f"dw_ssm_in_{part}{half}", a_cols=chans, b_cols=states)
            ins.append(jnp.einsum("ghgp->ghp", blk.reshape(hg, GROUP_CH, hg, N_STATE)))
            blk = mm_tn(xs, dy_perm, f"dw_ssm_out_{part}{half}", a_cols=states, b_cols=chans)
            outs.append(jnp.einsum("gpgh->gph", blk.reshape(hg, N_STATE, hg, GROUP_CH)))
        d_in.append(jnp.concatenate(ins, axis=0).transpose(0, 2, 1))
        d_out.append(jnp.concatenate(outs, axis=0).transpose(0, 2, 1))
    d_abar_r = jnp.sum(d_a[:, :STATE_W], axis=0).reshape(N_GROUPS, N_STATE)
    d_abar_i = jnp.sum(d_a[:, STATE_W:], axis=0).reshape(N_GROUPS, N_STATE)
    d_disc = disc_vjp((d_abar_r, d_abar_i, d_in[0], d_in[1]))
    for name, val in zip(("a_re", "a_im", "log_dt", "b_re", "b_im"), d_disc):
        grads[name] = val[None]
    grads["c_re"] = d_out[0][None]
    grads["c_im"] = -d_out[1][None]
    grads["d_skip"] = dd.reshape(1, N_GROUPS, GROUP_CH)

    dh1, dz, grads["g_mix"], dbf = mixin_bwd(dh2, h1, g_mix, w_in_r, dq_aug, dk_aug, dv_aug, ds_in, dc, fz)
    grads["b_f"] = dbf[:, :N_HEADS]
    d_w_in_r = mm_tn(u, dz, "dw_in")
    grads["w_in"] = jnp.concatenate([d_w_in_r[:, :ATTN_W] * QK_SCALE, d_w_in_r[:, ATTN_W:3 * ATTN_W],
                                     d_w_in_r[:, 3 * ATTN_W + SSM_W:3 * ATTN_W + SSM_W + N_HEADS],
                                     d_w_in_r[:, 3 * ATTN_W:3 * ATTN_W + SSM_W]], axis=1)

    dx, da1, da3, act, grads["g_ffn1"] = ffn_bwd(x2, g_ffn1, dh1, a1a, a3a, full["w1_a"], full["w3_a"], full["w2_a"], "ffn_a_bwd")
    grads["w1_a"] = mm_tn(da1, n1, "dw1_a")
    grads["w3_a"] = mm_tn(da3, n1, "dw3_a")
    grads["w2_a"] = mm_tn(act, dh1, "dw2_a", scale=0.5)
    return loss_part, dx, grads, late


def _reduce_and_update(weights, moms, vars_, loss_part, dx, grads, core, chip, late):
    pair_early = _pair_of(grads, GRAD_GROUPS[1], core)
    g_stored = {}
    for entries, (pair, got) in zip(GRAD_GROUPS, (late, (pair_early, chip_exchange(pair_early)))):
        half = chip_sum(pair, got, chip)
        g_stored.update(_unpack_shards(join_halves(half, sibling_swap(half), core), entries))
    g_out, d_out, m_out, v_out = {}, {}, {}, {}
    for n, _, _ in BIG:
        d, m, v = adamw(_stored(n, weights[n]), g_stored[n], _stored(n, moms[n]), _stored(n, vars_[n]), "adamw_" + n)
        g_out[n], d_out[n], m_out[n], v_out[n] = (_restored(n, a) for a in (g_stored[n], d, m, v))

    small = allreduce_small(_pack_small({n: grads[n] for n, _ in SMALL}, extra=loss_part[0, 0]))
    d_small, m_small, v_small = adamw(_pack_small(weights), small, _pack_small(moms), _pack_small(vars_), "adamw_small")

    g_small, loss = _unpack_small(small)
    g_out.update(g_small)
    outs = []
    for big, sm in ((d_out, d_small), (m_out, m_small), (v_out, v_small)):
        o, _ = _unpack_small(sm)
        o.update(big)
        outs.append(o)
    result = [loss, dx[None]] + [g_out[n] for n in WEIGHT_ORDER]
    for o in outs:
        result += [o[n] for n in WEIGHT_ORDER]
    return tuple(result)
```

```python
import functools
import math

import jax
import jax.numpy as jnp
from jax import lax
from jax.experimental import pallas as pl
from jax.experimental.pallas import tpu as pltpu

F32 = jnp.float32
BF16 = jnp.bfloat16

D_MODEL = 1024
D_FF = 2816
N_HEADS = 8
HEAD_DIM = 64
ATTN_W = 512
SSM_W = 512
N_GROUPS = 32
N_STATE = 64
GROUP_CH = 16
STATE_W = N_GROUPS * N_STATE
Z_COLS = 2176
QK_SCALE = 0.125
EPS = 1e-6

ADAM_LR = 0.001
ADAM_B1 = 0.9
ADAM_B2 = 0.999
ADAM_EPS = 1e-08
ADAM_WD = 0.01
ADAM_STEP = 10

TOKEN_TILE = 512
FFN_TOKEN_TILE = 256
FF_CHUNK = 1408
MM_K_TILE = 2048
ATTN_TILE = 512
SCAN_STEPS = 32
SCAN_LANES = 512
VMEM_LIMIT = 48 * 1024 * 1024
FFN_VMEM_LIMIT = 56 * 1024 * 1024
COPY_CHUNKS = 4

NT_DIMS = (((1,), (1,)), ((), ()))
TN_DIMS = (((0,), (0,)), ((), ()))
HIGHEST = lax.Precision.HIGHEST
MESH = pl.DeviceIdType.MESH

BIG = (
    ("w1_a", (1024, 704), 1), ("w3_a", (1024, 704), 1), ("w2_a", (704, 1024), 0),
    ("w_in", (1024, 514), 1), ("w_glu", (128, 512), 0), ("w_out", (256, 1024), 0),
    ("w1_b", (1024, 704), 1), ("w3_b", (1024, 704), 1), ("w2_b", (704, 1024), 0),
    ("w_ple_gate", (256, 1024), 0), ("w_ple_proj", (256, 256), 1),
)
PACK_COLS = 1024
PACK_ALIGN = 16
SMALL = (
    ("g_ffn1", (1, 1024)), ("g_mix", (1, 1024)), ("b_f", (1, 8)), ("a_re", (1, 32, 64)), ("a_im", (1, 32, 64)),
    ("log_dt", (1, 32)), ("b_re", (1, 32, 64, 16)), ("b_im", (1, 32, 64, 16)), ("c_re", (1, 32, 16, 64)),
    ("c_im", (1, 32, 16, 64)), ("d_skip", (1, 32, 16)), ("b_glu", (1, 512)), ("g_attn_out", (1, 512)),
    ("g_ssm_out", (1, 512)), ("g_ffn2", (1, 1024)), ("g_ple", (1, 1024)), ("g_final", (1024,)),
)
SMALL_ROWS = 1152
WEIGHT_ORDER = ("g_ffn1", "w1_a", "w3_a", "w2_a", "g_mix", "w_in", "b_f", "a_re", "a_im", "log_dt", "b_re", "b_im",
                "c_re", "c_im", "d_skip", "w_glu", "b_glu", "g_attn_out", "g_ssm_out", "w_out", "g_ffn2", "w1_b",
                "w3_b", "w2_b", "g_ple", "w_ple_gate", "w_ple_proj", "g_final")


def _params(sem=None, vmem=VMEM_LIMIT):
    kw = dict(vmem_limit_bytes=vmem)
    if sem is not None:
        kw["dimension_semantics"] = sem
    return pltpu.CompilerParams(**kw)


def _sds(shape, dtype):
    return jax.ShapeDtypeStruct(shape, dtype)


def _tile(n, pref):
    t = min(n, pref)
    assert n % t == 0, (n, pref)
    return t


def _rms_scale(x):
    return lax.rsqrt(jnp.mean(x * x, axis=-1, keepdims=True) + EPS)


def _rms_bwd(dy, x, g):
    r = _rms_scale(x)
    xh = x * r
    dxh = dy * g
    dx = r * (dxh - xh * jnp.mean(dxh * xh, axis=-1, keepdims=True))
    return dx, jnp.sum(dy * xh, axis=0, keepdims=True)


def _dot(a, b):
    return jnp.dot(a, b, preferred_element_type=F32)


def _dot_nt(a, b):
    return lax.dot_general(a, b, NT_DIMS, preferred_element_type=F32)


def _dot_tn(a, b):
    return lax.dot_general(a, b, TN_DIMS, preferred_element_type=F32)


_GELU_C = math.sqrt(2.0 / math.pi)


def _gelu_parts(x):
    t = jnp.tanh(_GELU_C * (x + 0.044715 * x * x * x))
    return 0.5 * x * (1.0 + t), t


def _gelu_grad(x, t):
    return 0.5 * (1.0 + t) + 0.5 * x * (1.0 - t * t) * _GELU_C * (1.0 + 3.0 * 0.044715 * x * x)


def _resident(shape):
    return pl.BlockSpec(shape, lambda i: (0,) * len(shape), pipeline_mode=pl.Buffered(1))


def ffn_fwd(h, g, w1, w3, w2, name):
    T = h.shape[0]
    tm = _tile(T, FFN_TOKEN_TILE)

    def body(h_ref, g_ref, w1_ref, w3_ref, w2_ref, ho_ref, a1_ref, a3_ref, n_ref):
        x = h_ref[...]
        n = (x * _rms_scale(x) * g_ref[...]).astype(BF16)
        n_ref[...] = n
        out = x
        for lo in range(0, D_FF, FF_CHUNK):
            cols = slice(lo, lo + FF_CHUNK)
            a1 = _dot_nt(n, w1_ref[cols, :])
            a3 = _dot_nt(n, w3_ref[cols, :])
            a1_ref[:, cols] = a1.astype(BF16)
            a3_ref[:, cols] = a3.astype(BF16)
            act = (a1 * jax.nn.sigmoid(a1) * a3).astype(BF16)
            out = out + 0.5 * _dot(act, w2_ref[cols, :])
        ho_ref[...] = out

    tok = lambda i: (i, 0)
    return pl.pallas_call(
        body, name=name, grid=(T // tm,),
        in_specs=[pl.BlockSpec((tm, D_MODEL), tok), _resident((1, D_MODEL)), _resident((D_FF, D_MODEL)),
                  _resident((D_FF, D_MODEL)), _resident((D_FF, D_MODEL))],
        out_specs=[pl.BlockSpec((tm, D_MODEL), tok), pl.BlockSpec((tm, D_FF), tok), pl.BlockSpec((tm, D_FF), tok),
                   pl.BlockSpec((tm, D_MODEL), tok)],
        out_shape=[_sds((T, D_MODEL), F32), _sds((T, D_FF), BF16), _sds((T, D_FF), BF16), _sds((T, D_MODEL), BF16)],
        compiler_params=_params(("arbitrary",), FFN_VMEM_LIMIT),
    )(h, g, w1, w3, w2)


def ffn_bwd(h, g, dho, a1, a3, w1, w3, w2, name):
    T = h.shape[0]
    tm = _tile(T, FFN_TOKEN_TILE)

    def body(h_ref, g_ref, dho_ref, a1_ref, a3_ref, w1_ref, w3_ref, w2_ref, dhi_ref, da1_ref, da3_ref, act_ref, dg_ref):
        @pl.when(pl.program_id(0) == 0)
        def _():
            dg_ref[...] = jnp.zeros_like(dg_ref)

        dho = dho_ref[...]
        dhb = (0.5 * dho).astype(BF16)
        dn = None
        for lo in range(0, D_FF, FF_CHUNK):
            cols = slice(lo, lo + FF_CHUNK)
            a1v = a1_ref[:, cols].astype(F32)
            a3v = a3_ref[:, cols].astype(F32)
            s = jax.nn.sigmoid(a1v)
            sl = a1v * s
            dact = _dot_nt(dhb, w2_ref[cols, :])
            act_ref[:, cols] = (sl * a3v).astype(BF16)
            da1 = (dact * a3v * s * (1.0 + a1v * (1.0 - s))).astype(BF16)
            da3 = (dact * sl).astype(BF16)
            da1_ref[:, cols] = da1
            da3_ref[:, cols] = da3
            part = _dot(da1, w1_ref[cols, :]) + _dot(da3, w3_ref[cols, :])
            dn = part if dn is None else dn + part
        dx, dg = _rms_bwd(dn, h_ref[...], g_ref[...])
        dg_ref[...] += dg
        dhi_ref[...] = dho + dx

    tok = lambda i: (i, 0)
    return pl.pallas_call(
        body, name=name, grid=(T // tm,),
        in_specs=[pl.BlockSpec((tm, D_MODEL), tok), _resident((1, D_MODEL)), pl.BlockSpec((tm, D_MODEL), tok),
                  pl.BlockSpec((tm, D_FF), tok), pl.BlockSpec((tm, D_FF), tok), _resident((D_FF, D_MODEL)),
                  _resident((D_FF, D_MODEL)), _resident((D_FF, D_MODEL))],
        out_specs=[pl.BlockSpec((tm, D_MODEL), tok), pl.BlockSpec((tm, D_FF), tok), pl.BlockSpec((tm, D_FF), tok),
                   pl.BlockSpec((tm, D_FF), tok), pl.BlockSpec((1, D_MODEL), lambda i: (0, 0))],
        out_shape=[_sds((T, D_MODEL), F32), _sds((T, D_FF), BF16), _sds((T, D_FF), BF16), _sds((T, D_FF), BF16),
                   _sds((1, D_MODEL), F32)],
        compiler_params=_params(("arbitrary",), FFN_VMEM_LIMIT),
    )(h, g, dho, a1, a3, w1, w3, w2)


def mm_tn(a, b, name, scale=1.0, a_cols=None, b_cols=None):
    T = a.shape[0]
    a_off, M = a_cols or (0, a.shape[1])
    b_off, N = b_cols or (0, b.shape[1])
    bm = 512 if M % 512 == 0 else (1408 if M == 2816 else 256)
    bn = N if N in (2176, 1408) else (1408 if N == 2816 else min(N, 1024))
    tk = _tile(T, MM_K_TILE)
    row_bytes = 2 * (bm * a.dtype.itemsize + bn * b.dtype.itemsize)
    while tk > TOKEN_TILE and tk * row_bytes > VMEM_LIMIT // 3:
        tk //= 2
    assert M % bm == 0 and N % bn == 0 and T % tk == 0 and a_off % bm == 0 and b_off % bn == 0
    n_k = T // tk
    m0, n0 = a_off // bm, b_off // bn

    def body(a_ref, b_ref, o_ref):
        k = pl.program_id(2)

        @pl.when(k == 0)
        def _():
            o_ref[...] = jnp.zeros_like(o_ref)

        o_ref[...] += _dot_tn(a_ref[...].astype(BF16), b_ref[...].astype(BF16))

        if scale != 1.0:
            @pl.when(k == n_k - 1)
            def _():
                o_ref[...] = o_ref[...] * scale

    return pl.pallas_call(
        body, name=name, grid=(M // bm, N // bn, n_k),
        in_specs=[pl.BlockSpec((tk, bm), lambda m, n, k: (k, m0 + m)), pl.BlockSpec((tk, bn), lambda m, n, k: (k, n0 + n))],
        out_specs=pl.BlockSpec((bm, bn), lambda m, n, k: (m, n)),
        out_shape=_sds((M, N), F32),
        compiler_params=_params(("arbitrary", "arbitrary", "arbitrary")),
    )(a, b)


def mixin_fwd(h1, g, w_in_r, b_f_pad):
    T = h1.shape[0]
    tm = _tile(T, TOKEN_TILE)

    def body(h_ref, g_ref, w_ref, bf_ref, u_ref, qkv_ref, s_ref, fz_ref, c_ref, carry):
        @pl.when(pl.program_id(0) == 0)
        def _():
            carry[...] = jnp.zeros_like(carry)

        x = h_ref[...]
        u = (x * _rms_scale(x) * g_ref[...]).astype(BF16)
        u_ref[...] = u
        z = _dot(u, w_ref[...])
        qkv_ref[...] = z[:, :3 * ATTN_W].astype(BF16)
        s_ref[...] = z[:, 3 * ATTN_W:3 * ATTN_W + SSM_W]
        fz = z[:, 3 * ATTN_W + SSM_W:] + bf_ref[...]
        fz_ref[...] = fz
        lane = lax.broadcasted_iota(jnp.int32, fz.shape, 1)
        logf = jnp.where(lane < N_HEADS, jnp.minimum(fz, 0.0) - jnp.log(1.0 + jnp.exp(-jnp.abs(fz))), 0.0)
        row = lax.broadcasted_iota(jnp.int32, (tm, tm), 0)
        col = lax.broadcasted_iota(jnp.int32, (tm, tm), 1)
        tri = (col <= row).astype(F32)
        cs = jnp.dot(tri, logf, precision=HIGHEST, preferred_element_type=F32) + carry[0:1, :]
        c_ref[...] = cs
        carry[...] = jnp.broadcast_to(cs[tm - 1:tm, :], carry.shape)

    tok = lambda i: (i, 0)
    fix = lambda i: (0, 0)
    return pl.pallas_call(
        body, name="mixin_fwd", grid=(T // tm,),
        in_specs=[pl.BlockSpec((tm, D_MODEL), tok), pl.BlockSpec((1, D_MODEL), fix),
                  pl.BlockSpec((D_MODEL, Z_COLS), fix), pl.BlockSpec((1, 128), fix)],
        out_specs=[pl.BlockSpec((tm, D_MODEL), tok), pl.BlockSpec((tm, 3 * ATTN_W), tok), pl.BlockSpec((tm, SSM_W), tok),
                   pl.BlockSpec((tm, 128), tok), pl.BlockSpec((tm, 128), tok)],
        out_shape=[_sds((T, D_MODEL), BF16), _sds((T, 3 * ATTN_W), BF16), _sds((T, SSM_W), F32),
                   _sds((T, 128), F32), _sds((T, 128), F32)],
        scratch_shapes=[pltpu.VMEM((8, 128), F32)],
        compiler_params=_params(("arbitrary",)),
    )(h1, g, w_in_r, b_f_pad)


def mixin_bwd(dh2, h1, g, w_in_r, dq, dk, dv, ds, dc, fz):
    T = h1.shape[0]
    tm = _tile(T, TOKEN_TILE)
    n_t = T // tm
    assert dq.shape[1:] == (n_t, 128, tm), dq.shape

    def body(dh2_ref, h_ref, g_ref, w_ref, dq_ref, dk_ref, dv_ref, ds_ref, dc_ref, fz_ref,
             dh1_ref, dz_ref, dg_ref, dbf_ref, carry):
        @pl.when(pl.program_id(0) == 0)
        def _():
            carry[...] = jnp.zeros_like(carry)
            dg_ref[...] = jnp.zeros_like(dg_ref)
            dbf_ref[...] = jnp.zeros_like(dbf_ref)

        row = lax.broadcasted_iota(jnp.int32, (tm, tm), 0)
        col = lax.broadcasted_iota(jnp.int32, (tm, tm), 1)
        tri = (col >= row).astype(F32)
        dlogf = jnp.dot(tri, dc_ref[...], precision=HIGHEST, preferred_element_type=F32) + carry[0:1, :]
        carry[...] = jnp.broadcast_to(dlogf[0:1, :], carry.shape)
        dfz = dlogf * jax.nn.sigmoid(-fz_ref[...])
        dbf_ref[...] += jnp.sum(dfz, axis=0, keepdims=True)
        dz = jnp.concatenate([_join_heads(dq_ref, BF16, transposed=True), _join_heads(dk_ref, BF16), _join_heads(dv_ref, BF16),
                              ds_ref[...], dfz], axis=1).astype(BF16)
        dz_ref[...] = dz
        du = _dot_nt(dz, w_ref[...])
        dx, dg = _rms_bwd(du, h_ref[...], g_ref[...])
        dg_ref[...] += dg
        dh1_ref[...] = dh2_ref[...] + dx

    tok = lambda i: (n_t - 1 - i, 0)
    fix = lambda i: (0, 0)
    heads = pl.BlockSpec((N_HEADS, tm, 128), lambda i: (0, n_t - 1 - i, 0))
    return pl.pallas_call(
        body, name="mixin_bwd", grid=(n_t,),
        in_specs=[pl.BlockSpec((tm, D_MODEL), tok), pl.BlockSpec((tm, D_MODEL), tok), pl.BlockSpec((1, D_MODEL), fix),
                  pl.BlockSpec((D_MODEL, Z_COLS), fix), pl.BlockSpec((N_HEADS, 1, 128, tm), lambda i: (0, n_t - 1 - i, 0, 0)),
                  heads, heads, pl.BlockSpec((tm, SSM_W), tok),
                  pl.BlockSpec((tm, 128), tok), pl.BlockSpec((tm, 128), tok)],
        out_specs=[pl.BlockSpec((tm, D_MODEL), tok), pl.BlockSpec((tm, Z_COLS), tok), pl.BlockSpec((1, D_MODEL), fix),
                   pl.BlockSpec((1, 128), fix)],
        out_shape=[_sds((T, D_MODEL), F32), _sds((T, Z_COLS), BF16), _sds((1, D_MODEL), F32), _sds((1, 128), F32)],
        scratch_shapes=[pltpu.VMEM((8, 128), F32)],
        compiler_params=_params(("arbitrary",)),
    )(dh2, h1, g, w_in_r, dq, dk, dv, ds, dc, fz)


def _lane_move(src_lo, dst_lo, width, dtype):
    r = lax.broadcasted_iota(jnp.int32, (128, 128), 0)
    c = lax.broadcasted_iota(jnp.int32, (128, 128), 1)
    return ((c - dst_lo == r - src_lo) & (r >= src_lo) & (r < src_lo + width)).astype(dtype)


def _lane_const(lo, width, value):
    lane = lax.broadcasted_iota(jnp.int32, (1, 128), 1)
    return jnp.where((lane >= lo) & (lane < lo + width), value, 0.0).astype(F32)


def _pieces(a):
    hi = a.astype(BF16)
    rest = a - hi.astype(F32)
    mid = rest.astype(BF16)
    return hi, mid, (rest - mid.astype(F32)).astype(BF16)


def _head_features(pair_block, e):
    return _dot(pair_block, _lane_move(HEAD_DIM * e, 0, HEAD_DIM, BF16))


def _helper_columns(pieces, head, sign):
    out = None
    for k, piece in enumerate(pieces):
        term = _dot(piece, _lane_move(head, HEAD_DIM + k, 1, BF16))
        out = term if out is None else out + term
    return sign * out


def heads_in(qkv, cum):
    T = qkv.shape[0]
    tm = _tile(T, TOKEN_TILE)

    def body(qkv_ref, c_ref, q_ref, k_ref, v_ref):
        c = _pieces(c_ref[...])
        for h in range(N_HEADS):
            p, e = divmod(h, 2)
            blk = lambda base: qkv_ref[:, base + 128 * p:base + 128 * (p + 1)]
            q_ref[h] = (_head_features(blk(0), e) + _lane_const(HEAD_DIM, 3, -1.0)).astype(BF16)
            k_ref[h] = (_head_features(blk(ATTN_W), e) + _helper_columns(c, h, 1.0)
                        + _lane_const(HEAD_DIM + 3, 3, 1.0)).astype(BF16)
            v_ref[h] = (_head_features(blk(2 * ATTN_W), e) + _lane_const(HEAD_DIM, 3, 1.0)).astype(BF16)

    tok = lambda i: (i, 0)
    heads = pl.BlockSpec((N_HEADS, tm, 128), lambda i: (0, i, 0))
    return pl.pallas_call(
        body, name="heads_in", grid=(T // tm,),
        in_specs=[pl.BlockSpec((tm, 3 * ATTN_W), tok), pl.BlockSpec((tm, 128), tok)],
        out_specs=[heads] * 3, out_shape=[_sds((N_HEADS, T, 128), BF16)] * 3,
        compiler_params=_params(("arbitrary",)),
    )(qkv, cum)


def attn_fwd(q_aug, k_aug, v_aug, send=None):
    H, T, wd = q_aug.shape
    hd = HEAD_DIM
    tq = _tile(T, ATTN_TILE)
    n = T // tq

    def body(q_ref, k_ref, v_ref, o_ref, qb_ref, m_sc, acc, s_even, s_odd):
        qi = pl.program_id(1)
        qv = q_ref[0]
        m_sc[...] = jnp.full_like(m_sc, -jnp.inf)
        acc[...] = jnp.zeros_like(acc)

        def key_rows(j):
            return pl.ds(pl.multiple_of(jnp.minimum(j, qi) * tq, tq), tq)

        def logits(j, buf):
            buf[...] = _dot_nt(k_ref[0, key_rows(j), :], qv)

        def update(j, buf, masked):
            st = buf[...]
            if masked:
                keep = lax.broadcasted_iota(jnp.int32, (tq, tq), 0) <= lax.broadcasted_iota(jnp.int32, (tq, tq), 1)
                st = jnp.where(keep, st, -1e30)
            m_old = m_sc[...]
            m_new = jnp.maximum(m_old, jnp.max(st, axis=0, keepdims=True))
            pt = jnp.exp(st - m_new).astype(BF16)
            acc[...] = jnp.exp(m_old - m_new) * acc[...] + _dot_tn(v_ref[0, key_rows(j), :], pt)
            m_sc[...] = m_new

        logits(0, s_even)

        def two_tiles(p, carry):
            j = 2 * p
            logits(j + 1, s_odd)
            update(j, s_even, False)
            logits(j + 2, s_even)
            update(j + 1, s_odd, False)
            return carry

        lax.fori_loop(0, qi // 2, two_tiles, 0)

        @pl.when(qi % 2 == 0)
        def _():
            update(qi, s_even, True)

        @pl.when(qi % 2 == 1)
        def _():
            logits(qi, s_odd)
            update(qi - 1, s_even, False)
            update(qi, s_odd, True)

        total = acc[hd:hd + 1, :]
        o_ref[0] = (acc[...] / total).T
        hi, mid, lo = (t.astype(F32) for t in _pieces(-(m_sc[...] + jnp.log(total))))
        row = lax.broadcasted_iota(jnp.int32, (wd, tq), 0)
        lse_rows = jnp.where(row == hd + 3, hi, jnp.where(row == hd + 4, mid, jnp.where(row == hd + 5, lo, 0.0)))
        qb_ref[0] = (qv.astype(F32) + lse_rows.T).astype(BF16)

    qmap = lambda h, i: (h, i, 0)
    head = lambda h, i: (h, 0, 0)
    in_specs = [pl.BlockSpec((1, tq, wd), qmap), pl.BlockSpec((1, T, wd), head), pl.BlockSpec((1, T, wd), head)]
    out_specs = [pl.BlockSpec((1, tq, wd), qmap), pl.BlockSpec((1, tq, wd), qmap)]
    out_shape = [_sds((H, T, wd), F32), _sds((H, T, wd), BF16)]
    scratch = [pltpu.VMEM((1, tq), F32), pltpu.VMEM((wd, tq), F32), pltpu.VMEM((tq, tq), F32), pltpu.VMEM((tq, tq), F32)]
    operands = (q_aug, k_aug, v_aug)
    if send is None:
        kernel_body = body
    else:
        def kernel_body(q_ref, k_ref, v_ref, send_ref, o_ref, qb_ref, got_ref, m_sc, acc, s_even, s_odd, *sems):
            h, i = pl.program_id(0), pl.program_id(1)

            @pl.when((h == 0) & (i == 0))
            def _():
                _gather_start(send_ref, got_ref, *sems)

            body(q_ref, k_ref, v_ref, o_ref, qb_ref, m_sc, acc, s_even, s_odd)

            @pl.when((h == H - 1) & (i == n - 1))
            def _():
                _gather_finish(send_ref, got_ref, *sems)

        in_specs, out_specs = in_specs + [_HBM], out_specs + [_HBM]
        out_shape = out_shape + [_sds((4,) + send.shape, send.dtype)]
        scratch, operands = scratch + _gather_scratch(send.shape[0]), operands + (send,)
    return pl.pallas_call(
        kernel_body, name="attn_fwd", grid=(H, n), in_specs=in_specs, out_specs=out_specs, out_shape=out_shape,
        scratch_shapes=scratch, compiler_params=_params(("arbitrary", "arbitrary")),
    )(*operands)


def attn_bwd(q_aug, k_aug, v_aug, do_aug, pair=None):
    H, T, wd = q_aug.shape
    tq = _tile(T, ATTN_TILE)
    n = T // tq

    def compute(q_ref, do_ref, k_ref, v_ref, dq_ref, dk_ref, dv_ref, dc_ref, dck, s_a, d_a, s_b, d_b):
        j = pl.program_id(1)

        @pl.when(j == 0)
        def _():
            dq_ref[...] = jnp.zeros_like(dq_ref)
            dc_ref[...] = jnp.zeros_like(dc_ref)

        dk_ref[...] = jnp.zeros_like(dk_ref)
        dv_ref[...] = jnp.zeros_like(dv_ref)
        dck[...] = jnp.zeros_like(dck)
        kv, vv = k_ref[0], v_ref[0]

        def query_rows(i):
            return pl.ds(pl.multiple_of(jnp.minimum(i, n - 1) * tq, tq), tq)

        def products(i, s_buf, d_buf):
            rows = query_rows(i)
            s_buf[...] = _dot_nt(kv, q_ref[0, rows, :])
            d_buf[...] = _dot_nt(vv, do_ref[0, rows, :])

        def update(i, s_buf, d_buf, masked):
            rows = query_rows(i)
            qv, dov = q_ref[0, rows, :], do_ref[0, rows, :]
            pt = jnp.exp(s_buf[...])
            if masked:
                keep = lax.broadcasted_iota(jnp.int32, (tq, tq), 0) <= lax.broadcasted_iota(jnp.int32, (tq, tq), 1)
                pt = jnp.where(keep, pt, 0.0)
            dv_ref[0] += _dot(pt.astype(BF16), dov)
            dst = pt * d_buf[...]
            dsb = dst.astype(BF16)
            dk_ref[0] += _dot(dsb, qv)
            dq_ref[0, i] += _dot_tn(kv, dsb)
            dck[...] += jnp.sum(dst, axis=1, keepdims=True)
            dc_ref[0, pl.ds(i, 1), :] += jnp.sum(dst, axis=0, keepdims=True)

        products(j, s_a, d_a)
        products(j + 1, s_b, d_b)
        update(j, s_a, d_a, True)
        left = n - 1 - j

        def two_tiles(p, carry):
            i = j + 1 + 2 * p
            products(i + 1, s_a, d_a)
            update(i, s_b, d_b, False)
            products(i + 2, s_b, d_b)
            update(i + 1, s_a, d_a, False)
            return carry

        lax.fori_loop(0, left // 2, two_tiles, 0)

        @pl.when(left % 2 == 1)
        def _():
            update(n - 1, s_b, d_b, False)

        dc_ref[0, pl.ds(j, 1), :] -= jnp.broadcast_to(dck[...], (tq, 128)).T[0:1, :]

    head = lambda h, j: (h, 0, 0)
    kmap = lambda h, j: (h, j, 0)
    in_specs = [pl.BlockSpec((1, T, wd), head), pl.BlockSpec((1, T, wd), head), pl.BlockSpec((1, tq, wd), kmap),
                pl.BlockSpec((1, tq, wd), kmap)]
    out_specs = [pl.BlockSpec((1, n, wd, tq), lambda h, j: (h, 0, 0, 0)), pl.BlockSpec((1, tq, wd), kmap),
                 pl.BlockSpec((1, tq, wd), kmap), pl.BlockSpec((1, n, tq), head)]
    out_shape = [_sds((H, n, wd, tq), F32), _sds((H, T, wd), F32), _sds((H, T, wd), F32), _sds((H, n, tq), F32)]
    scratch = [pltpu.VMEM((tq, 1), F32)] + [pltpu.VMEM((tq, tq), F32)] * 4
    operands = (q_aug, do_aug, k_aug, v_aug)
    if pair is None:
        body = compute
    else:
        def body(q_ref, do_ref, k_ref, v_ref, pair_ref, dq_ref, dk_ref, dv_ref, dc_ref, got_ref,
                 dck, s_a, d_a, s_b, d_b, send_sems, recv_sems):
            h, j = pl.program_id(0), pl.program_id(1)

            @pl.when((h == 0) & (j == 0))
            def _():
                for cp in _chip_copies(pair_ref, got_ref, send_sems, recv_sems):
                    cp.start()

            compute(q_ref, do_ref, k_ref, v_ref, dq_ref, dk_ref, dv_ref, dc_ref, dck, s_a, d_a, s_b, d_b)

            @pl.when((h == H - 1) & (j == n - 1))
            def _():
                for cp in _chip_copies(pair_ref, got_ref, send_sems, recv_sems):
                    cp.wait()

        sems = pltpu.SemaphoreType.DMA((3 * len(_spans(pair.shape[1], COPY_CHUNKS)),))
        in_specs, out_specs = in_specs + [_HBM], out_specs + [_HBM]
        out_shape = out_shape + [_sds((3,) + pair.shape[1:], pair.dtype)]
        scratch, operands = scratch + [sems, sems], operands + (pair,)
    return pl.pallas_call(
        body, name="attn_bwd", grid=(H, n), in_specs=in_specs, out_specs=out_specs, out_shape=out_shape,
        scratch_shapes=scratch, compiler_params=_params(("arbitrary", "arbitrary")),
    )(*operands)


def _complex_step(a_r, a_i, cr, ci, br, bi):
    return a_r * cr - a_i * ci + br, a_r * ci + a_i * cr + bi


_HALF_CH = SSM_W // 2
_HALF_ST = STATE_W // 2


def _state_cols(part, half):
    lo = part * STATE_W + half * _HALF_ST
    return slice(lo, lo + _HALF_ST)


def _channels_to_states(x, w_ref, out_ref):
    for half in range(2):
        ch = slice(half * _HALF_CH, (half + 1) * _HALF_CH)
        for part in range(2):
            cols = _state_cols(part, half)
            out_ref[:, cols] = _dot(x[:, ch], w_ref[ch, cols])


def _states_to_channels(x, w_ref):
    halves = []
    for half in range(2):
        ch = slice(half * _HALF_CH, (half + 1) * _HALF_CH)
        halves.append(_dot(x[:, _state_cols(0, half)], w_ref[_state_cols(0, half), ch])
                      + _dot(x[:, _state_cols(1, half)], w_ref[_state_cols(1, half), ch]))
    return jnp.concatenate(halves, axis=1)


def ssm_fwd(s_perm, wb, cbd, a_r, a_i, al_r, al_i, dvec):
    T = s_perm.shape[0]
    chunk = T // 8
    ts = _tile(chunk, SCAN_STEPS)
    tr, n_s = ts * 8, chunk // ts
    W, LB = STATE_W, SCAN_LANES

    def body(s_ref, wb_ref, cbd_ref, ar_ref, ai_ref, alr_ref, ali_ref, dv_ref, y_ref, xs_ref, bu, carry):
        ph, i = pl.program_id(0), pl.program_id(1)

        @pl.when((ph == 0) & (i == 0))
        def _():
            carry[...] = jnp.zeros_like(carry)

        _channels_to_states(s_ref[...].astype(BF16), wb_ref, bu)

        def scan(store):
            for lb in range(W // LB):
                lo = lb * LB
                re, im = slice(lo, lo + LB), slice(W + lo, W + lo + LB)
                ar = jnp.broadcast_to(ar_ref[:, re], (8, LB))
                ai = jnp.broadcast_to(ai_ref[:, re], (8, LB))

                def step(s, c):
                    rows = pl.ds(pl.multiple_of(s * 8, 8), 8)
                    nr, ni = _complex_step(ar, ai, c[0], c[1], bu[rows, re], bu[rows, im])
                    if store:
                        bu[rows, re] = nr
                        bu[rows, im] = ni
                    return nr, ni

                cr, ci = lax.fori_loop(0, ts, step, (carry[:, re], carry[:, im]), unroll=2)
                carry[:, re] = cr
                carry[:, im] = ci

        @pl.when(ph == 0)
        def _():
            scan(False)

            @pl.when(i == n_s - 1)
            def _():
                er, ei = carry[:, :W], carry[:, W:]
                alr = jnp.broadcast_to(alr_ref[...], (8, W))
                ali = jnp.broadcast_to(ali_ref[...], (8, W))
                first = lax.broadcasted_iota(jnp.int32, (8, W), 0) == 0
                sr, si = jnp.zeros((8, W), F32), jnp.zeros((8, W), F32)
                for _ in range(7):
                    vr, vi = _complex_step(alr, ali, sr, si, er, ei)
                    sr = jnp.where(first, 0.0, pltpu.roll(vr, 1, 0))
                    si = jnp.where(first, 0.0, pltpu.roll(vi, 1, 0))
                carry[:, :W] = sr
                carry[:, W:] = si

        @pl.when(ph == 1)
        def _():
            scan(True)
            xb = bu[...].astype(BF16)
            xs_ref[...] = xb
            y_ref[...] = _states_to_channels(xb, cbd_ref) + s_ref[...] * dv_ref[...]

    fix = lambda p, i: (0, 0)
    return pl.pallas_call(
        body, name="ssm_fwd", grid=(2, n_s),
        in_specs=[pl.BlockSpec((tr, SSM_W), lambda p, i: (i, 0)), pl.BlockSpec((SSM_W, 2 * W), fix),
                  pl.BlockSpec((2 * W, SSM_W), fix), pl.BlockSpec((1, W), fix), pl.BlockSpec((1, W), fix),
                  pl.BlockSpec((1, W), fix), pl.BlockSpec((1, W), fix), pl.BlockSpec((1, SSM_W), fix)],
        out_specs=[pl.BlockSpec((tr, SSM_W), lambda p, i: (i * p, 0)), pl.BlockSpec((tr, 2 * W), lambda p, i: (i * p, 0))],
        out_shape=[_sds((T, SSM_W), F32), _sds((T, 2 * W), BF16)],
        scratch_shapes=[pltpu.VMEM((tr, 2 * W), F32), pltpu.VMEM((8, 2 * W), F32)],
        compiler_params=_params(("arbitrary", "arbitrary")),
    )(s_perm, wb, cbd, a_r, a_i, al_r, al_i, dvec)


def ssm_bwd(dy_perm, s_perm, xs, cbd_t, wb_t, a_r, a_i, al_r, al_i, dvec):
    T = s_perm.shape[0]
    chunk = T // 8
    ts = _tile(chunk, SCAN_STEPS)
    tr, n_s = ts * 8, chunk // ts
    W, LB = STATE_W, SCAN_LANES

    def body(dy_ref, s_ref, xs_ref, cbt_ref, wbt_ref, ar_ref, ai_ref, alr_ref, ali_ref, dv_ref,
             du_ref, gs_ref, da_ref, dd_ref, gd, x32, carry):
        ph, i = pl.program_id(0), pl.program_id(1)

        @pl.when((ph == 0) & (i == 0))
        def _():
            carry[...] = jnp.zeros_like(carry)
            da_ref[...] = jnp.zeros_like(da_ref)
            dd_ref[...] = jnp.zeros_like(dd_ref)

        _channels_to_states(dy_ref[...].astype(BF16), cbt_ref, gd)

        def scan(store):
            for lb in range(W // LB):
                lo = lb * LB
                re, im = slice(lo, lo + LB), slice(W + lo, W + lo + LB)
                ar = jnp.broadcast_to(ar_ref[:, re], (8, LB))
                nai = -jnp.broadcast_to(ai_ref[:, re], (8, LB))

                def step(k, c):
                    rows = pl.ds(pl.multiple_of((ts - 1 - k) * 8, 8), 8)
                    cr, ci = c[0], c[1]
                    nr, ni = _complex_step(ar, nai, cr, ci, gd[rows, re], gd[rows, im])
                    if store:
                        xr, xi = x32[rows, re], x32[rows, im]
                        gd[rows, re] = nr
                        gd[rows, im] = ni
                        return nr, ni, c[2] + cr * xr + ci * xi, c[3] + ci * xr - cr * xi
                    return nr, ni

                init = (carry[:, re], carry[:, im])
                if store:
                    init = init + (da_ref[:, re], da_ref[:, im])
                out = lax.fori_loop(0, ts, step, init, unroll=2)
                carry[:, re] = out[0]
                carry[:, im] = out[1]
                if store:
                    da_ref[:, re] = out[2]
                    da_ref[:, im] = out[3]

        @pl.when(ph == 0)
        def _():
            scan(False)

            @pl.when(i == n_s - 1)
            def _():
                er, ei = carry[:, :W], carry[:, W:]
                alr = jnp.broadcast_to(alr_ref[...], (8, W))
                nali = -jnp.broadcast_to(ali_ref[...], (8, W))
                last = lax.broadcasted_iota(jnp.int32, (8, W), 0) == 7
                rr, ri = jnp.zeros((8, W), F32), jnp.zeros((8, W), F32)
                for _ in range(7):
                    vr, vi = _complex_step(alr, nali, rr, ri, er, ei)
                    rr = jnp.where(last, 0.0, pltpu.roll(vr, 7, 0))
                    ri = jnp.where(last, 0.0, pltpu.roll(vi, 7, 0))
                carry[:, :W] = rr
                carry[:, W:] = ri

        @pl.when(ph == 1)
        def _():
            x32[...] = xs_ref[...].astype(F32)
            scan(True)
            gb = gd[...].astype(BF16)
            gs_ref[...] = gb
            dy = dy_ref[...]
            du_ref[...] = _states_to_channels(gb, wbt_ref) + dy * dv_ref[...]
            dd_ref[...] += jnp.sum(dy * s_ref[...], axis=0, keepdims=True)

    fix = lambda p, i: (0, 0)
    rev = lambda p, i: (n_s - 1 - i, 0)
    rev_out = lambda p, i: (n_s - 1 - i * p, 0)
    return pl.pallas_call(
        body, name="ssm_bwd", grid=(2, n_s),
        in_specs=[pl.BlockSpec((tr, SSM_W), rev), pl.BlockSpec((tr, SSM_W), rev), pl.BlockSpec((tr, 2 * W), rev),
                  pl.BlockSpec((SSM_W, 2 * W), fix), pl.BlockSpec((2 * W, SSM_W), fix), pl.BlockSpec((1, W), fix),
                  pl.BlockSpec((1, W), fix), pl.BlockSpec((1, W), fix), pl.BlockSpec((1, W), fix),
                  pl.BlockSpec((1, SSM_W), fix)],
        out_specs=[pl.BlockSpec((tr, SSM_W), rev_out), pl.BlockSpec((tr, 2 * W), rev_out),
                   pl.BlockSpec((8, 2 * W), fix), pl.BlockSpec((1, SSM_W), fix)],
        out_shape=[_sds((T, SSM_W), F32), _sds((T, 2 * W), BF16), _sds((8, 2 * W), F32), _sds((1, SSM_W), F32)],
        scratch_shapes=[pltpu.VMEM((tr, 2 * W), F32), pltpu.VMEM((tr, 2 * W), F32), pltpu.VMEM((8, 2 * W), F32)],
        compiler_params=_params(("arbitrary", "arbitrary")),
    )(dy_perm, s_perm, xs, cbd_t, wb_t, a_r, a_i, al_r, al_i, dvec)


def _join_heads(ref, dtype, transposed=False):
    def move(h, dst):
        x = ref[h, 0] if transposed else ref[h]
        pieces = _pieces(x) if dtype == F32 else (x.astype(BF16),)
        out = None
        for piece in pieces:
            place = _lane_move(0, dst, HEAD_DIM, BF16)
            term = _dot_tn(piece, place) if transposed else _dot(piece, place)
            out = term if out is None else out + term
        return out

    return jnp.concatenate([move(2 * p, 0) + move(2 * p + 1, HEAD_DIM) for p in range(N_HEADS // 2)], axis=1)


def mixout_fwd(h1, o_heads, ypre, g_a, g_s, w_glu, b_glu, w_out):
    T = h1.shape[0]
    tm = _tile(T, TOKEN_TILE)

    def body(h_ref, at_ref, yp_ref, ga_ref, gs_ref, wg_ref, bg_ref, wo_ref, h2_ref, mixed_ref):
        yg, _ = _gelu_parts(yp_ref[...])
        gl = yg * jax.nn.sigmoid(_dot(yg.astype(BF16), wg_ref[...]) + bg_ref[...])
        at = _join_heads(at_ref, F32)
        mixed = jnp.concatenate([at * _rms_scale(at) * ga_ref[...], gl * _rms_scale(gl) * gs_ref[...]], axis=1)
        mixed = mixed.astype(BF16)
        mixed_ref[...] = mixed
        h2_ref[...] = h_ref[...] + _dot(mixed, wo_ref[...])

    tok = lambda i: (i, 0)
    fix = lambda i: (0, 0)
    return pl.pallas_call(
        body, name="mixout_fwd", grid=(T // tm,),
        in_specs=[pl.BlockSpec((tm, D_MODEL), tok), pl.BlockSpec((N_HEADS, tm, 128), lambda i: (0, i, 0)),
                  pl.BlockSpec((tm, SSM_W), tok),
                  pl.BlockSpec((1, ATTN_W), fix), pl.BlockSpec((1, SSM_W), fix), pl.BlockSpec((SSM_W, SSM_W), fix),
                  pl.BlockSpec((1, SSM_W), fix), pl.BlockSpec((D_MODEL, D_MODEL), fix)],
        out_specs=[pl.BlockSpec((tm, D_MODEL), tok), pl.BlockSpec((tm, D_MODEL), tok)],
        out_shape=[_sds((T, D_MODEL), F32), _sds((T, D_MODEL), BF16)],
        compiler_params=_params(("arbitrary",)),
    )(h1, o_heads, ypre, g_a, g_s, w_glu, b_glu, w_out)


def mixout_bwd(dh2, o_heads, ypre, g_a, g_s, w_glu, b_glu, w_out, seg):
    T = dh2.shape[0]
    tm = _tile(T, TOKEN_TILE)

    def body(dh_ref, at_ref, yp_ref, ga_ref, gs_ref, wg_ref, bg_ref, wo_ref, seg_ref,
             do_ref, dyp_ref, dpre_ref, yg_ref, dga_ref, dgs_ref, dbg_ref):
        @pl.when(pl.program_id(0) == 0)
        def _():
            dga_ref[...] = jnp.zeros_like(dga_ref)
            dgs_ref[...] = jnp.zeros_like(dgs_ref)
            dbg_ref[...] = jnp.zeros_like(dbg_ref)

        dmix = _dot_nt(dh_ref[...].astype(BF16), wo_ref[...])
        at = _join_heads(at_ref, F32)
        dat, dga = _rms_bwd(dmix[:, :ATTN_W], at, ga_ref[...])
        dga_ref[...] += dga
        delta = _pieces(jnp.dot(dat * at, seg_ref[...], precision=HIGHEST, preferred_element_type=F32))
        datb = dat.astype(BF16)
        for h in range(N_HEADS):
            p, e = divmod(h, 2)
            do_ref[h] = (_head_features(datb[:, 128 * p:128 * (p + 1)], e) + _helper_columns(delta, h, -1.0)).astype(BF16)
        yp = yp_ref[...]
        yg, t = _gelu_parts(yp)
        ygb = yg.astype(BF16)
        yg_ref[...] = ygb
        sg = jax.nn.sigmoid(_dot(ygb, wg_ref[...]) + bg_ref[...])
        dgl, dgs = _rms_bwd(dmix[:, ATTN_W:], yg * sg, gs_ref[...])
        dgs_ref[...] += dgs
        dpre = dgl * yg * sg * (1.0 - sg)
        dbg_ref[...] += jnp.sum(dpre, axis=0, keepdims=True)
        dpb = dpre.astype(BF16)
        dpre_ref[...] = dpb
        dyg = dgl * sg + _dot_nt(dpb, wg_ref[...])
        dyp_ref[...] = dyg * _gelu_grad(yp, t)

    tok = lambda i: (i, 0)
    fix = lambda i: (0, 0)
    heads = pl.BlockSpec((N_HEADS, tm, 128), lambda i: (0, i, 0))
    return pl.pallas_call(
        body, name="mixout_bwd", grid=(T // tm,),
        in_specs=[pl.BlockSpec((tm, D_MODEL), tok), heads, pl.BlockSpec((tm, SSM_W), tok),
                  pl.BlockSpec((1, ATTN_W), fix), pl.BlockSpec((1, SSM_W), fix), pl.BlockSpec((SSM_W, SSM_W), fix),
                  pl.BlockSpec((1, SSM_W), fix), pl.BlockSpec((D_MODEL, D_MODEL), fix), pl.BlockSpec((ATTN_W, 128), fix)],
        out_specs=[heads, pl.BlockSpec((tm, SSM_W), tok), pl.BlockSpec((tm, SSM_W), tok),
                   pl.BlockSpec((tm, SSM_W), tok), pl.BlockSpec((1, ATTN_W), fix),
                   pl.BlockSpec((1, SSM_W), fix), pl.BlockSpec((1, SSM_W), fix)],
        out_shape=[_sds((N_HEADS, T, 128), BF16), _sds((T, SSM_W), F32), _sds((T, SSM_W), BF16), _sds((T, SSM_W), BF16),
                   _sds((1, ATTN_W), F32), _sds((1, SSM_W), F32), _sds((1, SSM_W), F32)],
        compiler_params=_params(("arbitrary",)),
    )(dh2, o_heads, ypre, g_a, g_s, w_glu, b_glu, w_out, seg)


def head_fwd_bwd(h3, p, target, g_ple, g_final, w_gate, w_proj):
    T = h3.shape[0]
    tm = _tile(T, TOKEN_TILE)
    pd = p.shape[1]

    def body(h_ref, p_ref, tg_ref, gp_ref, gf_ref, wg_ref, wp_ref,
             dh_ref, n3_ref, dz_ref, dpp_ref, loss_ref, dgp_ref, dgf_ref):
        @pl.when(pl.program_id(0) == 0)
        def _():
            loss_ref[...] = jnp.zeros_like(loss_ref)
            dgp_ref[...] = jnp.zeros_like(dgp_ref)
            dgf_ref[...] = jnp.zeros_like(dgf_ref)

        x = h_ref[...]
        gp, gf = gp_ref[...], gf_ref[...]
        n3 = (x * _rms_scale(x) * gp).astype(BF16)
        n3_ref[...] = n3
        gate = jax.nn.sigmoid(_dot(n3, wg_ref[...]))
        pp = _dot(p_ref[...].astype(BF16), wp_ref[...])
        h4 = x + gate * pp
        y = h4 * _rms_scale(h4) * gf
        e = y - tg_ref[...]
        tile_loss = jnp.sum(jnp.sum(e * e, axis=1, keepdims=True), axis=0, keepdims=True) * (0.5 / D_MODEL)
        loss_ref[...] += jnp.broadcast_to(tile_loss, loss_ref.shape)
        dh4, dgf = _rms_bwd(e * (1.0 / D_MODEL), h4, gf)
        dgf_ref[...] += dgf
        dzg = dh4 * pp * gate * (1.0 - gate)
        dzb = dzg.astype(BF16)
        dz_ref[...] = dzb
        dpp_ref[...] = (dh4 * gate).astype(BF16)
        dx, dgp = _rms_bwd(_dot_nt(dzb, wg_ref[...]), x, gp)
        dgp_ref[...] += dgp
        dh_ref[...] = dh4 + dx

    tok = lambda i: (i, 0)
    fix = lambda i: (0, 0)
    return pl.pallas_call(
        body, name="head_fwd_bwd", grid=(T // tm,),
        in_specs=[pl.BlockSpec((tm, D_MODEL), tok), pl.BlockSpec((tm, pd), tok), pl.BlockSpec((tm, D_MODEL), tok),
                  pl.BlockSpec((1, D_MODEL), fix), pl.BlockSpec((1, D_MODEL), fix), pl.BlockSpec((D_MODEL, D_MODEL), fix),
                  pl.BlockSpec((pd, D_MODEL), fix)],
        out_specs=[pl.BlockSpec((tm, D_MODEL), tok), pl.BlockSpec((tm, D_MODEL), tok), pl.BlockSpec((tm, D_MODEL), tok),
                   pl.BlockSpec((tm, D_MODEL), tok), pl.BlockSpec((8, 128), fix), pl.BlockSpec((1, D_MODEL), fix),
                   pl.BlockSpec((1, D_MODEL), fix)],
        out_shape=[_sds((T, D_MODEL), F32), _sds((T, D_MODEL), BF16), _sds((T, D_MODEL), BF16), _sds((T, D_MODEL), BF16),
                   _sds((8, 128), F32), _sds((1, D_MODEL), F32), _sds((1, D_MODEL), F32)],
        compiler_params=_params(("arbitrary",)),
    )(h3, p, target, g_ple, g_final, w_gate, w_proj)


def _row_tile(rows, cols, n_arrays):
    lanes = -(-cols // 128) * 128
    cap = VMEM_LIMIT // 3 // (2 * n_arrays * lanes * 4)
    best = None
    for t in range(PACK_ALIGN, min(rows, cap) + 1, PACK_ALIGN):
        if rows % t == 0:
            best = t
    assert best is not None, (rows, cols)
    return best


def _adamw_math(w, g, m, v):
    nm = ADAM_B1 * m + (1.0 - ADAM_B1) * g
    nv = ADAM_B2 * v + (1.0 - ADAM_B2) * (g * g)
    c1 = 1.0 - ADAM_B1 ** ADAM_STEP
    c2 = 1.0 - ADAM_B2 ** ADAM_STEP
    return -ADAM_LR * ((nm / c1) / (jnp.sqrt(nv / c2) + ADAM_EPS) + ADAM_WD * w), nm, nv


def adamw(w, g, m, v, name):
    R, C = w.shape
    tr = _row_tile(R, C, 7)

    def body(w_ref, g_ref, m_ref, v_ref, d_ref, nm_ref, nv_ref):
        d_ref[...], nm_ref[...], nv_ref[...] = _adamw_math(w_ref[...], g_ref[...], m_ref[...], v_ref[...])

    spec = pl.BlockSpec((tr, C), lambda i: (i, 0))
    return pl.pallas_call(
        body, name=name, grid=(R // tr,), in_specs=[spec] * 4, out_specs=[spec] * 3,
        out_shape=[_sds((R, C), F32)] * 3, compiler_params=_params(("arbitrary",)),
    )(w, g, m, v)


def join_halves(mine, other, core):
    rh, C = mine.shape
    tr = _row_tile(rh, C, 3)
    nb = rh // tr

    def body(c_ref, m_ref, o_ref, out_ref):
        out_ref[...] = jnp.where((pl.program_id(0) // nb) == c_ref[0], m_ref[...], o_ref[...])

    half = pl.BlockSpec((tr, C), lambda i, c: (i % nb, 0))
    return pl.pallas_call(
        body, name="join_halves",
        grid_spec=pltpu.PrefetchScalarGridSpec(num_scalar_prefetch=1, grid=(2 * nb,), in_specs=[half, half],
                                               out_specs=pl.BlockSpec((tr, C), lambda i, c: (i, 0))),
        out_shape=_sds((2 * rh, C), F32), compiler_params=_params(("arbitrary",)),
    )(core, mine, other)


def pair_sum(g, theirs, core):
    n, R, C = g.shape
    rh = R // 2
    tr = _row_tile(rh, C, 3)
    nb = rh // tr

    def body(c_ref, g_ref, t_ref, o_ref):
        o_ref[...] = (g_ref[...] + t_ref[...]).astype(BF16)

    here = pl.BlockSpec((1, tr, C), lambda j, i, c: (j, i, 0))
    return pl.pallas_call(
        body, name="pair_sum",
        grid_spec=pltpu.PrefetchScalarGridSpec(
            num_scalar_prefetch=1, grid=(n, nb),
            in_specs=[pl.BlockSpec((1, tr, C), lambda j, i, c: (j, c[0] * nb + i, 0)), here], out_specs=here),
        out_shape=_sds((n, rh, C), BF16), compiler_params=_params(("arbitrary", "arbitrary")),
    )(core, g, theirs)


def chip_sum(pair, got, chip):
    _, R, C = pair.shape
    tr = _row_tile(R, C, 5)

    def body(c_ref, p_ref, g0_ref, g1_ref, g2_ref, o_ref):
        f = lambda ref: ref[0].astype(F32)
        o_ref[...] = ((f(p_ref) + f(g0_ref)) + f(g1_ref)) + f(g2_ref)

    slot = lambda k: pl.BlockSpec((1, tr, C), lambda i, c: (k, i, 0))
    return pl.pallas_call(
        body, name="chip_sum",
        grid_spec=pltpu.PrefetchScalarGridSpec(
            num_scalar_prefetch=1, grid=(R // tr,),
            in_specs=[pl.BlockSpec((1, tr, C), lambda i, c: (c[0], i, 0)), slot(0), slot(1), slot(2)],
            out_specs=pl.BlockSpec((tr, C), lambda i, c: (i, 0))),
        out_shape=_sds((R, C), F32), compiler_params=_params(("arbitrary",)),
    )(chip, pair, got, got, got)


_HBM = pl.BlockSpec(memory_space=pltpu.HBM)


def _place():
    x, y, c = lax.axis_index("x"), lax.axis_index("y"), lax.axis_index("c")
    return x, y, c, [(1 - x, y), (x, 1 - y), (1 - x, 1 - y)]


def _spans(rows, n):
    assert rows % PACK_ALIGN == 0
    tiles = rows // PACK_ALIGN
    n = min(n, tiles)
    cuts = [tiles * q // n for q in range(n + 1)]
    return [(cuts[q] * PACK_ALIGN, (cuts[q + 1] - cuts[q]) * PACK_ALIGN) for q in range(n)]


def _remote(src, dst, send_sem, recv_sem, to):
    return pltpu.make_async_remote_copy(src_ref=src, dst_ref=dst, send_sem=send_sem, recv_sem=recv_sem,
                                        device_id=to, device_id_type=MESH)


GATHER_SEMS = 6


def _gather_scratch(rows):
    ici = pltpu.SemaphoreType.DMA((3 * len(_spans(rows // 2, COPY_CHUNKS)),))
    own = pltpu.SemaphoreType.DMA((len(_spans(rows, 2 * COPY_CHUNKS)),))
    return [ici, ici, own, own, ici, ici]


def _gather_start(w_ref, out_ref, ici_send, ici_recv, own_send, own_recv, pass_send, pass_recv):
    R = w_ref.shape[0]
    rh = R // 2
    spans = _spans(rh, COPY_CHUNKS)
    x, y, c, chips = _place()
    me = 2 * x + y
    for k, (cx, cy) in enumerate(chips):
        for q, (o, n) in enumerate(spans):
            rows = pl.ds(c * rh + o, n)
            _remote(w_ref.at[rows, :], out_ref.at[me, rows, :], ici_send.at[k * len(spans) + q],
                    ici_recv.at[k * len(spans) + q], (cx, cy, c)).start()
    for q, (o, n) in enumerate(_spans(R, 2 * COPY_CHUNKS)):
        rows = pl.ds(o, n)
        _remote(w_ref.at[rows, :], out_ref.at[me, rows, :], own_send.at[q], own_recv.at[q], (x, y, 1 - c)).start()


def _gather_finish(w_ref, out_ref, ici_send, ici_recv, own_send, own_recv, pass_send, pass_recv):
    R = w_ref.shape[0]
    rh = R // 2
    spans = _spans(rh, COPY_CHUNKS)
    n_sp = len(spans)
    x, y, c, chips = _place()
    me = 2 * x + y
    sibling = (x, y, 1 - c)
    passed = []
    for q, (o, n) in enumerate(spans):
        for k, (cx, cy) in enumerate(chips):
            blk = out_ref.at[2 * cx + cy, pl.ds(c * rh + o, n), :]
            _remote(blk, blk, ici_send.at[k * n_sp + q], ici_recv.at[k * n_sp + q], (cx, cy, c)).wait_recv()
            cp = _remote(blk, blk, pass_send.at[k * n_sp + q], pass_recv.at[k * n_sp + q], sibling)
            cp.start()
            passed.append(cp)
    for k, (cx, cy) in enumerate(chips):
        for q, (o, n) in enumerate(spans):
            blk = out_ref.at[2 * cx + cy, pl.ds((1 - c) * rh + o, n), :]
            _remote(blk, blk, pass_send.at[k * n_sp + q], pass_recv.at[k * n_sp + q], sibling).wait_recv()
            rows = pl.ds(c * rh + o, n)
            _remote(w_ref.at[rows, :], out_ref.at[me, rows, :], ici_send.at[k * n_sp + q], ici_recv.at[k * n_sp + q],
                    (cx, cy, c)).wait_send()
    for q, (o, n) in enumerate(_spans(R, 2 * COPY_CHUNKS)):
        rows = pl.ds(o, n)
        _remote(w_ref.at[rows, :], out_ref.at[me, rows, :], own_send.at[q], own_recv.at[q], sibling).wait()
    for cp in passed:
        cp.wait_send()


def gather_shards(wp):
    R, C = wp.shape

    def body(w_ref, out_ref, *sems):
        _gather_start(w_ref, out_ref, *sems)
        _gather_finish(w_ref, out_ref, *sems)

    return pl.pallas_call(
        body, name="gather_shards", in_specs=[_HBM], out_specs=_HBM, out_shape=_sds((4, R, C), wp.dtype),
        scratch_shapes=_gather_scratch(R),
    )(wp)


def sibling_split(g):
    n_sl, R, C = g.shape
    rh = R // 2
    spans = _spans(rh, COPY_CHUNKS)
    n_sp = len(spans)

    def body(g_ref, got_ref, send_sems, recv_sems):
        x, y, c, _ = _place()
        copies = []
        for j in range(n_sl):
            for q, (o, n) in enumerate(spans):
                cp = _remote(g_ref.at[j, pl.ds((1 - c) * rh + o, n), :], got_ref.at[j, pl.ds(o, n), :],
                             send_sems.at[j * n_sp + q], recv_sems.at[j * n_sp + q], (x, y, 1 - c))
                cp.start()
                copies.append(cp)
        for cp in copies:
            cp.wait()

    sems = pltpu.SemaphoreType.DMA((n_sl * n_sp,))
    return pl.pallas_call(
        body, name="sibling_split", in_specs=[_HBM], out_specs=_HBM, out_shape=_sds((n_sl, rh, C), g.dtype),
        scratch_shapes=[sems, sems],
    )(g)


def _chip_copies(p_ref, buf_ref, send_sems, recv_sems):
    rows = p_ref.shape[1]
    spans = _spans(rows, COPY_CHUNKS)
    x, y, c, chips = _place()
    copies = []
    for k, (cx, cy) in enumerate(chips):
        for q, (o, n) in enumerate(spans):
            copies.append(_remote(p_ref.at[2 * cx + cy, pl.ds(o, n), :], buf_ref.at[k, pl.ds(o, n), :],
                                  send_sems.at[k * len(spans) + q], recv_sems.at[k * len(spans) + q], (cx, cy, c)))
    return copies


def chip_exchange(p):
    _, R, C = p.shape

    def body(p_ref, buf_ref, send_sems, recv_sems):
        copies = _chip_copies(p_ref, buf_ref, send_sems, recv_sems)
        for cp in copies:
            cp.start()
        for cp in copies:
            cp.wait()

    sems = pltpu.SemaphoreType.DMA((3 * len(_spans(R, COPY_CHUNKS)),))
    return pl.pallas_call(
        body, name="chip_exchange", in_specs=[_HBM], out_specs=_HBM, out_shape=_sds((3, R, C), p.dtype),
        scratch_shapes=[sems, sems],
    )(p)


def sibling_swap(half):
    R, C = half.shape
    spans = _spans(R, COPY_CHUNKS)

    def body(h_ref, got_ref, send_sems, recv_sems):
        x, y, c, _ = _place()
        copies = []
        for q, (o, n) in enumerate(spans):
            cp = _remote(h_ref.at[pl.ds(o, n), :], got_ref.at[pl.ds(o, n), :], send_sems.at[q], recv_sems.at[q], (x, y, 1 - c))
            cp.start()
            copies.append(cp)
        for cp in copies:
            cp.wait()

    sems = pltpu.SemaphoreType.DMA((len(spans),))
    return pl.pallas_call(
        body, name="sibling_swap", in_specs=[_HBM], out_specs=_HBM, out_shape=_sds((R, C), half.dtype),
        scratch_shapes=[sems, sems],
    )(half)


def allreduce_small(v):
    R, C = v.shape

    def body(v_ref, out_ref, sib, pair, buf, send_sems, recv_sems):
        x, y, c, chips = _place()
        me = 2 * x + y
        swap = _remote(v_ref, sib, send_sems.at[0], recv_sems.at[0], (x, y, 1 - c))
        swap.start()
        swap.wait()
        pair[...] = v_ref[...] + sib[...]
        buf[me] = pair[...]
        sends = []
        for k, (cx, cy) in enumerate(chips):
            cp = _remote(pair, buf.at[me], send_sems.at[1 + k], recv_sems.at[1 + k], (cx, cy, c))
            cp.start()
            sends.append(cp)
        for k, (cx, cy) in enumerate(chips):
            blk = buf.at[2 * cx + cy]
            _remote(blk, blk, send_sems.at[1 + k], recv_sems.at[1 + k], (cx, cy, c)).wait_recv()
        for cp in sends:
            cp.wait_send()
        out_ref[...] = ((buf[0] + buf[1]) + buf[2]) + buf[3]

    vm = pl.BlockSpec(memory_space=pltpu.VMEM)
    return pl.pallas_call(
        body, name="allreduce_small", in_specs=[vm], out_specs=vm, out_shape=_sds((R, C), F32),
        scratch_shapes=[pltpu.VMEM((R, C), F32), pltpu.VMEM((R, C), F32), pltpu.VMEM((4, R, C), F32),
                        pltpu.SemaphoreType.DMA((4,)), pltpu.SemaphoreType.DMA((4,))],
        compiler_params=pltpu.CompilerParams(vmem_limit_bytes=VMEM_LIMIT),
    )(v)


def _rows_of(shape):
    return shape[0] * shape[1] // PACK_COLS


def _slot_rows(shape):
    return -(-_rows_of(shape) // PACK_ALIGN) * PACK_ALIGN


TRANSPOSED = ("w1_a", "w3_a", "w1_b", "w3_b")


def _stored(name, shard):
    return shard[0].T if name in TRANSPOSED else shard[0]


def _restored(name, stored):
    return stored.T[None] if name in TRANSPOSED else stored[None]


def _pack_shards(shards, dtype, entries):
    parts = []
    for name, shape, _ in entries:
        part = _stored(name, shards[name]).reshape(_rows_of(shape), PACK_COLS).astype(dtype)
        parts.append(jnp.pad(part, ((0, _slot_rows(shape) - part.shape[0]), (0, 0))))
    used = sum(p.shape[0] for p in parts)
    parts.append(jnp.zeros((_group_rows(entries) - used, PACK_COLS), dtype))
    return jnp.concatenate(parts, axis=0)


def _unpack_gathered(ag, entries):
    out, off = {}, 0
    for name, shape, axis in entries:
        r = _rows_of(shape)
        piece = ag[:, off:off + r, :]
        off += _slot_rows(shape)
        if name in TRANSPOSED:
            out[name] = piece.reshape(4 * r, PACK_COLS)
        elif axis == 0:
            out[name] = piece.reshape(4 * shape[0], shape[1])
        else:
            out[name] = piece.reshape((4,) + shape).transpose(1, 0, 2).reshape(shape[0], 4 * shape[1])
    return out


LATE = ("w_glu", "w_out", "w1_b", "w3_b", "w2_b", "w_ple_gate", "w_ple_proj")
GRAD_GROUPS = (tuple(e for e in BIG if e[0] in LATE), tuple(e for e in BIG if e[0] not in LATE))


GROUP_ROW_UNIT = 2816


def _group_rows(entries):
    used = sum(_slot_rows(shape) for _, shape, _ in entries)
    return -(-used // GROUP_ROW_UNIT) * GROUP_ROW_UNIT


def _pack_full_grads(grads, entries):
    parts = []
    for name, shape, axis in entries:
        g = grads[name]
        if name in TRANSPOSED or axis == 0:
            piece = g.reshape(4, _rows_of(shape), PACK_COLS)
        else:
            piece = g.reshape(shape[0], 4, shape[1]).transpose(1, 0, 2).reshape(4, _rows_of(shape), PACK_COLS)
        parts.append(jnp.pad(piece, ((0, 0), (0, _slot_rows(shape) - piece.shape[1]), (0, 0))))
    used = sum(p.shape[1] for p in parts)
    if _group_rows(entries) > used:
        parts.append(jnp.zeros((4, _group_rows(entries) - used, PACK_COLS), F32))
    return jnp.concatenate(parts, axis=1)


def _unpack_shards(packed, entries):
    out, off = {}, 0
    for name, shape, _ in entries:
        r = _rows_of(shape)
        out[name] = packed[off:off + r] if name in TRANSPOSED else packed[off:off + r].reshape(shape)
        off += _slot_rows(shape)
    return out


def _small_rows(shape):
    return -(-math.prod(shape) // 1024) * 8


def _pack_small(vals, extra=None):
    def slot(v, rows):
        flat = v.reshape(-1)
        return jnp.pad(flat, (0, rows * 128 - flat.shape[0])).reshape(rows, 128)

    parts = [slot(vals[name], _small_rows(shape)) for name, shape in SMALL]
    parts.append(slot(extra if extra is not None else jnp.zeros((1,), F32), 8))
    assert sum(p.shape[0] for p in parts) == SMALL_ROWS
    return jnp.concatenate(parts, axis=0)


def _unpack_small(packed):
    out, off = {}, 0
    for name, shape in SMALL:
        rows = _small_rows(shape)
        out[name] = packed[off:off + rows].reshape(-1)[:math.prod(shape)].reshape(shape)
        off += rows
    return out, packed[off, 0]


def _permute_time(a):
    T, n = a.shape
    return a.reshape(8, T // 8, n).transpose(1, 0, 2).reshape(T, n)


def _unpermute_time(a):
    T, n = a.shape
    return a.reshape(T // 8, 8, n).transpose(1, 0, 2).reshape(T, n)


def _discretize(a_re, a_im, log_dt, b_re, b_im):
    dt = jnp.exp(log_dt)[:, None]
    decay = jnp.exp(dt * a_re)
    abar_r = decay * jnp.cos(dt * a_im)
    abar_i = decay * jnp.sin(dt * a_im)
    nr, ni = abar_r - 1.0, abar_i
    den = a_re * a_re + a_im * a_im
    fr = (nr * a_re + ni * a_im) / den
    fi = (ni * a_re - nr * a_im) / den
    bbar_r = fr[..., None] * b_re - fi[..., None] * b_im
    bbar_i = fr[..., None] * b_im + fi[..., None] * b_re
    return abar_r, abar_i, bbar_r, bbar_i


def _input_matrix(bbar_r, bbar_i):
    eye = jnp.eye(N_GROUPS, dtype=F32)
    blk = lambda b: jnp.einsum("ghp,gk->ghkp", b.transpose(0, 2, 1), eye).reshape(SSM_W, STATE_W)
    return jnp.concatenate([blk(bbar_r), blk(bbar_i)], axis=1)


def _output_matrix(c_re, c_im):
    eye = jnp.eye(N_GROUPS, dtype=F32)
    blk = lambda cm: jnp.einsum("ghp,gk->gpkh", cm, eye).reshape(STATE_W, SSM_W)
    return jnp.concatenate([blk(c_re), -blk(c_im)], axis=0)


def _state_power(ar, ai, n):
    steps = int(round(math.log2(n)))
    assert 1 << steps == n
    for _ in range(steps):
        ar, ai = ar * ar - ai * ai, 2.0 * ar * ai
    return ar, ai


def kernel(x, p, g_ffn1, w1_a, w3_a, w2_a, g_mix, w_in, b_f, a_re, a_im, log_dt, b_re, b_im, c_re, c_im, d_skip, w_glu, b_glu, g_attn_out, g_ssm_out, w_out, g_ffn2, w1_b, w3_b, w2_b, g_ple, w_ple_gate, w_ple_proj, g_final, loss_target, m_g_ffn1, m_w1_a, m_w3_a, m_w2_a, m_g_mix, m_w_in, m_b_f, m_a_re, m_a_im, m_log_dt, m_b_re, m_b_im, m_c_re, m_c_im, m_d_skip, m_w_glu, m_b_glu, m_g_attn_out, m_g_ssm_out, m_w_out, m_g_ffn2, m_w1_b, m_w3_b, m_w2_b, m_g_ple, m_w_ple_gate, m_w_ple_proj, m_g_final, v_g_ffn1, v_w1_a, v_w3_a, v_w2_a, v_g_mix, v_w_in, v_b_f, v_a_re, v_a_im, v_log_dt, v_b_re, v_b_im, v_c_re, v_c_im, v_d_skip, v_w_glu, v_b_glu, v_g_attn_out, v_g_ssm_out, v_w_out, v_g_ffn2, v_w1_b, v_w3_b, v_w2_b, v_g_ple, v_w_ple_gate, v_w_ple_proj, v_g_final):
    args = dict(locals())
    weights = {n: args[n] for n in WEIGHT_ORDER}
    moms = {n: args["m_" + n] for n in WEIGHT_ORDER}
    vars_ = {n: args["v_" + n] for n in WEIGHT_ORDER}
    T = x.shape[1]
    x2, p2, tgt = x[0], p[0, 0], loss_target[0]

    late_entries, early_entries = GRAD_GROUPS
    full = _unpack_gathered(gather_shards(_pack_shards(weights, BF16, early_entries)), early_entries)
    core = lax.axis_index("c").astype(jnp.int32).reshape(1)
    chip = (2 * lax.axis_index("x") + lax.axis_index("y")).astype(jnp.int32).reshape(1)
    loss_part, dx, grads, late = _local_step(x2, p2, tgt, {n: weights[n] for n, _ in SMALL}, full,
                                             early_exchange=lambda g: _pair_of(g, late_entries, core),
                                             late_pack=_pack_shards(weights, BF16, late_entries))
    return _reduce_and_update(weights, moms, vars_, loss_part, dx, grads, core, chip, late)


def _pair_of(grads, entries, core):
    packed = _pack_full_grads(grads, entries)
    return pair_sum(packed, sibling_split(packed), core)


def _local_step(x2, p2, tgt, sm, full, early_exchange=None, late_pack=None):
    full = dict(full)
    T = x2.shape[0]
    (g_ffn1, g_mix, b_f, a_re, a_im, log_dt, b_re, b_im, c_re, c_im, d_skip, b_glu, g_attn_out, g_ssm_out, g_ffn2, g_ple,
     g_final) = (sm[n] for n, _ in SMALL)
    w_in_f = full["w_in"]
    w_in_r = jnp.concatenate([w_in_f[:, :ATTN_W] * QK_SCALE, w_in_f[:, ATTN_W:3 * ATTN_W], w_in_f[:, 3 * ATTN_W + N_HEADS:],
                              w_in_f[:, 3 * ATTN_W:3 * ATTN_W + N_HEADS], jnp.zeros((D_MODEL, 128 - N_HEADS), BF16)], axis=1)
    b_f_pad = jnp.pad(b_f, ((0, 0), (0, 128 - N_HEADS)))

    disc_in = (a_re[0], a_im[0], log_dt[0], b_re[0], b_im[0])
    (abar_r, abar_i, bbar_r, bbar_i), disc_vjp = jax.vjp(_discretize, *disc_in)
    wb = _input_matrix(bbar_r, bbar_i)
    cbd = _output_matrix(c_re[0], c_im[0])
    ar, ai = abar_r.reshape(1, STATE_W), abar_i.reshape(1, STATE_W)
    alr, ali = _state_power(ar, ai, T // 8)
    dvec = d_skip.reshape(1, SSM_W)
    wb16, cbd16 = wb.astype(BF16), cbd.astype(BF16)

    h1, a1a, a3a, n1 = ffn_fwd(x2, g_ffn1, full["w1_a"], full["w3_a"], full["w2_a"], "ffn_a_fwd")
    u, qkv, s_in, fz, cum = mixin_fwd(h1, g_mix, w_in_r, b_f_pad)
    q_aug, k_aug, v_aug = heads_in(qkv, cum)
    if late_pack is None:
        o_heads, q_bwd = attn_fwd(q_aug, k_aug, v_aug)
    else:
        o_heads, q_bwd, gathered = attn_fwd(q_aug, k_aug, v_aug, send=late_pack)
        full.update(_unpack_gathered(gathered, GRAD_GROUPS[0]))
    s_perm = _permute_time(s_in)
    y_perm, xs = ssm_fwd(s_perm, wb16, cbd16, ar, ai, alr, ali, dvec)
    ypre = _unpermute_time(y_perm)
    h2, mixed = mixout_fwd(h1, o_heads, ypre, g_attn_out, g_ssm_out, full["w_glu"], b_glu, full["w_out"])
    h3, a1b, a3b, n2 = ffn_fwd(h2, g_ffn2, full["w1_b"], full["w3_b"], full["w2_b"], "ffn_b_fwd")

    dh3, n3, dzg, dpp, loss_part, dg_ple, dg_final = head_fwd_bwd(
        h3, p2, tgt, g_ple, g_final.reshape(1, D_MODEL), full["w_ple_gate"], full["w_ple_proj"])
    grads = {"g_ple": dg_ple, "g_final": dg_final.reshape(D_MODEL)}
    grads["w_ple_gate"] = mm_tn(n3, dzg, "dw_ple_gate")
    grads["w_ple_proj"] = mm_tn(p2, dpp, "dw_ple_proj")

    dh2, da1, da3, act, grads["g_ffn2"] = ffn_bwd(h2, g_ffn2, dh3, a1b, a3b, full["w1_b"], full["w3_b"], full["w2_b"], "ffn_b_bwd")
    grads["w1_b"] = mm_tn(da1, n2, "dw1_b")
    grads["w3_b"] = mm_tn(da3, n2, "dw3_b")
    grads["w2_b"] = mm_tn(act, dh3, "dw2_b", scale=0.5)

    seg = (jnp.arange(ATTN_W)[:, None] // HEAD_DIM == jnp.arange(128)[None, :]).astype(F32)
    do_aug, dypre, dpre, yg, grads["g_attn_out"], grads["g_ssm_out"], grads["b_glu"] = mixout_bwd(
        dh2, o_heads, ypre, g_attn_out, g_ssm_out, full["w_glu"], b_glu, full["w_out"], seg)
    grads["w_out"] = mm_tn(mixed, dh2, "dw_out")
    grads["w_glu"] = mm_tn(yg, dpre, "dw_glu")

    if early_exchange is None:
        late = None
        dq_aug, dk_aug, dv_aug, dc_rows = attn_bwd(q_bwd, k_aug, v_aug, do_aug)
    else:
        pair_late = early_exchange(grads)
        dq_aug, dk_aug, dv_aug, dc_rows, got_late = attn_bwd(q_bwd, k_aug, v_aug, do_aug, pair=pair_late)
        late = (pair_late, got_late)
    dc = jnp.pad(dc_rows.reshape(N_HEADS, T).T, ((0, 0), (0, 128 - N_HEADS)))

    dy_perm = _permute_time(dypre)
    du_perm, gs, d_a, dd = ssm_bwd(dy_perm, s_perm, xs, cbd16.T, wb16.T, ar, ai, alr, ali, dvec)
    ds_in = _unpermute_time(du_perm)
    hg = N_GROUPS // 2
    d_in, d_out = [], []
    for part in range(2):
        ins, outs = [], []
        for half in range(2):
            states = (part * STATE_W + half * _HALF_ST, _HALF_ST)
            chans = (half * _HALF_CH, _HALF_CH)
            blk = mm_tn(s_perm, gs, f"dw_ssm_in_{part}{half}", a_cols=chans, b_cols=states)
            ins.append(jnp.einsum("ghgp->ghp", blk.reshape(hg, GROUP_CH, hg, N_STATE)))
            blk = mm_tn(xs, dy_perm, f"dw_ssm_out_{part}{half}", a_cols=states, b_cols=chans)
            outs.append(jnp.einsum("gpgh->gph", blk.reshape(hg, N_STATE, hg, GROUP_CH)))
        d_in.append(jnp.concatenate(ins, axis=0).transpose(0, 2, 1))
        d_out.append(jnp.concatenate(outs, axis=0).transpose(0, 2, 1))
    d_abar_r = jnp.sum(d_a[:, :STATE_W], axis=0).reshape(N_GROUPS, N_STATE)
    d_abar_i = jnp.sum(d_a[:, STATE_W:], axis=0).reshape(N_GROUPS, N_STATE)
    d_disc = disc_vjp((d_abar_r, d_abar_i, d_in[0], d_in[1]))
    for name, val in zip(("a_re", "a_im", "log_dt", "b_re", "b_im"), d_disc):
        grads[name] = val[None]
    grads["c_re"] = d_out[0][None]
    grads["c_im"] = -d_out[1][None]
    grads["d_skip"] = dd.reshape(1, N_GROUPS, GROUP_CH)

    dh1, dz, grads["g_mix"], dbf = mixin_bwd(dh2, h1, g_mix, w_in_r, dq_aug, dk_aug, dv_aug, ds_in, dc, fz)
    grads["b_f"] = dbf[:, :N_HEADS]
    d_w_in_r = mm_tn(u, dz, "dw_in")
    grads["w_in"] = jnp.concatenate([d_w_in_r[:, :ATTN_W] * QK_SCALE, d_w_in_r[:, ATTN_W:3 * ATTN_W],
                                     d_w_in_r[:, 3 * ATTN_W + SSM_W:3 * ATTN_W + SSM_W + N_HEADS],
                                     d_w_in_r[:, 3 * ATTN_W:3 * ATTN_W + SSM_W]], axis=1)

    dx, da1, da3, act, grads["g_ffn1"] = ffn_bwd(x2, g_ffn1, dh1, a1a, a3a, full["w1_a"], full["w3_a"], full["w2_a"], "ffn_a_bwd")
    grads["w1_a"] = mm_tn(da1, n1, "dw1_a")
    grads["w3_a"] = mm_tn(da3, n1, "dw3_a")
    grads["w2_a"] = mm_tn(act, dh1, "dw2_a", scale=0.5)
    return loss_part, dx, grads, late


def _reduce_and_update(weights, moms, vars_, loss_part, dx, grads, core, chip, late):
    pair_early = _pair_of(grads, GRAD_GROUPS[1], core)
    g_stored = {}
    for entries, (pair, got) in zip(GRAD_GROUPS, (late, (pair_early, chip_exchange(pair_early)))):
        half = chip_sum(pair, got, chip)
        g_stored.update(_unpack_shards(join_halves(half, sibling_swap(half), core), entries))
    g_out, d_out, m_out, v_out = {}, {}, {}, {}
    for n, _, _ in BIG:
        d, m, v = adamw(_stored(n, weights[n]), g_stored[n], _stored(n, moms[n]), _stored(n, vars_[n]), "adamw_" + n)
        g_out[n], d_out[n], m_out[n], v_out[n] = (_restored(n, a) for a in (g_stored[n], d, m, v))

    small = allreduce_small(_pack_small({n: grads[n] for n, _ in SMALL}, extra=loss_part[0, 0]))
    d_small, m_small, v_small = adamw(_pack_small(weights), small, _pack_small(moms), _pack_small(vars_), "adamw_small")

    g_small, loss = _unpack_small(small)
    g_out.update(g_small)
    outs = []
    for big, sm in ((d_out, d_small), (m_out, m_small), (v_out, v_small)):
        o, _ = _unpack_small(sm)
        o.update(big)
        outs.append(o)
    result = [loss, dx[None]] + [g_out[n] for n in WEIGHT_ORDER]
    for o in outs:
        result += [o[n] for n in WEIGHT_ORDER]
    return tuple(result)
```

```python
import functools
import math

import jax
import jax.numpy as jnp
from jax import lax
from jax.experimental import pallas as pl
from jax.experimental.pallas import tpu as pltpu

F32 = jnp.float32
BF16 = jnp.bfloat16

D_MODEL = 1024
D_FF = 2816
N_HEADS = 8
HEAD_DIM = 64
ATTN_W = 512
SSM_W = 512
N_GROUPS = 32
N_STATE = 64
GROUP_CH = 16
STATE_W = N_GROUPS * N_STATE
Z_COLS = 2176
QK_SCALE = 0.125
EPS = 1e-6

ADAM_LR = 0.001
ADAM_B1 = 0.9
ADAM_B2 = 0.999
ADAM_EPS = 1e-08
ADAM_WD = 0.01
ADAM_STEP = 10

TOKEN_TILE = 512
FFN_TOKEN_TILE = 256
FF_CHUNK = 1408
MM_K_TILE = 2048
ATTN_TILE = 512
SCAN_STEPS = 32
SCAN_LANES = 512
VMEM_LIMIT = 48 * 1024 * 1024
FFN_VMEM_LIMIT = 56 * 1024 * 1024
COPY_CHUNKS = 4

NT_DIMS = (((1,), (1,)), ((), ()))
TN_DIMS = (((0,), (0,)), ((), ()))
TQ_DIMS = (((0,), (1,)), ((), ()))
HIGHEST = lax.Precision.HIGHEST
MESH = pl.DeviceIdType.MESH

BIG = (
    ("w1_a", (1024, 704), 1), ("w3_a", (1024, 704), 1), ("w2_a", (704, 1024), 0),
    ("w_in", (1024, 514), 1), ("w_glu", (128, 512), 0), ("w_out", (256, 1024), 0),
    ("w1_b", (1024, 704), 1), ("w3_b", (1024, 704), 1), ("w2_b", (704, 1024), 0),
    ("w_ple_gate", (256, 1024), 0), ("w_ple_proj", (256, 256), 1),
)
PACK_COLS = 1024
PACK_ALIGN = 16
SMALL = (
    ("g_ffn1", (1, 1024)), ("g_mix", (1, 1024)), ("b_f", (1, 8)), ("a_re", (1, 32, 64)), ("a_im", (1, 32, 64)),
    ("log_dt", (1, 32)), ("b_re", (1, 32, 64, 16)), ("b_im", (1, 32, 64, 16)), ("c_re", (1, 32, 16, 64)),
    ("c_im", (1, 32, 16, 64)), ("d_skip", (1, 32, 16)), ("b_glu", (1, 512)), ("g_attn_out", (1, 512)),
    ("g_ssm_out", (1, 512)), ("g_ffn2", (1, 1024)), ("g_ple", (1, 1024)), ("g_final", (1024,)),
)
SMALL_ROWS = 1152
WEIGHT_ORDER = ("g_ffn1", "w1_a", "w3_a", "w2_a", "g_mix", "w_in", "b_f", "a_re", "a_im", "log_dt", "b_re", "b_im",
                "c_re", "c_im", "d_skip", "w_glu", "b_glu", "g_attn_out", "g_ssm_out", "w_out", "g_ffn2", "w1_b",
                "w3_b", "w2_b", "g_ple", "w_ple_gate", "w_ple_proj", "g_final")


def _params(sem=None, vmem=VMEM_LIMIT):
    kw = dict(vmem_limit_bytes=vmem)
    if sem is not None:
        kw["dimension_semantics"] = sem
    return pltpu.CompilerParams(**kw)


def _sds(shape, dtype):
    return jax.ShapeDtypeStruct(shape, dtype)


def _tile(n, pref):
    t = min(n, pref)
    assert n % t == 0, (n, pref)
    return t


def _rms_scale(x):
    return lax.rsqrt(jnp.mean(x * x, axis=-1, keepdims=True) + EPS)


def _rms_bwd(dy, x, g):
    r = _rms_scale(x)
    xh = x * r
    dxh = dy * g
    dx = r * (dxh - xh * jnp.mean(dxh * xh, axis=-1, keepdims=True))
    return dx, jnp.sum(dy * xh, axis=0, keepdims=True)


def _dot(a, b):
    return jnp.dot(a, b, preferred_element_type=F32)


def _dot_nt(a, b):
    return lax.dot_general(a, b, NT_DIMS, preferred_element_type=F32)


def _dot_tn(a, b):
    return lax.dot_general(a, b, TN_DIMS, preferred_element_type=F32)


_GELU_C = math.sqrt(2.0 / math.pi)


def _gelu_parts(x):
    t = jnp.tanh(_GELU_C * (x + 0.044715 * x * x * x))
    return 0.5 * x * (1.0 + t), t


def _gelu_grad(x, t):
    return 0.5 * (1.0 + t) + 0.5 * x * (1.0 - t * t) * _GELU_C * (1.0 + 3.0 * 0.044715 * x * x)


def _resident(shape):
    return pl.BlockSpec(shape, lambda i: (0,) * len(shape), pipeline_mode=pl.Buffered(1))


def ffn_fwd(h, g, w1, w3, w2, name):
    T = h.shape[0]
    tm = _tile(T, FFN_TOKEN_TILE)

    def body(h_ref, g_ref, w1_ref, w3_ref, w2_ref, ho_ref, a1_ref, a3_ref, n_ref):
        x = h_ref[...]
        n = (x * _rms_scale(x) * g_ref[...]).astype(BF16)
        n_ref[...] = n
        out = x
        for lo in range(0, D_FF, FF_CHUNK):
            cols = slice(lo, lo + FF_CHUNK)
            a1 = _dot_nt(n, w1_ref[cols, :])
            a3 = _dot_nt(n, w3_ref[cols, :])
            a1_ref[:, cols] = a1.astype(BF16)
            a3_ref[:, cols] = a3.astype(BF16)
            act = (a1 * jax.nn.sigmoid(a1) * a3).astype(BF16)
            out = out + 0.5 * _dot(act, w2_ref[cols, :])
        ho_ref[...] = out

    tok = lambda i: (i, 0)
    return pl.pallas_call(
        body, name=name, grid=(T // tm,),
        in_specs=[pl.BlockSpec((tm, D_MODEL), tok), _resident((1, D_MODEL)), _resident((D_FF, D_MODEL)),
                  _resident((D_FF, D_MODEL)), _resident((D_FF, D_MODEL))],
        out_specs=[pl.BlockSpec((tm, D_MODEL), tok), pl.BlockSpec((tm, D_FF), tok), pl.BlockSpec((tm, D_FF), tok),
                   pl.BlockSpec((tm, D_MODEL), tok)],
        out_shape=[_sds((T, D_MODEL), F32), _sds((T, D_FF), BF16), _sds((T, D_FF), BF16), _sds((T, D_MODEL), BF16)],
        compiler_params=_params(("arbitrary",), FFN_VMEM_LIMIT),
    )(h, g, w1, w3, w2)


def ffn_bwd(h, g, dho, a1, a3, w1, w3, w2, name):
    T = h.shape[0]
    tm = _tile(T, FFN_TOKEN_TILE)

    def body(h_ref, g_ref, dho_ref, a1_ref, a3_ref, w1_ref, w3_ref, w2_ref, dhi_ref, da1_ref, da3_ref, act_ref, dg_ref):
        @pl.when(pl.program_id(0) == 0)
        def _():
            dg_ref[...] = jnp.zeros_like(dg_ref)

        dho = dho_ref[...]
        dhb = (0.5 * dho).astype(BF16)
        dn = None
        for lo in range(0, D_FF, FF_CHUNK):
            cols = slice(lo, lo + FF_CHUNK)
            a1v = a1_ref[:, cols].astype(F32)
            a3v = a3_ref[:, cols].astype(F32)
            s = jax.nn.sigmoid(a1v)
            sl = a1v * s
            dact = _dot_nt(dhb, w2_ref[cols, :])
            act_ref[:, cols] = (sl * a3v).astype(BF16)
            da1 = (dact * a3v * s * (1.0 + a1v * (1.0 - s))).astype(BF16)
            da3 = (dact * sl).astype(BF16)
            da1_ref[:, cols] = da1
            da3_ref[:, cols] = da3
            part = _dot(da1, w1_ref[cols, :]) + _dot(da3, w3_ref[cols, :])
            dn = part if dn is None else dn + part
        dx, dg = _rms_bwd(dn, h_ref[...], g_ref[...])
        dg_ref[...] += dg
        dhi_ref[...] = dho + dx

    tok = lambda i: (i, 0)
    return pl.pallas_call(
        body, name=name, grid=(T // tm,),
        in_specs=[pl.BlockSpec((tm, D_MODEL), tok), _resident((1, D_MODEL)), pl.BlockSpec((tm, D_MODEL), tok),
                  pl.BlockSpec((tm, D_FF), tok), pl.BlockSpec((tm, D_FF), tok), _resident((D_FF, D_MODEL)),
                  _resident((D_FF, D_MODEL)), _resident((D_FF, D_MODEL))],
        out_specs=[pl.BlockSpec((tm, D_MODEL), tok), pl.BlockSpec((tm, D_FF), tok), pl.BlockSpec((tm, D_FF), tok),
                   pl.BlockSpec((tm, D_FF), tok), pl.BlockSpec((1, D_MODEL), lambda i: (0, 0))],
        out_shape=[_sds((T, D_MODEL), F32), _sds((T, D_FF), BF16), _sds((T, D_FF), BF16), _sds((T, D_FF), BF16),
                   _sds((1, D_MODEL), F32)],
        compiler_params=_params(("arbitrary",), FFN_VMEM_LIMIT),
    )(h, g, dho, a1, a3, w1, w3, w2)


def mm_tn(a, b, name, scale=1.0, a_cols=None, b_cols=None):
    T = a.shape[0]
    a_off, M = a_cols or (0, a.shape[1])
    b_off, N = b_cols or (0, b.shape[1])
    bm = 512 if M % 512 == 0 else (1408 if M == 2816 else 256)
    bn = N if N in (2176, 1408) else (1408 if N == 2816 else min(N, 1024))
    tk = _tile(T, MM_K_TILE)
    row_bytes = 2 * (bm * a.dtype.itemsize + bn * b.dtype.itemsize)
    while tk > TOKEN_TILE and tk * row_bytes > VMEM_LIMIT // 3:
        tk //= 2
    assert M % bm == 0 and N % bn == 0 and T % tk == 0 and a_off % bm == 0 and b_off % bn == 0
    n_k = T // tk
    m0, n0 = a_off // bm, b_off // bn

    def body(a_ref, b_ref, o_ref):
        k = pl.program_id(2)

        @pl.when(k == 0)
        def _():
            o_ref[...] = jnp.zeros_like(o_ref)

        o_ref[...] += _dot_tn(a_ref[...].astype(BF16), b_ref[...].astype(BF16))

        if scale != 1.0:
            @pl.when(k == n_k - 1)
            def _():
                o_ref[...] = o_ref[...] * scale

    return pl.pallas_call(
        body, name=name, grid=(M // bm, N // bn, n_k),
        in_specs=[pl.BlockSpec((tk, bm), lambda m, n, k: (k, m0 + m)), pl.BlockSpec((tk, bn), lambda m, n, k: (k, n0 + n))],
        out_specs=pl.BlockSpec((bm, bn), lambda m, n, k: (m, n)),
        out_shape=_sds((M, N), F32),
        compiler_params=_params(("arbitrary", "arbitrary", "arbitrary")),
    )(a, b)


def mixin_fwd(h1, g, w_in_r, b_f_pad):
    T = h1.shape[0]
    tm = _tile(T, TOKEN_TILE)

    def body(h_ref, g_ref, w_ref, bf_ref, u_ref, qkv_ref, s_ref, fz_ref, c_ref, carry):
        @pl.when(pl.program_id(0) == 0)
        def _():
            carry[...] = jnp.zeros_like(carry)

        x = h_ref[...]
        u = (x * _rms_scale(x) * g_ref[...]).astype(BF16)
        u_ref[...] = u
        z = _dot(u, w_ref[...])
        qkv_ref[...] = z[:, :3 * ATTN_W].astype(BF16)
        s_ref[...] = z[:, 3 * ATTN_W:3 * ATTN_W + SSM_W]
        fz = z[:, 3 * ATTN_W + SSM_W:] + bf_ref[...]
        fz_ref[...] = fz
        lane = lax.broadcasted_iota(jnp.int32, fz.shape, 1)
        logf = jnp.where(lane < N_HEADS, jnp.minimum(fz, 0.0) - jnp.log(1.0 + jnp.exp(-jnp.abs(fz))), 0.0)
        row = lax.broadcasted_iota(jnp.int32, (tm, tm), 0)
        col = lax.broadcasted_iota(jnp.int32, (tm, tm), 1)
        tri = (col <= row).astype(F32)
        cs = jnp.dot(tri, logf, precision=HIGHEST, preferred_element_type=F32) + carry[0:1, :]
        c_ref[...] = cs
        carry[...] = jnp.broadcast_to(cs[tm - 1:tm, :], carry.shape)

    tok = lambda i: (i, 0)
    fix = lambda i: (0, 0)
    return pl.pallas_call(
        body, name="mixin_fwd", grid=(T // tm,),
        in_specs=[pl.BlockSpec((tm, D_MODEL), tok), pl.BlockSpec((1, D_MODEL), fix),
                  pl.BlockSpec((D_MODEL, Z_COLS), fix), pl.BlockSpec((1, 128), fix)],
        out_specs=[pl.BlockSpec((tm, D_MODEL), tok), pl.BlockSpec((tm, 3 * ATTN_W), tok), pl.BlockSpec((tm, SSM_W), tok),
                   pl.BlockSpec((tm, 128), tok), pl.BlockSpec((tm, 128), tok)],
        out_shape=[_sds((T, D_MODEL), BF16), _sds((T, 3 * ATTN_W), BF16), _sds((T, SSM_W), F32),
                   _sds((T, 128), F32), _sds((T, 128), F32)],
        scratch_shapes=[pltpu.VMEM((8, 128), F32)],
        compiler_params=_params(("arbitrary",)),
    )(h1, g, w_in_r, b_f_pad)


def mixin_bwd(dh2, h1, g, w_in_r, dq, dk, dv, ds, dc, fz):
    T = h1.shape[0]
    tm = _tile(T, TOKEN_TILE)
    n_t = T // tm
    assert dq.shape[1:] == (n_t, 128, tm), dq.shape

    def body(dh2_ref, h_ref, g_ref, w_ref, dq_ref, dk_ref, dv_ref, ds_ref, dc_ref, fz_ref,
             dh1_ref, dz_ref, dg_ref, dbf_ref, carry):
        @pl.when(pl.program_id(0) == 0)
        def _():
            carry[...] = jnp.zeros_like(carry)
            dg_ref[...] = jnp.zeros_like(dg_ref)
            dbf_ref[...] = jnp.zeros_like(dbf_ref)

        row = lax.broadcasted_iota(jnp.int32, (tm, tm), 0)
        col = lax.broadcasted_iota(jnp.int32, (tm, tm), 1)
        tri = (col >= row).astype(F32)
        dlogf = jnp.dot(tri, dc_ref[...], precision=HIGHEST, preferred_element_type=F32) + carry[0:1, :]
        carry[...] = jnp.broadcast_to(dlogf[0:1, :], carry.shape)
        dfz = dlogf * jax.nn.sigmoid(-fz_ref[...])
        dbf_ref[...] += jnp.sum(dfz, axis=0, keepdims=True)
        dz = jnp.concatenate([_join_heads(dq_ref, BF16, True), _join_heads(dk_ref, BF16, True), _join_heads(dv_ref, BF16, True),
                              ds_ref[...], dfz], axis=1).astype(BF16)
        dz_ref[...] = dz
        du = _dot_nt(dz, w_ref[...])
        dx, dg = _rms_bwd(du, h_ref[...], g_ref[...])
        dg_ref[...] += dg
        dh1_ref[...] = dh2_ref[...] + dx

    tok = lambda i: (n_t - 1 - i, 0)
    fix = lambda i: (0, 0)
    heads = pl.BlockSpec((N_HEADS, 1, 128, tm), lambda i: (0, n_t - 1 - i, 0, 0))
    return pl.pallas_call(
        body, name="mixin_bwd", grid=(n_t,),
        in_specs=[pl.BlockSpec((tm, D_MODEL), tok), pl.BlockSpec((tm, D_MODEL), tok), pl.BlockSpec((1, D_MODEL), fix),
                  pl.BlockSpec((D_MODEL, Z_COLS), fix), heads, heads, heads, pl.BlockSpec((tm, SSM_W), tok),
                  pl.BlockSpec((tm, 128), tok), pl.BlockSpec((tm, 128), tok)],
        out_specs=[pl.BlockSpec((tm, D_MODEL), tok), pl.BlockSpec((tm, Z_COLS), tok), pl.BlockSpec((1, D_MODEL), fix),
                   pl.BlockSpec((1, 128), fix)],
        out_shape=[_sds((T, D_MODEL), F32), _sds((T, Z_COLS), BF16), _sds((1, D_MODEL), F32), _sds((1, 128), F32)],
        scratch_shapes=[pltpu.VMEM((8, 128), F32)],
        compiler_params=_params(("arbitrary",)),
    )(dh2, h1, g, w_in_r, dq, dk, dv, ds, dc, fz)


def _lane_move(src_lo, dst_lo, width, dtype):
    r = lax.broadcasted_iota(jnp.int32, (128, 128), 0)
    c = lax.broadcasted_iota(jnp.int32, (128, 128), 1)
    return ((c - dst_lo == r - src_lo) & (r >= src_lo) & (r < src_lo + width)).astype(dtype)


def _lane_const(lo, width, value):
    lane = lax.broadcasted_iota(jnp.int32, (1, 128), 1)
    return jnp.where((lane >= lo) & (lane < lo + width), value, 0.0).astype(F32)


def _pieces(a):
    hi = a.astype(BF16)
    rest = a - hi.astype(F32)
    mid = rest.astype(BF16)
    return hi, mid, (rest - mid.astype(F32)).astype(BF16)


def _head_features(pair_block, e):
    return _dot(pair_block, _lane_move(HEAD_DIM * e, 0, HEAD_DIM, BF16))


def _helper_columns(pieces, head, sign):
    out = None
    for k, piece in enumerate(pieces):
        term = _dot(piece, _lane_move(head, HEAD_DIM + k, 1, BF16))
        out = term if out is None else out + term
    return sign * out


def heads_in(qkv, cum):
    T = qkv.shape[0]
    tm = _tile(T, TOKEN_TILE)

    def body(qkv_ref, c_ref, q_ref, k_ref, v_ref):
        c = _pieces(c_ref[...])
        for h in range(N_HEADS):
            p, e = divmod(h, 2)
            blk = lambda base: qkv_ref[:, base + 128 * p:base + 128 * (p + 1)]
            q_ref[h] = (_head_features(blk(0), e) + _lane_const(HEAD_DIM, 3, -1.0)).astype(BF16)
            k_ref[h] = (_head_features(blk(ATTN_W), e) + _helper_columns(c, h, 1.0)
                        + _lane_const(HEAD_DIM + 3, 3, 1.0)).astype(BF16)
            v_ref[h] = (_head_features(blk(2 * ATTN_W), e) + _lane_const(HEAD_DIM, 3, 1.0)).astype(BF16)

    tok = lambda i: (i, 0)
    heads = pl.BlockSpec((N_HEADS, tm, 128), lambda i: (0, i, 0))
    return pl.pallas_call(
        body, name="heads_in", grid=(T // tm,),
        in_specs=[pl.BlockSpec((tm, 3 * ATTN_W), tok), pl.BlockSpec((tm, 128), tok)],
        out_specs=[heads] * 3, out_shape=[_sds((N_HEADS, T, 128), BF16)] * 3,
        compiler_params=_params(("arbitrary",)),
    )(qkv, cum)


def attn_fwd(q_aug, k_aug, v_aug, send=None):
    H, T, wd = q_aug.shape
    hd = HEAD_DIM
    tq = _tile(T, ATTN_TILE)
    n = T // tq

    def body(q_ref, k_ref, v_ref, o_ref, qb_ref, m_sc, acc, s_even, s_odd):
        qi = pl.program_id(1)
        qv = q_ref[0]
        m_sc[...] = jnp.full_like(m_sc, -jnp.inf)
        acc[...] = jnp.zeros_like(acc)

        def key_rows(j):
            return pl.ds(pl.multiple_of(jnp.minimum(j, qi) * tq, tq), tq)

        def logits(j, buf):
            buf[...] = _dot_nt(k_ref[0, key_rows(j), :], qv)

        def update(j, buf, masked):
            st = buf[...]
            if masked:
                keep = lax.broadcasted_iota(jnp.int32, (tq, tq), 0) <= lax.broadcasted_iota(jnp.int32, (tq, tq), 1)
                st = jnp.where(keep, st, -1e30)
            m_old = m_sc[...]
            m_new = jnp.maximum(m_old, jnp.max(st, axis=0, keepdims=True))
            pt = jnp.exp(st - m_new).astype(BF16)
            acc[...] = jnp.exp(m_old - m_new) * acc[...] + _dot_tn(v_ref[0, key_rows(j), :], pt)
            m_sc[...] = m_new

        logits(0, s_even)

        def two_tiles(p, carry):
            j = 2 * p
            logits(j + 1, s_odd)
            update(j, s_even, False)
            logits(j + 2, s_even)
            update(j + 1, s_odd, False)
            return carry

        lax.fori_loop(0, qi // 2, two_tiles, 0)

        @pl.when(qi % 2 == 0)
        def _():
            update(qi, s_even, True)

        @pl.when(qi % 2 == 1)
        def _():
            logits(qi, s_odd)
            update(qi - 1, s_even, False)
            update(qi, s_odd, True)

        total = acc[hd:hd + 1, :]
        o_ref[0] = (acc[...] / total).T
        hi, mid, lo = (t.astype(F32) for t in _pieces(-(m_sc[...] + jnp.log(total))))
        row = lax.broadcasted_iota(jnp.int32, (wd, tq), 0)
        lse_rows = jnp.where(row == hd + 3, hi, jnp.where(row == hd + 4, mid, jnp.where(row == hd + 5, lo, 0.0)))
        qb_ref[0] = (qv.astype(F32) + lse_rows.T).astype(BF16)

    qmap = lambda h, i: (h, i, 0)
    head = lambda h, i: (h, 0, 0)
    in_specs = [pl.BlockSpec((1, tq, wd), qmap), pl.BlockSpec((1, T, wd), head), pl.BlockSpec((1, T, wd), head)]
    out_specs = [pl.BlockSpec((1, tq, wd), qmap), pl.BlockSpec((1, tq, wd), qmap)]
    out_shape = [_sds((H, T, wd), F32), _sds((H, T, wd), BF16)]
    scratch = [pltpu.VMEM((1, tq), F32), pltpu.VMEM((wd, tq), F32), pltpu.VMEM((tq, tq), F32), pltpu.VMEM((tq, tq), F32)]
    operands = (q_aug, k_aug, v_aug)
    if send is None:
        kernel_body = body
    else:
        def kernel_body(q_ref, k_ref, v_ref, send_ref, o_ref, qb_ref, got_ref, m_sc, acc, s_even, s_odd, *sems):
            h, i = pl.program_id(0), pl.program_id(1)

            @pl.when((h == 0) & (i == 0))
            def _():
                _gather_start(send_ref, got_ref, *sems)

            body(q_ref, k_ref, v_ref, o_ref, qb_ref, m_sc, acc, s_even, s_odd)

            @pl.when((h == H - 1) & (i == n - 1))
            def _():
                _gather_finish(send_ref, got_ref, *sems)

        in_specs, out_specs = in_specs + [_HBM], out_specs + [_HBM]
        out_shape = out_shape + [_sds((4,) + send.shape, send.dtype)]
        scratch, operands = scratch + _gather_scratch(send.shape[0]), operands + (send,)
    return pl.pallas_call(
        kernel_body, name="attn_fwd", grid=(H, n), in_specs=in_specs, out_specs=out_specs, out_shape=out_shape,
        scratch_shapes=scratch, compiler_params=_params(("arbitrary", "arbitrary")),
    )(*operands)


def attn_bwd(q_aug, k_aug, v_aug, do_aug, pair=None):
    H, T, wd = q_aug.shape
    tq = _tile(T, ATTN_TILE)
    n = T // tq

    def compute(q_ref, do_ref, k_ref, v_ref, dq_ref, dk_ref, dv_ref, dc_ref, dck, s_a, d_a, s_b, d_b):
        j = pl.program_id(1)

        @pl.when(j == 0)
        def _():
            dq_ref[...] = jnp.zeros_like(dq_ref)
            dc_ref[...] = jnp.zeros_like(dc_ref)

        dk_ref[...] = jnp.zeros_like(dk_ref)
        dv_ref[...] = jnp.zeros_like(dv_ref)
        dck[...] = jnp.zeros_like(dck)
        kv, vv = k_ref[0], v_ref[0]

        def query_rows(i):
            return pl.ds(pl.multiple_of(jnp.minimum(i, n - 1) * tq, tq), tq)

        def products(i, s_buf, d_buf):
            rows = query_rows(i)
            s_buf[...] = _dot_nt(kv, q_ref[0, rows, :])
            d_buf[...] = _dot_nt(vv, do_ref[0, rows, :])

        def update(i, s_buf, d_buf, masked):
            rows = query_rows(i)
            qv, dov = q_ref[0, rows, :], do_ref[0, rows, :]
            pt = jnp.exp(s_buf[...])
            if masked:
                keep = lax.broadcasted_iota(jnp.int32, (tq, tq), 0) <= lax.broadcasted_iota(jnp.int32, (tq, tq), 1)
                pt = jnp.where(keep, pt, 0.0)
            dv_ref[0, 0] += lax.dot_general(dov, pt.astype(BF16), TQ_DIMS, preferred_element_type=F32)
            dst = pt * d_buf[...]
            dsb = dst.astype(BF16)
            dk_ref[0, 0] += lax.dot_general(qv, dsb, TQ_DIMS, preferred_element_type=F32)
            dq_ref[0, i] += _dot_tn(kv, dsb)
            dck[...] += jnp.sum(dst, axis=1, keepdims=True)
            dc_ref[0, pl.ds(i, 1), :] += jnp.sum(dst, axis=0, keepdims=True)

        products(j, s_a, d_a)
        products(j + 1, s_b, d_b)
        update(j, s_a, d_a, True)
        left = n - 1 - j

        def two_tiles(p, carry):
            i = j + 1 + 2 * p
            products(i + 1, s_a, d_a)
            update(i, s_b, d_b, False)
            products(i + 2, s_b, d_b)
            update(i + 1, s_a, d_a, False)
            return carry

        lax.fori_loop(0, left // 2, two_tiles, 0)

        @pl.when(left % 2 == 1)
        def _():
            update(n - 1, s_b, d_b, False)

        dc_ref[0, pl.ds(j, 1), :] -= jnp.broadcast_to(dck[...], (tq, 128)).T[0:1, :]

    head = lambda h, j: (h, 0, 0)
    kmap = lambda h, j: (h, j, 0)
    in_specs = [pl.BlockSpec((1, T, wd), head), pl.BlockSpec((1, T, wd), head), pl.BlockSpec((1, tq, wd), kmap),
                pl.BlockSpec((1, tq, wd), kmap)]
    tile_t = pl.BlockSpec((1, 1, wd, tq), lambda h, j: (h, j, 0, 0))
    out_specs = [pl.BlockSpec((1, n, wd, tq), lambda h, j: (h, 0, 0, 0)), tile_t, tile_t, pl.BlockSpec((1, n, tq), head)]
    out_shape = [_sds((H, n, wd, tq), F32)] * 3 + [_sds((H, n, tq), F32)]
    scratch = [pltpu.VMEM((tq, 1), F32)] + [pltpu.VMEM((tq, tq), F32)] * 4
    operands = (q_aug, do_aug, k_aug, v_aug)
    if pair is None:
        body = compute
    else:
        def body(q_ref, do_ref, k_ref, v_ref, pair_ref, dq_ref, dk_ref, dv_ref, dc_ref, got_ref,
                 dck, s_a, d_a, s_b, d_b, send_sems, recv_sems):
            h, j = pl.program_id(0), pl.program_id(1)

            @pl.when((h == 0) & (j == 0))
            def _():
                for cp in _chip_copies(pair_ref, got_ref, send_sems, recv_sems):
                    cp.start()

            compute(q_ref, do_ref, k_ref, v_ref, dq_ref, dk_ref, dv_ref, dc_ref, dck, s_a, d_a, s_b, d_b)

            @pl.when((h == H - 1) & (j == n - 1))
            def _():
                for cp in _chip_copies(pair_ref, got_ref, send_sems, recv_sems):
                    cp.wait()

        sems = pltpu.SemaphoreType.DMA((3 * len(_spans(pair.shape[1], COPY_CHUNKS)),))
        in_specs, out_specs = in_specs + [_HBM], out_specs + [_HBM]
        out_shape = out_shape + [_sds((3,) + pair.shape[1:], pair.dtype)]
        scratch, operands = scratch + [sems, sems], operands + (pair,)
    return pl.pallas_call(
        body, name="attn_bwd", grid=(H, n), in_specs=in_specs, out_specs=out_specs, out_shape=out_shape,
        scratch_shapes=scratch, compiler_params=_params(("arbitrary", "arbitrary")),
    )(*operands)


def _complex_step(a_r, a_i, cr, ci, br, bi):
    return a_r * cr - a_i * ci + br, a_r * ci + a_i * cr + bi


_HALF_CH = SSM_W // 2
_HALF_ST = STATE_W // 2


def _state_cols(part, half):
    lo = part * STATE_W + half * _HALF_ST
    return slice(lo, lo + _HALF_ST)


def _channels_to_states(x, w_ref, out_ref):
    for half in range(2):
        ch = slice(half * _HALF_CH, (half + 1) * _HALF_CH)
        for part in range(2):
            cols = _state_cols(part, half)
            out_ref[:, cols] = _dot(x[:, ch], w_ref[ch, cols])


def _states_to_channels(x, w_ref):
    halves = []
    for half in range(2):
        ch = slice(half * _HALF_CH, (half + 1) * _HALF_CH)
        halves.append(_dot(x[:, _state_cols(0, half)], w_ref[_state_cols(0, half), ch])
                      + _dot(x[:, _state_cols(1, half)], w_ref[_state_cols(1, half), ch]))
    return jnp.concatenate(halves, axis=1)


def ssm_fwd(s_perm, wb, cbd, a_r, a_i, al_r, al_i, dvec):
    T = s_perm.shape[0]
    chunk = T // 8
    ts = _tile(chunk, SCAN_STEPS)
    tr, n_s = ts * 8, chunk // ts
    W, LB = STATE_W, SCAN_LANES

    def body(s_ref, wb_ref, cbd_ref, ar_ref, ai_ref, alr_ref, ali_ref, dv_ref, y_ref, xs_ref, bu, carry):
        ph, i = pl.program_id(0), pl.program_id(1)

        @pl.when((ph == 0) & (i == 0))
        def _():
            carry[...] = jnp.zeros_like(carry)

        _channels_to_states(s_ref[...].astype(BF16), wb_ref, bu)

        def scan(store):
            for lb in range(W // LB):
                lo = lb * LB
                re, im = slice(lo, lo + LB), slice(W + lo, W + lo + LB)
                ar = jnp.broadcast_to(ar_ref[:, re], (8, LB))
                ai = jnp.broadcast_to(ai_ref[:, re], (8, LB))

                def step(s, c):
                    rows = pl.ds(pl.multiple_of(s * 8, 8), 8)
                    nr, ni = _complex_step(ar, ai, c[0], c[1], bu[rows, re], bu[rows, im])
                    if store:
                        bu[rows, re] = nr
                        bu[rows, im] = ni
                    return nr, ni

                cr, ci = lax.fori_loop(0, ts, step, (carry[:, re], carry[:, im]), unroll=2)
                carry[:, re] = cr
                carry[:, im] = ci

        @pl.when(ph == 0)
        def _():
            scan(False)

            @pl.when(i == n_s - 1)
            def _():
                er, ei = carry[:, :W], carry[:, W:]
                alr = jnp.broadcast_to(alr_ref[...], (8, W))
                ali = jnp.broadcast_to(ali_ref[...], (8, W))
                first = lax.broadcasted_iota(jnp.int32, (8, W), 0) == 0
                sr, si = jnp.zeros((8, W), F32), jnp.zeros((8, W), F32)
                for _ in range(7):
                    vr, vi = _complex_step(alr, ali, sr, si, er, ei)
                    sr = jnp.where(first, 0.0, pltpu.roll(vr, 1, 0))
                    si = jnp.where(first, 0.0, pltpu.roll(vi, 1, 0))
                carry[:, :W] = sr
                carry[:, W:] = si

        @pl.when(ph == 1)
        def _():
            scan(True)
            xb = bu[...].astype(BF16)
            xs_ref[...] = xb
            y_ref[...] = _states_to_channels(xb, cbd_ref) + s_ref[...] * dv_ref[...]

    fix = lambda p, i: (0, 0)
    return pl.pallas_call(
        body, name="ssm_fwd", grid=(2, n_s),
        in_specs=[pl.BlockSpec((tr, SSM_W), lambda p, i: (i, 0)), pl.BlockSpec((SSM_W, 2 * W), fix),
                  pl.BlockSpec((2 * W, SSM_W), fix), pl.BlockSpec((1, W), fix), pl.BlockSpec((1, W), fix),
                  pl.BlockSpec((1, W), fix), pl.BlockSpec((1, W), fix), pl.BlockSpec((1, SSM_W), fix)],
        out_specs=[pl.BlockSpec((tr, SSM_W), lambda p, i: (i * p, 0)), pl.BlockSpec((tr, 2 * W), lambda p, i: (i * p, 0))],
        out_shape=[_sds((T, SSM_W), F32), _sds((T, 2 * W), BF16)],
        scratch_shapes=[pltpu.VMEM((tr, 2 * W), F32), pltpu.VMEM((8, 2 * W), F32)],
        compiler_params=_params(("arbitrary", "arbitrary")),
    )(s_perm, wb, cbd, a_r, a_i, al_r, al_i, dvec)


def ssm_bwd(dy_perm, s_perm, xs, cbd_t, wb_t, a_r, a_i, al_r, al_i, dvec):
    T = s_perm.shape[0]
    chunk = T // 8
    ts = _tile(chunk, SCAN_STEPS)
    tr, n_s = ts * 8, chunk // ts
    W, LB = STATE_W, SCAN_LANES

    def body(dy_ref, s_ref, xs_ref, cbt_ref, wbt_ref, ar_ref, ai_ref, alr_ref, ali_ref, dv_ref,
             du_ref, gs_ref, da_ref, dd_ref, gd, x32, carry):
        ph, i = pl.program_id(0), pl.program_id(1)

        @pl.when((ph == 0) & (i == 0))
        def _():
            carry[...] = jnp.zeros_like(carry)
            da_ref[...] = jnp.zeros_like(da_ref)
            dd_ref[...] = jnp.zeros_like(dd_ref)

        _channels_to_states(dy_ref[...].astype(BF16), cbt_ref, gd)

        def scan(store):
            for lb in range(W // LB):
                lo = lb * LB
                re, im = slice(lo, lo + LB), slice(W + lo, W + lo + LB)
                ar = jnp.broadcast_to(ar_ref[:, re], (8, LB))
                nai = -jnp.broadcast_to(ai_ref[:, re], (8, LB))

                def step(k, c):
                    rows = pl.ds(pl.multiple_of((ts - 1 - k) * 8, 8), 8)
                    cr, ci = c[0], c[1]
                    nr, ni = _complex_step(ar, nai, cr, ci, gd[rows, re], gd[rows, im])
                    if store:
                        xr, xi = x32[rows, re], x32[rows, im]
                        gd[rows, re] = nr
                        gd[rows, im] = ni
                        return nr, ni, c[2] + cr * xr + ci * xi, c[3] + ci * xr - cr * xi
                    return nr, ni

                init = (carry[:, re], carry[:, im])
                if store:
                    init = init + (da_ref[:, re], da_ref[:, im])
                out = lax.fori_loop(0, ts, step, init, unroll=2)
                carry[:, re] = out[0]
                carry[:, im] = out[1]
                if store:
                    da_ref[:, re] = out[2]
                    da_ref[:, im] = out[3]

        @pl.when(ph == 0)
        def _():
            scan(False)

            @pl.when(i == n_s - 1)
            def _():
                er, ei = carry[:, :W], carry[:, W:]
                alr = jnp.broadcast_to(alr_ref[...], (8, W))
                nali = -jnp.broadcast_to(ali_ref[...], (8, W))
                last = lax.broadcasted_iota(jnp.int32, (8, W), 0) == 7
                rr, ri = jnp.zeros((8, W), F32), jnp.zeros((8, W), F32)
                for _ in range(7):
                    vr, vi = _complex_step(alr, nali, rr, ri, er, ei)
                    rr = jnp.where(last, 0.0, pltpu.roll(vr, 7, 0))
                    ri = jnp.where(last, 0.0, pltpu.roll(vi, 7, 0))
                carry[:, :W] = rr
                carry[:, W:] = ri

        @pl.when(ph == 1)
        def _():
            x32[...] = xs_ref[...].astype(F32)
            scan(True)
            gb = gd[...].astype(BF16)
            gs_ref[...] = gb
            dy = dy_ref[...]
            du_ref[...] = _states_to_channels(gb, wbt_ref) + dy * dv_ref[...]
            dd_ref[...] += jnp.sum(dy * s_ref[...], axis=0, keepdims=True)

    fix = lambda p, i: (0, 0)
    rev = lambda p, i: (n_s - 1 - i, 0)
    rev_out = lambda p, i: (n_s - 1 - i * p, 0)
    return pl.pallas_call(
        body, name="ssm_bwd", grid=(2, n_s),
        in_specs=[pl.BlockSpec((tr, SSM_W), rev), pl.BlockSpec((tr, SSM_W), rev), pl.BlockSpec((tr, 2 * W), rev),
                  pl.BlockSpec((SSM_W, 2 * W), fix), pl.BlockSpec((2 * W, SSM_W), fix), pl.BlockSpec((1, W), fix),
                  pl.BlockSpec((1, W), fix), pl.BlockSpec((1, W), fix), pl.BlockSpec((1, W), fix),
                  pl.BlockSpec((1, SSM_W), fix)],
        out_specs=[pl.BlockSpec((tr, SSM_W), rev_out), pl.BlockSpec((tr, 2 * W), rev_out),
                   pl.BlockSpec((8, 2 * W), fix), pl.BlockSpec((1, SSM_W), fix)],
        out_shape=[_sds((T, SSM_W), F32), _sds((T, 2 * W), BF16), _sds((8, 2 * W), F32), _sds((1, SSM_W), F32)],
        scratch_shapes=[pltpu.VMEM((tr, 2 * W), F32), pltpu.VMEM((tr, 2 * W), F32), pltpu.VMEM((8, 2 * W), F32)],
        compiler_params=_params(("arbitrary", "arbitrary")),
    )(dy_perm, s_perm, xs, cbd_t, wb_t, a_r, a_i, al_r, al_i, dvec)


def _join_heads(ref, dtype, transposed=False):
    def move(h, dst):
        x = ref[h, 0] if transposed else ref[h]
        pieces = _pieces(x) if dtype == F32 else (x.astype(BF16),)
        out = None
        for piece in pieces:
            place = _lane_move(0, dst, HEAD_DIM, BF16)
            term = _dot_tn(piece, place) if transposed else _dot(piece, place)
            out = term if out is None else out + term
        return out

    return jnp.concatenate([move(2 * p, 0) + move(2 * p + 1, HEAD_DIM) for p in range(N_HEADS // 2)], axis=1)


def mixout_fwd(h1, o_heads, ypre, g_a, g_s, w_glu, b_glu, w_out):
    T = h1.shape[0]
    tm = _tile(T, TOKEN_TILE)

    def body(h_ref, at_ref, yp_ref, ga_ref, gs_ref, wg_ref, bg_ref, wo_ref, h2_ref, mixed_ref):
        yg, _ = _gelu_parts(yp_ref[...])
        gl = yg * jax.nn.sigmoid(_dot(yg.astype(BF16), wg_ref[...]) + bg_ref[...])
        at = _join_heads(at_ref, F32)
        mixed = jnp.concatenate([at * _rms_scale(at) * ga_ref[...], gl * _rms_scale(gl) * gs_ref[...]], axis=1)
        mixed = mixed.astype(BF16)
        mixed_ref[...] = mixed
        h2_ref[...] = h_ref[...] + _dot(mixed, wo_ref[...])

    tok = lambda i: (i, 0)
    fix = lambda i: (0, 0)
    return pl.pallas_call(
        body, name="mixout_fwd", grid=(T // tm,),
        in_specs=[pl.BlockSpec((tm, D_MODEL), tok), pl.BlockSpec((N_HEADS, tm, 128), lambda i: (0, i, 0)),
                  pl.BlockSpec((tm, SSM_W), tok),
                  pl.BlockSpec((1, ATTN_W), fix), pl.BlockSpec((1, SSM_W), fix), pl.BlockSpec((SSM_W, SSM_W), fix),
                  pl.BlockSpec((1, SSM_W), fix), pl.BlockSpec((D_MODEL, D_MODEL), fix)],
        out_specs=[pl.BlockSpec((tm, D_MODEL), tok), pl.BlockSpec((tm, D_MODEL), tok)],
        out_shape=[_sds((T, D_MODEL), F32), _sds((T, D_MODEL), BF16)],
        compiler_params=_params(("arbitrary",)),
    )(h1, o_heads, ypre, g_a, g_s, w_glu, b_glu, w_out)


def mixout_bwd(dh2, o_heads, ypre, g_a, g_s, w_glu, b_glu, w_out, seg):
    T = dh2.shape[0]
    tm = _tile(T, TOKEN_TILE)

    def body(dh_ref, at_ref, yp_ref, ga_ref, gs_ref, wg_ref, bg_ref, wo_ref, seg_ref,
             do_ref, dyp_ref, dpre_ref, yg_ref, dga_ref, dgs_ref, dbg_ref):
        @pl.when(pl.program_id(0) == 0)
        def _():
            dga_ref[...] = jnp.zeros_like(dga_ref)
            dgs_ref[...] = jnp.zeros_like(dgs_ref)
            dbg_ref[...] = jnp.zeros_like(dbg_ref)

        dmix = _dot_nt(dh_ref[...].astype(BF16), wo_ref[...])
        at = _join_heads(at_ref, F32)
        dat, dga = _rms_bwd(dmix[:, :ATTN_W], at, ga_ref[...])
        dga_ref[...] += dga
        delta = _pieces(jnp.dot(dat * at, seg_ref[...], precision=HIGHEST, preferred_element_type=F32))
        datb = dat.astype(BF16)
        for h in range(N_HEADS):
            p, e = divmod(h, 2)
            do_ref[h] = (_head_features(datb[:, 128 * p:128 * (p + 1)], e) + _helper_columns(delta, h, -1.0)).astype(BF16)
        yp = yp_ref[...]
        yg, t = _gelu_parts(yp)
        ygb = yg.astype(BF16)
        yg_ref[...] = ygb
        sg = jax.nn.sigmoid(_dot(ygb, wg_ref[...]) + bg_ref[...])
        dgl, dgs = _rms_bwd(dmix[:, ATTN_W:], yg * sg, gs_ref[...])
        dgs_ref[...] += dgs
        dpre = dgl * yg * sg * (1.0 - sg)
        dbg_ref[...] += jnp.sum(dpre, axis=0, keepdims=True)
        dpb = dpre.astype(BF16)
        dpre_ref[...] = dpb
        dyg = dgl * sg + _dot_nt(dpb, wg_ref[...])
        dyp_ref[...] = dyg * _gelu_grad(yp, t)

    tok = lambda i: (i, 0)
    fix = lambda i: (0, 0)
    heads = pl.BlockSpec((N_HEADS, tm, 128), lambda i: (0, i, 0))
    return pl.pallas_call(
        body, name="mixout_bwd", grid=(T // tm,),
        in_specs=[pl.BlockSpec((tm, D_MODEL), tok), heads, pl.BlockSpec((tm, SSM_W), tok),
                  pl.BlockSpec((1, ATTN_W), fix), pl.BlockSpec((1, SSM_W), fix), pl.BlockSpec((SSM_W, SSM_W), fix),
                  pl.BlockSpec((1, SSM_W), fix), pl.BlockSpec((D_MODEL, D_MODEL), fix), pl.BlockSpec((ATTN_W, 128), fix)],
        out_specs=[heads, pl.BlockSpec((tm, SSM_W), tok), pl.BlockSpec((tm, SSM_W), tok),
                   pl.BlockSpec((tm, SSM_W), tok), pl.BlockSpec((1, ATTN_W), fix),
                   pl.BlockSpec((1, SSM_W), fix), pl.BlockSpec((1, SSM_W), fix)],
        out_shape=[_sds((N_HEADS, T, 128), BF16), _sds((T, SSM_W), F32), _sds((T, SSM_W), BF16), _sds((T, SSM_W), BF16),
                   _sds((1, ATTN_W), F32), _sds((1, SSM_W), F32), _sds((1, SSM_W), F32)],
        compiler_params=_params(("arbitrary",)),
    )(dh2, o_heads, ypre, g_a, g_s, w_glu, b_glu, w_out, seg)


def head_fwd_bwd(h3, p, target, g_ple, g_final, w_gate, w_proj):
    T = h3.shape[0]
    tm = _tile(T, TOKEN_TILE)
    pd = p.shape[1]

    def body(h_ref, p_ref, tg_ref, gp_ref, gf_ref, wg_ref, wp_ref,
             dh_ref, n3_ref, dz_ref, dpp_ref, loss_ref, dgp_ref, dgf_ref):
        @pl.when(pl.program_id(0) == 0)
        def _():
            loss_ref[...] = jnp.zeros_like(loss_ref)
            dgp_ref[...] = jnp.zeros_like(dgp_ref)
            dgf_ref[...] = jnp.zeros_like(dgf_ref)

        x = h_ref[...]
        gp, gf = gp_ref[...], gf_ref[...]
        n3 = (x * _rms_scale(x) * gp).astype(BF16)
        n3_ref[...] = n3
        gate = jax.nn.sigmoid(_dot(n3, wg_ref[...]))
        pp = _dot(p_ref[...].astype(BF16), wp_ref[...])
        h4 = x + gate * pp
        y = h4 * _rms_scale(h4) * gf
        e = y - tg_ref[...]
        tile_loss = jnp.sum(jnp.sum(e * e, axis=1, keepdims=True), axis=0, keepdims=True) * (0.5 / D_MODEL)
        loss_ref[...] += jnp.broadcast_to(tile_loss, loss_ref.shape)
        dh4, dgf = _rms_bwd(e * (1.0 / D_MODEL), h4, gf)
        dgf_ref[...] += dgf
        dzg = dh4 * pp * gate * (1.0 - gate)
        dzb = dzg.astype(BF16)
        dz_ref[...] = dzb
        dpp_ref[...] = (dh4 * gate).astype(BF16)
        dx, dgp = _rms_bwd(_dot_nt(dzb, wg_ref[...]), x, gp)
        dgp_ref[...] += dgp
        dh_ref[...] = dh4 + dx

    tok = lambda i: (i, 0)
    fix = lambda i: (0, 0)
    return pl.pallas_call(
        body, name="head_fwd_bwd", grid=(T // tm,),
        in_specs=[pl.BlockSpec((tm, D_MODEL), tok), pl.BlockSpec((tm, pd), tok), pl.BlockSpec((tm, D_MODEL), tok),
                  pl.BlockSpec((1, D_MODEL), fix), pl.BlockSpec((1, D_MODEL), fix), pl.BlockSpec((D_MODEL, D_MODEL), fix),
                  pl.BlockSpec((pd, D_MODEL), fix)],
        out_specs=[pl.BlockSpec((tm, D_MODEL), tok), pl.BlockSpec((tm, D_MODEL), tok), pl.BlockSpec((tm, D_MODEL), tok),
                   pl.BlockSpec((tm, D_MODEL), tok), pl.BlockSpec((8, 128), fix), pl.BlockSpec((1, D_MODEL), fix),
                   pl.BlockSpec((1, D_MODEL), fix)],
        out_shape=[_sds((T, D_MODEL), F32), _sds((T, D_MODEL), BF16), _sds((T, D_MODEL), BF16), _sds((T, D_MODEL), BF16),
                   _sds((8, 128), F32), _sds((1, D_MODEL), F32), _sds((1, D_MODEL), F32)],
        compiler_params=_params(("arbitrary",)),
    )(h3, p, target, g_ple, g_final, w_gate, w_proj)


def _row_tile(rows, cols, n_arrays):
    lanes = -(-cols // 128) * 128
    cap = VMEM_LIMIT // 3 // (2 * n_arrays * lanes * 4)
    best = None
    for t in range(PACK_ALIGN, min(rows, cap) + 1, PACK_ALIGN):
        if rows % t == 0:
            best = t
    assert best is not None, (rows, cols)
    return best


def _adamw_math(w, g, m, v):
    nm = ADAM_B1 * m + (1.0 - ADAM_B1) * g
    nv = ADAM_B2 * v + (1.0 - ADAM_B2) * (g * g)
    c1 = 1.0 - ADAM_B1 ** ADAM_STEP
    c2 = 1.0 - ADAM_B2 ** ADAM_STEP
    return -ADAM_LR * ((nm / c1) / (jnp.sqrt(nv / c2) + ADAM_EPS) + ADAM_WD * w), nm, nv


def adamw(w, g, m, v, name):
    R, C = w.shape
    tr = _row_tile(R, C, 7)

    def body(w_ref, g_ref, m_ref, v_ref, d_ref, nm_ref, nv_ref):
        d_ref[...], nm_ref[...], nv_ref[...] = _adamw_math(w_ref[...], g_ref[...], m_ref[...], v_ref[...])

    spec = pl.BlockSpec((tr, C), lambda i: (i, 0))
    return pl.pallas_call(
        body, name=name, grid=(R // tr,), in_specs=[spec] * 4, out_specs=[spec] * 3,
        out_shape=[_sds((R, C), F32)] * 3, compiler_params=_params(("arbitrary",)),
    )(w, g, m, v)


def join_halves(mine, other, core):
    rh, C = mine.shape
    tr = _row_tile(rh, C, 3)
    nb = rh // tr

    def body(c_ref, m_ref, o_ref, out_ref):
        out_ref[...] = jnp.where((pl.program_id(0) // nb) == c_ref[0], m_ref[...], o_ref[...])

    half = pl.BlockSpec((tr, C), lambda i, c: (i % nb, 0))
    return pl.pallas_call(
        body, name="join_halves",
        grid_spec=pltpu.PrefetchScalarGridSpec(num_scalar_prefetch=1, grid=(2 * nb,), in_specs=[half, half],
                                               out_specs=pl.BlockSpec((tr, C), lambda i, c: (i, 0))),
        out_shape=_sds((2 * rh, C), F32), compiler_params=_params(("arbitrary",)),
    )(core, mine, other)


def pair_sum(g, theirs, core):
    n, R, C = g.shape
    rh = R // 2
    tr = _row_tile(rh, C, 3)
    nb = rh // tr

    def body(c_ref, g_ref, t_ref, o_ref):
        o_ref[...] = (g_ref[...] + t_ref[...]).astype(BF16)

    here = pl.BlockSpec((1, tr, C), lambda j, i, c: (j, i, 0))
    return pl.pallas_call(
        body, name="pair_sum",
        grid_spec=pltpu.PrefetchScalarGridSpec(
            num_scalar_prefetch=1, grid=(n, nb),
            in_specs=[pl.BlockSpec((1, tr, C), lambda j, i, c: (j, c[0] * nb + i, 0)), here], out_specs=here),
        out_shape=_sds((n, rh, C), BF16), compiler_params=_params(("arbitrary", "arbitrary")),
    )(core, g, theirs)


def chip_sum(pair, got, chip):
    _, R, C = pair.shape
    tr = _row_tile(R, C, 5)

    def body(c_ref, p_ref, g0_ref, g1_ref, g2_ref, o_ref):
        f = lambda ref: ref[0].astype(F32)
        o_ref[...] = ((f(p_ref) + f(g0_ref)) + f(g1_ref)) + f(g2_ref)

    slot = lambda k: pl.BlockSpec((1, tr, C), lambda i, c: (k, i, 0))
    return pl.pallas_call(
        body, name="chip_sum",
        grid_spec=pltpu.PrefetchScalarGridSpec(
            num_scalar_prefetch=1, grid=(R // tr,),
            in_specs=[pl.BlockSpec((1, tr, C), lambda i, c: (c[0], i, 0)), slot(0), slot(1), slot(2)],
            out_specs=pl.BlockSpec((tr, C), lambda i, c: (i, 0))),
        out_shape=_sds((R, C), F32), compiler_params=_params(("arbitrary",)),
    )(chip, pair, got, got, got)


_HBM = pl.BlockSpec(memory_space=pltpu.HBM)


def _place():
    x, y, c = lax.axis_index("x"), lax.axis_index("y"), lax.axis_index("c")
    return x, y, c, [(1 - x, y), (x, 1 - y), (1 - x, 1 - y)]


def _spans(rows, n):
    assert rows % PACK_ALIGN == 0
    tiles = rows // PACK_ALIGN
    n = min(n, tiles)
    cuts = [tiles * q // n for q in range(n + 1)]
    return [(cuts[q] * PACK_ALIGN, (cuts[q + 1] - cuts[q]) * PACK_ALIGN) for q in range(n)]


def _remote(src, dst, send_sem, recv_sem, to):
    return pltpu.make_async_remote_copy(src_ref=src, dst_ref=dst, send_sem=send_sem, recv_sem=recv_sem,
                                        device_id=to, device_id_type=MESH)


GATHER_SEMS = 6


def _gather_scratch(rows):
    ici = pltpu.SemaphoreType.DMA((3 * len(_spans(rows // 2, COPY_CHUNKS)),))
    own = pltpu.SemaphoreType.DMA((len(_spans(rows, 2 * COPY_CHUNKS)),))
    return [ici, ici, own, own, ici, ici]


def _gather_start(w_ref, out_ref, ici_send, ici_recv, own_send, own_recv, pass_send, pass_recv):
    R = w_ref.shape[0]
    rh = R // 2
    spans = _spans(rh, COPY_CHUNKS)
    x, y, c, chips = _place()
    me = 2 * x + y
    for k, (cx, cy) in enumerate(chips):
        for q, (o, n) in enumerate(spans):
            rows = pl.ds(c * rh + o, n)
            _remote(w_ref.at[rows, :], out_ref.at[me, rows, :], ici_send.at[k * len(spans) + q],
                    ici_recv.at[k * len(spans) + q], (cx, cy, c)).start()
    for q, (o, n) in enumerate(_spans(R, 2 * COPY_CHUNKS)):
        rows = pl.ds(o, n)
        _remote(w_ref.at[rows, :], out_ref.at[me, rows, :], own_send.at[q], own_recv.at[q], (x, y, 1 - c)).start()


def _gather_finish(w_ref, out_ref, ici_send, ici_recv, own_send, own_recv, pass_send, pass_recv):
    R = w_ref.shape[0]
    rh = R // 2
    spans = _spans(rh, COPY_CHUNKS)
    n_sp = len(spans)
    x, y, c, chips = _place()
    me = 2 * x + y
    sibling = (x, y, 1 - c)
    passed = []
    for q, (o, n) in enumerate(spans):
        for k, (cx, cy) in enumerate(chips):
            blk = out_ref.at[2 * cx + cy, pl.ds(c * rh + o, n), :]
            _remote(blk, blk, ici_send.at[k * n_sp + q], ici_recv.at[k * n_sp + q], (cx, cy, c)).wait_recv()
            cp = _remote(blk, blk, pass_send.at[k * n_sp + q], pass_recv.at[k * n_sp + q], sibling)
            cp.start()
            passed.append(cp)
    for k, (cx, cy) in enumerate(chips):
        for q, (o, n) in enumerate(spans):
            blk = out_ref.at[2 * cx + cy, pl.ds((1 - c) * rh + o, n), :]
            _remote(blk, blk, pass_send.at[k * n_sp + q], pass_recv.at[k * n_sp + q], sibling).wait_recv()
            rows = pl.ds(c * rh + o, n)
            _remote(w_ref.at[rows, :], out_ref.at[me, rows, :], ici_send.at[k * n_sp + q], ici_recv.at[k * n_sp + q],
                    (cx, cy, c)).wait_send()
    for q, (o, n) in enumerate(_spans(R, 2 * COPY_CHUNKS)):
        rows = pl.ds(o, n)
        _remote(w_ref.at[rows, :], out_ref.at[me, rows, :], own_send.at[q], own_recv.at[q], sibling).wait()
    for cp in passed:
        cp.wait_send()


def gather_shards(wp):
    R, C = wp.shape

    def body(w_ref, out_ref, *sems):
        _gather_start(w_ref, out_ref, *sems)
        _gather_finish(w_ref, out_ref, *sems)

    return pl.pallas_call(
        body, name="gather_shards", in_specs=[_HBM], out_specs=_HBM, out_shape=_sds((4, R, C), wp.dtype),
        scratch_shapes=_gather_scratch(R),
    )(wp)


def sibling_split(g):
    n_sl, R, C = g.shape
    rh = R // 2
    spans = _spans(rh, COPY_CHUNKS)
    n_sp = len(spans)

    def body(g_ref, got_ref, send_sems, recv_sems):
        x, y, c, _ = _place()
        copies = []
        for j in range(n_sl):
            for q, (o, n) in enumerate(spans):
                cp = _remote(g_ref.at[j, pl.ds((1 - c) * rh + o, n), :], got_ref.at[j, pl.ds(o, n), :],
                             send_sems.at[j * n_sp + q], recv_sems.at[j * n_sp + q], (x, y, 1 - c))
                cp.start()
                copies.append(cp)
        for cp in copies:
            cp.wait()

    sems = pltpu.SemaphoreType.DMA((n_sl * n_sp,))
    return pl.pallas_call(
        body, name="sibling_split", in_specs=[_HBM], out_specs=_HBM, out_shape=_sds((n_sl, rh, C), g.dtype),
        scratch_shapes=[sems, sems],
    )(g)


def _chip_copies(p_ref, buf_ref, send_sems, recv_sems):
    rows = p_ref.shape[1]
    spans = _spans(rows, COPY_CHUNKS)
    x, y, c, chips = _place()
    copies = []
    for k, (cx, cy) in enumerate(chips):
        for q, (o, n) in enumerate(spans):
            copies.append(_remote(p_ref.at[2 * cx + cy, pl.ds(o, n), :], buf_ref.at[k, pl.ds(o, n), :],
                                  send_sems.at[k * len(spans) + q], recv_sems.at[k * len(spans) + q], (cx, cy, c)))
    return copies


def chip_exchange(p):
    _, R, C = p.shape

    def body(p_ref, buf_ref, send_sems, recv_sems):
        copies = _chip_copies(p_ref, buf_ref, send_sems, recv_sems)
        for cp in copies:
            cp.start()
        for cp in copies:
            cp.wait()

    sems = pltpu.SemaphoreType.DMA((3 * len(_spans(R, COPY_CHUNKS)),))
    return pl.pallas_call(
        body, name="chip_exchange", in_specs=[_HBM], out_specs=_HBM, out_shape=_sds((3, R, C), p.dtype),
        scratch_shapes=[sems, sems],
    )(p)


def sibling_swap(half):
    R, C = half.shape
    spans = _spans(R, COPY_CHUNKS)

    def body(h_ref, got_ref, send_sems, recv_sems):
        x, y, c, _ = _place()
        copies = []
        for q, (o, n) in enumerate(spans):
            cp = _remote(h_ref.at[pl.ds(o, n), :], got_ref.at[pl.ds(o, n), :], send_sems.at[q], recv_sems.at[q], (x, y, 1 - c))
            cp.start()
            copies.append(cp)
        for cp in copies:
            cp.wait()

    sems = pltpu.SemaphoreType.DMA((len(spans),))
    return pl.pallas_call(
        body, name="sibling_swap", in_specs=[_HBM], out_specs=_HBM, out_shape=_sds((R, C), half.dtype),
        scratch_shapes=[sems, sems],
    )(half)


def allreduce_small(v):
    R, C = v.shape

    def body(v_ref, out_ref, sib, pair, buf, send_sems, recv_sems):
        x, y, c, chips = _place()
        me = 2 * x + y
        swap = _remote(v_ref, sib, send_sems.at[0], recv_sems.at[0], (x, y, 1 - c))
        swap.start()
        swap.wait()
        pair[...] = v_ref[...] + sib[...]
        buf[me] = pair[...]
        sends = []
        for k, (cx, cy) in enumerate(chips):
            cp = _remote(pair, buf.at[me], send_sems.at[1 + k], recv_sems.at[1 + k], (cx, cy, c))
            cp.start()
            sends.append(cp)
        for k, (cx, cy) in enumerate(chips):
            blk = buf.at[2 * cx + cy]
            _remote(blk, blk, send_sems.at[1 + k], recv_sems.at[1 + k], (cx, cy, c)).wait_recv()
        for cp in sends:
            cp.wait_send()
        out_ref[...] = ((buf[0] + buf[1]) + buf[2]) + buf[3]

    vm = pl.BlockSpec(memory_space=pltpu.VMEM)
    return pl.pallas_call(
        body, name="allreduce_small", in_specs=[vm], out_specs=vm, out_shape=_sds((R, C), F32),
        scratch_shapes=[pltpu.VMEM((R, C), F32), pltpu.VMEM((R, C), F32), pltpu.VMEM((4, R, C), F32),
                        pltpu.SemaphoreType.DMA((4,)), pltpu.SemaphoreType.DMA((4,))],
        compiler_params=pltpu.CompilerParams(vmem_limit_bytes=VMEM_LIMIT),
    )(v)


def _rows_of(shape):
    return shape[0] * shape[1] // PACK_COLS


def _slot_rows(shape):
    return -(-_rows_of(shape) // PACK_ALIGN) * PACK_ALIGN


TRANSPOSED = ("w1_a", "w3_a", "w1_b", "w3_b")


def _stored(name, shard):
    return shard[0].T if name in TRANSPOSED else shard[0]


def _restored(name, stored):
    return stored.T[None] if name in TRANSPOSED else stored[None]


def _pack_shards(shards, dtype, entries):
    parts = []
    for name, shape, _ in entries:
        part = _stored(name, shards[name]).reshape(_rows_of(shape), PACK_COLS).astype(dtype)
        parts.append(jnp.pad(part, ((0, _slot_rows(shape) - part.shape[0]), (0, 0))))
    used = sum(p.shape[0] for p in parts)
    parts.append(jnp.zeros((_group_rows(entries) - used, PACK_COLS), dtype))
    return jnp.concatenate(parts, axis=0)


def _unpack_gathered(ag, entries):
    out, off = {}, 0
    for name, shape, axis in entries:
        r = _rows_of(shape)
        piece = ag[:, off:off + r, :]
        off += _slot_rows(shape)
        if name in TRANSPOSED:
            out[name] = piece.reshape(4 * r, PACK_COLS)
        elif axis == 0:
            out[name] = piece.reshape(4 * shape[0], shape[1])
        else:
            out[name] = piece.reshape((4,) + shape).transpose(1, 0, 2).reshape(shape[0], 4 * shape[1])
    return out


LATE = ("w_glu", "w_out", "w1_b", "w3_b", "w2_b", "w_ple_gate", "w_ple_proj")
GRAD_GROUPS = (tuple(e for e in BIG if e[0] in LATE), tuple(e for e in BIG if e[0] not in LATE))


GROUP_ROW_UNIT = 2816


def _group_rows(entries):
    used = sum(_slot_rows(shape) for _, shape, _ in entries)
    return -(-used // GROUP_ROW_UNIT) * GROUP_ROW_UNIT


def _pack_full_grads(grads, entries):
    parts = []
    for name, shape, axis in entries:
        g = grads[name]
        if name in TRANSPOSED or axis == 0:
            piece = g.reshape(4, _rows_of(shape), PACK_COLS)
        else:
            piece = g.reshape(shape[0], 4, shape[1]).transpose(1, 0, 2).reshape(4, _rows_of(shape), PACK_COLS)
        parts.append(jnp.pad(piece, ((0, 0), (0, _slot_rows(shape) - piece.shape[1]), (0, 0))))
    used = sum(p.shape[1] for p in parts)
    if _group_rows(entries) > used:
        parts.append(jnp.zeros((4, _group_rows(entries) - used, PACK_COLS), F32))
    return jnp.concatenate(parts, axis=1)


def _unpack_shards(packed, entries):
    out, off = {}, 0
    for name, shape, _ in entries:
        r = _rows_of(shape)
        out[name] = packed[off:off + r] if name in TRANSPOSED else packed[off:off + r].reshape(shape)
        off += _slot_rows(shape)
    return out


def _small_rows(shape):
    return -(-math.prod(shape) // 1024) * 8


def _pack_small(vals, extra=None):
    def slot(v, rows):
        flat = v.reshape(-1)
        return jnp.pad(flat, (0, rows * 128 - flat.shape[0])).reshape(rows, 128)

    parts = [slot(vals[name], _small_rows(shape)) for name, shape in SMALL]
    parts.append(slot(extra if extra is not None else jnp.zeros((1,), F32), 8))
    assert sum(p.shape[0] for p in parts) == SMALL_ROWS
    return jnp.concatenate(parts, axis=0)


def _unpack_small(packed):
    out, off = {}, 0
    for name, shape in SMALL:
        rows = _small_rows(shape)
        out[name] = packed[off:off + rows].reshape(-1)[:math.prod(shape)].reshape(shape)
        off += rows
    return out, packed[off, 0]


def _permute_time(a):
    T, n = a.shape
    return a.reshape(8, T // 8, n).transpose(1, 0, 2).reshape(T, n)


def _unpermute_time(a):
    T, n = a.shape
    return a.reshape(T // 8, 8, n).transpose(1, 0, 2).reshape(T, n)


def _discretize(a_re, a_im, log_dt, b_re, b_im):
    dt = jnp.exp(log_dt)[:, None]
    decay = jnp.exp(dt * a_re)
    abar_r = decay * jnp.cos(dt * a_im)
    abar_i = decay * jnp.sin(dt * a_im)
    nr, ni = abar_r - 1.0, abar_i
    den = a_re * a_re + a_im * a_im
    fr = (nr * a_re + ni * a_im) / den
    fi = (ni * a_re - nr * a_im) / den
    bbar_r = fr[..., None] * b_re - fi[..., None] * b_im
    bbar_i = fr[..., None] * b_im + fi[..., None] * b_re
    return abar_r, abar_i, bbar_r, bbar_i


def _input_matrix(bbar_r, bbar_i):
    eye = jnp.eye(N_GROUPS, dtype=F32)
    blk = lambda b: jnp.einsum("ghp,gk->ghkp", b.transpose(0, 2, 1), eye).reshape(SSM_W, STATE_W)
    return jnp.concatenate([blk(bbar_r), blk(bbar_i)], axis=1)


def _output_matrix(c_re, c_im):
    eye = jnp.eye(N_GROUPS, dtype=F32)
    blk = lambda cm: jnp.einsum("ghp,gk->gpkh", cm, eye).reshape(STATE_W, SSM_W)
    return jnp.concatenate([blk(c_re), -blk(c_im)], axis=0)


def _state_power(ar, ai, n):
    steps = int(round(math.log2(n)))
    assert 1 << steps == n
    for _ in range(steps):
        ar, ai = ar * ar - ai * ai, 2.0 * ar * ai
    return ar, ai


def kernel(x, p, g_ffn1, w1_a, w3_a, w2_a, g_mix, w_in, b_f, a_re, a_im, log_dt, b_re, b_im, c_re, c_im, d_skip, w_glu, b_glu, g_attn_out, g_ssm_out, w_out, g_ffn2, w1_b, w3_b, w2_b, g_ple, w_ple_gate, w_ple_proj, g_final, loss_target, m_g_ffn1, m_w1_a, m_w3_a, m_w2_a, m_g_mix, m_w_in, m_b_f, m_a_re, m_a_im, m_log_dt, m_b_re, m_b_im, m_c_re, m_c_im, m_d_skip, m_w_glu, m_b_glu, m_g_attn_out, m_g_ssm_out, m_w_out, m_g_ffn2, m_w1_b, m_w3_b, m_w2_b, m_g_ple, m_w_ple_gate, m_w_ple_proj, m_g_final, v_g_ffn1, v_w1_a, v_w3_a, v_w2_a, v_g_mix, v_w_in, v_b_f, v_a_re, v_a_im, v_log_dt, v_b_re, v_b_im, v_c_re, v_c_im, v_d_skip, v_w_glu, v_b_glu, v_g_attn_out, v_g_ssm_out, v_w_out, v_g_ffn2, v_w1_b, v_w3_b, v_w2_b, v_g_ple, v_w_ple_gate, v_w_ple_proj, v_g_final):
    args = dict(locals())
    weights = {n: args[n] for n in WEIGHT_ORDER}
    moms = {n: args["m_" + n] for n in WEIGHT_ORDER}
    vars_ = {n: args["v_" + n] for n in WEIGHT_ORDER}
    T = x.shape[1]
    x2, p2, tgt = x[0], p[0, 0], loss_target[0]

    late_entries, early_entries = GRAD_GROUPS
    full = _unpack_gathered(gather_shards(_pack_shards(weights, BF16, early_entries)), early_entries)
    core = lax.axis_index("c").astype(jnp.int32).reshape(1)
    chip = (2 * lax.axis_index("x") + lax.axis_index("y")).astype(jnp.int32).reshape(1)
    loss_part, dx, grads, late = _local_step(x2, p2, tgt, {n: weights[n] for n, _ in SMALL}, full,
                                             early_exchange=lambda g: _pair_of(g, late_entries, core),
                                             late_pack=_pack_shards(weights, BF16, late_entries))
    return _reduce_and_update(weights, moms, vars_, loss_part, dx, grads, core, chip, late)


def _pair_of(grads, entries, core):
    packed = _pack_full_grads(grads, entries)
    return pair_sum(packed, sibling_split(packed), core)


def _local_step(x2, p2, tgt, sm, full, early_exchange=None, late_pack=None):
    full = dict(full)
    T = x2.shape[0]
    (g_ffn1, g_mix, b_f, a_re, a_im, log_dt, b_re, b_im, c_re, c_im, d_skip, b_glu, g_attn_out, g_ssm_out, g_ffn2, g_ple,
     g_final) = (sm[n] for n, _ in SMALL)
    w_in_f = full["w_in"]
    w_in_r = jnp.concatenate([w_in_f[:, :ATTN_W] * QK_SCALE, w_in_f[:, ATTN_W:3 * ATTN_W], w_in_f[:, 3 * ATTN_W + N_HEADS:],
                              w_in_f[:, 3 * ATTN_W:3 * ATTN_W + N_HEADS], jnp.zeros((D_MODEL, 128 - N_HEADS), BF16)], axis=1)
    b_f_pad = jnp.pad(b_f, ((0, 0), (0, 128 - N_HEADS)))

    disc_in = (a_re[0], a_im[0], log_dt[0], b_re[0], b_im[0])
    (abar_r, abar_i, bbar_r, bbar_i), disc_vjp = jax.vjp(_discretize, *disc_in)
    wb = _input_matrix(bbar_r, bbar_i)
    cbd = _output_matrix(c_re[0], c_im[0])
    ar, ai = abar_r.reshape(1, STATE_W), abar_i.reshape(1, STATE_W)
    alr, ali = _state_power(ar, ai, T // 8)
    dvec = d_skip.reshape(1, SSM_W)
    wb16, cbd16 = wb.astype(BF16), cbd.astype(BF16)

    h1, a1a, a3a, n1 = ffn_fwd(x2, g_ffn1, full["w1_a"], full["w3_a"], full["w2_a"], "ffn_a_fwd")
    u, qkv, s_in, fz, cum = mixin_fwd(h1, g_mix, w_in_r, b_f_pad)
    q_aug, k_aug, v_aug = heads_in(qkv, cum)
    if late_pack is None:
        o_heads, q_bwd = attn_fwd(q_aug, k_aug, v_aug)
    else:
        o_heads, q_bwd, gathered = attn_fwd(q_aug, k_aug, v_aug, send=late_pack)
        full.update(_unpack_gathered(gathered, GRAD_GROUPS[0]))
    s_perm = _permute_time(s_in)
    y_perm, xs = ssm_fwd(s_perm, wb16, cbd16, ar, ai, alr, ali, dvec)
    ypre = _unpermute_time(y_perm)
    h2, mixed = mixout_fwd(h1, o_heads, ypre, g_attn_out, g_ssm_out, full["w_glu"], b_glu, full["w_out"])
    h3, a1b, a3b, n2 = ffn_fwd(h2, g_ffn2, full["w1_b"], full["w3_b"], full["w2_b"], "ffn_b_fwd")

    dh3, n3, dzg, dpp, loss_part, dg_ple, dg_final = head_fwd_bwd(
        h3, p2, tgt, g_ple, g_final.reshape(1, D_MODEL), full["w_ple_gate"], full["w_ple_proj"])
    grads = {"g_ple": dg_ple, "g_final": dg_final.reshape(D_MODEL)}
    grads["w_ple_gate"] = mm_tn(n3, dzg, "dw_ple_gate")
    grads["w_ple_proj"] = mm_tn(p2, dpp, "dw_ple_proj")

    dh2, da1, da3, act, grads["g_ffn2"] = ffn_bwd(h2, g_ffn2, dh3, a1b, a3b, full["w1_b"], full["w3_b"], full["w2_b"], "ffn_b_bwd")
    grads["w1_b"] = mm_tn(da1, n2, "dw1_b")
    grads["w3_b"] = mm_tn(da3, n2, "dw3_b")
    grads["w2_b"] = mm_tn(act, dh3, "dw2_b", scale=0.5)

    seg = (jnp.arange(ATTN_W)[:, None] // HEAD_DIM == jnp.arange(128)[None, :]).astype(F32)
    do_aug, dypre, dpre, yg, grads["g_attn_out"], grads["g_ssm_out"], grads["b_glu"] = mixout_bwd(
        dh2, o_heads, ypre, g_attn_out, g_ssm_out, full["w_glu"], b_glu, full["w_out"], seg)
    grads["w_out"] = mm_tn(mixed, dh2, "dw_out")
    grads["w_glu"] = mm_tn(yg, dpre, "dw_glu")

    if early_exchange is None:
        late = None
        dq_aug, dk_aug, dv_aug, dc_rows = attn_bwd(q_bwd, k_aug, v_aug, do_aug)
    else:
        pair_late = early_exchange(grads)
        dq_aug, dk_aug, dv_aug, dc_rows, got_late = attn_bwd(q_bwd, k_aug, v_aug, do_aug, pair=pair_late)
        late = (pair_late, got_late)
    dc = jnp.pad(dc_rows.reshape(N_HEADS, T).T, ((0, 0), (0, 128 - N_HEADS)))

    dy_perm = _permute_time(dypre)
    du_perm, gs, d_a, dd = ssm_bwd(dy_perm, s_perm, xs, cbd16.T, wb16.T, ar, ai, alr, ali, dvec)
    ds_in = _unpermute_time(du_perm)
    hg = N_GROUPS // 2
    d_in, d_out = [], []
    for part in range(2):
        ins, outs = [], []
        for half in range(2):
            states = (part * STATE_W + half * _HALF_ST, _HALF_ST)
            chans = (half * _HALF_CH, _HALF_CH)
            blk = mm_tn(s_perm, gs, f"dw_ssm_in_{part}{half}", a_cols=chans, b_cols=states)
            ins.append(jnp.einsum("ghgp->ghp", blk.reshape(hg, GROUP_CH, hg, N_STATE)))
            blk = mm_tn(xs, dy_perm, f"dw_ssm_out_{part}{half}", a_cols=states, b_cols=chans)
            outs.append(jnp.einsum("gpgh->gph", blk.reshape(hg, N_STATE, hg, GROUP_CH)))
        d_in.append(jnp.concatenate(ins, axis=0).transpose(0, 2, 1))
        d_out.append(jnp.concatenate(outs, axis=0).transpose(0, 2, 1))
    d_abar_r = jnp.sum(d_a[:, :STATE_W], axis=0).reshape(N_GROUPS, N_STATE)
    d_abar_i = jnp.sum(d_a[:, STATE_W:], axis=0).reshape(N_GROUPS, N_STATE)
    d_disc = disc_vjp((d_abar_r, d_abar_i, d_in[0], d_in[1]))
    for name, val in zip(("a_re", "a_im", "log_dt", "b_re", "b_im"), d_disc):
        grads[name] = val[None]
    grads["c_re"] = d_out[0][None]
    grads["c_im"] = -d_out[1][None]
    grads["d_skip"] = dd.reshape(1, N_GROUPS, GROUP_CH)

    dh1, dz, grads["g_mix"], dbf = mixin_bwd(dh2, h1, g_mix, w_in_r, dq_aug, dk_aug, dv_aug, ds_in, dc, fz)
    grads["b_f"] = dbf[:, :N_HEADS]
    d_w_in_r = mm_tn(u, dz, "dw_in")
    grads["w_in"] = jnp.concatenate([d_w_in_r[:, :ATTN_W] * QK_SCALE, d_w_in_r[:, ATTN_W:3 * ATTN_W],
                                     d_w_in_r[:, 3 * ATTN_W + SSM_W:3 * ATTN_W + SSM_W + N_HEADS],
                                     d_w_in_r[:, 3 * ATTN_W:3 * ATTN_W + SSM_W]], axis=1)

    dx, da1, da3, act, grads["g_ffn1"] = ffn_bwd(x2, g_ffn1, dh1, a1a, a3a, full["w1_a"], full["w3_a"], full["w2_a"], "ffn_a_bwd")
    grads["w1_a"] = mm_tn(da1, n1, "dw1_a")
    grads["w3_a"] = mm_tn(da3, n1, "dw3_a")
    grads["w2_a"] = mm_tn(act, dh1, "dw2_a", scale=0.5)
    return loss_part, dx, grads, late


def _reduce_and_update(weights, moms, vars_, loss_part, dx, grads, core, chip, late):
    pair_early = _pair_of(grads, GRAD_GROUPS[1], core)
    g_stored = {}
    for entries, (pair, got) in zip(GRAD_GROUPS, (late, (pair_early, chip_exchange(pair_early)))):
        half = chip_sum(pair, got, chip)
        g_stored.update(_unpack_shards(join_halves(half, sibling_swap(half), core), entries))
    g_out, d_out, m_out, v_out = {}, {}, {}, {}
    for n, _, _ in BIG:
        d, m, v = adamw(_stored(n, weights[n]), g_stored[n], _stored(n, moms[n]), _stored(n, vars_[n]), "adamw_" + n)
        g_out[n], d_out[n], m_out[n], v_out[n] = (_restored(n, a) for a in (g_stored[n], d, m, v))

    small = allreduce_small(_pack_small({n: grads[n] for n, _ in SMALL}, extra=loss_part[0, 0]))
    d_small, m_small, v_small = adamw(_pack_small(weights), small, _pack_small(moms), _pack_small(vars_), "adamw_small")

    g_small, loss = _unpack_small(small)
    g_out.update(g_small)
    outs = []
    for big, sm in ((d_out, d_small), (m_out, m_small), (v_out, v_small)):
        o, _ = _unpack_small(sm)
        o.update(big)
        outs.append(o)
    result = [loss, dx[None]] + [g_out[n] for n in WEIGHT_ORDER]
    for o in outs:
        result += [o[n] for n in WEIGHT_ORDER]
    return tuple(result)
```
